```python
import numpy as np
import jax, jax.numpy as jnp
from jax import lax

D_MODEL = 1024
BATCH = 8
SEQ = 4096
DEPTH = 4

HEAD_DIM = 64
N_HEADS = D_MODEL // HEAD_DIM
MIX_WIDTH = N_HEADS * HEAD_DIM
ROT_DIM = HEAD_DIM // 4
ROPE_THETA = 500000.0
BLK = 128
NORM_EPS = 1e-6
MASK_VALUE = -1e30
N_MIXERS = 4
SCALE = HEAD_DIM ** -0.5

SWA_KV = 4
SWA_WINDOW = 128
DIL_KV = 4
DIL_PATTERNS = ((128, 1), (512, 4), (2048, 16))
FOX_HEADS = N_HEADS
NSA_KV = 2
CMP_LEN = 32
CMP_STRIDE = 16
CMP_HIDDEN = 256
SEL_LEN = 64
SEL_TOPK = 8
NSA_WINDOW = 256
FORCED_SCORE = 1e4

kernel_name = "interleaved_hybrid_swa_dilated_fox_nsa"


def rms_norm(x, g):
    xf = x.astype(jnp.float32)
    y = xf * lax.rsqrt(jnp.mean(xf * xf, axis=-1, keepdims=True) + NORM_EPS)
    return (y * g.astype(jnp.float32)).astype(x.dtype)


def partial_rope(x, pos):
    half = ROT_DIM // 2
    inv = jnp.power(ROPE_THETA, -jnp.arange(half, dtype=jnp.float32) / half)
    ang = pos.astype(jnp.float32)[..., None] * inv
    cos = jnp.cos(ang)[:, :, None, :]
    sin = jnp.sin(ang)[:, :, None, :]
    xf = x.astype(jnp.float32)
    x1, x2, rest = xf[..., :half], xf[..., half:ROT_DIM], xf[..., ROT_DIM:]
    out = jnp.concatenate([x1 * cos - x2 * sin, x2 * cos + x1 * sin, rest], axis=-1)
    return out.astype(x.dtype)


def split_cols(t, sizes):
    return jnp.split(t, [int(c) for c in np.cumsum(sizes)[:-1]], axis=-1)


def masked_softmax(s, mask, axis):
    s = jnp.where(mask, s, MASK_VALUE)
    m = jnp.max(s, axis=axis, keepdims=True)
    p = jnp.where(mask, jnp.exp(s - m), 0.0)
    den = jnp.sum(p, axis=axis, keepdims=True)
    return p / jnp.maximum(den, 1e-30)


def banded_attention(q, k, v, max_dist, sink=None):
    B, S, H, Dh = q.shape
    G = k.shape[2]
    R = H // G
    nb = S // BLK
    n_prev = -(-max_dist // BLK)
    pad = n_prev * BLK
    kp = jnp.pad(k, ((0, 0), (pad, 0), (0, 0), (0, 0))).reshape(B, nb + n_prev, BLK, G, Dh)
    vp = jnp.pad(v, ((0, 0), (pad, 0), (0, 0), (0, 0))).reshape(B, nb + n_prev, BLK, G, Dh)
    kb = jnp.concatenate([kp[:, j:j + nb] for j in range(n_prev + 1)], axis=2)
    vb = jnp.concatenate([vp[:, j:j + nb] for j in range(n_prev + 1)], axis=2)
    qb = q.reshape(B, nb, BLK, G, R, Dh)
    s = jnp.einsum('bnqgrd,bnkgd->bngrqk', qb, kb).astype(jnp.float32) * SCALE
    qi = jnp.arange(BLK)
    ki = jnp.arange((n_prev + 1) * BLK) - pad
    dist = qi[:, None] - ki[None, :]
    band = (dist >= 0) & (dist <= max_dist)
    start = jnp.arange(nb) * BLK
    mask = band[None] & ((start[:, None] + ki[None, :]) >= 0)[:, None, :]
    mask = mask[None, :, None, None]
    s = jnp.where(mask, s, MASK_VALUE)
    m = jnp.max(s, axis=-1, keepdims=True)
    if sink is not None:
        sk = sink.astype(jnp.float32).reshape(1, 1, G, R, 1, 1)
        m = jnp.maximum(m, sk)
    p = jnp.where(mask, jnp.exp(s - m), 0.0)
    den = jnp.sum(p, axis=-1, keepdims=True)
    if sink is not None:
        den = den + jnp.exp(sk - m)
    o = jnp.einsum('bngrqk,bnkgd->bnqgrd', (p / den).astype(v.dtype), vb).reshape(B, S, H, Dh)
    lse = (m + jnp.log(den))[..., 0].transpose(0, 1, 4, 2, 3).reshape(B, S, H)
    return o, lse


def swa_sink_mixer(h, pos, w_in, sinks):
    B, S, _ = h.shape
    q, k, v, gate = split_cols(h @ w_in, [MIX_WIDTH, SWA_KV * HEAD_DIM, SWA_KV * HEAD_DIM, MIX_WIDTH])
    q = partial_rope(q.reshape(B, S, N_HEADS, HEAD_DIM), pos)
    k = partial_rope(k.reshape(B, S, SWA_KV, HEAD_DIM), pos)
    v = v.reshape(B, S, SWA_KV, HEAD_DIM)
    o, _ = banded_attention(q, k, v, SWA_WINDOW - 1, sinks)
    return o.reshape(B, S, MIX_WIDTH) * jax.nn.silu(gate)


def dilated_attention(q, k, v, window, dil):
    B, S, H, Dh = q.shape
    seg = dil * BLK
    Sp = -(-S // seg) * seg
    L = Sp // dil

    def to_sub(t):
        n = t.shape[2]
        t = jnp.pad(t, ((0, 0), (0, Sp - S), (0, 0), (0, 0)))
        t = t.reshape(B, L, dil, n, Dh).transpose(0, 2, 1, 3, 4)
        return t.reshape(B * dil, L, n, Dh)

    o, lse = banded_attention(to_sub(q), to_sub(k), to_sub(v), window // dil)
    o = o.reshape(B, dil, L, H, Dh).transpose(0, 2, 1, 3, 4).reshape(B, Sp, H, Dh)[:, :S]
    lse = lse.reshape(B, dil, L, H).transpose(0, 2, 1, 3).reshape(B, Sp, H)[:, :S]
    return o, lse


def dilated_mixer(h, pos, w_in):
    B, S, _ = h.shape
    per_group = [MIX_WIDTH, DIL_KV * HEAD_DIM, DIL_KV * HEAD_DIM]
    cols = split_cols(h @ w_in, per_group * len(DIL_PATTERNS) + [MIX_WIDTH])
    gate = cols[-1]
    outs, lses = [], []
    for gi, (window, dil) in enumerate(DIL_PATTERNS):
        q, k, v = cols[3 * gi:3 * gi + 3]
        q = partial_rope(q.reshape(B, S, N_HEADS, HEAD_DIM), pos)
        k = partial_rope(k.reshape(B, S, DIL_KV, HEAD_DIM), pos)
        v = v.reshape(B, S, DIL_KV, HEAD_DIM)
        o, lse = dilated_attention(q, k, v, window, dil)
        outs.append(o)
        lses.append(lse)
    w = jax.nn.softmax(jnp.stack(lses, axis=0), axis=0)
    o = jnp.sum(w[..., None].astype(outs[0].dtype) * jnp.stack(outs, axis=0), axis=0)
    return o.reshape(B, S, MIX_WIDTH) * jax.nn.silu(gate)


def fox_mixer(h, w_in, b_f):
    B, S, _ = h.shape
    H = FOX_HEADS
    q, k, v, f_logit, gate = split_cols(h @ w_in, [MIX_WIDTH, MIX_WIDTH, MIX_WIDTH, H, MIX_WIDTH])
    q = q.reshape(B, S, H, HEAD_DIM)
    k = k.reshape(B, S, H, HEAD_DIM)
    v = v.reshape(B, S, H, HEAD_DIM)
    logf = jax.nn.log_sigmoid(f_logit.astype(jnp.float32) + b_f.astype(jnp.float32))
    c = jnp.cumsum(logf, axis=1)
    nb = S // BLK
    qb = q.reshape(B, nb, BLK, H, HEAD_DIM).transpose(1, 0, 2, 3, 4)
    cb = c.reshape(B, nb, BLK, H).transpose(1, 0, 2, 3)
    ck = c.transpose(0, 2, 1)[:, :, None, :]
    kpos = jnp.arange(S)

    def block(args):
        qn, cn, n = args
        s = jnp.einsum('bqhd,bkhd->bhqk', qn, k).astype(jnp.float32) * SCALE
        s = s + cn.transpose(0, 2, 1)[..., None] - ck
        t = n * BLK + jnp.arange(BLK)
        p = masked_softmax(s, kpos[None, :] <= t[:, None], -1)
        return jnp.einsum('bhqk,bkhd->bqhd', p.astype(v.dtype), v)

    o = lax.map(block, (qb, cb, jnp.arange(nb)))
    o = o.transpose(1, 0, 2, 3, 4).reshape(B, S, MIX_WIDTH)
    return o * jax.nn.silu(gate)


def compress_blocks(t, pe, w1, w2):
    B, S, G, Dh = t.shape
    ns = S // CMP_STRIDE
    tb = t.reshape(B, ns, CMP_STRIDE, G, Dh)
    win = jnp.concatenate([tb[:, :-1], tb[:, 1:]], axis=2)
    win = win + pe[None, None, :, None, :].astype(t.dtype)
    flat = win.transpose(0, 1, 3, 2, 4).reshape(B, ns - 1, G, CMP_LEN * Dh)
    return jax.nn.gelu(flat @ w1) @ w2


def selection_overlap(n_cmp, n_sel):
    cs = np.arange(n_cmp) * CMP_STRIDE
    js = np.arange(n_sel) * SEL_LEN
    ov = np.minimum(cs[:, None] + CMP_LEN, js[None, :] + SEL_LEN) - np.maximum(cs[:, None], js[None, :])
    return (np.clip(ov, 0, None) / CMP_LEN).astype(np.float32)


def nsa_mixer(h, pos, w_in, cmp_pe_k, cmp_w1_k, cmp_w2_k, cmp_pe_v, cmp_w1_v, cmp_w2_v):
    B, S, _ = h.shape
    H, G = N_HEADS, NSA_KV
    R = H // G
    kvw = G * HEAD_DIM
    q, kc, vc, ks, vs, kw, vw, g_logit, gate = split_cols(
        h @ w_in, [MIX_WIDTH, kvw, kvw, kvw, kvw, kvw, kvw, 3 * H, MIX_WIDTH])
    q = partial_rope(q.reshape(B, S, H, HEAD_DIM), pos)
    k_cmp = compress_blocks(kc.reshape(B, S, G, HEAD_DIM), cmp_pe_k, cmp_w1_k, cmp_w2_k)
    v_cmp = compress_blocks(vc.reshape(B, S, G, HEAD_DIM), cmp_pe_v, cmp_w1_v, cmp_w2_v)
    n_cmp = k_cmp.shape[1]
    cmp_end = np.arange(n_cmp) * CMP_STRIDE + CMP_LEN - 1
    k_cmp = partial_rope(k_cmp, pos[:, cmp_end])
    cmp_end_j = jnp.asarray(cmp_end)
    n_sel = S // SEL_LEN
    k_sel_n = min(SEL_TOPK, n_sel)
    ks = partial_rope(ks.reshape(B, S, G, HEAD_DIM), pos)
    ks = ks.reshape(B, n_sel, SEL_LEN, G, HEAD_DIM).transpose(0, 3, 1, 2, 4)
    vs = vs.reshape(B, n_sel, SEL_LEN, G, HEAD_DIM).transpose(0, 3, 1, 2, 4)
    overlap = jnp.asarray(selection_overlap(n_cmp, n_sel))
    nb = S // BLK
    qb = q.reshape(B, nb, BLK, G, R, HEAD_DIM).transpose(1, 0, 2, 3, 4, 5)
    bi = jnp.arange(B)[:, None, None, None]
    gi = jnp.arange(G)[None, :, None, None]
    jsel = jnp.arange(n_sel)

    def block(args):
        qn, n = args
        t = n * BLK + jnp.arange(BLK)
        s = jnp.einsum('bqgrd,bcgd->bgrqc', qn, k_cmp).astype(jnp.float32) * SCALE
        p_cmp = masked_softmax(s, cmp_end_j[None, :] <= t[:, None], -1)
        o_cmp = jnp.einsum('bgrqc,bcgd->bqgrd', p_cmp.astype(v_cmp.dtype), v_cmp)
        imp = jnp.einsum('bgrqc,cj->bgqj', p_cmp, overlap)
        cur = t // SEL_LEN
        forced = (jsel[None, :] == 0) | (jsel[None, :] == cur[:, None])
        causal = jsel[None, :] <= cur[:, None]
        score = jnp.where(causal, jnp.where(forced, FORCED_SCORE, imp), MASK_VALUE)
        vals, idx = lax.top_k(score, k_sel_n)
        valid = vals > 0.5 * MASK_VALUE
        kg = ks[bi, gi, idx]
        vg = vs[bi, gi, idx]
        tok = idx[..., None] * SEL_LEN + jnp.arange(SEL_LEN)
        smask = valid[..., None] & (tok <= t[None, None, :, None, None])
        s2 = jnp.einsum('bqgrd,bgqkld->bgqrkl', qn, kg).astype(jnp.float32) * SCALE
        p_slc = masked_softmax(s2, smask[:, :, :, None], (-2, -1))
        o_slc = jnp.einsum('bgqrkl,bgqkld->bqgrd', p_slc.astype(vg.dtype), vg)
        return o_cmp, o_slc

    o_cmp, o_slc = lax.map(block, (qb, jnp.arange(nb)))
    o_cmp = o_cmp.transpose(1, 0, 2, 3, 4, 5).reshape(B, S, H, HEAD_DIM)
    o_slc = o_slc.transpose(1, 0, 2, 3, 4, 5).reshape(B, S, H, HEAD_DIM)
    kw = partial_rope(kw.reshape(B, S, G, HEAD_DIM), pos)
    vw = vw.reshape(B, S, G, HEAD_DIM)
    o_win, _ = banded_attention(q, kw, vw, NSA_WINDOW - 1)
    g = jax.nn.sigmoid(g_logit.astype(jnp.float32)).reshape(B, S, H, 3, 1).astype(h.dtype)
    o = g[:, :, :, 0] * o_cmp + g[:, :, :, 1] * o_slc + g[:, :, :, 2] * o_win
    return o.reshape(B, S, MIX_WIDTH) * jax.nn.silu(gate)


def setup_inputs(seed: int = 0) -> dict:
    key = jax.random.key(seed)
    ks = jax.random.split(key, 24)
    f32 = jnp.float32

    def w(k, shape):
        return jax.random.normal(k, shape, f32) * shape[0] ** -0.5

    def gain(k):
        return 1.0 + 0.05 * jax.random.normal(k, (D_MODEL,), f32)

    kvw = NSA_KV * HEAD_DIM
    in_a = 2 * MIX_WIDTH + 2 * SWA_KV * HEAD_DIM
    in_b = len(DIL_PATTERNS) * (MIX_WIDTH + 2 * DIL_KV * HEAD_DIM) + MIX_WIDTH
    in_c = 4 * MIX_WIDTH + FOX_HEADS
    in_d = 2 * MIX_WIDTH + 6 * kvw + 3 * N_HEADS
    return {
        "x": jax.random.normal(ks[0], (BATCH, SEQ, D_MODEL), f32),
        "positions": (jnp.arange(SEQ, dtype=jnp.int32)[None, :]
                      + jax.random.randint(ks[1], (BATCH, 1), 0, 4096, dtype=jnp.int32)),
        "norm_0": gain(ks[2]),
        "w_in_0": w(ks[3], (D_MODEL, in_a)),
        "sinks_0": 0.5 * jax.random.normal(ks[4], (N_HEADS,), f32),
        "w_out_0": w(ks[5], (MIX_WIDTH, D_MODEL)),
        "norm_1": gain(ks[6]),
        "w_in_1": w(ks[7], (D_MODEL, in_b)),
        "w_out_1": w(ks[8], (MIX_WIDTH, D_MODEL)),
        "norm_2": gain(ks[9]),
        "w_in_2": w(ks[10], (D_MODEL, in_c)),
        "b_f_2": 3.0 + 0.5 * jax.random.normal(ks[11], (FOX_HEADS,), f32),
        "w_out_2": w(ks[12], (MIX_WIDTH, D_MODEL)),
        "norm_3": gain(ks[13]),
        "w_in_3": w(ks[14], (D_MODEL, in_d)),
        "cmp_pe_k_3": 0.02 * jax.random.normal(ks[15], (CMP_LEN, HEAD_DIM), f32),
        "cmp_w1_k_3": w(ks[16], (CMP_LEN * HEAD_DIM, CMP_HIDDEN)),
        "cmp_w2_k_3": w(ks[17], (CMP_HIDDEN, HEAD_DIM)),
        "cmp_pe_v_3": 0.02 * jax.random.normal(ks[18], (CMP_LEN, HEAD_DIM), f32),
        "cmp_w1_v_3": w(ks[19], (CMP_LEN * HEAD_DIM, CMP_HIDDEN)),
        "cmp_w2_v_3": w(ks[20], (CMP_HIDDEN, HEAD_DIM)),
        "w_out_3": w(ks[21], (MIX_WIDTH, D_MODEL)),
        "final_norm": gain(ks[22]),
    }


def reference(x, positions, norm_0, w_in_0, sinks_0, w_out_0, norm_1, w_in_1, w_out_1,
              norm_2, w_in_2, b_f_2, w_out_2, norm_3, w_in_3, cmp_pe_k_3, cmp_w1_k_3,
              cmp_w2_k_3, cmp_pe_v_3, cmp_w1_v_3, cmp_w2_v_3, w_out_3, final_norm):
    layers = [
        (norm_0, lambda h: swa_sink_mixer(h, positions, w_in_0, sinks_0), w_out_0),
        (norm_1, lambda h: dilated_mixer(h, positions, w_in_1), w_out_1),
        (norm_2, lambda h: fox_mixer(h, w_in_2, b_f_2), w_out_2),
        (norm_3, lambda h: nsa_mixer(h, positions, w_in_3, cmp_pe_k_3, cmp_w1_k_3, cmp_w2_k_3,
                                     cmp_pe_v_3, cmp_w1_v_3, cmp_w2_v_3), w_out_3),
    ]
    for i in range(DEPTH):
        norm, mixer, w_out = layers[i % N_MIXERS]
        x = x + mixer(rms_norm(x, norm)) @ w_out
    return rms_norm(x, final_norm)
```

```python
import functools
import math

import numpy as np
import jax
import jax.numpy as jnp
from jax import lax
from jax.experimental import pallas as pl
from jax.experimental.pallas import tpu as pltpu

HEAD_DIM = 64
N_HEADS = 16
N_PAIRS = N_HEADS // 2
MIX_WIDTH = N_HEADS * HEAD_DIM
ROT_DIM = HEAD_DIM // 4
ROT_HALF = ROT_DIM // 2
ROPE_THETA = 500000.0
BLK = 128
LANES = 128
NORM_EPS = 1e-6
MASK_VALUE = -1e30
PAD_SCORE = -3e38
SCALE = HEAD_DIM ** -0.5

SWA_KV = 4
SWA_WINDOW = 128
DIL_KV = 4
DIL_PATTERNS = ((128, 1), (512, 4), (2048, 16))
NSA_KV = 2
CMP_LEN = 32
CMP_STRIDE = 16
CMP_HIDDEN = 256
SEL_LEN = 64
SEL_TOPK = 8
NSA_WINDOW = 256
FORCED_SCORE = 1e4

VMEM_LIMIT_BYTES = 56 * 1024 * 1024
PROJ_ROWS = 512
PROJ_COLS = 512
F32 = jnp.float32
BF16 = jnp.bfloat16


def _cparams(sem):
    return pltpu.CompilerParams(dimension_semantics=sem, vmem_limit_bytes=VMEM_LIMIT_BYTES)


def _lane_half(shape):
    return lax.broadcasted_iota(jnp.int32, shape, 1) // HEAD_DIM


def _swap_halves(t):
    return jnp.concatenate([t[:, HEAD_DIM:], t[:, :HEAD_DIM]], axis=1)


def _head_query(qp, e, kv_half):
    qh = jnp.where(_lane_half(qp.shape) == e, qp, jnp.zeros_like(qp))
    if e != kv_half:
        qh = _swap_halves(qh)
    return qh


def _dot_nt(a, b):
    return lax.dot_general(a, b, (((1,), (1,)), ((), ())), preferred_element_type=F32)


def _dot(a, b):
    return jnp.dot(a, b, preferred_element_type=F32)


def _silu(x):
    return x * (1.0 / (1.0 + jnp.exp(-x)))


def _rope_table_kernel(pos_ref, inv_ref, c_ref, s_ref):
    pos = pos_ref[...].astype(F32)
    ang = pos * inv_ref[...]
    d = lax.broadcasted_iota(jnp.int32, ang.shape, 1) % HEAD_DIM
    cos = jnp.cos(ang)
    sin = jnp.sin(ang)
    c_ref[...] = jnp.where(d < ROT_DIM, cos, 1.0)
    s_ref[...] = jnp.where(d < ROT_HALF, -sin, jnp.where(d < ROT_DIM, sin, 0.0))


def _rope_tables(pos_flat):
    t = pos_flat.shape[0]
    rows = min(t, 2048)
    assert t % rows == 0
    inv = jnp.power(ROPE_THETA, -jnp.arange(ROT_HALF, dtype=F32) / ROT_HALF)
    inv_l = jnp.tile(inv, LANES // ROT_HALF)[None, :]
    out = jax.ShapeDtypeStruct((t, LANES), F32)
    return pl.pallas_call(
        _rope_table_kernel,
        grid=(t // rows,),
        in_specs=[pl.BlockSpec((rows, 1), lambda i: (i, 0)),
                  pl.BlockSpec((1, LANES), lambda i: (0, 0))],
        out_specs=[pl.BlockSpec((rows, LANES), lambda i: (i, 0))] * 2,
        out_shape=[out, out],
        compiler_params=_cparams(("arbitrary",)),
        name="rope_tables",
    )(pos_flat[:, None], inv_l)


def _apply_rope(y, c, s):
    outs = []
    for j in range(y.shape[1] // LANES):
        t = y[:, j * LANES:(j + 1) * LANES]
        d = lax.broadcasted_iota(jnp.int32, t.shape, 1) % HEAD_DIM
        partner = jnp.where(d < ROT_HALF, pltpu.roll(t, LANES - ROT_HALF, 1),
                            pltpu.roll(t, ROT_HALF, 1))
        outs.append(t * c + partner * s)
    return outs[0] if len(outs) == 1 else jnp.concatenate(outs, axis=1)


class _Seg:
    def __init__(self, width, rope=False, scale=None, dtype=BF16, dil=1):
        self.width, self.rope, self.scale, self.dtype, self.dil = width, rope, scale, dtype, dil


def _norm_proj_kernel(segs, use_rope, *refs):
    x_ref, g_ref, w_ref = refs[:3]
    k = 3
    if use_rope:
        c_ref, s_ref = refs[3:5]
        k = 5
    out_refs = refs[k:k + len(segs)]
    stage_ref = refs[k + len(segs)] if any(sg.dil > 1 for sg in segs) else None

    x = x_ref[...]
    var = jnp.mean(x * x, axis=-1, keepdims=True)
    h = (x * lax.rsqrt(var + NORM_EPS) * g_ref[...]).astype(BF16)
    rows = x.shape[0]
    col = 0
    for sg, o_ref in zip(segs, out_refs):
        for c0 in range(0, sg.width, PROJ_COLS):
            cw = min(PROJ_COLS, sg.width - c0)
            y = _dot(h, w_ref[:, col + c0:col + c0 + cw])
            if sg.rope:
                y = _apply_rope(y, c_ref[...], s_ref[...])
            if sg.scale is not None:
                y = y * sg.scale
            if sg.dil > 1:
                sub = rows // sg.dil
                for j in range(cw // LANES):
                    stage_ref[j] = y[:, j * LANES:(j + 1) * LANES]
                for r in range(sg.dil):
                    for j in range(cw // LANES):
                        lo = c0 + j * LANES
                        o_ref[r, :, lo:lo + LANES] = (
                            stage_ref[j, pl.ds(r, sub, stride=sg.dil), :].astype(sg.dtype))
            else:
                o_ref[:, c0:c0 + cw] = y.astype(sg.dtype)
        col += sg.width


def _norm_proj(x, gain, w_parts, segs, rope=None):
    b, s, d = x.shape
    tm = PROJ_ROWS
    assert s % tm == 0
    w_cols = []
    for wp, sg in zip(w_parts, segs):
        if wp.shape[1] < sg.width:
            wp = jnp.pad(wp, ((0, 0), (0, sg.width - wp.shape[1])))
        w_cols.append(wp)
    w = jnp.concatenate(w_cols, axis=1).astype(BF16)
    n = w.shape[1]
    use_rope = rope is not None
    in_specs = [pl.BlockSpec((None, tm, d), lambda bi, i: (bi, i, 0)),
                pl.BlockSpec((1, d), lambda bi, i: (0, 0)),
                pl.BlockSpec((d, n), lambda bi, i: (0, 0))]
    args = [x, gain[None, :], w]
    if use_rope:
        in_specs += [pl.BlockSpec((None, tm, LANES), lambda bi, i: (bi, i, 0))] * 2
        args += [rope[0].reshape(b, s, LANES), rope[1].reshape(b, s, LANES)]
    out_specs, out_shape = [], []
    for sg in segs:
        if sg.dil > 1:
            assert tm % sg.dil == 0
            out_shape.append(jax.ShapeDtypeStruct((b, sg.dil, s // sg.dil, sg.width), sg.dtype))
            out_specs.append(pl.BlockSpec((None, sg.dil, tm // sg.dil, sg.width),
                                          lambda bi, i: (bi, 0, i, 0)))
        else:
            out_shape.append(jax.ShapeDtypeStruct((b, s, sg.width), sg.dtype))
            out_specs.append(pl.BlockSpec((None, tm, sg.width), lambda bi, i: (bi, i, 0)))
    scratch = ([pltpu.VMEM((PROJ_COLS // LANES, tm, LANES), F32)]
               if any(sg.dil > 1 for sg in segs) else [])
    return pl.pallas_call(
        functools.partial(_norm_proj_kernel, segs, use_rope),
        grid=(b, s // tm),
        in_specs=in_specs,
        out_specs=out_specs,
        out_shape=out_shape,
        scratch_shapes=scratch,
        compiler_params=_cparams(("arbitrary", "arbitrary")),
        name="norm_proj",
    )(*args)


def _out_proj_kernel(final, *refs):
    if final:
        o_ref, w_ref, x_ref, g_ref, y_ref = refs
    else:
        o_ref, w_ref, x_ref, y_ref = refs
    y = x_ref[...] + _dot(o_ref[...], w_ref[...])
    if final:
        var = jnp.mean(y * y, axis=-1, keepdims=True)
        y = y * lax.rsqrt(var + NORM_EPS) * g_ref[...]
    y_ref[...] = y


def _out_proj(o, w, x, final_gain=None):
    b, s, d = x.shape
    m = o.shape[-1]
    tm = PROJ_ROWS
    final = final_gain is not None
    row = lambda bi, i: (bi, i, 0)
    in_specs = [pl.BlockSpec((None, tm, m), row),
                pl.BlockSpec((m, d), lambda bi, i: (0, 0)),
                pl.BlockSpec((None, tm, d), row)]
    args = [o, w.astype(BF16), x]
    if final:
        in_specs.append(pl.BlockSpec((1, d), lambda bi, i: (0, 0)))
        args.append(final_gain[None, :])
    return pl.pallas_call(
        functools.partial(_out_proj_kernel, final),
        grid=(b, s // tm),
        in_specs=in_specs,
        out_specs=pl.BlockSpec((None, tm, d), row),
        out_shape=jax.ShapeDtypeStruct((b, s, d), F32),
        compiler_params=_cparams(("arbitrary", "arbitrary")),
        name="out_proj",
    )(*args)


def _banded_kernel(max_dist, n_prev, n_kv, has_sink, has_gate, want_lse, *refs):
    q_ref = refs[0]
    k_refs = refs[1:2 + n_prev]
    v_refs = refs[2 + n_prev:3 + 2 * n_prev]
    k = 3 + 2 * n_prev
    sink_ref = gate_ref = None
    if has_sink:
        sink_ref = refs[k]; k += 1
    if has_gate:
        gate_ref = refs[k]; k += 1
    o_ref = refs[k]; k += 1
    lse_ref = refs[k] if want_lse else None

    n = pl.program_id(1)
    nblk = n_prev + 1
    kw = nblk * BLK
    rep = N_HEADS // n_kv
    qi = lax.broadcasted_iota(jnp.int32, (BLK, kw), 0)
    ki = lax.broadcasted_iota(jnp.int32, (BLK, kw), 1) - n_prev * BLK
    dist = qi - ki
    mask = (dist >= 0) & (dist <= max_dist) & (n * BLK + ki >= 0)
    half = _lane_half((BLK, LANES))
    lane = lax.broadcasted_iota(jnp.int32, (BLK, LANES), 1)
    lse_acc = jnp.zeros((BLK, LANES), F32)
    for p in range(N_PAIRS):
        qp = q_ref[:, p * LANES:(p + 1) * LANES]
        pair_out = None
        for e in range(2):
            h = 2 * p + e
            g = h // rep
            gp, kv_half = g // 2, g % 2
            qh = _head_query(qp, e, kv_half)
            s = jnp.concatenate(
                [_dot_nt(qh, kr[:, gp * LANES:(gp + 1) * LANES]) for kr in k_refs], axis=1)
            s = jnp.where(mask, s, MASK_VALUE)
            m = jnp.max(s, axis=-1, keepdims=True)
            if has_sink:
                sk = sink_ref[0:1, h:h + 1]
                m = jnp.maximum(m, sk)
            pr = jnp.where(mask, jnp.exp(s - m), 0.0)
            den = jnp.sum(pr, axis=-1, keepdims=True)
            if has_sink:
                den = den + jnp.exp(sk - m)
            prb = pr.astype(BF16)
            pv = None
            for bi, vr in enumerate(v_refs):
                t = _dot(prb[:, bi * BLK:(bi + 1) * BLK], vr[:, gp * LANES:(gp + 1) * LANES])
                pv = t if pv is None else pv + t
            oh = pv / den
            if e != kv_half:
                oh = pltpu.roll(oh, HEAD_DIM, 1)
            pair_out = oh if e == 0 else jnp.where(half == 0, pair_out, oh)
            if want_lse:
                lse_acc = jnp.where(lane == h, m + jnp.log(den), lse_acc)
        if has_gate:
            pair_out = pair_out * _silu(gate_ref[:, p * LANES:(p + 1) * LANES].astype(F32))
        o_ref[:, p * LANES:(p + 1) * LANES] = pair_out.astype(o_ref.dtype)
    if want_lse:
        lse_ref[...] = lse_acc


def _banded_attention(q, k, v, max_dist, sinks=None, gate=None, want_lse=False):
    bq, sq, _ = q.shape
    kvw = k.shape[-1]
    n_kv = kvw // HEAD_DIM
    n_prev = -(-max_dist // BLK)
    nb = sq // BLK
    row = lambda b, i: (b, i, 0)
    in_specs = [pl.BlockSpec((None, BLK, MIX_WIDTH), row)]
    args = [q]
    for arr in (k, v):
        for j in range(n_prev, -1, -1):
            in_specs.append(pl.BlockSpec((None, BLK, kvw),
                                         lambda b, i, j=j: (b, jnp.maximum(i - j, 0), 0)))
            args.append(arr)
    if sinks is not None:
        in_specs.append(pl.BlockSpec((1, N_HEADS), lambda b, i: (0, 0)))
        args.append(sinks[None, :].astype(F32))
    if gate is not None:
        in_specs.append(pl.BlockSpec((None, BLK, MIX_WIDTH), row))
        args.append(gate)
    out_specs = [pl.BlockSpec((None, BLK, MIX_WIDTH), row)]
    out_shape = [jax.ShapeDtypeStruct((bq, sq, MIX_WIDTH), BF16)]
    if want_lse:
        out_specs.append(pl.BlockSpec((None, BLK, LANES), row))
        out_shape.append(jax.ShapeDtypeStruct((bq, sq, LANES), F32))
    res = pl.pallas_call(
        functools.partial(_banded_kernel, max_dist, n_prev, n_kv, sinks is not None,
                          gate is not None, want_lse),
        grid=(bq, nb),
        in_specs=in_specs,
        out_specs=out_specs,
        out_shape=out_shape,
        compiler_params=_cparams(("arbitrary", "arbitrary")),
        name="banded_attn",
    )(*args)
    return res if want_lse else res[0]


def _dil_combine_kernel(dils, *refs):
    ng = len(dils)
    o_refs, l_refs = refs[:ng], refs[ng:2 * ng]
    gate_ref, out_ref, stage_ref, lstage_ref = refs[2 * ng:]
    rows = out_ref.shape[0]

    def natural(ref, dil, stage, slab):
        sl = slice(slab * LANES, (slab + 1) * LANES)
        if dil == 1:
            return ref[:, sl].astype(F32)
        sub = rows // dil
        for r in range(dil):
            stage[pl.ds(r, sub, stride=dil), :] = ref[r, :, sl].astype(F32)
        return stage[...]

    lses = [natural(l_refs[i], dils[i], lstage_ref, 0) for i in range(ng)]
    mx = functools.reduce(jnp.maximum, lses)
    ws = [jnp.exp(l - mx) for l in lses]
    tot = functools.reduce(lambda a, c: a + c, ws)
    ws = [w / tot for w in ws]
    half = _lane_half((rows, LANES))
    outs = [None] * N_PAIRS
    for gi in range(ng):
        for p in range(N_PAIRS):
            og = natural(o_refs[gi], dils[gi], stage_ref, p)
            w0 = ws[gi][:, 2 * p:2 * p + 1]
            w1 = ws[gi][:, 2 * p + 1:2 * p + 2]
            wp = jnp.where(half == 0, w0, w1)
            t = wp * og
            outs[p] = t if outs[p] is None else outs[p] + t
    for p in range(N_PAIRS):
        sl = slice(p * LANES, (p + 1) * LANES)
        out_ref[:, sl] = (outs[p] * _silu(gate_ref[:, sl].astype(F32))).astype(out_ref.dtype)


def _dil_combine(os_, lses, gate, dils):
    b, s, _ = gate.shape
    tm = 256
    in_specs, args = [], []
    for arrs, width in ((os_, MIX_WIDTH), (lses, LANES)):
        for arr, dil in zip(arrs, dils):
            if dil == 1:
                in_specs.append(pl.BlockSpec((None, tm, width), lambda bi, i: (bi, i, 0)))
            else:
                in_specs.append(pl.BlockSpec((None, dil, tm // dil, width),
                                             lambda bi, i: (bi, 0, i, 0)))
            args.append(arr)
    in_specs.append(pl.BlockSpec((None, tm, MIX_WIDTH), lambda bi, i: (bi, i, 0)))
    args.append(gate)
    return pl.pallas_call(
        functools.partial(_dil_combine_kernel, dils),
        grid=(b, s // tm),
        in_specs=in_specs,
        out_specs=pl.BlockSpec((None, tm, MIX_WIDTH), lambda bi, i: (bi, i, 0)),
        out_shape=jax.ShapeDtypeStruct((b, s, MIX_WIDTH), BF16),
        scratch_shapes=[pltpu.VMEM((tm, LANES), F32), pltpu.VMEM((tm, LANES), F32)],
        compiler_params=_cparams(("arbitrary", "arbitrary")),
        name="dil_combine",
    )(*args)


def _fox_decay_kernel(f_ref, b_ref, c_ref, ct_ref):
    s = f_ref.shape[0]
    x = f_ref[...] + b_ref[...]
    logf = jnp.minimum(x, 0.0) - jnp.log1p(jnp.exp(-jnp.abs(x)))
    r = lax.broadcasted_iota(jnp.int32, (BLK, BLK), 0)
    c = lax.broadcasted_iota(jnp.int32, (BLK, BLK), 1)
    tri = (c <= r).astype(F32)
    carry = jnp.zeros((1, LANES), F32)
    for i in range(s // BLK):
        blk = logf[i * BLK:(i + 1) * BLK, :]
        cs = jnp.dot(tri, blk, preferred_element_type=F32, precision=lax.Precision.HIGHEST) + carry
        c_ref[i * BLK:(i + 1) * BLK, :] = cs
        ct_ref[:, i * BLK:(i + 1) * BLK] = cs.T[:N_HEADS, :]
        carry = cs[BLK - 1:BLK, :]


def _fox_decay(f_logit, b_f):
    b, s, _ = f_logit.shape
    b_pad = jnp.pad(b_f.astype(F32), (0, LANES - N_HEADS))[None, :]
    return pl.pallas_call(
        _fox_decay_kernel,
        grid=(b,),
        in_specs=[pl.BlockSpec((None, s, LANES), lambda bi: (bi, 0, 0)),
                  pl.BlockSpec((1, LANES), lambda bi: (0, 0))],
        out_specs=[pl.BlockSpec((None, s, LANES), lambda bi: (bi, 0, 0)),
                   pl.BlockSpec((None, N_HEADS, s), lambda bi: (bi, 0, 0))],
        out_shape=[jax.ShapeDtypeStruct((b, s, LANES), F32),
                   jax.ShapeDtypeStruct((b, N_HEADS, s), F32)],
        compiler_params=_cparams(("arbitrary",)),
        name="fox_decay",
    )(f_logit, b_pad)


FOX_TQ = 256
FOX_TK = 256


def _fox_kernel(q_ref, k_ref, v_ref, c_ref, ct_ref, gate_ref, o_ref, m_sc, l_sc, acc_sc):
    pi = pl.program_id(1)
    i = pl.program_id(2)
    tq, tk = FOX_TQ, FOX_TK
    qp = q_ref[...]
    half = _lane_half((tq, LANES))
    lane = lax.broadcasted_iota(jnp.int32, (tq, LANES), 1)
    cblk = c_ref[...]
    qhs, cqs = [], []
    for e in range(2):
        qhs.append(_head_query(qp, e, e))
        cqs.append(jnp.sum(jnp.where(lane == 2 * pi + e, cblk, 0.0), axis=1, keepdims=True))
        m_sc[e] = jnp.full((tq, LANES), MASK_VALUE, F32)
        l_sc[e] = jnp.zeros((tq, LANES), F32)
        acc_sc[e] = jnp.zeros((tq, LANES), F32)

    def step(j, diag):
        k0 = pl.multiple_of(j * tk, tk)
        kt = k_ref[pl.ds(k0, tk), :]
        vt = v_ref[pl.ds(k0, tk), :]
        for e in range(2):
            ck = ct_ref[pl.ds(e, 1), pl.ds(k0, tk)]
            s = _dot_nt(qhs[e], kt) + (cqs[e] - ck)
            if diag:
                ok = (lax.broadcasted_iota(jnp.int32, (tq, tk), 1)
                      <= lax.broadcasted_iota(jnp.int32, (tq, tk), 0))
                s = jnp.where(ok, s, MASK_VALUE)
            m_old = m_sc[e][:, 0:1]
            m_new = jnp.maximum(m_old, jnp.max(s, axis=-1, keepdims=True))
            pr = jnp.exp(s - m_new)
            if diag:
                pr = jnp.where(ok, pr, 0.0)
            alpha = jnp.exp(m_old - m_new)
            l_new = alpha * l_sc[e][:, 0:1] + jnp.sum(pr, axis=-1, keepdims=True)
            acc_sc[e] = alpha * acc_sc[e] + _dot(pr.astype(BF16), vt)
            m_sc[e] = jnp.broadcast_to(m_new, (tq, LANES))
            l_sc[e] = jnp.broadcast_to(l_new, (tq, LANES))

    def body(j, carry):
        step(j, False)
        return carry

    lax.fori_loop(0, i, body, 0)
    step(i, True)
    den0 = jnp.maximum(l_sc[0][:, 0:1], 1e-30)
    den1 = jnp.maximum(l_sc[1][:, 0:1], 1e-30)
    out = jnp.where(half == 0, acc_sc[0] / den0, acc_sc[1] / den1)
    o_ref[...] = (out * _silu(gate_ref[...].astype(F32))).astype(o_ref.dtype)


def _fox_attention(q, k, v, c, ct, gate):
    b, s, _ = q.shape
    tq = FOX_TQ
    assert FOX_TQ == FOX_TK and s % tq == 0
    ct4 = ct.reshape(b, N_PAIRS, 2, s)
    tile = lambda bi, p, i: (bi, i, p)
    full = lambda bi, p, i: (bi, 0, p)
    return pl.pallas_call(
        _fox_kernel,
        grid=(b, N_PAIRS, s // tq),
        in_specs=[pl.BlockSpec((None, tq, LANES), tile),
                  pl.BlockSpec((None, s, LANES), full),
                  pl.BlockSpec((None, s, LANES), full),
                  pl.BlockSpec((None, tq, LANES), lambda bi, p, i: (bi, i, 0)),
                  pl.BlockSpec((None, None, 2, s), lambda bi, p, i: (bi, p, 0, 0)),
                  pl.BlockSpec((None, tq, LANES), tile)],
        out_specs=pl.BlockSpec((None, tq, LANES), tile),
        out_shape=jax.ShapeDtypeStruct((b, s, MIX_WIDTH), BF16),
        scratch_shapes=[pltpu.VMEM((2, tq, LANES), F32)] * 3,
        compiler_params=_cparams(("arbitrary", "arbitrary", "arbitrary")),
        name="fox_attn",
    )(q, k, v, c, ct4, gate)


def _gelu_tanh(x):
    return 0.5 * x * (1.0 + jnp.tanh(math.sqrt(2.0 / math.pi) * (x + 0.044715 * (x * x * x))))


def _nsa_compress_kernel(ak_ref, av_ref, pe_ref, wa_ref, wb_ref, w2_ref, c_ref, s_ref,
                         kc_ref, vc_ref):
    nrow = ak_ref.shape[0]
    for idx, (a_ref, o_ref) in enumerate(((ak_ref, kc_ref), (av_ref, vc_ref))):
        a = a_ref[...].astype(F32)
        xa = (a + pe_ref[idx, 0:1, :]).astype(BF16)
        xb = (a + pe_ref[idx, 1:2, :]).astype(BF16)
        ya = _dot(xa, wa_ref[idx])
        yb = _dot(xb, wb_ref[idx])
        hid = _gelu_tanh(ya + pltpu.roll(yb, nrow - 1, 0))
        y = _dot(hid.astype(BF16), w2_ref[idx])
        if idx == 0:
            y = _apply_rope(y, c_ref[...], s_ref[...])
        o_ref[...] = y.astype(o_ref.dtype)


def _nsa_compress(kc, vc, pe_k, w1_k, w2_k, pe_v, w1_v, w2_v, rope_c, rope_s):
    b, s, _ = kc.shape
    ns = s // CMP_STRIDE
    g = NSA_KV
    flat = CMP_STRIDE * g * HEAD_DIM

    def w1_halves(w1):
        w1r = w1.reshape(2, CMP_STRIDE, HEAD_DIM, CMP_HIDDEN)
        outs = []
        for hf in range(2):
            z = jnp.zeros((CMP_STRIDE, g, HEAD_DIM, g, CMP_HIDDEN), F32)
            for gi in range(g):
                z = z.at[:, gi, :, gi, :].set(w1r[hf])
            outs.append(z.reshape(flat, g * CMP_HIDDEN))
        return outs

    def pe_halves(pe):
        per = pe.reshape(2, CMP_STRIDE, 1, HEAD_DIM)
        return jnp.broadcast_to(per, (2, CMP_STRIDE, g, HEAD_DIM)).reshape(2, flat)

    def w2_bd(w2):
        z = jnp.zeros((g, CMP_HIDDEN, g, HEAD_DIM), F32)
        for gi in range(g):
            z = z.at[gi, :, gi, :].set(w2)
        return z.reshape(g * CMP_HIDDEN, g * HEAD_DIM)

    ka, kb = w1_halves(w1_k)
    va, vb = w1_halves(w1_v)
    wa = jnp.stack([ka, va]).astype(BF16)
    wb = jnp.stack([kb, vb]).astype(BF16)
    w2 = jnp.stack([w2_bd(w2_k), w2_bd(w2_v)]).astype(BF16)
    pe = jnp.stack([pe_halves(pe_k), pe_halves(pe_v)]).astype(F32)
    whole = lambda bi: (0, 0, 0)
    per_b = lambda bi: (bi, 0, 0)
    out = jax.ShapeDtypeStruct((b, ns, LANES), BF16)
    return pl.pallas_call(
        _nsa_compress_kernel,
        grid=(b,),
        in_specs=[pl.BlockSpec((None, ns, flat), per_b),
                  pl.BlockSpec((None, ns, flat), per_b),
                  pl.BlockSpec((2, 2, flat), whole),
                  pl.BlockSpec((2, flat, g * CMP_HIDDEN), whole),
                  pl.BlockSpec((2, flat, g * CMP_HIDDEN), whole),
                  pl.BlockSpec((2, g * CMP_HIDDEN, LANES), whole),
                  pl.BlockSpec((None, ns, LANES), per_b),
                  pl.BlockSpec((None, ns, LANES), per_b)],
        out_specs=[pl.BlockSpec((None, ns, LANES), per_b)] * 2,
        out_shape=[out, out],
        compiler_params=_cparams(("arbitrary",)),
        name="nsa_compress",
    )(kc.reshape(b, ns, flat), vc.reshape(b, ns, flat), pe, wa, wb, w2,
      rope_c.reshape(b, ns, LANES), rope_s.reshape(b, ns, LANES))


NSA_TQ = 128
NSA_TK = 512


def _nsa_kernel(n_cmp, n_sel, q_ref, kc_ref, vc_ref, ks_ref, vs_ref, ow_ref, gl_ref, gate_ref, ov_ref,
                o_ref, m_sc, l_sc, acc_sc, ocmp_sc):
    n = pl.program_id(1)
    tq, tk = NSA_TQ, NSA_TK
    ncp = kc_ref.shape[0]
    rep = N_HEADS // NSA_KV
    half = _lane_half((tq, LANES))
    lane = lax.broadcasted_iota(jnp.int32, (tq, LANES), 1)
    t_col = n * tq + lax.broadcasted_iota(jnp.int32, (tq, 1), 0)

    ci = lax.broadcasted_iota(jnp.int32, (tq, ncp), 1)
    cmask = (ci * CMP_STRIDE + (CMP_LEN - 1) <= t_col) & (ci < n_cmp)
    kc = kc_ref[...]
    vc = vc_ref[...]
    psum = [jnp.zeros((tq, ncp), F32) for _ in range(NSA_KV)]
    for p in range(N_PAIRS):
        qp = q_ref[:, p * LANES:(p + 1) * LANES]
        g = (2 * p) // rep
        pair_out = None
        for e in range(2):
            qh = _head_query(qp, e, g)
            s = jnp.where(cmask, _dot_nt(qh, kc), MASK_VALUE)
            m = jnp.max(s, axis=-1, keepdims=True)
            pr = jnp.where(cmask, jnp.exp(s - m), 0.0)
            pr = pr / jnp.maximum(jnp.sum(pr, axis=-1, keepdims=True), 1e-30)
            psum[g] = psum[g] + pr
            oh = _dot(pr.astype(BF16), vc)
            if e != g:
                oh = pltpu.roll(oh, HEAD_DIM, 1)
            pair_out = oh if e == 0 else jnp.where(half == 0, pair_out, oh)
        ocmp_sc[p] = pair_out

    cur = t_col // SEL_LEN
    lane_f = lane.astype(F32)
    forced = (lane == 0) | (lane == cur)
    causal = lane <= cur
    sel = []
    for g in range(NSA_KV):
        imp = jnp.dot(psum[g], ov_ref[...], preferred_element_type=F32,
                      precision=lax.Precision.HIGHEST)
        score = jnp.where(causal, jnp.where(forced, FORCED_SCORE, imp), MASK_VALUE)
        score = jnp.where(lane < n_sel, score, PAD_SCORE)
        chosen = jnp.zeros((tq, LANES), F32)
        for _ in range(min(SEL_TOPK, n_sel)):
            mx = jnp.max(score, axis=-1, keepdims=True)
            first = jnp.min(jnp.where(score == mx, lane_f, float(LANES)), axis=-1, keepdims=True)
            hit = lane_f == first
            chosen = jnp.where(hit, 1.0, chosen)
            score = jnp.where(hit, PAD_SCORE, score)
        sel.append(jnp.where(causal, chosen, 0.0).astype(BF16))

    for g in range(NSA_KV):
        for r in range(rep):
            m_sc[g * rep + r] = jnp.full((tq, LANES), MASK_VALUE, F32)
            l_sc[g * rep + r] = jnp.zeros((tq, LANES), F32)
            acc_sc[g * rep + r] = jnp.zeros((tq, LANES), F32)

    n_chunks = (n * tq + tq + tk - 1) // tk

    def body(c, carry):
        k0 = pl.multiple_of(c * tk, tk)
        kt = ks_ref[pl.ds(k0, tk), :]
        vt = vs_ref[pl.ds(k0, tk), :]
        tok = k0 + lax.broadcasted_iota(jnp.int32, (tq, tk), 1)
        erow = lax.broadcasted_iota(jnp.int32, (LANES, tk), 0)
        ecol = k0 + lax.broadcasted_iota(jnp.int32, (LANES, tk), 1)
        expand = jnp.where(ecol // SEL_LEN == erow, 1.0, 0.0).astype(BF16)
        for g in range(NSA_KV):
            ok = (_dot(sel[g], expand) > 0.5) & (tok <= t_col)
            for r in range(rep):
                h = g * rep + r
                qp = q_ref[:, (h // 2) * LANES:(h // 2 + 1) * LANES]
                qh = _head_query(qp, h % 2, g)
                s = jnp.where(ok, _dot_nt(qh, kt), MASK_VALUE)
                m_old = m_sc[h][:, 0:1]
                m_new = jnp.maximum(m_old, jnp.max(s, axis=-1, keepdims=True))
                pr = jnp.where(ok, jnp.exp(s - m_new), 0.0)
                alpha = jnp.exp(m_old - m_new)
                l_new = alpha * l_sc[h][:, 0:1] + jnp.sum(pr, axis=-1, keepdims=True)
                acc_sc[h] = alpha * acc_sc[h] + _dot(pr.astype(BF16), vt)
                m_sc[h] = jnp.broadcast_to(m_new, (tq, LANES))
                l_sc[h] = jnp.broadcast_to(l_new, (tq, LANES))
        return carry

    lax.fori_loop(0, n_chunks, body, 0)

    gl = 1.0 / (1.0 + jnp.exp(-gl_ref[...]))
    for p in range(N_PAIRS):
        sl = slice(p * LANES, (p + 1) * LANES)
        g = (2 * p) // rep
        oslc = None
        for e in range(2):
            h = 2 * p + e
            oh = acc_sc[h] / jnp.maximum(l_sc[h][:, 0:1], 1e-30)
            if e != g:
                oh = pltpu.roll(oh, HEAD_DIM, 1)
            oslc = oh if e == 0 else jnp.where(half == 0, oslc, oh)
        branches = (ocmp_sc[p], oslc, ow_ref[:, sl].astype(F32))
        out = None
        for j, br in enumerate(branches):
            w0 = gl[:, 3 * (2 * p) + j:3 * (2 * p) + j + 1]
            w1 = gl[:, 3 * (2 * p + 1) + j:3 * (2 * p + 1) + j + 1]
            t = jnp.where(half == 0, w0, w1) * br
            out = t if out is None else out + t
        o_ref[:, sl] = (out * _silu(gate_ref[:, sl].astype(F32))).astype(o_ref.dtype)


def _selection_overlap(n_cmp_pad, n_cmp, n_sel):
    cs = np.arange(n_cmp_pad) * CMP_STRIDE
    js = np.arange(LANES) * SEL_LEN
    ov = np.minimum(cs[:, None] + CMP_LEN, js[None, :] + SEL_LEN) - np.maximum(cs[:, None], js[None, :])
    ov = (np.clip(ov, 0, None) / CMP_LEN).astype(np.float32)
    ov[n_cmp:, :] = 0.0
    ov[:, n_sel:] = 0.0
    return ov


def _nsa_attention(q, k_cmp, v_cmp, ks, vs, o_win, g_logit, gate):
    b, s, _ = q.shape
    tq = NSA_TQ
    ncp = k_cmp.shape[1]
    n_cmp = s // CMP_STRIDE - 1
    n_sel = s // SEL_LEN
    assert n_sel <= LANES and s % NSA_TK == 0
    ov = jnp.asarray(_selection_overlap(ncp, n_cmp, n_sel))
    row = lambda bi, i: (bi, i, 0)
    per_b = lambda bi, i: (bi, 0, 0)
    nh = N_HEADS
    return pl.pallas_call(
        functools.partial(_nsa_kernel, n_cmp, n_sel),
        grid=(b, s // tq),
        in_specs=[pl.BlockSpec((None, tq, MIX_WIDTH), row),
                  pl.BlockSpec((None, ncp, LANES), per_b),
                  pl.BlockSpec((None, ncp, LANES), per_b),
                  pl.BlockSpec((None, s, LANES), per_b),
                  pl.BlockSpec((None, s, LANES), per_b),
                  pl.BlockSpec((None, tq, MIX_WIDTH), row),
                  pl.BlockSpec((None, tq, LANES), row),
                  pl.BlockSpec((None, tq, MIX_WIDTH), row),
                  pl.BlockSpec((ncp, LANES), lambda bi, i: (0, 0))],
        out_specs=pl.BlockSpec((None, tq, MIX_WIDTH), row),
        out_shape=jax.ShapeDtypeStruct((b, s, MIX_WIDTH), BF16),
        scratch_shapes=[pltpu.VMEM((nh, tq, LANES), F32), pltpu.VMEM((nh, tq, LANES), F32),
                        pltpu.VMEM((nh, tq, LANES), F32), pltpu.VMEM((N_PAIRS, tq, LANES), F32)],
        compiler_params=_cparams(("arbitrary", "arbitrary")),
        name="nsa_attn",
    )(q, k_cmp, v_cmp, ks, vs, o_win, g_logit, gate, ov)


def _split(w, sizes):
    offs = np.cumsum([0] + list(sizes))
    return [w[:, int(offs[i]):int(offs[i + 1])] for i in range(len(sizes))]


def _swa_layer(x, rope, gain, w_in, sinks, w_out):
    kvw = SWA_KV * HEAD_DIM
    parts = _split(w_in, [MIX_WIDTH, kvw, kvw, MIX_WIDTH])
    segs = [_Seg(MIX_WIDTH, rope=True, scale=SCALE), _Seg(kvw, rope=True), _Seg(kvw), _Seg(MIX_WIDTH)]
    q, k, v, gate = _norm_proj(x, gain, parts, segs, rope)
    o = _banded_attention(q, k, v, SWA_WINDOW - 1, sinks=sinks, gate=gate)
    return o, w_out


def _dilated_layer(x, rope, gain, w_in, w_out):
    b, s, _ = x.shape
    kvw = DIL_KV * HEAD_DIM
    sizes, segs = [], []
    for window, dil in DIL_PATTERNS:
        assert s % (dil * BLK) == 0
        sizes += [MIX_WIDTH, kvw, kvw]
        segs += [_Seg(MIX_WIDTH, rope=True, scale=SCALE, dil=dil), _Seg(kvw, rope=True, dil=dil),
                 _Seg(kvw, dil=dil)]
    sizes.append(MIX_WIDTH)
    segs.append(_Seg(MIX_WIDTH))
    res = _norm_proj(x, gain, _split(w_in, sizes), segs, rope)
    gate = res[-1]
    os_, lses, dils = [], [], []
    for gi, (window, dil) in enumerate(DIL_PATTERNS):
        q, k, v = res[3 * gi:3 * gi + 3]
        if dil > 1:
            q, k, v = (t.reshape(b * dil, s // dil, t.shape[-1]) for t in (q, k, v))
        o, lse = _banded_attention(q, k, v, window // dil, want_lse=True)
        if dil > 1:
            o = o.reshape(b, dil, s // dil, MIX_WIDTH)
            lse = lse.reshape(b, dil, s // dil, LANES)
        os_.append(o)
        lses.append(lse)
        dils.append(dil)
    return _dil_combine(os_, lses, gate, tuple(dils)), w_out


def _fox_layer(x, gain, w_in, b_f, w_out):
    sizes = [MIX_WIDTH, MIX_WIDTH, MIX_WIDTH, N_HEADS, MIX_WIDTH]
    segs = [_Seg(MIX_WIDTH, scale=SCALE), _Seg(MIX_WIDTH), _Seg(MIX_WIDTH),
            _Seg(LANES, dtype=F32), _Seg(MIX_WIDTH)]
    q, k, v, f_logit, gate = _norm_proj(x, gain, _split(w_in, sizes), segs)
    c, ct = _fox_decay(f_logit, b_f)
    return _fox_attention(q, k, v, c, ct, gate), w_out


def _nsa_layer(x, positions, rope, gain, w_in, pe_k, w1_k, w2_k, pe_v, w1_v, w2_v, w_out):
    b, s, _ = x.shape
    kvw = NSA_KV * HEAD_DIM
    sizes = [MIX_WIDTH] + [kvw] * 6 + [3 * N_HEADS, MIX_WIDTH]
    segs = [_Seg(MIX_WIDTH, rope=True, scale=SCALE), _Seg(kvw), _Seg(kvw), _Seg(kvw, rope=True),
            _Seg(kvw), _Seg(kvw, rope=True), _Seg(kvw), _Seg(LANES, dtype=F32), _Seg(MIX_WIDTH)]
    q, kc, vc, ks, vs, kw, vw, g_logit, gate = _norm_proj(x, gain, _split(w_in, sizes), segs, rope)
    ns = s // CMP_STRIDE
    cmp_pos = jnp.concatenate(
        [positions[:, CMP_LEN - 1::CMP_STRIDE], positions[:, -1:]], axis=1)[:, :ns]
    cmp_c, cmp_s = _rope_tables(cmp_pos.reshape(-1))
    k_cmp, v_cmp = _nsa_compress(kc, vc, pe_k, w1_k, w2_k, pe_v, w1_v, w2_v, cmp_c, cmp_s)
    o_win = _banded_attention(q, kw, vw, NSA_WINDOW - 1)
    return _nsa_attention(q, k_cmp, v_cmp, ks, vs, o_win, g_logit, gate), w_out


def kernel(x, positions, norm_0, w_in_0, sinks_0, w_out_0, norm_1, w_in_1, w_out_1, norm_2, w_in_2, b_f_2, w_out_2, norm_3, w_in_3, cmp_pe_k_3, cmp_w1_k_3, cmp_w2_k_3, cmp_pe_v_3, cmp_w1_v_3, cmp_w2_v_3, w_out_3, final_norm):
    rope = _rope_tables(positions.reshape(-1))
    o, w = _swa_layer(x, rope, norm_0, w_in_0, sinks_0, w_out_0)
    x = _out_proj(o, w, x)
    o, w = _dilated_layer(x, rope, norm_1, w_in_1, w_out_1)
    x = _out_proj(o, w, x)
    o, w = _fox_layer(x, norm_2, w_in_2, b_f_2, w_out_2)
    x = _out_proj(o, w, x)
    o, w = _nsa_layer(x, positions, rope, norm_3, w_in_3, cmp_pe_k_3, cmp_w1_k_3, cmp_w2_k_3,
                      cmp_pe_v_3, cmp_w1_v_3, cmp_w2_v_3, w_out_3)
    return _out_proj(o, w, x, final_gain=final_norm)
```

```python
import functools
import math

import numpy as np
import jax
import jax.numpy as jnp
from jax import lax
from jax.experimental import pallas as pl
from jax.experimental.pallas import tpu as pltpu

HEAD_DIM = 64
N_HEADS = 16
N_PAIRS = N_HEADS // 2
MIX_WIDTH = N_HEADS * HEAD_DIM
ROT_DIM = HEAD_DIM // 4
ROT_HALF = ROT_DIM // 2
ROPE_THETA = 500000.0
BLK = 128
LANES = 128
NORM_EPS = 1e-6
MASK_VALUE = -1e30
PAD_SCORE = -3e38
SCALE = HEAD_DIM ** -0.5

SWA_KV = 4
SWA_WINDOW = 128
DIL_KV = 4
DIL_PATTERNS = ((128, 1), (512, 4), (2048, 16))
NSA_KV = 2
CMP_LEN = 32
CMP_STRIDE = 16
CMP_HIDDEN = 256
SEL_LEN = 64
SEL_TOPK = 8
NSA_WINDOW = 256
FORCED_SCORE = 1e4

VMEM_LIMIT_BYTES = 56 * 1024 * 1024
PROJ_ROWS = 512
PROJ_COLS = 512
F32 = jnp.float32
BF16 = jnp.bfloat16


def _cparams(sem):
    return pltpu.CompilerParams(dimension_semantics=sem, vmem_limit_bytes=VMEM_LIMIT_BYTES)


def _lane_half(shape):
    return lax.broadcasted_iota(jnp.int32, shape, 1) // HEAD_DIM


def _swap_halves(t):
    return jnp.concatenate([t[:, HEAD_DIM:], t[:, :HEAD_DIM]], axis=1)


def _head_query(qp, e, kv_half):
    qh = jnp.where(_lane_half(qp.shape) == e, qp, jnp.zeros_like(qp))
    if e != kv_half:
        qh = _swap_halves(qh)
    return qh


def _dot_nt(a, b):
    return lax.dot_general(a, b, (((1,), (1,)), ((), ())), preferred_element_type=F32)


def _dot(a, b):
    return jnp.dot(a, b, preferred_element_type=F32)


def _silu(x):
    return x * (1.0 / (1.0 + jnp.exp(-x)))


def _rope_table_kernel(pos_ref, inv_ref, c_ref, s_ref):
    pos = pos_ref[...].astype(F32)
    ang = pos * inv_ref[...]
    d = lax.broadcasted_iota(jnp.int32, ang.shape, 1) % HEAD_DIM
    cos = jnp.cos(ang)
    sin = jnp.sin(ang)
    c_ref[...] = jnp.where(d < ROT_DIM, cos, 1.0)
    s_ref[...] = jnp.where(d < ROT_HALF, -sin, jnp.where(d < ROT_DIM, sin, 0.0))


def _rope_tables(pos_flat):
    t = pos_flat.shape[0]
    rows = min(t, 2048)
    assert t % rows == 0
    inv = jnp.power(ROPE_THETA, -jnp.arange(ROT_HALF, dtype=F32) / ROT_HALF)
    inv_l = jnp.tile(inv, LANES // ROT_HALF)[None, :]
    out = jax.ShapeDtypeStruct((t, LANES), F32)
    return pl.pallas_call(
        _rope_table_kernel,
        grid=(t // rows,),
        in_specs=[pl.BlockSpec((rows, 1), lambda i: (i, 0)),
                  pl.BlockSpec((1, LANES), lambda i: (0, 0))],
        out_specs=[pl.BlockSpec((rows, LANES), lambda i: (i, 0))] * 2,
        out_shape=[out, out],
        compiler_params=_cparams(("arbitrary",)),
        name="rope_tables",
    )(pos_flat[:, None], inv_l)


def _apply_rope(y, c, s):
    outs = []
    for j in range(y.shape[1] // LANES):
        t = y[:, j * LANES:(j + 1) * LANES]
        d = lax.broadcasted_iota(jnp.int32, t.shape, 1) % HEAD_DIM
        partner = jnp.where(d < ROT_HALF, pltpu.roll(t, LANES - ROT_HALF, 1),
                            pltpu.roll(t, ROT_HALF, 1))
        outs.append(t * c + partner * s)
    return outs[0] if len(outs) == 1 else jnp.concatenate(outs, axis=1)


class _Seg:
    def __init__(self, width, rope=False, scale=None, dtype=BF16, dil=1, transposed=False):
        self.width, self.rope, self.scale, self.dtype, self.dil = width, rope, scale, dtype, dil
        self.transposed = transposed
        assert not (transposed and (rope or dil > 1))


def _norm_proj_kernel(segs, use_rope, *refs):
    has_t = any(sg.transposed for sg in segs)
    x_ref, g_ref, w_ref = refs[:3]
    k = 3
    if has_t:
        wt_ref = refs[k]
        k += 1
    if use_rope:
        c_ref, s_ref = refs[k:k + 2]
        k += 2
    out_refs = refs[k:k + len(segs)]
    stage_ref = refs[k + len(segs)] if any(sg.dil > 1 for sg in segs) else None

    x = x_ref[...]
    var = jnp.mean(x * x, axis=-1, keepdims=True)
    h = (x * lax.rsqrt(var + NORM_EPS) * g_ref[...]).astype(BF16)
    rows = x.shape[0]
    col = 0
    tcol = 0
    for sg, o_ref in zip(segs, out_refs):
        if sg.transposed:
            for c0 in range(0, sg.width, PROJ_COLS):
                cw = min(PROJ_COLS, sg.width - c0)
                yt = _dot_nt(wt_ref[tcol + c0:tcol + c0 + cw, :], h)
                if sg.scale is not None:
                    yt = yt * sg.scale
                o_ref[c0:c0 + cw, :] = yt.astype(sg.dtype)
            tcol += sg.width
            continue
        for c0 in range(0, sg.width, PROJ_COLS):
            cw = min(PROJ_COLS, sg.width - c0)
            y = _dot(h, w_ref[:, col + c0:col + c0 + cw])
            if sg.rope:
                y = _apply_rope(y, c_ref[...], s_ref[...])
            if sg.scale is not None:
                y = y * sg.scale
            if sg.dil > 1:
                sub = rows // sg.dil
                for j in range(cw // LANES):
                    stage_ref[j] = y[:, j * LANES:(j + 1) * LANES]
                for r in range(sg.dil):
                    for j in range(cw // LANES):
                        lo = c0 + j * LANES
                        o_ref[r, :, lo:lo + LANES] = (
                            stage_ref[j, pl.ds(r, sub, stride=sg.dil), :].astype(sg.dtype))
            else:
                o_ref[:, c0:c0 + cw] = y.astype(sg.dtype)
        col += sg.width


def _norm_proj(x, gain, w_parts, segs, rope=None):
    b, s, d = x.shape
    tm = PROJ_ROWS
    assert s % tm == 0
    w_cols, wt_rows = [], []
    for wp, sg in zip(w_parts, segs):
        if wp.shape[1] < sg.width:
            wp = jnp.pad(wp, ((0, 0), (0, sg.width - wp.shape[1])))
        if sg.transposed:
            wt_rows.append(wp.T)
        else:
            w_cols.append(wp)
    w = jnp.concatenate(w_cols, axis=1).astype(BF16)
    n = w.shape[1]
    use_rope = rope is not None
    in_specs = [pl.BlockSpec((None, tm, d), lambda bi, i: (bi, i, 0)),
                pl.BlockSpec((1, d), lambda bi, i: (0, 0)),
                pl.BlockSpec((d, n), lambda bi, i: (0, 0))]
    args = [x, gain[None, :], w]
    if wt_rows:
        wt = jnp.concatenate(wt_rows, axis=0).astype(BF16)
        in_specs.append(pl.BlockSpec(wt.shape, lambda bi, i: (0, 0)))
        args.append(wt)
    if use_rope:
        in_specs += [pl.BlockSpec((None, tm, LANES), lambda bi, i: (bi, i, 0))] * 2
        args += [rope[0].reshape(b, s, LANES), rope[1].reshape(b, s, LANES)]
    out_specs, out_shape = [], []
    for sg in segs:
        if sg.dil > 1:
            assert tm % sg.dil == 0
            out_shape.append(jax.ShapeDtypeStruct((b, sg.dil, s // sg.dil, sg.width), sg.dtype))
            out_specs.append(pl.BlockSpec((None, sg.dil, tm // sg.dil, sg.width),
                                          lambda bi, i: (bi, 0, i, 0)))
        elif sg.transposed:
            out_shape.append(jax.ShapeDtypeStruct((b, s // tm, sg.width, tm), sg.dtype))
            out_specs.append(pl.BlockSpec((None, None, sg.width, tm), lambda bi, i: (bi, i, 0, 0)))
        else:
            out_shape.append(jax.ShapeDtypeStruct((b, s, sg.width), sg.dtype))
            out_specs.append(pl.BlockSpec((None, tm, sg.width), lambda bi, i: (bi, i, 0)))
    scratch = ([pltpu.VMEM((PROJ_COLS // LANES, tm, LANES), F32)]
               if any(sg.dil > 1 for sg in segs) else [])
    return pl.pallas_call(
        functools.partial(_norm_proj_kernel, segs, use_rope),
        grid=(b, s // tm),
        in_specs=in_specs,
        out_specs=out_specs,
        out_shape=out_shape,
        scratch_shapes=scratch,
        compiler_params=_cparams(("arbitrary", "arbitrary")),
        name="norm_proj",
    )(*args)


def _out_proj_kernel(final, *refs):
    if final:
        o_ref, w_ref, x_ref, g_ref, y_ref = refs
    else:
        o_ref, w_ref, x_ref, y_ref = refs
    y = x_ref[...] + _dot(o_ref[...], w_ref[...])
    if final:
        var = jnp.mean(y * y, axis=-1, keepdims=True)
        y = y * lax.rsqrt(var + NORM_EPS) * g_ref[...]
    y_ref[...] = y


def _out_proj(o, w, x, final_gain=None):
    b, s, d = x.shape
    m = o.shape[-1]
    tm = PROJ_ROWS
    final = final_gain is not None
    row = lambda bi, i: (bi, i, 0)
    in_specs = [pl.BlockSpec((None, tm, m), row),
                pl.BlockSpec((m, d), lambda bi, i: (0, 0)),
                pl.BlockSpec((None, tm, d), row)]
    args = [o, w.astype(BF16), x]
    if final:
        in_specs.append(pl.BlockSpec((1, d), lambda bi, i: (0, 0)))
        args.append(final_gain[None, :])
    return pl.pallas_call(
        functools.partial(_out_proj_kernel, final),
        grid=(b, s // tm),
        in_specs=in_specs,
        out_specs=pl.BlockSpec((None, tm, d), row),
        out_shape=jax.ShapeDtypeStruct((b, s, d), F32),
        compiler_params=_cparams(("arbitrary", "arbitrary")),
        name="out_proj",
    )(*args)


def _banded_kernel(max_dist, n_prev, n_kv, has_sink, has_gate, want_lse, *refs):
    q_ref = refs[0]
    k_refs = refs[1:2 + n_prev]
    v_refs = refs[2 + n_prev:3 + 2 * n_prev]
    k = 3 + 2 * n_prev
    sink_ref = gate_ref = None
    if has_sink:
        sink_ref = refs[k]; k += 1
    if has_gate:
        gate_ref = refs[k]; k += 1
    o_ref = refs[k]; k += 1
    lse_ref = refs[k] if want_lse else None

    n = pl.program_id(1)
    nblk = n_prev + 1
    kw = nblk * BLK
    rep = N_HEADS // n_kv
    qi = lax.broadcasted_iota(jnp.int32, (BLK, kw), 0)
    ki = lax.broadcasted_iota(jnp.int32, (BLK, kw), 1) - n_prev * BLK
    dist = qi - ki
    mask = (dist >= 0) & (dist <= max_dist) & (n * BLK + ki >= 0)
    half = _lane_half((BLK, LANES))
    lane = lax.broadcasted_iota(jnp.int32, (BLK, LANES), 1)
    lse_acc = jnp.zeros((BLK, LANES), F32)
    for p in range(N_PAIRS):
        qp = q_ref[:, p * LANES:(p + 1) * LANES]
        pair_out = None
        for e in range(2):
            h = 2 * p + e
            g = h // rep
            gp, kv_half = g // 2, g % 2
            qh = _head_query(qp, e, kv_half)
            s = jnp.concatenate(
                [_dot_nt(qh, kr[:, gp * LANES:(gp + 1) * LANES]) for kr in k_refs], axis=1)
            s = jnp.where(mask, s, MASK_VALUE)
            m = jnp.max(s, axis=-1, keepdims=True)
            if has_sink:
                sk = sink_ref[0:1, h:h + 1]
                m = jnp.maximum(m, sk)
            pr = jnp.where(mask, jnp.exp(s - m), 0.0)
            den = jnp.sum(pr, axis=-1, keepdims=True)
            if has_sink:
                den = den + jnp.exp(sk - m)
            prb = pr.astype(BF16)
            pv = None
            for bi, vr in enumerate(v_refs):
                t = _dot(prb[:, bi * BLK:(bi + 1) * BLK], vr[:, gp * LANES:(gp + 1) * LANES])
                pv = t if pv is None else pv + t
            oh = pv / den
            if e != kv_half:
                oh = pltpu.roll(oh, HEAD_DIM, 1)
            pair_out = oh if e == 0 else jnp.where(half == 0, pair_out, oh)
            if want_lse:
                lse_acc = jnp.where(lane == h, m + jnp.log(den), lse_acc)
        if has_gate:
            pair_out = pair_out * _silu(gate_ref[:, p * LANES:(p + 1) * LANES].astype(F32))
        o_ref[:, p * LANES:(p + 1) * LANES] = pair_out.astype(o_ref.dtype)
    if want_lse:
        lse_ref[...] = lse_acc


def _banded_attention(q, k, v, max_dist, sinks=None, gate=None, want_lse=False):
    bq, sq, _ = q.shape
    kvw = k.shape[-1]
    n_kv = kvw // HEAD_DIM
    n_prev = -(-max_dist // BLK)
    nb = sq // BLK
    row = lambda b, i: (b, i, 0)
    in_specs = [pl.BlockSpec((None, BLK, MIX_WIDTH), row)]
    args = [q]
    for arr in (k, v):
        for j in range(n_prev, -1, -1):
            in_specs.append(pl.BlockSpec((None, BLK, kvw),
                                         lambda b, i, j=j: (b, jnp.maximum(i - j, 0), 0)))
            args.append(arr)
    if sinks is not None:
        in_specs.append(pl.BlockSpec((1, N_HEADS), lambda b, i: (0, 0)))
        args.append(sinks[None, :].astype(F32))
    if gate is not None:
        in_specs.append(pl.BlockSpec((None, BLK, MIX_WIDTH), row))
        args.append(gate)
    out_specs = [pl.BlockSpec((None, BLK, MIX_WIDTH), row)]
    out_shape = [jax.ShapeDtypeStruct((bq, sq, MIX_WIDTH), BF16)]
    if want_lse:
        out_specs.append(pl.BlockSpec((None, BLK, LANES), row))
        out_shape.append(jax.ShapeDtypeStruct((bq, sq, LANES), F32))
    res = pl.pallas_call(
        functools.partial(_banded_kernel, max_dist, n_prev, n_kv, sinks is not None,
                          gate is not None, want_lse),
        grid=(bq, nb),
        in_specs=in_specs,
        out_specs=out_specs,
        out_shape=out_shape,
        compiler_params=_cparams(("arbitrary", "arbitrary")),
        name="banded_attn",
    )(*args)
    return res if want_lse else res[0]


def _dil_combine_kernel(dils, *refs):
    ng = len(dils)
    o_refs, l_refs = refs[:ng], refs[ng:2 * ng]
    gate_ref, out_ref, stage_ref, lstage_ref = refs[2 * ng:]
    rows = out_ref.shape[0]

    def natural(ref, dil, stage, slab):
        sl = slice(slab * LANES, (slab + 1) * LANES)
        if dil == 1:
            return ref[:, sl].astype(F32)
        sub = rows // dil
        for r in range(dil):
            stage[pl.ds(r, sub, stride=dil), :] = ref[r, :, sl].astype(F32)
        return stage[...]

    lses = [natural(l_refs[i], dils[i], lstage_ref, 0) for i in range(ng)]
    mx = functools.reduce(jnp.maximum, lses)
    ws = [jnp.exp(l - mx) for l in lses]
    tot = functools.reduce(lambda a, c: a + c, ws)
    ws = [w / tot for w in ws]
    half = _lane_half((rows, LANES))
    outs = [None] * N_PAIRS
    for gi in range(ng):
        for p in range(N_PAIRS):
            og = natural(o_refs[gi], dils[gi], stage_ref, p)
            w0 = ws[gi][:, 2 * p:2 * p + 1]
            w1 = ws[gi][:, 2 * p + 1:2 * p + 2]
            wp = jnp.where(half == 0, w0, w1)
            t = wp * og
            outs[p] = t if outs[p] is None else outs[p] + t
    for p in range(N_PAIRS):
        sl = slice(p * LANES, (p + 1) * LANES)
        out_ref[:, sl] = (outs[p] * _silu(gate_ref[:, sl].astype(F32))).astype(out_ref.dtype)


def _dil_combine(os_, lses, gate, dils):
    b, s, _ = gate.shape
    tm = 256
    in_specs, args = [], []
    for arrs, width in ((os_, MIX_WIDTH), (lses, LANES)):
        for arr, dil in zip(arrs, dils):
            if dil == 1:
                in_specs.append(pl.BlockSpec((None, tm, width), lambda bi, i: (bi, i, 0)))
            else:
                in_specs.append(pl.BlockSpec((None, dil, tm // dil, width),
                                             lambda bi, i: (bi, 0, i, 0)))
            args.append(arr)
    in_specs.append(pl.BlockSpec((None, tm, MIX_WIDTH), lambda bi, i: (bi, i, 0)))
    args.append(gate)
    return pl.pallas_call(
        functools.partial(_dil_combine_kernel, dils),
        grid=(b, s // tm),
        in_specs=in_specs,
        out_specs=pl.BlockSpec((None, tm, MIX_WIDTH), lambda bi, i: (bi, i, 0)),
        out_shape=jax.ShapeDtypeStruct((b, s, MIX_WIDTH), BF16),
        scratch_shapes=[pltpu.VMEM((tm, LANES), F32), pltpu.VMEM((tm, LANES), F32)],
        compiler_params=_cparams(("arbitrary", "arbitrary")),
        name="dil_combine",
    )(*args)


N_BIAS_PIECES = 3


def _fox_decay_kernel(f_ref, b_ref, kb_ref):
    s = f_ref.shape[0]
    x = f_ref[...] + b_ref[...]
    logf = jnp.minimum(x, 0.0) - jnp.log1p(jnp.exp(-jnp.abs(x)))
    r = lax.broadcasted_iota(jnp.int32, (BLK, BLK), 0)
    c = lax.broadcasted_iota(jnp.int32, (BLK, BLK), 1)
    tri = (c <= r).astype(F32)
    lane = lax.broadcasted_iota(jnp.int32, (BLK, LANES), 1)
    carry = jnp.zeros((1, LANES), F32)
    for i in range(s // BLK):
        blk = logf[i * BLK:(i + 1) * BLK, :]
        cs = jnp.dot(tri, blk, preferred_element_type=F32, precision=lax.Precision.HIGHEST) + carry
        carry = cs[BLK - 1:BLK, :]
        rest = cs * (-LOG2E)
        out = jnp.zeros((BLK, LANES), F32)
        for j in range(N_BIAS_PIECES):
            piece = rest.astype(BF16).astype(F32)
            rest = rest - piece
            moved = piece if j == 0 else pltpu.roll(piece, N_HEADS * j, 1)
            out = jnp.where((lane >= N_HEADS * j) & (lane < N_HEADS * (j + 1)), moved, out)
        kb_ref[i * BLK:(i + 1) * BLK, :] = out.astype(BF16)


def _fox_decay(f_logit, b_f):
    b, s, _ = f_logit.shape
    b_pad = jnp.pad(b_f.astype(F32), (0, LANES - N_HEADS))[None, :]
    return pl.pallas_call(
        _fox_decay_kernel,
        grid=(b,),
        in_specs=[pl.BlockSpec((None, s, LANES), lambda bi: (bi, 0, 0)),
                  pl.BlockSpec((1, LANES), lambda bi: (0, 0))],
        out_specs=pl.BlockSpec((None, s, LANES), lambda bi: (bi, 0, 0)),
        out_shape=jax.ShapeDtypeStruct((b, s, LANES), BF16),
        compiler_params=_cparams(("arbitrary",)),
        name="fox_decay",
    )(f_logit, b_pad)


FOX_T = PROJ_ROWS
ACC_ROWS = HEAD_DIM + 16
LOG2E = math.log2(math.e)


def _fox_kernel(qt_ref, k_ref, kb_ref, vt_ref, gate_ref, o_ref, m_sc, acc_sc, st_sc):
    pi = pl.program_id(1)
    i = pl.program_id(2)
    t = FOX_T
    qt = qt_ref[...]
    row = lax.broadcasted_iota(jnp.int32, (LANES, t), 0)
    zeros = jnp.zeros((HEAD_DIM, t), BF16)
    ones_rows = jnp.ones((ACC_ROWS - HEAD_DIM, FOX_T // 2), BF16)
    qx = []
    for e in range(2):
        h = 2 * pi + e
        pick = (row % N_HEADS == h) & (row < N_HEADS * N_BIAS_PIECES)
        sel = jnp.where(pick, 1.0, 0.0).astype(BF16)
        mine = qt[e * HEAD_DIM:(e + 1) * HEAD_DIM, :]
        top = [mine, zeros] if e == 0 else [zeros, mine]
        qx.append(jnp.concatenate(top + [sel], axis=0))
        m_sc[e] = jnp.full((1, t), MASK_VALUE, F32)
        acc_sc[e] = jnp.zeros((ACC_ROWS, t), F32)

    tk = t // 2

    def scores(jt, hf, e):
        k0 = pl.multiple_of(jt * t + hf * tk, tk)
        kx = jnp.concatenate([k_ref[pl.ds(k0, tk), :], kb_ref[pl.ds(k0, tk), :]], axis=1)
        st_sc[e, hf] = _dot(kx, qx[e])

    def consume(jt, hf, e, diag):
        st = st_sc[e, hf]
        if diag:
            ok = (lax.broadcasted_iota(jnp.int32, (tk, t), 0) + hf * tk
                  <= lax.broadcasted_iota(jnp.int32, (tk, t), 1))
            st = jnp.where(ok, st, MASK_VALUE)
        m_old = m_sc[e]
        m_new = jnp.maximum(m_old, jnp.max(st, axis=0, keepdims=True))
        pt = jnp.exp2(st - m_new).astype(BF16)
        alpha = jnp.exp2(m_old - m_new)
        vt = vt_ref[jt, e * HEAD_DIM:(e + 1) * HEAD_DIM, hf * tk:(hf + 1) * tk]
        acc_sc[e] = alpha * acc_sc[e] + _dot(jnp.concatenate([vt, ones_rows], axis=0), pt)
        m_sc[e] = m_new

    for e in range(2):
        scores(0, 0, e)

    def body(jt, carry):
        for e in range(2):
            scores(jt, 1, e)
            consume(jt, 0, e, False)
        for e in range(2):
            scores(jt + 1, 0, e)
            consume(jt, 1, e, False)
        return carry

    lax.fori_loop(0, i, body, 0)
    for e in range(2):
        scores(i, 1, e)
        consume(i, 0, e, True)
    for e in range(2):
        consume(i, 1, e, True)
    outs = []
    for e in range(2):
        acc = acc_sc[e]
        outs.append(acc[:HEAD_DIM] / jnp.maximum(acc[HEAD_DIM:HEAD_DIM + 1], 1e-30))
    out_t = jnp.concatenate(outs, axis=0)
    o_ref[...] = (out_t.T * _silu(gate_ref[...].astype(F32))).astype(o_ref.dtype)


def _fox_attention(qt, k, kb, vt, gate):
    b, s, _ = k.shape
    t = FOX_T
    assert s % t == 0 and vt.shape == (b, s // t, MIX_WIDTH, t) and qt.shape == vt.shape
    tile = lambda bi, p, i: (bi, i, p)
    return pl.pallas_call(
        _fox_kernel,
        grid=(b, N_PAIRS, s // t),
        in_specs=[pl.BlockSpec((None, None, LANES, t), lambda bi, p, i: (bi, i, p, 0)),
                  pl.BlockSpec((None, s, LANES), lambda bi, p, i: (bi, 0, p)),
                  pl.BlockSpec((None, s, LANES), lambda bi, p, i: (bi, 0, 0)),
                  pl.BlockSpec((None, s // t, LANES, t), lambda bi, p, i: (bi, 0, p, 0)),
                  pl.BlockSpec((None, t, LANES), tile)],
        out_specs=pl.BlockSpec((None, t, LANES), tile),
        out_shape=jax.ShapeDtypeStruct((b, s, MIX_WIDTH), BF16),
        scratch_shapes=[pltpu.VMEM((2, 1, t), F32), pltpu.VMEM((2, ACC_ROWS, t), F32),
                        pltpu.VMEM((2, 2, t // 2, t), F32)],
        compiler_params=_cparams(("arbitrary", "arbitrary", "arbitrary")),
        name="fox_attn",
    )(qt, k, kb, vt, gate)


def _gelu_tanh(x):
    return 0.5 * x * (1.0 + jnp.tanh(math.sqrt(2.0 / math.pi) * (x + 0.044715 * (x * x * x))))


def _nsa_compress_kernel(ak_ref, av_ref, pe_ref, wa_ref, wb_ref, w2_ref, c_ref, s_ref,
                         kc_ref, vc_ref):
    nrow = ak_ref.shape[0]
    for idx, (a_ref, o_ref) in enumerate(((ak_ref, kc_ref), (av_ref, vc_ref))):
        a = a_ref[...].astype(F32)
        xa = (a + pe_ref[idx, 0:1, :]).astype(BF16)
        xb = (a + pe_ref[idx, 1:2, :]).astype(BF16)
        ya = _dot(xa, wa_ref[idx])
        yb = _dot(xb, wb_ref[idx])
        hid = _gelu_tanh(ya + pltpu.roll(yb, nrow - 1, 0))
        y = _dot(hid.astype(BF16), w2_ref[idx])
        if idx == 0:
            y = _apply_rope(y, c_ref[...], s_ref[...])
        o_ref[...] = y.astype(o_ref.dtype)


def _nsa_compress(kc, vc, pe_k, w1_k, w2_k, pe_v, w1_v, w2_v, rope_c, rope_s):
    b, s, _ = kc.shape
    ns = s // CMP_STRIDE
    g = NSA_KV
    flat = CMP_STRIDE * g * HEAD_DIM

    def w1_halves(w1):
        w1r = w1.reshape(2, CMP_STRIDE, HEAD_DIM, CMP_HIDDEN)
        outs = []
        for hf in range(2):
            z = jnp.zeros((CMP_STRIDE, g, HEAD_DIM, g, CMP_HIDDEN), F32)
            for gi in range(g):
                z = z.at[:, gi, :, gi, :].set(w1r[hf])
            outs.append(z.reshape(flat, g * CMP_HIDDEN))
        return outs

    def pe_halves(pe):
        per = pe.reshape(2, CMP_STRIDE, 1, HEAD_DIM)
        return jnp.broadcast_to(per, (2, CMP_STRIDE, g, HEAD_DIM)).reshape(2, flat)

    def w2_bd(w2):
        z = jnp.zeros((g, CMP_HIDDEN, g, HEAD_DIM), F32)
        for gi in range(g):
            z = z.at[gi, :, gi, :].set(w2)
        return z.reshape(g * CMP_HIDDEN, g * HEAD_DIM)

    ka, kb = w1_halves(w1_k)
    va, vb = w1_halves(w1_v)
    wa = jnp.stack([ka, va]).astype(BF16)
    wb = jnp.stack([kb, vb]).astype(BF16)
    w2 = jnp.stack([w2_bd(w2_k), w2_bd(w2_v)]).astype(BF16)
    pe = jnp.stack([pe_halves(pe_k), pe_halves(pe_v)]).astype(F32)
    whole = lambda bi: (0, 0, 0)
    per_b = lambda bi: (bi, 0, 0)
    out = jax.ShapeDtypeStruct((b, ns, LANES), BF16)
    return pl.pallas_call(
        _nsa_compress_kernel,
        grid=(b,),
        in_specs=[pl.BlockSpec((None, ns, flat), per_b),
                  pl.BlockSpec((None, ns, flat), per_b),
                  pl.BlockSpec((2, 2, flat), whole),
                  pl.BlockSpec((2, flat, g * CMP_HIDDEN), whole),
                  pl.BlockSpec((2, flat, g * CMP_HIDDEN), whole),
                  pl.BlockSpec((2, g * CMP_HIDDEN, LANES), whole),
                  pl.BlockSpec((None, ns, LANES), per_b),
                  pl.BlockSpec((None, ns, LANES), per_b)],
        out_specs=[pl.BlockSpec((None, ns, LANES), per_b)] * 2,
        out_shape=[out, out],
        compiler_params=_cparams(("arbitrary",)),
        name="nsa_compress",
    )(kc.reshape(b, ns, flat), vc.reshape(b, ns, flat), pe, wa, wb, w2,
      rope_c.reshape(b, ns, LANES), rope_s.reshape(b, ns, LANES))


NSA_TQ = 128
NSA_TK = 512


def _nsa_kernel(n_cmp, n_sel, q_ref, kc_ref, vc_ref, ks_ref, vs_ref, ow_ref, gl_ref, gate_ref, ov_ref,
                o_ref, m_sc, l_sc, acc_sc, ocmp_sc):
    n = pl.program_id(1)
    tq, tk = NSA_TQ, NSA_TK
    ncp = kc_ref.shape[0]
    rep = N_HEADS // NSA_KV
    half = _lane_half((tq, LANES))
    lane = lax.broadcasted_iota(jnp.int32, (tq, LANES), 1)
    t_col = n * tq + lax.broadcasted_iota(jnp.int32, (tq, 1), 0)

    ci = lax.broadcasted_iota(jnp.int32, (tq, ncp), 1)
    cmask = (ci * CMP_STRIDE + (CMP_LEN - 1) <= t_col) & (ci < n_cmp)
    kc = kc_ref[...]
    vc = vc_ref[...]
    psum = [jnp.zeros((tq, ncp), F32) for _ in range(NSA_KV)]
    for p in range(N_PAIRS):
        qp = q_ref[:, p * LANES:(p + 1) * LANES]
        g = (2 * p) // rep
        pair_out = None
        for e in range(2):
            qh = _head_query(qp, e, g)
            s = jnp.where(cmask, _dot_nt(qh, kc), MASK_VALUE)
            m = jnp.max(s, axis=-1, keepdims=True)
            pr = jnp.where(cmask, jnp.exp(s - m), 0.0)
            pr = pr / jnp.maximum(jnp.sum(pr, axis=-1, keepdims=True), 1e-30)
            psum[g] = psum[g] + pr
            oh = _dot(pr.astype(BF16), vc)
            if e != g:
                oh = pltpu.roll(oh, HEAD_DIM, 1)
            pair_out = oh if e == 0 else jnp.where(half == 0, pair_out, oh)
        ocmp_sc[p] = pair_out

    cur = t_col // SEL_LEN
    lane_f = lane.astype(F32)
    forced = (lane == 0) | (lane == cur)
    causal = lane <= cur
    sel = []
    for g in range(NSA_KV):
        imp = jnp.dot(psum[g], ov_ref[...], preferred_element_type=F32,
                      precision=lax.Precision.HIGHEST)
        score = jnp.where(causal, jnp.where(forced, FORCED_SCORE, imp), MASK_VALUE)
        score = jnp.where(lane < n_sel, score, PAD_SCORE)
        chosen = jnp.zeros((tq, LANES), F32)
        for _ in range(min(SEL_TOPK, n_sel)):
            mx = jnp.max(score, axis=-1, keepdims=True)
            first = jnp.min(jnp.where(score == mx, lane_f, float(LANES)), axis=-1, keepdims=True)
            hit = lane_f == first
            chosen = jnp.where(hit, 1.0, chosen)
            score = jnp.where(hit, PAD_SCORE, score)
        sel.append(jnp.where(causal, chosen, 0.0).astype(BF16))

    for g in range(NSA_KV):
        for r in range(rep):
            m_sc[g * rep + r] = jnp.full((tq, LANES), MASK_VALUE, F32)
            l_sc[g * rep + r] = jnp.zeros((tq, LANES), F32)
            acc_sc[g * rep + r] = jnp.zeros((tq, LANES), F32)

    n_chunks = (n * tq + tq + tk - 1) // tk

    def body(c, carry):
        k0 = pl.multiple_of(c * tk, tk)
        kt = ks_ref[pl.ds(k0, tk), :]
        vt = vs_ref[pl.ds(k0, tk), :]
        tok = k0 + lax.broadcasted_iota(jnp.int32, (tq, tk), 1)
        erow = lax.broadcasted_iota(jnp.int32, (LANES, tk), 0)
        ecol = k0 + lax.broadcasted_iota(jnp.int32, (LANES, tk), 1)
        expand = jnp.where(ecol // SEL_LEN == erow, 1.0, 0.0).astype(BF16)
        for g in range(NSA_KV):
            ok = (_dot(sel[g], expand) > 0.5) & (tok <= t_col)
            for r in range(rep):
                h = g * rep + r
                qp = q_ref[:, (h // 2) * LANES:(h // 2 + 1) * LANES]
                qh = _head_query(qp, h % 2, g)
                s = jnp.where(ok, _dot_nt(qh, kt), MASK_VALUE)
                m_old = m_sc[h][:, 0:1]
                m_new = jnp.maximum(m_old, jnp.max(s, axis=-1, keepdims=True))
                pr = jnp.where(ok, jnp.exp(s - m_new), 0.0)
                alpha = jnp.exp(m_old - m_new)
                l_new = alpha * l_sc[h][:, 0:1] + jnp.sum(pr, axis=-1, keepdims=True)
                acc_sc[h] = alpha * acc_sc[h] + _dot(pr.astype(BF16), vt)
                m_sc[h] = jnp.broadcast_to(m_new, (tq, LANES))
                l_sc[h] = jnp.broadcast_to(l_new, (tq, LANES))
        return carry

    lax.fori_loop(0, n_chunks, body, 0)

    gl = 1.0 / (1.0 + jnp.exp(-gl_ref[...]))
    for p in range(N_PAIRS):
        sl = slice(p * LANES, (p + 1) * LANES)
        g = (2 * p) // rep
        oslc = None
        for e in range(2):
            h = 2 * p + e
            oh = acc_sc[h] / jnp.maximum(l_sc[h][:, 0:1], 1e-30)
            if e != g:
                oh = pltpu.roll(oh, HEAD_DIM, 1)
            oslc = oh if e == 0 else jnp.where(half == 0, oslc, oh)
        branches = (ocmp_sc[p], oslc, ow_ref[:, sl].astype(F32))
        out = None
        for j, br in enumerate(branches):
            w0 = gl[:, 3 * (2 * p) + j:3 * (2 * p) + j + 1]
            w1 = gl[:, 3 * (2 * p + 1) + j:3 * (2 * p + 1) + j + 1]
            t = jnp.where(half == 0, w0, w1) * br
            out = t if out is None else out + t
        o_ref[:, sl] = (out * _silu(gate_ref[:, sl].astype(F32))).astype(o_ref.dtype)


def _selection_overlap(n_cmp_pad, n_cmp, n_sel):
    cs = np.arange(n_cmp_pad) * CMP_STRIDE
    js = np.arange(LANES) * SEL_LEN
    ov = np.minimum(cs[:, None] + CMP_LEN, js[None, :] + SEL_LEN) - np.maximum(cs[:, None], js[None, :])
    ov = (np.clip(ov, 0, None) / CMP_LEN).astype(np.float32)
    ov[n_cmp:, :] = 0.0
    ov[:, n_sel:] = 0.0
    return ov


def _nsa_attention(q, k_cmp, v_cmp, ks, vs, o_win, g_logit, gate):
    b, s, _ = q.shape
    tq = NSA_TQ
    ncp = k_cmp.shape[1]
    n_cmp = s // CMP_STRIDE - 1
    n_sel = s // SEL_LEN
    assert n_sel <= LANES and s % NSA_TK == 0
    ov = jnp.asarray(_selection_overlap(ncp, n_cmp, n_sel))
    row = lambda bi, i: (bi, i, 0)
    per_b = lambda bi, i: (bi, 0, 0)
    nh = N_HEADS
    return pl.pallas_call(
        functools.partial(_nsa_kernel, n_cmp, n_sel),
        grid=(b, s // tq),
        in_specs=[pl.BlockSpec((None, tq, MIX_WIDTH), row),
                  pl.BlockSpec((None, ncp, LANES), per_b),
                  pl.BlockSpec((None, ncp, LANES), per_b),
                  pl.BlockSpec((None, s, LANES), per_b),
                  pl.BlockSpec((None, s, LANES), per_b),
                  pl.BlockSpec((None, tq, MIX_WIDTH), row),
                  pl.BlockSpec((None, tq, LANES), row),
                  pl.BlockSpec((None, tq, MIX_WIDTH), row),
                  pl.BlockSpec((ncp, LANES), lambda bi, i: (0, 0))],
        out_specs=pl.BlockSpec((None, tq, MIX_WIDTH), row),
        out_shape=jax.ShapeDtypeStruct((b, s, MIX_WIDTH), BF16),
        scratch_shapes=[pltpu.VMEM((nh, tq, LANES), F32), pltpu.VMEM((nh, tq, LANES), F32),
                        pltpu.VMEM((nh, tq, LANES), F32), pltpu.VMEM((N_PAIRS, tq, LANES), F32)],
        compiler_params=_cparams(("arbitrary", "arbitrary")),
        name="nsa_attn",
    )(q, k_cmp, v_cmp, ks, vs, o_win, g_logit, gate, ov)


def _split(w, sizes):
    offs = np.cumsum([0] + list(sizes))
    return [w[:, int(offs[i]):int(offs[i + 1])] for i in range(len(sizes))]


def _swa_layer(x, rope, gain, w_in, sinks, w_out):
    kvw = SWA_KV * HEAD_DIM
    parts = _split(w_in, [MIX_WIDTH, kvw, kvw, MIX_WIDTH])
    segs = [_Seg(MIX_WIDTH, rope=True, scale=SCALE), _Seg(kvw, rope=True), _Seg(kvw), _Seg(MIX_WIDTH)]
    q, k, v, gate = _norm_proj(x, gain, parts, segs, rope)
    o = _banded_attention(q, k, v, SWA_WINDOW - 1, sinks=sinks, gate=gate)
    return o, w_out


def _dilated_layer(x, rope, gain, w_in, w_out):
    b, s, _ = x.shape
    kvw = DIL_KV * HEAD_DIM
    sizes, segs = [], []
    for window, dil in DIL_PATTERNS:
        assert s % (dil * BLK) == 0
        sizes += [MIX_WIDTH, kvw, kvw]
        segs += [_Seg(MIX_WIDTH, rope=True, scale=SCALE, dil=dil), _Seg(kvw, rope=True, dil=dil),
                 _Seg(kvw, dil=dil)]
    sizes.append(MIX_WIDTH)
    segs.append(_Seg(MIX_WIDTH))
    res = _norm_proj(x, gain, _split(w_in, sizes), segs, rope)
    gate = res[-1]
    os_, lses, dils = [], [], []
    for gi, (window, dil) in enumerate(DIL_PATTERNS):
        q, k, v = res[3 * gi:3 * gi + 3]
        if dil > 1:
            q, k, v = (t.reshape(b * dil, s // dil, t.shape[-1]) for t in (q, k, v))
        o, lse = _banded_attention(q, k, v, window // dil, want_lse=True)
        if dil > 1:
            o = o.reshape(b, dil, s // dil, MIX_WIDTH)
            lse = lse.reshape(b, dil, s // dil, LANES)
        os_.append(o)
        lses.append(lse)
        dils.append(dil)
    return _dil_combine(os_, lses, gate, tuple(dils)), w_out


def _fox_layer(x, gain, w_in, b_f, w_out):
    sizes = [MIX_WIDTH, MIX_WIDTH, MIX_WIDTH, N_HEADS, MIX_WIDTH]
    segs = [_Seg(MIX_WIDTH, scale=SCALE * LOG2E, transposed=True), _Seg(MIX_WIDTH),
            _Seg(MIX_WIDTH, transposed=True), _Seg(LANES, dtype=F32), _Seg(MIX_WIDTH)]
    qt, k, vt, f_logit, gate = _norm_proj(x, gain, _split(w_in, sizes), segs)
    kb = _fox_decay(f_logit, b_f)
    return _fox_attention(qt, k, kb, vt, gate), w_out


def _nsa_layer(x, positions, rope, gain, w_in, pe_k, w1_k, w2_k, pe_v, w1_v, w2_v, w_out):
    b, s, _ = x.shape
    kvw = NSA_KV * HEAD_DIM
    sizes = [MIX_WIDTH] + [kvw] * 6 + [3 * N_HEADS, MIX_WIDTH]
    segs = [_Seg(MIX_WIDTH, rope=True, scale=SCALE), _Seg(kvw), _Seg(kvw), _Seg(kvw, rope=True),
            _Seg(kvw), _Seg(kvw, rope=True), _Seg(kvw), _Seg(LANES, dtype=F32), _Seg(MIX_WIDTH)]
    q, kc, vc, ks, vs, kw, vw, g_logit, gate = _norm_proj(x, gain, _split(w_in, sizes), segs, rope)
    ns = s // CMP_STRIDE
    cmp_pos = jnp.concatenate(
        [positions[:, CMP_LEN - 1::CMP_STRIDE], positions[:, -1:]], axis=1)[:, :ns]
    cmp_c, cmp_s = _rope_tables(cmp_pos.reshape(-1))
    k_cmp, v_cmp = _nsa_compress(kc, vc, pe_k, w1_k, w2_k, pe_v, w1_v, w2_v, cmp_c, cmp_s)
    o_win = _banded_attention(q, kw, vw, NSA_WINDOW - 1)
    return _nsa_attention(q, k_cmp, v_cmp, ks, vs, o_win, g_logit, gate), w_out


def kernel(x, positions, norm_0, w_in_0, sinks_0, w_out_0, norm_1, w_in_1, w_out_1, norm_2, w_in_2, b_f_2, w_out_2, norm_3, w_in_3, cmp_pe_k_3, cmp_w1_k_3, cmp_w2_k_3, cmp_pe_v_3, cmp_w1_v_3, cmp_w2_v_3, w_out_3, final_norm):
    rope = _rope_tables(positions.reshape(-1))
    o, w = _swa_layer(x, rope, norm_0, w_in_0, sinks_0, w_out_0)
    x = _out_proj(o, w, x)
    o, w = _dilated_layer(x, rope, norm_1, w_in_1, w_out_1)
    x = _out_proj(o, w, x)
    o, w = _fox_layer(x, norm_2, w_in_2, b_f_2, w_out_2)
    x = _out_proj(o, w, x)
    o, w = _nsa_layer(x, positions, rope, norm_3, w_in_3, cmp_pe_k_3, cmp_w1_k_3, cmp_w2_k_3,
                      cmp_pe_v_3, cmp_w1_v_3, cmp_w2_v_3, w_out_3)
    return _out_proj(o, w, x, final_gain=final_norm)
```

```python
import functools
import math

import numpy as np
import jax
import jax.numpy as jnp
from jax import lax
from jax.experimental import pallas as pl
from jax.experimental.pallas import tpu as pltpu

HEAD_DIM = 64
N_HEADS = 16
N_PAIRS = N_HEADS // 2
MIX_WIDTH = N_HEADS * HEAD_DIM
ROT_DIM = HEAD_DIM // 4
ROT_HALF = ROT_DIM // 2
ROPE_THETA = 500000.0
BLK = 128
LANES = 128
NORM_EPS = 1e-6
MASK_VALUE = -1e30
PAD_SCORE = -3e38
LOG2E = math.log2(math.e)
Q_SCALE = HEAD_DIM ** -0.5 * LOG2E

SWA_KV = 4
SWA_WINDOW = 128
DIL_KV = 4
DIL_PATTERNS = ((128, 1), (512, 4), (2048, 16))
NSA_KV = 2
CMP_LEN = 32
CMP_STRIDE = 16
CMP_HIDDEN = 256
SEL_LEN = 64
SEL_TOPK = 8
NSA_WINDOW = 256
FORCED_SCORE = 1e4

VMEM_LIMIT_BYTES = 56 * 1024 * 1024
PROJ_ROWS = 512
PROJ_COLS = 512
ACC_ROWS = HEAD_DIM + 16
F32 = jnp.float32
BF16 = jnp.bfloat16


def _cparams(sem):
    return pltpu.CompilerParams(dimension_semantics=sem, vmem_limit_bytes=VMEM_LIMIT_BYTES)


def _lane_half(shape):
    return lax.broadcasted_iota(jnp.int32, shape, 1) // HEAD_DIM


def _swap_halves(t):
    return jnp.concatenate([t[:, HEAD_DIM:], t[:, :HEAD_DIM]], axis=1)


def _head_query(qp, e, kv_half):
    qh = jnp.where(_lane_half(qp.shape) == e, qp, jnp.zeros_like(qp))
    if e != kv_half:
        qh = _swap_halves(qh)
    return qh


def _group_queries(q_ref, g, rep):
    tiles = []
    for r in range(rep):
        h = g * rep + r
        tiles.append(_head_query(q_ref[:, (h // 2) * LANES:(h // 2 + 1) * LANES], h % 2, g % 2))
    return jnp.concatenate(tiles, axis=0)


def _dot_nt(a, b):
    return lax.dot_general(a, b, (((1,), (1,)), ((), ())), preferred_element_type=F32)


def _dot(a, b):
    return jnp.dot(a, b, preferred_element_type=F32)


def _silu(x):
    return x * (1.0 / (1.0 + jnp.exp(-x)))


def _with_ones(vt):
    return jnp.concatenate([vt, jnp.ones((ACC_ROWS - HEAD_DIM, vt.shape[1]), BF16)], axis=0)


def _rope_table_kernel(pos_ref, inv_ref, c_ref, s_ref):
    pos = pos_ref[...].astype(F32)
    ang = pos * inv_ref[...]
    d = lax.broadcasted_iota(jnp.int32, ang.shape, 1) % HEAD_DIM
    cos = jnp.cos(ang)
    sin = jnp.sin(ang)
    c_ref[...] = jnp.where(d < ROT_DIM, cos, 1.0)
    s_ref[...] = jnp.where(d < ROT_HALF, -sin, jnp.where(d < ROT_DIM, sin, 0.0))


def _rope_tables(pos_flat):
    t = pos_flat.shape[0]
    rows = min(t, 2048)
    assert t % rows == 0
    inv = jnp.power(ROPE_THETA, -jnp.arange(ROT_HALF, dtype=F32) / ROT_HALF)
    inv_l = jnp.tile(inv, LANES // ROT_HALF)[None, :]
    out = jax.ShapeDtypeStruct((t, LANES), F32)
    return pl.pallas_call(
        _rope_table_kernel,
        grid=(t // rows,),
        in_specs=[pl.BlockSpec((rows, 1), lambda i: (i, 0)),
                  pl.BlockSpec((1, LANES), lambda i: (0, 0))],
        out_specs=[pl.BlockSpec((rows, LANES), lambda i: (i, 0))] * 2,
        out_shape=[out, out],
        compiler_params=_cparams(("arbitrary",)),
        name="rope_tables",
    )(pos_flat[:, None], inv_l)


def _apply_rope(y, c, s):
    outs = []
    for j in range(y.shape[1] // LANES):
        t = y[:, j * LANES:(j + 1) * LANES]
        d = lax.broadcasted_iota(jnp.int32, t.shape, 1) % HEAD_DIM
        partner = jnp.where(d < ROT_HALF, pltpu.roll(t, LANES - ROT_HALF, 1),
                            pltpu.roll(t, ROT_HALF, 1))
        outs.append(t * c + partner * s)
    return outs[0] if len(outs) == 1 else jnp.concatenate(outs, axis=1)


class _Seg:
    def __init__(self, width, rope=False, scale=None, dtype=BF16, dil=1, tile=None):
        self.width, self.rope, self.scale, self.dtype, self.dil = width, rope, scale, dtype, dil
        self.tile = tile
        assert not (tile and (rope or dil > 1))


def _norm_proj_kernel(segs, use_rope, *refs):
    has_t = any(sg.tile for sg in segs)
    x_ref, g_ref, w_ref = refs[:3]
    k = 3
    if has_t:
        wt_ref = refs[k]
        k += 1
    if use_rope:
        c_ref, s_ref = refs[k:k + 2]
        k += 2
    out_refs = refs[k:k + len(segs)]
    stage_ref = refs[k + len(segs)] if any(sg.dil > 1 for sg in segs) else None

    x = x_ref[...]
    var = jnp.mean(x * x, axis=-1, keepdims=True)
    h = (x * lax.rsqrt(var + NORM_EPS) * g_ref[...]).astype(BF16)
    rows = x.shape[0]
    col = 0
    tcol = 0
    for sg, o_ref in zip(segs, out_refs):
        if sg.tile:
            for c0 in range(0, sg.width, PROJ_COLS):
                cw = min(PROJ_COLS, sg.width - c0)
                yt = _dot_nt(wt_ref[tcol + c0:tcol + c0 + cw, :], h)
                if sg.scale is not None:
                    yt = yt * sg.scale
                for ti in range(rows // sg.tile):
                    o_ref[ti, c0:c0 + cw, :] = yt[:, ti * sg.tile:(ti + 1) * sg.tile].astype(sg.dtype)
            tcol += sg.width
            continue
        for c0 in range(0, sg.width, PROJ_COLS):
            cw = min(PROJ_COLS, sg.width - c0)
            y = _dot(h, w_ref[:, col + c0:col + c0 + cw])
            if sg.rope:
                y = _apply_rope(y, c_ref[...], s_ref[...])
            if sg.scale is not None:
                y = y * sg.scale
            if sg.dil > 1:
                sub = rows // sg.dil
                for j in range(cw // LANES):
                    stage_ref[j] = y[:, j * LANES:(j + 1) * LANES]
                for r in range(sg.dil):
                    for j in range(cw // LANES):
                        lo = c0 + j * LANES
                        o_ref[r, :, lo:lo + LANES] = (
                            stage_ref[j, pl.ds(r, sub, stride=sg.dil), :].astype(sg.dtype))
            else:
                o_ref[:, c0:c0 + cw] = y.astype(sg.dtype)
        col += sg.width


def _norm_proj(x, gain, w_parts, segs, rope=None):
    b, s, d = x.shape
    tm = PROJ_ROWS
    assert s % tm == 0
    w_cols, wt_rows = [], []
    for wp, sg in zip(w_parts, segs):
        if wp.shape[1] < sg.width:
            wp = jnp.pad(wp, ((0, 0), (0, sg.width - wp.shape[1])))
        if sg.tile:
            wt_rows.append(wp.T)
        else:
            w_cols.append(wp)
    w = jnp.concatenate(w_cols, axis=1).astype(BF16)
    n = w.shape[1]
    use_rope = rope is not None
    in_specs = [pl.BlockSpec((None, tm, d), lambda bi, i: (bi, i, 0)),
                pl.BlockSpec((1, d), lambda bi, i: (0, 0)),
                pl.BlockSpec((d, n), lambda bi, i: (0, 0))]
    args = [x, gain[None, :], w]
    if wt_rows:
        wt = jnp.concatenate(wt_rows, axis=0).astype(BF16)
        in_specs.append(pl.BlockSpec(wt.shape, lambda bi, i: (0, 0)))
        args.append(wt)
    if use_rope:
        in_specs += [pl.BlockSpec((None, tm, LANES), lambda bi, i: (bi, i, 0))] * 2
        args += [rope[0].reshape(b, s, LANES), rope[1].reshape(b, s, LANES)]
    out_specs, out_shape = [], []
    for sg in segs:
        if sg.dil > 1:
            assert tm % sg.dil == 0
            out_shape.append(jax.ShapeDtypeStruct((b, sg.dil, s // sg.dil, sg.width), sg.dtype))
            out_specs.append(pl.BlockSpec((None, sg.dil, tm // sg.dil, sg.width),
                                          lambda bi, i: (bi, 0, i, 0)))
        elif sg.tile:
            assert tm % sg.tile == 0
            out_shape.append(jax.ShapeDtypeStruct((b, s // sg.tile, sg.width, sg.tile), sg.dtype))
            out_specs.append(pl.BlockSpec((None, tm // sg.tile, sg.width, sg.tile),
                                          lambda bi, i: (bi, i, 0, 0)))
        else:
            out_shape.append(jax.ShapeDtypeStruct((b, s, sg.width), sg.dtype))
            out_specs.append(pl.BlockSpec((None, tm, sg.width), lambda bi, i: (bi, i, 0)))
    scratch = ([pltpu.VMEM((PROJ_COLS // LANES, tm, LANES), F32)]
               if any(sg.dil > 1 for sg in segs) else [])
    return pl.pallas_call(
        functools.partial(_norm_proj_kernel, segs, use_rope),
        grid=(b, s // tm),
        in_specs=in_specs,
        out_specs=out_specs,
        out_shape=out_shape,
        scratch_shapes=scratch,
        compiler_params=_cparams(("arbitrary", "arbitrary")),
        name="norm_proj",
    )(*args)


def _out_proj_kernel(final, *refs):
    if final:
        o_ref, w_ref, x_ref, g_ref, y_ref = refs
    else:
        o_ref, w_ref, x_ref, y_ref = refs
    y = x_ref[...] + _dot(o_ref[...], w_ref[...])
    if final:
        var = jnp.mean(y * y, axis=-1, keepdims=True)
        y = y * lax.rsqrt(var + NORM_EPS) * g_ref[...]
    y_ref[...] = y


def _out_proj(o, w, x, final_gain=None):
    b, s, d = x.shape
    m = o.shape[-1]
    tm = PROJ_ROWS
    final = final_gain is not None
    row = lambda bi, i: (bi, i, 0)
    in_specs = [pl.BlockSpec((None, tm, m), row),
                pl.BlockSpec((m, d), lambda bi, i: (0, 0)),
                pl.BlockSpec((None, tm, d), row)]
    args = [o, w.astype(BF16), x]
    if final:
        in_specs.append(pl.BlockSpec((1, d), lambda bi, i: (0, 0)))
        args.append(final_gain[None, :])
    return pl.pallas_call(
        functools.partial(_out_proj_kernel, final),
        grid=(b, s // tm),
        in_specs=in_specs,
        out_specs=pl.BlockSpec((None, tm, d), row),
        out_shape=jax.ShapeDtypeStruct((b, s, d), F32),
        compiler_params=_cparams(("arbitrary", "arbitrary")),
        name="out_proj",
    )(*args)


def _banded_kernel(max_dist, n_prev, n_kv, has_sink, has_gate, want_lse, *refs):
    q_ref = refs[0]
    k_refs = refs[1:2 + n_prev]
    v_refs = refs[2 + n_prev:3 + 2 * n_prev]
    k = 3 + 2 * n_prev
    sink_ref = gate_ref = lse_ref = None
    if has_sink:
        sink_ref = refs[k]; k += 1
    if has_gate:
        gate_ref = refs[k]; k += 1
    o_ref = refs[k]; k += 1
    if want_lse:
        lse_ref = refs[k]; k += 1
    ot_sc = refs[k]

    n = pl.program_id(1)
    tq = BLK
    kw = (n_prev + 1) * BLK
    rep = N_HEADS // n_kv
    krow = lax.broadcasted_iota(jnp.int32, (kw, tq), 0) - n_prev * BLK
    dist = lax.broadcasted_iota(jnp.int32, (kw, tq), 1) - krow
    ok = (dist >= 0) & (dist <= max_dist) & (n * BLK + krow >= 0)
    lse_rows = []
    for gp in range(n_kv // 2):
        sl = slice(gp * LANES, (gp + 1) * LANES)
        kcat = jnp.concatenate([kr[:, sl] for kr in k_refs], axis=0)
        vcat = jnp.concatenate([vr[:, sl] for vr in v_refs], axis=0)
        vt = vcat.astype(F32).T.astype(BF16)
        for gh in range(2):
            g = 2 * gp + gh
            st = _dot_nt(kcat, _group_queries(q_ref, g, rep))
            ms, pts = [], []
            for r in range(rep):
                s_r = jnp.where(ok, st[:, r * tq:(r + 1) * tq], MASK_VALUE)
                m = jnp.max(s_r, axis=0, keepdims=True)
                if has_sink:
                    h = g * rep + r
                    m = jnp.maximum(m, sink_ref[0:1, h:h + 1])
                pts.append(jnp.exp2(s_r - m).astype(BF16))
                ms.append(m)
            acc = _dot(_with_ones(vt[gh * HEAD_DIM:(gh + 1) * HEAD_DIM, :]),
                       jnp.concatenate(pts, axis=1))
            for r in range(rep):
                h = g * rep + r
                den = acc[HEAD_DIM:HEAD_DIM + 1, r * tq:(r + 1) * tq]
                if has_sink:
                    den = den + jnp.exp2(sink_ref[0:1, h:h + 1] - ms[r])
                ot_sc[h] = acc[:HEAD_DIM, r * tq:(r + 1) * tq] / den
                if want_lse:
                    lse_rows.append(ms[r] + jnp.log2(den))
    for p in range(N_PAIRS):
        sl = slice(p * LANES, (p + 1) * LANES)
        pair = jnp.concatenate([ot_sc[2 * p], ot_sc[2 * p + 1]], axis=0).T
        if has_gate:
            pair = pair * _silu(gate_ref[:, sl].astype(F32))
        o_ref[:, sl] = pair.astype(o_ref.dtype)
    if want_lse:
        lse_t = jnp.concatenate(lse_rows + [jnp.zeros((LANES - N_HEADS, tq), F32)], axis=0)
        lse_ref[...] = lse_t.T


def _banded_attention(q, k, v, max_dist, sinks=None, gate=None, want_lse=False):
    bq, sq, _ = q.shape
    kvw = k.shape[-1]
    n_kv = kvw // HEAD_DIM
    n_prev = -(-max_dist // BLK)
    nb = sq // BLK
    row = lambda b, i: (b, i, 0)
    in_specs = [pl.BlockSpec((None, BLK, MIX_WIDTH), row)]
    args = [q]
    for arr in (k, v):
        for j in range(n_prev, -1, -1):
            in_specs.append(pl.BlockSpec((None, BLK, kvw),
                                         lambda b, i, j=j: (b, jnp.maximum(i - j, 0), 0)))
            args.append(arr)
    if sinks is not None:
        in_specs.append(pl.BlockSpec((1, LANES), lambda b, i: (0, 0)))
        args.append(jnp.pad(sinks.astype(F32) * LOG2E, (0, LANES - N_HEADS))[None, :])
    if gate is not None:
        in_specs.append(pl.BlockSpec((None, BLK, MIX_WIDTH), row))
        args.append(gate)
    out_specs = [pl.BlockSpec((None, BLK, MIX_WIDTH), row)]
    out_shape = [jax.ShapeDtypeStruct((bq, sq, MIX_WIDTH), BF16)]
    if want_lse:
        out_specs.append(pl.BlockSpec((None, BLK, LANES), row))
        out_shape.append(jax.ShapeDtypeStruct((bq, sq, LANES), F32))
    res = pl.pallas_call(
        functools.partial(_banded_kernel, max_dist, n_prev, n_kv, sinks is not None,
                          gate is not None, want_lse),
        grid=(bq, nb),
        in_specs=in_specs,
        out_specs=out_specs,
        out_shape=out_shape,
        scratch_shapes=[pltpu.VMEM((N_HEADS, HEAD_DIM, BLK), F32)],
        compiler_params=_cparams(("arbitrary", "arbitrary")),
        name="banded_attn",
    )(*args)
    return res if want_lse else res[0]


def _dil_combine_kernel(dils, *refs):
    ng = len(dils)
    o_refs, l_refs = refs[:ng], refs[ng:2 * ng]
    gate_ref, out_ref, stage_ref, lstage_ref = refs[2 * ng:]
    rows = out_ref.shape[0]

    def natural(ref, dil, stage, slab):
        sl = slice(slab * LANES, (slab + 1) * LANES)
        if dil == 1:
            return ref[:, sl].astype(F32)
        sub = rows // dil
        for r in range(dil):
            stage[pl.ds(r, sub, stride=dil), :] = ref[r, :, sl].astype(F32)
        return stage[...]

    lses = [natural(l_refs[i], dils[i], lstage_ref, 0) for i in range(ng)]
    mx = functools.reduce(jnp.maximum, lses)
    ws = [jnp.exp2(l - mx) for l in lses]
    tot = functools.reduce(lambda a, c: a + c, ws)
    ws = [w / tot for w in ws]
    half = _lane_half((rows, LANES))
    outs = [None] * N_PAIRS
    for gi in range(ng):
        for p in range(N_PAIRS):
            og = natural(o_refs[gi], dils[gi], stage_ref, p)
            w0 = ws[gi][:, 2 * p:2 * p + 1]
            w1 = ws[gi][:, 2 * p + 1:2 * p + 2]
            wp = jnp.where(half == 0, w0, w1)
            t = wp * og
            outs[p] = t if outs[p] is None else outs[p] + t
    for p in range(N_PAIRS):
        sl = slice(p * LANES, (p + 1) * LANES)
        out_ref[:, sl] = (outs[p] * _silu(gate_ref[:, sl].astype(F32))).astype(out_ref.dtype)


def _dil_combine(os_, lses, gate, dils):
    b, s, _ = gate.shape
    tm = 256
    in_specs, args = [], []
    for arrs, width in ((os_, MIX_WIDTH), (lses, LANES)):
        for arr, dil in zip(arrs, dils):
            if dil == 1:
                in_specs.append(pl.BlockSpec((None, tm, width), lambda bi, i: (bi, i, 0)))
            else:
                in_specs.append(pl.BlockSpec((None, dil, tm // dil, width),
                                             lambda bi, i: (bi, 0, i, 0)))
            args.append(arr)
    in_specs.append(pl.BlockSpec((None, tm, MIX_WIDTH), lambda bi, i: (bi, i, 0)))
    args.append(gate)
    return pl.pallas_call(
        functools.partial(_dil_combine_kernel, dils),
        grid=(b, s // tm),
        in_specs=in_specs,
        out_specs=pl.BlockSpec((None, tm, MIX_WIDTH), lambda bi, i: (bi, i, 0)),
        out_shape=jax.ShapeDtypeStruct((b, s, MIX_WIDTH), BF16),
        scratch_shapes=[pltpu.VMEM((tm, LANES), F32), pltpu.VMEM((tm, LANES), F32)],
        compiler_params=_cparams(("arbitrary", "arbitrary")),
        name="dil_combine",
    )(*args)


N_BIAS_PIECES = 3


def _fox_decay_kernel(f_ref, b_ref, kb_ref):
    s = f_ref.shape[0]
    x = f_ref[...] + b_ref[...]
    logf = jnp.minimum(x, 0.0) - jnp.log1p(jnp.exp(-jnp.abs(x)))
    r = lax.broadcasted_iota(jnp.int32, (BLK, BLK), 0)
    c = lax.broadcasted_iota(jnp.int32, (BLK, BLK), 1)
    tri = (c <= r).astype(F32)
    lane = lax.broadcasted_iota(jnp.int32, (BLK, LANES), 1)
    carry = jnp.zeros((1, LANES), F32)
    for i in range(s // BLK):
        blk = logf[i * BLK:(i + 1) * BLK, :]
        cs = jnp.dot(tri, blk, preferred_element_type=F32, precision=lax.Precision.HIGHEST) + carry
        carry = cs[BLK - 1:BLK, :]
        rest = cs * (-LOG2E)
        out = jnp.zeros((BLK, LANES), F32)
        for j in range(N_BIAS_PIECES):
            piece = rest.astype(BF16).astype(F32)
            rest = rest - piece
            moved = piece if j == 0 else pltpu.roll(piece, N_HEADS * j, 1)
            out = jnp.where((lane >= N_HEADS * j) & (lane < N_HEADS * (j + 1)), moved, out)
        kb_ref[i * BLK:(i + 1) * BLK, :] = out.astype(BF16)


def _fox_decay(f_logit, b_f):
    b, s, _ = f_logit.shape
    b_pad = jnp.pad(b_f.astype(F32), (0, LANES - N_HEADS))[None, :]
    return pl.pallas_call(
        _fox_decay_kernel,
        grid=(b,),
        in_specs=[pl.BlockSpec((None, s, LANES), lambda bi: (bi, 0, 0)),
                  pl.BlockSpec((1, LANES), lambda bi: (0, 0))],
        out_specs=pl.BlockSpec((None, s, LANES), lambda bi: (bi, 0, 0)),
        out_shape=jax.ShapeDtypeStruct((b, s, LANES), BF16),
        compiler_params=_cparams(("arbitrary",)),
        name="fox_decay",
    )(f_logit, b_pad)


FOX_T = PROJ_ROWS


def _fox_kernel(qt_ref, k_ref, kb_ref, vt_ref, gate_ref, o_ref, m_sc, acc_sc, st_sc):
    pi = pl.program_id(1)
    i = pl.program_id(2)
    t = FOX_T
    qt = qt_ref[0]
    row = lax.broadcasted_iota(jnp.int32, (LANES, t), 0)
    zeros = jnp.zeros((HEAD_DIM, t), BF16)
    qx = []
    for e in range(2):
        h = 2 * pi + e
        pick = (row % N_HEADS == h) & (row < N_HEADS * N_BIAS_PIECES)
        sel = jnp.where(pick, 1.0, 0.0).astype(BF16)
        mine = qt[e * HEAD_DIM:(e + 1) * HEAD_DIM, :]
        top = [mine, zeros] if e == 0 else [zeros, mine]
        qx.append(jnp.concatenate(top + [sel], axis=0))
        m_sc[e] = jnp.full((1, t), MASK_VALUE, F32)
        acc_sc[e] = jnp.zeros((ACC_ROWS, t), F32)

    tk = t // 2

    def scores(jt, hf, e):
        k0 = pl.multiple_of(jt * t + hf * tk, tk)
        kx = jnp.concatenate([k_ref[pl.ds(k0, tk), :], kb_ref[pl.ds(k0, tk), :]], axis=1)
        st_sc[e, hf] = _dot(kx, qx[e])

    def consume(jt, hf, e, diag):
        st = st_sc[e, hf]
        if diag:
            ok = (lax.broadcasted_iota(jnp.int32, (tk, t), 0) + hf * tk
                  <= lax.broadcasted_iota(jnp.int32, (tk, t), 1))
            st = jnp.where(ok, st, MASK_VALUE)
        m_old = m_sc[e]
        m_new = jnp.maximum(m_old, jnp.max(st, axis=0, keepdims=True))
        pt = jnp.exp2(st - m_new).astype(BF16)
        alpha = jnp.exp2(m_old - m_new)
        vt = vt_ref[jt, e * HEAD_DIM:(e + 1) * HEAD_DIM, hf * tk:(hf + 1) * tk]
        acc_sc[e] = alpha * acc_sc[e] + _dot(_with_ones(vt), pt)
        m_sc[e] = m_new

    for e in range(2):
        scores(0, 0, e)

    def body(jt, carry):
        for e in range(2):
            scores(jt, 1, e)
            consume(jt, 0, e, False)
        for e in range(2):
            scores(jt + 1, 0, e)
            consume(jt, 1, e, False)
        return carry

    lax.fori_loop(0, i, body, 0)
    for e in range(2):
        scores(i, 1, e)
        consume(i, 0, e, True)
    for e in range(2):
        consume(i, 1, e, True)
    outs = []
    for e in range(2):
        acc = acc_sc[e]
        outs.append(acc[:HEAD_DIM] / jnp.maximum(acc[HEAD_DIM:HEAD_DIM + 1], 1e-30))
    out_t = jnp.concatenate(outs, axis=0)
    o_ref[...] = (out_t.T * _silu(gate_ref[...].astype(F32))).astype(o_ref.dtype)


def _fox_attention(qt, k, kb, vt, gate):
    b, s, _ = k.shape
    t = FOX_T
    assert s % t == 0 and vt.shape == (b, s // t, MIX_WIDTH, t) and qt.shape == vt.shape
    tile = lambda bi, p, i: (bi, i, p)
    return pl.pallas_call(
        _fox_kernel,
        grid=(b, N_PAIRS, s // t),
        in_specs=[pl.BlockSpec((None, 1, LANES, t), lambda bi, p, i: (bi, i, p, 0)),
                  pl.BlockSpec((None, s, LANES), lambda bi, p, i: (bi, 0, p)),
                  pl.BlockSpec((None, s, LANES), lambda bi, p, i: (bi, 0, 0)),
                  pl.BlockSpec((None, s // t, LANES, t), lambda bi, p, i: (bi, 0, p, 0)),
                  pl.BlockSpec((None, t, LANES), tile)],
        out_specs=pl.BlockSpec((None, t, LANES), tile),
        out_shape=jax.ShapeDtypeStruct((b, s, MIX_WIDTH), BF16),
        scratch_shapes=[pltpu.VMEM((2, 1, t), F32), pltpu.VMEM((2, ACC_ROWS, t), F32),
                        pltpu.VMEM((2, 2, t // 2, t), F32)],
        compiler_params=_cparams(("arbitrary", "arbitrary", "arbitrary")),
        name="fox_attn",
    )(qt, k, kb, vt, gate)


def _gelu_tanh(x):
    return 0.5 * x * (1.0 + jnp.tanh(math.sqrt(2.0 / math.pi) * (x + 0.044715 * (x * x * x))))


def _nsa_compress_kernel(ak_ref, av_ref, pe_ref, wa_ref, wb_ref, w2_ref, w2t_ref, c_ref, s_ref,
                         kc_ref, vct_ref):
    nrow = ak_ref.shape[0]
    for idx, a_ref in enumerate((ak_ref, av_ref)):
        a = a_ref[...].astype(F32)
        xa = (a + pe_ref[idx, 0:1, :]).astype(BF16)
        xb = (a + pe_ref[idx, 1:2, :]).astype(BF16)
        ya = _dot(xa, wa_ref[idx])
        yb = _dot(xb, wb_ref[idx])
        hid = _gelu_tanh(ya + pltpu.roll(yb, nrow - 1, 0)).astype(BF16)
        if idx == 0:
            y = _dot(hid, w2_ref[...])
            kc_ref[...] = _apply_rope(y, c_ref[...], s_ref[...]).astype(kc_ref.dtype)
        else:
            vct_ref[...] = _dot_nt(w2t_ref[...], hid).astype(vct_ref.dtype)


def _nsa_compress(kc, vc, pe_k, w1_k, w2_k, pe_v, w1_v, w2_v, rope_c, rope_s):
    b, s, _ = kc.shape
    ns = s // CMP_STRIDE
    g = NSA_KV
    flat = CMP_STRIDE * g * HEAD_DIM

    def w1_halves(w1):
        w1r = w1.reshape(2, CMP_STRIDE, HEAD_DIM, CMP_HIDDEN)
        outs = []
        for hf in range(2):
            z = jnp.zeros((CMP_STRIDE, g, HEAD_DIM, g, CMP_HIDDEN), F32)
            for gi in range(g):
                z = z.at[:, gi, :, gi, :].set(w1r[hf])
            outs.append(z.reshape(flat, g * CMP_HIDDEN))
        return outs

    def pe_halves(pe):
        per = pe.reshape(2, CMP_STRIDE, 1, HEAD_DIM)
        return jnp.broadcast_to(per, (2, CMP_STRIDE, g, HEAD_DIM)).reshape(2, flat)

    def w2_bd(w2):
        z = jnp.zeros((g, CMP_HIDDEN, g, HEAD_DIM), F32)
        for gi in range(g):
            z = z.at[gi, :, gi, :].set(w2)
        return z.reshape(g * CMP_HIDDEN, g * HEAD_DIM)

    ka, kb = w1_halves(w1_k)
    va, vb = w1_halves(w1_v)
    wa = jnp.stack([ka, va]).astype(BF16)
    wb = jnp.stack([kb, vb]).astype(BF16)
    w2 = w2_bd(w2_k).astype(BF16)
    w2t = w2_bd(w2_v).T.astype(BF16)
    pe = jnp.stack([pe_halves(pe_k), pe_halves(pe_v)]).astype(F32)
    whole = lambda bi: (0, 0, 0)
    per_b = lambda bi: (bi, 0, 0)
    return pl.pallas_call(
        _nsa_compress_kernel,
        grid=(b,),
        in_specs=[pl.BlockSpec((None, ns, flat), per_b),
                  pl.BlockSpec((None, ns, flat), per_b),
                  pl.BlockSpec((2, 2, flat), whole),
                  pl.BlockSpec((2, flat, g * CMP_HIDDEN), whole),
                  pl.BlockSpec((2, flat, g * CMP_HIDDEN), whole),
                  pl.BlockSpec((g * CMP_HIDDEN, LANES), lambda bi: (0, 0)),
                  pl.BlockSpec((LANES, g * CMP_HIDDEN), lambda bi: (0, 0)),
                  pl.BlockSpec((None, ns, LANES), per_b),
                  pl.BlockSpec((None, ns, LANES), per_b)],
        out_specs=[pl.BlockSpec((None, ns, LANES), per_b),
                   pl.BlockSpec((None, LANES, ns), per_b)],
        out_shape=[jax.ShapeDtypeStruct((b, ns, LANES), BF16),
                   jax.ShapeDtypeStruct((b, LANES, ns), BF16)],
        compiler_params=_cparams(("arbitrary",)),
        name="nsa_compress",
    )(kc.reshape(b, ns, flat), vc.reshape(b, ns, flat), pe, wa, wb, w2, w2t,
      rope_c.reshape(b, ns, LANES), rope_s.reshape(b, ns, LANES))


NSA_TQ = 128
NSA_TK = 256
NSA_REP = N_HEADS // NSA_KV


def _nsa_kernel(n_cmp, n_sel, n_win, *refs):
    q_ref, kc_ref, vct_ref, ks_ref, vst_ref = refs[:5]
    kw_refs = refs[5:5 + n_win]
    vwt_refs = refs[5 + n_win:5 + 2 * n_win]
    gl_ref, gate_ref, ovt_ref, o_ref = refs[5 + 2 * n_win:9 + 2 * n_win]
    st_sc, m_sc, acc_sc, sel_sc, ocmp_sc, owin_sc = refs[9 + 2 * n_win:]

    n = pl.program_id(1)
    tq, tk, rep = NSA_TQ, NSA_TK, NSA_REP
    ncp = kc_ref.shape[0]
    t_lane = n * tq + lax.broadcasted_iota(jnp.int32, (1, tq), 1)
    qg = [_group_queries(q_ref, g, rep) for g in range(NSA_KV)]

    ci = lax.broadcasted_iota(jnp.int32, (ncp, tq), 0)
    cmask = (ci * CMP_STRIDE + (CMP_LEN - 1) <= t_lane) & (ci < n_cmp)
    psum = []
    for g in range(NSA_KV):
        st = _dot_nt(kc_ref[...], qg[g])
        ps, tot = [], None
        for r in range(rep):
            s_r = jnp.where(cmask, st[:, r * tq:(r + 1) * tq], MASK_VALUE)
            m = jnp.max(s_r, axis=0, keepdims=True)
            pr = jnp.where(cmask, jnp.exp2(s_r - m), 0.0)
            pr = pr * (1.0 / jnp.maximum(jnp.sum(pr, axis=0, keepdims=True), 1e-30))
            tot = pr if tot is None else tot + pr
            ps.append(pr.astype(BF16))
        psum.append(tot)
        ocmp_sc[g] = _dot(vct_ref[g * HEAD_DIM:(g + 1) * HEAD_DIM, :], jnp.concatenate(ps, axis=1))

    rowi = lax.broadcasted_iota(jnp.int32, (LANES, tq), 0)
    rowf = rowi.astype(F32)
    cur = t_lane // SEL_LEN
    forced = (rowi == 0) | (rowi == cur)
    causal = rowi <= cur
    for g in range(NSA_KV):
        imp = jnp.dot(ovt_ref[...], psum[g], preferred_element_type=F32,
                      precision=lax.Precision.HIGHEST)
        score = jnp.where(causal, jnp.where(forced, FORCED_SCORE, imp), MASK_VALUE)
        score = jnp.where(rowi < n_sel, score, PAD_SCORE)
        chosen = jnp.zeros((LANES, tq), F32)
        for _ in range(min(SEL_TOPK, n_sel)):
            mx = jnp.max(score, axis=0, keepdims=True)
            first = jnp.min(jnp.where(score == mx, rowf, float(LANES)), axis=0, keepdims=True)
            hit = rowf == first
            chosen = jnp.where(hit, 1.0, chosen)
            score = jnp.where(hit, PAD_SCORE, score)
        sel_sc[g] = jnp.where(causal, chosen, 0.0)

    kwn = n_win * BLK
    krow = lax.broadcasted_iota(jnp.int32, (kwn, tq), 0) - (n_win - 1) * BLK
    dist = lax.broadcasted_iota(jnp.int32, (kwn, tq), 1) - krow
    wok = (dist >= 0) & (dist <= NSA_WINDOW - 1) & (n * BLK + krow >= 0)
    kwcat = jnp.concatenate([kr[...] for kr in kw_refs], axis=0)
    vwt = jnp.concatenate([vr[0] for vr in vwt_refs], axis=1)
    for g in range(NSA_KV):
        st = _dot_nt(kwcat, qg[g])
        pts = []
        for r in range(rep):
            s_r = jnp.where(wok, st[:, r * tq:(r + 1) * tq], MASK_VALUE)
            m = jnp.max(s_r, axis=0, keepdims=True)
            pts.append(jnp.exp2(s_r - m).astype(BF16))
        owin_sc[g] = _dot(_with_ones(vwt[g * HEAD_DIM:(g + 1) * HEAD_DIM, :]),
                          jnp.concatenate(pts, axis=1))

    for g in range(NSA_KV):
        m_sc[g] = jnp.full((1, rep * tq), MASK_VALUE, F32)
        acc_sc[g] = jnp.zeros((ACC_ROWS, rep * tq), F32)
    blocks_per_step = tk // SEL_LEN

    def scores(c, g):
        k0 = pl.multiple_of(c * tk, tk)
        st_sc[g] = _dot_nt(ks_ref[pl.ds(k0, tk), :], qg[g])

    def consume(c, g):
        tok = c * tk + lax.broadcasted_iota(jnp.int32, (tk, tq), 0)
        picked = jnp.concatenate(
            [jnp.broadcast_to(sel_sc[g, pl.ds(c * blocks_per_step + j, 1), :], (SEL_LEN, tq))
             for j in range(blocks_per_step)], axis=0)
        ok = (picked > 0.5) & (tok <= t_lane)
        m_old = m_sc[g]
        ms, pts = [], []
        for r in range(rep):
            s_r = jnp.where(ok, st_sc[g, :, r * tq:(r + 1) * tq], MASK_VALUE)
            m_new = jnp.maximum(m_old[:, r * tq:(r + 1) * tq], jnp.max(s_r, axis=0, keepdims=True))
            pts.append(jnp.exp2(s_r - m_new).astype(BF16))
            ms.append(m_new)
        m_new = jnp.concatenate(ms, axis=1)
        alpha = jnp.exp2(m_old - m_new)
        vt = vst_ref[c, g * HEAD_DIM:(g + 1) * HEAD_DIM, :]
        acc_sc[g] = alpha * acc_sc[g] + _dot(_with_ones(vt), jnp.concatenate(pts, axis=1))
        m_sc[g] = m_new

    n_steps = (n * tq + tq + tk - 1) // tk
    scores(0, 0)

    def body(c, carry):
        scores(c, 1)
        consume(c, 0)
        scores(c + 1, 0)
        consume(c, 1)
        return carry

    lax.fori_loop(0, n_steps - 1, body, 0)
    scores(n_steps - 1, 1)
    consume(n_steps - 1, 0)
    consume(n_steps - 1, 1)

    gate_t = (1.0 / (1.0 + jnp.exp(-gl_ref[...]))).T
    for p in range(N_PAIRS):
        sl = slice(p * LANES, (p + 1) * LANES)
        g = (2 * p) // rep
        rows = []
        for e in range(2):
            h = 2 * p + e
            ql = slice((h - g * rep) * tq, (h - g * rep + 1) * tq)
            slc, win = acc_sc[g][:, ql], owin_sc[g][:, ql]
            branches = (ocmp_sc[g][:, ql],
                        slc[:HEAD_DIM] / jnp.maximum(slc[HEAD_DIM:HEAD_DIM + 1], 1e-30),
                        win[:HEAD_DIM] / win[HEAD_DIM:HEAD_DIM + 1])
            out = None
            for j, br in enumerate(branches):
                t = gate_t[3 * h + j:3 * h + j + 1, :] * br
                out = t if out is None else out + t
            rows.append(out)
        pair = jnp.concatenate(rows, axis=0).T
        o_ref[:, sl] = (pair * _silu(gate_ref[:, sl].astype(F32))).astype(o_ref.dtype)


def _selection_overlap_t(n_cmp_pad, n_cmp, n_sel):
    cs = np.arange(n_cmp_pad) * CMP_STRIDE
    js = np.arange(LANES) * SEL_LEN
    ov = np.minimum(cs[None, :] + CMP_LEN, js[:, None] + SEL_LEN) - np.maximum(cs[None, :], js[:, None])
    ov = (np.clip(ov, 0, None) / CMP_LEN).astype(np.float32)
    ov[:, n_cmp:] = 0.0
    ov[n_sel:, :] = 0.0
    return ov


def _nsa_attention(q, k_cmp, v_cmp_t, ks, vs_t, kw, vw_t, g_logit, gate):
    b, s, _ = q.shape
    tq = NSA_TQ
    ncp = k_cmp.shape[1]
    n_cmp = s // CMP_STRIDE - 1
    n_sel = s // SEL_LEN
    n_win = -(-(NSA_WINDOW - 1) // BLK) + 1
    assert n_sel <= LANES and s % NSA_TK == 0 and tq == BLK
    ov_t = jnp.asarray(_selection_overlap_t(ncp, n_cmp, n_sel))
    row = lambda bi, i: (bi, i, 0)
    per_b = lambda bi, i: (bi, 0, 0)
    in_specs = [pl.BlockSpec((None, tq, MIX_WIDTH), row),
                pl.BlockSpec((None, ncp, LANES), per_b),
                pl.BlockSpec((None, LANES, ncp), per_b),
                pl.BlockSpec((None, s, LANES), per_b),
                pl.BlockSpec((None, s // NSA_TK, LANES, NSA_TK), lambda bi, i: (bi, 0, 0, 0))]
    args = [q, k_cmp, v_cmp_t, ks, vs_t]
    for j in range(n_win - 1, -1, -1):
        in_specs.append(pl.BlockSpec((None, BLK, LANES),
                                     lambda bi, i, j=j: (bi, jnp.maximum(i - j, 0), 0)))
        args.append(kw)
    for j in range(n_win - 1, -1, -1):
        in_specs.append(pl.BlockSpec((None, 1, LANES, BLK),
                                     lambda bi, i, j=j: (bi, jnp.maximum(i - j, 0), 0, 0)))
        args.append(vw_t)
    in_specs += [pl.BlockSpec((None, tq, LANES), row),
                 pl.BlockSpec((None, tq, MIX_WIDTH), row),
                 pl.BlockSpec((LANES, ncp), lambda bi, i: (0, 0))]
    args += [g_logit, gate, ov_t]
    wide = NSA_REP * tq
    return pl.pallas_call(
        functools.partial(_nsa_kernel, n_cmp, n_sel, n_win),
        grid=(b, s // tq),
        in_specs=in_specs,
        out_specs=pl.BlockSpec((None, tq, MIX_WIDTH), row),
        out_shape=jax.ShapeDtypeStruct((b, s, MIX_WIDTH), BF16),
        scratch_shapes=[pltpu.VMEM((NSA_KV, NSA_TK, wide), F32),
                        pltpu.VMEM((NSA_KV, 1, wide), F32),
                        pltpu.VMEM((NSA_KV, ACC_ROWS, wide), F32),
                        pltpu.VMEM((NSA_KV, LANES, tq), F32),
                        pltpu.VMEM((NSA_KV, HEAD_DIM, wide), F32),
                        pltpu.VMEM((NSA_KV, ACC_ROWS, wide), F32)],
        compiler_params=_cparams(("arbitrary", "arbitrary")),
        name="nsa_attn",
    )(*args)


def _split(w, sizes):
    offs = np.cumsum([0] + list(sizes))
    return [w[:, int(offs[i]):int(offs[i + 1])] for i in range(len(sizes))]


def _swa_layer(x, rope, gain, w_in, sinks, w_out):
    kvw = SWA_KV * HEAD_DIM
    parts = _split(w_in, [MIX_WIDTH, kvw, kvw, MIX_WIDTH])
    segs = [_Seg(MIX_WIDTH, rope=True, scale=Q_SCALE), _Seg(kvw, rope=True), _Seg(kvw),
            _Seg(MIX_WIDTH)]
    q, k, v, gate = _norm_proj(x, gain, parts, segs, rope)
    o = _banded_attention(q, k, v, SWA_WINDOW - 1, sinks=sinks, gate=gate)
    return o, w_out


def _dilated_layer(x, rope, gain, w_in, w_out):
    b, s, _ = x.shape
    kvw = DIL_KV * HEAD_DIM
    sizes, segs = [], []
    for window, dil in DIL_PATTERNS:
        assert s % (dil * BLK) == 0
        sizes += [MIX_WIDTH, kvw, kvw]
        segs += [_Seg(MIX_WIDTH, rope=True, scale=Q_SCALE, dil=dil), _Seg(kvw, rope=True, dil=dil),
                 _Seg(kvw, dil=dil)]
    sizes.append(MIX_WIDTH)
    segs.append(_Seg(MIX_WIDTH))
    res = _norm_proj(x, gain, _split(w_in, sizes), segs, rope)
    gate = res[-1]
    os_, lses, dils = [], [], []
    for gi, (window, dil) in enumerate(DIL_PATTERNS):
        q, k, v = res[3 * gi:3 * gi + 3]
        if dil > 1:
            q, k, v = (t.reshape(b * dil, s // dil, t.shape[-1]) for t in (q, k, v))
        o, lse = _banded_attention(q, k, v, window // dil, want_lse=True)
        if dil > 1:
            o = o.reshape(b, dil, s // dil, MIX_WIDTH)
            lse = lse.reshape(b, dil, s // dil, LANES)
        os_.append(o)
        lses.append(lse)
        dils.append(dil)
    return _dil_combine(os_, lses, gate, tuple(dils)), w_out


def _fox_layer(x, gain, w_in, b_f, w_out):
    sizes = [MIX_WIDTH, MIX_WIDTH, MIX_WIDTH, N_HEADS, MIX_WIDTH]
    segs = [_Seg(MIX_WIDTH, scale=Q_SCALE, tile=FOX_T), _Seg(MIX_WIDTH),
            _Seg(MIX_WIDTH, tile=FOX_T), _Seg(LANES, dtype=F32), _Seg(MIX_WIDTH)]
    qt, k, vt, f_logit, gate = _norm_proj(x, gain, _split(w_in, sizes), segs)
    kb = _fox_decay(f_logit, b_f)
    return _fox_attention(qt, k, kb, vt, gate), w_out


def _nsa_layer(x, positions, rope, gain, w_in, pe_k, w1_k, w2_k, pe_v, w1_v, w2_v, w_out):
    b, s, _ = x.shape
    kvw = NSA_KV * HEAD_DIM
    sizes = [MIX_WIDTH] + [kvw] * 6 + [3 * N_HEADS, MIX_WIDTH]
    segs = [_Seg(MIX_WIDTH, rope=True, scale=Q_SCALE), _Seg(kvw), _Seg(kvw), _Seg(kvw, rope=True),
            _Seg(kvw, tile=NSA_TK), _Seg(kvw, rope=True), _Seg(kvw, tile=BLK),
            _Seg(LANES, dtype=F32), _Seg(MIX_WIDTH)]
    q, kc, vc, ks, vs_t, kw, vw_t, g_logit, gate = _norm_proj(
        x, gain, _split(w_in, sizes), segs, rope)
    ns = s // CMP_STRIDE
    cmp_pos = jnp.concatenate(
        [positions[:, CMP_LEN - 1::CMP_STRIDE], positions[:, -1:]], axis=1)[:, :ns]
    cmp_c, cmp_s = _rope_tables(cmp_pos.reshape(-1))
    k_cmp, v_cmp_t = _nsa_compress(kc, vc, pe_k, w1_k, w2_k, pe_v, w1_v, w2_v, cmp_c, cmp_s)
    return _nsa_attention(q, k_cmp, v_cmp_t, ks, vs_t, kw, vw_t, g_logit, gate), w_out


def kernel(x, positions, norm_0, w_in_0, sinks_0, w_out_0, norm_1, w_in_1, w_out_1, norm_2, w_in_2, b_f_2, w_out_2, norm_3, w_in_3, cmp_pe_k_3, cmp_w1_k_3, cmp_w2_k_3, cmp_pe_v_3, cmp_w1_v_3, cmp_w2_v_3, w_out_3, final_norm):
    rope = _rope_tables(positions.reshape(-1))
    o, w = _swa_layer(x, rope, norm_0, w_in_0, sinks_0, w_out_0)
    x = _out_proj(o, w, x)
    o, w = _dilated_layer(x, rope, norm_1, w_in_1, w_out_1)
    x = _out_proj(o, w, x)
    o, w = _fox_layer(x, norm_2, w_in_2, b_f_2, w_out_2)
    x = _out_proj(o, w, x)
    o, w = _nsa_layer(x, positions, rope, norm_3, w_in_3, cmp_pe_k_3, cmp_w1_k_3, cmp_w2_k_3,
                      cmp_pe_v_3, cmp_w1_v_3, cmp_w2_v_3, w_out_3)
    return _out_proj(o, w, x, final_gain=final_norm)
```

```python
import functools
import math

import numpy as np
import jax
import jax.numpy as jnp
from jax import lax
from jax.experimental import pallas as pl
from jax.experimental.pallas import tpu as pltpu

HEAD_DIM = 64
N_HEADS = 16
N_PAIRS = N_HEADS // 2
MIX_WIDTH = N_HEADS * HEAD_DIM
ROT_DIM = HEAD_DIM // 4
ROT_HALF = ROT_DIM // 2
ROPE_THETA = 500000.0
BLK = 128
LANES = 128
NORM_EPS = 1e-6
MASK_VALUE = -1e30
PAD_SCORE = -3e38
LOG2E = math.log2(math.e)
Q_SCALE = HEAD_DIM ** -0.5 * LOG2E

SWA_KV = 4
SWA_WINDOW = 128
DIL_KV = 4
DIL_PATTERNS = ((128, 1), (512, 4), (2048, 16))
NSA_KV = 2
CMP_LEN = 32
CMP_STRIDE = 16
CMP_HIDDEN = 256
SEL_LEN = 64
SEL_TOPK = 8
NSA_WINDOW = 256
FORCED_SCORE = 1e4

VMEM_LIMIT_BYTES = 56 * 1024 * 1024
PROJ_ROWS = 512
PROJ_COLS = 512
ACC_ROWS = HEAD_DIM + 16
F32 = jnp.float32
BF16 = jnp.bfloat16


def _cparams(sem):
    return pltpu.CompilerParams(dimension_semantics=sem, vmem_limit_bytes=VMEM_LIMIT_BYTES)


def _lane_half(shape):
    return lax.broadcasted_iota(jnp.int32, shape, 1) // HEAD_DIM


def _swap_halves(t):
    return jnp.concatenate([t[:, HEAD_DIM:], t[:, :HEAD_DIM]], axis=1)


def _head_query(qp, e, kv_half):
    qh = jnp.where(_lane_half(qp.shape) == e, qp, jnp.zeros_like(qp))
    if e != kv_half:
        qh = _swap_halves(qh)
    return qh


def _group_queries(q_ref, g, rep):
    tiles = []
    for r in range(rep):
        h = g * rep + r
        tiles.append(_head_query(q_ref[:, (h // 2) * LANES:(h // 2 + 1) * LANES], h % 2, g % 2))
    return jnp.concatenate(tiles, axis=0)


def _dot_nt(a, b):
    return lax.dot_general(a, b, (((1,), (1,)), ((), ())), preferred_element_type=F32)


def _dot(a, b):
    return jnp.dot(a, b, preferred_element_type=F32)


def _silu(x):
    return x * (1.0 / (1.0 + jnp.exp(-x)))


def _with_ones(vt):
    return jnp.concatenate([vt, jnp.ones((ACC_ROWS - HEAD_DIM, vt.shape[1]), BF16)], axis=0)


def _rope_table_kernel(pos_ref, inv_ref, c_ref, s_ref):
    pos = pos_ref[...].astype(F32)
    ang = pos * inv_ref[...]
    d = lax.broadcasted_iota(jnp.int32, ang.shape, 1) % HEAD_DIM
    cos = jnp.cos(ang)
    sin = jnp.sin(ang)
    c_ref[...] = jnp.where(d < ROT_DIM, cos, 1.0)
    s_ref[...] = jnp.where(d < ROT_HALF, -sin, jnp.where(d < ROT_DIM, sin, 0.0))


def _rope_tables(pos_flat):
    t = pos_flat.shape[0]
    rows = min(t, 2048)
    assert t % rows == 0
    inv = jnp.power(ROPE_THETA, -jnp.arange(ROT_HALF, dtype=F32) / ROT_HALF)
    inv_l = jnp.tile(inv, LANES // ROT_HALF)[None, :]
    out = jax.ShapeDtypeStruct((t, LANES), F32)
    return pl.pallas_call(
        _rope_table_kernel,
        grid=(t // rows,),
        in_specs=[pl.BlockSpec((rows, 1), lambda i: (i, 0)),
                  pl.BlockSpec((1, LANES), lambda i: (0, 0))],
        out_specs=[pl.BlockSpec((rows, LANES), lambda i: (i, 0))] * 2,
        out_shape=[out, out],
        compiler_params=_cparams(("arbitrary",)),
        name="rope_tables",
    )(pos_flat[:, None], inv_l)


def _apply_rope(y, c, s):
    outs = []
    for j in range(y.shape[1] // LANES):
        t = y[:, j * LANES:(j + 1) * LANES]
        d = lax.broadcasted_iota(jnp.int32, t.shape, 1) % HEAD_DIM
        partner = jnp.where(d < ROT_HALF, pltpu.roll(t, LANES - ROT_HALF, 1),
                            pltpu.roll(t, ROT_HALF, 1))
        outs.append(t * c + partner * s)
    return outs[0] if len(outs) == 1 else jnp.concatenate(outs, axis=1)


class _Seg:
    def __init__(self, width, rope=False, scale=None, dtype=BF16, dil=1, tile=None):
        self.width, self.rope, self.scale, self.dtype, self.dil = width, rope, scale, dtype, dil
        self.tile = tile
        assert not (tile and (rope or dil > 1))


def _norm_proj_kernel(segs, use_rope, *refs):
    has_t = any(sg.tile for sg in segs)
    x_ref, g_ref, w_ref = refs[:3]
    k = 3
    if has_t:
        wt_ref = refs[k]
        k += 1
    if use_rope:
        c_ref, s_ref = refs[k:k + 2]
        k += 2
    out_refs = refs[k:k + len(segs)]
    stage_ref = refs[k + len(segs)] if any(sg.dil > 1 for sg in segs) else None

    x = x_ref[...]
    var = jnp.mean(x * x, axis=-1, keepdims=True)
    h = (x * lax.rsqrt(var + NORM_EPS) * g_ref[...]).astype(BF16)
    rows = x.shape[0]
    col = 0
    tcol = 0
    for sg, o_ref in zip(segs, out_refs):
        if sg.tile:
            for c0 in range(0, sg.width, PROJ_COLS):
                cw = min(PROJ_COLS, sg.width - c0)
                yt = _dot_nt(wt_ref[tcol + c0:tcol + c0 + cw, :], h)
                if sg.scale is not None:
                    yt = yt * sg.scale
                for ti in range(rows // sg.tile):
                    o_ref[ti, c0:c0 + cw, :] = yt[:, ti * sg.tile:(ti + 1) * sg.tile].astype(sg.dtype)
            tcol += sg.width
            continue
        for c0 in range(0, sg.width, PROJ_COLS):
            cw = min(PROJ_COLS, sg.width - c0)
            y = _dot(h, w_ref[:, col + c0:col + c0 + cw])
            if sg.rope:
                y = _apply_rope(y, c_ref[...], s_ref[...])
            if sg.scale is not None:
                y = y * sg.scale
            if sg.dil > 1:
                sub = rows // sg.dil
                for j in range(cw // LANES):
                    stage_ref[j] = y[:, j * LANES:(j + 1) * LANES]
                for r in range(sg.dil):
                    for j in range(cw // LANES):
                        lo = c0 + j * LANES
                        o_ref[r, :, lo:lo + LANES] = (
                            stage_ref[j, pl.ds(r, sub, stride=sg.dil), :].astype(sg.dtype))
            else:
                o_ref[:, c0:c0 + cw] = y.astype(sg.dtype)
        col += sg.width


def _norm_proj(x, gain, w_parts, segs, rope=None):
    b, s, d = x.shape
    tm = PROJ_ROWS
    assert s % tm == 0
    w_cols, wt_rows = [], []
    for wp, sg in zip(w_parts, segs):
        if wp.shape[1] < sg.width:
            wp = jnp.pad(wp, ((0, 0), (0, sg.width - wp.shape[1])))
        if sg.tile:
            wt_rows.append(wp.T)
        else:
            w_cols.append(wp)
    w = jnp.concatenate(w_cols, axis=1).astype(BF16)
    n = w.shape[1]
    use_rope = rope is not None
    in_specs = [pl.BlockSpec((None, tm, d), lambda bi, i: (bi, i, 0)),
                pl.BlockSpec((1, d), lambda bi, i: (0, 0)),
                pl.BlockSpec((d, n), lambda bi, i: (0, 0))]
    args = [x, gain[None, :], w]
    if wt_rows:
        wt = jnp.concatenate(wt_rows, axis=0).astype(BF16)
        in_specs.append(pl.BlockSpec(wt.shape, lambda bi, i: (0, 0)))
        args.append(wt)
    if use_rope:
        in_specs += [pl.BlockSpec((None, tm, LANES), lambda bi, i: (bi, i, 0))] * 2
        args += [rope[0].reshape(b, s, LANES), rope[1].reshape(b, s, LANES)]
    out_specs, out_shape = [], []
    for sg in segs:
        if sg.dil > 1:
            assert tm % sg.dil == 0
            out_shape.append(jax.ShapeDtypeStruct((b, sg.dil, s // sg.dil, sg.width), sg.dtype))
            out_specs.append(pl.BlockSpec((None, sg.dil, tm // sg.dil, sg.width),
                                          lambda bi, i: (bi, 0, i, 0)))
        elif sg.tile:
            assert tm % sg.tile == 0
            out_shape.append(jax.ShapeDtypeStruct((b, s // sg.tile, sg.width, sg.tile), sg.dtype))
            out_specs.append(pl.BlockSpec((None, tm // sg.tile, sg.width, sg.tile),
                                          lambda bi, i: (bi, i, 0, 0)))
        else:
            out_shape.append(jax.ShapeDtypeStruct((b, s, sg.width), sg.dtype))
            out_specs.append(pl.BlockSpec((None, tm, sg.width), lambda bi, i: (bi, i, 0)))
    scratch = ([pltpu.VMEM((PROJ_COLS // LANES, tm, LANES), F32)]
               if any(sg.dil > 1 for sg in segs) else [])
    return pl.pallas_call(
        functools.partial(_norm_proj_kernel, segs, use_rope),
        grid=(b, s // tm),
        in_specs=in_specs,
        out_specs=out_specs,
        out_shape=out_shape,
        scratch_shapes=scratch,
        compiler_params=_cparams(("arbitrary", "arbitrary")),
        name="norm_proj",
    )(*args)


def _out_proj_kernel(final, *refs):
    if final:
        o_ref, w_ref, x_ref, g_ref, y_ref = refs
    else:
        o_ref, w_ref, x_ref, y_ref = refs
    y = x_ref[...] + _dot(o_ref[...], w_ref[...])
    if final:
        var = jnp.mean(y * y, axis=-1, keepdims=True)
        y = y * lax.rsqrt(var + NORM_EPS) * g_ref[...]
    y_ref[...] = y


def _out_proj(o, w, x, final_gain=None):
    b, s, d = x.shape
    m = o.shape[-1]
    tm = PROJ_ROWS
    final = final_gain is not None
    row = lambda bi, i: (bi, i, 0)
    in_specs = [pl.BlockSpec((None, tm, m), row),
                pl.BlockSpec((m, d), lambda bi, i: (0, 0)),
                pl.BlockSpec((None, tm, d), row)]
    args = [o, w.astype(BF16), x]
    if final:
        in_specs.append(pl.BlockSpec((1, d), lambda bi, i: (0, 0)))
        args.append(final_gain[None, :])
    return pl.pallas_call(
        functools.partial(_out_proj_kernel, final),
        grid=(b, s // tm),
        in_specs=in_specs,
        out_specs=pl.BlockSpec((None, tm, d), row),
        out_shape=jax.ShapeDtypeStruct((b, s, d), F32),
        compiler_params=_cparams(("arbitrary", "arbitrary")),
        name="out_proj",
    )(*args)


def _banded_kernel(max_dist, n_prev, n_kv, has_sink, has_gate, want_lse, *refs):
    q_ref = refs[0]
    k_refs = refs[1:2 + n_prev]
    v_refs = refs[2 + n_prev:3 + 2 * n_prev]
    k = 3 + 2 * n_prev
    sink_ref = gate_ref = lse_ref = None
    if has_sink:
        sink_ref = refs[k]; k += 1
    if has_gate:
        gate_ref = refs[k]; k += 1
    o_ref = refs[k]; k += 1
    if want_lse:
        lse_ref = refs[k]; k += 1
    ot_sc = refs[k]
    st_sc = refs[k + 1]

    n = pl.program_id(1)
    tq = BLK
    kw = (n_prev + 1) * BLK
    rep = N_HEADS // n_kv
    krow = lax.broadcasted_iota(jnp.int32, (kw, tq), 0) - n_prev * BLK
    dist = lax.broadcasted_iota(jnp.int32, (kw, tq), 1) - krow
    ok = (dist >= 0) & (dist <= max_dist) & (n * BLK + krow >= 0)
    lse_rows = []

    def scores(g):
        sl = slice((g // 2) * LANES, (g // 2 + 1) * LANES)
        kcat = jnp.concatenate([kr[:, sl] for kr in k_refs], axis=0)
        st_sc[g] = _dot_nt(kcat, _group_queries(q_ref, g, rep))

    def consume(g):
        sl = slice((g // 2) * LANES, (g // 2 + 1) * LANES)
        vcat = jnp.concatenate([vr[:, sl] for vr in v_refs], axis=0)
        vt = vcat.astype(F32).T.astype(BF16)
        ms, pts = [], []
        for r in range(rep):
            s_r = jnp.where(ok, st_sc[g, :, r * tq:(r + 1) * tq], MASK_VALUE)
            m = jnp.max(s_r, axis=0, keepdims=True)
            if has_sink:
                h = g * rep + r
                m = jnp.maximum(m, sink_ref[0:1, h:h + 1])
            pts.append(jnp.exp2(s_r - m).astype(BF16))
            ms.append(m)
        acc = _dot(_with_ones(vt[(g % 2) * HEAD_DIM:(g % 2 + 1) * HEAD_DIM, :]),
                   jnp.concatenate(pts, axis=1))
        for r in range(rep):
            h = g * rep + r
            den = acc[HEAD_DIM:HEAD_DIM + 1, r * tq:(r + 1) * tq]
            if has_sink:
                den = den + jnp.exp2(sink_ref[0:1, h:h + 1] - ms[r])
            ot_sc[h] = acc[:HEAD_DIM, r * tq:(r + 1) * tq] / den
            if want_lse:
                lse_rows.append(ms[r] + jnp.log2(den))

    scores(0)
    scores(1)
    for g in range(n_kv):
        if g + 2 < n_kv:
            scores(g + 2)
        consume(g)
    for p in range(N_PAIRS):
        sl = slice(p * LANES, (p + 1) * LANES)
        pair = jnp.concatenate([ot_sc[2 * p], ot_sc[2 * p + 1]], axis=0).T
        if has_gate:
            pair = pair * _silu(gate_ref[:, sl].astype(F32))
        o_ref[:, sl] = pair.astype(o_ref.dtype)
    if want_lse:
        lse_t = jnp.concatenate(lse_rows + [jnp.zeros((LANES - N_HEADS, tq), F32)], axis=0)
        lse_ref[...] = lse_t.T


def _banded_attention(q, k, v, max_dist, sinks=None, gate=None, want_lse=False):
    bq, sq, _ = q.shape
    kvw = k.shape[-1]
    n_kv = kvw // HEAD_DIM
    n_prev = -(-max_dist // BLK)
    nb = sq // BLK
    row = lambda b, i: (b, i, 0)
    in_specs = [pl.BlockSpec((None, BLK, MIX_WIDTH), row)]
    args = [q]
    for arr in (k, v):
        for j in range(n_prev, -1, -1):
            in_specs.append(pl.BlockSpec((None, BLK, kvw),
                                         lambda b, i, j=j: (b, jnp.maximum(i - j, 0), 0)))
            args.append(arr)
    if sinks is not None:
        in_specs.append(pl.BlockSpec((1, LANES), lambda b, i: (0, 0)))
        args.append(jnp.pad(sinks.astype(F32) * LOG2E, (0, LANES - N_HEADS))[None, :])
    if gate is not None:
        in_specs.append(pl.BlockSpec((None, BLK, MIX_WIDTH), row))
        args.append(gate)
    out_specs = [pl.BlockSpec((None, BLK, MIX_WIDTH), row)]
    out_shape = [jax.ShapeDtypeStruct((bq, sq, MIX_WIDTH), BF16)]
    if want_lse:
        out_specs.append(pl.BlockSpec((None, BLK, LANES), row))
        out_shape.append(jax.ShapeDtypeStruct((bq, sq, LANES), F32))
    res = pl.pallas_call(
        functools.partial(_banded_kernel, max_dist, n_prev, n_kv, sinks is not None,
                          gate is not None, want_lse),
        grid=(bq, nb),
        in_specs=in_specs,
        out_specs=out_specs,
        out_shape=out_shape,
        scratch_shapes=[pltpu.VMEM((N_HEADS, HEAD_DIM, BLK), F32),
                        pltpu.VMEM((n_kv, (n_prev + 1) * BLK, (N_HEADS // n_kv) * BLK), F32)],
        compiler_params=_cparams(("arbitrary", "arbitrary")),
        name="banded_attn",
    )(*args)
    return res if want_lse else res[0]


def _dil_combine_kernel(dils, *refs):
    ng = len(dils)
    o_refs, l_refs = refs[:ng], refs[ng:2 * ng]
    gate_ref, out_ref, stage_ref, lstage_ref = refs[2 * ng:]
    rows = out_ref.shape[0]

    def natural(ref, dil, stage, slab):
        sl = slice(slab * LANES, (slab + 1) * LANES)
        if dil == 1:
            return ref[:, sl].astype(F32)
        sub = rows // dil
        for r in range(dil):
            stage[pl.ds(r, sub, stride=dil), :] = ref[r, :, sl].astype(F32)
        return stage[...]

    lses = [natural(l_refs[i], dils[i], lstage_ref, 0) for i in range(ng)]
    mx = functools.reduce(jnp.maximum, lses)
    ws = [jnp.exp2(l - mx) for l in lses]
    tot = functools.reduce(lambda a, c: a + c, ws)
    ws = [w / tot for w in ws]
    half = _lane_half((rows, LANES))
    outs = [None] * N_PAIRS
    for gi in range(ng):
        for p in range(N_PAIRS):
            og = natural(o_refs[gi], dils[gi], stage_ref, p)
            w0 = ws[gi][:, 2 * p:2 * p + 1]
            w1 = ws[gi][:, 2 * p + 1:2 * p + 2]
            wp = jnp.where(half == 0, w0, w1)
            t = wp * og
            outs[p] = t if outs[p] is None else outs[p] + t
    for p in range(N_PAIRS):
        sl = slice(p * LANES, (p + 1) * LANES)
        out_ref[:, sl] = (outs[p] * _silu(gate_ref[:, sl].astype(F32))).astype(out_ref.dtype)


def _dil_combine(os_, lses, gate, dils):
    b, s, _ = gate.shape
    tm = 256
    in_specs, args = [], []
    for arrs, width in ((os_, MIX_WIDTH), (lses, LANES)):
        for arr, dil in zip(arrs, dils):
            if dil == 1:
                in_specs.append(pl.BlockSpec((None, tm, width), lambda bi, i: (bi, i, 0)))
            else:
                in_specs.append(pl.BlockSpec((None, dil, tm // dil, width),
                                             lambda bi, i: (bi, 0, i, 0)))
            args.append(arr)
    in_specs.append(pl.BlockSpec((None, tm, MIX_WIDTH), lambda bi, i: (bi, i, 0)))
    args.append(gate)
    return pl.pallas_call(
        functools.partial(_dil_combine_kernel, dils),
        grid=(b, s // tm),
        in_specs=in_specs,
        out_specs=pl.BlockSpec((None, tm, MIX_WIDTH), lambda bi, i: (bi, i, 0)),
        out_shape=jax.ShapeDtypeStruct((b, s, MIX_WIDTH), BF16),
        scratch_shapes=[pltpu.VMEM((tm, LANES), F32), pltpu.VMEM((tm, LANES), F32)],
        compiler_params=_cparams(("arbitrary", "arbitrary")),
        name="dil_combine",
    )(*args)


N_BIAS_PIECES = 3


def _fox_decay_kernel(f_ref, b_ref, kb_ref):
    s = f_ref.shape[0]
    x = f_ref[...] + b_ref[...]
    logf = jnp.minimum(x, 0.0) - jnp.log1p(jnp.exp(-jnp.abs(x)))
    r = lax.broadcasted_iota(jnp.int32, (BLK, BLK), 0)
    c = lax.broadcasted_iota(jnp.int32, (BLK, BLK), 1)
    tri = (c <= r).astype(F32)
    lane = lax.broadcasted_iota(jnp.int32, (BLK, LANES), 1)
    carry = jnp.zeros((1, LANES), F32)
    for i in range(s // BLK):
        blk = logf[i * BLK:(i + 1) * BLK, :]
        cs = jnp.dot(tri, blk, preferred_element_type=F32, precision=lax.Precision.HIGHEST) + carry
        carry = cs[BLK - 1:BLK, :]
        rest = cs * (-LOG2E)
        out = jnp.zeros((BLK, LANES), F32)
        for j in range(N_BIAS_PIECES):
            piece = rest.astype(BF16).astype(F32)
            rest = rest - piece
            moved = piece if j == 0 else pltpu.roll(piece, N_HEADS * j, 1)
            out = jnp.where((lane >= N_HEADS * j) & (lane < N_HEADS * (j + 1)), moved, out)
        kb_ref[i * BLK:(i + 1) * BLK, :] = out.astype(BF16)


def _fox_decay(f_logit, b_f):
    b, s, _ = f_logit.shape
    b_pad = jnp.pad(b_f.astype(F32), (0, LANES - N_HEADS))[None, :]
    return pl.pallas_call(
        _fox_decay_kernel,
        grid=(b,),
        in_specs=[pl.BlockSpec((None, s, LANES), lambda bi: (bi, 0, 0)),
                  pl.BlockSpec((1, LANES), lambda bi: (0, 0))],
        out_specs=pl.BlockSpec((None, s, LANES), lambda bi: (bi, 0, 0)),
        out_shape=jax.ShapeDtypeStruct((b, s, LANES), BF16),
        compiler_params=_cparams(("arbitrary",)),
        name="fox_decay",
    )(f_logit, b_pad)


FOX_T = PROJ_ROWS


def _fox_kernel(qt_ref, k_ref, kb_ref, vt_ref, gate_ref, o_ref, m_sc, acc_sc, st_sc):
    pi = pl.program_id(1)
    i = pl.program_id(2)
    t = FOX_T
    qt = qt_ref[0]
    row = lax.broadcasted_iota(jnp.int32, (LANES, t), 0)
    zeros = jnp.zeros((HEAD_DIM, t), BF16)
    qx = []
    for e in range(2):
        h = 2 * pi + e
        pick = (row % N_HEADS == h) & (row < N_HEADS * N_BIAS_PIECES)
        sel = jnp.where(pick, 1.0, 0.0).astype(BF16)
        mine = qt[e * HEAD_DIM:(e + 1) * HEAD_DIM, :]
        top = [mine, zeros] if e == 0 else [zeros, mine]
        qx.append(jnp.concatenate(top + [sel], axis=0))
        m_sc[e] = jnp.full((1, t), MASK_VALUE, F32)
        acc_sc[e] = jnp.zeros((ACC_ROWS, t), F32)

    tk = t // 2

    def scores(jt, hf, e):
        k0 = pl.multiple_of(jt * t + hf * tk, tk)
        kx = jnp.concatenate([k_ref[pl.ds(k0, tk), :], kb_ref[pl.ds(k0, tk), :]], axis=1)
        st_sc[e, hf] = _dot(kx, qx[e])

    def consume(jt, hf, e, diag):
        st = st_sc[e, hf]
        if diag:
            ok = (lax.broadcasted_iota(jnp.int32, (tk, t), 0) + hf * tk
                  <= lax.broadcasted_iota(jnp.int32, (tk, t), 1))
            st = jnp.where(ok, st, MASK_VALUE)
        m_old = m_sc[e]
        m_new = jnp.maximum(m_old, jnp.max(st, axis=0, keepdims=True))
        pt = jnp.exp2(st - m_new).astype(BF16)
        alpha = jnp.exp2(m_old - m_new)
        vt = vt_ref[jt, e * HEAD_DIM:(e + 1) * HEAD_DIM, hf * tk:(hf + 1) * tk]
        acc_sc[e] = alpha * acc_sc[e] + _dot(_with_ones(vt), pt)
        m_sc[e] = m_new

    for e in range(2):
        scores(0, 0, e)

    def body(jt, carry):
        for e in range(2):
            scores(jt, 1, e)
            consume(jt, 0, e, False)
        for e in range(2):
            scores(jt + 1, 0, e)
            consume(jt, 1, e, False)
        return carry

    lax.fori_loop(0, i, body, 0)
    for e in range(2):
        scores(i, 1, e)
        consume(i, 0, e, True)
    for e in range(2):
        consume(i, 1, e, True)
    outs = []
    for e in range(2):
        acc = acc_sc[e]
        outs.append(acc[:HEAD_DIM] / jnp.maximum(acc[HEAD_DIM:HEAD_DIM + 1], 1e-30))
    out_t = jnp.concatenate(outs, axis=0)
    o_ref[...] = (out_t.T * _silu(gate_ref[...].astype(F32))).astype(o_ref.dtype)


def _fox_attention(qt, k, kb, vt, gate):
    b, s, _ = k.shape
    t = FOX_T
    assert s % t == 0 and vt.shape == (b, s // t, MIX_WIDTH, t) and qt.shape == vt.shape
    tile = lambda bi, p, i: (bi, i, p)
    return pl.pallas_call(
        _fox_kernel,
        grid=(b, N_PAIRS, s // t),
        in_specs=[pl.BlockSpec((None, 1, LANES, t), lambda bi, p, i: (bi, i, p, 0)),
                  pl.BlockSpec((None, s, LANES), lambda bi, p, i: (bi, 0, p)),
                  pl.BlockSpec((None, s, LANES), lambda bi, p, i: (bi, 0, 0)),
                  pl.BlockSpec((None, s // t, LANES, t), lambda bi, p, i: (bi, 0, p, 0)),
                  pl.BlockSpec((None, t, LANES), tile)],
        out_specs=pl.BlockSpec((None, t, LANES), tile),
        out_shape=jax.ShapeDtypeStruct((b, s, MIX_WIDTH), BF16),
        scratch_shapes=[pltpu.VMEM((2, 1, t), F32), pltpu.VMEM((2, ACC_ROWS, t), F32),
                        pltpu.VMEM((2, 2, t // 2, t), F32)],
        compiler_params=_cparams(("arbitrary", "arbitrary", "arbitrary")),
        name="fox_attn",
    )(qt, k, kb, vt, gate)


def _gelu_tanh(x):
    return 0.5 * x * (1.0 + jnp.tanh(math.sqrt(2.0 / math.pi) * (x + 0.044715 * (x * x * x))))


def _nsa_compress_kernel(ak_ref, av_ref, pe_ref, wa_ref, wb_ref, w2_ref, w2t_ref, c_ref, s_ref,
                         kc_ref, vct_ref):
    nrow = ak_ref.shape[0]
    for idx, a_ref in enumerate((ak_ref, av_ref)):
        a = a_ref[...].astype(F32)
        xa = (a + pe_ref[idx, 0:1, :]).astype(BF16)
        xb = (a + pe_ref[idx, 1:2, :]).astype(BF16)
        ya = _dot(xa, wa_ref[idx])
        yb = _dot(xb, wb_ref[idx])
        hid = _gelu_tanh(ya + pltpu.roll(yb, nrow - 1, 0)).astype(BF16)
        if idx == 0:
            y = _dot(hid, w2_ref[...])
            kc_ref[...] = _apply_rope(y, c_ref[...], s_ref[...]).astype(kc_ref.dtype)
        else:
            vct_ref[...] = _dot_nt(w2t_ref[...], hid).astype(vct_ref.dtype)


def _nsa_compress(kc, vc, pe_k, w1_k, w2_k, pe_v, w1_v, w2_v, rope_c, rope_s):
    b, s, _ = kc.shape
    ns = s // CMP_STRIDE
    g = NSA_KV
    flat = CMP_STRIDE * g * HEAD_DIM

    def w1_halves(w1):
        w1r = w1.reshape(2, CMP_STRIDE, HEAD_DIM, CMP_HIDDEN)
        outs = []
        for hf in range(2):
            z = jnp.zeros((CMP_STRIDE, g, HEAD_DIM, g, CMP_HIDDEN), F32)
            for gi in range(g):
                z = z.at[:, gi, :, gi, :].set(w1r[hf])
            outs.append(z.reshape(flat, g * CMP_HIDDEN))
        return outs

    def pe_halves(pe):
        per = pe.reshape(2, CMP_STRIDE, 1, HEAD_DIM)
        return jnp.broadcast_to(per, (2, CMP_STRIDE, g, HEAD_DIM)).reshape(2, flat)

    def w2_bd(w2):
        z = jnp.zeros((g, CMP_HIDDEN, g, HEAD_DIM), F32)
        for gi in range(g):
            z = z.at[gi, :, gi, :].set(w2)
        return z.reshape(g * CMP_HIDDEN, g * HEAD_DIM)

    ka, kb = w1_halves(w1_k)
    va, vb = w1_halves(w1_v)
    wa = jnp.stack([ka, va]).astype(BF16)
    wb = jnp.stack([kb, vb]).astype(BF16)
    w2 = w2_bd(w2_k).astype(BF16)
    w2t = w2_bd(w2_v).T.astype(BF16)
    pe = jnp.stack([pe_halves(pe_k), pe_halves(pe_v)]).astype(F32)
    whole = lambda bi: (0, 0, 0)
    per_b = lambda bi: (bi, 0, 0)
    return pl.pallas_call(
        _nsa_compress_kernel,
        grid=(b,),
        in_specs=[pl.BlockSpec((None, ns, flat), per_b),
                  pl.BlockSpec((None, ns, flat), per_b),
                  pl.BlockSpec((2, 2, flat), whole),
                  pl.BlockSpec((2, flat, g * CMP_HIDDEN), whole),
                  pl.BlockSpec((2, flat, g * CMP_HIDDEN), whole),
                  pl.BlockSpec((g * CMP_HIDDEN, LANES), lambda bi: (0, 0)),
                  pl.BlockSpec((LANES, g * CMP_HIDDEN), lambda bi: (0, 0)),
                  pl.BlockSpec((None, ns, LANES), per_b),
                  pl.BlockSpec((None, ns, LANES), per_b)],
        out_specs=[pl.BlockSpec((None, ns, LANES), per_b),
                   pl.BlockSpec((None, LANES, ns), per_b)],
        out_shape=[jax.ShapeDtypeStruct((b, ns, LANES), BF16),
                   jax.ShapeDtypeStruct((b, LANES, ns), BF16)],
        compiler_params=_cparams(("arbitrary",)),
        name="nsa_compress",
    )(kc.reshape(b, ns, flat), vc.reshape(b, ns, flat), pe, wa, wb, w2, w2t,
      rope_c.reshape(b, ns, LANES), rope_s.reshape(b, ns, LANES))


NSA_TQ = 128
NSA_TK = 256
NSA_REP = N_HEADS // NSA_KV


def _nsa_kernel(n_cmp, n_sel, n_win, *refs):
    q_ref, kc_ref, vct_ref, ks_ref, vst_ref = refs[:5]
    kw_refs = refs[5:5 + n_win]
    vwt_refs = refs[5 + n_win:5 + 2 * n_win]
    gl_ref, gate_ref, ovt_ref, blk_ref, o_ref = refs[5 + 2 * n_win:10 + 2 * n_win]
    st_sc, stc_sc, stw_sc, m_sc, acc_sc, ocmp_sc, owin_sc = refs[10 + 2 * n_win:]

    n = pl.program_id(1)
    tq, tk, rep = NSA_TQ, NSA_TK, NSA_REP
    ncp = kc_ref.shape[0]
    t_lane = n * tq + lax.broadcasted_iota(jnp.int32, (1, tq), 1)
    qg = [_group_queries(q_ref, g, rep) for g in range(NSA_KV)]

    ci = lax.broadcasted_iota(jnp.int32, (ncp, tq), 0)
    cmask = (ci * CMP_STRIDE + (CMP_LEN - 1) <= t_lane) & (ci < n_cmp)
    kwn = n_win * BLK
    krow = lax.broadcasted_iota(jnp.int32, (kwn, tq), 0) - (n_win - 1) * BLK
    dist = lax.broadcasted_iota(jnp.int32, (kwn, tq), 1) - krow
    wok = (dist >= 0) & (dist <= NSA_WINDOW - 1) & (n * BLK + krow >= 0)
    rowi = lax.broadcasted_iota(jnp.int32, (LANES, tq), 0)
    rowf = rowi.astype(F32)
    cur = t_lane // SEL_LEN
    forced = (rowi == 0) | (rowi == cur)
    causal = rowi <= cur

    def cmp_scores(g):
        stc_sc[g] = _dot_nt(kc_ref[...], qg[g])

    def cmp_consume(g):
        ps, tot = [], None
        for r in range(rep):
            s_r = jnp.where(cmask, stc_sc[g, :, r * tq:(r + 1) * tq], MASK_VALUE)
            m = jnp.max(s_r, axis=0, keepdims=True)
            pr = jnp.where(cmask, jnp.exp2(s_r - m), 0.0)
            pr = pr * (1.0 / jnp.maximum(jnp.sum(pr, axis=0, keepdims=True), 1e-30))
            tot = pr if tot is None else tot + pr
            ps.append(pr.astype(BF16))
        ocmp_sc[g] = _dot(vct_ref[g * HEAD_DIM:(g + 1) * HEAD_DIM, :], jnp.concatenate(ps, axis=1))
        return tot

    def select(g, psum):
        imp = jnp.dot(ovt_ref[...], psum, preferred_element_type=F32,
                      precision=lax.Precision.HIGHEST)
        score = jnp.where(causal, jnp.where(forced, FORCED_SCORE, imp), MASK_VALUE)
        score = jnp.where(rowi < n_sel, score, PAD_SCORE)
        chosen = jnp.zeros((LANES, tq), F32)
        for _ in range(min(SEL_TOPK, n_sel)):
            mx = jnp.max(score, axis=0, keepdims=True)
            first = jnp.min(jnp.where(score == mx, rowf, float(LANES)), axis=0, keepdims=True)
            hit = rowf == first
            chosen = jnp.where(hit, 1.0, chosen)
            score = jnp.where(hit, PAD_SCORE, score)
        bias = jnp.where(causal & (chosen > 0.5), 0.0, MASK_VALUE).T.astype(BF16)
        return jnp.concatenate([qg[g], jnp.concatenate([bias] * rep, axis=0)], axis=1)

    def win_scores(g):
        kwcat = jnp.concatenate([kr[...] for kr in kw_refs], axis=0)
        stw_sc[g] = _dot_nt(kwcat, qg[g])

    def win_consume(g):
        vwt = jnp.concatenate([vr[0] for vr in vwt_refs], axis=1)
        pts = []
        for r in range(rep):
            s_r = jnp.where(wok, stw_sc[g, :, r * tq:(r + 1) * tq], MASK_VALUE)
            m = jnp.max(s_r, axis=0, keepdims=True)
            pts.append(jnp.exp2(s_r - m).astype(BF16))
        owin_sc[g] = _dot(_with_ones(vwt[g * HEAD_DIM:(g + 1) * HEAD_DIM, :]),
                          jnp.concatenate(pts, axis=1))

    cmp_scores(0)
    cmp_scores(1)
    psum0 = cmp_consume(0)
    win_scores(0)
    psum1 = cmp_consume(1)
    win_scores(1)
    qsel = [select(0, psum0)]
    win_consume(0)
    qsel.append(select(1, psum1))
    win_consume(1)

    for g in range(NSA_KV):
        m_sc[g] = jnp.full((1, rep * tq), MASK_VALUE, F32)
        acc_sc[g] = jnp.zeros((ACC_ROWS, rep * tq), F32)

    def scores(c, g):
        k0 = pl.multiple_of(c * tk, tk)
        kx = jnp.concatenate([ks_ref[pl.ds(k0, tk), :], blk_ref[pl.ds(k0, tk), :]], axis=1)
        st_sc[g] = _dot_nt(kx, qsel[g])

    def consume(c, g, last):
        if last:
            tok_ok = c * tk + lax.broadcasted_iota(jnp.int32, (tk, tq), 0) <= t_lane
        m_old = m_sc[g]
        ms, pts = [], []
        for r in range(rep):
            s_r = st_sc[g, :, r * tq:(r + 1) * tq]
            if last:
                s_r = jnp.where(tok_ok, s_r, MASK_VALUE)
            m_new = jnp.maximum(m_old[:, r * tq:(r + 1) * tq], jnp.max(s_r, axis=0, keepdims=True))
            pts.append(jnp.exp2(s_r - m_new).astype(BF16))
            ms.append(m_new)
        m_new = jnp.concatenate(ms, axis=1)
        alpha = jnp.exp2(m_old - m_new)
        vt = vst_ref[c, g * HEAD_DIM:(g + 1) * HEAD_DIM, :]
        acc_sc[g] = alpha * acc_sc[g] + _dot(_with_ones(vt), jnp.concatenate(pts, axis=1))
        m_sc[g] = m_new

    n_steps = (n * tq + tq + tk - 1) // tk
    scores(0, 0)

    def body(c, carry):
        scores(c, 1)
        consume(c, 0, False)
        scores(c + 1, 0)
        consume(c, 1, False)
        return carry

    lax.fori_loop(0, n_steps - 1, body, 0)
    scores(n_steps - 1, 1)
    consume(n_steps - 1, 0, True)
    consume(n_steps - 1, 1, True)

    gate_t = (1.0 / (1.0 + jnp.exp(-gl_ref[...]))).T
    for p in range(N_PAIRS):
        sl = slice(p * LANES, (p + 1) * LANES)
        g = (2 * p) // rep
        rows = []
        for e in range(2):
            h = 2 * p + e
            ql = slice((h - g * rep) * tq, (h - g * rep + 1) * tq)
            slc, win = acc_sc[g][:, ql], owin_sc[g][:, ql]
            branches = (ocmp_sc[g][:, ql],
                        slc[:HEAD_DIM] / jnp.maximum(slc[HEAD_DIM:HEAD_DIM + 1], 1e-30),
                        win[:HEAD_DIM] / win[HEAD_DIM:HEAD_DIM + 1])
            out = None
            for j, br in enumerate(branches):
                t = gate_t[3 * h + j:3 * h + j + 1, :] * br
                out = t if out is None else out + t
            rows.append(out)
        pair = jnp.concatenate(rows, axis=0).T
        o_ref[:, sl] = (pair * _silu(gate_ref[:, sl].astype(F32))).astype(o_ref.dtype)


def _selection_overlap_t(n_cmp_pad, n_cmp, n_sel):
    cs = np.arange(n_cmp_pad) * CMP_STRIDE
    js = np.arange(LANES) * SEL_LEN
    ov = np.minimum(cs[None, :] + CMP_LEN, js[:, None] + SEL_LEN) - np.maximum(cs[None, :], js[:, None])
    ov = (np.clip(ov, 0, None) / CMP_LEN).astype(np.float32)
    ov[:, n_cmp:] = 0.0
    ov[n_sel:, :] = 0.0
    return ov


def _nsa_attention(q, k_cmp, v_cmp_t, ks, vs_t, kw, vw_t, g_logit, gate):
    b, s, _ = q.shape
    tq = NSA_TQ
    ncp = k_cmp.shape[1]
    n_cmp = s // CMP_STRIDE - 1
    n_sel = s // SEL_LEN
    n_win = -(-(NSA_WINDOW - 1) // BLK) + 1
    assert n_sel <= LANES and s % NSA_TK == 0 and tq == BLK
    ov_t = jnp.asarray(_selection_overlap_t(ncp, n_cmp, n_sel))
    row = lambda bi, i: (bi, i, 0)
    per_b = lambda bi, i: (bi, 0, 0)
    in_specs = [pl.BlockSpec((None, tq, MIX_WIDTH), row),
                pl.BlockSpec((None, ncp, LANES), per_b),
                pl.BlockSpec((None, LANES, ncp), per_b),
                pl.BlockSpec((None, s, LANES), per_b),
                pl.BlockSpec((None, s // NSA_TK, LANES, NSA_TK), lambda bi, i: (bi, 0, 0, 0))]
    args = [q, k_cmp, v_cmp_t, ks, vs_t]
    for j in range(n_win - 1, -1, -1):
        in_specs.append(pl.BlockSpec((None, BLK, LANES),
                                     lambda bi, i, j=j: (bi, jnp.maximum(i - j, 0), 0)))
        args.append(kw)
    for j in range(n_win - 1, -1, -1):
        in_specs.append(pl.BlockSpec((None, 1, LANES, BLK),
                                     lambda bi, i, j=j: (bi, jnp.maximum(i - j, 0), 0, 0)))
        args.append(vw_t)
    block_onehot = jnp.asarray(
        (np.arange(s)[:, None] // SEL_LEN == np.arange(LANES)[None, :]).astype(np.float32), BF16)
    in_specs += [pl.BlockSpec((None, tq, LANES), row),
                 pl.BlockSpec((None, tq, MIX_WIDTH), row),
                 pl.BlockSpec((LANES, ncp), lambda bi, i: (0, 0)),
                 pl.BlockSpec((s, LANES), lambda bi, i: (0, 0))]
    args += [g_logit, gate, ov_t, block_onehot]
    wide = NSA_REP * tq
    return pl.pallas_call(
        functools.partial(_nsa_kernel, n_cmp, n_sel, n_win),
        grid=(b, s // tq),
        in_specs=in_specs,
        out_specs=pl.BlockSpec((None, tq, MIX_WIDTH), row),
        out_shape=jax.ShapeDtypeStruct((b, s, MIX_WIDTH), BF16),
        scratch_shapes=[pltpu.VMEM((NSA_KV, NSA_TK, wide), F32),
                        pltpu.VMEM((NSA_KV, ncp, wide), F32),
                        pltpu.VMEM((NSA_KV, n_win * BLK, wide), F32),
                        pltpu.VMEM((NSA_KV, 1, wide), F32),
                        pltpu.VMEM((NSA_KV, ACC_ROWS, wide), F32),
                        pltpu.VMEM((NSA_KV, HEAD_DIM, wide), F32),
                        pltpu.VMEM((NSA_KV, ACC_ROWS, wide), F32)],
        compiler_params=_cparams(("arbitrary", "arbitrary")),
        name="nsa_attn",
    )(*args)


def _split(w, sizes):
    offs = np.cumsum([0] + list(sizes))
    return [w[:, int(offs[i]):int(offs[i + 1])] for i in range(len(sizes))]


def _swa_layer(x, rope, gain, w_in, sinks, w_out):
    kvw = SWA_KV * HEAD_DIM
    parts = _split(w_in, [MIX_WIDTH, kvw, kvw, MIX_WIDTH])
    segs = [_Seg(MIX_WIDTH, rope=True, scale=Q_SCALE), _Seg(kvw, rope=True), _Seg(kvw),
            _Seg(MIX_WIDTH)]
    q, k, v, gate = _norm_proj(x, gain, parts, segs, rope)
    o = _banded_attention(q, k, v, SWA_WINDOW - 1, sinks=sinks, gate=gate)
    return o, w_out


def _dilated_layer(x, rope, gain, w_in, w_out):
    b, s, _ = x.shape
    kvw = DIL_KV * HEAD_DIM
    sizes, segs = [], []
    for window, dil in DIL_PATTERNS:
        assert s % (dil * BLK) == 0
        sizes += [MIX_WIDTH, kvw, kvw]
        segs += [_Seg(MIX_WIDTH, rope=True, scale=Q_SCALE, dil=dil), _Seg(kvw, rope=True, dil=dil),
                 _Seg(kvw, dil=dil)]
    sizes.append(MIX_WIDTH)
    segs.append(_Seg(MIX_WIDTH))
    res = _norm_proj(x, gain, _split(w_in, sizes), segs, rope)
    gate = res[-1]
    os_, lses, dils = [], [], []
    for gi, (window, dil) in enumerate(DIL_PATTERNS):
        q, k, v = res[3 * gi:3 * gi + 3]
        if dil > 1:
            q, k, v = (t.reshape(b * dil, s // dil, t.shape[-1]) for t in (q, k, v))
        o, lse = _banded_attention(q, k, v, window // dil, want_lse=True)
        if dil > 1:
            o = o.reshape(b, dil, s // dil, MIX_WIDTH)
            lse = lse.reshape(b, dil, s // dil, LANES)
        os_.append(o)
        lses.append(lse)
        dils.append(dil)
    return _dil_combine(os_, lses, gate, tuple(dils)), w_out


def _fox_layer(x, gain, w_in, b_f, w_out):
    sizes = [MIX_WIDTH, MIX_WIDTH, MIX_WIDTH, N_HEADS, MIX_WIDTH]
    segs = [_Seg(MIX_WIDTH, scale=Q_SCALE, tile=FOX_T), _Seg(MIX_WIDTH),
            _Seg(MIX_WIDTH, tile=FOX_T), _Seg(LANES, dtype=F32), _Seg(MIX_WIDTH)]
    qt, k, vt, f_logit, gate = _norm_proj(x, gain, _split(w_in, sizes), segs)
    kb = _fox_decay(f_logit, b_f)
    return _fox_attention(qt, k, kb, vt, gate), w_out


def _nsa_layer(x, positions, rope, gain, w_in, pe_k, w1_k, w2_k, pe_v, w1_v, w2_v, w_out):
    b, s, _ = x.shape
    kvw = NSA_KV * HEAD_DIM
    sizes = [MIX_WIDTH] + [kvw] * 6 + [3 * N_HEADS, MIX_WIDTH]
    segs = [_Seg(MIX_WIDTH, rope=True, scale=Q_SCALE), _Seg(kvw), _Seg(kvw), _Seg(kvw, rope=True),
            _Seg(kvw, tile=NSA_TK), _Seg(kvw, rope=True), _Seg(kvw, tile=BLK),
            _Seg(LANES, dtype=F32), _Seg(MIX_WIDTH)]
    q, kc, vc, ks, vs_t, kw, vw_t, g_logit, gate = _norm_proj(
        x, gain, _split(w_in, sizes), segs, rope)
    ns = s // CMP_STRIDE
    cmp_pos = jnp.concatenate(
        [positions[:, CMP_LEN - 1::CMP_STRIDE], positions[:, -1:]], axis=1)[:, :ns]
    cmp_c, cmp_s = _rope_tables(cmp_pos.reshape(-1))
    k_cmp, v_cmp_t = _nsa_compress(kc, vc, pe_k, w1_k, w2_k, pe_v, w1_v, w2_v, cmp_c, cmp_s)
    return _nsa_attention(q, k_cmp, v_cmp_t, ks, vs_t, kw, vw_t, g_logit, gate), w_out


def kernel(x, positions, norm_0, w_in_0, sinks_0, w_out_0, norm_1, w_in_1, w_out_1, norm_2, w_in_2, b_f_2, w_out_2, norm_3, w_in_3, cmp_pe_k_3, cmp_w1_k_3, cmp_w2_k_3, cmp_pe_v_3, cmp_w1_v_3, cmp_w2_v_3, w_out_3, final_norm):
    rope = _rope_tables(positions.reshape(-1))
    o, w = _swa_layer(x, rope, norm_0, w_in_0, sinks_0, w_out_0)
    x = _out_proj(o, w, x)
    o, w = _dilated_layer(x, rope, norm_1, w_in_1, w_out_1)
    x = _out_proj(o, w, x)
    o, w = _fox_layer(x, norm_2, w_in_2, b_f_2, w_out_2)
    x = _out_proj(o, w, x)
    o, w = _nsa_layer(x, positions, rope, norm_3, w_in_3, cmp_pe_k_3, cmp_w1_k_3, cmp_w2_k_3,
                      cmp_pe_v_3, cmp_w1_v_3, cmp_w2_v_3, w_out_3)
    return _out_proj(o, w, x, final_gain=final_norm)
```

```python
import functools
import math

import numpy as np
import jax
import jax.numpy as jnp
from jax import lax
from jax.experimental import pallas as pl
from jax.experimental.pallas import tpu as pltpu

HEAD_DIM = 64
N_HEADS = 16
N_PAIRS = N_HEADS // 2
MIX_WIDTH = N_HEADS * HEAD_DIM
ROT_DIM = HEAD_DIM // 4
ROT_HALF = ROT_DIM // 2
ROPE_THETA = 500000.0
BLK = 128
LANES = 128
NORM_EPS = 1e-6
MASK_VALUE = -1e30
PAD_SCORE = -3e38
LOG2E = math.log2(math.e)
Q_SCALE = HEAD_DIM ** -0.5 * LOG2E

SWA_KV = 4
SWA_WINDOW = 128
DIL_KV = 4
DIL_PATTERNS = ((128, 1), (512, 4), (2048, 16))
NSA_KV = 2
CMP_LEN = 32
CMP_STRIDE = 16
CMP_HIDDEN = 256
SEL_LEN = 64
SEL_TOPK = 8
NSA_WINDOW = 256
FORCED_SCORE = 1e4

VMEM_LIMIT_BYTES = 56 * 1024 * 1024
PROJ_ROWS = 512
PROJ_COLS = 512
ACC_ROWS = HEAD_DIM + 16
F32 = jnp.float32
BF16 = jnp.bfloat16


def _cparams(sem):
    return pltpu.CompilerParams(dimension_semantics=sem, vmem_limit_bytes=VMEM_LIMIT_BYTES)


def _lane_half(shape):
    return lax.broadcasted_iota(jnp.int32, shape, 1) // HEAD_DIM


def _swap_halves(t):
    return jnp.concatenate([t[:, HEAD_DIM:], t[:, :HEAD_DIM]], axis=1)


def _head_query(qp, e, kv_half):
    qh = jnp.where(_lane_half(qp.shape) == e, qp, jnp.zeros_like(qp))
    if e != kv_half:
        qh = _swap_halves(qh)
    return qh


def _group_queries(q_ref, g, rep):
    tiles = []
    for r in range(rep):
        h = g * rep + r
        tiles.append(_head_query(q_ref[:, (h // 2) * LANES:(h // 2 + 1) * LANES], h % 2, g % 2))
    return jnp.concatenate(tiles, axis=0)


def _dot_nt(a, b):
    return lax.dot_general(a, b, (((1,), (1,)), ((), ())), preferred_element_type=F32)


def _dot(a, b):
    return jnp.dot(a, b, preferred_element_type=F32)


def _silu(x):
    return x * (1.0 / (1.0 + jnp.exp(-x)))


def _loop_in_pairs(count, body):
    odd = count % 2

    @pl.when(odd == 1)
    def _():
        body(0)

    def two(jj, carry):
        body(odd + 2 * jj)
        body(odd + 2 * jj + 1)
        return carry

    lax.fori_loop(0, count // 2, two, 0)


def _with_ones(vt):
    return jnp.concatenate([vt, jnp.ones((ACC_ROWS - HEAD_DIM, vt.shape[1]), BF16)], axis=0)


def _rope_table_kernel(pos_ref, inv_ref, c_ref, s_ref):
    pos = pos_ref[...].astype(F32)
    ang = pos * inv_ref[...]
    d = lax.broadcasted_iota(jnp.int32, ang.shape, 1) % HEAD_DIM
    cos = jnp.cos(ang)
    sin = jnp.sin(ang)
    c_ref[...] = jnp.where(d < ROT_DIM, cos, 1.0)
    s_ref[...] = jnp.where(d < ROT_HALF, -sin, jnp.where(d < ROT_DIM, sin, 0.0))


def _rope_tables(pos_flat):
    t = pos_flat.shape[0]
    rows = min(t, 2048)
    assert t % rows == 0
    inv = jnp.power(ROPE_THETA, -jnp.arange(ROT_HALF, dtype=F32) / ROT_HALF)
    inv_l = jnp.tile(inv, LANES // ROT_HALF)[None, :]
    out = jax.ShapeDtypeStruct((t, LANES), F32)
    return pl.pallas_call(
        _rope_table_kernel,
        grid=(t // rows,),
        in_specs=[pl.BlockSpec((rows, 1), lambda i: (i, 0)),
                  pl.BlockSpec((1, LANES), lambda i: (0, 0))],
        out_specs=[pl.BlockSpec((rows, LANES), lambda i: (i, 0))] * 2,
        out_shape=[out, out],
        compiler_params=_cparams(("arbitrary",)),
        name="rope_tables",
    )(pos_flat[:, None], inv_l)


def _apply_rope(y, c, s):
    outs = []
    for j in range(y.shape[1] // LANES):
        t = y[:, j * LANES:(j + 1) * LANES]
        d = lax.broadcasted_iota(jnp.int32, t.shape, 1) % HEAD_DIM
        partner = jnp.where(d < ROT_HALF, pltpu.roll(t, LANES - ROT_HALF, 1),
                            pltpu.roll(t, ROT_HALF, 1))
        outs.append(t * c + partner * s)
    return outs[0] if len(outs) == 1 else jnp.concatenate(outs, axis=1)


class _Seg:
    def __init__(self, width, rope=False, scale=None, dtype=BF16, dil=1, tile=None):
        self.width, self.rope, self.scale, self.dtype, self.dil = width, rope, scale, dtype, dil
        self.tile = tile
        assert not (tile and (rope or dil > 1))


def _norm_proj_kernel(segs, use_rope, *refs):
    has_t = any(sg.tile for sg in segs)
    x_ref, g_ref, w_ref = refs[:3]
    k = 3
    if has_t:
        wt_ref = refs[k]
        k += 1
    if use_rope:
        c_ref, s_ref = refs[k:k + 2]
        k += 2
    out_refs = refs[k:k + len(segs)]
    stage_ref = refs[k + len(segs)] if any(sg.dil > 1 for sg in segs) else None

    x = x_ref[...]
    var = jnp.mean(x * x, axis=-1, keepdims=True)
    h = (x * lax.rsqrt(var + NORM_EPS) * g_ref[...]).astype(BF16)
    rows = x.shape[0]
    col = 0
    tcol = 0
    for sg, o_ref in zip(segs, out_refs):
        if sg.tile:
            for c0 in range(0, sg.width, PROJ_COLS):
                cw = min(PROJ_COLS, sg.width - c0)
                yt = _dot_nt(wt_ref[tcol + c0:tcol + c0 + cw, :], h)
                if sg.scale is not None:
                    yt = yt * sg.scale
                for ti in range(rows // sg.tile):
                    o_ref[ti, c0:c0 + cw, :] = yt[:, ti * sg.tile:(ti + 1) * sg.tile].astype(sg.dtype)
            tcol += sg.width
            continue
        for c0 in range(0, sg.width, PROJ_COLS):
            cw = min(PROJ_COLS, sg.width - c0)
            y = _dot(h, w_ref[:, col + c0:col + c0 + cw])
            if sg.rope:
                y = _apply_rope(y, c_ref[...], s_ref[...])
            if sg.scale is not None:
                y = y * sg.scale
            if sg.dil > 1:
                sub = rows // sg.dil
                for j in range(cw // LANES):
                    stage_ref[j] = y[:, j * LANES:(j + 1) * LANES]
                for r in range(sg.dil):
                    for j in range(cw // LANES):
                        lo = c0 + j * LANES
                        o_ref[r, :, lo:lo + LANES] = (
                            stage_ref[j, pl.ds(r, sub, stride=sg.dil), :].astype(sg.dtype))
            else:
                o_ref[:, c0:c0 + cw] = y.astype(sg.dtype)
        col += sg.width


def _norm_proj(x, gain, w_parts, segs, rope=None):
    b, s, d = x.shape
    tm = PROJ_ROWS
    assert s % tm == 0
    w_cols, wt_rows = [], []
    for wp, sg in zip(w_parts, segs):
        if wp.shape[1] < sg.width:
            wp = jnp.pad(wp, ((0, 0), (0, sg.width - wp.shape[1])))
        if sg.tile:
            wt_rows.append(wp.T)
        else:
            w_cols.append(wp)
    w = jnp.concatenate(w_cols, axis=1).astype(BF16)
    n = w.shape[1]
    use_rope = rope is not None
    in_specs = [pl.BlockSpec((None, tm, d), lambda bi, i: (bi, i, 0)),
                pl.BlockSpec((1, d), lambda bi, i: (0, 0)),
                pl.BlockSpec((d, n), lambda bi, i: (0, 0))]
    args = [x, gain[None, :], w]
    if wt_rows:
        wt = jnp.concatenate(wt_rows, axis=0).astype(BF16)
        in_specs.append(pl.BlockSpec(wt.shape, lambda bi, i: (0, 0)))
        args.append(wt)
    if use_rope:
        in_specs += [pl.BlockSpec((None, tm, LANES), lambda bi, i: (bi, i, 0))] * 2
        args += [rope[0].reshape(b, s, LANES), rope[1].reshape(b, s, LANES)]
    out_specs, out_shape = [], []
    for sg in segs:
        if sg.dil > 1:
            assert tm % sg.dil == 0
            out_shape.append(jax.ShapeDtypeStruct((b, sg.dil, s // sg.dil, sg.width), sg.dtype))
            out_specs.append(pl.BlockSpec((None, sg.dil, tm // sg.dil, sg.width),
                                          lambda bi, i: (bi, 0, i, 0)))
        elif sg.tile:
            assert tm % sg.tile == 0
            out_shape.append(jax.ShapeDtypeStruct((b, s // sg.tile, sg.width, sg.tile), sg.dtype))
            out_specs.append(pl.BlockSpec((None, tm // sg.tile, sg.width, sg.tile),
                                          lambda bi, i: (bi, i, 0, 0)))
        else:
            out_shape.append(jax.ShapeDtypeStruct((b, s, sg.width), sg.dtype))
            out_specs.append(pl.BlockSpec((None, tm, sg.width), lambda bi, i: (bi, i, 0)))
    scratch = ([pltpu.VMEM((PROJ_COLS // LANES, tm, LANES), F32)]
               if any(sg.dil > 1 for sg in segs) else [])
    return pl.pallas_call(
        functools.partial(_norm_proj_kernel, segs, use_rope),
        grid=(b, s // tm),
        in_specs=in_specs,
        out_specs=out_specs,
        out_shape=out_shape,
        scratch_shapes=scratch,
        compiler_params=_cparams(("arbitrary", "arbitrary")),
        name="norm_proj",
    )(*args)


def _out_proj_kernel(final, *refs):
    if final:
        o_ref, w_ref, x_ref, g_ref, y_ref = refs
    else:
        o_ref, w_ref, x_ref, y_ref = refs
    y = x_ref[...] + _dot(o_ref[...], w_ref[...])
    if final:
        var = jnp.mean(y * y, axis=-1, keepdims=True)
        y = y * lax.rsqrt(var + NORM_EPS) * g_ref[...]
    y_ref[...] = y


def _out_proj(o, w, x, final_gain=None):
    b, s, d = x.shape
    m = o.shape[-1]
    tm = PROJ_ROWS
    final = final_gain is not None
    row = lambda bi, i: (bi, i, 0)
    in_specs = [pl.BlockSpec((None, tm, m), row),
                pl.BlockSpec((m, d), lambda bi, i: (0, 0)),
                pl.BlockSpec((None, tm, d), row)]
    args = [o, w.astype(BF16), x]
    if final:
        in_specs.append(pl.BlockSpec((1, d), lambda bi, i: (0, 0)))
        args.append(final_gain[None, :])
    return pl.pallas_call(
        functools.partial(_out_proj_kernel, final),
        grid=(b, s // tm),
        in_specs=in_specs,
        out_specs=pl.BlockSpec((None, tm, d), row),
        out_shape=jax.ShapeDtypeStruct((b, s, d), F32),
        compiler_params=_cparams(("arbitrary", "arbitrary")),
        name="out_proj",
    )(*args)


BANDED_MAX_QB = 2


def _banded_kernel(n_qb, max_dist, n_prev, n_kv, has_sink, has_gate, want_lse, *refs):
    q_ref = refs[0]
    n_blocks = n_qb + n_prev
    k_refs = refs[1:1 + n_blocks]
    v_refs = refs[1 + n_blocks:1 + 2 * n_blocks]
    k = 1 + 2 * n_blocks
    sink_ref = gate_ref = lse_ref = None
    if has_sink:
        sink_ref = refs[k]; k += 1
    if has_gate:
        gate_ref = refs[k]; k += 1
    o_ref = refs[k]; k += 1
    if want_lse:
        lse_ref = refs[k]; k += 1
    ot_sc = refs[k]
    st_sc = refs[k + 1]

    n = pl.program_id(1)
    tq = BLK
    kw = (n_prev + 1) * BLK
    rep = N_HEADS // n_kv
    krow = lax.broadcasted_iota(jnp.int32, (kw, tq), 0) - n_prev * BLK
    dist = lax.broadcasted_iota(jnp.int32, (kw, tq), 1) - krow
    band = (dist >= 0) & (dist <= max_dist)
    lse_rows = [[] for _ in range(n_qb)]
    units = [(qb, g) for qb in range(n_qb) for g in range(n_kv)]

    def scores(u):
        qb, g = units[u]
        sl = slice((g // 2) * LANES, (g // 2 + 1) * LANES)
        kcat = jnp.concatenate([kr[:, sl] for kr in k_refs[qb:qb + n_prev + 1]], axis=0)
        st_sc[u] = _dot_nt(kcat, _group_queries(q_ref.at[qb * tq:(qb + 1) * tq], g, rep))

    def consume(u):
        qb, g = units[u]
        sl = slice((g // 2) * LANES, (g // 2 + 1) * LANES)
        ok = band & ((n * n_qb + qb) * BLK + krow >= 0)
        vcat = jnp.concatenate([vr[:, sl] for vr in v_refs[qb:qb + n_prev + 1]], axis=0)
        vt = vcat.astype(F32).T.astype(BF16)
        ms, pts = [], []
        for r in range(rep):
            s_r = jnp.where(ok, st_sc[u, :, r * tq:(r + 1) * tq], MASK_VALUE)
            m = jnp.max(s_r, axis=0, keepdims=True)
            if has_sink:
                h = g * rep + r
                m = jnp.maximum(m, sink_ref[0:1, h:h + 1])
            pts.append(jnp.exp2(s_r - m).astype(BF16))
            ms.append(m)
        acc = _dot(_with_ones(vt[(g % 2) * HEAD_DIM:(g % 2 + 1) * HEAD_DIM, :]),
                   jnp.concatenate(pts, axis=1))
        for r in range(rep):
            h = g * rep + r
            den = acc[HEAD_DIM:HEAD_DIM + 1, r * tq:(r + 1) * tq]
            if has_sink:
                den = den + jnp.exp2(sink_ref[0:1, h:h + 1] - ms[r])
            ot_sc[qb * N_HEADS + h] = acc[:HEAD_DIM, r * tq:(r + 1) * tq] / den
            if want_lse:
                lse_rows[qb].append(ms[r] + jnp.log2(den))

    scores(0)
    scores(1)
    for u in range(len(units)):
        if u + 2 < len(units):
            scores(u + 2)
        consume(u)
    for qb in range(n_qb):
        rows = slice(qb * tq, (qb + 1) * tq)
        for p in range(N_PAIRS):
            sl = slice(p * LANES, (p + 1) * LANES)
            pair = jnp.concatenate([ot_sc[qb * N_HEADS + 2 * p],
                                    ot_sc[qb * N_HEADS + 2 * p + 1]], axis=0).T
            if has_gate:
                pair = pair * _silu(gate_ref[rows, sl].astype(F32))
            o_ref[rows, sl] = pair.astype(o_ref.dtype)
        if want_lse:
            lse_t = jnp.concatenate(
                lse_rows[qb] + [jnp.zeros((LANES - N_HEADS, tq), F32)], axis=0)
            lse_ref[rows, :] = lse_t.T


def _banded_attention(q, k, v, max_dist, sinks=None, gate=None, want_lse=False):
    bq, sq, _ = q.shape
    kvw = k.shape[-1]
    n_kv = kvw // HEAD_DIM
    n_prev = -(-max_dist // BLK)
    n_qb = BANDED_MAX_QB if sq % (BANDED_MAX_QB * BLK) == 0 else 1
    tq = n_qb * BLK
    assert sq % tq == 0
    row = lambda b, i: (b, i, 0)
    in_specs = [pl.BlockSpec((None, tq, MIX_WIDTH), row)]
    args = [q]
    for arr in (k, v):
        for j in range(n_prev, -n_qb, -1):
            in_specs.append(pl.BlockSpec(
                (None, BLK, kvw), lambda b, i, j=j: (b, jnp.maximum(n_qb * i - j, 0), 0)))
            args.append(arr)
    if sinks is not None:
        in_specs.append(pl.BlockSpec((1, LANES), lambda b, i: (0, 0)))
        args.append(jnp.pad(sinks.astype(F32) * LOG2E, (0, LANES - N_HEADS))[None, :])
    if gate is not None:
        in_specs.append(pl.BlockSpec((None, tq, MIX_WIDTH), row))
        args.append(gate)
    out_specs = [pl.BlockSpec((None, tq, MIX_WIDTH), row)]
    out_shape = [jax.ShapeDtypeStruct((bq, sq, MIX_WIDTH), BF16)]
    if want_lse:
        out_specs.append(pl.BlockSpec((None, tq, LANES), row))
        out_shape.append(jax.ShapeDtypeStruct((bq, sq, LANES), F32))
    res = pl.pallas_call(
        functools.partial(_banded_kernel, n_qb, max_dist, n_prev, n_kv, sinks is not None,
                          gate is not None, want_lse),
        grid=(bq, sq // tq),
        in_specs=in_specs,
        out_specs=out_specs,
        out_shape=out_shape,
        scratch_shapes=[pltpu.VMEM((n_qb * N_HEADS, HEAD_DIM, BLK), F32),
                        pltpu.VMEM((n_qb * n_kv, (n_prev + 1) * BLK,
                                    (N_HEADS // n_kv) * BLK), F32)],
        compiler_params=_cparams(("arbitrary", "arbitrary")),
        name="banded_attn",
    )(*args)
    return res if want_lse else res[0]


def _dil_combine_kernel(dils, *refs):
    ng = len(dils)
    o_refs, l_refs = refs[:ng], refs[ng:2 * ng]
    gate_ref, out_ref, stage_ref, lstage_ref = refs[2 * ng:]
    rows = out_ref.shape[0]

    def natural(ref, dil, stage, slab):
        sl = slice(slab * LANES, (slab + 1) * LANES)
        if dil == 1:
            return ref[:, sl].astype(F32)
        sub = rows // dil
        for r in range(dil):
            stage[pl.ds(r, sub, stride=dil), :] = ref[r, :, sl].astype(F32)
        return stage[...]

    lses = [natural(l_refs[i], dils[i], lstage_ref, 0) for i in range(ng)]
    mx = functools.reduce(jnp.maximum, lses)
    ws = [jnp.exp2(l - mx) for l in lses]
    tot = functools.reduce(lambda a, c: a + c, ws)
    ws = [w / tot for w in ws]
    half = _lane_half((rows, LANES))
    outs = [None] * N_PAIRS
    for gi in range(ng):
        for p in range(N_PAIRS):
            og = natural(o_refs[gi], dils[gi], stage_ref, p)
            w0 = ws[gi][:, 2 * p:2 * p + 1]
            w1 = ws[gi][:, 2 * p + 1:2 * p + 2]
            wp = jnp.where(half == 0, w0, w1)
            t = wp * og
            outs[p] = t if outs[p] is None else outs[p] + t
    for p in range(N_PAIRS):
        sl = slice(p * LANES, (p + 1) * LANES)
        out_ref[:, sl] = (outs[p] * _silu(gate_ref[:, sl].astype(F32))).astype(out_ref.dtype)


def _dil_combine(os_, lses, gate, dils):
    b, s, _ = gate.shape
    tm = 256
    in_specs, args = [], []
    for arrs, width in ((os_, MIX_WIDTH), (lses, LANES)):
        for arr, dil in zip(arrs, dils):
            if dil == 1:
                in_specs.append(pl.BlockSpec((None, tm, width), lambda bi, i: (bi, i, 0)))
            else:
                in_specs.append(pl.BlockSpec((None, dil, tm // dil, width),
                                             lambda bi, i: (bi, 0, i, 0)))
            args.append(arr)
    in_specs.append(pl.BlockSpec((None, tm, MIX_WIDTH), lambda bi, i: (bi, i, 0)))
    args.append(gate)
    return pl.pallas_call(
        functools.partial(_dil_combine_kernel, dils),
        grid=(b, s // tm),
        in_specs=in_specs,
        out_specs=pl.BlockSpec((None, tm, MIX_WIDTH), lambda bi, i: (bi, i, 0)),
        out_shape=jax.ShapeDtypeStruct((b, s, MIX_WIDTH), BF16),
        scratch_shapes=[pltpu.VMEM((tm, LANES), F32), pltpu.VMEM((tm, LANES), F32)],
        compiler_params=_cparams(("arbitrary", "arbitrary")),
        name="dil_combine",
    )(*args)


N_BIAS_PIECES = 3


def _fox_decay_kernel(f_ref, b_ref, kb_ref):
    s = f_ref.shape[0]
    x = f_ref[...] + b_ref[...]
    logf = jnp.minimum(x, 0.0) - jnp.log1p(jnp.exp(-jnp.abs(x)))
    r = lax.broadcasted_iota(jnp.int32, (BLK, BLK), 0)
    c = lax.broadcasted_iota(jnp.int32, (BLK, BLK), 1)
    tri = (c <= r).astype(F32)
    lane = lax.broadcasted_iota(jnp.int32, (BLK, LANES), 1)
    carry = jnp.zeros((1, LANES), F32)
    for i in range(s // BLK):
        blk = logf[i * BLK:(i + 1) * BLK, :]
        cs = jnp.dot(tri, blk, preferred_element_type=F32, precision=lax.Precision.HIGHEST) + carry
        carry = cs[BLK - 1:BLK, :]
        rest = cs * (-LOG2E)
        out = jnp.zeros((BLK, LANES), F32)
        for j in range(N_BIAS_PIECES):
            piece = rest.astype(BF16).astype(F32)
            rest = rest - piece
            moved = piece if j == 0 else pltpu.roll(piece, N_HEADS * j, 1)
            out = jnp.where((lane >= N_HEADS * j) & (lane < N_HEADS * (j + 1)), moved, out)
        kb_ref[i * BLK:(i + 1) * BLK, :] = out.astype(BF16)


def _fox_decay(f_logit, b_f):
    b, s, _ = f_logit.shape
    b_pad = jnp.pad(b_f.astype(F32), (0, LANES - N_HEADS))[None, :]
    return pl.pallas_call(
        _fox_decay_kernel,
        grid=(b,),
        in_specs=[pl.BlockSpec((None, s, LANES), lambda bi: (bi, 0, 0)),
                  pl.BlockSpec((1, LANES), lambda bi: (0, 0))],
        out_specs=pl.BlockSpec((None, s, LANES), lambda bi: (bi, 0, 0)),
        out_shape=jax.ShapeDtypeStruct((b, s, LANES), BF16),
        compiler_params=_cparams(("arbitrary",)),
        name="fox_decay",
    )(f_logit, b_pad)


FOX_T = PROJ_ROWS


def _fox_kernel(qt_ref, k_ref, kb_ref, vt_ref, gate_ref, o_ref, m_sc, acc_sc, st_sc):
    pi = pl.program_id(1)
    i = pl.program_id(2)
    t = FOX_T
    qt = qt_ref[0]
    row = lax.broadcasted_iota(jnp.int32, (LANES, t), 0)
    zeros = jnp.zeros((HEAD_DIM, t), BF16)
    qx = []
    for e in range(2):
        h = 2 * pi + e
        pick = (row % N_HEADS == h) & (row < N_HEADS * N_BIAS_PIECES)
        sel = jnp.where(pick, 1.0, 0.0).astype(BF16)
        mine = qt[e * HEAD_DIM:(e + 1) * HEAD_DIM, :]
        top = [mine, zeros] if e == 0 else [zeros, mine]
        qx.append(jnp.concatenate(top + [sel], axis=0))
        m_sc[e] = jnp.full((1, t), MASK_VALUE, F32)
        acc_sc[e] = jnp.zeros((ACC_ROWS, t), F32)

    tk = t // 2

    def scores(jt, hf, e):
        k0 = pl.multiple_of(jt * t + hf * tk, tk)
        kx = jnp.concatenate([k_ref[pl.ds(k0, tk), :], kb_ref[pl.ds(k0, tk), :]], axis=1)
        st_sc[e, hf] = _dot(kx, qx[e])

    def consume(jt, hf, e, diag):
        st = st_sc[e, hf]
        if diag:
            ok = (lax.broadcasted_iota(jnp.int32, (tk, t), 0) + hf * tk
                  <= lax.broadcasted_iota(jnp.int32, (tk, t), 1))
            st = jnp.where(ok, st, MASK_VALUE)
        m_old = m_sc[e]
        m_new = jnp.maximum(m_old, jnp.max(st, axis=0, keepdims=True))
        pt = jnp.exp2(st - m_new).astype(BF16)
        alpha = jnp.exp2(m_old - m_new)
        vt = vt_ref[jt, e * HEAD_DIM:(e + 1) * HEAD_DIM, hf * tk:(hf + 1) * tk]
        acc_sc[e] = alpha * acc_sc[e] + _dot(_with_ones(vt), pt)
        m_sc[e] = m_new

    for e in range(2):
        scores(0, 0, e)

    def body(jt):
        for e in range(2):
            scores(jt, 1, e)
            consume(jt, 0, e, False)
        for e in range(2):
            scores(jt + 1, 0, e)
            consume(jt, 1, e, False)

    _loop_in_pairs(i, body)
    for e in range(2):
        scores(i, 1, e)
        consume(i, 0, e, True)
    for e in range(2):
        consume(i, 1, e, True)
    outs = []
    for e in range(2):
        acc = acc_sc[e]
        outs.append(acc[:HEAD_DIM] / jnp.maximum(acc[HEAD_DIM:HEAD_DIM + 1], 1e-30))
    out_t = jnp.concatenate(outs, axis=0)
    o_ref[...] = (out_t.T * _silu(gate_ref[...].astype(F32))).astype(o_ref.dtype)


def _fox_attention(qt, k, kb, vt, gate):
    b, s, _ = k.shape
    t = FOX_T
    assert s % t == 0 and vt.shape == (b, s // t, MIX_WIDTH, t) and qt.shape == vt.shape
    tile = lambda bi, p, i: (bi, i, p)
    return pl.pallas_call(
        _fox_kernel,
        grid=(b, N_PAIRS, s // t),
        in_specs=[pl.BlockSpec((None, 1, LANES, t), lambda bi, p, i: (bi, i, p, 0)),
                  pl.BlockSpec((None, s, LANES), lambda bi, p, i: (bi, 0, p)),
                  pl.BlockSpec((None, s, LANES), lambda bi, p, i: (bi, 0, 0)),
                  pl.BlockSpec((None, s // t, LANES, t), lambda bi, p, i: (bi, 0, p, 0)),
                  pl.BlockSpec((None, t, LANES), tile)],
        out_specs=pl.BlockSpec((None, t, LANES), tile),
        out_shape=jax.ShapeDtypeStruct((b, s, MIX_WIDTH), BF16),
        scratch_shapes=[pltpu.VMEM((2, 1, t), F32), pltpu.VMEM((2, ACC_ROWS, t), F32),
                        pltpu.VMEM((2, 2, t // 2, t), F32)],
        compiler_params=_cparams(("arbitrary", "arbitrary", "arbitrary")),
        name="fox_attn",
    )(qt, k, kb, vt, gate)


def _gelu_tanh(x):
    return 0.5 * x * (1.0 + jnp.tanh(math.sqrt(2.0 / math.pi) * (x + 0.044715 * (x * x * x))))


def _nsa_compress_kernel(ak_ref, av_ref, pe_ref, wa_ref, wb_ref, w2_ref, w2t_ref, c_ref, s_ref,
                         kc_ref, vct_ref):
    nrow = ak_ref.shape[0]
    for idx, a_ref in enumerate((ak_ref, av_ref)):
        a = a_ref[...].astype(F32)
        xa = (a + pe_ref[idx, 0:1, :]).astype(BF16)
        xb = (a + pe_ref[idx, 1:2, :]).astype(BF16)
        ya = _dot(xa, wa_ref[idx])
        yb = _dot(xb, wb_ref[idx])
        hid = _gelu_tanh(ya + pltpu.roll(yb, nrow - 1, 0)).astype(BF16)
        if idx == 0:
            y = _dot(hid, w2_ref[...])
            kc_ref[...] = _apply_rope(y, c_ref[...], s_ref[...]).astype(kc_ref.dtype)
        else:
            vct_ref[...] = _dot_nt(w2t_ref[...], hid).astype(vct_ref.dtype)


def _nsa_compress(kc, vc, pe_k, w1_k, w2_k, pe_v, w1_v, w2_v, rope_c, rope_s):
    b, s, _ = kc.shape
    ns = s // CMP_STRIDE
    g = NSA_KV
    flat = CMP_STRIDE * g * HEAD_DIM

    def w1_halves(w1):
        w1r = w1.reshape(2, CMP_STRIDE, HEAD_DIM, CMP_HIDDEN)
        outs = []
        for hf in range(2):
            z = jnp.einsum('ldc,gh->lgdhc', w1r[hf], jnp.eye(g, dtype=F32))
            outs.append(z.reshape(flat, g * CMP_HIDDEN))
        return outs

    def pe_halves(pe):
        per = pe.reshape(2, CMP_STRIDE, 1, HEAD_DIM)
        return jnp.broadcast_to(per, (2, CMP_STRIDE, g, HEAD_DIM)).reshape(2, flat)

    def w2_bd(w2):
        z = jnp.einsum('cd,gh->gchd', w2, jnp.eye(g, dtype=F32))
        return z.reshape(g * CMP_HIDDEN, g * HEAD_DIM)

    ka, kb = w1_halves(w1_k)
    va, vb = w1_halves(w1_v)
    wa = jnp.stack([ka, va]).astype(BF16)
    wb = jnp.stack([kb, vb]).astype(BF16)
    w2 = w2_bd(w2_k).astype(BF16)
    w2t = w2_bd(w2_v).T.astype(BF16)
    pe = jnp.stack([pe_halves(pe_k), pe_halves(pe_v)]).astype(F32)
    whole = lambda bi: (0, 0, 0)
    per_b = lambda bi: (bi, 0, 0)
    return pl.pallas_call(
        _nsa_compress_kernel,
        grid=(b,),
        in_specs=[pl.BlockSpec((None, ns, flat), per_b),
                  pl.BlockSpec((None, ns, flat), per_b),
                  pl.BlockSpec((2, 2, flat), whole),
                  pl.BlockSpec((2, flat, g * CMP_HIDDEN), whole),
                  pl.BlockSpec((2, flat, g * CMP_HIDDEN), whole),
                  pl.BlockSpec((g * CMP_HIDDEN, LANES), lambda bi: (0, 0)),
                  pl.BlockSpec((LANES, g * CMP_HIDDEN), lambda bi: (0, 0)),
                  pl.BlockSpec((None, ns, LANES), per_b),
                  pl.BlockSpec((None, ns, LANES), per_b)],
        out_specs=[pl.BlockSpec((None, ns, LANES), per_b),
                   pl.BlockSpec((None, LANES, ns), per_b)],
        out_shape=[jax.ShapeDtypeStruct((b, ns, LANES), BF16),
                   jax.ShapeDtypeStruct((b, LANES, ns), BF16)],
        compiler_params=_cparams(("arbitrary",)),
        name="nsa_compress",
    )(kc.reshape(b, ns, flat), vc.reshape(b, ns, flat), pe, wa, wb, w2, w2t,
      rope_c.reshape(b, ns, LANES), rope_s.reshape(b, ns, LANES))


NSA_TQ = 128
NSA_TK = 256
NSA_REP = N_HEADS // NSA_KV


def _nsa_kernel(n_cmp, n_sel, n_win, *refs):
    q_ref, kc_ref, vct_ref, ks_ref, vst_ref = refs[:5]
    kw_refs = refs[5:5 + n_win]
    vwt_refs = refs[5 + n_win:5 + 2 * n_win]
    gl_ref, gate_ref, ovt_ref, blk_ref, o_ref = refs[5 + 2 * n_win:10 + 2 * n_win]
    st_sc, stc_sc, stw_sc, m_sc, acc_sc, ocmp_sc, owin_sc = refs[10 + 2 * n_win:]

    n = pl.program_id(1)
    tq, tk, rep = NSA_TQ, NSA_TK, NSA_REP
    ncp = kc_ref.shape[0]
    t_lane = n * tq + lax.broadcasted_iota(jnp.int32, (1, tq), 1)
    qg = [_group_queries(q_ref, g, rep) for g in range(NSA_KV)]

    ci = lax.broadcasted_iota(jnp.int32, (ncp, tq), 0)
    cmask = (ci * CMP_STRIDE + (CMP_LEN - 1) <= t_lane) & (ci < n_cmp)
    kwn = n_win * BLK
    krow = lax.broadcasted_iota(jnp.int32, (kwn, tq), 0) - (n_win - 1) * BLK
    dist = lax.broadcasted_iota(jnp.int32, (kwn, tq), 1) - krow
    wok = (dist >= 0) & (dist <= NSA_WINDOW - 1) & (n * BLK + krow >= 0)
    sel_rows = ovt_ref.shape[0]
    rowi = lax.broadcasted_iota(jnp.int32, (sel_rows, tq), 0)
    rowf = rowi.astype(F32)
    cur = t_lane // SEL_LEN
    forced = (rowi == 0) | (rowi == cur)
    causal = rowi <= cur

    def cmp_scores(g):
        stc_sc[g] = _dot_nt(kc_ref[...], qg[g])

    def cmp_consume(g):
        ps = []
        for r in range(rep):
            s_r = jnp.where(cmask, stc_sc[g, :, r * tq:(r + 1) * tq], MASK_VALUE)
            m = jnp.max(s_r, axis=0, keepdims=True)
            pr = jnp.where(cmask, jnp.exp2(s_r - m), 0.0)
            ps.append(pr.astype(BF16))
        lhs = jnp.concatenate([_with_ones(vct_ref[g * HEAD_DIM:(g + 1) * HEAD_DIM, :]),
                               ovt_ref[...]], axis=0)
        res = _dot(lhs, jnp.concatenate(ps, axis=1))
        inv = 1.0 / jnp.maximum(res[HEAD_DIM:HEAD_DIM + 1], 1e-30)
        ocmp_sc[g] = res[:HEAD_DIM] * inv
        imp = None
        for r in range(rep):
            ql = slice(r * tq, (r + 1) * tq)
            t = res[ACC_ROWS:, ql] * inv[:, ql]
            imp = t if imp is None else imp + t
        return imp

    def select(g, imp):
        score = jnp.where(causal, jnp.where(forced, FORCED_SCORE, imp), MASK_VALUE)
        score = jnp.where(rowi < n_sel, score, PAD_SCORE)
        chosen = jnp.zeros((sel_rows, tq), F32)
        for _ in range(min(SEL_TOPK, n_sel)):
            mx = jnp.max(score, axis=0, keepdims=True)
            first = jnp.min(jnp.where(score == mx, rowf, float(LANES)), axis=0, keepdims=True)
            hit = rowf == first
            chosen = jnp.where(hit, 1.0, chosen)
            score = jnp.where(hit, PAD_SCORE, score)
        bias_t = jnp.where(causal & (chosen > 0.5), 0.0, MASK_VALUE)
        bias_t = jnp.concatenate([bias_t, jnp.zeros((LANES - sel_rows, tq), F32)], axis=0)
        bias = bias_t.T.astype(BF16)
        return jnp.concatenate([qg[g], jnp.concatenate([bias] * rep, axis=0)], axis=1)

    def win_scores(g):
        kwcat = jnp.concatenate([kr[...] for kr in kw_refs], axis=0)
        stw_sc[g] = _dot_nt(kwcat, qg[g])

    def win_consume(g):
        vwt = jnp.concatenate([vr[0] for vr in vwt_refs], axis=1)
        pts = []
        for r in range(rep):
            s_r = jnp.where(wok, stw_sc[g, :, r * tq:(r + 1) * tq], MASK_VALUE)
            m = jnp.max(s_r, axis=0, keepdims=True)
            pts.append(jnp.exp2(s_r - m).astype(BF16))
        owin_sc[g] = _dot(_with_ones(vwt[g * HEAD_DIM:(g + 1) * HEAD_DIM, :]),
                          jnp.concatenate(pts, axis=1))

    cmp_scores(0)
    cmp_scores(1)
    psum0 = cmp_consume(0)
    win_scores(0)
    psum1 = cmp_consume(1)
    win_scores(1)
    qsel = [select(0, psum0)]
    win_consume(0)
    qsel.append(select(1, psum1))
    win_consume(1)

    for g in range(NSA_KV):
        m_sc[g] = jnp.full((1, rep * tq), MASK_VALUE, F32)
        acc_sc[g] = jnp.zeros((ACC_ROWS, rep * tq), F32)

    hrep = rep // 2
    wide = hrep * tq
    units = [(g, hf) for g in range(NSA_KV) for hf in range(2)]

    def scores(c, u):
        g, hf = units[u]
        k0 = pl.multiple_of(c * tk, tk)
        kx = jnp.concatenate([ks_ref[pl.ds(k0, tk), :], blk_ref[pl.ds(k0, tk), :]], axis=1)
        st_sc[u] = _dot_nt(kx, qsel[g][hf * wide:(hf + 1) * wide, :])

    def consume(c, u, last):
        g, hf = units[u]
        lanes = slice(hf * wide, (hf + 1) * wide)
        if last:
            tok_ok = c * tk + lax.broadcasted_iota(jnp.int32, (tk, tq), 0) <= t_lane
        m_old = m_sc[g, :, lanes]
        ms, pts = [], []
        for r in range(hrep):
            s_r = st_sc[u, :, r * tq:(r + 1) * tq]
            if last:
                s_r = jnp.where(tok_ok, s_r, MASK_VALUE)
            m_new = jnp.maximum(m_old[:, r * tq:(r + 1) * tq], jnp.max(s_r, axis=0, keepdims=True))
            pts.append(jnp.exp2(s_r - m_new).astype(BF16))
            ms.append(m_new)
        m_new = jnp.concatenate(ms, axis=1)
        alpha = jnp.exp2(m_old - m_new)
        vt = vst_ref[c, g * HEAD_DIM:(g + 1) * HEAD_DIM, :]
        acc_sc[g, :, lanes] = (alpha * acc_sc[g, :, lanes]
                               + _dot(_with_ones(vt), jnp.concatenate(pts, axis=1)))
        m_sc[g, :, lanes] = m_new

    n_steps = (n * tq + tq + tk - 1) // tk
    scores(0, 0)
    scores(0, 1)

    def body(c):
        scores(c, 2)
        consume(c, 0, False)
        scores(c, 3)
        consume(c, 1, False)
        scores(c + 1, 0)
        consume(c, 2, False)
        scores(c + 1, 1)
        consume(c, 3, False)

    _loop_in_pairs(n_steps - 1, body)
    scores(n_steps - 1, 2)
    consume(n_steps - 1, 0, True)
    scores(n_steps - 1, 3)
    consume(n_steps - 1, 1, True)
    consume(n_steps - 1, 2, True)
    consume(n_steps - 1, 3, True)

    gate_t = (1.0 / (1.0 + jnp.exp(-gl_ref[...]))).T
    for p in range(N_PAIRS):
        sl = slice(p * LANES, (p + 1) * LANES)
        g = (2 * p) // rep
        rows = []
        for e in range(2):
            h = 2 * p + e
            ql = slice((h - g * rep) * tq, (h - g * rep + 1) * tq)
            slc, win = acc_sc[g][:, ql], owin_sc[g][:, ql]
            branches = (ocmp_sc[g][:, ql],
                        slc[:HEAD_DIM] / jnp.maximum(slc[HEAD_DIM:HEAD_DIM + 1], 1e-30),
                        win[:HEAD_DIM] / win[HEAD_DIM:HEAD_DIM + 1])
            out = None
            for j, br in enumerate(branches):
                t = gate_t[3 * h + j:3 * h + j + 1, :] * br
                out = t if out is None else out + t
            rows.append(out)
        pair = jnp.concatenate(rows, axis=0).T
        o_ref[:, sl] = (pair * _silu(gate_ref[:, sl].astype(F32))).astype(o_ref.dtype)


def _selection_overlap_t(n_cmp_pad, n_cmp, n_sel):
    cs = np.arange(n_cmp_pad) * CMP_STRIDE
    js = np.arange(LANES) * SEL_LEN
    ov = np.minimum(cs[None, :] + CMP_LEN, js[:, None] + SEL_LEN) - np.maximum(cs[None, :], js[:, None])
    ov = (np.clip(ov, 0, None) / CMP_LEN).astype(np.float32)
    ov[:, n_cmp:] = 0.0
    ov[n_sel:, :] = 0.0
    return ov


def _nsa_attention(q, k_cmp, v_cmp_t, ks, vs_t, kw, vw_t, g_logit, gate):
    b, s, _ = q.shape
    tq = NSA_TQ
    ncp = k_cmp.shape[1]
    n_cmp = s // CMP_STRIDE - 1
    n_sel = s // SEL_LEN
    n_win = -(-(NSA_WINDOW - 1) // BLK) + 1
    assert n_sel <= LANES and s % NSA_TK == 0 and tq == BLK
    sel_rows = -(-n_sel // 16) * 16
    ov_t = jnp.asarray(_selection_overlap_t(ncp, n_cmp, n_sel)[:sel_rows], BF16)
    row = lambda bi, i: (bi, i, 0)
    per_b = lambda bi, i: (bi, 0, 0)
    in_specs = [pl.BlockSpec((None, tq, MIX_WIDTH), row),
                pl.BlockSpec((None, ncp, LANES), per_b),
                pl.BlockSpec((None, LANES, ncp), per_b),
                pl.BlockSpec((None, s, LANES), per_b),
                pl.BlockSpec((None, s // NSA_TK, LANES, NSA_TK), lambda bi, i: (bi, 0, 0, 0))]
    args = [q, k_cmp, v_cmp_t, ks, vs_t]
    for j in range(n_win - 1, -1, -1):
        in_specs.append(pl.BlockSpec((None, BLK, LANES),
                                     lambda bi, i, j=j: (bi, jnp.maximum(i - j, 0), 0)))
        args.append(kw)
    for j in range(n_win - 1, -1, -1):
        in_specs.append(pl.BlockSpec((None, 1, LANES, BLK),
                                     lambda bi, i, j=j: (bi, jnp.maximum(i - j, 0), 0, 0)))
        args.append(vw_t)
    block_onehot = jnp.asarray(
        (np.arange(s)[:, None] // SEL_LEN == np.arange(LANES)[None, :]).astype(np.float32), BF16)
    in_specs += [pl.BlockSpec((None, tq, LANES), row),
                 pl.BlockSpec((None, tq, MIX_WIDTH), row),
                 pl.BlockSpec((sel_rows, ncp), lambda bi, i: (0, 0)),
                 pl.BlockSpec((s, LANES), lambda bi, i: (0, 0))]
    args += [g_logit, gate, ov_t, block_onehot]
    wide = NSA_REP * tq
    return pl.pallas_call(
        functools.partial(_nsa_kernel, n_cmp, n_sel, n_win),
        grid=(b, s // tq),
        in_specs=in_specs,
        out_specs=pl.BlockSpec((None, tq, MIX_WIDTH), row),
        out_shape=jax.ShapeDtypeStruct((b, s, MIX_WIDTH), BF16),
        scratch_shapes=[pltpu.VMEM((2 * NSA_KV, NSA_TK, wide // 2), F32),
                        pltpu.VMEM((NSA_KV, ncp, wide), F32),
                        pltpu.VMEM((NSA_KV, n_win * BLK, wide), F32),
                        pltpu.VMEM((NSA_KV, 1, wide), F32),
                        pltpu.VMEM((NSA_KV, ACC_ROWS, wide), F32),
                        pltpu.VMEM((NSA_KV, HEAD_DIM, wide), F32),
                        pltpu.VMEM((NSA_KV, ACC_ROWS, wide), F32)],
        compiler_params=_cparams(("arbitrary", "arbitrary")),
        name="nsa_attn",
    )(*args)


def _split(w, sizes):
    offs = np.cumsum([0] + list(sizes))
    return [w[:, int(offs[i]):int(offs[i + 1])] for i in range(len(sizes))]


def _swa_layer(x, rope, gain, w_in, sinks, w_out):
    kvw = SWA_KV * HEAD_DIM
    parts = _split(w_in, [MIX_WIDTH, kvw, kvw, MIX_WIDTH])
    segs = [_Seg(MIX_WIDTH, rope=True, scale=Q_SCALE), _Seg(kvw, rope=True), _Seg(kvw),
            _Seg(MIX_WIDTH)]
    q, k, v, gate = _norm_proj(x, gain, parts, segs, rope)
    o = _banded_attention(q, k, v, SWA_WINDOW - 1, sinks=sinks, gate=gate)
    return o, w_out


def _dilated_layer(x, rope, gain, w_in, w_out):
    b, s, _ = x.shape
    kvw = DIL_KV * HEAD_DIM
    sizes, segs = [], []
    for window, dil in DIL_PATTERNS:
        assert s % (dil * BLK) == 0
        sizes += [MIX_WIDTH, kvw, kvw]
        segs += [_Seg(MIX_WIDTH, rope=True, scale=Q_SCALE, dil=dil), _Seg(kvw, rope=True, dil=dil),
                 _Seg(kvw, dil=dil)]
    sizes.append(MIX_WIDTH)
    segs.append(_Seg(MIX_WIDTH))
    res = _norm_proj(x, gain, _split(w_in, sizes), segs, rope)
    gate = res[-1]
    os_, lses, dils = [], [], []
    for gi, (window, dil) in enumerate(DIL_PATTERNS):
        q, k, v = res[3 * gi:3 * gi + 3]
        if dil > 1:
            q, k, v = (t.reshape(b * dil, s // dil, t.shape[-1]) for t in (q, k, v))
        o, lse = _banded_attention(q, k, v, window // dil, want_lse=True)
        if dil > 1:
            o = o.reshape(b, dil, s // dil, MIX_WIDTH)
            lse = lse.reshape(b, dil, s // dil, LANES)
        os_.append(o)
        lses.append(lse)
        dils.append(dil)
    return _dil_combine(os_, lses, gate, tuple(dils)), w_out


def _fox_layer(x, gain, w_in, b_f, w_out):
    sizes = [MIX_WIDTH, MIX_WIDTH, MIX_WIDTH, N_HEADS, MIX_WIDTH]
    segs = [_Seg(MIX_WIDTH, scale=Q_SCALE, tile=FOX_T), _Seg(MIX_WIDTH),
            _Seg(MIX_WIDTH, tile=FOX_T), _Seg(LANES, dtype=F32), _Seg(MIX_WIDTH)]
    qt, k, vt, f_logit, gate = _norm_proj(x, gain, _split(w_in, sizes), segs)
    kb = _fox_decay(f_logit, b_f)
    return _fox_attention(qt, k, kb, vt, gate), w_out


def _nsa_layer(x, positions, rope, gain, w_in, pe_k, w1_k, w2_k, pe_v, w1_v, w2_v, w_out):
    b, s, _ = x.shape
    kvw = NSA_KV * HEAD_DIM
    sizes = [MIX_WIDTH] + [kvw] * 6 + [3 * N_HEADS, MIX_WIDTH]
    segs = [_Seg(MIX_WIDTH, rope=True, scale=Q_SCALE), _Seg(kvw), _Seg(kvw), _Seg(kvw, rope=True),
            _Seg(kvw, tile=NSA_TK), _Seg(kvw, rope=True), _Seg(kvw, tile=BLK),
            _Seg(LANES, dtype=F32), _Seg(MIX_WIDTH)]
    q, kc, vc, ks, vs_t, kw, vw_t, g_logit, gate = _norm_proj(
        x, gain, _split(w_in, sizes), segs, rope)
    ns = s // CMP_STRIDE
    cmp_pos = jnp.concatenate(
        [positions[:, CMP_LEN - 1::CMP_STRIDE], positions[:, -1:]], axis=1)[:, :ns]
    cmp_c, cmp_s = _rope_tables(cmp_pos.reshape(-1))
    k_cmp, v_cmp_t = _nsa_compress(kc, vc, pe_k, w1_k, w2_k, pe_v, w1_v, w2_v, cmp_c, cmp_s)
    return _nsa_attention(q, k_cmp, v_cmp_t, ks, vs_t, kw, vw_t, g_logit, gate), w_out


def kernel(x, positions, norm_0, w_in_0, sinks_0, w_out_0, norm_1, w_in_1, w_out_1, norm_2, w_in_2, b_f_2, w_out_2, norm_3, w_in_3, cmp_pe_k_3, cmp_w1_k_3, cmp_w2_k_3, cmp_pe_v_3, cmp_w1_v_3, cmp_w2_v_3, w_out_3, final_norm):
    rope = _rope_tables(positions.reshape(-1))
    o, w = _swa_layer(x, rope, norm_0, w_in_0, sinks_0, w_out_0)
    x = _out_proj(o, w, x)
    o, w = _dilated_layer(x, rope, norm_1, w_in_1, w_out_1)
    x = _out_proj(o, w, x)
    o, w = _fox_layer(x, norm_2, w_in_2, b_f_2, w_out_2)
    x = _out_proj(o, w, x)
    o, w = _nsa_layer(x, positions, rope, norm_3, w_in_3, cmp_pe_k_3, cmp_w1_k_3, cmp_w2_k_3,
                      cmp_pe_v_3, cmp_w1_v_3, cmp_w2_v_3, w_out_3)
    return _out_proj(o, w, x, final_gain=final_norm)
```

```python
import functools
import math

import numpy as np
import jax
import jax.numpy as jnp
from jax import lax
from jax.experimental import pallas as pl
from jax.experimental.pallas import tpu as pltpu

HEAD_DIM = 64
N_HEADS = 16
N_PAIRS = N_HEADS // 2
MIX_WIDTH = N_HEADS * HEAD_DIM
ROT_DIM = HEAD_DIM // 4
ROT_HALF = ROT_DIM // 2
ROPE_THETA = 500000.0
BLK = 128
LANES = 128
NORM_EPS = 1e-6
MASK_VALUE = -1e30
PAD_SCORE = -3e38
LOG2E = math.log2(math.e)
Q_SCALE = HEAD_DIM ** -0.5 * LOG2E

SWA_KV = 4
SWA_WINDOW = 128
DIL_KV = 4
DIL_PATTERNS = ((128, 1), (512, 4), (2048, 16))
NSA_KV = 2
CMP_LEN = 32
CMP_STRIDE = 16
CMP_HIDDEN = 256
SEL_LEN = 64
SEL_TOPK = 8
NSA_WINDOW = 256
FORCED_SCORE = 1e4

VMEM_LIMIT_BYTES = 56 * 1024 * 1024
PROJ_ROWS = 512
PROJ_COLS = 512
ACC_ROWS = HEAD_DIM + 16
F32 = jnp.float32
BF16 = jnp.bfloat16


def _cparams(sem):
    return pltpu.CompilerParams(dimension_semantics=sem, vmem_limit_bytes=VMEM_LIMIT_BYTES)


def _lane_half(shape):
    return lax.broadcasted_iota(jnp.int32, shape, 1) // HEAD_DIM


def _swap_halves(t):
    return jnp.concatenate([t[:, HEAD_DIM:], t[:, :HEAD_DIM]], axis=1)


def _head_query(qp, e, kv_half):
    qh = jnp.where(_lane_half(qp.shape) == e, qp, jnp.zeros_like(qp))
    if e != kv_half:
        qh = _swap_halves(qh)
    return qh


def _group_queries(q_ref, g, rep):
    tiles = []
    for r in range(rep):
        h = g * rep + r
        tiles.append(_head_query(q_ref[:, (h // 2) * LANES:(h // 2 + 1) * LANES], h % 2, g % 2))
    return jnp.concatenate(tiles, axis=0)


def _dot_nt(a, b):
    return lax.dot_general(a, b, (((1,), (1,)), ((), ())), preferred_element_type=F32)


def _dot(a, b):
    return jnp.dot(a, b, preferred_element_type=F32)


def _silu(x):
    return x * (1.0 / (1.0 + jnp.exp(-x)))


def _loop_in_pairs(count, body):
    odd = count % 2

    @pl.when(odd == 1)
    def _():
        body(0)

    def two(jj, carry):
        body(odd + 2 * jj)
        body(odd + 2 * jj + 1)
        return carry

    lax.fori_loop(0, count // 2, two, 0)


def _with_ones(vt):
    return jnp.concatenate([vt, jnp.ones((ACC_ROWS - HEAD_DIM, vt.shape[1]), BF16)], axis=0)


def _rope_table_kernel(pos_ref, inv_ref, c_ref, s_ref):
    pos = pos_ref[...].astype(F32)
    ang = pos * inv_ref[...]
    d = lax.broadcasted_iota(jnp.int32, ang.shape, 1) % HEAD_DIM
    cos = jnp.cos(ang)
    sin = jnp.sin(ang)
    c_ref[...] = jnp.where(d < ROT_DIM, cos, 1.0)
    s_ref[...] = jnp.where(d < ROT_HALF, -sin, jnp.where(d < ROT_DIM, sin, 0.0))


def _rope_tables(pos_flat):
    t = pos_flat.shape[0]
    rows = min(t, 2048)
    assert t % rows == 0
    inv = jnp.power(ROPE_THETA, -jnp.arange(ROT_HALF, dtype=F32) / ROT_HALF)
    inv_l = jnp.tile(inv, LANES // ROT_HALF)[None, :]
    out = jax.ShapeDtypeStruct((t, LANES), F32)
    return pl.pallas_call(
        _rope_table_kernel,
        grid=(t // rows,),
        in_specs=[pl.BlockSpec((rows, 1), lambda i: (i, 0)),
                  pl.BlockSpec((1, LANES), lambda i: (0, 0))],
        out_specs=[pl.BlockSpec((rows, LANES), lambda i: (i, 0))] * 2,
        out_shape=[out, out],
        compiler_params=_cparams(("arbitrary",)),
        name="rope_tables",
    )(pos_flat[:, None], inv_l)


def _apply_rope(y, c, s):
    outs = []
    for j in range(y.shape[1] // LANES):
        t = y[:, j * LANES:(j + 1) * LANES]
        d = lax.broadcasted_iota(jnp.int32, t.shape, 1) % HEAD_DIM
        partner = jnp.where(d < ROT_HALF, pltpu.roll(t, LANES - ROT_HALF, 1),
                            pltpu.roll(t, ROT_HALF, 1))
        outs.append(t * c + partner * s)
    return outs[0] if len(outs) == 1 else jnp.concatenate(outs, axis=1)


class _Seg:
    def __init__(self, width, rope=False, scale=None, dtype=BF16, dil=1, tile=None):
        self.width, self.rope, self.scale, self.dtype, self.dil = width, rope, scale, dtype, dil
        self.tile = tile
        assert not (tile and (rope or dil > 1))


def _norm_proj_kernel(segs, use_rope, *refs):
    has_t = any(sg.tile for sg in segs)
    x_ref, g_ref, w_ref = refs[:3]
    k = 3
    if has_t:
        wt_ref = refs[k]
        k += 1
    if use_rope:
        c_ref, s_ref = refs[k:k + 2]
        k += 2
    out_refs = refs[k:k + len(segs)]
    stage_ref = refs[k + len(segs)] if any(sg.dil > 1 for sg in segs) else None

    x = x_ref[...]
    var = jnp.mean(x * x, axis=-1, keepdims=True)
    h = (x * lax.rsqrt(var + NORM_EPS) * g_ref[...]).astype(BF16)
    rows = x.shape[0]
    col = 0
    tcol = 0
    for sg, o_ref in zip(segs, out_refs):
        if sg.tile:
            for c0 in range(0, sg.width, PROJ_COLS):
                cw = min(PROJ_COLS, sg.width - c0)
                yt = _dot_nt(wt_ref[tcol + c0:tcol + c0 + cw, :], h)
                if sg.scale is not None:
                    yt = yt * sg.scale
                for ti in range(rows // sg.tile):
                    o_ref[ti, c0:c0 + cw, :] = yt[:, ti * sg.tile:(ti + 1) * sg.tile].astype(sg.dtype)
            tcol += sg.width
            continue
        for c0 in range(0, sg.width, PROJ_COLS):
            cw = min(PROJ_COLS, sg.width - c0)
            y = _dot(h, w_ref[:, col + c0:col + c0 + cw])
            if sg.rope:
                y = _apply_rope(y, c_ref[...], s_ref[...])
            if sg.scale is not None:
                y = y * sg.scale
            if sg.dil > 1:
                sub = rows // sg.dil
                for j in range(cw // LANES):
                    stage_ref[j] = y[:, j * LANES:(j + 1) * LANES]
                for r in range(sg.dil):
                    for j in range(cw // LANES):
                        lo = c0 + j * LANES
                        o_ref[r, :, lo:lo + LANES] = (
                            stage_ref[j, pl.ds(r, sub, stride=sg.dil), :].astype(sg.dtype))
            else:
                o_ref[:, c0:c0 + cw] = y.astype(sg.dtype)
        col += sg.width


def _norm_proj(x, gain, w_parts, segs, rope=None):
    b, s, d = x.shape
    tm = PROJ_ROWS
    assert s % tm == 0
    w_cols, wt_rows = [], []
    for wp, sg in zip(w_parts, segs):
        if wp.shape[1] < sg.width:
            wp = jnp.pad(wp, ((0, 0), (0, sg.width - wp.shape[1])))
        if sg.tile:
            wt_rows.append(wp.T)
        else:
            w_cols.append(wp)
    w = jnp.concatenate(w_cols, axis=1).astype(BF16)
    n = w.shape[1]
    use_rope = rope is not None
    in_specs = [pl.BlockSpec((None, tm, d), lambda bi, i: (bi, i, 0)),
                pl.BlockSpec((1, d), lambda bi, i: (0, 0)),
                pl.BlockSpec((d, n), lambda bi, i: (0, 0))]
    args = [x, gain[None, :], w]
    if wt_rows:
        wt = jnp.concatenate(wt_rows, axis=0).astype(BF16)
        in_specs.append(pl.BlockSpec(wt.shape, lambda bi, i: (0, 0)))
        args.append(wt)
    if use_rope:
        in_specs += [pl.BlockSpec((None, tm, LANES), lambda bi, i: (bi, i, 0))] * 2
        args += [rope[0].reshape(b, s, LANES), rope[1].reshape(b, s, LANES)]
    out_specs, out_shape = [], []
    for sg in segs:
        if sg.dil > 1:
            assert tm % sg.dil == 0
            out_shape.append(jax.ShapeDtypeStruct((b, sg.dil, s // sg.dil, sg.width), sg.dtype))
            out_specs.append(pl.BlockSpec((None, sg.dil, tm // sg.dil, sg.width),
                                          lambda bi, i: (bi, 0, i, 0)))
        elif sg.tile:
            assert tm % sg.tile == 0
            out_shape.append(jax.ShapeDtypeStruct((b, s // sg.tile, sg.width, sg.tile), sg.dtype))
            out_specs.append(pl.BlockSpec((None, tm // sg.tile, sg.width, sg.tile),
                                          lambda bi, i: (bi, i, 0, 0)))
        else:
            out_shape.append(jax.ShapeDtypeStruct((b, s, sg.width), sg.dtype))
            out_specs.append(pl.BlockSpec((None, tm, sg.width), lambda bi, i: (bi, i, 0)))
    scratch = ([pltpu.VMEM((PROJ_COLS // LANES, tm, LANES), F32)]
               if any(sg.dil > 1 for sg in segs) else [])
    return pl.pallas_call(
        functools.partial(_norm_proj_kernel, segs, use_rope),
        grid=(b, s // tm),
        in_specs=in_specs,
        out_specs=out_specs,
        out_shape=out_shape,
        scratch_shapes=scratch,
        compiler_params=_cparams(("arbitrary", "arbitrary")),
        name="norm_proj",
    )(*args)


def _out_proj_kernel(final, *refs):
    if final:
        o_ref, w_ref, x_ref, g_ref, y_ref = refs
    else:
        o_ref, w_ref, x_ref, y_ref = refs
    y = x_ref[...] + _dot(o_ref[...], w_ref[...])
    if final:
        var = jnp.mean(y * y, axis=-1, keepdims=True)
        y = y * lax.rsqrt(var + NORM_EPS) * g_ref[...]
    y_ref[...] = y


def _out_proj(o, w, x, final_gain=None):
    b, s, d = x.shape
    m = o.shape[-1]
    tm = PROJ_ROWS
    final = final_gain is not None
    row = lambda bi, i: (bi, i, 0)
    in_specs = [pl.BlockSpec((None, tm, m), row),
                pl.BlockSpec((m, d), lambda bi, i: (0, 0)),
                pl.BlockSpec((None, tm, d), row)]
    args = [o, w.astype(BF16), x]
    if final:
        in_specs.append(pl.BlockSpec((1, d), lambda bi, i: (0, 0)))
        args.append(final_gain[None, :])
    return pl.pallas_call(
        functools.partial(_out_proj_kernel, final),
        grid=(b, s // tm),
        in_specs=in_specs,
        out_specs=pl.BlockSpec((None, tm, d), row),
        out_shape=jax.ShapeDtypeStruct((b, s, d), F32),
        compiler_params=_cparams(("arbitrary", "arbitrary")),
        name="out_proj",
    )(*args)


BANDED_MAX_QB = 2


def _banded_kernel(n_qb, max_dist, n_prev, n_kv, has_sink, has_gate, want_lse, *refs):
    q_ref = refs[0]
    n_blocks = n_qb + n_prev
    k_refs = refs[1:1 + n_blocks]
    v_refs = refs[1 + n_blocks:1 + 2 * n_blocks]
    k = 1 + 2 * n_blocks
    sink_ref = gate_ref = lse_ref = None
    if has_sink:
        sink_ref = refs[k]; k += 1
    if has_gate:
        gate_ref = refs[k]; k += 1
    o_ref = refs[k]; k += 1
    if want_lse:
        lse_ref = refs[k]; k += 1
    ot_sc = refs[k]
    st_sc = refs[k + 1]

    n = pl.program_id(1)
    tq = BLK
    kw = (n_prev + 1) * BLK
    rep = N_HEADS // n_kv
    krow = lax.broadcasted_iota(jnp.int32, (kw, tq), 0) - n_prev * BLK
    dist = lax.broadcasted_iota(jnp.int32, (kw, tq), 1) - krow
    band = (dist >= 0) & (dist <= max_dist)
    lse_rows = [[] for _ in range(n_qb)]
    units = [(qb, g) for qb in range(n_qb) for g in range(n_kv)]

    def scores(u):
        qb, g = units[u]
        sl = slice((g // 2) * LANES, (g // 2 + 1) * LANES)
        kcat = jnp.concatenate([kr[:, sl] for kr in k_refs[qb:qb + n_prev + 1]], axis=0)
        st_sc[u] = _dot_nt(kcat, _group_queries(q_ref.at[qb * tq:(qb + 1) * tq], g, rep))

    def consume(u):
        qb, g = units[u]
        sl = slice((g // 2) * LANES, (g // 2 + 1) * LANES)
        ok = band & ((n * n_qb + qb) * BLK + krow >= 0)
        vcat = jnp.concatenate([vr[:, sl] for vr in v_refs[qb:qb + n_prev + 1]], axis=0)
        vt = vcat.astype(F32).T.astype(BF16)
        ms, pts = [], []
        for r in range(rep):
            s_r = jnp.where(ok, st_sc[u, :, r * tq:(r + 1) * tq], MASK_VALUE)
            m = jnp.max(s_r, axis=0, keepdims=True)
            if has_sink:
                h = g * rep + r
                m = jnp.maximum(m, sink_ref[0:1, h:h + 1])
            pts.append(jnp.exp2(s_r - m).astype(BF16))
            ms.append(m)
        acc = _dot(_with_ones(vt[(g % 2) * HEAD_DIM:(g % 2 + 1) * HEAD_DIM, :]),
                   jnp.concatenate(pts, axis=1))
        for r in range(rep):
            h = g * rep + r
            den = acc[HEAD_DIM:HEAD_DIM + 1, r * tq:(r + 1) * tq]
            if has_sink:
                den = den + jnp.exp2(sink_ref[0:1, h:h + 1] - ms[r])
            ot_sc[qb * N_HEADS + h] = acc[:HEAD_DIM, r * tq:(r + 1) * tq] / den
            if want_lse:
                lse_rows[qb].append(ms[r] + jnp.log2(den))

    scores(0)
    scores(1)
    for u in range(len(units)):
        if u + 2 < len(units):
            scores(u + 2)
        consume(u)
    for qb in range(n_qb):
        rows = slice(qb * tq, (qb + 1) * tq)
        for p in range(N_PAIRS):
            sl = slice(p * LANES, (p + 1) * LANES)
            pair = jnp.concatenate([ot_sc[qb * N_HEADS + 2 * p],
                                    ot_sc[qb * N_HEADS + 2 * p + 1]], axis=0).T
            if has_gate:
                pair = pair * _silu(gate_ref[rows, sl].astype(F32))
            o_ref[rows, sl] = pair.astype(o_ref.dtype)
        if want_lse:
            lse_t = jnp.concatenate(
                lse_rows[qb] + [jnp.zeros((LANES - N_HEADS, tq), F32)], axis=0)
            lse_ref[rows, :] = lse_t.T


def _banded_attention(q, k, v, max_dist, sinks=None, gate=None, want_lse=False):
    bq, sq, _ = q.shape
    kvw = k.shape[-1]
    n_kv = kvw // HEAD_DIM
    n_prev = -(-max_dist // BLK)
    n_qb = BANDED_MAX_QB if sq % (BANDED_MAX_QB * BLK) == 0 else 1
    tq = n_qb * BLK
    assert sq % tq == 0
    row = lambda b, i: (b, i, 0)
    in_specs = [pl.BlockSpec((None, tq, MIX_WIDTH), row)]
    args = [q]
    for arr in (k, v):
        for j in range(n_prev, -n_qb, -1):
            in_specs.append(pl.BlockSpec(
                (None, BLK, kvw), lambda b, i, j=j: (b, jnp.maximum(n_qb * i - j, 0), 0)))
            args.append(arr)
    if sinks is not None:
        in_specs.append(pl.BlockSpec((1, LANES), lambda b, i: (0, 0)))
        args.append(jnp.pad(sinks.astype(F32) * LOG2E, (0, LANES - N_HEADS))[None, :])
    if gate is not None:
        in_specs.append(pl.BlockSpec((None, tq, MIX_WIDTH), row))
        args.append(gate)
    out_specs = [pl.BlockSpec((None, tq, MIX_WIDTH), row)]
    out_shape = [jax.ShapeDtypeStruct((bq, sq, MIX_WIDTH), BF16)]
    if want_lse:
        out_specs.append(pl.BlockSpec((None, tq, LANES), row))
        out_shape.append(jax.ShapeDtypeStruct((bq, sq, LANES), F32))
    res = pl.pallas_call(
        functools.partial(_banded_kernel, n_qb, max_dist, n_prev, n_kv, sinks is not None,
                          gate is not None, want_lse),
        grid=(bq, sq // tq),
        in_specs=in_specs,
        out_specs=out_specs,
        out_shape=out_shape,
        scratch_shapes=[pltpu.VMEM((n_qb * N_HEADS, HEAD_DIM, BLK), F32),
                        pltpu.VMEM((n_qb * n_kv, (n_prev + 1) * BLK,
                                    (N_HEADS // n_kv) * BLK), F32)],
        compiler_params=_cparams(("arbitrary", "arbitrary")),
        name="banded_attn",
    )(*args)
    return res if want_lse else res[0]


def _dil_combine_kernel(dils, *refs):
    ng = len(dils)
    o_refs, l_refs = refs[:ng], refs[ng:2 * ng]
    gate_ref, expand_ref, w_ref, x_ref, out_ref, stage_ref, lstage_ref = refs[2 * ng:]
    rows = out_ref.shape[0]

    def natural(ref, dil, stage, slab):
        sl = slice(slab * LANES, (slab + 1) * LANES)
        if dil == 1:
            return ref[:, sl].astype(F32)
        sub = rows // dil
        for r in range(dil):
            stage[pl.ds(r, sub, stride=dil), :] = ref[r, :, sl].astype(F32)
        return stage[...]

    lses = [natural(l_refs[i], dils[i], lstage_ref, 0) for i in range(ng)]
    mx = functools.reduce(jnp.maximum, lses)
    ws = [jnp.exp2(l - mx) for l in lses]
    tot = functools.reduce(lambda a, c: a + c, ws)
    inv = 1.0 / tot

    def widen(w):
        hi = w.astype(BF16)
        lo = (w - hi.astype(F32)).astype(BF16)
        return _dot(hi, expand_ref[...]) + _dot(lo, expand_ref[...])

    wide = [widen(w * inv) for w in ws[:-1]]
    wide.append(1.0 - functools.reduce(lambda a, c: a + c, wide))
    gated = []
    for p in range(N_PAIRS):
        sl = slice(p * LANES, (p + 1) * LANES)
        acc = None
        for gi in range(ng):
            t = wide[gi][:, sl] * natural(o_refs[gi], dils[gi], stage_ref, p)
            acc = t if acc is None else acc + t
        gated.append((acc * _silu(gate_ref[:, sl].astype(F32))).astype(BF16))
    out_ref[...] = x_ref[...] + _dot(jnp.concatenate(gated, axis=1), w_ref[...])


def _dil_combine_out_proj(os_, lses, gate, dils, w_out, x):
    b, s, d = x.shape
    tm = PROJ_ROWS
    expand = jnp.asarray((np.arange(LANES)[:, None] == np.arange(MIX_WIDTH)[None, :] // HEAD_DIM)
                         .astype(np.float32), BF16)
    in_specs, args = [], []
    for arrs, width in ((os_, MIX_WIDTH), (lses, LANES)):
        for arr, dil in zip(arrs, dils):
            if dil == 1:
                in_specs.append(pl.BlockSpec((None, tm, width), lambda bi, i: (bi, i, 0)))
            else:
                in_specs.append(pl.BlockSpec((None, dil, tm // dil, width),
                                             lambda bi, i: (bi, 0, i, 0)))
            args.append(arr)
    row = lambda bi, i: (bi, i, 0)
    in_specs += [pl.BlockSpec((None, tm, MIX_WIDTH), row),
                 pl.BlockSpec((LANES, MIX_WIDTH), lambda bi, i: (0, 0)),
                 pl.BlockSpec((MIX_WIDTH, d), lambda bi, i: (0, 0)),
                 pl.BlockSpec((None, tm, d), row)]
    args += [gate, expand, w_out.astype(BF16), x]
    return pl.pallas_call(
        functools.partial(_dil_combine_kernel, dils),
        grid=(b, s // tm),
        in_specs=in_specs,
        out_specs=pl.BlockSpec((None, tm, d), row),
        out_shape=jax.ShapeDtypeStruct((b, s, d), F32),
        scratch_shapes=[pltpu.VMEM((tm, LANES), F32), pltpu.VMEM((tm, LANES), F32)],
        compiler_params=_cparams(("arbitrary", "arbitrary")),
        name="dil_combine_out_proj",
    )(*args)


N_BIAS_PIECES = 3


def _fox_decay_kernel(f_ref, b_ref, kb_ref):
    s = f_ref.shape[0]
    x = f_ref[...] + b_ref[...]
    logf = jnp.minimum(x, 0.0) - jnp.log1p(jnp.exp(-jnp.abs(x)))
    r = lax.broadcasted_iota(jnp.int32, (BLK, BLK), 0)
    c = lax.broadcasted_iota(jnp.int32, (BLK, BLK), 1)
    tri = (c <= r).astype(F32)
    lane = lax.broadcasted_iota(jnp.int32, (BLK, LANES), 1)
    carry = jnp.zeros((1, LANES), F32)
    for i in range(s // BLK):
        blk = logf[i * BLK:(i + 1) * BLK, :]
        cs = jnp.dot(tri, blk, preferred_element_type=F32, precision=lax.Precision.HIGHEST) + carry
        carry = cs[BLK - 1:BLK, :]
        rest = cs * (-LOG2E)
        out = jnp.zeros((BLK, LANES), F32)
        for j in range(N_BIAS_PIECES):
            piece = rest.astype(BF16).astype(F32)
            rest = rest - piece
            moved = piece if j == 0 else pltpu.roll(piece, N_HEADS * j, 1)
            out = jnp.where((lane >= N_HEADS * j) & (lane < N_HEADS * (j + 1)), moved, out)
        kb_ref[i * BLK:(i + 1) * BLK, :] = out.astype(BF16)


def _fox_decay(f_logit, b_f):
    b, s, _ = f_logit.shape
    b_pad = jnp.pad(b_f.astype(F32), (0, LANES - N_HEADS))[None, :]
    return pl.pallas_call(
        _fox_decay_kernel,
        grid=(b,),
        in_specs=[pl.BlockSpec((None, s, LANES), lambda bi: (bi, 0, 0)),
                  pl.BlockSpec((1, LANES), lambda bi: (0, 0))],
        out_specs=pl.BlockSpec((None, s, LANES), lambda bi: (bi, 0, 0)),
        out_shape=jax.ShapeDtypeStruct((b, s, LANES), BF16),
        compiler_params=_cparams(("arbitrary",)),
        name="fox_decay",
    )(f_logit, b_pad)


FOX_T = PROJ_ROWS
FOX_HEADS_PER_STEP = 4


def _fox_kernel(qt_ref, k_ref, kb_ref, vt_ref, gate_ref, o_ref, m_sc, acc_sc, st_sc):
    pi = pl.program_id(1)
    i = pl.program_id(2)
    t = FOX_T
    heads = range(FOX_HEADS_PER_STEP)
    row = lax.broadcasted_iota(jnp.int32, (LANES, t), 0)
    zeros = jnp.zeros((HEAD_DIM, t), BF16)
    qx = []
    for e in heads:
        h = FOX_HEADS_PER_STEP * pi + e
        pick = (row % N_HEADS == h) & (row < N_HEADS * N_BIAS_PIECES)
        sel = jnp.where(pick, 1.0, 0.0).astype(BF16)
        mine = qt_ref[0, e * HEAD_DIM:(e + 1) * HEAD_DIM, :]
        top = [mine, zeros] if e % 2 == 0 else [zeros, mine]
        qx.append(jnp.concatenate(top + [sel], axis=0))
        m_sc[e] = jnp.full((1, t), MASK_VALUE, F32)
        acc_sc[e] = jnp.zeros((ACC_ROWS, t), F32)

    tk = t // 2

    def scores(jt, hf, e):
        k0 = pl.multiple_of(jt * t + hf * tk, tk)
        pair = slice((e // 2) * LANES, (e // 2 + 1) * LANES)
        kx = jnp.concatenate([k_ref[pl.ds(k0, tk), pair], kb_ref[pl.ds(k0, tk), :]], axis=1)
        st_sc[e, hf] = _dot(kx, qx[e])

    def consume(jt, hf, e, diag):
        st = st_sc[e, hf]
        if diag:
            ok = (lax.broadcasted_iota(jnp.int32, (tk, t), 0) + hf * tk
                  <= lax.broadcasted_iota(jnp.int32, (tk, t), 1))
            st = jnp.where(ok, st, MASK_VALUE)
        m_old = m_sc[e]
        m_new = jnp.maximum(m_old, jnp.max(st, axis=0, keepdims=True))
        pt = jnp.exp2(st - m_new).astype(BF16)
        alpha = jnp.exp2(m_old - m_new)
        vt = vt_ref[jt, e * HEAD_DIM:(e + 1) * HEAD_DIM, hf * tk:(hf + 1) * tk]
        acc_sc[e] = alpha * acc_sc[e] + _dot(_with_ones(vt), pt)
        m_sc[e] = m_new

    for e in heads:
        scores(0, 0, e)

    def body(jt):
        for e in heads:
            scores(jt, 1, e)
            consume(jt, 0, e, False)
        for e in heads:
            scores(jt + 1, 0, e)
            consume(jt, 1, e, False)

    _loop_in_pairs(i, body)
    for e in heads:
        scores(i, 1, e)
        consume(i, 0, e, True)
    for e in heads:
        consume(i, 1, e, True)
    for p in range(FOX_HEADS_PER_STEP // 2):
        outs = []
        for e in (2 * p, 2 * p + 1):
            acc = acc_sc[e]
            outs.append(acc[:HEAD_DIM] / jnp.maximum(acc[HEAD_DIM:HEAD_DIM + 1], 1e-30))
        sl = slice(p * LANES, (p + 1) * LANES)
        out = jnp.concatenate(outs, axis=0).T
        o_ref[:, sl] = (out * _silu(gate_ref[:, sl].astype(F32))).astype(o_ref.dtype)


def _fox_attention(qt, k, kb, vt, gate):
    b, s, _ = k.shape
    t = FOX_T
    assert s % t == 0 and vt.shape == (b, s // t, MIX_WIDTH, t) and qt.shape == vt.shape
    tile = lambda bi, p, i: (bi, i, p)
    nh = FOX_HEADS_PER_STEP
    wd = nh * HEAD_DIM
    return pl.pallas_call(
        _fox_kernel,
        grid=(b, N_HEADS // nh, s // t),
        in_specs=[pl.BlockSpec((None, 1, wd, t), lambda bi, p, i: (bi, i, p, 0)),
                  pl.BlockSpec((None, s, wd), lambda bi, p, i: (bi, 0, p)),
                  pl.BlockSpec((None, s, LANES), lambda bi, p, i: (bi, 0, 0)),
                  pl.BlockSpec((None, s // t, wd, t), lambda bi, p, i: (bi, 0, p, 0)),
                  pl.BlockSpec((None, t, wd), tile)],
        out_specs=pl.BlockSpec((None, t, wd), tile),
        out_shape=jax.ShapeDtypeStruct((b, s, MIX_WIDTH), BF16),
        scratch_shapes=[pltpu.VMEM((nh, 1, t), F32), pltpu.VMEM((nh, ACC_ROWS, t), F32),
                        pltpu.VMEM((nh, 2, t // 2, t), F32)],
        compiler_params=_cparams(("arbitrary", "arbitrary", "arbitrary")),
        name="fox_attn",
    )(qt, k, kb, vt, gate)


def _gelu_tanh(x):
    return 0.5 * x * (1.0 + jnp.tanh(math.sqrt(2.0 / math.pi) * (x + 0.044715 * (x * x * x))))


def _nsa_compress_kernel(ak_ref, av_ref, pe_ref, wa_ref, wb_ref, w2_ref, w2t_ref, c_ref, s_ref,
                         kc_ref, vct_ref):
    nrow = ak_ref.shape[0]
    for idx, a_ref in enumerate((ak_ref, av_ref)):
        a = a_ref[...].astype(F32)
        xa = (a + pe_ref[idx, 0:1, :]).astype(BF16)
        xb = (a + pe_ref[idx, 1:2, :]).astype(BF16)
        ya = _dot(xa, wa_ref[idx])
        yb = _dot(xb, wb_ref[idx])
        hid = _gelu_tanh(ya + pltpu.roll(yb, nrow - 1, 0)).astype(BF16)
        if idx == 0:
            y = _dot(hid, w2_ref[...])
            kc_ref[...] = _apply_rope(y, c_ref[...], s_ref[...]).astype(kc_ref.dtype)
        else:
            vct_ref[...] = _dot_nt(w2t_ref[...], hid).astype(vct_ref.dtype)


def _nsa_compress(kc, vc, pe_k, w1_k, w2_k, pe_v, w1_v, w2_v, rope_c, rope_s):
    b, s, _ = kc.shape
    ns = s // CMP_STRIDE
    g = NSA_KV
    flat = CMP_STRIDE * g * HEAD_DIM

    def w1_halves(w1):
        w1r = w1.reshape(2, CMP_STRIDE, HEAD_DIM, CMP_HIDDEN)
        outs = []
        for hf in range(2):
            z = jnp.einsum('ldc,gh->lgdhc', w1r[hf], jnp.eye(g, dtype=F32))
            outs.append(z.reshape(flat, g * CMP_HIDDEN))
        return outs

    def pe_halves(pe):
        per = pe.reshape(2, CMP_STRIDE, 1, HEAD_DIM)
        return jnp.broadcast_to(per, (2, CMP_STRIDE, g, HEAD_DIM)).reshape(2, flat)

    def w2_bd(w2):
        z = jnp.einsum('cd,gh->gchd', w2, jnp.eye(g, dtype=F32))
        return z.reshape(g * CMP_HIDDEN, g * HEAD_DIM)

    ka, kb = w1_halves(w1_k)
    va, vb = w1_halves(w1_v)
    wa = jnp.stack([ka, va]).astype(BF16)
    wb = jnp.stack([kb, vb]).astype(BF16)
    w2 = w2_bd(w2_k).astype(BF16)
    w2t = w2_bd(w2_v).T.astype(BF16)
    pe = jnp.stack([pe_halves(pe_k), pe_halves(pe_v)]).astype(F32)
    whole = lambda bi: (0, 0, 0)
    per_b = lambda bi: (bi, 0, 0)
    return pl.pallas_call(
        _nsa_compress_kernel,
        grid=(b,),
        in_specs=[pl.BlockSpec((None, ns, flat), per_b),
                  pl.BlockSpec((None, ns, flat), per_b),
                  pl.BlockSpec((2, 2, flat), whole),
                  pl.BlockSpec((2, flat, g * CMP_HIDDEN), whole),
                  pl.BlockSpec((2, flat, g * CMP_HIDDEN), whole),
                  pl.BlockSpec((g * CMP_HIDDEN, LANES), lambda bi: (0, 0)),
                  pl.BlockSpec((LANES, g * CMP_HIDDEN), lambda bi: (0, 0)),
                  pl.BlockSpec((None, ns, LANES), per_b),
                  pl.BlockSpec((None, ns, LANES), per_b)],
        out_specs=[pl.BlockSpec((None, ns, LANES), per_b),
                   pl.BlockSpec((None, LANES, ns), per_b)],
        out_shape=[jax.ShapeDtypeStruct((b, ns, LANES), BF16),
                   jax.ShapeDtypeStruct((b, LANES, ns), BF16)],
        compiler_params=_cparams(("arbitrary",)),
        name="nsa_compress",
    )(kc.reshape(b, ns, flat), vc.reshape(b, ns, flat), pe, wa, wb, w2, w2t,
      rope_c.reshape(b, ns, LANES), rope_s.reshape(b, ns, LANES))


NSA_TQ = 128
NSA_TK = 256
NSA_REP = N_HEADS // NSA_KV


def _nsa_kernel(n_cmp, n_sel, n_win, *refs):
    q_ref, kc_ref, vct_ref, ks_ref, vst_ref = refs[:5]
    kw_refs = refs[5:5 + n_win]
    vwt_refs = refs[5 + n_win:5 + 2 * n_win]
    gl_ref, gate_ref, ovt_ref, blk_ref, o_ref = refs[5 + 2 * n_win:10 + 2 * n_win]
    st_sc, stc_sc, stw_sc, m_sc, acc_sc, ocmp_sc, owin_sc = refs[10 + 2 * n_win:]

    n = pl.program_id(1)
    tq, tk, rep = NSA_TQ, NSA_TK, NSA_REP
    ncp = kc_ref.shape[0]
    t_lane = n * tq + lax.broadcasted_iota(jnp.int32, (1, tq), 1)
    qg = [_group_queries(q_ref, g, rep) for g in range(NSA_KV)]

    ci = lax.broadcasted_iota(jnp.int32, (ncp, tq), 0)
    cmask = (ci * CMP_STRIDE + (CMP_LEN - 1) <= t_lane) & (ci < n_cmp)
    kwn = n_win * BLK
    krow = lax.broadcasted_iota(jnp.int32, (kwn, tq), 0) - (n_win - 1) * BLK
    dist = lax.broadcasted_iota(jnp.int32, (kwn, tq), 1) - krow
    wok = (dist >= 0) & (dist <= NSA_WINDOW - 1) & (n * BLK + krow >= 0)
    sel_rows = ovt_ref.shape[0]
    rowi = lax.broadcasted_iota(jnp.int32, (sel_rows, tq), 0)
    rowf = rowi.astype(F32)
    cur = t_lane // SEL_LEN
    forced = (rowi == 0) | (rowi == cur)
    causal = rowi <= cur

    def cmp_scores(g):
        stc_sc[g] = _dot_nt(kc_ref[...], qg[g])

    def cmp_consume(g):
        ps = []
        for r in range(rep):
            s_r = jnp.where(cmask, stc_sc[g, :, r * tq:(r + 1) * tq], MASK_VALUE)
            m = jnp.max(s_r, axis=0, keepdims=True)
            pr = jnp.where(cmask, jnp.exp2(s_r - m), 0.0)
            ps.append(pr.astype(BF16))
        lhs = jnp.concatenate([_with_ones(vct_ref[g * HEAD_DIM:(g + 1) * HEAD_DIM, :]),
                               ovt_ref[...]], axis=0)
        res = _dot(lhs, jnp.concatenate(ps, axis=1))
        inv = 1.0 / jnp.maximum(res[HEAD_DIM:HEAD_DIM + 1], 1e-30)
        ocmp_sc[g] = res[:HEAD_DIM] * inv
        imp = None
        for r in range(rep):
            ql = slice(r * tq, (r + 1) * tq)
            t = res[ACC_ROWS:, ql] * inv[:, ql]
            imp = t if imp is None else imp + t
        return imp

    def select(g, imp):
        score = jnp.where(causal, jnp.where(forced, FORCED_SCORE, imp), MASK_VALUE)
        score = jnp.where(rowi < n_sel, score, PAD_SCORE)
        chosen = jnp.zeros((sel_rows, tq), F32)
        for _ in range(min(SEL_TOPK, n_sel)):
            mx = jnp.max(score, axis=0, keepdims=True)
            first = jnp.min(jnp.where(score == mx, rowf, float(LANES)), axis=0, keepdims=True)
            hit = rowf == first
            chosen = jnp.where(hit, 1.0, chosen)
            score = jnp.where(hit, PAD_SCORE, score)
        bias_t = jnp.where(causal & (chosen > 0.5), 0.0, MASK_VALUE)
        bias_t = jnp.concatenate([bias_t, jnp.zeros((LANES - sel_rows, tq), F32)], axis=0)
        bias = bias_t.T.astype(BF16)
        return jnp.concatenate([qg[g], jnp.concatenate([bias] * rep, axis=0)], axis=1)

    def win_scores(g):
        kwcat = jnp.concatenate([kr[...] for kr in kw_refs], axis=0)
        stw_sc[g] = _dot_nt(kwcat, qg[g])

    def win_consume(g):
        vwt = jnp.concatenate([vr[0] for vr in vwt_refs], axis=1)
        pts = []
        for r in range(rep):
            s_r = jnp.where(wok, stw_sc[g, :, r * tq:(r + 1) * tq], MASK_VALUE)
            m = jnp.max(s_r, axis=0, keepdims=True)
            pts.append(jnp.exp2(s_r - m).astype(BF16))
        owin_sc[g] = _dot(_with_ones(vwt[g * HEAD_DIM:(g + 1) * HEAD_DIM, :]),
                          jnp.concatenate(pts, axis=1))

    cmp_scores(0)
    cmp_scores(1)
    psum0 = cmp_consume(0)
    win_scores(0)
    psum1 = cmp_consume(1)
    win_scores(1)
    qsel = [select(0, psum0)]
    win_consume(0)
    qsel.append(select(1, psum1))
    win_consume(1)

    for g in range(NSA_KV):
        m_sc[g] = jnp.full((1, rep * tq), MASK_VALUE, F32)
        acc_sc[g] = jnp.zeros((ACC_ROWS, rep * tq), F32)

    hrep = rep // 2
    wide = hrep * tq
    units = [(g, hf) for g in range(NSA_KV) for hf in range(2)]

    def scores(c, u):
        g, hf = units[u]
        k0 = pl.multiple_of(c * tk, tk)
        kx = jnp.concatenate([ks_ref[pl.ds(k0, tk), :], blk_ref[pl.ds(k0, tk), :]], axis=1)
        st_sc[u] = _dot_nt(kx, qsel[g][hf * wide:(hf + 1) * wide, :])

    def consume(c, u, last):
        g, hf = units[u]
        lanes = slice(hf * wide, (hf + 1) * wide)
        if last:
            tok_ok = c * tk + lax.broadcasted_iota(jnp.int32, (tk, tq), 0) <= t_lane
        m_old = m_sc[g, :, lanes]
        ms, pts = [], []
        for r in range(hrep):
            s_r = st_sc[u, :, r * tq:(r + 1) * tq]
            if last:
                s_r = jnp.where(tok_ok, s_r, MASK_VALUE)
            m_new = jnp.maximum(m_old[:, r * tq:(r + 1) * tq], jnp.max(s_r, axis=0, keepdims=True))
            pts.append(jnp.exp2(s_r - m_new).astype(BF16))
            ms.append(m_new)
        m_new = jnp.concatenate(ms, axis=1)
        alpha = jnp.exp2(m_old - m_new)
        vt = vst_ref[c, g * HEAD_DIM:(g + 1) * HEAD_DIM, :]
        acc_sc[g, :, lanes] = (alpha * acc_sc[g, :, lanes]
                               + _dot(_with_ones(vt), jnp.concatenate(pts, axis=1)))
        m_sc[g, :, lanes] = m_new

    n_steps = (n * tq + tq + tk - 1) // tk
    scores(0, 0)
    scores(0, 1)

    def body(c):
        scores(c, 2)
        consume(c, 0, False)
        scores(c, 3)
        consume(c, 1, False)
        scores(c + 1, 0)
        consume(c, 2, False)
        scores(c + 1, 1)
        consume(c, 3, False)

    _loop_in_pairs(n_steps - 1, body)
    scores(n_steps - 1, 2)
    consume(n_steps - 1, 0, True)
    scores(n_steps - 1, 3)
    consume(n_steps - 1, 1, True)
    consume(n_steps - 1, 2, True)
    consume(n_steps - 1, 3, True)

    gate_t = (1.0 / (1.0 + jnp.exp(-gl_ref[...]))).T
    for p in range(N_PAIRS):
        sl = slice(p * LANES, (p + 1) * LANES)
        g = (2 * p) // rep
        rows = []
        for e in range(2):
            h = 2 * p + e
            ql = slice((h - g * rep) * tq, (h - g * rep + 1) * tq)
            slc, win = acc_sc[g][:, ql], owin_sc[g][:, ql]
            branches = (ocmp_sc[g][:, ql],
                        slc[:HEAD_DIM] / jnp.maximum(slc[HEAD_DIM:HEAD_DIM + 1], 1e-30),
                        win[:HEAD_DIM] / win[HEAD_DIM:HEAD_DIM + 1])
            out = None
            for j, br in enumerate(branches):
                t = gate_t[3 * h + j:3 * h + j + 1, :] * br
                out = t if out is None else out + t
            rows.append(out)
        pair = jnp.concatenate(rows, axis=0).T
        o_ref[:, sl] = (pair * _silu(gate_ref[:, sl].astype(F32))).astype(o_ref.dtype)


def _selection_overlap_t(n_cmp_pad, n_cmp, n_sel):
    cs = np.arange(n_cmp_pad) * CMP_STRIDE
    js = np.arange(LANES) * SEL_LEN
    ov = np.minimum(cs[None, :] + CMP_LEN, js[:, None] + SEL_LEN) - np.maximum(cs[None, :], js[:, None])
    ov = (np.clip(ov, 0, None) / CMP_LEN).astype(np.float32)
    ov[:, n_cmp:] = 0.0
    ov[n_sel:, :] = 0.0
    return ov


def _nsa_attention(q, k_cmp, v_cmp_t, ks, vs_t, kw, vw_t, g_logit, gate):
    b, s, _ = q.shape
    tq = NSA_TQ
    ncp = k_cmp.shape[1]
    n_cmp = s // CMP_STRIDE - 1
    n_sel = s // SEL_LEN
    n_win = -(-(NSA_WINDOW - 1) // BLK) + 1
    assert n_sel <= LANES and s % NSA_TK == 0 and tq == BLK
    sel_rows = -(-n_sel // 16) * 16
    ov_t = jnp.asarray(_selection_overlap_t(ncp, n_cmp, n_sel)[:sel_rows], BF16)
    row = lambda bi, i: (bi, i, 0)
    per_b = lambda bi, i: (bi, 0, 0)
    in_specs = [pl.BlockSpec((None, tq, MIX_WIDTH), row),
                pl.BlockSpec((None, ncp, LANES), per_b),
                pl.BlockSpec((None, LANES, ncp), per_b),
                pl.BlockSpec((None, s, LANES), per_b),
                pl.BlockSpec((None, s // NSA_TK, LANES, NSA_TK), lambda bi, i: (bi, 0, 0, 0))]
    args = [q, k_cmp, v_cmp_t, ks, vs_t]
    for j in range(n_win - 1, -1, -1):
        in_specs.append(pl.BlockSpec((None, BLK, LANES),
                                     lambda bi, i, j=j: (bi, jnp.maximum(i - j, 0), 0)))
        args.append(kw)
    for j in range(n_win - 1, -1, -1):
        in_specs.append(pl.BlockSpec((None, 1, LANES, BLK),
                                     lambda bi, i, j=j: (bi, jnp.maximum(i - j, 0), 0, 0)))
        args.append(vw_t)
    block_onehot = jnp.asarray(
        (np.arange(s)[:, None] // SEL_LEN == np.arange(LANES)[None, :]).astype(np.float32), BF16)
    in_specs += [pl.BlockSpec((None, tq, LANES), row),
                 pl.BlockSpec((None, tq, MIX_WIDTH), row),
                 pl.BlockSpec((sel_rows, ncp), lambda bi, i: (0, 0)),
                 pl.BlockSpec((s, LANES), lambda bi, i: (0, 0))]
    args += [g_logit, gate, ov_t, block_onehot]
    wide = NSA_REP * tq
    return pl.pallas_call(
        functools.partial(_nsa_kernel, n_cmp, n_sel, n_win),
        grid=(b, s // tq),
        in_specs=in_specs,
        out_specs=pl.BlockSpec((None, tq, MIX_WIDTH), row),
        out_shape=jax.ShapeDtypeStruct((b, s, MIX_WIDTH), BF16),
        scratch_shapes=[pltpu.VMEM((2 * NSA_KV, NSA_TK, wide // 2), F32),
                        pltpu.VMEM((NSA_KV, ncp, wide), F32),
                        pltpu.VMEM((NSA_KV, n_win * BLK, wide), F32),
                        pltpu.VMEM((NSA_KV, 1, wide), F32),
                        pltpu.VMEM((NSA_KV, ACC_ROWS, wide), F32),
                        pltpu.VMEM((NSA_KV, HEAD_DIM, wide), F32),
                        pltpu.VMEM((NSA_KV, ACC_ROWS, wide), F32)],
        compiler_params=_cparams(("arbitrary", "arbitrary")),
        name="nsa_attn",
    )(*args)


def _split(w, sizes):
    offs = np.cumsum([0] + list(sizes))
    return [w[:, int(offs[i]):int(offs[i + 1])] for i in range(len(sizes))]


def _swa_layer(x, rope, gain, w_in, sinks, w_out):
    kvw = SWA_KV * HEAD_DIM
    parts = _split(w_in, [MIX_WIDTH, kvw, kvw, MIX_WIDTH])
    segs = [_Seg(MIX_WIDTH, rope=True, scale=Q_SCALE), _Seg(kvw, rope=True), _Seg(kvw),
            _Seg(MIX_WIDTH)]
    q, k, v, gate = _norm_proj(x, gain, parts, segs, rope)
    o = _banded_attention(q, k, v, SWA_WINDOW - 1, sinks=sinks, gate=gate)
    return o, w_out


def _dilated_layer(x, rope, gain, w_in, w_out):
    b, s, _ = x.shape
    kvw = DIL_KV * HEAD_DIM
    sizes, segs = [], []
    for window, dil in DIL_PATTERNS:
        assert s % (dil * BLK) == 0
        sizes += [MIX_WIDTH, kvw, kvw]
        segs += [_Seg(MIX_WIDTH, rope=True, scale=Q_SCALE, dil=dil), _Seg(kvw, rope=True, dil=dil),
                 _Seg(kvw, dil=dil)]
    sizes.append(MIX_WIDTH)
    segs.append(_Seg(MIX_WIDTH))
    res = _norm_proj(x, gain, _split(w_in, sizes), segs, rope)
    gate = res[-1]
    os_, lses, dils = [], [], []
    for gi, (window, dil) in enumerate(DIL_PATTERNS):
        q, k, v = res[3 * gi:3 * gi + 3]
        if dil > 1:
            q, k, v = (t.reshape(b * dil, s // dil, t.shape[-1]) for t in (q, k, v))
        o, lse = _banded_attention(q, k, v, window // dil, want_lse=True)
        if dil > 1:
            o = o.reshape(b, dil, s // dil, MIX_WIDTH)
            lse = lse.reshape(b, dil, s // dil, LANES)
        os_.append(o)
        lses.append(lse)
        dils.append(dil)
    return _dil_combine_out_proj(os_, lses, gate, tuple(dils), w_out, x)


def _fox_layer(x, gain, w_in, b_f, w_out):
    sizes = [MIX_WIDTH, MIX_WIDTH, MIX_WIDTH, N_HEADS, MIX_WIDTH]
    segs = [_Seg(MIX_WIDTH, scale=Q_SCALE, tile=FOX_T), _Seg(MIX_WIDTH),
            _Seg(MIX_WIDTH, tile=FOX_T), _Seg(LANES, dtype=F32), _Seg(MIX_WIDTH)]
    qt, k, vt, f_logit, gate = _norm_proj(x, gain, _split(w_in, sizes), segs)
    kb = _fox_decay(f_logit, b_f)
    return _fox_attention(qt, k, kb, vt, gate), w_out


def _nsa_layer(x, positions, rope, gain, w_in, pe_k, w1_k, w2_k, pe_v, w1_v, w2_v, w_out):
    b, s, _ = x.shape
    kvw = NSA_KV * HEAD_DIM
    sizes = [MIX_WIDTH] + [kvw] * 6 + [3 * N_HEADS, MIX_WIDTH]
    segs = [_Seg(MIX_WIDTH, rope=True, scale=Q_SCALE), _Seg(kvw), _Seg(kvw), _Seg(kvw, rope=True),
            _Seg(kvw, tile=NSA_TK), _Seg(kvw, rope=True), _Seg(kvw, tile=BLK),
            _Seg(LANES, dtype=F32), _Seg(MIX_WIDTH)]
    q, kc, vc, ks, vs_t, kw, vw_t, g_logit, gate = _norm_proj(
        x, gain, _split(w_in, sizes), segs, rope)
    ns = s // CMP_STRIDE
    cmp_pos = jnp.concatenate(
        [positions[:, CMP_LEN - 1::CMP_STRIDE], positions[:, -1:]], axis=1)[:, :ns]
    cmp_c, cmp_s = _rope_tables(cmp_pos.reshape(-1))
    k_cmp, v_cmp_t = _nsa_compress(kc, vc, pe_k, w1_k, w2_k, pe_v, w1_v, w2_v, cmp_c, cmp_s)
    return _nsa_attention(q, k_cmp, v_cmp_t, ks, vs_t, kw, vw_t, g_logit, gate), w_out


def kernel(x, positions, norm_0, w_in_0, sinks_0, w_out_0, norm_1, w_in_1, w_out_1, norm_2, w_in_2, b_f_2, w_out_2, norm_3, w_in_3, cmp_pe_k_3, cmp_w1_k_3, cmp_w2_k_3, cmp_pe_v_3, cmp_w1_v_3, cmp_w2_v_3, w_out_3, final_norm):
    rope = _rope_tables(positions.reshape(-1))
    o, w = _swa_layer(x, rope, norm_0, w_in_0, sinks_0, w_out_0)
    x = _out_proj(o, w, x)
    x = _dilated_layer(x, rope, norm_1, w_in_1, w_out_1)
    o, w = _fox_layer(x, norm_2, w_in_2, b_f_2, w_out_2)
    x = _out_proj(o, w, x)
    o, w = _nsa_layer(x, positions, rope, norm_3, w_in_3, cmp_pe_k_3, cmp_w1_k_3, cmp_w2_k_3,
                      cmp_pe_v_3, cmp_w1_v_3, cmp_w2_v_3, w_out_3)
    return _out_proj(o, w, x, final_gain=final_norm)
```

```python
import functools
import math

import numpy as np
import jax
import jax.numpy as jnp
from jax import lax
from jax.experimental import pallas as pl
from jax.experimental.pallas import tpu as pltpu

HEAD_DIM = 64
N_HEADS = 16
N_PAIRS = N_HEADS // 2
MIX_WIDTH = N_HEADS * HEAD_DIM
ROT_DIM = HEAD_DIM // 4
ROT_HALF = ROT_DIM // 2
ROPE_THETA = 500000.0
BLK = 128
LANES = 128
NORM_EPS = 1e-6
MASK_VALUE = -1e30
PAD_SCORE = -3e38
LOG2E = math.log2(math.e)
Q_SCALE = HEAD_DIM ** -0.5 * LOG2E

SWA_KV = 4
SWA_WINDOW = 128
DIL_KV = 4
DIL_PATTERNS = ((128, 1), (512, 4), (2048, 16))
NSA_KV = 2
CMP_LEN = 32
CMP_STRIDE = 16
CMP_HIDDEN = 256
SEL_LEN = 64
SEL_TOPK = 8
NSA_WINDOW = 256
FORCED_SCORE = 1e4

VMEM_LIMIT_BYTES = 56 * 1024 * 1024
PROJ_ROWS = 512
PROJ_COLS = 512
ACC_ROWS = HEAD_DIM + 16
F32 = jnp.float32
BF16 = jnp.bfloat16


def _cparams(sem):
    return pltpu.CompilerParams(dimension_semantics=sem, vmem_limit_bytes=VMEM_LIMIT_BYTES)


def _lane_half(shape):
    return lax.broadcasted_iota(jnp.int32, shape, 1) // HEAD_DIM


def _swap_halves(t):
    return jnp.concatenate([t[:, HEAD_DIM:], t[:, :HEAD_DIM]], axis=1)


def _head_query(qp, e, kv_half):
    qh = jnp.where(_lane_half(qp.shape) == e, qp, jnp.zeros_like(qp))
    if e != kv_half:
        qh = _swap_halves(qh)
    return qh


def _group_queries(q_ref, g, rep):
    tiles = []
    for r in range(rep):
        h = g * rep + r
        tiles.append(_head_query(q_ref[:, (h // 2) * LANES:(h // 2 + 1) * LANES], h % 2, g % 2))
    return jnp.concatenate(tiles, axis=0)


def _dot_nt(a, b):
    return lax.dot_general(a, b, (((1,), (1,)), ((), ())), preferred_element_type=F32)


def _dot(a, b):
    return jnp.dot(a, b, preferred_element_type=F32)


def _silu(x):
    return x * (1.0 / (1.0 + jnp.exp(-x)))


def _loop_in_pairs(count, body):
    odd = count % 2

    @pl.when(odd == 1)
    def _():
        body(0)

    def two(jj, carry):
        body(odd + 2 * jj)
        body(odd + 2 * jj + 1)
        return carry

    lax.fori_loop(0, count // 2, two, 0)


def _with_ones(vt):
    return jnp.concatenate([vt, jnp.ones((ACC_ROWS - HEAD_DIM, vt.shape[1]), BF16)], axis=0)


def _rope_table_kernel(pos_ref, inv_ref, c_ref, s_ref):
    pos = pos_ref[...].astype(F32)
    ang = pos * inv_ref[...]
    d = lax.broadcasted_iota(jnp.int32, ang.shape, 1) % HEAD_DIM
    cos = jnp.cos(ang)
    sin = jnp.sin(ang)
    c_ref[...] = jnp.where(d < ROT_DIM, cos, 1.0)
    s_ref[...] = jnp.where(d < ROT_HALF, -sin, jnp.where(d < ROT_DIM, sin, 0.0))


def _rope_tables(pos_flat):
    t = pos_flat.shape[0]
    rows = min(t, 2048)
    assert t % rows == 0
    inv = jnp.power(ROPE_THETA, -jnp.arange(ROT_HALF, dtype=F32) / ROT_HALF)
    inv_l = jnp.tile(inv, LANES // ROT_HALF)[None, :]
    out = jax.ShapeDtypeStruct((t, LANES), F32)
    return pl.pallas_call(
        _rope_table_kernel,
        grid=(t // rows,),
        in_specs=[pl.BlockSpec((rows, 1), lambda i: (i, 0)),
                  pl.BlockSpec((1, LANES), lambda i: (0, 0))],
        out_specs=[pl.BlockSpec((rows, LANES), lambda i: (i, 0))] * 2,
        out_shape=[out, out],
        compiler_params=_cparams(("arbitrary",)),
        name="rope_tables",
    )(pos_flat[:, None], inv_l)


def _apply_rope(y, c, s):
    outs = []
    for j in range(y.shape[1] // LANES):
        t = y[:, j * LANES:(j + 1) * LANES]
        d = lax.broadcasted_iota(jnp.int32, t.shape, 1) % HEAD_DIM
        partner = jnp.where(d < ROT_HALF, pltpu.roll(t, LANES - ROT_HALF, 1),
                            pltpu.roll(t, ROT_HALF, 1))
        outs.append(t * c + partner * s)
    return outs[0] if len(outs) == 1 else jnp.concatenate(outs, axis=1)


class _Seg:
    def __init__(self, width, rope=False, scale=None, dtype=BF16, dil=1, tile=None):
        self.width, self.rope, self.scale, self.dtype, self.dil = width, rope, scale, dtype, dil
        self.tile = tile
        assert not (tile and (rope or dil > 1))


def _norm_proj_kernel(segs, use_rope, *refs):
    has_t = any(sg.tile for sg in segs)
    x_ref, g_ref, w_ref = refs[:3]
    k = 3
    if has_t:
        wt_ref = refs[k]
        k += 1
    if use_rope:
        c_ref, s_ref = refs[k:k + 2]
        k += 2
    out_refs = refs[k:k + len(segs)]
    stage_ref = refs[k + len(segs)] if any(sg.dil > 1 for sg in segs) else None

    x = x_ref[...]
    var = jnp.mean(x * x, axis=-1, keepdims=True)
    h = (x * lax.rsqrt(var + NORM_EPS) * g_ref[...]).astype(BF16)
    rows = x.shape[0]
    col = 0
    tcol = 0
    for sg, o_ref in zip(segs, out_refs):
        if sg.tile:
            for c0 in range(0, sg.width, PROJ_COLS):
                cw = min(PROJ_COLS, sg.width - c0)
                yt = _dot_nt(wt_ref[tcol + c0:tcol + c0 + cw, :], h)
                if sg.scale is not None:
                    yt = yt * sg.scale
                for ti in range(rows // sg.tile):
                    o_ref[ti, c0:c0 + cw, :] = yt[:, ti * sg.tile:(ti + 1) * sg.tile].astype(sg.dtype)
            tcol += sg.width
            continue
        for c0 in range(0, sg.width, PROJ_COLS):
            cw = min(PROJ_COLS, sg.width - c0)
            y = _dot(h, w_ref[:, col + c0:col + c0 + cw])
            if sg.rope:
                y = _apply_rope(y, c_ref[...], s_ref[...])
            if sg.scale is not None:
                y = y * sg.scale
            if sg.dil > 1:
                sub = rows // sg.dil
                for j in range(cw // LANES):
                    stage_ref[j] = y[:, j * LANES:(j + 1) * LANES]
                for r in range(sg.dil):
                    for j in range(cw // LANES):
                        lo = c0 + j * LANES
                        o_ref[r, :, lo:lo + LANES] = (
                            stage_ref[j, pl.ds(r, sub, stride=sg.dil), :].astype(sg.dtype))
            else:
                o_ref[:, c0:c0 + cw] = y.astype(sg.dtype)
        col += sg.width


def _norm_proj(x, gain, w_parts, segs, rope=None):
    b, s, d = x.shape
    tm = PROJ_ROWS
    assert s % tm == 0
    w_cols, wt_rows = [], []
    for wp, sg in zip(w_parts, segs):
        if wp.shape[1] < sg.width:
            wp = jnp.pad(wp, ((0, 0), (0, sg.width - wp.shape[1])))
        if sg.tile:
            wt_rows.append(wp.T)
        else:
            w_cols.append(wp)
    w = jnp.concatenate(w_cols, axis=1).astype(BF16)
    n = w.shape[1]
    use_rope = rope is not None
    in_specs = [pl.BlockSpec((None, tm, d), lambda bi, i: (bi, i, 0)),
                pl.BlockSpec((1, d), lambda bi, i: (0, 0)),
                pl.BlockSpec((d, n), lambda bi, i: (0, 0))]
    args = [x, gain[None, :], w]
    if wt_rows:
        wt = jnp.concatenate(wt_rows, axis=0).astype(BF16)
        in_specs.append(pl.BlockSpec(wt.shape, lambda bi, i: (0, 0)))
        args.append(wt)
    if use_rope:
        in_specs += [pl.BlockSpec((None, tm, LANES), lambda bi, i: (bi, i, 0))] * 2
        args += [rope[0].reshape(b, s, LANES), rope[1].reshape(b, s, LANES)]
    out_specs, out_shape = [], []
    for sg in segs:
        if sg.dil > 1:
            assert tm % sg.dil == 0
            out_shape.append(jax.ShapeDtypeStruct((b, sg.dil, s // sg.dil, sg.width), sg.dtype))
            out_specs.append(pl.BlockSpec((None, sg.dil, tm // sg.dil, sg.width),
                                          lambda bi, i: (bi, 0, i, 0)))
        elif sg.tile:
            assert tm % sg.tile == 0
            out_shape.append(jax.ShapeDtypeStruct((b, s // sg.tile, sg.width, sg.tile), sg.dtype))
            out_specs.append(pl.BlockSpec((None, tm // sg.tile, sg.width, sg.tile),
                                          lambda bi, i: (bi, i, 0, 0)))
        else:
            out_shape.append(jax.ShapeDtypeStruct((b, s, sg.width), sg.dtype))
            out_specs.append(pl.BlockSpec((None, tm, sg.width), lambda bi, i: (bi, i, 0)))
    scratch = ([pltpu.VMEM((PROJ_COLS // LANES, tm, LANES), F32)]
               if any(sg.dil > 1 for sg in segs) else [])
    return pl.pallas_call(
        functools.partial(_norm_proj_kernel, segs, use_rope),
        grid=(b, s // tm),
        in_specs=in_specs,
        out_specs=out_specs,
        out_shape=out_shape,
        scratch_shapes=scratch,
        compiler_params=_cparams(("arbitrary", "arbitrary")),
        name="norm_proj",
    )(*args)


def _out_proj_kernel(final, *refs):
    if final:
        o_ref, w_ref, x_ref, g_ref, y_ref = refs
    else:
        o_ref, w_ref, x_ref, y_ref = refs
    y = x_ref[...] + _dot(o_ref[...], w_ref[...])
    if final:
        var = jnp.mean(y * y, axis=-1, keepdims=True)
        y = y * lax.rsqrt(var + NORM_EPS) * g_ref[...]
    y_ref[...] = y


def _out_proj(o, w, x, final_gain=None):
    b, s, d = x.shape
    m = o.shape[-1]
    tm = PROJ_ROWS
    final = final_gain is not None
    row = lambda bi, i: (bi, i, 0)
    in_specs = [pl.BlockSpec((None, tm, m), row),
                pl.BlockSpec((m, d), lambda bi, i: (0, 0)),
                pl.BlockSpec((None, tm, d), row)]
    args = [o, w.astype(BF16), x]
    if final:
        in_specs.append(pl.BlockSpec((1, d), lambda bi, i: (0, 0)))
        args.append(final_gain[None, :])
    return pl.pallas_call(
        functools.partial(_out_proj_kernel, final),
        grid=(b, s // tm),
        in_specs=in_specs,
        out_specs=pl.BlockSpec((None, tm, d), row),
        out_shape=jax.ShapeDtypeStruct((b, s, d), F32),
        compiler_params=_cparams(("arbitrary", "arbitrary")),
        name="out_proj",
    )(*args)


BANDED_MAX_QB = 2


def _banded_kernel(n_qb, max_dist, n_prev, n_kv, has_sink, has_gate, want_lse, *refs):
    q_ref = refs[0]
    n_blocks = n_qb + n_prev
    k_refs = refs[1:1 + n_blocks]
    v_refs = refs[1 + n_blocks:1 + 2 * n_blocks]
    k = 1 + 2 * n_blocks
    sink_ref = gate_ref = lse_ref = None
    if has_sink:
        sink_ref = refs[k]; k += 1
    if has_gate:
        gate_ref = refs[k]; k += 1
    o_ref = refs[k]; k += 1
    if want_lse:
        lse_ref = refs[k]; k += 1
    ot_sc = refs[k]
    st_sc = refs[k + 1]

    n = pl.program_id(1)
    tq = BLK
    kw = (n_prev + 1) * BLK
    rep = N_HEADS // n_kv
    krow = lax.broadcasted_iota(jnp.int32, (kw, tq), 0) - n_prev * BLK
    dist = lax.broadcasted_iota(jnp.int32, (kw, tq), 1) - krow
    band = (dist >= 0) & (dist <= max_dist)
    lse_rows = [[] for _ in range(n_qb)]
    units = [(qb, g) for qb in range(n_qb) for g in range(n_kv)]

    def scores(u):
        qb, g = units[u]
        sl = slice((g // 2) * LANES, (g // 2 + 1) * LANES)
        kcat = jnp.concatenate([kr[:, sl] for kr in k_refs[qb:qb + n_prev + 1]], axis=0)
        st_sc[u] = _dot_nt(kcat, _group_queries(q_ref.at[qb * tq:(qb + 1) * tq], g, rep))

    def consume(u):
        qb, g = units[u]
        sl = slice((g // 2) * LANES, (g // 2 + 1) * LANES)
        ok = band & ((n * n_qb + qb) * BLK + krow >= 0)
        vcat = jnp.concatenate([vr[:, sl] for vr in v_refs[qb:qb + n_prev + 1]], axis=0)
        vt = vcat.astype(F32).T.astype(BF16)
        ms, pts = [], []
        for r in range(rep):
            s_r = jnp.where(ok, st_sc[u, :, r * tq:(r + 1) * tq], MASK_VALUE)
            m = jnp.max(s_r, axis=0, keepdims=True)
            if has_sink:
                h = g * rep + r
                m = jnp.maximum(m, sink_ref[0:1, h:h + 1])
            pts.append(jnp.exp2(s_r - m).astype(BF16))
            ms.append(m)
        acc = _dot(_with_ones(vt[(g % 2) * HEAD_DIM:(g % 2 + 1) * HEAD_DIM, :]),
                   jnp.concatenate(pts, axis=1))
        for r in range(rep):
            h = g * rep + r
            den = acc[HEAD_DIM:HEAD_DIM + 1, r * tq:(r + 1) * tq]
            if has_sink:
                den = den + jnp.exp2(sink_ref[0:1, h:h + 1] - ms[r])
            ot_sc[qb * N_HEADS + h] = acc[:HEAD_DIM, r * tq:(r + 1) * tq] / den
            if want_lse:
                lse_rows[qb].append(ms[r] + jnp.log2(den))

    scores(0)
    scores(1)
    for u in range(len(units)):
        if u + 2 < len(units):
            scores(u + 2)
        consume(u)
    for qb in range(n_qb):
        rows = slice(qb * tq, (qb + 1) * tq)
        for p in range(N_PAIRS):
            sl = slice(p * LANES, (p + 1) * LANES)
            pair = jnp.concatenate([ot_sc[qb * N_HEADS + 2 * p],
                                    ot_sc[qb * N_HEADS + 2 * p + 1]], axis=0).T
            if has_gate:
                pair = pair * _silu(gate_ref[rows, sl].astype(F32))
            o_ref[rows, sl] = pair.astype(o_ref.dtype)
        if want_lse:
            lse_t = jnp.concatenate(
                lse_rows[qb] + [jnp.zeros((LANES - N_HEADS, tq), F32)], axis=0)
            lse_ref[rows, :] = lse_t.T


def _banded_attention(q, k, v, max_dist, sinks=None, gate=None, want_lse=False):
    bq, sq, _ = q.shape
    kvw = k.shape[-1]
    n_kv = kvw // HEAD_DIM
    n_prev = -(-max_dist // BLK)
    n_qb = BANDED_MAX_QB if sq % (BANDED_MAX_QB * BLK) == 0 else 1
    tq = n_qb * BLK
    assert sq % tq == 0
    row = lambda b, i: (b, i, 0)
    in_specs = [pl.BlockSpec((None, tq, MIX_WIDTH), row)]
    args = [q]
    for arr in (k, v):
        for j in range(n_prev, -n_qb, -1):
            in_specs.append(pl.BlockSpec(
                (None, BLK, kvw), lambda b, i, j=j: (b, jnp.maximum(n_qb * i - j, 0), 0)))
            args.append(arr)
    if sinks is not None:
        in_specs.append(pl.BlockSpec((1, LANES), lambda b, i: (0, 0)))
        args.append(jnp.pad(sinks.astype(F32) * LOG2E, (0, LANES - N_HEADS))[None, :])
    if gate is not None:
        in_specs.append(pl.BlockSpec((None, tq, MIX_WIDTH), row))
        args.append(gate)
    out_specs = [pl.BlockSpec((None, tq, MIX_WIDTH), row)]
    out_shape = [jax.ShapeDtypeStruct((bq, sq, MIX_WIDTH), BF16)]
    if want_lse:
        out_specs.append(pl.BlockSpec((None, tq, LANES), row))
        out_shape.append(jax.ShapeDtypeStruct((bq, sq, LANES), F32))
    res = pl.pallas_call(
        functools.partial(_banded_kernel, n_qb, max_dist, n_prev, n_kv, sinks is not None,
                          gate is not None, want_lse),
        grid=(bq, sq // tq),
        in_specs=in_specs,
        out_specs=out_specs,
        out_shape=out_shape,
        scratch_shapes=[pltpu.VMEM((n_qb * N_HEADS, HEAD_DIM, BLK), F32),
                        pltpu.VMEM((n_qb * n_kv, (n_prev + 1) * BLK,
                                    (N_HEADS // n_kv) * BLK), F32)],
        compiler_params=_cparams(("arbitrary", "arbitrary")),
        name="banded_attn",
    )(*args)
    return res if want_lse else res[0]


def _dil_combine_kernel(dils, *refs):
    ng = len(dils)
    o_refs, l_refs = refs[:ng], refs[ng:2 * ng]
    gate_ref, expand_ref, w_ref, x_ref, out_ref, stage_ref, lstage_ref = refs[2 * ng:]
    rows = out_ref.shape[0]

    def natural(ref, dil, stage, slab):
        sl = slice(slab * LANES, (slab + 1) * LANES)
        if dil == 1:
            return ref[:, sl].astype(F32)
        sub = rows // dil
        for r in range(dil):
            stage[pl.ds(r, sub, stride=dil), :] = ref[r, :, sl].astype(F32)
        return stage[...]

    lses = [natural(l_refs[i], dils[i], lstage_ref, 0) for i in range(ng)]
    mx = functools.reduce(jnp.maximum, lses)
    ws = [jnp.exp2(l - mx) for l in lses]
    tot = functools.reduce(lambda a, c: a + c, ws)
    inv = 1.0 / tot

    def widen(w):
        hi = w.astype(BF16)
        lo = (w - hi.astype(F32)).astype(BF16)
        return _dot(hi, expand_ref[...]) + _dot(lo, expand_ref[...])

    wide = [widen(w * inv) for w in ws[:-1]]
    wide.append(1.0 - functools.reduce(lambda a, c: a + c, wide))
    gated = []
    for p in range(N_PAIRS):
        sl = slice(p * LANES, (p + 1) * LANES)
        acc = None
        for gi in range(ng):
            t = wide[gi][:, sl] * natural(o_refs[gi], dils[gi], stage_ref, p)
            acc = t if acc is None else acc + t
        gated.append((acc * _silu(gate_ref[:, sl].astype(F32))).astype(BF16))
    out_ref[...] = x_ref[...] + _dot(jnp.concatenate(gated, axis=1), w_ref[...])


def _dil_combine_out_proj(os_, lses, gate, dils, w_out, x):
    b, s, d = x.shape
    tm = PROJ_ROWS
    expand = jnp.asarray((np.arange(LANES)[:, None] == np.arange(MIX_WIDTH)[None, :] // HEAD_DIM)
                         .astype(np.float32), BF16)
    in_specs, args = [], []
    for arrs, width in ((os_, MIX_WIDTH), (lses, LANES)):
        for arr, dil in zip(arrs, dils):
            if dil == 1:
                in_specs.append(pl.BlockSpec((None, tm, width), lambda bi, i: (bi, i, 0)))
            else:
                in_specs.append(pl.BlockSpec((None, dil, tm // dil, width),
                                             lambda bi, i: (bi, 0, i, 0)))
            args.append(arr)
    row = lambda bi, i: (bi, i, 0)
    in_specs += [pl.BlockSpec((None, tm, MIX_WIDTH), row),
                 pl.BlockSpec((LANES, MIX_WIDTH), lambda bi, i: (0, 0)),
                 pl.BlockSpec((MIX_WIDTH, d), lambda bi, i: (0, 0)),
                 pl.BlockSpec((None, tm, d), row)]
    args += [gate, expand, w_out.astype(BF16), x]
    return pl.pallas_call(
        functools.partial(_dil_combine_kernel, dils),
        grid=(b, s // tm),
        in_specs=in_specs,
        out_specs=pl.BlockSpec((None, tm, d), row),
        out_shape=jax.ShapeDtypeStruct((b, s, d), F32),
        scratch_shapes=[pltpu.VMEM((tm, LANES), F32), pltpu.VMEM((tm, LANES), F32)],
        compiler_params=_cparams(("arbitrary", "arbitrary")),
        name="dil_combine_out_proj",
    )(*args)


N_BIAS_PIECES = 3


def _fox_decay_kernel(f_ref, b_ref, kb_ref):
    s = f_ref.shape[0]
    x = f_ref[...] + b_ref[...]
    logf = jnp.minimum(x, 0.0) - jnp.log1p(jnp.exp(-jnp.abs(x)))
    r = lax.broadcasted_iota(jnp.int32, (BLK, BLK), 0)
    c = lax.broadcasted_iota(jnp.int32, (BLK, BLK), 1)
    tri = (c <= r).astype(F32)
    lane = lax.broadcasted_iota(jnp.int32, (BLK, LANES), 1)
    carry = jnp.zeros((1, LANES), F32)
    for i in range(s // BLK):
        blk = logf[i * BLK:(i + 1) * BLK, :]
        cs = jnp.dot(tri, blk, preferred_element_type=F32, precision=lax.Precision.HIGHEST) + carry
        carry = cs[BLK - 1:BLK, :]
        rest = cs * (-LOG2E)
        out = jnp.zeros((BLK, LANES), F32)
        for j in range(N_BIAS_PIECES):
            piece = rest.astype(BF16).astype(F32)
            rest = rest - piece
            moved = piece if j == 0 else pltpu.roll(piece, N_HEADS * j, 1)
            out = jnp.where((lane >= N_HEADS * j) & (lane < N_HEADS * (j + 1)), moved, out)
        kb_ref[i * BLK:(i + 1) * BLK, :] = out.astype(BF16)


def _fox_decay(f_logit, b_f):
    b, s, _ = f_logit.shape
    b_pad = jnp.pad(b_f.astype(F32), (0, LANES - N_HEADS))[None, :]
    return pl.pallas_call(
        _fox_decay_kernel,
        grid=(b,),
        in_specs=[pl.BlockSpec((None, s, LANES), lambda bi: (bi, 0, 0)),
                  pl.BlockSpec((1, LANES), lambda bi: (0, 0))],
        out_specs=pl.BlockSpec((None, s, LANES), lambda bi: (bi, 0, 0)),
        out_shape=jax.ShapeDtypeStruct((b, s, LANES), BF16),
        compiler_params=_cparams(("arbitrary",)),
        name="fox_decay",
    )(f_logit, b_pad)


FOX_T = PROJ_ROWS
FOX_HEADS_PER_STEP = 4


def _fox_kernel(qt_ref, k_ref, kb_ref, vt_ref, gate_ref, o_ref, m_sc, acc_sc, st_sc):
    pi = pl.program_id(1)
    i = pl.program_id(2)
    t = FOX_T
    heads = range(FOX_HEADS_PER_STEP)
    row = lax.broadcasted_iota(jnp.int32, (LANES, t), 0)
    zeros = jnp.zeros((HEAD_DIM, t), BF16)
    qx = []
    for e in heads:
        h = FOX_HEADS_PER_STEP * pi + e
        pick = (row % N_HEADS == h) & (row < N_HEADS * N_BIAS_PIECES)
        sel = jnp.where(pick, 1.0, 0.0).astype(BF16)
        mine = qt_ref[0, e * HEAD_DIM:(e + 1) * HEAD_DIM, :]
        top = [mine, zeros] if e % 2 == 0 else [zeros, mine]
        qx.append(jnp.concatenate(top + [sel], axis=0))
        m_sc[e] = jnp.full((1, t), MASK_VALUE, F32)
        acc_sc[e] = jnp.zeros((ACC_ROWS, t), F32)

    tk = t // 2

    def scores(jt, hf, e):
        k0 = pl.multiple_of(jt * t + hf * tk, tk)
        pair = slice((e // 2) * LANES, (e // 2 + 1) * LANES)
        kx = jnp.concatenate([k_ref[pl.ds(k0, tk), pair], kb_ref[pl.ds(k0, tk), :]], axis=1)
        st_sc[e, hf] = _dot(kx, qx[e])

    def consume(jt, hf, e, diag):
        st = st_sc[e, hf]
        if diag:
            ok = (lax.broadcasted_iota(jnp.int32, (tk, t), 0) + hf * tk
                  <= lax.broadcasted_iota(jnp.int32, (tk, t), 1))
            st = jnp.where(ok, st, MASK_VALUE)
        m_old = m_sc[e]
        m_new = jnp.maximum(m_old, jnp.max(st, axis=0, keepdims=True))
        pt = jnp.exp2(st - m_new).astype(BF16)
        alpha = jnp.exp2(m_old - m_new)
        vt = vt_ref[jt, e * HEAD_DIM:(e + 1) * HEAD_DIM, hf * tk:(hf + 1) * tk]
        acc_sc[e] = alpha * acc_sc[e] + _dot(_with_ones(vt), pt)
        m_sc[e] = m_new

    for e in heads:
        scores(0, 0, e)

    def body(jt):
        for e in heads:
            scores(jt, 1, e)
            consume(jt, 0, e, False)
        for e in heads:
            scores(jt + 1, 0, e)
            consume(jt, 1, e, False)

    _loop_in_pairs(i, body)
    for e in heads:
        scores(i, 1, e)
        consume(i, 0, e, True)
    for e in heads:
        consume(i, 1, e, True)
    for p in range(FOX_HEADS_PER_STEP // 2):
        outs = []
        for e in (2 * p, 2 * p + 1):
            acc = acc_sc[e]
            outs.append(acc[:HEAD_DIM] / jnp.maximum(acc[HEAD_DIM:HEAD_DIM + 1], 1e-30))
        sl = slice(p * LANES, (p + 1) * LANES)
        out = jnp.concatenate(outs, axis=0).T
        o_ref[:, sl] = (out * _silu(gate_ref[:, sl].astype(F32))).astype(o_ref.dtype)


def _fox_attention(qt, k, kb, vt, gate):
    b, s, _ = k.shape
    t = FOX_T
    assert s % t == 0 and vt.shape == (b, s // t, MIX_WIDTH, t) and qt.shape == vt.shape
    tile = lambda bi, p, i: (bi, i, p)
    nh = FOX_HEADS_PER_STEP
    wd = nh * HEAD_DIM
    return pl.pallas_call(
        _fox_kernel,
        grid=(b, N_HEADS // nh, s // t),
        in_specs=[pl.BlockSpec((None, 1, wd, t), lambda bi, p, i: (bi, i, p, 0)),
                  pl.BlockSpec((None, s, wd), lambda bi, p, i: (bi, 0, p)),
                  pl.BlockSpec((None, s, LANES), lambda bi, p, i: (bi, 0, 0)),
                  pl.BlockSpec((None, s // t, wd, t), lambda bi, p, i: (bi, 0, p, 0)),
                  pl.BlockSpec((None, t, wd), tile)],
        out_specs=pl.BlockSpec((None, t, wd), tile),
        out_shape=jax.ShapeDtypeStruct((b, s, MIX_WIDTH), BF16),
        scratch_shapes=[pltpu.VMEM((nh, 1, t), F32), pltpu.VMEM((nh, ACC_ROWS, t), F32),
                        pltpu.VMEM((nh, 2, t // 2, t), F32)],
        compiler_params=_cparams(("arbitrary", "arbitrary", "arbitrary")),
        name="fox_attn",
    )(qt, k, kb, vt, gate)


def _gelu_tanh(x):
    return 0.5 * x * (1.0 + jnp.tanh(math.sqrt(2.0 / math.pi) * (x + 0.044715 * (x * x * x))))


def _nsa_compress_kernel(ak_ref, av_ref, pe_ref, wa_ref, wb_ref, w2_ref, w2t_ref, c_ref, s_ref,
                         kc_ref, vct_ref):
    nrow = ak_ref.shape[0]
    for idx, a_ref in enumerate((ak_ref, av_ref)):
        a = a_ref[...].astype(F32)
        xa = (a + pe_ref[idx, 0:1, :]).astype(BF16)
        xb = (a + pe_ref[idx, 1:2, :]).astype(BF16)
        ya = _dot(xa, wa_ref[idx])
        yb = _dot(xb, wb_ref[idx])
        hid = _gelu_tanh(ya + pltpu.roll(yb, nrow - 1, 0)).astype(BF16)
        if idx == 0:
            y = _dot(hid, w2_ref[...])
            kc_ref[...] = _apply_rope(y, c_ref[...], s_ref[...]).astype(kc_ref.dtype)
        else:
            vct_ref[...] = _dot_nt(w2t_ref[...], hid).astype(vct_ref.dtype)


def _nsa_compress(kc, vc, pe_k, w1_k, w2_k, pe_v, w1_v, w2_v, rope_c, rope_s):
    b, s, _ = kc.shape
    ns = s // CMP_STRIDE
    g = NSA_KV
    flat = CMP_STRIDE * g * HEAD_DIM

    def w1_halves(w1):
        w1r = w1.reshape(2, CMP_STRIDE, HEAD_DIM, CMP_HIDDEN)
        outs = []
        for hf in range(2):
            z = jnp.einsum('ldc,gh->lgdhc', w1r[hf], jnp.eye(g, dtype=F32))
            outs.append(z.reshape(flat, g * CMP_HIDDEN))
        return outs

    def pe_halves(pe):
        per = pe.reshape(2, CMP_STRIDE, 1, HEAD_DIM)
        return jnp.broadcast_to(per, (2, CMP_STRIDE, g, HEAD_DIM)).reshape(2, flat)

    def w2_bd(w2):
        z = jnp.einsum('cd,gh->gchd', w2, jnp.eye(g, dtype=F32))
        return z.reshape(g * CMP_HIDDEN, g * HEAD_DIM)

    ka, kb = w1_halves(w1_k)
    va, vb = w1_halves(w1_v)
    wa = jnp.stack([ka, va]).astype(BF16)
    wb = jnp.stack([kb, vb]).astype(BF16)
    w2 = w2_bd(w2_k).astype(BF16)
    w2t = w2_bd(w2_v).T.astype(BF16)
    pe = jnp.stack([pe_halves(pe_k), pe_halves(pe_v)]).astype(F32)
    whole = lambda bi: (0, 0, 0)
    per_b = lambda bi: (bi, 0, 0)
    return pl.pallas_call(
        _nsa_compress_kernel,
        grid=(b,),
        in_specs=[pl.BlockSpec((None, ns, flat), per_b),
                  pl.BlockSpec((None, ns, flat), per_b),
                  pl.BlockSpec((2, 2, flat), whole),
                  pl.BlockSpec((2, flat, g * CMP_HIDDEN), whole),
                  pl.BlockSpec((2, flat, g * CMP_HIDDEN), whole),
                  pl.BlockSpec((g * CMP_HIDDEN, LANES), lambda bi: (0, 0)),
                  pl.BlockSpec((LANES, g * CMP_HIDDEN), lambda bi: (0, 0)),
                  pl.BlockSpec((None, ns, LANES), per_b),
                  pl.BlockSpec((None, ns, LANES), per_b)],
        out_specs=[pl.BlockSpec((None, ns, LANES), per_b),
                   pl.BlockSpec((None, LANES, ns), per_b)],
        out_shape=[jax.ShapeDtypeStruct((b, ns, LANES), BF16),
                   jax.ShapeDtypeStruct((b, LANES, ns), BF16)],
        compiler_params=_cparams(("arbitrary",)),
        name="nsa_compress",
    )(kc.reshape(b, ns, flat), vc.reshape(b, ns, flat), pe, wa, wb, w2, w2t,
      rope_c.reshape(b, ns, LANES), rope_s.reshape(b, ns, LANES))


NSA_QB = 2
NSA_TK = 256
NSA_REP = N_HEADS // NSA_KV


def _nsa_kernel(n_cmp, n_sel, n_win, *refs):
    n_wblk = n_win + NSA_QB - 1
    q_ref, kc_ref, vct_ref, ks_ref, vst_ref = refs[:5]
    kw_refs = refs[5:5 + n_wblk]
    vwt_refs = refs[5 + n_wblk:5 + 2 * n_wblk]
    gl_ref, gate_ref, ovt_ref, blk_ref, o_ref = refs[5 + 2 * n_wblk:10 + 2 * n_wblk]
    st_sc, stc_sc, stw_sc, m_sc, acc_sc, ocmp_sc, owin_sc = refs[10 + 2 * n_wblk:]

    n = pl.program_id(1)
    tq, tk, rep = BLK, NSA_TK, NSA_REP
    streams = [(qb, g) for qb in range(NSA_QB) for g in range(NSA_KV)]
    ncp = kc_ref.shape[0]
    t_lane = [(n * NSA_QB + qb) * tq + lax.broadcasted_iota(jnp.int32, (1, tq), 1)
              for qb in range(NSA_QB)]
    qg = [_group_queries(q_ref.at[qb * tq:(qb + 1) * tq], g, rep) for qb, g in streams]

    ci = lax.broadcasted_iota(jnp.int32, (ncp, tq), 0)
    kwn = n_win * BLK
    krow = lax.broadcasted_iota(jnp.int32, (kwn, tq), 0) - (n_win - 1) * BLK
    dist = lax.broadcasted_iota(jnp.int32, (kwn, tq), 1) - krow
    band = (dist >= 0) & (dist <= NSA_WINDOW - 1)
    sel_rows = ovt_ref.shape[0]
    rowi = lax.broadcasted_iota(jnp.int32, (sel_rows, tq), 0)
    rowf = rowi.astype(F32)

    def cmp_scores(sg):
        stc_sc[sg] = _dot_nt(kc_ref[...], qg[sg])

    def cmp_consume(sg):
        qb, g = streams[sg]
        cmask = (ci * CMP_STRIDE + (CMP_LEN - 1) <= t_lane[qb]) & (ci < n_cmp)
        ps = []
        for r in range(rep):
            s_r = jnp.where(cmask, stc_sc[sg, :, r * tq:(r + 1) * tq], MASK_VALUE)
            m = jnp.max(s_r, axis=0, keepdims=True)
            pr = jnp.where(cmask, jnp.exp2(s_r - m), 0.0)
            ps.append(pr.astype(BF16))
        lhs = jnp.concatenate([_with_ones(vct_ref[g * HEAD_DIM:(g + 1) * HEAD_DIM, :]),
                               ovt_ref[...]], axis=0)
        res = _dot(lhs, jnp.concatenate(ps, axis=1))
        inv = 1.0 / jnp.maximum(res[HEAD_DIM:HEAD_DIM + 1], 1e-30)
        ocmp_sc[sg] = res[:HEAD_DIM] * inv
        imp = None
        for r in range(rep):
            ql = slice(r * tq, (r + 1) * tq)
            t = res[ACC_ROWS:, ql] * inv[:, ql]
            imp = t if imp is None else imp + t
        return imp

    def select(sg, imp):
        cur = t_lane[streams[sg][0]] // SEL_LEN
        forced = (rowi == 0) | (rowi == cur)
        causal = rowi <= cur
        score = jnp.where(causal, jnp.where(forced, FORCED_SCORE, imp), MASK_VALUE)
        score = jnp.where(rowi < n_sel, score, PAD_SCORE)
        chosen = jnp.zeros((sel_rows, tq), F32)
        for _ in range(min(SEL_TOPK, n_sel)):
            mx = jnp.max(score, axis=0, keepdims=True)
            first = jnp.min(jnp.where(score == mx, rowf, float(LANES)), axis=0, keepdims=True)
            hit = rowf == first
            chosen = jnp.where(hit, 1.0, chosen)
            score = jnp.where(hit, PAD_SCORE, score)
        bias_t = jnp.where(causal & (chosen > 0.5), 0.0, MASK_VALUE)
        bias_t = jnp.concatenate([bias_t, jnp.zeros((LANES - sel_rows, tq), F32)], axis=0)
        bias = bias_t.T.astype(BF16)
        return jnp.concatenate([qg[sg], jnp.concatenate([bias] * rep, axis=0)], axis=1)

    def win_scores(sg):
        qb = streams[sg][0]
        kwcat = jnp.concatenate([kr[...] for kr in kw_refs[qb:qb + n_win]], axis=0)
        stw_sc[sg] = _dot_nt(kwcat, qg[sg])

    def win_consume(sg):
        qb, g = streams[sg]
        wok = band & ((n * NSA_QB + qb) * BLK + krow >= 0)
        vwt = jnp.concatenate([vr[0] for vr in vwt_refs[qb:qb + n_win]], axis=1)
        pts = []
        for r in range(rep):
            s_r = jnp.where(wok, stw_sc[sg, :, r * tq:(r + 1) * tq], MASK_VALUE)
            m = jnp.max(s_r, axis=0, keepdims=True)
            pts.append(jnp.exp2(s_r - m).astype(BF16))
        owin_sc[sg] = _dot(_with_ones(vwt[g * HEAD_DIM:(g + 1) * HEAD_DIM, :]),
                           jnp.concatenate(pts, axis=1))

    ns = len(streams)
    cmp_scores(0)
    cmp_scores(1)
    imps = []
    for sg in range(ns):
        if sg + 2 < ns:
            cmp_scores(sg + 2)
        else:
            win_scores(sg + 2 - ns)
        imps.append(cmp_consume(sg))
    qsel = []
    for sg in range(ns):
        if sg + 2 < ns:
            win_scores(sg + 2)
        qsel.append(select(sg, imps[sg]))
        win_consume(sg)

    for sg in range(ns):
        m_sc[sg] = jnp.full((1, rep * tq), MASK_VALUE, F32)
        acc_sc[sg] = jnp.zeros((ACC_ROWS, rep * tq), F32)

    hrep = rep // 2
    wide = hrep * tq
    units = [(sg, hf) for sg in range(ns) for hf in range(2)]
    nu = len(units)

    def scores(c, u):
        sg, hf = units[u]
        k0 = pl.multiple_of(c * tk, tk)
        kx = jnp.concatenate([ks_ref[pl.ds(k0, tk), :], blk_ref[pl.ds(k0, tk), :]], axis=1)
        st_sc[u] = _dot_nt(kx, qsel[sg][hf * wide:(hf + 1) * wide, :])

    def consume(c, u, last):
        sg, hf = units[u]
        qb, g = streams[sg]
        lanes = slice(hf * wide, (hf + 1) * wide)
        if last:
            tok_ok = c * tk + lax.broadcasted_iota(jnp.int32, (tk, tq), 0) <= t_lane[qb]
        m_old = m_sc[sg, :, lanes]
        ms, pts = [], []
        for r in range(hrep):
            s_r = st_sc[u, :, r * tq:(r + 1) * tq]
            if last:
                s_r = jnp.where(tok_ok, s_r, MASK_VALUE)
            m_new = jnp.maximum(m_old[:, r * tq:(r + 1) * tq], jnp.max(s_r, axis=0, keepdims=True))
            pts.append(jnp.exp2(s_r - m_new).astype(BF16))
            ms.append(m_new)
        m_new = jnp.concatenate(ms, axis=1)
        alpha = jnp.exp2(m_old - m_new)
        vt = vst_ref[c, g * HEAD_DIM:(g + 1) * HEAD_DIM, :]
        acc_sc[sg, :, lanes] = (alpha * acc_sc[sg, :, lanes]
                                + _dot(_with_ones(vt), jnp.concatenate(pts, axis=1)))
        m_sc[sg, :, lanes] = m_new

    n_steps = ((n * NSA_QB + 1) * tq + tk - 1) // tk
    scores(0, 0)
    scores(0, 1)

    def body(c):
        for u in range(nu):
            if u + 2 < nu:
                scores(c, u + 2)
            else:
                scores(c + 1, u + 2 - nu)
            consume(c, u, False)

    _loop_in_pairs(n_steps - 1, body)
    for u in range(nu):
        if u + 2 < nu:
            scores(n_steps - 1, u + 2)
        consume(n_steps - 1, u, True)

    for qb in range(NSA_QB):
        tok = slice(qb * tq, (qb + 1) * tq)
        gate_t = (1.0 / (1.0 + jnp.exp(-gl_ref[tok, :]))).T
        for p in range(N_PAIRS):
            sl = slice(p * LANES, (p + 1) * LANES)
            g = (2 * p) // rep
            sg = qb * NSA_KV + g
            rows = []
            for e in range(2):
                h = 2 * p + e
                ql = slice((h - g * rep) * tq, (h - g * rep + 1) * tq)
                slc, win = acc_sc[sg][:, ql], owin_sc[sg][:, ql]
                branches = (ocmp_sc[sg][:, ql],
                            slc[:HEAD_DIM] / jnp.maximum(slc[HEAD_DIM:HEAD_DIM + 1], 1e-30),
                            win[:HEAD_DIM] / win[HEAD_DIM:HEAD_DIM + 1])
                out = None
                for j, br in enumerate(branches):
                    t = gate_t[3 * h + j:3 * h + j + 1, :] * br
                    out = t if out is None else out + t
                rows.append(out)
            pair = jnp.concatenate(rows, axis=0).T
            o_ref[tok, sl] = (pair * _silu(gate_ref[tok, sl].astype(F32))).astype(o_ref.dtype)


def _selection_overlap_t(n_cmp_pad, n_cmp, n_sel):
    cs = np.arange(n_cmp_pad) * CMP_STRIDE
    js = np.arange(LANES) * SEL_LEN
    ov = np.minimum(cs[None, :] + CMP_LEN, js[:, None] + SEL_LEN) - np.maximum(cs[None, :], js[:, None])
    ov = (np.clip(ov, 0, None) / CMP_LEN).astype(np.float32)
    ov[:, n_cmp:] = 0.0
    ov[n_sel:, :] = 0.0
    return ov


def _nsa_attention(q, k_cmp, v_cmp_t, ks, vs_t, kw, vw_t, g_logit, gate):
    b, s, _ = q.shape
    tq = NSA_QB * BLK
    ncp = k_cmp.shape[1]
    n_cmp = s // CMP_STRIDE - 1
    n_sel = s // SEL_LEN
    n_win = -(-(NSA_WINDOW - 1) // BLK) + 1
    assert n_sel <= LANES and s % NSA_TK == 0 and tq == NSA_TK
    sel_rows = -(-n_sel // 16) * 16
    ov_t = jnp.asarray(_selection_overlap_t(ncp, n_cmp, n_sel)[:sel_rows], BF16)
    row = lambda bi, i: (bi, i, 0)
    per_b = lambda bi, i: (bi, 0, 0)
    in_specs = [pl.BlockSpec((None, tq, MIX_WIDTH), row),
                pl.BlockSpec((None, ncp, LANES), per_b),
                pl.BlockSpec((None, LANES, ncp), per_b),
                pl.BlockSpec((None, s, LANES), per_b),
                pl.BlockSpec((None, s // NSA_TK, LANES, NSA_TK), lambda bi, i: (bi, 0, 0, 0))]
    args = [q, k_cmp, v_cmp_t, ks, vs_t]
    for j in range(n_win - 1, -NSA_QB, -1):
        in_specs.append(pl.BlockSpec(
            (None, BLK, LANES), lambda bi, i, j=j: (bi, jnp.maximum(NSA_QB * i - j, 0), 0)))
        args.append(kw)
    for j in range(n_win - 1, -NSA_QB, -1):
        in_specs.append(pl.BlockSpec(
            (None, 1, LANES, BLK), lambda bi, i, j=j: (bi, jnp.maximum(NSA_QB * i - j, 0), 0, 0)))
        args.append(vw_t)
    block_onehot = jnp.asarray(
        (np.arange(s)[:, None] // SEL_LEN == np.arange(LANES)[None, :]).astype(np.float32), BF16)
    in_specs += [pl.BlockSpec((None, tq, LANES), row),
                 pl.BlockSpec((None, tq, MIX_WIDTH), row),
                 pl.BlockSpec((sel_rows, ncp), lambda bi, i: (0, 0)),
                 pl.BlockSpec((s, LANES), lambda bi, i: (0, 0))]
    args += [g_logit, gate, ov_t, block_onehot]
    wide = NSA_REP * BLK
    ns = NSA_QB * NSA_KV
    return pl.pallas_call(
        functools.partial(_nsa_kernel, n_cmp, n_sel, n_win),
        grid=(b, s // tq),
        in_specs=in_specs,
        out_specs=pl.BlockSpec((None, tq, MIX_WIDTH), row),
        out_shape=jax.ShapeDtypeStruct((b, s, MIX_WIDTH), BF16),
        scratch_shapes=[pltpu.VMEM((2 * ns, NSA_TK, wide // 2), F32),
                        pltpu.VMEM((ns, ncp, wide), F32),
                        pltpu.VMEM((ns, n_win * BLK, wide), F32),
                        pltpu.VMEM((ns, 1, wide), F32),
                        pltpu.VMEM((ns, ACC_ROWS, wide), F32),
                        pltpu.VMEM((ns, HEAD_DIM, wide), F32),
                        pltpu.VMEM((ns, ACC_ROWS, wide), F32)],
        compiler_params=_cparams(("arbitrary", "arbitrary")),
        name="nsa_attn",
    )(*args)


def _split(w, sizes):
    offs = np.cumsum([0] + list(sizes))
    return [w[:, int(offs[i]):int(offs[i + 1])] for i in range(len(sizes))]


def _swa_layer(x, rope, gain, w_in, sinks, w_out):
    kvw = SWA_KV * HEAD_DIM
    parts = _split(w_in, [MIX_WIDTH, kvw, kvw, MIX_WIDTH])
    segs = [_Seg(MIX_WIDTH, rope=True, scale=Q_SCALE), _Seg(kvw, rope=True), _Seg(kvw),
            _Seg(MIX_WIDTH)]
    q, k, v, gate = _norm_proj(x, gain, parts, segs, rope)
    o = _banded_attention(q, k, v, SWA_WINDOW - 1, sinks=sinks, gate=gate)
    return o, w_out


def _dilated_layer(x, rope, gain, w_in, w_out):
    b, s, _ = x.shape
    kvw = DIL_KV * HEAD_DIM
    sizes, segs = [], []
    for window, dil in DIL_PATTERNS:
        assert s % (dil * BLK) == 0
        sizes += [MIX_WIDTH, kvw, kvw]
        segs += [_Seg(MIX_WIDTH, rope=True, scale=Q_SCALE, dil=dil), _Seg(kvw, rope=True, dil=dil),
                 _Seg(kvw, dil=dil)]
    sizes.append(MIX_WIDTH)
    segs.append(_Seg(MIX_WIDTH))
    res = _norm_proj(x, gain, _split(w_in, sizes), segs, rope)
    gate = res[-1]
    os_, lses, dils = [], [], []
    for gi, (window, dil) in enumerate(DIL_PATTERNS):
        q, k, v = res[3 * gi:3 * gi + 3]
        if dil > 1:
            q, k, v = (t.reshape(b * dil, s // dil, t.shape[-1]) for t in (q, k, v))
        o, lse = _banded_attention(q, k, v, window // dil, want_lse=True)
        if dil > 1:
            o = o.reshape(b, dil, s // dil, MIX_WIDTH)
            lse = lse.reshape(b, dil, s // dil, LANES)
        os_.append(o)
        lses.append(lse)
        dils.append(dil)
    return _dil_combine_out_proj(os_, lses, gate, tuple(dils), w_out, x)


def _fox_layer(x, gain, w_in, b_f, w_out):
    sizes = [MIX_WIDTH, MIX_WIDTH, MIX_WIDTH, N_HEADS, MIX_WIDTH]
    segs = [_Seg(MIX_WIDTH, scale=Q_SCALE, tile=FOX_T), _Seg(MIX_WIDTH),
            _Seg(MIX_WIDTH, tile=FOX_T), _Seg(LANES, dtype=F32), _Seg(MIX_WIDTH)]
    qt, k, vt, f_logit, gate = _norm_proj(x, gain, _split(w_in, sizes), segs)
    kb = _fox_decay(f_logit, b_f)
    return _fox_attention(qt, k, kb, vt, gate), w_out


def _nsa_layer(x, positions, rope, gain, w_in, pe_k, w1_k, w2_k, pe_v, w1_v, w2_v, w_out):
    b, s, _ = x.shape
    kvw = NSA_KV * HEAD_DIM
    sizes = [MIX_WIDTH] + [kvw] * 6 + [3 * N_HEADS, MIX_WIDTH]
    segs = [_Seg(MIX_WIDTH, rope=True, scale=Q_SCALE), _Seg(kvw), _Seg(kvw), _Seg(kvw, rope=True),
            _Seg(kvw, tile=NSA_TK), _Seg(kvw, rope=True), _Seg(kvw, tile=BLK),
            _Seg(LANES, dtype=F32), _Seg(MIX_WIDTH)]
    q, kc, vc, ks, vs_t, kw, vw_t, g_logit, gate = _norm_proj(
        x, gain, _split(w_in, sizes), segs, rope)
    ns = s // CMP_STRIDE
    cmp_pos = jnp.concatenate(
        [positions[:, CMP_LEN - 1::CMP_STRIDE], positions[:, -1:]], axis=1)[:, :ns]
    cmp_c, cmp_s = _rope_tables(cmp_pos.reshape(-1))
    k_cmp, v_cmp_t = _nsa_compress(kc, vc, pe_k, w1_k, w2_k, pe_v, w1_v, w2_v, cmp_c, cmp_s)
    return _nsa_attention(q, k_cmp, v_cmp_t, ks, vs_t, kw, vw_t, g_logit, gate), w_out


def kernel(x, positions, norm_0, w_in_0, sinks_0, w_out_0, norm_1, w_in_1, w_out_1, norm_2, w_in_2, b_f_2, w_out_2, norm_3, w_in_3, cmp_pe_k_3, cmp_w1_k_3, cmp_w2_k_3, cmp_pe_v_3, cmp_w1_v_3, cmp_w2_v_3, w_out_3, final_norm):
    rope = _rope_tables(positions.reshape(-1))
    o, w = _swa_layer(x, rope, norm_0, w_in_0, sinks_0, w_out_0)
    x = _out_proj(o, w, x)
    x = _dilated_layer(x, rope, norm_1, w_in_1, w_out_1)
    o, w = _fox_layer(x, norm_2, w_in_2, b_f_2, w_out_2)
    x = _out_proj(o, w, x)
    o, w = _nsa_layer(x, positions, rope, norm_3, w_in_3, cmp_pe_k_3, cmp_w1_k_3, cmp_w2_k_3,
                      cmp_pe_v_3, cmp_w1_v_3, cmp_w2_v_3, w_out_3)
    return _out_proj(o, w, x, final_gain=final_norm)
```

```python
import functools
import math

import numpy as np
import jax
import jax.numpy as jnp
from jax import lax
from jax.experimental import pallas as pl
from jax.experimental.pallas import tpu as pltpu

HEAD_DIM = 64
N_HEADS = 16
N_PAIRS = N_HEADS // 2
MIX_WIDTH = N_HEADS * HEAD_DIM
ROT_DIM = HEAD_DIM // 4
ROT_HALF = ROT_DIM // 2
ROPE_THETA = 500000.0
BLK = 128
LANES = 128
NORM_EPS = 1e-6
MASK_VALUE = -1e30
PAD_SCORE = -3e38
LOG2E = math.log2(math.e)
Q_SCALE = HEAD_DIM ** -0.5 * LOG2E

SWA_KV = 4
SWA_WINDOW = 128
DIL_KV = 4
DIL_PATTERNS = ((128, 1), (512, 4), (2048, 16))
NSA_KV = 2
CMP_LEN = 32
CMP_STRIDE = 16
CMP_HIDDEN = 256
SEL_LEN = 64
SEL_TOPK = 8
NSA_WINDOW = 256
FORCED_SCORE = 1e4

VMEM_LIMIT_BYTES = 56 * 1024 * 1024
PROJ_ROWS = 512
PROJ_COLS = 512
ACC_ROWS = HEAD_DIM + 16
F32 = jnp.float32
BF16 = jnp.bfloat16


def _cparams(sem):
    return pltpu.CompilerParams(dimension_semantics=sem, vmem_limit_bytes=VMEM_LIMIT_BYTES)


def _lane_half(shape):
    return lax.broadcasted_iota(jnp.int32, shape, 1) // HEAD_DIM


def _swap_halves(t):
    return jnp.concatenate([t[:, HEAD_DIM:], t[:, :HEAD_DIM]], axis=1)


def _head_query(qp, e, kv_half):
    qh = jnp.where(_lane_half(qp.shape) == e, qp, jnp.zeros_like(qp))
    if e != kv_half:
        qh = _swap_halves(qh)
    return qh


def _group_queries(q_ref, g, rep):
    tiles = []
    for r in range(rep):
        h = g * rep + r
        tiles.append(_head_query(q_ref[:, (h // 2) * LANES:(h // 2 + 1) * LANES], h % 2, g % 2))
    return jnp.concatenate(tiles, axis=0)


def _dot_nt(a, b):
    return lax.dot_general(a, b, (((1,), (1,)), ((), ())), preferred_element_type=F32)


def _dot(a, b):
    return jnp.dot(a, b, preferred_element_type=F32)


def _silu(x):
    return x * (1.0 / (1.0 + jnp.exp(-x)))


def _loop_in_pairs(count, body):
    odd = count % 2

    @pl.when(odd == 1)
    def _():
        body(0)

    def two(jj, carry):
        body(odd + 2 * jj)
        body(odd + 2 * jj + 1)
        return carry

    lax.fori_loop(0, count // 2, two, 0)


def _with_ones(vt):
    return jnp.concatenate([vt, jnp.ones((ACC_ROWS - HEAD_DIM, vt.shape[1]), BF16)], axis=0)


def _rope_table_kernel(pos_ref, inv_ref, c_ref, s_ref):
    pos = pos_ref[...].astype(F32)
    ang = pos * inv_ref[...]
    d = lax.broadcasted_iota(jnp.int32, ang.shape, 1) % HEAD_DIM
    cos = jnp.cos(ang)
    sin = jnp.sin(ang)
    c_ref[...] = jnp.where(d < ROT_DIM, cos, 1.0)
    s_ref[...] = jnp.where(d < ROT_HALF, -sin, jnp.where(d < ROT_DIM, sin, 0.0))


def _rope_tables(pos_flat):
    t = pos_flat.shape[0]
    rows = min(t, 2048)
    assert t % rows == 0
    inv = jnp.power(ROPE_THETA, -jnp.arange(ROT_HALF, dtype=F32) / ROT_HALF)
    inv_l = jnp.tile(inv, LANES // ROT_HALF)[None, :]
    out = jax.ShapeDtypeStruct((t, LANES), F32)
    return pl.pallas_call(
        _rope_table_kernel,
        grid=(t // rows,),
        in_specs=[pl.BlockSpec((rows, 1), lambda i: (i, 0)),
                  pl.BlockSpec((1, LANES), lambda i: (0, 0))],
        out_specs=[pl.BlockSpec((rows, LANES), lambda i: (i, 0))] * 2,
        out_shape=[out, out],
        compiler_params=_cparams(("arbitrary",)),
        name="rope_tables",
    )(pos_flat[:, None], inv_l)


def _apply_rope(y, c, s):
    outs = []
    for j in range(y.shape[1] // LANES):
        t = y[:, j * LANES:(j + 1) * LANES]
        d = lax.broadcasted_iota(jnp.int32, t.shape, 1) % HEAD_DIM
        partner = jnp.where(d < ROT_HALF, pltpu.roll(t, LANES - ROT_HALF, 1),
                            pltpu.roll(t, ROT_HALF, 1))
        outs.append(t * c + partner * s)
    return outs[0] if len(outs) == 1 else jnp.concatenate(outs, axis=1)


class _Seg:
    def __init__(self, width, rope=False, scale=None, dtype=BF16, dil=1, tile=None):
        self.width, self.rope, self.scale, self.dtype, self.dil = width, rope, scale, dtype, dil
        self.tile = tile
        assert not (tile and (rope or dil > 1))


def _norm_proj_kernel(segs, use_rope, has_residual, *refs):
    has_t = any(sg.tile for sg in segs)
    x_ref, g_ref, w_ref = refs[:3]
    k = 3
    if has_t:
        wt_ref = refs[k]
        k += 1
    if use_rope:
        c_ref, s_ref = refs[k:k + 2]
        k += 2
    if has_residual:
        po_ref, pw_ref = refs[k:k + 2]
        k += 2
        xnew_ref = refs[k]
        k += 1
    out_refs = refs[k:k + len(segs)]
    stage_ref = refs[k + len(segs)] if any(sg.dil > 1 for sg in segs) else None

    x = x_ref[...]
    if has_residual:
        x = x + _dot(po_ref[...], pw_ref[...])
        xnew_ref[...] = x
    var = jnp.mean(x * x, axis=-1, keepdims=True)
    h = (x * lax.rsqrt(var + NORM_EPS) * g_ref[...]).astype(BF16)
    rows = x.shape[0]
    col = 0
    tcol = 0
    for sg, o_ref in zip(segs, out_refs):
        if sg.tile:
            for c0 in range(0, sg.width, PROJ_COLS):
                cw = min(PROJ_COLS, sg.width - c0)
                yt = _dot_nt(wt_ref[tcol + c0:tcol + c0 + cw, :], h)
                if sg.scale is not None:
                    yt = yt * sg.scale
                for ti in range(rows // sg.tile):
                    o_ref[ti, c0:c0 + cw, :] = yt[:, ti * sg.tile:(ti + 1) * sg.tile].astype(sg.dtype)
            tcol += sg.width
            continue
        for c0 in range(0, sg.width, PROJ_COLS):
            cw = min(PROJ_COLS, sg.width - c0)
            y = _dot(h, w_ref[:, col + c0:col + c0 + cw])
            if sg.rope:
                y = _apply_rope(y, c_ref[...], s_ref[...])
            if sg.scale is not None:
                y = y * sg.scale
            if sg.dil > 1:
                sub = rows // sg.dil
                for j in range(cw // LANES):
                    stage_ref[j] = y[:, j * LANES:(j + 1) * LANES]
                for r in range(sg.dil):
                    for j in range(cw // LANES):
                        lo = c0 + j * LANES
                        o_ref[r, :, lo:lo + LANES] = (
                            stage_ref[j, pl.ds(r, sub, stride=sg.dil), :].astype(sg.dtype))
            else:
                o_ref[:, c0:c0 + cw] = y.astype(sg.dtype)
        col += sg.width


def _norm_proj(x, gain, w_parts, segs, rope=None, residual=None):
    b, s, d = x.shape
    tm = PROJ_ROWS
    assert s % tm == 0
    w_cols, wt_rows = [], []
    for wp, sg in zip(w_parts, segs):
        if wp.shape[1] < sg.width:
            wp = jnp.pad(wp, ((0, 0), (0, sg.width - wp.shape[1])))
        if sg.tile:
            wt_rows.append(wp.T)
        else:
            w_cols.append(wp)
    w = jnp.concatenate(w_cols, axis=1).astype(BF16)
    n = w.shape[1]
    use_rope = rope is not None
    once = dict(pipeline_mode=pl.Buffered(1))
    in_specs = [pl.BlockSpec((None, tm, d), lambda bi, i: (bi, i, 0)),
                pl.BlockSpec((1, d), lambda bi, i: (0, 0)),
                pl.BlockSpec((d, n), lambda bi, i: (0, 0), **once)]
    args = [x, gain[None, :], w]
    if wt_rows:
        wt = jnp.concatenate(wt_rows, axis=0).astype(BF16)
        in_specs.append(pl.BlockSpec(wt.shape, lambda bi, i: (0, 0), **once))
        args.append(wt)
    if use_rope:
        in_specs += [pl.BlockSpec((None, tm, LANES), lambda bi, i: (bi, i, 0))] * 2
        args += [rope[0].reshape(b, s, LANES), rope[1].reshape(b, s, LANES)]
    out_specs, out_shape = [], []
    if residual is not None:
        po, pw = residual
        in_specs += [pl.BlockSpec((None, tm, po.shape[-1]), lambda bi, i: (bi, i, 0)),
                     pl.BlockSpec(pw.shape, lambda bi, i: (0, 0), **once)]
        args += [po, pw.astype(BF16)]
        out_shape.append(jax.ShapeDtypeStruct((b, s, d), F32))
        out_specs.append(pl.BlockSpec((None, tm, d), lambda bi, i: (bi, i, 0)))
    for sg in segs:
        if sg.dil > 1:
            assert tm % sg.dil == 0
            out_shape.append(jax.ShapeDtypeStruct((b, sg.dil, s // sg.dil, sg.width), sg.dtype))
            out_specs.append(pl.BlockSpec((None, sg.dil, tm // sg.dil, sg.width),
                                          lambda bi, i: (bi, 0, i, 0)))
        elif sg.tile:
            assert tm % sg.tile == 0
            out_shape.append(jax.ShapeDtypeStruct((b, s // sg.tile, sg.width, sg.tile), sg.dtype))
            out_specs.append(pl.BlockSpec((None, tm // sg.tile, sg.width, sg.tile),
                                          lambda bi, i: (bi, i, 0, 0)))
        else:
            out_shape.append(jax.ShapeDtypeStruct((b, s, sg.width), sg.dtype))
            out_specs.append(pl.BlockSpec((None, tm, sg.width), lambda bi, i: (bi, i, 0)))
    scratch = ([pltpu.VMEM((PROJ_COLS // LANES, tm, LANES), F32)]
               if any(sg.dil > 1 for sg in segs) else [])
    return pl.pallas_call(
        functools.partial(_norm_proj_kernel, segs, use_rope, residual is not None),
        grid=(b, s // tm),
        in_specs=in_specs,
        out_specs=out_specs,
        out_shape=out_shape,
        scratch_shapes=scratch,
        compiler_params=_cparams(("arbitrary", "arbitrary")),
        name="norm_proj",
    )(*args)


def _out_proj_kernel(final, *refs):
    if final:
        o_ref, w_ref, x_ref, g_ref, y_ref = refs
    else:
        o_ref, w_ref, x_ref, y_ref = refs
    y = x_ref[...] + _dot(o_ref[...], w_ref[...])
    if final:
        var = jnp.mean(y * y, axis=-1, keepdims=True)
        y = y * lax.rsqrt(var + NORM_EPS) * g_ref[...]
    y_ref[...] = y


def _out_proj(o, w, x, final_gain=None):
    b, s, d = x.shape
    m = o.shape[-1]
    tm = PROJ_ROWS
    final = final_gain is not None
    row = lambda bi, i: (bi, i, 0)
    in_specs = [pl.BlockSpec((None, tm, m), row),
                pl.BlockSpec((m, d), lambda bi, i: (0, 0)),
                pl.BlockSpec((None, tm, d), row)]
    args = [o, w.astype(BF16), x]
    if final:
        in_specs.append(pl.BlockSpec((1, d), lambda bi, i: (0, 0)))
        args.append(final_gain[None, :])
    return pl.pallas_call(
        functools.partial(_out_proj_kernel, final),
        grid=(b, s // tm),
        in_specs=in_specs,
        out_specs=pl.BlockSpec((None, tm, d), row),
        out_shape=jax.ShapeDtypeStruct((b, s, d), F32),
        compiler_params=_cparams(("arbitrary", "arbitrary")),
        name="out_proj",
    )(*args)


BANDED_MAX_QB = 2


def _banded_kernel(n_qb, max_dist, n_prev, n_kv, has_sink, has_gate, want_lse, *refs):
    q_ref = refs[0]
    n_blocks = n_qb + n_prev
    k_refs = refs[1:1 + n_blocks]
    v_refs = refs[1 + n_blocks:1 + 2 * n_blocks]
    k = 1 + 2 * n_blocks
    sink_ref = gate_ref = lse_ref = None
    if has_sink:
        sink_ref = refs[k]; k += 1
    if has_gate:
        gate_ref = refs[k]; k += 1
    o_ref = refs[k]; k += 1
    if want_lse:
        lse_ref = refs[k]; k += 1
    ot_sc = refs[k]
    st_sc = refs[k + 1]

    n = pl.program_id(1)
    tq = BLK
    kw = (n_prev + 1) * BLK
    rep = N_HEADS // n_kv
    krow = lax.broadcasted_iota(jnp.int32, (kw, tq), 0) - n_prev * BLK
    dist = lax.broadcasted_iota(jnp.int32, (kw, tq), 1) - krow
    band = (dist >= 0) & (dist <= max_dist)
    lse_rows = [[] for _ in range(n_qb)]
    units = [(qb, g) for qb in range(n_qb) for g in range(n_kv)]

    def scores(u):
        qb, g = units[u]
        sl = slice((g // 2) * LANES, (g // 2 + 1) * LANES)
        kcat = jnp.concatenate([kr[:, sl] for kr in k_refs[qb:qb + n_prev + 1]], axis=0)
        st_sc[u] = _dot_nt(kcat, _group_queries(q_ref.at[qb * tq:(qb + 1) * tq], g, rep))

    def consume(u):
        qb, g = units[u]
        sl = slice((g // 2) * LANES, (g // 2 + 1) * LANES)
        ok = band & ((n * n_qb + qb) * BLK + krow >= 0)
        vcat = jnp.concatenate([vr[:, sl] for vr in v_refs[qb:qb + n_prev + 1]], axis=0)
        vt = vcat.astype(F32).T.astype(BF16)
        ms, pts = [], []
        for r in range(rep):
            s_r = jnp.where(ok, st_sc[u, :, r * tq:(r + 1) * tq], MASK_VALUE)
            m = jnp.max(s_r, axis=0, keepdims=True)
            if has_sink:
                h = g * rep + r
                m = jnp.maximum(m, sink_ref[0:1, h:h + 1])
            pts.append(jnp.exp2(s_r - m).astype(BF16))
            ms.append(m)
        acc = _dot(_with_ones(vt[(g % 2) * HEAD_DIM:(g % 2 + 1) * HEAD_DIM, :]),
                   jnp.concatenate(pts, axis=1))
        for r in range(rep):
            h = g * rep + r
            den = acc[HEAD_DIM:HEAD_DIM + 1, r * tq:(r + 1) * tq]
            if has_sink:
                den = den + jnp.exp2(sink_ref[0:1, h:h + 1] - ms[r])
            ot_sc[qb * N_HEADS + h] = acc[:HEAD_DIM, r * tq:(r + 1) * tq] / den
            if want_lse:
                lse_rows[qb].append(ms[r] + jnp.log2(den))

    scores(0)
    scores(1)
    for u in range(len(units)):
        if u + 2 < len(units):
            scores(u + 2)
        consume(u)
    for qb in range(n_qb):
        rows = slice(qb * tq, (qb + 1) * tq)
        for p in range(N_PAIRS):
            sl = slice(p * LANES, (p + 1) * LANES)
            pair = jnp.concatenate([ot_sc[qb * N_HEADS + 2 * p],
                                    ot_sc[qb * N_HEADS + 2 * p + 1]], axis=0).T
            if has_gate:
                pair = pair * _silu(gate_ref[rows, sl].astype(F32))
            o_ref[rows, sl] = pair.astype(o_ref.dtype)
        if want_lse:
            lse_t = jnp.concatenate(
                lse_rows[qb] + [jnp.zeros((LANES - N_HEADS, tq), F32)], axis=0)
            lse_ref[rows, :] = lse_t.T


def _banded_attention(q, k, v, max_dist, sinks=None, gate=None, want_lse=False):
    bq, sq, _ = q.shape
    kvw = k.shape[-1]
    n_kv = kvw // HEAD_DIM
    n_prev = -(-max_dist // BLK)
    n_qb = BANDED_MAX_QB if sq % (BANDED_MAX_QB * BLK) == 0 else 1
    tq = n_qb * BLK
    assert sq % tq == 0
    row = lambda b, i: (b, i, 0)
    in_specs = [pl.BlockSpec((None, tq, MIX_WIDTH), row)]
    args = [q]
    for arr in (k, v):
        for j in range(n_prev, -n_qb, -1):
            in_specs.append(pl.BlockSpec(
                (None, BLK, kvw), lambda b, i, j=j: (b, jnp.maximum(n_qb * i - j, 0), 0)))
            args.append(arr)
    if sinks is not None:
        in_specs.append(pl.BlockSpec((1, LANES), lambda b, i: (0, 0)))
        args.append(jnp.pad(sinks.astype(F32) * LOG2E, (0, LANES - N_HEADS))[None, :])
    if gate is not None:
        in_specs.append(pl.BlockSpec((None, tq, MIX_WIDTH), row))
        args.append(gate)
    out_specs = [pl.BlockSpec((None, tq, MIX_WIDTH), row)]
    out_shape = [jax.ShapeDtypeStruct((bq, sq, MIX_WIDTH), BF16)]
    if want_lse:
        out_specs.append(pl.BlockSpec((None, tq, LANES), row))
        out_shape.append(jax.ShapeDtypeStruct((bq, sq, LANES), F32))
    res = pl.pallas_call(
        functools.partial(_banded_kernel, n_qb, max_dist, n_prev, n_kv, sinks is not None,
                          gate is not None, want_lse),
        grid=(bq, sq // tq),
        in_specs=in_specs,
        out_specs=out_specs,
        out_shape=out_shape,
        scratch_shapes=[pltpu.VMEM((n_qb * N_HEADS, HEAD_DIM, BLK), F32),
                        pltpu.VMEM((n_qb * n_kv, (n_prev + 1) * BLK,
                                    (N_HEADS // n_kv) * BLK), F32)],
        compiler_params=_cparams(("arbitrary", "arbitrary")),
        name="banded_attn",
    )(*args)
    return res if want_lse else res[0]


def _dil_combine_kernel(dils, *refs):
    ng = len(dils)
    o_refs, l_refs = refs[:ng], refs[ng:2 * ng]
    gate_ref, expand_ref, w_ref, x_ref, out_ref, stage_ref, lstage_ref = refs[2 * ng:]
    rows = out_ref.shape[0]

    def natural(ref, dil, stage, slab):
        sl = slice(slab * LANES, (slab + 1) * LANES)
        if dil == 1:
            return ref[:, sl].astype(F32)
        sub = rows // dil
        for r in range(dil):
            stage[pl.ds(r, sub, stride=dil), :] = ref[r, :, sl].astype(F32)
        return stage[...]

    lses = [natural(l_refs[i], dils[i], lstage_ref, 0) for i in range(ng)]
    mx = functools.reduce(jnp.maximum, lses)
    ws = [jnp.exp2(l - mx) for l in lses]
    tot = functools.reduce(lambda a, c: a + c, ws)
    inv = 1.0 / tot

    def widen(w):
        hi = w.astype(BF16)
        lo = (w - hi.astype(F32)).astype(BF16)
        return _dot(hi, expand_ref[...]) + _dot(lo, expand_ref[...])

    wide = [widen(w * inv) for w in ws[:-1]]
    wide.append(1.0 - functools.reduce(lambda a, c: a + c, wide))
    gated = []
    for p in range(N_PAIRS):
        sl = slice(p * LANES, (p + 1) * LANES)
        acc = None
        for gi in range(ng):
            t = wide[gi][:, sl] * natural(o_refs[gi], dils[gi], stage_ref, p)
            acc = t if acc is None else acc + t
        gated.append((acc * _silu(gate_ref[:, sl].astype(F32))).astype(BF16))
    out_ref[...] = x_ref[...] + _dot(jnp.concatenate(gated, axis=1), w_ref[...])


def _dil_combine_out_proj(os_, lses, gate, dils, w_out, x):
    b, s, d = x.shape
    tm = PROJ_ROWS
    expand = jnp.asarray((np.arange(LANES)[:, None] == np.arange(MIX_WIDTH)[None, :] // HEAD_DIM)
                         .astype(np.float32), BF16)
    in_specs, args = [], []
    for arrs, width in ((os_, MIX_WIDTH), (lses, LANES)):
        for arr, dil in zip(arrs, dils):
            if dil == 1:
                in_specs.append(pl.BlockSpec((None, tm, width), lambda bi, i: (bi, i, 0)))
            else:
                in_specs.append(pl.BlockSpec((None, dil, tm // dil, width),
                                             lambda bi, i: (bi, 0, i, 0)))
            args.append(arr)
    row = lambda bi, i: (bi, i, 0)
    in_specs += [pl.BlockSpec((None, tm, MIX_WIDTH), row),
                 pl.BlockSpec((LANES, MIX_WIDTH), lambda bi, i: (0, 0)),
                 pl.BlockSpec((MIX_WIDTH, d), lambda bi, i: (0, 0)),
                 pl.BlockSpec((None, tm, d), row)]
    args += [gate, expand, w_out.astype(BF16), x]
    return pl.pallas_call(
        functools.partial(_dil_combine_kernel, dils),
        grid=(b, s // tm),
        in_specs=in_specs,
        out_specs=pl.BlockSpec((None, tm, d), row),
        out_shape=jax.ShapeDtypeStruct((b, s, d), F32),
        scratch_shapes=[pltpu.VMEM((tm, LANES), F32), pltpu.VMEM((tm, LANES), F32)],
        compiler_params=_cparams(("arbitrary", "arbitrary")),
        name="dil_combine_out_proj",
    )(*args)


N_BIAS_PIECES = 3


def _fox_decay_kernel(f_ref, b_ref, kb_ref):
    s = f_ref.shape[0]
    x = f_ref[...] + b_ref[...]
    logf = jnp.minimum(x, 0.0) - jnp.log1p(jnp.exp(-jnp.abs(x)))
    r = lax.broadcasted_iota(jnp.int32, (BLK, BLK), 0)
    c = lax.broadcasted_iota(jnp.int32, (BLK, BLK), 1)
    tri = (c <= r).astype(F32)
    lane = lax.broadcasted_iota(jnp.int32, (BLK, LANES), 1)
    carry = jnp.zeros((1, LANES), F32)
    for i in range(s // BLK):
        blk = logf[i * BLK:(i + 1) * BLK, :]
        cs = jnp.dot(tri, blk, preferred_element_type=F32, precision=lax.Precision.HIGHEST) + carry
        carry = cs[BLK - 1:BLK, :]
        rest = cs * (-LOG2E)
        out = jnp.zeros((BLK, LANES), F32)
        for j in range(N_BIAS_PIECES):
            piece = rest.astype(BF16).astype(F32)
            rest = rest - piece
            moved = piece if j == 0 else pltpu.roll(piece, N_HEADS * j, 1)
            out = jnp.where((lane >= N_HEADS * j) & (lane < N_HEADS * (j + 1)), moved, out)
        kb_ref[i * BLK:(i + 1) * BLK, :] = out.astype(BF16)


def _fox_decay(f_logit, b_f):
    b, s, _ = f_logit.shape
    b_pad = jnp.pad(b_f.astype(F32), (0, LANES - N_HEADS))[None, :]
    return pl.pallas_call(
        _fox_decay_kernel,
        grid=(b,),
        in_specs=[pl.BlockSpec((None, s, LANES), lambda bi: (bi, 0, 0)),
                  pl.BlockSpec((1, LANES), lambda bi: (0, 0))],
        out_specs=pl.BlockSpec((None, s, LANES), lambda bi: (bi, 0, 0)),
        out_shape=jax.ShapeDtypeStruct((b, s, LANES), BF16),
        compiler_params=_cparams(("arbitrary",)),
        name="fox_decay",
    )(f_logit, b_pad)


FOX_T = PROJ_ROWS
FOX_HEADS_PER_STEP = 4


def _fox_kernel(qt_ref, k_ref, kb_ref, vt_ref, gate_ref, o_ref, m_sc, acc_sc, st_sc):
    pi = pl.program_id(1)
    i = pl.program_id(2)
    t = FOX_T
    heads = range(FOX_HEADS_PER_STEP)
    row = lax.broadcasted_iota(jnp.int32, (LANES, t), 0)
    zeros = jnp.zeros((HEAD_DIM, t), BF16)
    qx = []
    for e in heads:
        h = FOX_HEADS_PER_STEP * pi + e
        pick = (row % N_HEADS == h) & (row < N_HEADS * N_BIAS_PIECES)
        sel = jnp.where(pick, 1.0, 0.0).astype(BF16)
        mine = qt_ref[0, e * HEAD_DIM:(e + 1) * HEAD_DIM, :]
        top = [mine, zeros] if e % 2 == 0 else [zeros, mine]
        qx.append(jnp.concatenate(top + [sel], axis=0))
        m_sc[e] = jnp.full((1, t), MASK_VALUE, F32)
        acc_sc[e] = jnp.zeros((ACC_ROWS, t), F32)

    tk = t // 2

    def scores(jt, hf, e):
        k0 = pl.multiple_of(jt * t + hf * tk, tk)
        pair = slice((e // 2) * LANES, (e // 2 + 1) * LANES)
        kx = jnp.concatenate([k_ref[pl.ds(k0, tk), pair], kb_ref[pl.ds(k0, tk), :]], axis=1)
        st_sc[e, hf] = _dot(kx, qx[e])

    def consume(jt, hf, e, diag):
        st = st_sc[e, hf]
        if diag:
            ok = (lax.broadcasted_iota(jnp.int32, (tk, t), 0) + hf * tk
                  <= lax.broadcasted_iota(jnp.int32, (tk, t), 1))
            st = jnp.where(ok, st, MASK_VALUE)
        m_old = m_sc[e]
        m_new = jnp.maximum(m_old, jnp.max(st, axis=0, keepdims=True))
        pt = jnp.exp2(st - m_new).astype(BF16)
        alpha = jnp.exp2(m_old - m_new)
        vt = vt_ref[jt, e * HEAD_DIM:(e + 1) * HEAD_DIM, hf * tk:(hf + 1) * tk]
        acc_sc[e] = alpha * acc_sc[e] + _dot(_with_ones(vt), pt)
        m_sc[e] = m_new

    for e in heads:
        scores(0, 0, e)

    def body(jt):
        for e in heads:
            scores(jt, 1, e)
            consume(jt, 0, e, False)
        for e in heads:
            scores(jt + 1, 0, e)
            consume(jt, 1, e, False)

    _loop_in_pairs(i, body)
    for e in heads:
        scores(i, 1, e)
        consume(i, 0, e, True)
    for e in heads:
        consume(i, 1, e, True)
    for p in range(FOX_HEADS_PER_STEP // 2):
        outs = []
        for e in (2 * p, 2 * p + 1):
            acc = acc_sc[e]
            outs.append(acc[:HEAD_DIM] / jnp.maximum(acc[HEAD_DIM:HEAD_DIM + 1], 1e-30))
        sl = slice(p * LANES, (p + 1) * LANES)
        out = jnp.concatenate(outs, axis=0).T
        o_ref[:, sl] = (out * _silu(gate_ref[:, sl].astype(F32))).astype(o_ref.dtype)


def _fox_attention(qt, k, kb, vt, gate):
    b, s, _ = k.shape
    t = FOX_T
    assert s % t == 0 and vt.shape == (b, s // t, MIX_WIDTH, t) and qt.shape == vt.shape
    tile = lambda bi, p, i: (bi, i, p)
    nh = FOX_HEADS_PER_STEP
    wd = nh * HEAD_DIM
    return pl.pallas_call(
        _fox_kernel,
        grid=(b, N_HEADS // nh, s // t),
        in_specs=[pl.BlockSpec((None, 1, wd, t), lambda bi, p, i: (bi, i, p, 0)),
                  pl.BlockSpec((None, s, wd), lambda bi, p, i: (bi, 0, p)),
                  pl.BlockSpec((None, s, LANES), lambda bi, p, i: (bi, 0, 0)),
                  pl.BlockSpec((None, s // t, wd, t), lambda bi, p, i: (bi, 0, p, 0)),
                  pl.BlockSpec((None, t, wd), tile)],
        out_specs=pl.BlockSpec((None, t, wd), tile),
        out_shape=jax.ShapeDtypeStruct((b, s, MIX_WIDTH), BF16),
        scratch_shapes=[pltpu.VMEM((nh, 1, t), F32), pltpu.VMEM((nh, ACC_ROWS, t), F32),
                        pltpu.VMEM((nh, 2, t // 2, t), F32)],
        compiler_params=_cparams(("arbitrary", "arbitrary", "arbitrary")),
        name="fox_attn",
    )(qt, k, kb, vt, gate)


def _gelu_tanh(x):
    return 0.5 * x * (1.0 + jnp.tanh(math.sqrt(2.0 / math.pi) * (x + 0.044715 * (x * x * x))))


def _nsa_compress_kernel(ak_ref, av_ref, pe_ref, wa_ref, wb_ref, w2_ref, w2t_ref, c_ref, s_ref,
                         kc_ref, vct_ref):
    nrow = ak_ref.shape[0]
    for idx, a_ref in enumerate((ak_ref, av_ref)):
        a = a_ref[...].astype(F32)
        xa = (a + pe_ref[idx, 0:1, :]).astype(BF16)
        xb = (a + pe_ref[idx, 1:2, :]).astype(BF16)
        ya = _dot(xa, wa_ref[idx])
        yb = _dot(xb, wb_ref[idx])
        hid = _gelu_tanh(ya + pltpu.roll(yb, nrow - 1, 0)).astype(BF16)
        if idx == 0:
            y = _dot(hid, w2_ref[...])
            kc_ref[...] = _apply_rope(y, c_ref[...], s_ref[...]).astype(kc_ref.dtype)
        else:
            vct_ref[...] = _dot_nt(w2t_ref[...], hid).astype(vct_ref.dtype)


def _nsa_compress(kc, vc, pe_k, w1_k, w2_k, pe_v, w1_v, w2_v, rope_c, rope_s):
    b, s, _ = kc.shape
    ns = s // CMP_STRIDE
    g = NSA_KV
    flat = CMP_STRIDE * g * HEAD_DIM

    def w1_halves(w1):
        w1r = w1.reshape(2, CMP_STRIDE, HEAD_DIM, CMP_HIDDEN)
        outs = []
        for hf in range(2):
            z = jnp.einsum('ldc,gh->lgdhc', w1r[hf], jnp.eye(g, dtype=F32))
            outs.append(z.reshape(flat, g * CMP_HIDDEN))
        return outs

    def pe_halves(pe):
        per = pe.reshape(2, CMP_STRIDE, 1, HEAD_DIM)
        return jnp.broadcast_to(per, (2, CMP_STRIDE, g, HEAD_DIM)).reshape(2, flat)

    def w2_bd(w2):
        z = jnp.einsum('cd,gh->gchd', w2, jnp.eye(g, dtype=F32))
        return z.reshape(g * CMP_HIDDEN, g * HEAD_DIM)

    ka, kb = w1_halves(w1_k)
    va, vb = w1_halves(w1_v)
    wa = jnp.stack([ka, va]).astype(BF16)
    wb = jnp.stack([kb, vb]).astype(BF16)
    w2 = w2_bd(w2_k).astype(BF16)
    w2t = w2_bd(w2_v).T.astype(BF16)
    pe = jnp.stack([pe_halves(pe_k), pe_halves(pe_v)]).astype(F32)
    whole = lambda bi: (0, 0, 0)
    per_b = lambda bi: (bi, 0, 0)
    return pl.pallas_call(
        _nsa_compress_kernel,
        grid=(b,),
        in_specs=[pl.BlockSpec((None, ns, flat), per_b),
                  pl.BlockSpec((None, ns, flat), per_b),
                  pl.BlockSpec((2, 2, flat), whole),
                  pl.BlockSpec((2, flat, g * CMP_HIDDEN), whole),
                  pl.BlockSpec((2, flat, g * CMP_HIDDEN), whole),
                  pl.BlockSpec((g * CMP_HIDDEN, LANES), lambda bi: (0, 0)),
                  pl.BlockSpec((LANES, g * CMP_HIDDEN), lambda bi: (0, 0)),
                  pl.BlockSpec((None, ns, LANES), per_b),
                  pl.BlockSpec((None, ns, LANES), per_b)],
        out_specs=[pl.BlockSpec((None, ns, LANES), per_b),
                   pl.BlockSpec((None, LANES, ns), per_b)],
        out_shape=[jax.ShapeDtypeStruct((b, ns, LANES), BF16),
                   jax.ShapeDtypeStruct((b, LANES, ns), BF16)],
        compiler_params=_cparams(("arbitrary",)),
        name="nsa_compress",
    )(kc.reshape(b, ns, flat), vc.reshape(b, ns, flat), pe, wa, wb, w2, w2t,
      rope_c.reshape(b, ns, LANES), rope_s.reshape(b, ns, LANES))


NSA_QB = 2
NSA_TK = 256
NSA_REP = N_HEADS // NSA_KV


def _nsa_kernel(n_cmp, n_sel, n_win, *refs):
    n_wblk = n_win + NSA_QB - 1
    q_ref, kc_ref, vct_ref, ks_ref, vst_ref = refs[:5]
    kw_refs = refs[5:5 + n_wblk]
    vwt_refs = refs[5 + n_wblk:5 + 2 * n_wblk]
    gl_ref, gate_ref, ovt_ref, blk_ref, o_ref = refs[5 + 2 * n_wblk:10 + 2 * n_wblk]
    st_sc, stc_sc, stw_sc, m_sc, acc_sc, ocmp_sc, owin_sc = refs[10 + 2 * n_wblk:]

    n = pl.program_id(1)
    tq, tk, rep = BLK, NSA_TK, NSA_REP
    streams = [(qb, g) for qb in range(NSA_QB) for g in range(NSA_KV)]
    ncp = kc_ref.shape[0]
    t_lane = [(n * NSA_QB + qb) * tq + lax.broadcasted_iota(jnp.int32, (1, tq), 1)
              for qb in range(NSA_QB)]
    qg = [_group_queries(q_ref.at[qb * tq:(qb + 1) * tq], g, rep) for qb, g in streams]

    ci = lax.broadcasted_iota(jnp.int32, (ncp, tq), 0)
    kwn = n_win * BLK
    krow = lax.broadcasted_iota(jnp.int32, (kwn, tq), 0) - (n_win - 1) * BLK
    dist = lax.broadcasted_iota(jnp.int32, (kwn, tq), 1) - krow
    band = (dist >= 0) & (dist <= NSA_WINDOW - 1)
    sel_rows = ovt_ref.shape[0]
    rowi = lax.broadcasted_iota(jnp.int32, (sel_rows, tq), 0)
    rowf = rowi.astype(F32)

    def cmp_scores(sg):
        stc_sc[sg] = _dot_nt(kc_ref[...], qg[sg])

    def cmp_consume(sg):
        qb, g = streams[sg]
        cmask = (ci * CMP_STRIDE + (CMP_LEN - 1) <= t_lane[qb]) & (ci < n_cmp)
        ps = []
        for r in range(rep):
            s_r = jnp.where(cmask, stc_sc[sg, :, r * tq:(r + 1) * tq], MASK_VALUE)
            m = jnp.max(s_r, axis=0, keepdims=True)
            pr = jnp.where(cmask, jnp.exp2(s_r - m), 0.0)
            ps.append(pr.astype(BF16))
        lhs = jnp.concatenate([_with_ones(vct_ref[g * HEAD_DIM:(g + 1) * HEAD_DIM, :]),
                               ovt_ref[...]], axis=0)
        res = _dot(lhs, jnp.concatenate(ps, axis=1))
        inv = 1.0 / jnp.maximum(res[HEAD_DIM:HEAD_DIM + 1], 1e-30)
        ocmp_sc[sg] = res[:HEAD_DIM] * inv
        imp = None
        for r in range(rep):
            ql = slice(r * tq, (r + 1) * tq)
            t = res[ACC_ROWS:, ql] * inv[:, ql]
            imp = t if imp is None else imp + t
        return imp

    def select(sg, imp):
        cur = t_lane[streams[sg][0]] // SEL_LEN
        forced = (rowi == 0) | (rowi == cur)
        causal = rowi <= cur
        score = jnp.where(causal, jnp.where(forced, FORCED_SCORE, imp), MASK_VALUE)
        score = jnp.where(rowi < n_sel, score, PAD_SCORE)
        chosen = jnp.zeros((sel_rows, tq), F32)
        for _ in range(min(SEL_TOPK, n_sel)):
            mx = jnp.max(score, axis=0, keepdims=True)
            first = jnp.min(jnp.where(score == mx, rowf, float(LANES)), axis=0, keepdims=True)
            hit = rowf == first
            chosen = jnp.where(hit, 1.0, chosen)
            score = jnp.where(hit, PAD_SCORE, score)
        bias_t = jnp.where(causal & (chosen > 0.5), 0.0, MASK_VALUE)
        bias_t = jnp.concatenate([bias_t, jnp.zeros((LANES - sel_rows, tq), F32)], axis=0)
        bias = bias_t.T.astype(BF16)
        return jnp.concatenate([qg[sg], jnp.concatenate([bias] * rep, axis=0)], axis=1)

    def win_scores(sg):
        qb = streams[sg][0]
        kwcat = jnp.concatenate([kr[...] for kr in kw_refs[qb:qb + n_win]], axis=0)
        stw_sc[sg] = _dot_nt(kwcat, qg[sg])

    def win_consume(sg):
        qb, g = streams[sg]
        wok = band & ((n * NSA_QB + qb) * BLK + krow >= 0)
        vwt = jnp.concatenate([vr[0] for vr in vwt_refs[qb:qb + n_win]], axis=1)
        pts = []
        for r in range(rep):
            s_r = jnp.where(wok, stw_sc[sg, :, r * tq:(r + 1) * tq], MASK_VALUE)
            m = jnp.max(s_r, axis=0, keepdims=True)
            pts.append(jnp.exp2(s_r - m).astype(BF16))
        owin_sc[sg] = _dot(_with_ones(vwt[g * HEAD_DIM:(g + 1) * HEAD_DIM, :]),
                           jnp.concatenate(pts, axis=1))

    ns = len(streams)
    cmp_scores(0)
    cmp_scores(1)
    imps = []
    for sg in range(ns):
        if sg + 2 < ns:
            cmp_scores(sg + 2)
        else:
            win_scores(sg + 2 - ns)
        imps.append(cmp_consume(sg))
    qsel = []
    for sg in range(ns):
        if sg + 2 < ns:
            win_scores(sg + 2)
        qsel.append(select(sg, imps[sg]))
        win_consume(sg)

    for sg in range(ns):
        m_sc[sg] = jnp.full((1, rep * tq), MASK_VALUE, F32)
        acc_sc[sg] = jnp.zeros((ACC_ROWS, rep * tq), F32)

    hrep = rep // 2
    wide = hrep * tq
    units = [(sg, hf) for sg in range(ns) for hf in range(2)]
    nu = len(units)

    def scores(c, u):
        sg, hf = units[u]
        k0 = pl.multiple_of(c * tk, tk)
        kx = jnp.concatenate([ks_ref[pl.ds(k0, tk), :], blk_ref[pl.ds(k0, tk), :]], axis=1)
        st_sc[u] = _dot_nt(kx, qsel[sg][hf * wide:(hf + 1) * wide, :])

    def consume(c, u, last):
        sg, hf = units[u]
        qb, g = streams[sg]
        lanes = slice(hf * wide, (hf + 1) * wide)
        if last:
            tok_ok = c * tk + lax.broadcasted_iota(jnp.int32, (tk, tq), 0) <= t_lane[qb]
        m_old = m_sc[sg, :, lanes]
        ms, pts = [], []
        for r in range(hrep):
            s_r = st_sc[u, :, r * tq:(r + 1) * tq]
            if last:
                s_r = jnp.where(tok_ok, s_r, MASK_VALUE)
            m_new = jnp.maximum(m_old[:, r * tq:(r + 1) * tq], jnp.max(s_r, axis=0, keepdims=True))
            pts.append(jnp.exp2(s_r - m_new).astype(BF16))
            ms.append(m_new)
        m_new = jnp.concatenate(ms, axis=1)
        alpha = jnp.exp2(m_old - m_new)
        vt = vst_ref[c, g * HEAD_DIM:(g + 1) * HEAD_DIM, :]
        acc_sc[sg, :, lanes] = (alpha * acc_sc[sg, :, lanes]
                                + _dot(_with_ones(vt), jnp.concatenate(pts, axis=1)))
        m_sc[sg, :, lanes] = m_new

    n_steps = ((n * NSA_QB + 1) * tq + tk - 1) // tk
    scores(0, 0)
    scores(0, 1)

    def body(c):
        for u in range(nu):
            if u + 2 < nu:
                scores(c, u + 2)
            else:
                scores(c + 1, u + 2 - nu)
            consume(c, u, False)

    _loop_in_pairs(n_steps - 1, body)
    for u in range(nu):
        if u + 2 < nu:
            scores(n_steps - 1, u + 2)
        consume(n_steps - 1, u, True)

    for qb in range(NSA_QB):
        tok = slice(qb * tq, (qb + 1) * tq)
        gate_t = (1.0 / (1.0 + jnp.exp(-gl_ref[tok, :]))).T
        for p in range(N_PAIRS):
            sl = slice(p * LANES, (p + 1) * LANES)
            g = (2 * p) // rep
            sg = qb * NSA_KV + g
            rows = []
            for e in range(2):
                h = 2 * p + e
                ql = slice((h - g * rep) * tq, (h - g * rep + 1) * tq)
                slc, win = acc_sc[sg][:, ql], owin_sc[sg][:, ql]
                branches = (ocmp_sc[sg][:, ql],
                            slc[:HEAD_DIM] / jnp.maximum(slc[HEAD_DIM:HEAD_DIM + 1], 1e-30),
                            win[:HEAD_DIM] / win[HEAD_DIM:HEAD_DIM + 1])
                out = None
                for j, br in enumerate(branches):
                    t = gate_t[3 * h + j:3 * h + j + 1, :] * br
                    out = t if out is None else out + t
                rows.append(out)
            pair = jnp.concatenate(rows, axis=0).T
            o_ref[tok, sl] = (pair * _silu(gate_ref[tok, sl].astype(F32))).astype(o_ref.dtype)


def _selection_overlap_t(n_cmp_pad, n_cmp, n_sel):
    cs = np.arange(n_cmp_pad) * CMP_STRIDE
    js = np.arange(LANES) * SEL_LEN
    ov = np.minimum(cs[None, :] + CMP_LEN, js[:, None] + SEL_LEN) - np.maximum(cs[None, :], js[:, None])
    ov = (np.clip(ov, 0, None) / CMP_LEN).astype(np.float32)
    ov[:, n_cmp:] = 0.0
    ov[n_sel:, :] = 0.0
    return ov


def _nsa_attention(q, k_cmp, v_cmp_t, ks, vs_t, kw, vw_t, g_logit, gate):
    b, s, _ = q.shape
    tq = NSA_QB * BLK
    ncp = k_cmp.shape[1]
    n_cmp = s // CMP_STRIDE - 1
    n_sel = s // SEL_LEN
    n_win = -(-(NSA_WINDOW - 1) // BLK) + 1
    assert n_sel <= LANES and s % NSA_TK == 0 and tq == NSA_TK
    sel_rows = -(-n_sel // 16) * 16
    ov_t = jnp.asarray(_selection_overlap_t(ncp, n_cmp, n_sel)[:sel_rows], BF16)
    row = lambda bi, i: (bi, i, 0)
    per_b = lambda bi, i: (bi, 0, 0)
    in_specs = [pl.BlockSpec((None, tq, MIX_WIDTH), row),
                pl.BlockSpec((None, ncp, LANES), per_b),
                pl.BlockSpec((None, LANES, ncp), per_b),
                pl.BlockSpec((None, s, LANES), per_b),
                pl.BlockSpec((None, s // NSA_TK, LANES, NSA_TK), lambda bi, i: (bi, 0, 0, 0))]
    args = [q, k_cmp, v_cmp_t, ks, vs_t]
    for j in range(n_win - 1, -NSA_QB, -1):
        in_specs.append(pl.BlockSpec(
            (None, BLK, LANES), lambda bi, i, j=j: (bi, jnp.maximum(NSA_QB * i - j, 0), 0)))
        args.append(kw)
    for j in range(n_win - 1, -NSA_QB, -1):
        in_specs.append(pl.BlockSpec(
            (None, 1, LANES, BLK), lambda bi, i, j=j: (bi, jnp.maximum(NSA_QB * i - j, 0), 0, 0)))
        args.append(vw_t)
    block_onehot = jnp.asarray(
        (np.arange(s)[:, None] // SEL_LEN == np.arange(LANES)[None, :]).astype(np.float32), BF16)
    in_specs += [pl.BlockSpec((None, tq, LANES), row),
                 pl.BlockSpec((None, tq, MIX_WIDTH), row),
                 pl.BlockSpec((sel_rows, ncp), lambda bi, i: (0, 0)),
                 pl.BlockSpec((s, LANES), lambda bi, i: (0, 0))]
    args += [g_logit, gate, ov_t, block_onehot]
    wide = NSA_REP * BLK
    ns = NSA_QB * NSA_KV
    return pl.pallas_call(
        functools.partial(_nsa_kernel, n_cmp, n_sel, n_win),
        grid=(b, s // tq),
        in_specs=in_specs,
        out_specs=pl.BlockSpec((None, tq, MIX_WIDTH), row),
        out_shape=jax.ShapeDtypeStruct((b, s, MIX_WIDTH), BF16),
        scratch_shapes=[pltpu.VMEM((2 * ns, NSA_TK, wide // 2), F32),
                        pltpu.VMEM((ns, ncp, wide), F32),
                        pltpu.VMEM((ns, n_win * BLK, wide), F32),
                        pltpu.VMEM((ns, 1, wide), F32),
                        pltpu.VMEM((ns, ACC_ROWS, wide), F32),
                        pltpu.VMEM((ns, HEAD_DIM, wide), F32),
                        pltpu.VMEM((ns, ACC_ROWS, wide), F32)],
        compiler_params=_cparams(("arbitrary", "arbitrary")),
        name="nsa_attn",
    )(*args)


def _split(w, sizes):
    offs = np.cumsum([0] + list(sizes))
    return [w[:, int(offs[i]):int(offs[i + 1])] for i in range(len(sizes))]


def _swa_layer(x, rope, gain, w_in, sinks, w_out):
    kvw = SWA_KV * HEAD_DIM
    parts = _split(w_in, [MIX_WIDTH, kvw, kvw, MIX_WIDTH])
    segs = [_Seg(MIX_WIDTH, rope=True, scale=Q_SCALE), _Seg(kvw, rope=True), _Seg(kvw),
            _Seg(MIX_WIDTH)]
    q, k, v, gate = _norm_proj(x, gain, parts, segs, rope)
    o = _banded_attention(q, k, v, SWA_WINDOW - 1, sinks=sinks, gate=gate)
    return o, w_out


def _dilated_layer(x, rope, gain, w_in, w_out, residual=None):
    b, s, _ = x.shape
    kvw = DIL_KV * HEAD_DIM
    sizes, segs = [], []
    for window, dil in DIL_PATTERNS:
        assert s % (dil * BLK) == 0
        sizes += [MIX_WIDTH, kvw, kvw]
        segs += [_Seg(MIX_WIDTH, rope=True, scale=Q_SCALE, dil=dil), _Seg(kvw, rope=True, dil=dil),
                 _Seg(kvw, dil=dil)]
    sizes.append(MIX_WIDTH)
    segs.append(_Seg(MIX_WIDTH))
    res = _norm_proj(x, gain, _split(w_in, sizes), segs, rope, residual)
    if residual is not None:
        x, res = res[0], res[1:]
    gate = res[-1]
    os_, lses, dils = [], [], []
    for gi, (window, dil) in enumerate(DIL_PATTERNS):
        q, k, v = res[3 * gi:3 * gi + 3]
        if dil > 1:
            q, k, v = (t.reshape(b * dil, s // dil, t.shape[-1]) for t in (q, k, v))
        o, lse = _banded_attention(q, k, v, window // dil, want_lse=True)
        if dil > 1:
            o = o.reshape(b, dil, s // dil, MIX_WIDTH)
            lse = lse.reshape(b, dil, s // dil, LANES)
        os_.append(o)
        lses.append(lse)
        dils.append(dil)
    return _dil_combine_out_proj(os_, lses, gate, tuple(dils), w_out, x)


def _fox_layer(x, gain, w_in, b_f, w_out):
    sizes = [MIX_WIDTH, MIX_WIDTH, MIX_WIDTH, N_HEADS, MIX_WIDTH]
    segs = [_Seg(MIX_WIDTH, scale=Q_SCALE, tile=FOX_T), _Seg(MIX_WIDTH),
            _Seg(MIX_WIDTH, tile=FOX_T), _Seg(LANES, dtype=F32), _Seg(MIX_WIDTH)]
    qt, k, vt, f_logit, gate = _norm_proj(x, gain, _split(w_in, sizes), segs)
    kb = _fox_decay(f_logit, b_f)
    return _fox_attention(qt, k, kb, vt, gate), w_out


def _nsa_layer(x, positions, rope, gain, w_in, pe_k, w1_k, w2_k, pe_v, w1_v, w2_v, w_out,
               residual):
    b, s, _ = x.shape
    kvw = NSA_KV * HEAD_DIM
    sizes = [MIX_WIDTH] + [kvw] * 6 + [3 * N_HEADS, MIX_WIDTH]
    segs = [_Seg(MIX_WIDTH, rope=True, scale=Q_SCALE), _Seg(kvw), _Seg(kvw), _Seg(kvw, rope=True),
            _Seg(kvw, tile=NSA_TK), _Seg(kvw, rope=True), _Seg(kvw, tile=BLK),
            _Seg(LANES, dtype=F32), _Seg(MIX_WIDTH)]
    x, q, kc, vc, ks, vs_t, kw, vw_t, g_logit, gate = _norm_proj(
        x, gain, _split(w_in, sizes), segs, rope, residual)
    ns = s // CMP_STRIDE
    cmp_pos = jnp.concatenate(
        [positions[:, CMP_LEN - 1::CMP_STRIDE], positions[:, -1:]], axis=1)[:, :ns]
    cmp_c, cmp_s = _rope_tables(cmp_pos.reshape(-1))
    k_cmp, v_cmp_t = _nsa_compress(kc, vc, pe_k, w1_k, w2_k, pe_v, w1_v, w2_v, cmp_c, cmp_s)
    return x, _nsa_attention(q, k_cmp, v_cmp_t, ks, vs_t, kw, vw_t, g_logit, gate), w_out


def kernel(x, positions, norm_0, w_in_0, sinks_0, w_out_0, norm_1, w_in_1, w_out_1, norm_2, w_in_2, b_f_2, w_out_2, norm_3, w_in_3, cmp_pe_k_3, cmp_w1_k_3, cmp_w2_k_3, cmp_pe_v_3, cmp_w1_v_3, cmp_w2_v_3, w_out_3, final_norm):
    rope = _rope_tables(positions.reshape(-1))
    o, w = _swa_layer(x, rope, norm_0, w_in_0, sinks_0, w_out_0)
    x = _dilated_layer(x, rope, norm_1, w_in_1, w_out_1, residual=(o, w))
    o, w = _fox_layer(x, norm_2, w_in_2, b_f_2, w_out_2)
    x, o, w = _nsa_layer(x, positions, rope, norm_3, w_in_3, cmp_pe_k_3, cmp_w1_k_3, cmp_w2_k_3,
                         cmp_pe_v_3, cmp_w1_v_3, cmp_w2_v_3, w_out_3, residual=(o, w))
    return _out_proj(o, w, x, final_gain=final_norm)
```

```python
import functools
import math

import numpy as np
import jax
import jax.numpy as jnp
from jax import lax
from jax.experimental import pallas as pl
from jax.experimental.pallas import tpu as pltpu

HEAD_DIM = 64
N_HEADS = 16
N_PAIRS = N_HEADS // 2
MIX_WIDTH = N_HEADS * HEAD_DIM
ROT_DIM = HEAD_DIM // 4
ROT_HALF = ROT_DIM // 2
ROPE_THETA = 500000.0
BLK = 128
LANES = 128
NORM_EPS = 1e-6
MASK_VALUE = -1e30
PAD_SCORE = -3e38
LOG2E = math.log2(math.e)
Q_SCALE = HEAD_DIM ** -0.5 * LOG2E

SWA_KV = 4
SWA_WINDOW = 128
DIL_KV = 4
DIL_PATTERNS = ((128, 1), (512, 4), (2048, 16))
NSA_KV = 2
CMP_LEN = 32
CMP_STRIDE = 16
CMP_HIDDEN = 256
SEL_LEN = 64
SEL_TOPK = 8
NSA_WINDOW = 256
FORCED_SCORE = 1e4

VMEM_LIMIT_BYTES = 56 * 1024 * 1024
PROJ_ROWS = 512
PROJ_COLS = 512
ACC_ROWS = HEAD_DIM + 16
F32 = jnp.float32
BF16 = jnp.bfloat16


def _cparams(sem):
    return pltpu.CompilerParams(dimension_semantics=sem, vmem_limit_bytes=VMEM_LIMIT_BYTES)


def _lane_half(shape):
    return lax.broadcasted_iota(jnp.int32, shape, 1) // HEAD_DIM


def _swap_halves(t):
    return jnp.concatenate([t[:, HEAD_DIM:], t[:, :HEAD_DIM]], axis=1)


def _head_query(qp, e, kv_half):
    qh = jnp.where(_lane_half(qp.shape) == e, qp, jnp.zeros_like(qp))
    if e != kv_half:
        qh = _swap_halves(qh)
    return qh


def _group_queries(q_ref, g, rep):
    tiles = []
    for r in range(rep):
        h = g * rep + r
        tiles.append(_head_query(q_ref[:, (h // 2) * LANES:(h // 2 + 1) * LANES], h % 2, g % 2))
    return jnp.concatenate(tiles, axis=0)


def _dot_nt(a, b):
    return lax.dot_general(a, b, (((1,), (1,)), ((), ())), preferred_element_type=F32)


def _dot(a, b):
    return jnp.dot(a, b, preferred_element_type=F32)


def _silu(x):
    return x * (1.0 / (1.0 + jnp.exp(-x)))


def _loop_in_pairs(count, body):
    odd = count % 2

    @pl.when(odd == 1)
    def _():
        body(0)

    def two(jj, carry):
        body(odd + 2 * jj)
        body(odd + 2 * jj + 1)
        return carry

    lax.fori_loop(0, count // 2, two, 0)


def _with_ones(vt):
    return jnp.concatenate([vt, jnp.ones((ACC_ROWS - HEAD_DIM, vt.shape[1]), BF16)], axis=0)


def _rope_table_kernel(pos_ref, inv_ref, c_ref, s_ref):
    pos = pos_ref[...].astype(F32)
    ang = pos * inv_ref[...]
    d = lax.broadcasted_iota(jnp.int32, ang.shape, 1) % HEAD_DIM
    cos = jnp.cos(ang)
    sin = jnp.sin(ang)
    c_ref[...] = jnp.where(d < ROT_DIM, cos, 1.0)
    s_ref[...] = jnp.where(d < ROT_HALF, -sin, jnp.where(d < ROT_DIM, sin, 0.0))


def _rope_tables(pos_flat):
    t = pos_flat.shape[0]
    rows = min(t, 2048)
    assert t % rows == 0
    inv = jnp.power(ROPE_THETA, -jnp.arange(ROT_HALF, dtype=F32) / ROT_HALF)
    inv_l = jnp.tile(inv, LANES // ROT_HALF)[None, :]
    out = jax.ShapeDtypeStruct((t, LANES), F32)
    return pl.pallas_call(
        _rope_table_kernel,
        grid=(t // rows,),
        in_specs=[pl.BlockSpec((rows, 1), lambda i: (i, 0)),
                  pl.BlockSpec((1, LANES), lambda i: (0, 0))],
        out_specs=[pl.BlockSpec((rows, LANES), lambda i: (i, 0))] * 2,
        out_shape=[out, out],
        compiler_params=_cparams(("arbitrary",)),
        name="rope_tables",
    )(pos_flat[:, None], inv_l)


def _apply_rope(y, c, s):
    outs = []
    for j in range(y.shape[1] // LANES):
        t = y[:, j * LANES:(j + 1) * LANES]
        d = lax.broadcasted_iota(jnp.int32, t.shape, 1) % HEAD_DIM
        partner = jnp.where(d < ROT_HALF, pltpu.roll(t, LANES - ROT_HALF, 1),
                            pltpu.roll(t, ROT_HALF, 1))
        outs.append(t * c + partner * s)
    return outs[0] if len(outs) == 1 else jnp.concatenate(outs, axis=1)


class _Seg:
    def __init__(self, width, rope=False, scale=None, dtype=BF16, dil=1, tile=None):
        self.width, self.rope, self.scale, self.dtype, self.dil = width, rope, scale, dtype, dil
        self.tile = tile
        assert not (tile and dil > 1)


def _norm_proj_kernel(segs, use_rope, has_residual, *refs):
    has_t = any(sg.tile and not sg.rope for sg in segs)
    x_ref, g_ref, w_ref = refs[:3]
    k = 3
    if has_t:
        wt_ref = refs[k]
        k += 1
    if use_rope:
        c_ref, s_ref = refs[k:k + 2]
        k += 2
    if has_residual:
        po_ref, pw_ref = refs[k:k + 2]
        k += 2
        xnew_ref = refs[k]
        k += 1
    out_refs = refs[k:k + len(segs)]
    stage_ref = refs[k + len(segs)] if any(sg.dil > 1 for sg in segs) else None

    x = x_ref[...]
    if has_residual:
        x = x + _dot(po_ref[...], pw_ref[...])
        xnew_ref[...] = x
    var = jnp.mean(x * x, axis=-1, keepdims=True)
    h = (x * lax.rsqrt(var + NORM_EPS) * g_ref[...]).astype(BF16)
    rows = x.shape[0]
    col = 0
    tcol = 0
    for sg, o_ref in zip(segs, out_refs):
        if sg.tile and not sg.rope:
            for c0 in range(0, sg.width, PROJ_COLS):
                cw = min(PROJ_COLS, sg.width - c0)
                yt = _dot_nt(wt_ref[tcol + c0:tcol + c0 + cw, :], h)
                if sg.scale is not None:
                    yt = yt * sg.scale
                for ti in range(rows // sg.tile):
                    o_ref[ti, c0:c0 + cw, :] = yt[:, ti * sg.tile:(ti + 1) * sg.tile].astype(sg.dtype)
            tcol += sg.width
            continue
        for c0 in range(0, sg.width, PROJ_COLS):
            cw = min(PROJ_COLS, sg.width - c0)
            y = _dot(h, w_ref[:, col + c0:col + c0 + cw])
            if sg.rope:
                y = _apply_rope(y, c_ref[...], s_ref[...])
            if sg.scale is not None:
                y = y * sg.scale
            if sg.tile:
                yt = y.T
                for ti in range(rows // sg.tile):
                    o_ref[ti, c0:c0 + cw, :] = yt[:, ti * sg.tile:(ti + 1) * sg.tile].astype(sg.dtype)
            elif sg.dil > 1:
                sub = rows // sg.dil
                for j in range(cw // LANES):
                    stage_ref[j] = y[:, j * LANES:(j + 1) * LANES]
                for r in range(sg.dil):
                    for j in range(cw // LANES):
                        lo = c0 + j * LANES
                        o_ref[r, :, lo:lo + LANES] = (
                            stage_ref[j, pl.ds(r, sub, stride=sg.dil), :].astype(sg.dtype))
            else:
                o_ref[:, c0:c0 + cw] = y.astype(sg.dtype)
        col += sg.width


def _norm_proj(x, gain, w_parts, segs, rope=None, residual=None):
    b, s, d = x.shape
    tm = PROJ_ROWS
    assert s % tm == 0
    w_cols, wt_rows = [], []
    for wp, sg in zip(w_parts, segs):
        if wp.shape[1] < sg.width:
            wp = jnp.pad(wp, ((0, 0), (0, sg.width - wp.shape[1])))
        if sg.tile and not sg.rope:
            wt_rows.append(wp.T)
        else:
            w_cols.append(wp)
    w = jnp.concatenate(w_cols, axis=1).astype(BF16)
    n = w.shape[1]
    use_rope = rope is not None
    once = dict(pipeline_mode=pl.Buffered(1))
    in_specs = [pl.BlockSpec((None, tm, d), lambda bi, i: (bi, i, 0)),
                pl.BlockSpec((1, d), lambda bi, i: (0, 0)),
                pl.BlockSpec((d, n), lambda bi, i: (0, 0), **once)]
    args = [x, gain[None, :], w]
    if wt_rows:
        wt = jnp.concatenate(wt_rows, axis=0).astype(BF16)
        in_specs.append(pl.BlockSpec(wt.shape, lambda bi, i: (0, 0), **once))
        args.append(wt)
    if use_rope:
        in_specs += [pl.BlockSpec((None, tm, LANES), lambda bi, i: (bi, i, 0))] * 2
        args += [rope[0].reshape(b, s, LANES), rope[1].reshape(b, s, LANES)]
    out_specs, out_shape = [], []
    if residual is not None:
        po, pw = residual
        in_specs += [pl.BlockSpec((None, tm, po.shape[-1]), lambda bi, i: (bi, i, 0)),
                     pl.BlockSpec(pw.shape, lambda bi, i: (0, 0), **once)]
        args += [po, pw.astype(BF16)]
        out_shape.append(jax.ShapeDtypeStruct((b, s, d), F32))
        out_specs.append(pl.BlockSpec((None, tm, d), lambda bi, i: (bi, i, 0)))
    for sg in segs:
        if sg.dil > 1:
            assert tm % sg.dil == 0
            out_shape.append(jax.ShapeDtypeStruct((b, sg.dil, s // sg.dil, sg.width), sg.dtype))
            out_specs.append(pl.BlockSpec((None, sg.dil, tm // sg.dil, sg.width),
                                          lambda bi, i: (bi, 0, i, 0)))
        elif sg.tile:
            assert tm % sg.tile == 0
            out_shape.append(jax.ShapeDtypeStruct((b, s // sg.tile, sg.width, sg.tile), sg.dtype))
            out_specs.append(pl.BlockSpec((None, tm // sg.tile, sg.width, sg.tile),
                                          lambda bi, i: (bi, i, 0, 0)))
        else:
            out_shape.append(jax.ShapeDtypeStruct((b, s, sg.width), sg.dtype))
            out_specs.append(pl.BlockSpec((None, tm, sg.width), lambda bi, i: (bi, i, 0)))
    scratch = ([pltpu.VMEM((PROJ_COLS // LANES, tm, LANES), F32)]
               if any(sg.dil > 1 for sg in segs) else [])
    return pl.pallas_call(
        functools.partial(_norm_proj_kernel, segs, use_rope, residual is not None),
        grid=(b, s // tm),
        in_specs=in_specs,
        out_specs=out_specs,
        out_shape=out_shape,
        scratch_shapes=scratch,
        compiler_params=_cparams(("arbitrary", "arbitrary")),
        name="norm_proj",
    )(*args)


def _out_proj_kernel(final, *refs):
    if final:
        o_ref, w_ref, x_ref, g_ref, y_ref = refs
    else:
        o_ref, w_ref, x_ref, y_ref = refs
    y = x_ref[...] + _dot(o_ref[...], w_ref[...])
    if final:
        var = jnp.mean(y * y, axis=-1, keepdims=True)
        y = y * lax.rsqrt(var + NORM_EPS) * g_ref[...]
    y_ref[...] = y


def _out_proj(o, w, x, final_gain=None):
    b, s, d = x.shape
    m = o.shape[-1]
    tm = PROJ_ROWS
    final = final_gain is not None
    row = lambda bi, i: (bi, i, 0)
    in_specs = [pl.BlockSpec((None, tm, m), row),
                pl.BlockSpec((m, d), lambda bi, i: (0, 0)),
                pl.BlockSpec((None, tm, d), row)]
    args = [o, w.astype(BF16), x]
    if final:
        in_specs.append(pl.BlockSpec((1, d), lambda bi, i: (0, 0)))
        args.append(final_gain[None, :])
    return pl.pallas_call(
        functools.partial(_out_proj_kernel, final),
        grid=(b, s // tm),
        in_specs=in_specs,
        out_specs=pl.BlockSpec((None, tm, d), row),
        out_shape=jax.ShapeDtypeStruct((b, s, d), F32),
        compiler_params=_cparams(("arbitrary", "arbitrary")),
        name="out_proj",
    )(*args)


BANDED_MAX_QB = 2


def _banded_kernel(n_qb, max_dist, n_prev, n_kv, has_sink, has_gate, want_lse, *refs):
    q_ref = refs[0]
    n_blocks = n_qb + n_prev
    k_refs = refs[1:1 + n_blocks]
    v_refs = refs[1 + n_blocks:1 + 2 * n_blocks]
    k = 1 + 2 * n_blocks
    sink_ref = gate_ref = lse_ref = None
    if has_sink:
        sink_ref = refs[k]; k += 1
    if has_gate:
        gate_ref = refs[k]; k += 1
    o_ref = refs[k]; k += 1
    if want_lse:
        lse_ref = refs[k]; k += 1
    ot_sc = refs[k]
    st_sc = refs[k + 1]

    n = pl.program_id(1)
    tq = BLK
    kw = (n_prev + 1) * BLK
    rep = N_HEADS // n_kv
    krow = lax.broadcasted_iota(jnp.int32, (kw, tq), 0) - n_prev * BLK
    dist = lax.broadcasted_iota(jnp.int32, (kw, tq), 1) - krow
    band = (dist >= 0) & (dist <= max_dist)
    lse_rows = [[] for _ in range(n_qb)]
    units = [(qb, g) for qb in range(n_qb) for g in range(n_kv)]

    def scores(u):
        qb, g = units[u]
        sl = slice((g // 2) * LANES, (g // 2 + 1) * LANES)
        kcat = jnp.concatenate([kr[:, sl] for kr in k_refs[qb:qb + n_prev + 1]], axis=0)
        st_sc[u] = _dot_nt(kcat, _group_queries(q_ref.at[qb * tq:(qb + 1) * tq], g, rep))

    def consume(u):
        qb, g = units[u]
        sl = slice((g // 2) * LANES, (g // 2 + 1) * LANES)
        ok = band & ((n * n_qb + qb) * BLK + krow >= 0)
        vcat = jnp.concatenate([vr[:, sl] for vr in v_refs[qb:qb + n_prev + 1]], axis=0)
        vt = vcat.astype(F32).T.astype(BF16)
        ms, pts = [], []
        for r in range(rep):
            s_r = jnp.where(ok, st_sc[u, :, r * tq:(r + 1) * tq], MASK_VALUE)
            m = jnp.max(s_r, axis=0, keepdims=True)
            if has_sink:
                h = g * rep + r
                m = jnp.maximum(m, sink_ref[0:1, h:h + 1])
            pts.append(jnp.exp2(s_r - m).astype(BF16))
            ms.append(m)
        acc = _dot(_with_ones(vt[(g % 2) * HEAD_DIM:(g % 2 + 1) * HEAD_DIM, :]),
                   jnp.concatenate(pts, axis=1))
        for r in range(rep):
            h = g * rep + r
            den = acc[HEAD_DIM:HEAD_DIM + 1, r * tq:(r + 1) * tq]
            if has_sink:
                den = den + jnp.exp2(sink_ref[0:1, h:h + 1] - ms[r])
            ot_sc[qb * N_HEADS + h] = acc[:HEAD_DIM, r * tq:(r + 1) * tq] / den
            if want_lse:
                lse_rows[qb].append(ms[r] + jnp.log2(den))

    scores(0)
    scores(1)
    for u in range(len(units)):
        if u + 2 < len(units):
            scores(u + 2)
        consume(u)
    for qb in range(n_qb):
        rows = slice(qb * tq, (qb + 1) * tq)
        for p in range(N_PAIRS):
            sl = slice(p * LANES, (p + 1) * LANES)
            pair = jnp.concatenate([ot_sc[qb * N_HEADS + 2 * p],
                                    ot_sc[qb * N_HEADS + 2 * p + 1]], axis=0).T
            if has_gate:
                pair = pair * _silu(gate_ref[rows, sl].astype(F32))
            o_ref[rows, sl] = pair.astype(o_ref.dtype)
        if want_lse:
            lse_t = jnp.concatenate(
                lse_rows[qb] + [jnp.zeros((LANES - N_HEADS, tq), F32)], axis=0)
            lse_ref[rows, :] = lse_t.T


def _banded_attention(q, k, v, max_dist, sinks=None, gate=None, want_lse=False):
    bq, sq, _ = q.shape
    kvw = k.shape[-1]
    n_kv = kvw // HEAD_DIM
    n_prev = -(-max_dist // BLK)
    n_qb = BANDED_MAX_QB if sq % (BANDED_MAX_QB * BLK) == 0 else 1
    tq = n_qb * BLK
    assert sq % tq == 0
    row = lambda b, i: (b, i, 0)
    in_specs = [pl.BlockSpec((None, tq, MIX_WIDTH), row)]
    args = [q]
    for arr in (k, v):
        for j in range(n_prev, -n_qb, -1):
            in_specs.append(pl.BlockSpec(
                (None, BLK, kvw), lambda b, i, j=j: (b, jnp.maximum(n_qb * i - j, 0), 0)))
            args.append(arr)
    if sinks is not None:
        in_specs.append(pl.BlockSpec((1, LANES), lambda b, i: (0, 0)))
        args.append(jnp.pad(sinks.astype(F32) * LOG2E, (0, LANES - N_HEADS))[None, :])
    if gate is not None:
        in_specs.append(pl.BlockSpec((None, tq, MIX_WIDTH), row))
        args.append(gate)
    out_specs = [pl.BlockSpec((None, tq, MIX_WIDTH), row)]
    out_shape = [jax.ShapeDtypeStruct((bq, sq, MIX_WIDTH), BF16)]
    if want_lse:
        out_specs.append(pl.BlockSpec((None, tq, LANES), row))
        out_shape.append(jax.ShapeDtypeStruct((bq, sq, LANES), F32))
    res = pl.pallas_call(
        functools.partial(_banded_kernel, n_qb, max_dist, n_prev, n_kv, sinks is not None,
                          gate is not None, want_lse),
        grid=(bq, sq // tq),
        in_specs=in_specs,
        out_specs=out_specs,
        out_shape=out_shape,
        scratch_shapes=[pltpu.VMEM((n_qb * N_HEADS, HEAD_DIM, BLK), F32),
                        pltpu.VMEM((n_qb * n_kv, (n_prev + 1) * BLK,
                                    (N_HEADS // n_kv) * BLK), F32)],
        compiler_params=_cparams(("arbitrary", "arbitrary")),
        name="banded_attn",
    )(*args)
    return res if want_lse else res[0]


def _dil_combine_kernel(dils, *refs):
    ng = len(dils)
    o_refs, l_refs = refs[:ng], refs[ng:2 * ng]
    gate_ref, expand_ref, w_ref, x_ref, out_ref, stage_ref, lstage_ref = refs[2 * ng:]
    rows = out_ref.shape[0]

    def natural(ref, dil, stage, slab):
        sl = slice(slab * LANES, (slab + 1) * LANES)
        if dil == 1:
            return ref[:, sl].astype(F32)
        sub = rows // dil
        for r in range(dil):
            stage[pl.ds(r, sub, stride=dil), :] = ref[r, :, sl].astype(F32)
        return stage[...]

    lses = [natural(l_refs[i], dils[i], lstage_ref, 0) for i in range(ng)]
    mx = functools.reduce(jnp.maximum, lses)
    ws = [jnp.exp2(l - mx) for l in lses]
    tot = functools.reduce(lambda a, c: a + c, ws)
    inv = 1.0 / tot

    def widen(w):
        hi = w.astype(BF16)
        lo = (w - hi.astype(F32)).astype(BF16)
        return _dot(hi, expand_ref[...]) + _dot(lo, expand_ref[...])

    wide = [widen(w * inv) for w in ws[:-1]]
    wide.append(1.0 - functools.reduce(lambda a, c: a + c, wide))
    gated = []
    for p in range(N_PAIRS):
        sl = slice(p * LANES, (p + 1) * LANES)
        acc = None
        for gi in range(ng):
            t = wide[gi][:, sl] * natural(o_refs[gi], dils[gi], stage_ref, p)
            acc = t if acc is None else acc + t
        gated.append((acc * _silu(gate_ref[:, sl].astype(F32))).astype(BF16))
    out_ref[...] = x_ref[...] + _dot(jnp.concatenate(gated, axis=1), w_ref[...])


def _dil_combine_out_proj(os_, lses, gate, dils, w_out, x):
    b, s, d = x.shape
    tm = PROJ_ROWS
    expand = jnp.asarray((np.arange(LANES)[:, None] == np.arange(MIX_WIDTH)[None, :] // HEAD_DIM)
                         .astype(np.float32), BF16)
    in_specs, args = [], []
    for arrs, width in ((os_, MIX_WIDTH), (lses, LANES)):
        for arr, dil in zip(arrs, dils):
            if dil == 1:
                in_specs.append(pl.BlockSpec((None, tm, width), lambda bi, i: (bi, i, 0)))
            else:
                in_specs.append(pl.BlockSpec((None, dil, tm // dil, width),
                                             lambda bi, i: (bi, 0, i, 0)))
            args.append(arr)
    row = lambda bi, i: (bi, i, 0)
    in_specs += [pl.BlockSpec((None, tm, MIX_WIDTH), row),
                 pl.BlockSpec((LANES, MIX_WIDTH), lambda bi, i: (0, 0)),
                 pl.BlockSpec((MIX_WIDTH, d), lambda bi, i: (0, 0)),
                 pl.BlockSpec((None, tm, d), row)]
    args += [gate, expand, w_out.astype(BF16), x]
    return pl.pallas_call(
        functools.partial(_dil_combine_kernel, dils),
        grid=(b, s // tm),
        in_specs=in_specs,
        out_specs=pl.BlockSpec((None, tm, d), row),
        out_shape=jax.ShapeDtypeStruct((b, s, d), F32),
        scratch_shapes=[pltpu.VMEM((tm, LANES), F32), pltpu.VMEM((tm, LANES), F32)],
        compiler_params=_cparams(("arbitrary", "arbitrary")),
        name="dil_combine_out_proj",
    )(*args)


N_BIAS_PIECES = 3


def _fox_decay_kernel(f_ref, b_ref, kb_ref):
    s = f_ref.shape[0]
    x = f_ref[...] + b_ref[...]
    logf = jnp.minimum(x, 0.0) - jnp.log1p(jnp.exp(-jnp.abs(x)))
    r = lax.broadcasted_iota(jnp.int32, (BLK, BLK), 0)
    c = lax.broadcasted_iota(jnp.int32, (BLK, BLK), 1)
    tri = (c <= r).astype(F32)
    lane = lax.broadcasted_iota(jnp.int32, (BLK, LANES), 1)
    carry = jnp.zeros((1, LANES), F32)
    for i in range(s // BLK):
        blk = logf[i * BLK:(i + 1) * BLK, :]
        cs = jnp.dot(tri, blk, preferred_element_type=F32, precision=lax.Precision.HIGHEST) + carry
        carry = cs[BLK - 1:BLK, :]
        rest = cs * (-LOG2E)
        out = jnp.zeros((BLK, LANES), F32)
        for j in range(N_BIAS_PIECES):
            piece = rest.astype(BF16).astype(F32)
            rest = rest - piece
            moved = piece if j == 0 else pltpu.roll(piece, N_HEADS * j, 1)
            out = jnp.where((lane >= N_HEADS * j) & (lane < N_HEADS * (j + 1)), moved, out)
        kb_ref[i * BLK:(i + 1) * BLK, :] = out.astype(BF16)


def _fox_decay(f_logit, b_f):
    b, s, _ = f_logit.shape
    b_pad = jnp.pad(b_f.astype(F32), (0, LANES - N_HEADS))[None, :]
    return pl.pallas_call(
        _fox_decay_kernel,
        grid=(b,),
        in_specs=[pl.BlockSpec((None, s, LANES), lambda bi: (bi, 0, 0)),
                  pl.BlockSpec((1, LANES), lambda bi: (0, 0))],
        out_specs=pl.BlockSpec((None, s, LANES), lambda bi: (bi, 0, 0)),
        out_shape=jax.ShapeDtypeStruct((b, s, LANES), BF16),
        compiler_params=_cparams(("arbitrary",)),
        name="fox_decay",
    )(f_logit, b_pad)


FOX_T = PROJ_ROWS
FOX_HEADS_PER_STEP = 4


def _fox_kernel(qt_ref, k_ref, kb_ref, vt_ref, gate_ref, o_ref, m_sc, acc_sc, st_sc):
    pi = pl.program_id(1)
    i = pl.program_id(2)
    t = FOX_T
    heads = range(FOX_HEADS_PER_STEP)
    row = lax.broadcasted_iota(jnp.int32, (LANES, t), 0)
    zeros = jnp.zeros((HEAD_DIM, t), BF16)
    qx = []
    for e in heads:
        h = FOX_HEADS_PER_STEP * pi + e
        pick = (row % N_HEADS == h) & (row < N_HEADS * N_BIAS_PIECES)
        sel = jnp.where(pick, 1.0, 0.0).astype(BF16)
        mine = qt_ref[0, e * HEAD_DIM:(e + 1) * HEAD_DIM, :]
        top = [mine, zeros] if e % 2 == 0 else [zeros, mine]
        qx.append(jnp.concatenate(top + [sel], axis=0))
        m_sc[e] = jnp.full((1, t), MASK_VALUE, F32)
        acc_sc[e] = jnp.zeros((ACC_ROWS, t), F32)

    tk = t // 2

    def scores(jt, hf, e):
        k0 = pl.multiple_of(jt * t + hf * tk, tk)
        pair = slice((e // 2) * LANES, (e // 2 + 1) * LANES)
        kx = jnp.concatenate([k_ref[pl.ds(k0, tk), pair], kb_ref[pl.ds(k0, tk), :]], axis=1)
        st_sc[e, hf] = _dot(kx, qx[e])

    def consume(jt, hf, e, diag):
        st = st_sc[e, hf]
        if diag:
            ok = (lax.broadcasted_iota(jnp.int32, (tk, t), 0) + hf * tk
                  <= lax.broadcasted_iota(jnp.int32, (tk, t), 1))
            st = jnp.where(ok, st, MASK_VALUE)
        m_old = m_sc[e]
        m_new = jnp.maximum(m_old, jnp.max(st, axis=0, keepdims=True))
        pt = jnp.exp2(st - m_new).astype(BF16)
        alpha = jnp.exp2(m_old - m_new)
        vt = vt_ref[jt, e * HEAD_DIM:(e + 1) * HEAD_DIM, hf * tk:(hf + 1) * tk]
        acc_sc[e] = alpha * acc_sc[e] + _dot(_with_ones(vt), pt)
        m_sc[e] = m_new

    for e in heads:
        scores(0, 0, e)

    def body(jt):
        for e in heads:
            scores(jt, 1, e)
            consume(jt, 0, e, False)
        for e in heads:
            scores(jt + 1, 0, e)
            consume(jt, 1, e, False)

    _loop_in_pairs(i, body)
    for e in heads:
        scores(i, 1, e)
        consume(i, 0, e, True)
    for e in heads:
        consume(i, 1, e, True)
    for p in range(FOX_HEADS_PER_STEP // 2):
        outs = []
        for e in (2 * p, 2 * p + 1):
            acc = acc_sc[e]
            outs.append(acc[:HEAD_DIM] / jnp.maximum(acc[HEAD_DIM:HEAD_DIM + 1], 1e-30))
        sl = slice(p * LANES, (p + 1) * LANES)
        out = jnp.concatenate(outs, axis=0).T
        o_ref[:, sl] = (out * _silu(gate_ref[:, sl].astype(F32))).astype(o_ref.dtype)


def _fox_attention(qt, k, kb, vt, gate):
    b, s, _ = k.shape
    t = FOX_T
    assert s % t == 0 and vt.shape == (b, s // t, MIX_WIDTH, t) and qt.shape == vt.shape
    tile = lambda bi, p, i: (bi, i, p)
    nh = FOX_HEADS_PER_STEP
    wd = nh * HEAD_DIM
    return pl.pallas_call(
        _fox_kernel,
        grid=(b, N_HEADS // nh, s // t),
        in_specs=[pl.BlockSpec((None, 1, wd, t), lambda bi, p, i: (bi, i, p, 0)),
                  pl.BlockSpec((None, s, wd), lambda bi, p, i: (bi, 0, p)),
                  pl.BlockSpec((None, s, LANES), lambda bi, p, i: (bi, 0, 0)),
                  pl.BlockSpec((None, s // t, wd, t), lambda bi, p, i: (bi, 0, p, 0)),
                  pl.BlockSpec((None, t, wd), tile)],
        out_specs=pl.BlockSpec((None, t, wd), tile),
        out_shape=jax.ShapeDtypeStruct((b, s, MIX_WIDTH), BF16),
        scratch_shapes=[pltpu.VMEM((nh, 1, t), F32), pltpu.VMEM((nh, ACC_ROWS, t), F32),
                        pltpu.VMEM((nh, 2, t // 2, t), F32)],
        compiler_params=_cparams(("arbitrary", "arbitrary", "arbitrary")),
        name="fox_attn",
    )(qt, k, kb, vt, gate)


def _gelu_tanh(x):
    return 0.5 * x * (1.0 + jnp.tanh(math.sqrt(2.0 / math.pi) * (x + 0.044715 * (x * x * x))))


def _nsa_compress_kernel(ak_ref, av_ref, pe_ref, wa_ref, wb_ref, w2_ref, w2t_ref, c_ref, s_ref,
                         kc_ref, vct_ref):
    nrow = ak_ref.shape[0]
    for idx, a_ref in enumerate((ak_ref, av_ref)):
        a = a_ref[...].astype(F32)
        xa = (a + pe_ref[idx, 0:1, :]).astype(BF16)
        xb = (a + pe_ref[idx, 1:2, :]).astype(BF16)
        ya = _dot(xa, wa_ref[idx])
        yb = _dot(xb, wb_ref[idx])
        hid = _gelu_tanh(ya + pltpu.roll(yb, nrow - 1, 0)).astype(BF16)
        if idx == 0:
            y = _dot(hid, w2_ref[...])
            kc_ref[...] = _apply_rope(y, c_ref[...], s_ref[...]).astype(kc_ref.dtype)
        else:
            vct_ref[...] = _dot_nt(w2t_ref[...], hid).astype(vct_ref.dtype)


def _nsa_compress(kc, vc, pe_k, w1_k, w2_k, pe_v, w1_v, w2_v, rope_c, rope_s):
    b, s, _ = kc.shape
    ns = s // CMP_STRIDE
    g = NSA_KV
    flat = CMP_STRIDE * g * HEAD_DIM

    def w1_halves(w1):
        w1r = w1.reshape(2, CMP_STRIDE, HEAD_DIM, CMP_HIDDEN)
        outs = []
        for hf in range(2):
            z = jnp.einsum('ldc,gh->lgdhc', w1r[hf], jnp.eye(g, dtype=F32))
            outs.append(z.reshape(flat, g * CMP_HIDDEN))
        return outs

    def pe_halves(pe):
        per = pe.reshape(2, CMP_STRIDE, 1, HEAD_DIM)
        return jnp.broadcast_to(per, (2, CMP_STRIDE, g, HEAD_DIM)).reshape(2, flat)

    def w2_bd(w2):
        z = jnp.einsum('cd,gh->gchd', w2, jnp.eye(g, dtype=F32))
        return z.reshape(g * CMP_HIDDEN, g * HEAD_DIM)

    ka, kb = w1_halves(w1_k)
    va, vb = w1_halves(w1_v)
    wa = jnp.stack([ka, va]).astype(BF16)
    wb = jnp.stack([kb, vb]).astype(BF16)
    w2 = w2_bd(w2_k).astype(BF16)
    w2t = w2_bd(w2_v).T.astype(BF16)
    pe = jnp.stack([pe_halves(pe_k), pe_halves(pe_v)]).astype(F32)
    whole = lambda bi: (0, 0, 0)
    per_b = lambda bi: (bi, 0, 0)
    return pl.pallas_call(
        _nsa_compress_kernel,
        grid=(b,),
        in_specs=[pl.BlockSpec((None, ns, flat), per_b),
                  pl.BlockSpec((None, ns, flat), per_b),
                  pl.BlockSpec((2, 2, flat), whole),
                  pl.BlockSpec((2, flat, g * CMP_HIDDEN), whole),
                  pl.BlockSpec((2, flat, g * CMP_HIDDEN), whole),
                  pl.BlockSpec((g * CMP_HIDDEN, LANES), lambda bi: (0, 0)),
                  pl.BlockSpec((LANES, g * CMP_HIDDEN), lambda bi: (0, 0)),
                  pl.BlockSpec((None, ns, LANES), per_b),
                  pl.BlockSpec((None, ns, LANES), per_b)],
        out_specs=[pl.BlockSpec((None, ns, LANES), per_b),
                   pl.BlockSpec((None, LANES, ns), per_b)],
        out_shape=[jax.ShapeDtypeStruct((b, ns, LANES), BF16),
                   jax.ShapeDtypeStruct((b, LANES, ns), BF16)],
        compiler_params=_cparams(("arbitrary",)),
        name="nsa_compress",
    )(kc.reshape(b, ns, flat), vc.reshape(b, ns, flat), pe, wa, wb, w2, w2t,
      rope_c.reshape(b, ns, LANES), rope_s.reshape(b, ns, LANES))


NSA_QB = 2
NSA_TK = 256
NSA_REP = N_HEADS // NSA_KV


def _nsa_kernel(n_cmp, n_sel, n_win, *refs):
    n_wblk = n_win + NSA_QB - 1
    q_ref, kc_ref, vct_ref, ks_ref, vst_ref = refs[:5]
    kw_refs = refs[5:5 + n_wblk]
    vwt_refs = refs[5 + n_wblk:5 + 2 * n_wblk]
    gl_ref, gate_ref, ovt_ref, blk_ref, o_ref = refs[5 + 2 * n_wblk:10 + 2 * n_wblk]
    st_sc, stc_sc, stw_sc, m_sc, acc_sc, ocmp_sc, owin_sc = refs[10 + 2 * n_wblk:]

    n = pl.program_id(1)
    tq, tk, rep = BLK, NSA_TK, NSA_REP
    streams = [(qb, g) for qb in range(NSA_QB) for g in range(NSA_KV)]
    ncp = kc_ref.shape[0]
    t_lane = [(n * NSA_QB + qb) * tq + lax.broadcasted_iota(jnp.int32, (1, tq), 1)
              for qb in range(NSA_QB)]
    zeros = jnp.zeros((HEAD_DIM, tq), BF16)
    qg = []
    for qb, g in streams:
        tiles = []
        for r in range(rep):
            h = g * rep + r
            mine = q_ref[qb, h * HEAD_DIM:(h + 1) * HEAD_DIM, :]
            tiles.append(jnp.concatenate([mine, zeros] if g % 2 == 0 else [zeros, mine], axis=0))
        qg.append(jnp.concatenate(tiles, axis=1))

    ci = lax.broadcasted_iota(jnp.int32, (ncp, tq), 0)
    kwn = n_win * BLK
    krow = lax.broadcasted_iota(jnp.int32, (kwn, tq), 0) - (n_win - 1) * BLK
    dist = lax.broadcasted_iota(jnp.int32, (kwn, tq), 1) - krow
    band = (dist >= 0) & (dist <= NSA_WINDOW - 1)
    sel_rows = ovt_ref.shape[0]
    rowi = lax.broadcasted_iota(jnp.int32, (sel_rows, tq), 0)
    rowf = rowi.astype(F32)

    def cmp_scores(sg):
        stc_sc[sg] = _dot(kc_ref[...], qg[sg])

    def cmp_consume(sg):
        qb, g = streams[sg]
        cmask = (ci * CMP_STRIDE + (CMP_LEN - 1) <= t_lane[qb]) & (ci < n_cmp)
        ps = []
        for r in range(rep):
            s_r = jnp.where(cmask, stc_sc[sg, :, r * tq:(r + 1) * tq], MASK_VALUE)
            m = jnp.max(s_r, axis=0, keepdims=True)
            pr = jnp.where(cmask, jnp.exp2(s_r - m), 0.0)
            ps.append(pr.astype(BF16))
        lhs = jnp.concatenate([_with_ones(vct_ref[g * HEAD_DIM:(g + 1) * HEAD_DIM, :]),
                               ovt_ref[...]], axis=0)
        res = _dot(lhs, jnp.concatenate(ps, axis=1))
        inv = 1.0 / jnp.maximum(res[HEAD_DIM:HEAD_DIM + 1], 1e-30)
        ocmp_sc[sg] = res[:HEAD_DIM] * inv
        imp = None
        for r in range(rep):
            ql = slice(r * tq, (r + 1) * tq)
            t = res[ACC_ROWS:, ql] * inv[:, ql]
            imp = t if imp is None else imp + t
        return imp

    def select(sg, imp):
        cur = t_lane[streams[sg][0]] // SEL_LEN
        forced = (rowi == 0) | (rowi == cur)
        causal = rowi <= cur
        score = jnp.where(causal, jnp.where(forced, FORCED_SCORE, imp), MASK_VALUE)
        score = jnp.where(rowi < n_sel, score, PAD_SCORE)
        chosen = jnp.zeros((sel_rows, tq), F32)
        for _ in range(min(SEL_TOPK, n_sel)):
            mx = jnp.max(score, axis=0, keepdims=True)
            first = jnp.min(jnp.where(score == mx, rowf, float(LANES)), axis=0, keepdims=True)
            hit = rowf == first
            chosen = jnp.where(hit, 1.0, chosen)
            score = jnp.where(hit, PAD_SCORE, score)
        bias_t = jnp.where(causal & (chosen > 0.5), 0.0, MASK_VALUE)
        bias_t = jnp.concatenate([bias_t, jnp.zeros((LANES - sel_rows, tq), F32)], axis=0)
        bias_t = bias_t.astype(BF16)
        return jnp.concatenate([qg[sg], jnp.concatenate([bias_t] * rep, axis=1)], axis=0)

    def win_scores(sg):
        qb = streams[sg][0]
        kwcat = jnp.concatenate([kr[...] for kr in kw_refs[qb:qb + n_win]], axis=0)
        stw_sc[sg] = _dot(kwcat, qg[sg])

    def win_consume(sg):
        qb, g = streams[sg]
        wok = band & ((n * NSA_QB + qb) * BLK + krow >= 0)
        vwt = jnp.concatenate([vr[0] for vr in vwt_refs[qb:qb + n_win]], axis=1)
        pts = []
        for r in range(rep):
            s_r = jnp.where(wok, stw_sc[sg, :, r * tq:(r + 1) * tq], MASK_VALUE)
            m = jnp.max(s_r, axis=0, keepdims=True)
            pts.append(jnp.exp2(s_r - m).astype(BF16))
        owin_sc[sg] = _dot(_with_ones(vwt[g * HEAD_DIM:(g + 1) * HEAD_DIM, :]),
                           jnp.concatenate(pts, axis=1))

    ns = len(streams)
    cmp_scores(0)
    cmp_scores(1)
    imps = []
    for sg in range(ns):
        if sg + 2 < ns:
            cmp_scores(sg + 2)
        else:
            win_scores(sg + 2 - ns)
        imps.append(cmp_consume(sg))
    qsel = []
    for sg in range(ns):
        if sg + 2 < ns:
            win_scores(sg + 2)
        qsel.append(select(sg, imps[sg]))
        win_consume(sg)

    for sg in range(ns):
        m_sc[sg] = jnp.full((1, rep * tq), MASK_VALUE, F32)
        acc_sc[sg] = jnp.zeros((ACC_ROWS, rep * tq), F32)

    hrep = rep // 2
    wide = hrep * tq
    units = [(sg, hf) for sg in range(ns) for hf in range(2)]
    nu = len(units)

    def scores(c, u):
        sg, hf = units[u]
        k0 = pl.multiple_of(c * tk, tk)
        kx = jnp.concatenate([ks_ref[pl.ds(k0, tk), :], blk_ref[pl.ds(k0, tk), :]], axis=1)
        st_sc[u] = _dot(kx, qsel[sg][:, hf * wide:(hf + 1) * wide])

    def consume(c, u, last):
        sg, hf = units[u]
        qb, g = streams[sg]
        lanes = slice(hf * wide, (hf + 1) * wide)
        if last:
            tok_ok = c * tk + lax.broadcasted_iota(jnp.int32, (tk, tq), 0) <= t_lane[qb]
        m_old = m_sc[sg, :, lanes]
        ms, pts = [], []
        for r in range(hrep):
            s_r = st_sc[u, :, r * tq:(r + 1) * tq]
            if last:
                s_r = jnp.where(tok_ok, s_r, MASK_VALUE)
            m_new = jnp.maximum(m_old[:, r * tq:(r + 1) * tq], jnp.max(s_r, axis=0, keepdims=True))
            pts.append(jnp.exp2(s_r - m_new).astype(BF16))
            ms.append(m_new)
        m_new = jnp.concatenate(ms, axis=1)
        alpha = jnp.exp2(m_old - m_new)
        vt = vst_ref[c, g * HEAD_DIM:(g + 1) * HEAD_DIM, :]
        acc_sc[sg, :, lanes] = (alpha * acc_sc[sg, :, lanes]
                                + _dot(_with_ones(vt), jnp.concatenate(pts, axis=1)))
        m_sc[sg, :, lanes] = m_new

    n_steps = ((n * NSA_QB + 1) * tq + tk - 1) // tk
    scores(0, 0)
    scores(0, 1)

    def body(c):
        for u in range(nu):
            if u + 2 < nu:
                scores(c, u + 2)
            else:
                scores(c + 1, u + 2 - nu)
            consume(c, u, False)

    _loop_in_pairs(n_steps - 1, body)
    for u in range(nu):
        if u + 2 < nu:
            scores(n_steps - 1, u + 2)
        consume(n_steps - 1, u, True)

    for qb in range(NSA_QB):
        tok = slice(qb * tq, (qb + 1) * tq)
        gate_t = (1.0 / (1.0 + jnp.exp(-gl_ref[tok, :]))).T
        for p in range(N_PAIRS):
            sl = slice(p * LANES, (p + 1) * LANES)
            g = (2 * p) // rep
            sg = qb * NSA_KV + g
            rows = []
            for e in range(2):
                h = 2 * p + e
                ql = slice((h - g * rep) * tq, (h - g * rep + 1) * tq)
                slc, win = acc_sc[sg][:, ql], owin_sc[sg][:, ql]
                branches = (ocmp_sc[sg][:, ql],
                            slc[:HEAD_DIM] / jnp.maximum(slc[HEAD_DIM:HEAD_DIM + 1], 1e-30),
                            win[:HEAD_DIM] / win[HEAD_DIM:HEAD_DIM + 1])
                out = None
                for j, br in enumerate(branches):
                    t = gate_t[3 * h + j:3 * h + j + 1, :] * br
                    out = t if out is None else out + t
                rows.append(out)
            pair = jnp.concatenate(rows, axis=0).T
            o_ref[tok, sl] = (pair * _silu(gate_ref[tok, sl].astype(F32))).astype(o_ref.dtype)


def _selection_overlap_t(n_cmp_pad, n_cmp, n_sel):
    cs = np.arange(n_cmp_pad) * CMP_STRIDE
    js = np.arange(LANES) * SEL_LEN
    ov = np.minimum(cs[None, :] + CMP_LEN, js[:, None] + SEL_LEN) - np.maximum(cs[None, :], js[:, None])
    ov = (np.clip(ov, 0, None) / CMP_LEN).astype(np.float32)
    ov[:, n_cmp:] = 0.0
    ov[n_sel:, :] = 0.0
    return ov


def _nsa_attention(q_t, k_cmp, v_cmp_t, ks, vs_t, kw, vw_t, g_logit, gate):
    b, s, _ = gate.shape
    tq = NSA_QB * BLK
    ncp = k_cmp.shape[1]
    n_cmp = s // CMP_STRIDE - 1
    n_sel = s // SEL_LEN
    n_win = -(-(NSA_WINDOW - 1) // BLK) + 1
    assert n_sel <= LANES and s % NSA_TK == 0 and tq == NSA_TK
    sel_rows = -(-n_sel // 16) * 16
    ov_t = jnp.asarray(_selection_overlap_t(ncp, n_cmp, n_sel)[:sel_rows], BF16)
    row = lambda bi, i: (bi, i, 0)
    per_b = lambda bi, i: (bi, 0, 0)
    in_specs = [pl.BlockSpec((None, NSA_QB, MIX_WIDTH, BLK), lambda bi, i: (bi, i, 0, 0)),
                pl.BlockSpec((None, ncp, LANES), per_b),
                pl.BlockSpec((None, LANES, ncp), per_b),
                pl.BlockSpec((None, s, LANES), per_b),
                pl.BlockSpec((None, s // NSA_TK, LANES, NSA_TK), lambda bi, i: (bi, 0, 0, 0))]
    args = [q_t, k_cmp, v_cmp_t, ks, vs_t]
    for j in range(n_win - 1, -NSA_QB, -1):
        in_specs.append(pl.BlockSpec(
            (None, BLK, LANES), lambda bi, i, j=j: (bi, jnp.maximum(NSA_QB * i - j, 0), 0)))
        args.append(kw)
    for j in range(n_win - 1, -NSA_QB, -1):
        in_specs.append(pl.BlockSpec(
            (None, 1, LANES, BLK), lambda bi, i, j=j: (bi, jnp.maximum(NSA_QB * i - j, 0), 0, 0)))
        args.append(vw_t)
    block_onehot = jnp.asarray(
        (np.arange(s)[:, None] // SEL_LEN == np.arange(LANES)[None, :]).astype(np.float32), BF16)
    in_specs += [pl.BlockSpec((None, tq, LANES), row),
                 pl.BlockSpec((None, tq, MIX_WIDTH), row),
                 pl.BlockSpec((sel_rows, ncp), lambda bi, i: (0, 0)),
                 pl.BlockSpec((s, LANES), lambda bi, i: (0, 0))]
    args += [g_logit, gate, ov_t, block_onehot]
    wide = NSA_REP * BLK
    ns = NSA_QB * NSA_KV
    return pl.pallas_call(
        functools.partial(_nsa_kernel, n_cmp, n_sel, n_win),
        grid=(b, s // tq),
        in_specs=in_specs,
        out_specs=pl.BlockSpec((None, tq, MIX_WIDTH), row),
        out_shape=jax.ShapeDtypeStruct((b, s, MIX_WIDTH), BF16),
        scratch_shapes=[pltpu.VMEM((2 * ns, NSA_TK, wide // 2), F32),
                        pltpu.VMEM((ns, ncp, wide), F32),
                        pltpu.VMEM((ns, n_win * BLK, wide), F32),
                        pltpu.VMEM((ns, 1, wide), F32),
                        pltpu.VMEM((ns, ACC_ROWS, wide), F32),
                        pltpu.VMEM((ns, HEAD_DIM, wide), F32),
                        pltpu.VMEM((ns, ACC_ROWS, wide), F32)],
        compiler_params=_cparams(("arbitrary", "arbitrary")),
        name="nsa_attn",
    )(*args)


def _split(w, sizes):
    offs = np.cumsum([0] + list(sizes))
    return [w[:, int(offs[i]):int(offs[i + 1])] for i in range(len(sizes))]


def _swa_layer(x, rope, gain, w_in, sinks, w_out):
    kvw = SWA_KV * HEAD_DIM
    parts = _split(w_in, [MIX_WIDTH, kvw, kvw, MIX_WIDTH])
    segs = [_Seg(MIX_WIDTH, rope=True, scale=Q_SCALE), _Seg(kvw, rope=True), _Seg(kvw),
            _Seg(MIX_WIDTH)]
    q, k, v, gate = _norm_proj(x, gain, parts, segs, rope)
    o = _banded_attention(q, k, v, SWA_WINDOW - 1, sinks=sinks, gate=gate)
    return o, w_out


def _dilated_layer(x, rope, gain, w_in, w_out, residual=None):
    b, s, _ = x.shape
    kvw = DIL_KV * HEAD_DIM
    sizes, segs = [], []
    for window, dil in DIL_PATTERNS:
        assert s % (dil * BLK) == 0
        sizes += [MIX_WIDTH, kvw, kvw]
        segs += [_Seg(MIX_WIDTH, rope=True, scale=Q_SCALE, dil=dil), _Seg(kvw, rope=True, dil=dil),
                 _Seg(kvw, dil=dil)]
    sizes.append(MIX_WIDTH)
    segs.append(_Seg(MIX_WIDTH))
    res = _norm_proj(x, gain, _split(w_in, sizes), segs, rope, residual)
    if residual is not None:
        x, res = res[0], res[1:]
    gate = res[-1]
    os_, lses, dils = [], [], []
    for gi, (window, dil) in enumerate(DIL_PATTERNS):
        q, k, v = res[3 * gi:3 * gi + 3]
        if dil > 1:
            q, k, v = (t.reshape(b * dil, s // dil, t.shape[-1]) for t in (q, k, v))
        o, lse = _banded_attention(q, k, v, window // dil, want_lse=True)
        if dil > 1:
            o = o.reshape(b, dil, s // dil, MIX_WIDTH)
            lse = lse.reshape(b, dil, s // dil, LANES)
        os_.append(o)
        lses.append(lse)
        dils.append(dil)
    return _dil_combine_out_proj(os_, lses, gate, tuple(dils), w_out, x)


def _fox_layer(x, gain, w_in, b_f, w_out):
    sizes = [MIX_WIDTH, MIX_WIDTH, MIX_WIDTH, N_HEADS, MIX_WIDTH]
    segs = [_Seg(MIX_WIDTH, scale=Q_SCALE, tile=FOX_T), _Seg(MIX_WIDTH),
            _Seg(MIX_WIDTH, tile=FOX_T), _Seg(LANES, dtype=F32), _Seg(MIX_WIDTH)]
    qt, k, vt, f_logit, gate = _norm_proj(x, gain, _split(w_in, sizes), segs)
    kb = _fox_decay(f_logit, b_f)
    return _fox_attention(qt, k, kb, vt, gate), w_out


def _nsa_layer(x, positions, rope, gain, w_in, pe_k, w1_k, w2_k, pe_v, w1_v, w2_v, w_out,
               residual):
    b, s, _ = x.shape
    kvw = NSA_KV * HEAD_DIM
    sizes = [MIX_WIDTH] + [kvw] * 6 + [3 * N_HEADS, MIX_WIDTH]
    segs = [_Seg(MIX_WIDTH, rope=True, scale=Q_SCALE, tile=BLK), _Seg(kvw), _Seg(kvw),
            _Seg(kvw, rope=True), _Seg(kvw, tile=NSA_TK), _Seg(kvw, rope=True), _Seg(kvw, tile=BLK),
            _Seg(LANES, dtype=F32), _Seg(MIX_WIDTH)]
    x, q, kc, vc, ks, vs_t, kw, vw_t, g_logit, gate = _norm_proj(
        x, gain, _split(w_in, sizes), segs, rope, residual)
    ns = s // CMP_STRIDE
    cmp_pos = jnp.concatenate(
        [positions[:, CMP_LEN - 1::CMP_STRIDE], positions[:, -1:]], axis=1)[:, :ns]
    cmp_c, cmp_s = _rope_tables(cmp_pos.reshape(-1))
    k_cmp, v_cmp_t = _nsa_compress(kc, vc, pe_k, w1_k, w2_k, pe_v, w1_v, w2_v, cmp_c, cmp_s)
    return x, _nsa_attention(q, k_cmp, v_cmp_t, ks, vs_t, kw, vw_t, g_logit, gate), w_out


def kernel(x, positions, norm_0, w_in_0, sinks_0, w_out_0, norm_1, w_in_1, w_out_1, norm_2, w_in_2, b_f_2, w_out_2, norm_3, w_in_3, cmp_pe_k_3, cmp_w1_k_3, cmp_w2_k_3, cmp_pe_v_3, cmp_w1_v_3, cmp_w2_v_3, w_out_3, final_norm):
    rope = _rope_tables(positions.reshape(-1))
    o, w = _swa_layer(x, rope, norm_0, w_in_0, sinks_0, w_out_0)
    x = _dilated_layer(x, rope, norm_1, w_in_1, w_out_1, residual=(o, w))
    o, w = _fox_layer(x, norm_2, w_in_2, b_f_2, w_out_2)
    x, o, w = _nsa_layer(x, positions, rope, norm_3, w_in_3, cmp_pe_k_3, cmp_w1_k_3, cmp_w2_k_3,
                         cmp_pe_v_3, cmp_w1_v_3, cmp_w2_v_3, w_out_3, residual=(o, w))
    return _out_proj(o, w, x, final_gain=final_norm)
```

```python
import functools
import math

import numpy as np
import jax
import jax.numpy as jnp
from jax import lax
from jax.experimental import pallas as pl
from jax.experimental.pallas import tpu as pltpu

HEAD_DIM = 64
N_HEADS = 16
N_PAIRS = N_HEADS // 2
MIX_WIDTH = N_HEADS * HEAD_DIM
ROT_DIM = HEAD_DIM // 4
ROT_HALF = ROT_DIM // 2
ROPE_THETA = 500000.0
BLK = 128
LANES = 128
NORM_EPS = 1e-6
MASK_VALUE = -1e30
PAD_SCORE = -3e38
LOG2E = math.log2(math.e)
Q_SCALE = HEAD_DIM ** -0.5 * LOG2E

SWA_KV = 4
SWA_WINDOW = 128
DIL_KV = 4
DIL_PATTERNS = ((128, 1), (512, 4), (2048, 16))
NSA_KV = 2
CMP_LEN = 32
CMP_STRIDE = 16
CMP_HIDDEN = 256
SEL_LEN = 64
SEL_TOPK = 8
NSA_WINDOW = 256
FORCED_SCORE = 1e4

VMEM_LIMIT_BYTES = 56 * 1024 * 1024
PROJ_ROWS = 512
PROJ_COLS = 512
ACC_ROWS = HEAD_DIM + 16
F32 = jnp.float32
BF16 = jnp.bfloat16


def _cparams(sem):
    return pltpu.CompilerParams(dimension_semantics=sem, vmem_limit_bytes=VMEM_LIMIT_BYTES)


def _lane_half(shape):
    return lax.broadcasted_iota(jnp.int32, shape, 1) // HEAD_DIM


def _swap_halves(t):
    return jnp.concatenate([t[:, HEAD_DIM:], t[:, :HEAD_DIM]], axis=1)


def _head_query(qp, e, kv_half):
    qh = jnp.where(_lane_half(qp.shape) == e, qp, jnp.zeros_like(qp))
    if e != kv_half:
        qh = _swap_halves(qh)
    return qh


def _group_queries(q_ref, g, rep):
    tiles = []
    for r in range(rep):
        h = g * rep + r
        tiles.append(_head_query(q_ref[:, (h // 2) * LANES:(h // 2 + 1) * LANES], h % 2, g % 2))
    return jnp.concatenate(tiles, axis=0)


def _dot_nt(a, b):
    return lax.dot_general(a, b, (((1,), (1,)), ((), ())), preferred_element_type=F32)


def _dot(a, b):
    return jnp.dot(a, b, preferred_element_type=F32)


def _silu(x):
    return x * (1.0 / (1.0 + jnp.exp(-x)))


def _loop_in_pairs(count, body):
    odd = count % 2

    @pl.when(odd == 1)
    def _():
        body(0)

    def two(jj, carry):
        body(odd + 2 * jj)
        body(odd + 2 * jj + 1)
        return carry

    lax.fori_loop(0, count // 2, two, 0)


def _with_ones(vt):
    return jnp.concatenate([vt, jnp.ones((ACC_ROWS - HEAD_DIM, vt.shape[1]), BF16)], axis=0)


def _rope_table_kernel(pos_ref, inv_ref, c_ref, s_ref):
    pos = pos_ref[...].astype(F32)
    ang = pos * inv_ref[...]
    d = lax.broadcasted_iota(jnp.int32, ang.shape, 1) % HEAD_DIM
    cos = jnp.cos(ang)
    sin = jnp.sin(ang)
    c_ref[...] = jnp.where(d < ROT_DIM, cos, 1.0)
    s_ref[...] = jnp.where(d < ROT_HALF, -sin, jnp.where(d < ROT_DIM, sin, 0.0))


def _rope_tables(pos_flat):
    t = pos_flat.shape[0]
    rows = min(t, 2048)
    assert t % rows == 0
    inv = jnp.power(ROPE_THETA, -jnp.arange(ROT_HALF, dtype=F32) / ROT_HALF)
    inv_l = jnp.tile(inv, LANES // ROT_HALF)[None, :]
    out = jax.ShapeDtypeStruct((t, LANES), F32)
    return pl.pallas_call(
        _rope_table_kernel,
        grid=(t // rows,),
        in_specs=[pl.BlockSpec((rows, 1), lambda i: (i, 0)),
                  pl.BlockSpec((1, LANES), lambda i: (0, 0))],
        out_specs=[pl.BlockSpec((rows, LANES), lambda i: (i, 0))] * 2,
        out_shape=[out, out],
        compiler_params=_cparams(("arbitrary",)),
        name="rope_tables",
    )(pos_flat[:, None], inv_l)


def _apply_rope(y, c, s):
    outs = []
    for j in range(y.shape[1] // LANES):
        t = y[:, j * LANES:(j + 1) * LANES]
        d = lax.broadcasted_iota(jnp.int32, t.shape, 1) % HEAD_DIM
        partner = jnp.where(d < ROT_HALF, pltpu.roll(t, LANES - ROT_HALF, 1),
                            pltpu.roll(t, ROT_HALF, 1))
        outs.append(t * c + partner * s)
    return outs[0] if len(outs) == 1 else jnp.concatenate(outs, axis=1)


class _Seg:
    def __init__(self, width, rope=False, scale=None, dtype=BF16, dil=1, tile=None):
        self.width, self.rope, self.scale, self.dtype, self.dil = width, rope, scale, dtype, dil
        self.tile = tile
        assert not (tile and dil > 1)


def _norm_proj_kernel(segs, use_rope, has_residual, *refs):
    has_t = any(sg.tile and not sg.rope for sg in segs)
    x_ref, g_ref, w_ref = refs[:3]
    k = 3
    if has_t:
        wt_ref = refs[k]
        k += 1
    if use_rope:
        c_ref, s_ref = refs[k:k + 2]
        k += 2
    if has_residual:
        po_ref, pw_ref = refs[k:k + 2]
        k += 2
        xnew_ref = refs[k]
        k += 1
    out_refs = refs[k:k + len(segs)]
    stage_ref = refs[k + len(segs)] if any(sg.dil > 1 for sg in segs) else None

    x = x_ref[...]
    if has_residual:
        x = x + _dot(po_ref[...], pw_ref[...])
        xnew_ref[...] = x
    var = jnp.mean(x * x, axis=-1, keepdims=True)
    h = (x * lax.rsqrt(var + NORM_EPS) * g_ref[...]).astype(BF16)
    rows = x.shape[0]
    col = 0
    tcol = 0
    for sg, o_ref in zip(segs, out_refs):
        if sg.tile and not sg.rope:
            for c0 in range(0, sg.width, PROJ_COLS):
                cw = min(PROJ_COLS, sg.width - c0)
                yt = _dot_nt(wt_ref[tcol + c0:tcol + c0 + cw, :], h)
                if sg.scale is not None:
                    yt = yt * sg.scale
                for ti in range(rows // sg.tile):
                    o_ref[ti, c0:c0 + cw, :] = yt[:, ti * sg.tile:(ti + 1) * sg.tile].astype(sg.dtype)
            tcol += sg.width
            continue
        for c0 in range(0, sg.width, PROJ_COLS):
            cw = min(PROJ_COLS, sg.width - c0)
            y = _dot(h, w_ref[:, col + c0:col + c0 + cw])
            if sg.rope:
                y = _apply_rope(y, c_ref[...], s_ref[...])
            if sg.scale is not None:
                y = y * sg.scale
            if sg.tile:
                yt = y.T
                for ti in range(rows // sg.tile):
                    o_ref[ti, c0:c0 + cw, :] = yt[:, ti * sg.tile:(ti + 1) * sg.tile].astype(sg.dtype)
            elif sg.dil > 1:
                sub = rows // sg.dil
                for j in range(cw // LANES):
                    stage_ref[j] = y[:, j * LANES:(j + 1) * LANES]
                for r in range(sg.dil):
                    for j in range(cw // LANES):
                        lo = c0 + j * LANES
                        o_ref[r, :, lo:lo + LANES] = (
                            stage_ref[j, pl.ds(r, sub, stride=sg.dil), :].astype(sg.dtype))
            else:
                o_ref[:, c0:c0 + cw] = y.astype(sg.dtype)
        col += sg.width


def _norm_proj(x, gain, w_parts, segs, rope=None, residual=None):
    b, s, d = x.shape
    tm = PROJ_ROWS
    assert s % tm == 0
    w_cols, wt_rows = [], []
    for wp, sg in zip(w_parts, segs):
        if wp.shape[1] < sg.width:
            wp = jnp.pad(wp, ((0, 0), (0, sg.width - wp.shape[1])))
        if sg.tile and not sg.rope:
            wt_rows.append(wp.T)
        else:
            w_cols.append(wp)
    w = jnp.concatenate(w_cols, axis=1).astype(BF16)
    n = w.shape[1]
    use_rope = rope is not None
    once = dict(pipeline_mode=pl.Buffered(1))
    in_specs = [pl.BlockSpec((None, tm, d), lambda bi, i: (bi, i, 0)),
                pl.BlockSpec((1, d), lambda bi, i: (0, 0)),
                pl.BlockSpec((d, n), lambda bi, i: (0, 0), **once)]
    args = [x, gain[None, :], w]
    if wt_rows:
        wt = jnp.concatenate(wt_rows, axis=0).astype(BF16)
        in_specs.append(pl.BlockSpec(wt.shape, lambda bi, i: (0, 0), **once))
        args.append(wt)
    if use_rope:
        in_specs += [pl.BlockSpec((None, tm, LANES), lambda bi, i: (bi, i, 0))] * 2
        args += [rope[0].reshape(b, s, LANES), rope[1].reshape(b, s, LANES)]
    out_specs, out_shape = [], []
    if residual is not None:
        po, pw = residual
        in_specs += [pl.BlockSpec((None, tm, po.shape[-1]), lambda bi, i: (bi, i, 0)),
                     pl.BlockSpec(pw.shape, lambda bi, i: (0, 0), **once)]
        args += [po, pw.astype(BF16)]
        out_shape.append(jax.ShapeDtypeStruct((b, s, d), F32))
        out_specs.append(pl.BlockSpec((None, tm, d), lambda bi, i: (bi, i, 0)))
    for sg in segs:
        if sg.dil > 1:
            assert tm % sg.dil == 0
            out_shape.append(jax.ShapeDtypeStruct((b, sg.dil, s // sg.dil, sg.width), sg.dtype))
            out_specs.append(pl.BlockSpec((None, sg.dil, tm // sg.dil, sg.width),
                                          lambda bi, i: (bi, 0, i, 0)))
        elif sg.tile:
            assert tm % sg.tile == 0
            out_shape.append(jax.ShapeDtypeStruct((b, s // sg.tile, sg.width, sg.tile), sg.dtype))
            out_specs.append(pl.BlockSpec((None, tm // sg.tile, sg.width, sg.tile),
                                          lambda bi, i: (bi, i, 0, 0)))
        else:
            out_shape.append(jax.ShapeDtypeStruct((b, s, sg.width), sg.dtype))
            out_specs.append(pl.BlockSpec((None, tm, sg.width), lambda bi, i: (bi, i, 0)))
    scratch = ([pltpu.VMEM((PROJ_COLS // LANES, tm, LANES), F32)]
               if any(sg.dil > 1 for sg in segs) else [])
    return pl.pallas_call(
        functools.partial(_norm_proj_kernel, segs, use_rope, residual is not None),
        grid=(b, s // tm),
        in_specs=in_specs,
        out_specs=out_specs,
        out_shape=out_shape,
        scratch_shapes=scratch,
        compiler_params=_cparams(("arbitrary", "arbitrary")),
        name="norm_proj",
    )(*args)


BANDED_MAX_QB = 2


def _banded_kernel(n_qb, max_dist, n_prev, n_kv, has_sink, has_gate, want_lse, *refs):
    q_ref = refs[0]
    n_blocks = n_qb + n_prev
    k_refs = refs[1:1 + n_blocks]
    v_refs = refs[1 + n_blocks:1 + 2 * n_blocks]
    k = 1 + 2 * n_blocks
    sink_ref = gate_ref = lse_ref = None
    if has_sink:
        sink_ref = refs[k]; k += 1
    if has_gate:
        gate_ref = refs[k]; k += 1
    o_ref = refs[k]; k += 1
    if want_lse:
        lse_ref = refs[k]; k += 1
    ot_sc = refs[k]
    st_sc = refs[k + 1]

    n = pl.program_id(1)
    tq = BLK
    kw = (n_prev + 1) * BLK
    rep = N_HEADS // n_kv
    krow = lax.broadcasted_iota(jnp.int32, (kw, tq), 0) - n_prev * BLK
    dist = lax.broadcasted_iota(jnp.int32, (kw, tq), 1) - krow
    band = (dist >= 0) & (dist <= max_dist)
    lse_rows = [[] for _ in range(n_qb)]
    units = [(qb, g) for qb in range(n_qb) for g in range(n_kv)]

    def scores(u):
        qb, g = units[u]
        sl = slice((g // 2) * LANES, (g // 2 + 1) * LANES)
        kcat = jnp.concatenate([kr[:, sl] for kr in k_refs[qb:qb + n_prev + 1]], axis=0)
        st_sc[u] = _dot_nt(kcat, _group_queries(q_ref.at[qb * tq:(qb + 1) * tq], g, rep))

    def consume(u):
        qb, g = units[u]
        sl = slice((g // 2) * LANES, (g // 2 + 1) * LANES)
        ok = band & ((n * n_qb + qb) * BLK + krow >= 0)
        vcat = jnp.concatenate([vr[:, sl] for vr in v_refs[qb:qb + n_prev + 1]], axis=0)
        vt = vcat.astype(F32).T.astype(BF16)
        ms, pts = [], []
        for r in range(rep):
            s_r = jnp.where(ok, st_sc[u, :, r * tq:(r + 1) * tq], MASK_VALUE)
            m = jnp.max(s_r, axis=0, keepdims=True)
            if has_sink:
                h = g * rep + r
                m = jnp.maximum(m, sink_ref[0:1, h:h + 1])
            pts.append(jnp.exp2(s_r - m).astype(BF16))
            ms.append(m)
        acc = _dot(_with_ones(vt[(g % 2) * HEAD_DIM:(g % 2 + 1) * HEAD_DIM, :]),
                   jnp.concatenate(pts, axis=1))
        for r in range(rep):
            h = g * rep + r
            den = acc[HEAD_DIM:HEAD_DIM + 1, r * tq:(r + 1) * tq]
            if has_sink:
                den = den + jnp.exp2(sink_ref[0:1, h:h + 1] - ms[r])
            ot_sc[qb * N_HEADS + h] = acc[:HEAD_DIM, r * tq:(r + 1) * tq] / den
            if want_lse:
                lse_rows[qb].append(ms[r] + jnp.log2(den))

    scores(0)
    scores(1)
    for u in range(len(units)):
        if u + 2 < len(units):
            scores(u + 2)
        consume(u)
    for qb in range(n_qb):
        rows = slice(qb * tq, (qb + 1) * tq)
        for p in range(N_PAIRS):
            sl = slice(p * LANES, (p + 1) * LANES)
            pair = jnp.concatenate([ot_sc[qb * N_HEADS + 2 * p],
                                    ot_sc[qb * N_HEADS + 2 * p + 1]], axis=0).T
            if has_gate:
                pair = pair * _silu(gate_ref[rows, sl].astype(F32))
            o_ref[rows, sl] = pair.astype(o_ref.dtype)
        if want_lse:
            lse_t = jnp.concatenate(
                lse_rows[qb] + [jnp.zeros((LANES - N_HEADS, tq), F32)], axis=0)
            lse_ref[rows, :] = lse_t.T


def _banded_attention(q, k, v, max_dist, sinks=None, gate=None, want_lse=False):
    bq, sq, _ = q.shape
    kvw = k.shape[-1]
    n_kv = kvw // HEAD_DIM
    n_prev = -(-max_dist // BLK)
    n_qb = BANDED_MAX_QB if sq % (BANDED_MAX_QB * BLK) == 0 else 1
    tq = n_qb * BLK
    assert sq % tq == 0
    row = lambda b, i: (b, i, 0)
    in_specs = [pl.BlockSpec((None, tq, MIX_WIDTH), row)]
    args = [q]
    for arr in (k, v):
        for j in range(n_prev, -n_qb, -1):
            in_specs.append(pl.BlockSpec(
                (None, BLK, kvw), lambda b, i, j=j: (b, jnp.maximum(n_qb * i - j, 0), 0)))
            args.append(arr)
    if sinks is not None:
        in_specs.append(pl.BlockSpec((1, LANES), lambda b, i: (0, 0)))
        args.append(jnp.pad(sinks.astype(F32) * LOG2E, (0, LANES - N_HEADS))[None, :])
    if gate is not None:
        in_specs.append(pl.BlockSpec((None, tq, MIX_WIDTH), row))
        args.append(gate)
    out_specs = [pl.BlockSpec((None, tq, MIX_WIDTH), row)]
    out_shape = [jax.ShapeDtypeStruct((bq, sq, MIX_WIDTH), BF16)]
    if want_lse:
        out_specs.append(pl.BlockSpec((None, tq, LANES), row))
        out_shape.append(jax.ShapeDtypeStruct((bq, sq, LANES), F32))
    res = pl.pallas_call(
        functools.partial(_banded_kernel, n_qb, max_dist, n_prev, n_kv, sinks is not None,
                          gate is not None, want_lse),
        grid=(bq, sq // tq),
        in_specs=in_specs,
        out_specs=out_specs,
        out_shape=out_shape,
        scratch_shapes=[pltpu.VMEM((n_qb * N_HEADS, HEAD_DIM, BLK), F32),
                        pltpu.VMEM((n_qb * n_kv, (n_prev + 1) * BLK,
                                    (N_HEADS // n_kv) * BLK), F32)],
        compiler_params=_cparams(("arbitrary", "arbitrary")),
        name="banded_attn",
    )(*args)
    return res if want_lse else res[0]


def _dil_combine_kernel(dils, *refs):
    ng = len(dils)
    o_refs, l_refs = refs[:ng], refs[ng:2 * ng]
    gate_ref, expand_ref, w_ref, x_ref, out_ref, stage_ref, lstage_ref = refs[2 * ng:]
    rows = out_ref.shape[0]

    def natural(ref, dil, stage, slab):
        sl = slice(slab * LANES, (slab + 1) * LANES)
        if dil == 1:
            return ref[:, sl].astype(F32)
        sub = rows // dil
        for r in range(dil):
            stage[pl.ds(r, sub, stride=dil), :] = ref[r, :, sl].astype(F32)
        return stage[...]

    lses = [natural(l_refs[i], dils[i], lstage_ref, 0) for i in range(ng)]
    mx = functools.reduce(jnp.maximum, lses)
    ws = [jnp.exp2(l - mx) for l in lses]
    tot = functools.reduce(lambda a, c: a + c, ws)
    inv = 1.0 / tot

    def widen(w):
        hi = w.astype(BF16)
        lo = (w - hi.astype(F32)).astype(BF16)
        return _dot(hi, expand_ref[...]) + _dot(lo, expand_ref[...])

    wide = [widen(w * inv) for w in ws[:-1]]
    wide.append(1.0 - functools.reduce(lambda a, c: a + c, wide))
    gated = []
    for p in range(N_PAIRS):
        sl = slice(p * LANES, (p + 1) * LANES)
        acc = None
        for gi in range(ng):
            t = wide[gi][:, sl] * natural(o_refs[gi], dils[gi], stage_ref, p)
            acc = t if acc is None else acc + t
        gated.append((acc * _silu(gate_ref[:, sl].astype(F32))).astype(BF16))
    out_ref[...] = x_ref[...] + _dot(jnp.concatenate(gated, axis=1), w_ref[...])


def _dil_combine_out_proj(os_, lses, gate, dils, w_out, x):
    b, s, d = x.shape
    tm = PROJ_ROWS
    expand = jnp.asarray((np.arange(LANES)[:, None] == np.arange(MIX_WIDTH)[None, :] // HEAD_DIM)
                         .astype(np.float32), BF16)
    in_specs, args = [], []
    for arrs, width in ((os_, MIX_WIDTH), (lses, LANES)):
        for arr, dil in zip(arrs, dils):
            if dil == 1:
                in_specs.append(pl.BlockSpec((None, tm, width), lambda bi, i: (bi, i, 0)))
            else:
                in_specs.append(pl.BlockSpec((None, dil, tm // dil, width),
                                             lambda bi, i: (bi, 0, i, 0)))
            args.append(arr)
    row = lambda bi, i: (bi, i, 0)
    in_specs += [pl.BlockSpec((None, tm, MIX_WIDTH), row),
                 pl.BlockSpec((LANES, MIX_WIDTH), lambda bi, i: (0, 0)),
                 pl.BlockSpec((MIX_WIDTH, d), lambda bi, i: (0, 0)),
                 pl.BlockSpec((None, tm, d), row)]
    args += [gate, expand, w_out.astype(BF16), x]
    return pl.pallas_call(
        functools.partial(_dil_combine_kernel, dils),
        grid=(b, s // tm),
        in_specs=in_specs,
        out_specs=pl.BlockSpec((None, tm, d), row),
        out_shape=jax.ShapeDtypeStruct((b, s, d), F32),
        scratch_shapes=[pltpu.VMEM((tm, LANES), F32), pltpu.VMEM((tm, LANES), F32)],
        compiler_params=_cparams(("arbitrary", "arbitrary")),
        name="dil_combine_out_proj",
    )(*args)


N_BIAS_PIECES = 3


def _fox_decay_kernel(f_ref, b_ref, kb_ref):
    s = f_ref.shape[0]
    x = f_ref[...] + b_ref[...]
    logf = jnp.minimum(x, 0.0) - jnp.log1p(jnp.exp(-jnp.abs(x)))
    r = lax.broadcasted_iota(jnp.int32, (BLK, BLK), 0)
    c = lax.broadcasted_iota(jnp.int32, (BLK, BLK), 1)
    tri = (c <= r).astype(F32)
    lane = lax.broadcasted_iota(jnp.int32, (BLK, LANES), 1)
    carry = jnp.zeros((1, LANES), F32)
    for i in range(s // BLK):
        blk = logf[i * BLK:(i + 1) * BLK, :]
        cs = jnp.dot(tri, blk, preferred_element_type=F32, precision=lax.Precision.HIGHEST) + carry
        carry = cs[BLK - 1:BLK, :]
        rest = cs * (-LOG2E)
        out = jnp.zeros((BLK, LANES), F32)
        for j in range(N_BIAS_PIECES):
            piece = rest.astype(BF16).astype(F32)
            rest = rest - piece
            moved = piece if j == 0 else pltpu.roll(piece, N_HEADS * j, 1)
            out = jnp.where((lane >= N_HEADS * j) & (lane < N_HEADS * (j + 1)), moved, out)
        kb_ref[i * BLK:(i + 1) * BLK, :] = out.astype(BF16)


def _fox_decay(f_logit, b_f):
    b, s, _ = f_logit.shape
    b_pad = jnp.pad(b_f.astype(F32), (0, LANES - N_HEADS))[None, :]
    return pl.pallas_call(
        _fox_decay_kernel,
        grid=(b,),
        in_specs=[pl.BlockSpec((None, s, LANES), lambda bi: (bi, 0, 0)),
                  pl.BlockSpec((1, LANES), lambda bi: (0, 0))],
        out_specs=pl.BlockSpec((None, s, LANES), lambda bi: (bi, 0, 0)),
        out_shape=jax.ShapeDtypeStruct((b, s, LANES), BF16),
        compiler_params=_cparams(("arbitrary",)),
        name="fox_decay",
    )(f_logit, b_pad)


FOX_T = PROJ_ROWS
FOX_HEADS_PER_STEP = 4


def _fox_kernel(qt_ref, k_ref, kb_ref, vt_ref, gate_ref, o_ref, m_sc, acc_sc, st_sc):
    pi = pl.program_id(1)
    i = pl.program_id(2)
    t = FOX_T
    heads = range(FOX_HEADS_PER_STEP)
    row = lax.broadcasted_iota(jnp.int32, (LANES, t), 0)
    zeros = jnp.zeros((HEAD_DIM, t), BF16)
    qx = []
    for e in heads:
        h = FOX_HEADS_PER_STEP * pi + e
        pick = (row % N_HEADS == h) & (row < N_HEADS * N_BIAS_PIECES)
        sel = jnp.where(pick, 1.0, 0.0).astype(BF16)
        mine = qt_ref[0, e * HEAD_DIM:(e + 1) * HEAD_DIM, :]
        top = [mine, zeros] if e % 2 == 0 else [zeros, mine]
        qx.append(jnp.concatenate(top + [sel], axis=0))
        m_sc[e] = jnp.full((1, t), MASK_VALUE, F32)
        acc_sc[e] = jnp.zeros((ACC_ROWS, t), F32)

    tk = t // 2

    def scores(jt, hf, e):
        k0 = pl.multiple_of(jt * t + hf * tk, tk)
        pair = slice((e // 2) * LANES, (e // 2 + 1) * LANES)
        kx = jnp.concatenate([k_ref[pl.ds(k0, tk), pair], kb_ref[pl.ds(k0, tk), :]], axis=1)
        st_sc[e, hf] = _dot(kx, qx[e])

    def consume(jt, hf, e, diag):
        st = st_sc[e, hf]
        if diag:
            ok = (lax.broadcasted_iota(jnp.int32, (tk, t), 0) + hf * tk
                  <= lax.broadcasted_iota(jnp.int32, (tk, t), 1))
            st = jnp.where(ok, st, MASK_VALUE)
        m_old = m_sc[e]
        m_new = jnp.maximum(m_old, jnp.max(st, axis=0, keepdims=True))
        pt = jnp.exp2(st - m_new).astype(BF16)
        alpha = jnp.exp2(m_old - m_new)
        vt = vt_ref[jt, e * HEAD_DIM:(e + 1) * HEAD_DIM, hf * tk:(hf + 1) * tk]
        acc_sc[e] = alpha * acc_sc[e] + _dot(_with_ones(vt), pt)
        m_sc[e] = m_new

    for e in heads:
        scores(0, 0, e)

    def body(jt):
        for e in heads:
            scores(jt, 1, e)
            consume(jt, 0, e, False)
        for e in heads:
            scores(jt + 1, 0, e)
            consume(jt, 1, e, False)

    _loop_in_pairs(i, body)
    for e in heads:
        scores(i, 1, e)
        consume(i, 0, e, True)
    for e in heads:
        consume(i, 1, e, True)
    for p in range(FOX_HEADS_PER_STEP // 2):
        outs = []
        for e in (2 * p, 2 * p + 1):
            acc = acc_sc[e]
            outs.append(acc[:HEAD_DIM] / jnp.maximum(acc[HEAD_DIM:HEAD_DIM + 1], 1e-30))
        sl = slice(p * LANES, (p + 1) * LANES)
        out = jnp.concatenate(outs, axis=0).T
        o_ref[:, sl] = (out * _silu(gate_ref[:, sl].astype(F32))).astype(o_ref.dtype)


def _fox_attention(qt, k, kb, vt, gate):
    b, s, _ = k.shape
    t = FOX_T
    assert s % t == 0 and vt.shape == (b, s // t, MIX_WIDTH, t) and qt.shape == vt.shape
    tile = lambda bi, p, i: (bi, i, p)
    nh = FOX_HEADS_PER_STEP
    wd = nh * HEAD_DIM
    return pl.pallas_call(
        _fox_kernel,
        grid=(b, N_HEADS // nh, s // t),
        in_specs=[pl.BlockSpec((None, 1, wd, t), lambda bi, p, i: (bi, i, p, 0)),
                  pl.BlockSpec((None, s, wd), lambda bi, p, i: (bi, 0, p)),
                  pl.BlockSpec((None, s, LANES), lambda bi, p, i: (bi, 0, 0)),
                  pl.BlockSpec((None, s // t, wd, t), lambda bi, p, i: (bi, 0, p, 0)),
                  pl.BlockSpec((None, t, wd), tile)],
        out_specs=pl.BlockSpec((None, t, wd), tile),
        out_shape=jax.ShapeDtypeStruct((b, s, MIX_WIDTH), BF16),
        scratch_shapes=[pltpu.VMEM((nh, 1, t), F32), pltpu.VMEM((nh, ACC_ROWS, t), F32),
                        pltpu.VMEM((nh, 2, t // 2, t), F32)],
        compiler_params=_cparams(("arbitrary", "arbitrary", "arbitrary")),
        name="fox_attn",
    )(qt, k, kb, vt, gate)


def _gelu_tanh(x):
    return 0.5 * x * (1.0 + jnp.tanh(math.sqrt(2.0 / math.pi) * (x + 0.044715 * (x * x * x))))


def _nsa_compress_kernel(ak_ref, av_ref, pe_ref, wa_ref, wb_ref, w2_ref, w2t_ref, c_ref, s_ref,
                         kc_ref, vct_ref):
    nrow = ak_ref.shape[0]
    for idx, a_ref in enumerate((ak_ref, av_ref)):
        a = a_ref[...].astype(F32)
        xa = (a + pe_ref[idx, 0:1, :]).astype(BF16)
        xb = (a + pe_ref[idx, 1:2, :]).astype(BF16)
        ya = _dot(xa, wa_ref[idx])
        yb = _dot(xb, wb_ref[idx])
        hid = _gelu_tanh(ya + pltpu.roll(yb, nrow - 1, 0)).astype(BF16)
        if idx == 0:
            y = _dot(hid, w2_ref[...])
            kc_ref[...] = _apply_rope(y, c_ref[...], s_ref[...]).astype(kc_ref.dtype)
        else:
            vct_ref[...] = _dot_nt(w2t_ref[...], hid).astype(vct_ref.dtype)


def _nsa_compress(kc, vc, pe_k, w1_k, w2_k, pe_v, w1_v, w2_v, rope_c, rope_s):
    b, s, _ = kc.shape
    ns = s // CMP_STRIDE
    g = NSA_KV
    flat = CMP_STRIDE * g * HEAD_DIM

    def w1_halves(w1):
        w1r = w1.reshape(2, CMP_STRIDE, HEAD_DIM, CMP_HIDDEN)
        outs = []
        for hf in range(2):
            z = jnp.einsum('ldc,gh->lgdhc', w1r[hf], jnp.eye(g, dtype=F32))
            outs.append(z.reshape(flat, g * CMP_HIDDEN))
        return outs

    def pe_halves(pe):
        per = pe.reshape(2, CMP_STRIDE, 1, HEAD_DIM)
        return jnp.broadcast_to(per, (2, CMP_STRIDE, g, HEAD_DIM)).reshape(2, flat)

    def w2_bd(w2):
        z = jnp.einsum('cd,gh->gchd', w2, jnp.eye(g, dtype=F32))
        return z.reshape(g * CMP_HIDDEN, g * HEAD_DIM)

    ka, kb = w1_halves(w1_k)
    va, vb = w1_halves(w1_v)
    wa = jnp.stack([ka, va]).astype(BF16)
    wb = jnp.stack([kb, vb]).astype(BF16)
    w2 = w2_bd(w2_k).astype(BF16)
    w2t = w2_bd(w2_v).T.astype(BF16)
    pe = jnp.stack([pe_halves(pe_k), pe_halves(pe_v)]).astype(F32)
    whole = lambda bi: (0, 0, 0)
    per_b = lambda bi: (bi, 0, 0)
    return pl.pallas_call(
        _nsa_compress_kernel,
        grid=(b,),
        in_specs=[pl.BlockSpec((None, ns, flat), per_b),
                  pl.BlockSpec((None, ns, flat), per_b),
                  pl.BlockSpec((2, 2, flat), whole),
                  pl.BlockSpec((2, flat, g * CMP_HIDDEN), whole),
                  pl.BlockSpec((2, flat, g * CMP_HIDDEN), whole),
                  pl.BlockSpec((g * CMP_HIDDEN, LANES), lambda bi: (0, 0)),
                  pl.BlockSpec((LANES, g * CMP_HIDDEN), lambda bi: (0, 0)),
                  pl.BlockSpec((None, ns, LANES), per_b),
                  pl.BlockSpec((None, ns, LANES), per_b)],
        out_specs=[pl.BlockSpec((None, ns, LANES), per_b),
                   pl.BlockSpec((None, LANES, ns), per_b)],
        out_shape=[jax.ShapeDtypeStruct((b, ns, LANES), BF16),
                   jax.ShapeDtypeStruct((b, LANES, ns), BF16)],
        compiler_params=_cparams(("arbitrary",)),
        name="nsa_compress",
    )(kc.reshape(b, ns, flat), vc.reshape(b, ns, flat), pe, wa, wb, w2, w2t,
      rope_c.reshape(b, ns, LANES), rope_s.reshape(b, ns, LANES))


NSA_QB = 2
NSA_TK = 256
NSA_REP = N_HEADS // NSA_KV


def _nsa_kernel(n_cmp, n_sel, n_win, *refs):
    n_wblk = n_win + NSA_QB - 1
    q_ref, kc_ref, vct_ref, ks_ref, vst_ref = refs[:5]
    kw_refs = refs[5:5 + n_wblk]
    vwt_refs = refs[5 + n_wblk:5 + 2 * n_wblk]
    (gl_ref, gate_ref, ovt_ref, blk_ref, wout_ref, x_ref, fg_ref,
     o_ref) = refs[5 + 2 * n_wblk:13 + 2 * n_wblk]
    st_sc, stc_sc, stw_sc, m_sc, acc_sc, ocmp_sc, owin_sc = refs[13 + 2 * n_wblk:]

    n = pl.program_id(1)
    tq, tk, rep = BLK, NSA_TK, NSA_REP
    streams = [(qb, g) for qb in range(NSA_QB) for g in range(NSA_KV)]
    ncp = kc_ref.shape[0]
    t_lane = [(n * NSA_QB + qb) * tq + lax.broadcasted_iota(jnp.int32, (1, tq), 1)
              for qb in range(NSA_QB)]
    zeros = jnp.zeros((HEAD_DIM, tq), BF16)
    qg = []
    for qb, g in streams:
        tiles = []
        for r in range(rep):
            h = g * rep + r
            mine = q_ref[qb, h * HEAD_DIM:(h + 1) * HEAD_DIM, :]
            tiles.append(jnp.concatenate([mine, zeros] if g % 2 == 0 else [zeros, mine], axis=0))
        qg.append(jnp.concatenate(tiles, axis=1))

    ci = lax.broadcasted_iota(jnp.int32, (ncp, tq), 0)
    kwn = n_win * BLK
    krow = lax.broadcasted_iota(jnp.int32, (kwn, tq), 0) - (n_win - 1) * BLK
    dist = lax.broadcasted_iota(jnp.int32, (kwn, tq), 1) - krow
    band = (dist >= 0) & (dist <= NSA_WINDOW - 1)
    sel_rows = ovt_ref.shape[0]
    rowi = lax.broadcasted_iota(jnp.int32, (sel_rows, tq), 0)
    rowf = rowi.astype(F32)

    def cmp_scores(sg):
        stc_sc[sg] = _dot(kc_ref[...], qg[sg])

    def cmp_consume(sg):
        qb, g = streams[sg]
        cmask = (ci * CMP_STRIDE + (CMP_LEN - 1) <= t_lane[qb]) & (ci < n_cmp)
        ps = []
        for r in range(rep):
            s_r = jnp.where(cmask, stc_sc[sg, :, r * tq:(r + 1) * tq], MASK_VALUE)
            m = jnp.max(s_r, axis=0, keepdims=True)
            pr = jnp.where(cmask, jnp.exp2(s_r - m), 0.0)
            ps.append(pr.astype(BF16))
        lhs = jnp.concatenate([_with_ones(vct_ref[g * HEAD_DIM:(g + 1) * HEAD_DIM, :]),
                               ovt_ref[...]], axis=0)
        res = _dot(lhs, jnp.concatenate(ps, axis=1))
        inv = 1.0 / jnp.maximum(res[HEAD_DIM:HEAD_DIM + 1], 1e-30)
        ocmp_sc[sg] = res[:HEAD_DIM] * inv
        imp = None
        for r in range(rep):
            ql = slice(r * tq, (r + 1) * tq)
            t = res[ACC_ROWS:, ql] * inv[:, ql]
            imp = t if imp is None else imp + t
        return imp

    def select(sg, imp):
        cur = t_lane[streams[sg][0]] // SEL_LEN
        forced = (rowi == 0) | (rowi == cur)
        causal = rowi <= cur
        score = jnp.where(causal, jnp.where(forced, FORCED_SCORE, imp), MASK_VALUE)
        score = jnp.where(rowi < n_sel, score, PAD_SCORE)
        chosen = jnp.zeros((sel_rows, tq), F32)
        for _ in range(min(SEL_TOPK, n_sel)):
            mx = jnp.max(score, axis=0, keepdims=True)
            first = jnp.min(jnp.where(score == mx, rowf, float(LANES)), axis=0, keepdims=True)
            hit = rowf == first
            chosen = jnp.where(hit, 1.0, chosen)
            score = jnp.where(hit, PAD_SCORE, score)
        bias_t = jnp.where(causal & (chosen > 0.5), 0.0, MASK_VALUE)
        bias_t = jnp.concatenate([bias_t, jnp.zeros((LANES - sel_rows, tq), F32)], axis=0)
        bias_t = bias_t.astype(BF16)
        return jnp.concatenate([qg[sg], jnp.concatenate([bias_t] * rep, axis=1)], axis=0)

    def win_scores(sg):
        qb = streams[sg][0]
        kwcat = jnp.concatenate([kr[...] for kr in kw_refs[qb:qb + n_win]], axis=0)
        stw_sc[sg] = _dot(kwcat, qg[sg])

    def win_consume(sg):
        qb, g = streams[sg]
        wok = band & ((n * NSA_QB + qb) * BLK + krow >= 0)
        vwt = jnp.concatenate([vr[0] for vr in vwt_refs[qb:qb + n_win]], axis=1)
        pts = []
        for r in range(rep):
            s_r = jnp.where(wok, stw_sc[sg, :, r * tq:(r + 1) * tq], MASK_VALUE)
            m = jnp.max(s_r, axis=0, keepdims=True)
            pts.append(jnp.exp2(s_r - m).astype(BF16))
        owin_sc[sg] = _dot(_with_ones(vwt[g * HEAD_DIM:(g + 1) * HEAD_DIM, :]),
                           jnp.concatenate(pts, axis=1))

    ns = len(streams)
    cmp_scores(0)
    cmp_scores(1)
    imps = []
    for sg in range(ns):
        if sg + 2 < ns:
            cmp_scores(sg + 2)
        else:
            win_scores(sg + 2 - ns)
        imps.append(cmp_consume(sg))
    qsel = []
    for sg in range(ns):
        if sg + 2 < ns:
            win_scores(sg + 2)
        qsel.append(select(sg, imps[sg]))
        win_consume(sg)

    for sg in range(ns):
        m_sc[sg] = jnp.full((1, rep * tq), MASK_VALUE, F32)
        acc_sc[sg] = jnp.zeros((ACC_ROWS, rep * tq), F32)

    hrep = rep // 2
    wide = hrep * tq
    units = [(sg, hf) for sg in range(ns) for hf in range(2)]
    nu = len(units)

    def scores(c, u):
        sg, hf = units[u]
        k0 = pl.multiple_of(c * tk, tk)
        kx = jnp.concatenate([ks_ref[pl.ds(k0, tk), :], blk_ref[pl.ds(k0, tk), :]], axis=1)
        st_sc[u] = _dot(kx, qsel[sg][:, hf * wide:(hf + 1) * wide])

    def consume(c, u, last):
        sg, hf = units[u]
        qb, g = streams[sg]
        lanes = slice(hf * wide, (hf + 1) * wide)
        if last:
            tok_ok = c * tk + lax.broadcasted_iota(jnp.int32, (tk, tq), 0) <= t_lane[qb]
        m_old = m_sc[sg, :, lanes]
        ms, pts = [], []
        for r in range(hrep):
            s_r = st_sc[u, :, r * tq:(r + 1) * tq]
            if last:
                s_r = jnp.where(tok_ok, s_r, MASK_VALUE)
            m_new = jnp.maximum(m_old[:, r * tq:(r + 1) * tq], jnp.max(s_r, axis=0, keepdims=True))
            pts.append(jnp.exp2(s_r - m_new).astype(BF16))
            ms.append(m_new)
        m_new = jnp.concatenate(ms, axis=1)
        alpha = jnp.exp2(m_old - m_new)
        vt = vst_ref[c, g * HEAD_DIM:(g + 1) * HEAD_DIM, :]
        acc_sc[sg, :, lanes] = (alpha * acc_sc[sg, :, lanes]
                                + _dot(_with_ones(vt), jnp.concatenate(pts, axis=1)))
        m_sc[sg, :, lanes] = m_new

    n_steps = ((n * NSA_QB + 1) * tq + tk - 1) // tk
    scores(0, 0)
    scores(0, 1)

    def body(c):
        for u in range(nu):
            if u + 2 < nu:
                scores(c, u + 2)
            else:
                scores(c + 1, u + 2 - nu)
            consume(c, u, False)

    _loop_in_pairs(n_steps - 1, body)
    for u in range(nu):
        if u + 2 < nu:
            scores(n_steps - 1, u + 2)
        consume(n_steps - 1, u, True)

    for qb in range(NSA_QB):
        tok = slice(qb * tq, (qb + 1) * tq)
        gate_t = (1.0 / (1.0 + jnp.exp(-gl_ref[tok, :]))).T
        pairs = []
        for p in range(N_PAIRS):
            sl = slice(p * LANES, (p + 1) * LANES)
            g = (2 * p) // rep
            sg = qb * NSA_KV + g
            rows = []
            for e in range(2):
                h = 2 * p + e
                ql = slice((h - g * rep) * tq, (h - g * rep + 1) * tq)
                slc, win = acc_sc[sg][:, ql], owin_sc[sg][:, ql]
                branches = (ocmp_sc[sg][:, ql],
                            slc[:HEAD_DIM] / jnp.maximum(slc[HEAD_DIM:HEAD_DIM + 1], 1e-30),
                            win[:HEAD_DIM] / win[HEAD_DIM:HEAD_DIM + 1])
                out = None
                for j, br in enumerate(branches):
                    t = gate_t[3 * h + j:3 * h + j + 1, :] * br
                    out = t if out is None else out + t
                rows.append(out)
            pair = jnp.concatenate(rows, axis=0).T
            pairs.append((pair * _silu(gate_ref[tok, sl].astype(F32))).astype(BF16))
        y = x_ref[tok, :] + _dot(jnp.concatenate(pairs, axis=1), wout_ref[...])
        var = jnp.mean(y * y, axis=-1, keepdims=True)
        o_ref[tok, :] = y * lax.rsqrt(var + NORM_EPS) * fg_ref[...]


def _selection_overlap_t(n_cmp_pad, n_cmp, n_sel):
    cs = np.arange(n_cmp_pad) * CMP_STRIDE
    js = np.arange(LANES) * SEL_LEN
    ov = np.minimum(cs[None, :] + CMP_LEN, js[:, None] + SEL_LEN) - np.maximum(cs[None, :], js[:, None])
    ov = (np.clip(ov, 0, None) / CMP_LEN).astype(np.float32)
    ov[:, n_cmp:] = 0.0
    ov[n_sel:, :] = 0.0
    return ov


def _nsa_attention(q_t, k_cmp, v_cmp_t, ks, vs_t, kw, vw_t, g_logit, gate, w_out, x, final_gain):
    b, s, d = x.shape
    tq = NSA_QB * BLK
    ncp = k_cmp.shape[1]
    n_cmp = s // CMP_STRIDE - 1
    n_sel = s // SEL_LEN
    n_win = -(-(NSA_WINDOW - 1) // BLK) + 1
    assert n_sel <= LANES and s % NSA_TK == 0 and tq == NSA_TK
    sel_rows = -(-n_sel // 16) * 16
    ov_t = jnp.asarray(_selection_overlap_t(ncp, n_cmp, n_sel)[:sel_rows], BF16)
    row = lambda bi, i: (bi, i, 0)
    per_b = lambda bi, i: (bi, 0, 0)
    in_specs = [pl.BlockSpec((None, NSA_QB, MIX_WIDTH, BLK), lambda bi, i: (bi, i, 0, 0)),
                pl.BlockSpec((None, ncp, LANES), per_b),
                pl.BlockSpec((None, LANES, ncp), per_b),
                pl.BlockSpec((None, s, LANES), per_b),
                pl.BlockSpec((None, s // NSA_TK, LANES, NSA_TK), lambda bi, i: (bi, 0, 0, 0))]
    args = [q_t, k_cmp, v_cmp_t, ks, vs_t]
    for j in range(n_win - 1, -NSA_QB, -1):
        in_specs.append(pl.BlockSpec(
            (None, BLK, LANES), lambda bi, i, j=j: (bi, jnp.maximum(NSA_QB * i - j, 0), 0)))
        args.append(kw)
    for j in range(n_win - 1, -NSA_QB, -1):
        in_specs.append(pl.BlockSpec(
            (None, 1, LANES, BLK), lambda bi, i, j=j: (bi, jnp.maximum(NSA_QB * i - j, 0), 0, 0)))
        args.append(vw_t)
    block_onehot = jnp.asarray(
        (np.arange(s)[:, None] // SEL_LEN == np.arange(LANES)[None, :]).astype(np.float32), BF16)
    in_specs += [pl.BlockSpec((None, tq, LANES), row),
                 pl.BlockSpec((None, tq, MIX_WIDTH), row),
                 pl.BlockSpec((sel_rows, ncp), lambda bi, i: (0, 0)),
                 pl.BlockSpec((s, LANES), lambda bi, i: (0, 0)),
                 pl.BlockSpec((MIX_WIDTH, d), lambda bi, i: (0, 0), pipeline_mode=pl.Buffered(1)),
                 pl.BlockSpec((None, tq, d), row),
                 pl.BlockSpec((1, d), lambda bi, i: (0, 0))]
    args += [g_logit, gate, ov_t, block_onehot, w_out.astype(BF16), x, final_gain[None, :]]
    wide = NSA_REP * BLK
    ns = NSA_QB * NSA_KV
    return pl.pallas_call(
        functools.partial(_nsa_kernel, n_cmp, n_sel, n_win),
        grid=(b, s // tq),
        in_specs=in_specs,
        out_specs=pl.BlockSpec((None, tq, d), row),
        out_shape=jax.ShapeDtypeStruct((b, s, d), F32),
        scratch_shapes=[pltpu.VMEM((2 * ns, NSA_TK, wide // 2), F32),
                        pltpu.VMEM((ns, ncp, wide), F32),
                        pltpu.VMEM((ns, n_win * BLK, wide), F32),
                        pltpu.VMEM((ns, 1, wide), F32),
                        pltpu.VMEM((ns, ACC_ROWS, wide), F32),
                        pltpu.VMEM((ns, HEAD_DIM, wide), F32),
                        pltpu.VMEM((ns, ACC_ROWS, wide), F32)],
        compiler_params=_cparams(("arbitrary", "arbitrary")),
        name="nsa_attn",
    )(*args)


def _split(w, sizes):
    offs = np.cumsum([0] + list(sizes))
    return [w[:, int(offs[i]):int(offs[i + 1])] for i in range(len(sizes))]


def _swa_layer(x, rope, gain, w_in, sinks, w_out):
    kvw = SWA_KV * HEAD_DIM
    parts = _split(w_in, [MIX_WIDTH, kvw, kvw, MIX_WIDTH])
    segs = [_Seg(MIX_WIDTH, rope=True, scale=Q_SCALE), _Seg(kvw, rope=True), _Seg(kvw),
            _Seg(MIX_WIDTH)]
    q, k, v, gate = _norm_proj(x, gain, parts, segs, rope)
    o = _banded_attention(q, k, v, SWA_WINDOW - 1, sinks=sinks, gate=gate)
    return o, w_out


def _dilated_layer(x, rope, gain, w_in, w_out, residual=None):
    b, s, _ = x.shape
    kvw = DIL_KV * HEAD_DIM
    sizes, segs = [], []
    for window, dil in DIL_PATTERNS:
        assert s % (dil * BLK) == 0
        sizes += [MIX_WIDTH, kvw, kvw]
        segs += [_Seg(MIX_WIDTH, rope=True, scale=Q_SCALE, dil=dil), _Seg(kvw, rope=True, dil=dil),
                 _Seg(kvw, dil=dil)]
    sizes.append(MIX_WIDTH)
    segs.append(_Seg(MIX_WIDTH))
    res = _norm_proj(x, gain, _split(w_in, sizes), segs, rope, residual)
    if residual is not None:
        x, res = res[0], res[1:]
    gate = res[-1]
    os_, lses, dils = [], [], []
    for gi, (window, dil) in enumerate(DIL_PATTERNS):
        q, k, v = res[3 * gi:3 * gi + 3]
        if dil > 1:
            q, k, v = (t.reshape(b * dil, s // dil, t.shape[-1]) for t in (q, k, v))
        o, lse = _banded_attention(q, k, v, window // dil, want_lse=True)
        if dil > 1:
            o = o.reshape(b, dil, s // dil, MIX_WIDTH)
            lse = lse.reshape(b, dil, s // dil, LANES)
        os_.append(o)
        lses.append(lse)
        dils.append(dil)
    return _dil_combine_out_proj(os_, lses, gate, tuple(dils), w_out, x)


def _fox_layer(x, gain, w_in, b_f, w_out):
    sizes = [MIX_WIDTH, MIX_WIDTH, MIX_WIDTH, N_HEADS, MIX_WIDTH]
    segs = [_Seg(MIX_WIDTH, scale=Q_SCALE, tile=FOX_T), _Seg(MIX_WIDTH),
            _Seg(MIX_WIDTH, tile=FOX_T), _Seg(LANES, dtype=F32), _Seg(MIX_WIDTH)]
    qt, k, vt, f_logit, gate = _norm_proj(x, gain, _split(w_in, sizes), segs)
    kb = _fox_decay(f_logit, b_f)
    return _fox_attention(qt, k, kb, vt, gate), w_out


def _nsa_layer(x, positions, rope, gain, w_in, pe_k, w1_k, w2_k, pe_v, w1_v, w2_v, w_out,
               residual, final_gain):
    b, s, _ = x.shape
    kvw = NSA_KV * HEAD_DIM
    sizes = [MIX_WIDTH] + [kvw] * 6 + [3 * N_HEADS, MIX_WIDTH]
    segs = [_Seg(MIX_WIDTH, rope=True, scale=Q_SCALE, tile=BLK), _Seg(kvw), _Seg(kvw),
            _Seg(kvw, rope=True), _Seg(kvw, tile=NSA_TK), _Seg(kvw, rope=True), _Seg(kvw, tile=BLK),
            _Seg(LANES, dtype=F32), _Seg(MIX_WIDTH)]
    x, q, kc, vc, ks, vs_t, kw, vw_t, g_logit, gate = _norm_proj(
        x, gain, _split(w_in, sizes), segs, rope, residual)
    ns = s // CMP_STRIDE
    cmp_pos = jnp.concatenate(
        [positions[:, CMP_LEN - 1::CMP_STRIDE], positions[:, -1:]], axis=1)[:, :ns]
    cmp_c, cmp_s = _rope_tables(cmp_pos.reshape(-1))
    k_cmp, v_cmp_t = _nsa_compress(kc, vc, pe_k, w1_k, w2_k, pe_v, w1_v, w2_v, cmp_c, cmp_s)
    return _nsa_attention(q, k_cmp, v_cmp_t, ks, vs_t, kw, vw_t, g_logit, gate, w_out, x,
                          final_gain)


def kernel(x, positions, norm_0, w_in_0, sinks_0, w_out_0, norm_1, w_in_1, w_out_1, norm_2, w_in_2, b_f_2, w_out_2, norm_3, w_in_3, cmp_pe_k_3, cmp_w1_k_3, cmp_w2_k_3, cmp_pe_v_3, cmp_w1_v_3, cmp_w2_v_3, w_out_3, final_norm):
    rope = _rope_tables(positions.reshape(-1))
    o, w = _swa_layer(x, rope, norm_0, w_in_0, sinks_0, w_out_0)
    x = _dilated_layer(x, rope, norm_1, w_in_1, w_out_1, residual=(o, w))
    o, w = _fox_layer(x, norm_2, w_in_2, b_f_2, w_out_2)
    return _nsa_layer(x, positions, rope, norm_3, w_in_3, cmp_pe_k_3, cmp_w1_k_3, cmp_w2_k_3,
                      cmp_pe_v_3, cmp_w1_v_3, cmp_w2_v_3, w_out_3, residual=(o, w),
                      final_gain=final_norm)
```

```python
import functools
import math

import numpy as np
import jax
import jax.numpy as jnp
from jax import lax
from jax.experimental import pallas as pl
from jax.experimental.pallas import tpu as pltpu

HEAD_DIM = 64
N_HEADS = 16
N_PAIRS = N_HEADS // 2
MIX_WIDTH = N_HEADS * HEAD_DIM
ROT_DIM = HEAD_DIM // 4
ROT_HALF = ROT_DIM // 2
ROPE_THETA = 500000.0
BLK = 128
LANES = 128
NORM_EPS = 1e-6
MASK_VALUE = -1e30
PAD_SCORE = -3e38
LOG2E = math.log2(math.e)
Q_SCALE = HEAD_DIM ** -0.5 * LOG2E

SWA_KV = 4
SWA_WINDOW = 128
DIL_KV = 4
DIL_PATTERNS = ((128, 1), (512, 4), (2048, 16))
NSA_KV = 2
CMP_LEN = 32
CMP_STRIDE = 16
CMP_HIDDEN = 256
SEL_LEN = 64
SEL_TOPK = 8
NSA_WINDOW = 256
FORCED_SCORE = 1e4

VMEM_LIMIT_BYTES = 56 * 1024 * 1024
PROJ_ROWS = 512
PROJ_COLS = 512
ACC_ROWS = HEAD_DIM + 16
F32 = jnp.float32
BF16 = jnp.bfloat16


def _cparams(sem):
    return pltpu.CompilerParams(dimension_semantics=sem, vmem_limit_bytes=VMEM_LIMIT_BYTES)


def _lane_half(shape):
    return lax.broadcasted_iota(jnp.int32, shape, 1) // HEAD_DIM


def _swap_halves(t):
    return jnp.concatenate([t[:, HEAD_DIM:], t[:, :HEAD_DIM]], axis=1)


def _head_query(qp, e, kv_half):
    qh = jnp.where(_lane_half(qp.shape) == e, qp, jnp.zeros_like(qp))
    if e != kv_half:
        qh = _swap_halves(qh)
    return qh


def _group_queries(q_ref, g, rep):
    tiles = []
    for r in range(rep):
        h = g * rep + r
        tiles.append(_head_query(q_ref[:, (h // 2) * LANES:(h // 2 + 1) * LANES], h % 2, g % 2))
    return jnp.concatenate(tiles, axis=0)


def _dot_nt(a, b):
    return lax.dot_general(a, b, (((1,), (1,)), ((), ())), preferred_element_type=F32)


def _dot(a, b):
    return jnp.dot(a, b, preferred_element_type=F32)


def _silu(x):
    return x * (1.0 / (1.0 + jnp.exp(-x)))


def _loop_in_pairs(count, body):
    odd = count % 2

    @pl.when(odd == 1)
    def _():
        body(0)

    def two(jj, carry):
        body(odd + 2 * jj)
        body(odd + 2 * jj + 1)
        return carry

    lax.fori_loop(0, count // 2, two, 0)


def _with_ones(vt):
    return jnp.concatenate([vt, jnp.ones((ACC_ROWS - HEAD_DIM, vt.shape[1]), BF16)], axis=0)


def _rope_table_kernel(pos_ref, inv_ref, c_ref, s_ref):
    pos = pos_ref[...].astype(F32)
    ang = pos * inv_ref[...]
    d = lax.broadcasted_iota(jnp.int32, ang.shape, 1) % HEAD_DIM
    cos = jnp.cos(ang)
    sin = jnp.sin(ang)
    c_ref[...] = jnp.where(d < ROT_DIM, cos, 1.0)
    s_ref[...] = jnp.where(d < ROT_HALF, -sin, jnp.where(d < ROT_DIM, sin, 0.0))


def _rope_tables(pos_flat):
    t = pos_flat.shape[0]
    rows = min(t, 2048)
    assert t % rows == 0
    inv = jnp.power(ROPE_THETA, -jnp.arange(ROT_HALF, dtype=F32) / ROT_HALF)
    inv_l = jnp.tile(inv, LANES // ROT_HALF)[None, :]
    out = jax.ShapeDtypeStruct((t, LANES), F32)
    return pl.pallas_call(
        _rope_table_kernel,
        grid=(t // rows,),
        in_specs=[pl.BlockSpec((rows, 1), lambda i: (i, 0)),
                  pl.BlockSpec((1, LANES), lambda i: (0, 0))],
        out_specs=[pl.BlockSpec((rows, LANES), lambda i: (i, 0))] * 2,
        out_shape=[out, out],
        compiler_params=_cparams(("arbitrary",)),
        name="rope_tables",
    )(pos_flat[:, None], inv_l)


def _apply_rope(y, c, s):
    outs = []
    for j in range(y.shape[1] // LANES):
        t = y[:, j * LANES:(j + 1) * LANES]
        d = lax.broadcasted_iota(jnp.int32, t.shape, 1) % HEAD_DIM
        partner = jnp.where(d < ROT_HALF, pltpu.roll(t, LANES - ROT_HALF, 1),
                            pltpu.roll(t, ROT_HALF, 1))
        outs.append(t * c + partner * s)
    return outs[0] if len(outs) == 1 else jnp.concatenate(outs, axis=1)


class _Seg:
    def __init__(self, width, rope=False, scale=None, dtype=BF16, dil=1, tile=None):
        self.width, self.rope, self.scale, self.dtype, self.dil = width, rope, scale, dtype, dil
        self.tile = tile
        assert not (tile and dil > 1)


def _norm_proj_kernel(segs, use_rope, has_residual, *refs):
    has_t = any(sg.tile and not sg.rope for sg in segs)
    x_ref, g_ref, w_ref = refs[:3]
    k = 3
    if has_t:
        wt_ref = refs[k]
        k += 1
    if use_rope:
        c_ref, s_ref = refs[k:k + 2]
        k += 2
    if has_residual:
        po_ref, pw_ref = refs[k:k + 2]
        k += 2
        xnew_ref = refs[k]
        k += 1
    out_refs = refs[k:k + len(segs)]
    stage_ref = refs[k + len(segs)] if any(sg.dil > 1 for sg in segs) else None

    x = x_ref[...]
    if has_residual:
        x = x + _dot(po_ref[...], pw_ref[...])
        xnew_ref[...] = x
    var = jnp.mean(x * x, axis=-1, keepdims=True)
    h = (x * lax.rsqrt(var + NORM_EPS) * g_ref[...]).astype(BF16)
    rows = x.shape[0]
    col = 0
    tcol = 0
    for sg, o_ref in zip(segs, out_refs):
        if sg.tile and not sg.rope:
            for c0 in range(0, sg.width, PROJ_COLS):
                cw = min(PROJ_COLS, sg.width - c0)
                yt = _dot_nt(wt_ref[tcol + c0:tcol + c0 + cw, :], h)
                if sg.scale is not None:
                    yt = yt * sg.scale
                for ti in range(rows // sg.tile):
                    o_ref[ti, c0:c0 + cw, :] = yt[:, ti * sg.tile:(ti + 1) * sg.tile].astype(sg.dtype)
            tcol += sg.width
            continue
        for c0 in range(0, sg.width, PROJ_COLS):
            cw = min(PROJ_COLS, sg.width - c0)
            y = _dot(h, w_ref[:, col + c0:col + c0 + cw])
            if sg.rope:
                y = _apply_rope(y, c_ref[...], s_ref[...])
            if sg.scale is not None:
                y = y * sg.scale
            if sg.tile:
                yt = y.T
                for ti in range(rows // sg.tile):
                    o_ref[ti, c0:c0 + cw, :] = yt[:, ti * sg.tile:(ti + 1) * sg.tile].astype(sg.dtype)
            elif sg.dil > 1:
                sub = rows // sg.dil
                for j in range(cw // LANES):
                    stage_ref[j] = y[:, j * LANES:(j + 1) * LANES]
                for r in range(sg.dil):
                    for j in range(cw // LANES):
                        lo = c0 + j * LANES
                        o_ref[r, :, lo:lo + LANES] = (
                            stage_ref[j, pl.ds(r, sub, stride=sg.dil), :].astype(sg.dtype))
            else:
                o_ref[:, c0:c0 + cw] = y.astype(sg.dtype)
        col += sg.width


def _norm_proj(x, gain, w_parts, segs, rope=None, residual=None):
    b, s, d = x.shape
    tm = PROJ_ROWS
    assert s % tm == 0
    w_cols, wt_rows = [], []
    for wp, sg in zip(w_parts, segs):
        if wp.shape[1] < sg.width:
            wp = jnp.pad(wp, ((0, 0), (0, sg.width - wp.shape[1])))
        if sg.tile and not sg.rope:
            wt_rows.append(wp.T)
        else:
            w_cols.append(wp)
    w = jnp.concatenate(w_cols, axis=1).astype(BF16)
    n = w.shape[1]
    use_rope = rope is not None
    once = dict(pipeline_mode=pl.Buffered(1))
    in_specs = [pl.BlockSpec((None, tm, d), lambda bi, i: (bi, i, 0)),
                pl.BlockSpec((1, d), lambda bi, i: (0, 0)),
                pl.BlockSpec((d, n), lambda bi, i: (0, 0), **once)]
    args = [x, gain[None, :], w]
    if wt_rows:
        wt = jnp.concatenate(wt_rows, axis=0).astype(BF16)
        in_specs.append(pl.BlockSpec(wt.shape, lambda bi, i: (0, 0), **once))
        args.append(wt)
    if use_rope:
        in_specs += [pl.BlockSpec((None, tm, LANES), lambda bi, i: (bi, i, 0))] * 2
        args += [rope[0].reshape(b, s, LANES), rope[1].reshape(b, s, LANES)]
    out_specs, out_shape = [], []
    if residual is not None:
        po, pw = residual
        in_specs += [pl.BlockSpec((None, tm, po.shape[-1]), lambda bi, i: (bi, i, 0)),
                     pl.BlockSpec(pw.shape, lambda bi, i: (0, 0), **once)]
        args += [po, pw.astype(BF16)]
        out_shape.append(jax.ShapeDtypeStruct((b, s, d), F32))
        out_specs.append(pl.BlockSpec((None, tm, d), lambda bi, i: (bi, i, 0)))
    for sg in segs:
        if sg.dil > 1:
            assert tm % sg.dil == 0
            out_shape.append(jax.ShapeDtypeStruct((b, sg.dil, s // sg.dil, sg.width), sg.dtype))
            out_specs.append(pl.BlockSpec((None, sg.dil, tm // sg.dil, sg.width),
                                          lambda bi, i: (bi, 0, i, 0)))
        elif sg.tile:
            assert tm % sg.tile == 0
            out_shape.append(jax.ShapeDtypeStruct((b, s // sg.tile, sg.width, sg.tile), sg.dtype))
            out_specs.append(pl.BlockSpec((None, tm // sg.tile, sg.width, sg.tile),
                                          lambda bi, i: (bi, i, 0, 0)))
        else:
            out_shape.append(jax.ShapeDtypeStruct((b, s, sg.width), sg.dtype))
            out_specs.append(pl.BlockSpec((None, tm, sg.width), lambda bi, i: (bi, i, 0)))
    scratch = ([pltpu.VMEM((PROJ_COLS // LANES, tm, LANES), F32)]
               if any(sg.dil > 1 for sg in segs) else [])
    return pl.pallas_call(
        functools.partial(_norm_proj_kernel, segs, use_rope, residual is not None),
        grid=(b, s // tm),
        in_specs=in_specs,
        out_specs=out_specs,
        out_shape=out_shape,
        scratch_shapes=scratch,
        compiler_params=_cparams(("arbitrary", "arbitrary")),
        name="norm_proj",
    )(*args)


BANDED_MAX_QB = 4


def _banded_kernel(n_qb, max_dist, n_prev, n_kv, has_sink, has_gate, want_lse, *refs):
    q_ref = refs[0]
    kp_refs, kc_ref = refs[1:1 + n_prev], refs[1 + n_prev]
    vp_refs, vc_ref = refs[2 + n_prev:2 + 2 * n_prev], refs[2 + 2 * n_prev]
    k_refs = list(kp_refs) + [kc_ref.at[a * BLK:(a + 1) * BLK] for a in range(n_qb)]
    v_refs = list(vp_refs) + [vc_ref.at[a * BLK:(a + 1) * BLK] for a in range(n_qb)]
    k = 3 + 2 * n_prev
    sink_ref = gate_ref = lse_ref = None
    if has_sink:
        sink_ref = refs[k]; k += 1
    if has_gate:
        gate_ref = refs[k]; k += 1
    o_ref = refs[k]; k += 1
    if want_lse:
        lse_ref = refs[k]; k += 1
    ot_sc = refs[k]
    st_sc = refs[k + 1]

    n = pl.program_id(1)
    tq = BLK
    kw = (n_prev + 1) * BLK
    rep = N_HEADS // n_kv
    krow = lax.broadcasted_iota(jnp.int32, (kw, tq), 0) - n_prev * BLK
    dist = lax.broadcasted_iota(jnp.int32, (kw, tq), 1) - krow
    band = (dist >= 0) & (dist <= max_dist)
    lse_rows = [[] for _ in range(n_qb)]
    units = [(qb, g) for qb in range(n_qb) for g in range(n_kv)]

    def scores(u):
        qb, g = units[u]
        sl = slice((g // 2) * LANES, (g // 2 + 1) * LANES)
        kcat = jnp.concatenate([kr[:, sl] for kr in k_refs[qb:qb + n_prev + 1]], axis=0)
        st_sc[u] = _dot_nt(kcat, _group_queries(q_ref.at[qb * tq:(qb + 1) * tq], g, rep))

    oks, vts = {}, {}

    def consume(u):
        qb, g = units[u]
        if qb not in oks:
            oks[qb] = band & ((n * n_qb + qb) * BLK + krow >= 0)
        ok = oks[qb]
        if (qb, g // 2) not in vts:
            sl = slice((g // 2) * LANES, (g // 2 + 1) * LANES)
            vcat = jnp.concatenate([vr[:, sl] for vr in v_refs[qb:qb + n_prev + 1]], axis=0)
            vts[qb, g // 2] = vcat.astype(F32).T.astype(BF16)
        vt = vts[qb, g // 2]
        ms, pts = [], []
        for r in range(rep):
            s_r = jnp.where(ok, st_sc[u, :, r * tq:(r + 1) * tq], MASK_VALUE)
            m = jnp.max(s_r, axis=0, keepdims=True)
            if has_sink:
                h = g * rep + r
                m = jnp.maximum(m, sink_ref[0:1, h:h + 1])
            pts.append(jnp.exp2(s_r - m).astype(BF16))
            ms.append(m)
        acc = _dot(_with_ones(vt[(g % 2) * HEAD_DIM:(g % 2 + 1) * HEAD_DIM, :]),
                   jnp.concatenate(pts, axis=1))
        for r in range(rep):
            h = g * rep + r
            den = acc[HEAD_DIM:HEAD_DIM + 1, r * tq:(r + 1) * tq]
            if has_sink:
                den = den + jnp.exp2(sink_ref[0:1, h:h + 1] - ms[r])
            ot_sc[qb * N_HEADS + h] = acc[:HEAD_DIM, r * tq:(r + 1) * tq] / den
            if want_lse:
                lse_rows[qb].append(ms[r] + jnp.log2(den))

    scores(0)
    scores(1)
    for u in range(len(units)):
        if u + 2 < len(units):
            scores(u + 2)
        consume(u)
    for qb in range(n_qb):
        rows = slice(qb * tq, (qb + 1) * tq)
        for p in range(N_PAIRS):
            sl = slice(p * LANES, (p + 1) * LANES)
            pair = jnp.concatenate([ot_sc[qb * N_HEADS + 2 * p],
                                    ot_sc[qb * N_HEADS + 2 * p + 1]], axis=0).T
            if has_gate:
                pair = pair * _silu(gate_ref[rows, sl].astype(F32))
            o_ref[rows, sl] = pair.astype(o_ref.dtype)
        if want_lse:
            lse_t = jnp.concatenate(
                lse_rows[qb] + [jnp.zeros((LANES - N_HEADS, tq), F32)], axis=0)
            lse_ref[rows, :] = lse_t.T


def _banded_attention(q, k, v, max_dist, sinks=None, gate=None, want_lse=False):
    bq, sq, _ = q.shape
    kvw = k.shape[-1]
    n_kv = kvw // HEAD_DIM
    n_prev = -(-max_dist // BLK)
    n_qb = max(d for d in range(1, BANDED_MAX_QB + 1) if (sq // BLK) % d == 0)
    tq = n_qb * BLK
    assert sq % tq == 0
    row = lambda b, i: (b, i, 0)
    in_specs = [pl.BlockSpec((None, tq, MIX_WIDTH), row)]
    args = [q]
    for arr in (k, v):
        for j in range(n_prev, 0, -1):
            in_specs.append(pl.BlockSpec(
                (None, BLK, kvw), lambda b, i, j=j: (b, jnp.maximum(n_qb * i - j, 0), 0)))
            args.append(arr)
        in_specs.append(pl.BlockSpec((None, tq, kvw), row))
        args.append(arr)
    if sinks is not None:
        in_specs.append(pl.BlockSpec((1, LANES), lambda b, i: (0, 0)))
        args.append(jnp.pad(sinks.astype(F32) * LOG2E, (0, LANES - N_HEADS))[None, :])
    if gate is not None:
        in_specs.append(pl.BlockSpec((None, tq, MIX_WIDTH), row))
        args.append(gate)
    out_specs = [pl.BlockSpec((None, tq, MIX_WIDTH), row)]
    out_shape = [jax.ShapeDtypeStruct((bq, sq, MIX_WIDTH), BF16)]
    if want_lse:
        out_specs.append(pl.BlockSpec((None, tq, LANES), row))
        out_shape.append(jax.ShapeDtypeStruct((bq, sq, LANES), F32))
    res = pl.pallas_call(
        functools.partial(_banded_kernel, n_qb, max_dist, n_prev, n_kv, sinks is not None,
                          gate is not None, want_lse),
        grid=(bq, sq // tq),
        in_specs=in_specs,
        out_specs=out_specs,
        out_shape=out_shape,
        scratch_shapes=[pltpu.VMEM((n_qb * N_HEADS, HEAD_DIM, BLK), F32),
                        pltpu.VMEM((n_qb * n_kv, (n_prev + 1) * BLK,
                                    (N_HEADS // n_kv) * BLK), F32)],
        compiler_params=_cparams(("arbitrary", "arbitrary")),
        name="banded_attn",
    )(*args)
    return res if want_lse else res[0]


def _dil_combine_kernel(dils, *refs):
    ng = len(dils)
    o_refs, l_refs = refs[:ng], refs[ng:2 * ng]
    gate_ref, expand_ref, w_ref, x_ref, out_ref, stage_ref, lstage_ref = refs[2 * ng:]
    rows = out_ref.shape[0]

    def natural(ref, dil, stage, slab):
        sl = slice(slab * LANES, (slab + 1) * LANES)
        if dil == 1:
            return ref[:, sl].astype(F32)
        sub = rows // dil
        for r in range(dil):
            stage[pl.ds(r, sub, stride=dil), :] = ref[r, :, sl].astype(F32)
        return stage[...]

    lses = [natural(l_refs[i], dils[i], lstage_ref, 0) for i in range(ng)]
    mx = functools.reduce(jnp.maximum, lses)
    ws = [jnp.exp2(l - mx) for l in lses]
    tot = functools.reduce(lambda a, c: a + c, ws)
    inv = 1.0 / tot

    def widen(w):
        hi = w.astype(BF16)
        lo = (w - hi.astype(F32)).astype(BF16)
        return _dot(hi, expand_ref[...]) + _dot(lo, expand_ref[...])

    wide = [widen(w * inv) for w in ws[:-1]]
    wide.append(1.0 - functools.reduce(lambda a, c: a + c, wide))
    gated = []
    for p in range(N_PAIRS):
        sl = slice(p * LANES, (p + 1) * LANES)
        acc = None
        for gi in range(ng):
            t = wide[gi][:, sl] * natural(o_refs[gi], dils[gi], stage_ref, p)
            acc = t if acc is None else acc + t
        gated.append((acc * _silu(gate_ref[:, sl].astype(F32))).astype(BF16))
    out_ref[...] = x_ref[...] + _dot(jnp.concatenate(gated, axis=1), w_ref[...])


def _dil_combine_out_proj(os_, lses, gate, dils, w_out, x):
    b, s, d = x.shape
    tm = PROJ_ROWS
    expand = jnp.asarray((np.arange(LANES)[:, None] == np.arange(MIX_WIDTH)[None, :] // HEAD_DIM)
                         .astype(np.float32), BF16)
    in_specs, args = [], []
    for arrs, width in ((os_, MIX_WIDTH), (lses, LANES)):
        for arr, dil in zip(arrs, dils):
            if dil == 1:
                in_specs.append(pl.BlockSpec((None, tm, width), lambda bi, i: (bi, i, 0)))
            else:
                in_specs.append(pl.BlockSpec((None, dil, tm // dil, width),
                                             lambda bi, i: (bi, 0, i, 0)))
            args.append(arr)
    row = lambda bi, i: (bi, i, 0)
    in_specs += [pl.BlockSpec((None, tm, MIX_WIDTH), row),
                 pl.BlockSpec((LANES, MIX_WIDTH), lambda bi, i: (0, 0)),
                 pl.BlockSpec((MIX_WIDTH, d), lambda bi, i: (0, 0)),
                 pl.BlockSpec((None, tm, d), row)]
    args += [gate, expand, w_out.astype(BF16), x]
    return pl.pallas_call(
        functools.partial(_dil_combine_kernel, dils),
        grid=(b, s // tm),
        in_specs=in_specs,
        out_specs=pl.BlockSpec((None, tm, d), row),
        out_shape=jax.ShapeDtypeStruct((b, s, d), F32),
        scratch_shapes=[pltpu.VMEM((tm, LANES), F32), pltpu.VMEM((tm, LANES), F32)],
        compiler_params=_cparams(("arbitrary", "arbitrary")),
        name="dil_combine_out_proj",
    )(*args)


N_BIAS_PIECES = 3


def _fox_decay_kernel(f_ref, b_ref, kb_ref):
    s = f_ref.shape[0]
    x = f_ref[...] + b_ref[...]
    logf = jnp.minimum(x, 0.0) - jnp.log1p(jnp.exp(-jnp.abs(x)))
    r = lax.broadcasted_iota(jnp.int32, (BLK, BLK), 0)
    c = lax.broadcasted_iota(jnp.int32, (BLK, BLK), 1)
    tri = (c <= r).astype(F32)
    lane = lax.broadcasted_iota(jnp.int32, (BLK, LANES), 1)
    carry = jnp.zeros((1, LANES), F32)
    for i in range(s // BLK):
        blk = logf[i * BLK:(i + 1) * BLK, :]
        cs = jnp.dot(tri, blk, preferred_element_type=F32, precision=lax.Precision.HIGHEST) + carry
        carry = cs[BLK - 1:BLK, :]
        rest = cs * (-LOG2E)
        out = jnp.zeros((BLK, LANES), F32)
        for j in range(N_BIAS_PIECES):
            piece = rest.astype(BF16).astype(F32)
            rest = rest - piece
            moved = piece if j == 0 else pltpu.roll(piece, N_HEADS * j, 1)
            out = jnp.where((lane >= N_HEADS * j) & (lane < N_HEADS * (j + 1)), moved, out)
        kb_ref[i * BLK:(i + 1) * BLK, :] = out.astype(BF16)


def _fox_decay(f_logit, b_f):
    b, s, _ = f_logit.shape
    b_pad = jnp.pad(b_f.astype(F32), (0, LANES - N_HEADS))[None, :]
    return pl.pallas_call(
        _fox_decay_kernel,
        grid=(b,),
        in_specs=[pl.BlockSpec((None, s, LANES), lambda bi: (bi, 0, 0)),
                  pl.BlockSpec((1, LANES), lambda bi: (0, 0))],
        out_specs=pl.BlockSpec((None, s, LANES), lambda bi: (bi, 0, 0)),
        out_shape=jax.ShapeDtypeStruct((b, s, LANES), BF16),
        compiler_params=_cparams(("arbitrary",)),
        name="fox_decay",
    )(f_logit, b_pad)


FOX_T = PROJ_ROWS
FOX_HEADS_PER_STEP = 8


def _fox_kernel(qt_ref, k_ref, kb_ref, vt_ref, gate_ref, o_ref, m_sc, acc_sc, st_sc):
    pi = pl.program_id(1)
    i = pl.program_id(2)
    t = FOX_T
    heads = range(FOX_HEADS_PER_STEP)
    row = lax.broadcasted_iota(jnp.int32, (LANES, t), 0)
    zeros = jnp.zeros((HEAD_DIM, t), BF16)
    qx = []
    for e in heads:
        h = FOX_HEADS_PER_STEP * pi + e
        pick = (row % N_HEADS == h) & (row < N_HEADS * N_BIAS_PIECES)
        sel = jnp.where(pick, 1.0, 0.0).astype(BF16)
        mine = qt_ref[0, e * HEAD_DIM:(e + 1) * HEAD_DIM, :]
        top = [mine, zeros] if e % 2 == 0 else [zeros, mine]
        qx.append(jnp.concatenate(top + [sel], axis=0))
        m_sc[e] = jnp.full((1, t), MASK_VALUE, F32)
        acc_sc[e] = jnp.zeros((ACC_ROWS, t), F32)

    tk = t // 2

    def scores(jt, hf, e):
        k0 = pl.multiple_of(jt * t + hf * tk, tk)
        pair = slice((e // 2) * LANES, (e // 2 + 1) * LANES)
        kx = jnp.concatenate([k_ref[pl.ds(k0, tk), pair], kb_ref[pl.ds(k0, tk), :]], axis=1)
        st_sc[e, hf] = _dot(kx, qx[e])

    def consume(jt, hf, e, diag):
        st = st_sc[e, hf]
        if diag:
            ok = (lax.broadcasted_iota(jnp.int32, (tk, t), 0) + hf * tk
                  <= lax.broadcasted_iota(jnp.int32, (tk, t), 1))
            st = jnp.where(ok, st, MASK_VALUE)
        m_old = m_sc[e]
        m_new = jnp.maximum(m_old, jnp.max(st, axis=0, keepdims=True))
        pt = jnp.exp2(st - m_new).astype(BF16)
        alpha = jnp.exp2(m_old - m_new)
        vt = vt_ref[jt, e * HEAD_DIM:(e + 1) * HEAD_DIM, hf * tk:(hf + 1) * tk]
        acc_sc[e] = alpha * acc_sc[e] + _dot(_with_ones(vt), pt)
        m_sc[e] = m_new

    for e in heads:
        scores(0, 0, e)

    def body(jt):
        for e in heads:
            scores(jt, 1, e)
            consume(jt, 0, e, False)
        for e in heads:
            scores(jt + 1, 0, e)
            consume(jt, 1, e, False)

    _loop_in_pairs(i, body)
    for e in heads:
        scores(i, 1, e)
        consume(i, 0, e, True)
    for e in heads:
        consume(i, 1, e, True)
    for p in range(FOX_HEADS_PER_STEP // 2):
        outs = []
        for e in (2 * p, 2 * p + 1):
            acc = acc_sc[e]
            outs.append(acc[:HEAD_DIM] / jnp.maximum(acc[HEAD_DIM:HEAD_DIM + 1], 1e-30))
        sl = slice(p * LANES, (p + 1) * LANES)
        out = jnp.concatenate(outs, axis=0).T
        o_ref[:, sl] = (out * _silu(gate_ref[:, sl].astype(F32))).astype(o_ref.dtype)


def _fox_attention(qt, k, kb, vt, gate):
    b, s, _ = k.shape
    t = FOX_T
    assert s % t == 0 and vt.shape == (b, s // t, MIX_WIDTH, t) and qt.shape == vt.shape
    tile = lambda bi, p, i: (bi, i, p)
    nh = FOX_HEADS_PER_STEP
    wd = nh * HEAD_DIM
    return pl.pallas_call(
        _fox_kernel,
        grid=(b, N_HEADS // nh, s // t),
        in_specs=[pl.BlockSpec((None, 1, wd, t), lambda bi, p, i: (bi, i, p, 0)),
                  pl.BlockSpec((None, s, wd), lambda bi, p, i: (bi, 0, p)),
                  pl.BlockSpec((None, s, LANES), lambda bi, p, i: (bi, 0, 0)),
                  pl.BlockSpec((None, s // t, wd, t), lambda bi, p, i: (bi, 0, p, 0)),
                  pl.BlockSpec((None, t, wd), tile)],
        out_specs=pl.BlockSpec((None, t, wd), tile),
        out_shape=jax.ShapeDtypeStruct((b, s, MIX_WIDTH), BF16),
        scratch_shapes=[pltpu.VMEM((nh, 1, t), F32), pltpu.VMEM((nh, ACC_ROWS, t), F32),
                        pltpu.VMEM((nh, 2, t // 2, t), F32)],
        compiler_params=_cparams(("arbitrary", "arbitrary", "arbitrary")),
        name="fox_attn",
    )(qt, k, kb, vt, gate)


def _gelu_tanh(x):
    return 0.5 * x * (1.0 + jnp.tanh(math.sqrt(2.0 / math.pi) * (x + 0.044715 * (x * x * x))))


def _nsa_compress_kernel(ak_ref, av_ref, pe_ref, wa_ref, wb_ref, w2_ref, w2t_ref, c_ref, s_ref,
                         kc_ref, vct_ref):
    nrow = ak_ref.shape[0]
    for idx, a_ref in enumerate((ak_ref, av_ref)):
        a = a_ref[...].astype(F32)
        xa = (a + pe_ref[idx, 0:1, :]).astype(BF16)
        xb = (a + pe_ref[idx, 1:2, :]).astype(BF16)
        ya = _dot(xa, wa_ref[idx])
        yb = _dot(xb, wb_ref[idx])
        hid = _gelu_tanh(ya + pltpu.roll(yb, nrow - 1, 0)).astype(BF16)
        if idx == 0:
            y = _dot(hid, w2_ref[...])
            kc_ref[...] = _apply_rope(y, c_ref[...], s_ref[...]).astype(kc_ref.dtype)
        else:
            vct_ref[...] = _dot_nt(w2t_ref[...], hid).astype(vct_ref.dtype)


def _nsa_compress(kc, vc, pe_k, w1_k, w2_k, pe_v, w1_v, w2_v, rope_c, rope_s):
    b, s, _ = kc.shape
    ns = s // CMP_STRIDE
    g = NSA_KV
    flat = CMP_STRIDE * g * HEAD_DIM

    def w1_halves(w1):
        w1r = w1.reshape(2, CMP_STRIDE, HEAD_DIM, CMP_HIDDEN)
        outs = []
        for hf in range(2):
            z = jnp.einsum('ldc,gh->lgdhc', w1r[hf], jnp.eye(g, dtype=F32))
            outs.append(z.reshape(flat, g * CMP_HIDDEN))
        return outs

    def pe_halves(pe):
        per = pe.reshape(2, CMP_STRIDE, 1, HEAD_DIM)
        return jnp.broadcast_to(per, (2, CMP_STRIDE, g, HEAD_DIM)).reshape(2, flat)

    def w2_bd(w2):
        z = jnp.einsum('cd,gh->gchd', w2, jnp.eye(g, dtype=F32))
        return z.reshape(g * CMP_HIDDEN, g * HEAD_DIM)

    ka, kb = w1_halves(w1_k)
    va, vb = w1_halves(w1_v)
    wa = jnp.stack([ka, va]).astype(BF16)
    wb = jnp.stack([kb, vb]).astype(BF16)
    w2 = w2_bd(w2_k).astype(BF16)
    w2t = w2_bd(w2_v).T.astype(BF16)
    pe = jnp.stack([pe_halves(pe_k), pe_halves(pe_v)]).astype(F32)
    whole = lambda bi: (0, 0, 0)
    per_b = lambda bi: (bi, 0, 0)
    return pl.pallas_call(
        _nsa_compress_kernel,
        grid=(b,),
        in_specs=[pl.BlockSpec((None, ns, flat), per_b),
                  pl.BlockSpec((None, ns, flat), per_b),
                  pl.BlockSpec((2, 2, flat), whole),
                  pl.BlockSpec((2, flat, g * CMP_HIDDEN), whole),
                  pl.BlockSpec((2, flat, g * CMP_HIDDEN), whole),
                  pl.BlockSpec((g * CMP_HIDDEN, LANES), lambda bi: (0, 0)),
                  pl.BlockSpec((LANES, g * CMP_HIDDEN), lambda bi: (0, 0)),
                  pl.BlockSpec((None, ns, LANES), per_b),
                  pl.BlockSpec((None, ns, LANES), per_b)],
        out_specs=[pl.BlockSpec((None, ns, LANES), per_b),
                   pl.BlockSpec((None, LANES, ns), per_b)],
        out_shape=[jax.ShapeDtypeStruct((b, ns, LANES), BF16),
                   jax.ShapeDtypeStruct((b, LANES, ns), BF16)],
        compiler_params=_cparams(("arbitrary",)),
        name="nsa_compress",
    )(kc.reshape(b, ns, flat), vc.reshape(b, ns, flat), pe, wa, wb, w2, w2t,
      rope_c.reshape(b, ns, LANES), rope_s.reshape(b, ns, LANES))


NSA_QB = 2
NSA_TK = 256
NSA_REP = N_HEADS // NSA_KV


def _nsa_kernel(n_cmp, n_sel, n_win, *refs):
    n_wblk = n_win + NSA_QB - 1
    q_ref, kc_ref, vct_ref, ks_ref, vst_ref = refs[:5]
    kw_refs = refs[5:5 + n_wblk]
    vwt_refs = refs[5 + n_wblk:5 + 2 * n_wblk]
    (gl_ref, gate_ref, ovt_ref, blk_ref, wout_ref, x_ref, fg_ref,
     o_ref) = refs[5 + 2 * n_wblk:13 + 2 * n_wblk]
    st_sc, stc_sc, stw_sc, m_sc, acc_sc, ocmp_sc, owin_sc = refs[13 + 2 * n_wblk:]

    n = pl.program_id(1)
    tq, tk, rep = BLK, NSA_TK, NSA_REP
    streams = [(qb, g) for qb in range(NSA_QB) for g in range(NSA_KV)]
    ncp = kc_ref.shape[0]
    t_lane = [(n * NSA_QB + qb) * tq + lax.broadcasted_iota(jnp.int32, (1, tq), 1)
              for qb in range(NSA_QB)]
    zeros = jnp.zeros((HEAD_DIM, tq), BF16)
    qg = []
    for qb, g in streams:
        tiles = []
        for r in range(rep):
            h = g * rep + r
            mine = q_ref[qb, h * HEAD_DIM:(h + 1) * HEAD_DIM, :]
            tiles.append(jnp.concatenate([mine, zeros] if g % 2 == 0 else [zeros, mine], axis=0))
        qg.append(jnp.concatenate(tiles, axis=1))

    ci = lax.broadcasted_iota(jnp.int32, (ncp, tq), 0)
    kwn = n_win * BLK
    krow = lax.broadcasted_iota(jnp.int32, (kwn, tq), 0) - (n_win - 1) * BLK
    dist = lax.broadcasted_iota(jnp.int32, (kwn, tq), 1) - krow
    band = (dist >= 0) & (dist <= NSA_WINDOW - 1)
    sel_rows = ovt_ref.shape[0]
    rowi = lax.broadcasted_iota(jnp.int32, (sel_rows, tq), 0)
    rowf = rowi.astype(F32)

    def cmp_scores(sg):
        stc_sc[sg] = _dot(kc_ref[...], qg[sg])

    def cmp_consume(sg):
        qb, g = streams[sg]
        cmask = (ci * CMP_STRIDE + (CMP_LEN - 1) <= t_lane[qb]) & (ci < n_cmp)
        ps = []
        for r in range(rep):
            s_r = jnp.where(cmask, stc_sc[sg, :, r * tq:(r + 1) * tq], MASK_VALUE)
            m = jnp.max(s_r, axis=0, keepdims=True)
            pr = jnp.where(cmask, jnp.exp2(s_r - m), 0.0)
            ps.append(pr.astype(BF16))
        lhs = jnp.concatenate([_with_ones(vct_ref[g * HEAD_DIM:(g + 1) * HEAD_DIM, :]),
                               ovt_ref[...]], axis=0)
        res = _dot(lhs, jnp.concatenate(ps, axis=1))
        inv = 1.0 / jnp.maximum(res[HEAD_DIM:HEAD_DIM + 1], 1e-30)
        ocmp_sc[sg] = res[:HEAD_DIM] * inv
        imp = None
        for r in range(rep):
            ql = slice(r * tq, (r + 1) * tq)
            t = res[ACC_ROWS:, ql] * inv[:, ql]
            imp = t if imp is None else imp + t
        return imp

    def select(sg, imp):
        cur = t_lane[streams[sg][0]] // SEL_LEN
        forced = (rowi == 0) | (rowi == cur)
        causal = rowi <= cur
        score = jnp.where(causal, jnp.where(forced, FORCED_SCORE, imp), MASK_VALUE)
        score = jnp.where(rowi < n_sel, score, PAD_SCORE)
        chosen = jnp.zeros((sel_rows, tq), F32)
        for _ in range(min(SEL_TOPK, n_sel)):
            mx = jnp.max(score, axis=0, keepdims=True)
            first = jnp.min(jnp.where(score == mx, rowf, float(LANES)), axis=0, keepdims=True)
            hit = rowf == first
            chosen = jnp.where(hit, 1.0, chosen)
            score = jnp.where(hit, PAD_SCORE, score)
        bias_t = jnp.where(causal & (chosen > 0.5), 0.0, MASK_VALUE)
        bias_t = jnp.concatenate([bias_t, jnp.zeros((LANES - sel_rows, tq), F32)], axis=0)
        bias_t = bias_t.astype(BF16)
        return jnp.concatenate([qg[sg], jnp.concatenate([bias_t] * rep, axis=1)], axis=0)

    def win_scores(sg):
        qb = streams[sg][0]
        kwcat = jnp.concatenate([kr[...] for kr in kw_refs[qb:qb + n_win]], axis=0)
        stw_sc[sg] = _dot(kwcat, qg[sg])

    def win_consume(sg):
        qb, g = streams[sg]
        wok = band & ((n * NSA_QB + qb) * BLK + krow >= 0)
        vwt = jnp.concatenate([vr[0] for vr in vwt_refs[qb:qb + n_win]], axis=1)
        pts = []
        for r in range(rep):
            s_r = jnp.where(wok, stw_sc[sg, :, r * tq:(r + 1) * tq], MASK_VALUE)
            m = jnp.max(s_r, axis=0, keepdims=True)
            pts.append(jnp.exp2(s_r - m).astype(BF16))
        owin_sc[sg] = _dot(_with_ones(vwt[g * HEAD_DIM:(g + 1) * HEAD_DIM, :]),
                           jnp.concatenate(pts, axis=1))

    ns = len(streams)
    cmp_scores(0)
    cmp_scores(1)
    imps = []
    for sg in range(ns):
        if sg + 2 < ns:
            cmp_scores(sg + 2)
        else:
            win_scores(sg + 2 - ns)
        imps.append(cmp_consume(sg))
    qsel = []
    for sg in range(ns):
        if sg + 2 < ns:
            win_scores(sg + 2)
        qsel.append(select(sg, imps[sg]))
        win_consume(sg)

    for sg in range(ns):
        m_sc[sg] = jnp.full((1, rep * tq), MASK_VALUE, F32)
        acc_sc[sg] = jnp.zeros((ACC_ROWS, rep * tq), F32)

    hrep = rep // 2
    wide = hrep * tq
    units = [(sg, hf) for sg in range(ns) for hf in range(2)]
    nu = len(units)

    def scores(c, u):
        sg, hf = units[u]
        k0 = pl.multiple_of(c * tk, tk)
        kx = jnp.concatenate([ks_ref[pl.ds(k0, tk), :], blk_ref[pl.ds(k0, tk), :]], axis=1)
        st_sc[u] = _dot(kx, qsel[sg][:, hf * wide:(hf + 1) * wide])

    def consume(c, u, last):
        sg, hf = units[u]
        qb, g = streams[sg]
        lanes = slice(hf * wide, (hf + 1) * wide)
        if last:
            tok_ok = c * tk + lax.broadcasted_iota(jnp.int32, (tk, tq), 0) <= t_lane[qb]
        m_old = m_sc[sg, :, lanes]
        ms, pts = [], []
        for r in range(hrep):
            s_r = st_sc[u, :, r * tq:(r + 1) * tq]
            if last:
                s_r = jnp.where(tok_ok, s_r, MASK_VALUE)
            m_new = jnp.maximum(m_old[:, r * tq:(r + 1) * tq], jnp.max(s_r, axis=0, keepdims=True))
            pts.append(jnp.exp2(s_r - m_new).astype(BF16))
            ms.append(m_new)
        m_new = jnp.concatenate(ms, axis=1)
        alpha = jnp.exp2(m_old - m_new)
        vt = vst_ref[c, g * HEAD_DIM:(g + 1) * HEAD_DIM, :]
        acc_sc[sg, :, lanes] = (alpha * acc_sc[sg, :, lanes]
                                + _dot(_with_ones(vt), jnp.concatenate(pts, axis=1)))
        m_sc[sg, :, lanes] = m_new

    n_steps = ((n * NSA_QB + 1) * tq + tk - 1) // tk
    scores(0, 0)
    scores(0, 1)

    def body(c):
        for u in range(nu):
            if u + 2 < nu:
                scores(c, u + 2)
            else:
                scores(c + 1, u + 2 - nu)
            consume(c, u, False)

    _loop_in_pairs(n_steps - 1, body)
    for u in range(nu):
        if u + 2 < nu:
            scores(n_steps - 1, u + 2)
        consume(n_steps - 1, u, True)

    for qb in range(NSA_QB):
        tok = slice(qb * tq, (qb + 1) * tq)
        gate_t = (1.0 / (1.0 + jnp.exp(-gl_ref[tok, :]))).T
        pairs = []
        for p in range(N_PAIRS):
            sl = slice(p * LANES, (p + 1) * LANES)
            g = (2 * p) // rep
            sg = qb * NSA_KV + g
            rows = []
            for e in range(2):
                h = 2 * p + e
                ql = slice((h - g * rep) * tq, (h - g * rep + 1) * tq)
                slc, win = acc_sc[sg][:, ql], owin_sc[sg][:, ql]
                branches = (ocmp_sc[sg][:, ql],
                            slc[:HEAD_DIM] / jnp.maximum(slc[HEAD_DIM:HEAD_DIM + 1], 1e-30),
                            win[:HEAD_DIM] / win[HEAD_DIM:HEAD_DIM + 1])
                out = None
                for j, br in enumerate(branches):
                    t = gate_t[3 * h + j:3 * h + j + 1, :] * br
                    out = t if out is None else out + t
                rows.append(out)
            pair = jnp.concatenate(rows, axis=0).T
            pairs.append((pair * _silu(gate_ref[tok, sl].astype(F32))).astype(BF16))
        y = x_ref[tok, :] + _dot(jnp.concatenate(pairs, axis=1), wout_ref[...])
        var = jnp.mean(y * y, axis=-1, keepdims=True)
        o_ref[tok, :] = y * lax.rsqrt(var + NORM_EPS) * fg_ref[...]


def _selection_overlap_t(n_cmp_pad, n_cmp, n_sel):
    cs = np.arange(n_cmp_pad) * CMP_STRIDE
    js = np.arange(LANES) * SEL_LEN
    ov = np.minimum(cs[None, :] + CMP_LEN, js[:, None] + SEL_LEN) - np.maximum(cs[None, :], js[:, None])
    ov = (np.clip(ov, 0, None) / CMP_LEN).astype(np.float32)
    ov[:, n_cmp:] = 0.0
    ov[n_sel:, :] = 0.0
    return ov


def _nsa_attention(q_t, k_cmp, v_cmp_t, ks, vs_t, kw, vw_t, g_logit, gate, w_out, x, final_gain):
    b, s, d = x.shape
    tq = NSA_QB * BLK
    ncp = k_cmp.shape[1]
    n_cmp = s // CMP_STRIDE - 1
    n_sel = s // SEL_LEN
    n_win = -(-(NSA_WINDOW - 1) // BLK) + 1
    assert n_sel <= LANES and s % NSA_TK == 0 and tq == NSA_TK
    sel_rows = -(-n_sel // 16) * 16
    ov_t = jnp.asarray(_selection_overlap_t(ncp, n_cmp, n_sel)[:sel_rows], BF16)
    row = lambda bi, i: (bi, i, 0)
    per_b = lambda bi, i: (bi, 0, 0)
    in_specs = [pl.BlockSpec((None, NSA_QB, MIX_WIDTH, BLK), lambda bi, i: (bi, i, 0, 0)),
                pl.BlockSpec((None, ncp, LANES), per_b),
                pl.BlockSpec((None, LANES, ncp), per_b),
                pl.BlockSpec((None, s, LANES), per_b),
                pl.BlockSpec((None, s // NSA_TK, LANES, NSA_TK), lambda bi, i: (bi, 0, 0, 0))]
    args = [q_t, k_cmp, v_cmp_t, ks, vs_t]
    for j in range(n_win - 1, -NSA_QB, -1):
        in_specs.append(pl.BlockSpec(
            (None, BLK, LANES), lambda bi, i, j=j: (bi, jnp.maximum(NSA_QB * i - j, 0), 0)))
        args.append(kw)
    for j in range(n_win - 1, -NSA_QB, -1):
        in_specs.append(pl.BlockSpec(
            (None, 1, LANES, BLK), lambda bi, i, j=j: (bi, jnp.maximum(NSA_QB * i - j, 0), 0, 0)))
        args.append(vw_t)
    block_onehot = jnp.asarray(
        (np.arange(s)[:, None] // SEL_LEN == np.arange(LANES)[None, :]).astype(np.float32), BF16)
    in_specs += [pl.BlockSpec((None, tq, LANES), row),
                 pl.BlockSpec((None, tq, MIX_WIDTH), row),
                 pl.BlockSpec((sel_rows, ncp), lambda bi, i: (0, 0)),
                 pl.BlockSpec((s, LANES), lambda bi, i: (0, 0)),
                 pl.BlockSpec((MIX_WIDTH, d), lambda bi, i: (0, 0), pipeline_mode=pl.Buffered(1)),
                 pl.BlockSpec((None, tq, d), row),
                 pl.BlockSpec((1, d), lambda bi, i: (0, 0))]
    args += [g_logit, gate, ov_t, block_onehot, w_out.astype(BF16), x, final_gain[None, :]]
    wide = NSA_REP * BLK
    ns = NSA_QB * NSA_KV
    return pl.pallas_call(
        functools.partial(_nsa_kernel, n_cmp, n_sel, n_win),
        grid=(b, s // tq),
        in_specs=in_specs,
        out_specs=pl.BlockSpec((None, tq, d), row),
        out_shape=jax.ShapeDtypeStruct((b, s, d), F32),
        scratch_shapes=[pltpu.VMEM((2 * ns, NSA_TK, wide // 2), F32),
                        pltpu.VMEM((ns, ncp, wide), F32),
                        pltpu.VMEM((ns, n_win * BLK, wide), F32),
                        pltpu.VMEM((ns, 1, wide), F32),
                        pltpu.VMEM((ns, ACC_ROWS, wide), F32),
                        pltpu.VMEM((ns, HEAD_DIM, wide), F32),
                        pltpu.VMEM((ns, ACC_ROWS, wide), F32)],
        compiler_params=_cparams(("arbitrary", "arbitrary")),
        name="nsa_attn",
    )(*args)


def _split(w, sizes):
    offs = np.cumsum([0] + list(sizes))
    return [w[:, int(offs[i]):int(offs[i + 1])] for i in range(len(sizes))]


def _swa_layer(x, rope, gain, w_in, sinks, w_out):
    kvw = SWA_KV * HEAD_DIM
    parts = _split(w_in, [MIX_WIDTH, kvw, kvw, MIX_WIDTH])
    segs = [_Seg(MIX_WIDTH, rope=True, scale=Q_SCALE), _Seg(kvw, rope=True), _Seg(kvw),
            _Seg(MIX_WIDTH)]
    q, k, v, gate = _norm_proj(x, gain, parts, segs, rope)
    o = _banded_attention(q, k, v, SWA_WINDOW - 1, sinks=sinks, gate=gate)
    return o, w_out


def _dilated_layer(x, rope, gain, w_in, w_out, residual=None):
    b, s, _ = x.shape
    kvw = DIL_KV * HEAD_DIM
    sizes, segs = [], []
    for window, dil in DIL_PATTERNS:
        assert s % (dil * BLK) == 0
        sizes += [MIX_WIDTH, kvw, kvw]
        segs += [_Seg(MIX_WIDTH, rope=True, scale=Q_SCALE, dil=dil), _Seg(kvw, rope=True, dil=dil),
                 _Seg(kvw, dil=dil)]
    sizes.append(MIX_WIDTH)
    segs.append(_Seg(MIX_WIDTH))
    res = _norm_proj(x, gain, _split(w_in, sizes), segs, rope, residual)
    if residual is not None:
        x, res = res[0], res[1:]
    gate = res[-1]
    os_, lses, dils = [], [], []
    for gi, (window, dil) in enumerate(DIL_PATTERNS):
        q, k, v = res[3 * gi:3 * gi + 3]
        if dil > 1:
            q, k, v = (t.reshape(b * dil, s // dil, t.shape[-1]) for t in (q, k, v))
        o, lse = _banded_attention(q, k, v, window // dil, want_lse=True)
        if dil > 1:
            o = o.reshape(b, dil, s // dil, MIX_WIDTH)
            lse = lse.reshape(b, dil, s // dil, LANES)
        os_.append(o)
        lses.append(lse)
        dils.append(dil)
    return _dil_combine_out_proj(os_, lses, gate, tuple(dils), w_out, x)


def _fox_layer(x, gain, w_in, b_f, w_out):
    sizes = [MIX_WIDTH, MIX_WIDTH, MIX_WIDTH, N_HEADS, MIX_WIDTH]
    segs = [_Seg(MIX_WIDTH, scale=Q_SCALE, tile=FOX_T), _Seg(MIX_WIDTH),
            _Seg(MIX_WIDTH, tile=FOX_T), _Seg(LANES, dtype=F32), _Seg(MIX_WIDTH)]
    qt, k, vt, f_logit, gate = _norm_proj(x, gain, _split(w_in, sizes), segs)
    kb = _fox_decay(f_logit, b_f)
    return _fox_attention(qt, k, kb, vt, gate), w_out


def _nsa_layer(x, positions, rope, gain, w_in, pe_k, w1_k, w2_k, pe_v, w1_v, w2_v, w_out,
               residual, final_gain):
    b, s, _ = x.shape
    kvw = NSA_KV * HEAD_DIM
    sizes = [MIX_WIDTH] + [kvw] * 6 + [3 * N_HEADS, MIX_WIDTH]
    segs = [_Seg(MIX_WIDTH, rope=True, scale=Q_SCALE, tile=BLK), _Seg(kvw), _Seg(kvw),
            _Seg(kvw, rope=True), _Seg(kvw, tile=NSA_TK), _Seg(kvw, rope=True), _Seg(kvw, tile=BLK),
            _Seg(LANES, dtype=F32), _Seg(MIX_WIDTH)]
    x, q, kc, vc, ks, vs_t, kw, vw_t, g_logit, gate = _norm_proj(
        x, gain, _split(w_in, sizes), segs, rope, residual)
    ns = s // CMP_STRIDE
    cmp_pos = jnp.concatenate(
        [positions[:, CMP_LEN - 1::CMP_STRIDE], positions[:, -1:]], axis=1)[:, :ns]
    cmp_c, cmp_s = _rope_tables(cmp_pos.reshape(-1))
    k_cmp, v_cmp_t = _nsa_compress(kc, vc, pe_k, w1_k, w2_k, pe_v, w1_v, w2_v, cmp_c, cmp_s)
    return _nsa_attention(q, k_cmp, v_cmp_t, ks, vs_t, kw, vw_t, g_logit, gate, w_out, x,
                          final_gain)


def kernel(x, positions, norm_0, w_in_0, sinks_0, w_out_0, norm_1, w_in_1, w_out_1, norm_2, w_in_2, b_f_2, w_out_2, norm_3, w_in_3, cmp_pe_k_3, cmp_w1_k_3, cmp_w2_k_3, cmp_pe_v_3, cmp_w1_v_3, cmp_w2_v_3, w_out_3, final_norm):
    rope = _rope_tables(positions.reshape(-1))
    o, w = _swa_layer(x, rope, norm_0, w_in_0, sinks_0, w_out_0)
    x = _dilated_layer(x, rope, norm_1, w_in_1, w_out_1, residual=(o, w))
    o, w = _fox_layer(x, norm_2, w_in_2, b_f_2, w_out_2)
    return _nsa_layer(x, positions, rope, norm_3, w_in_3, cmp_pe_k_3, cmp_w1_k_3, cmp_w2_k_3,
                      cmp_pe_v_3, cmp_w1_v_3, cmp_w2_v_3, w_out_3, residual=(o, w),
                      final_gain=final_norm)
```

```python
import functools
import math

import numpy as np
import jax
import jax.numpy as jnp
from jax import lax
from jax.experimental import pallas as pl
from jax.experimental.pallas import tpu as pltpu

HEAD_DIM = 64
N_HEADS = 16
N_PAIRS = N_HEADS // 2
MIX_WIDTH = N_HEADS * HEAD_DIM
ROT_DIM = HEAD_DIM // 4
ROT_HALF = ROT_DIM // 2
ROPE_THETA = 500000.0
BLK = 128
LANES = 128
NORM_EPS = 1e-6
MASK_VALUE = -1e30
PAD_SCORE = -3e38
LOG2E = math.log2(math.e)
Q_SCALE = HEAD_DIM ** -0.5 * LOG2E

SWA_KV = 4
SWA_WINDOW = 128
DIL_KV = 4
DIL_PATTERNS = ((128, 1), (512, 4), (2048, 16))
NSA_KV = 2
CMP_LEN = 32
CMP_STRIDE = 16
CMP_HIDDEN = 256
SEL_LEN = 64
SEL_TOPK = 8
NSA_WINDOW = 256
FORCED_SCORE = 1e4

VMEM_LIMIT_BYTES = 56 * 1024 * 1024
PROJ_ROWS = 512
PROJ_COLS = 512
ACC_ROWS = HEAD_DIM + 16
F32 = jnp.float32
BF16 = jnp.bfloat16


def _cparams(sem):
    return pltpu.CompilerParams(dimension_semantics=sem, vmem_limit_bytes=VMEM_LIMIT_BYTES)


def _lane_half(shape):
    return lax.broadcasted_iota(jnp.int32, shape, 1) // HEAD_DIM


def _swap_halves(t):
    return jnp.concatenate([t[:, HEAD_DIM:], t[:, :HEAD_DIM]], axis=1)


def _head_query(qp, e, kv_half):
    qh = jnp.where(_lane_half(qp.shape) == e, qp, jnp.zeros_like(qp))
    if e != kv_half:
        qh = _swap_halves(qh)
    return qh


def _group_queries(q_ref, g, rep):
    tiles = []
    for r in range(rep):
        h = g * rep + r
        tiles.append(_head_query(q_ref[:, (h // 2) * LANES:(h // 2 + 1) * LANES], h % 2, g % 2))
    return jnp.concatenate(tiles, axis=0)


def _dot_nt(a, b):
    return lax.dot_general(a, b, (((1,), (1,)), ((), ())), preferred_element_type=F32)


def _dot(a, b):
    return jnp.dot(a, b, preferred_element_type=F32)


def _silu(x):
    return x * (1.0 / (1.0 + jnp.exp(-x)))


def _loop_in_pairs(count, body):
    odd = count % 2

    @pl.when(odd == 1)
    def _():
        body(0)

    def two(jj, carry):
        body(odd + 2 * jj)
        body(odd + 2 * jj + 1)
        return carry

    lax.fori_loop(0, count // 2, two, 0)


def _with_ones(vt):
    return jnp.concatenate([vt, jnp.ones((ACC_ROWS - HEAD_DIM, vt.shape[1]), BF16)], axis=0)


def _rope_table_kernel(pos_ref, inv_ref, c_ref, s_ref):
    pos = pos_ref[...].astype(F32)
    ang = pos * inv_ref[...]
    d = lax.broadcasted_iota(jnp.int32, ang.shape, 1) % HEAD_DIM
    cos = jnp.cos(ang)
    sin = jnp.sin(ang)
    c_ref[...] = jnp.where(d < ROT_DIM, cos, 1.0)
    s_ref[...] = jnp.where(d < ROT_HALF, -sin, jnp.where(d < ROT_DIM, sin, 0.0))


def _rope_tables(pos_flat):
    t = pos_flat.shape[0]
    rows = min(t, 2048)
    assert t % rows == 0
    inv = jnp.power(ROPE_THETA, -jnp.arange(ROT_HALF, dtype=F32) / ROT_HALF)
    inv_l = jnp.tile(inv, LANES // ROT_HALF)[None, :]
    out = jax.ShapeDtypeStruct((t, LANES), F32)
    return pl.pallas_call(
        _rope_table_kernel,
        grid=(t // rows,),
        in_specs=[pl.BlockSpec((rows, 1), lambda i: (i, 0)),
                  pl.BlockSpec((1, LANES), lambda i: (0, 0))],
        out_specs=[pl.BlockSpec((rows, LANES), lambda i: (i, 0))] * 2,
        out_shape=[out, out],
        compiler_params=_cparams(("arbitrary",)),
        name="rope_tables",
    )(pos_flat[:, None], inv_l)


def _apply_rope(y, c, s):
    outs = []
    for j in range(y.shape[1] // LANES):
        t = y[:, j * LANES:(j + 1) * LANES]
        d = lax.broadcasted_iota(jnp.int32, t.shape, 1) % HEAD_DIM
        partner = jnp.where(d < ROT_HALF, pltpu.roll(t, LANES - ROT_HALF, 1),
                            pltpu.roll(t, ROT_HALF, 1))
        outs.append(t * c + partner * s)
    return outs[0] if len(outs) == 1 else jnp.concatenate(outs, axis=1)


class _Seg:
    def __init__(self, width, rope=False, scale=None, dtype=BF16, tile=None):
        self.width, self.rope, self.scale, self.dtype, self.tile = width, rope, scale, dtype, tile


def _norm_proj_kernel(segs, use_rope, has_residual, *refs):
    has_t = any(sg.tile and not sg.rope for sg in segs)
    x_ref, g_ref, w_ref = refs[:3]
    k = 3
    if has_t:
        wt_ref = refs[k]
        k += 1
    if use_rope:
        c_ref, s_ref = refs[k:k + 2]
        k += 2
    if has_residual:
        po_ref, pw_ref = refs[k:k + 2]
        k += 2
        xnew_ref = refs[k]
        k += 1
    out_refs = refs[k:k + len(segs)]

    x = x_ref[...]
    if has_residual:
        x = x + _dot(po_ref[...], pw_ref[...])
        xnew_ref[...] = x
    var = jnp.mean(x * x, axis=-1, keepdims=True)
    h = (x * lax.rsqrt(var + NORM_EPS) * g_ref[...]).astype(BF16)
    rows = x.shape[0]
    col = 0
    tcol = 0
    for sg, o_ref in zip(segs, out_refs):
        if sg.tile and not sg.rope:
            for c0 in range(0, sg.width, PROJ_COLS):
                cw = min(PROJ_COLS, sg.width - c0)
                yt = _dot_nt(wt_ref[tcol + c0:tcol + c0 + cw, :], h)
                if sg.scale is not None:
                    yt = yt * sg.scale
                for ti in range(rows // sg.tile):
                    o_ref[ti, c0:c0 + cw, :] = yt[:, ti * sg.tile:(ti + 1) * sg.tile].astype(sg.dtype)
            tcol += sg.width
            continue
        for c0 in range(0, sg.width, PROJ_COLS):
            cw = min(PROJ_COLS, sg.width - c0)
            y = _dot(h, w_ref[:, col + c0:col + c0 + cw])
            if sg.rope:
                y = _apply_rope(y, c_ref[...], s_ref[...])
            if sg.scale is not None:
                y = y * sg.scale
            if sg.tile:
                yt = y.T
                for ti in range(rows // sg.tile):
                    o_ref[ti, c0:c0 + cw, :] = yt[:, ti * sg.tile:(ti + 1) * sg.tile].astype(sg.dtype)
            else:
                o_ref[:, c0:c0 + cw] = y.astype(sg.dtype)
        col += sg.width


def _norm_proj(x, gain, w_parts, segs, rope=None, residual=None):
    b, s, d = x.shape
    tm = PROJ_ROWS
    assert s % tm == 0
    w_cols, wt_rows = [], []
    for wp, sg in zip(w_parts, segs):
        if wp.shape[1] < sg.width:
            wp = jnp.pad(wp, ((0, 0), (0, sg.width - wp.shape[1])))
        if sg.tile and not sg.rope:
            wt_rows.append(wp.T)
        else:
            w_cols.append(wp)
    w = jnp.concatenate(w_cols, axis=1).astype(BF16)
    n = w.shape[1]
    use_rope = rope is not None
    once = dict(pipeline_mode=pl.Buffered(1))
    in_specs = [pl.BlockSpec((None, tm, d), lambda bi, i: (bi, i, 0)),
                pl.BlockSpec((1, d), lambda bi, i: (0, 0)),
                pl.BlockSpec((d, n), lambda bi, i: (0, 0), **once)]
    args = [x, gain[None, :], w]
    if wt_rows:
        wt = jnp.concatenate(wt_rows, axis=0).astype(BF16)
        in_specs.append(pl.BlockSpec(wt.shape, lambda bi, i: (0, 0), **once))
        args.append(wt)
    if use_rope:
        in_specs += [pl.BlockSpec((None, tm, LANES), lambda bi, i: (bi, i, 0))] * 2
        args += [rope[0].reshape(b, s, LANES), rope[1].reshape(b, s, LANES)]
    out_specs, out_shape = [], []
    if residual is not None:
        po, pw = residual
        in_specs += [pl.BlockSpec((None, tm, po.shape[-1]), lambda bi, i: (bi, i, 0)),
                     pl.BlockSpec(pw.shape, lambda bi, i: (0, 0), **once)]
        args += [po, pw.astype(BF16)]
        out_shape.append(jax.ShapeDtypeStruct((b, s, d), F32))
        out_specs.append(pl.BlockSpec((None, tm, d), lambda bi, i: (bi, i, 0)))
    for sg in segs:
        if sg.tile:
            assert tm % sg.tile == 0
            out_shape.append(jax.ShapeDtypeStruct((b, s // sg.tile, sg.width, sg.tile), sg.dtype))
            out_specs.append(pl.BlockSpec((None, tm // sg.tile, sg.width, sg.tile),
                                          lambda bi, i: (bi, i, 0, 0)))
        else:
            out_shape.append(jax.ShapeDtypeStruct((b, s, sg.width), sg.dtype))
            out_specs.append(pl.BlockSpec((None, tm, sg.width), lambda bi, i: (bi, i, 0)))
    return pl.pallas_call(
        functools.partial(_norm_proj_kernel, segs, use_rope, residual is not None),
        grid=(b, s // tm),
        in_specs=in_specs,
        out_specs=out_specs,
        out_shape=out_shape,
        compiler_params=_cparams(("arbitrary", "arbitrary")),
        name="norm_proj",
    )(*args)


BANDED_MAX_QB = 4


def _banded_kernel(n_qb, max_dist, n_prev, n_kv, has_sink, has_gate, want_lse, *refs):
    q_ref = refs[0]
    kp_refs, kc_ref = refs[1:1 + n_prev], refs[1 + n_prev]
    vp_refs, vc_ref = refs[2 + n_prev:2 + 2 * n_prev], refs[2 + 2 * n_prev]
    k_refs = list(kp_refs) + [kc_ref.at[a * BLK:(a + 1) * BLK] for a in range(n_qb)]
    v_refs = list(vp_refs) + [vc_ref.at[a * BLK:(a + 1) * BLK] for a in range(n_qb)]
    k = 3 + 2 * n_prev
    sink_ref = gate_ref = lse_ref = None
    if has_sink:
        sink_ref = refs[k]; k += 1
    if has_gate:
        gate_ref = refs[k]; k += 1
    o_ref = refs[k]; k += 1
    if want_lse:
        lse_ref = refs[k]; k += 1
    ot_sc = refs[k]
    st_sc = refs[k + 1]

    n = pl.program_id(1)
    tq = BLK
    kw = (n_prev + 1) * BLK
    rep = N_HEADS // n_kv
    krow = lax.broadcasted_iota(jnp.int32, (kw, tq), 0) - n_prev * BLK
    dist = lax.broadcasted_iota(jnp.int32, (kw, tq), 1) - krow
    band = (dist >= 0) & (dist <= max_dist)
    lse_rows = [[] for _ in range(n_qb)]
    units = [(qb, g) for qb in range(n_qb) for g in range(n_kv)]

    def scores(u):
        qb, g = units[u]
        sl = slice((g // 2) * LANES, (g // 2 + 1) * LANES)
        kcat = jnp.concatenate([kr[:, sl] for kr in k_refs[qb:qb + n_prev + 1]], axis=0)
        st_sc[u] = _dot_nt(kcat, _group_queries(q_ref.at[qb * tq:(qb + 1) * tq], g, rep))

    oks, vts = {}, {}

    def consume(u):
        qb, g = units[u]
        if qb not in oks:
            oks[qb] = band & ((n * n_qb + qb) * BLK + krow >= 0)
        ok = oks[qb]
        if (qb, g // 2) not in vts:
            sl = slice((g // 2) * LANES, (g // 2 + 1) * LANES)
            vcat = jnp.concatenate([vr[:, sl] for vr in v_refs[qb:qb + n_prev + 1]], axis=0)
            vts[qb, g // 2] = vcat.astype(F32).T.astype(BF16)
        vt = vts[qb, g // 2]
        ms, pts = [], []
        for r in range(rep):
            s_r = jnp.where(ok, st_sc[u, :, r * tq:(r + 1) * tq], MASK_VALUE)
            m = jnp.max(s_r, axis=0, keepdims=True)
            if has_sink:
                h = g * rep + r
                m = jnp.maximum(m, sink_ref[0:1, h:h + 1])
            pts.append(jnp.exp2(s_r - m).astype(BF16))
            ms.append(m)
        acc = _dot(_with_ones(vt[(g % 2) * HEAD_DIM:(g % 2 + 1) * HEAD_DIM, :]),
                   jnp.concatenate(pts, axis=1))
        for r in range(rep):
            h = g * rep + r
            den = acc[HEAD_DIM:HEAD_DIM + 1, r * tq:(r + 1) * tq]
            if has_sink:
                den = den + jnp.exp2(sink_ref[0:1, h:h + 1] - ms[r])
            ot_sc[qb * N_HEADS + h] = acc[:HEAD_DIM, r * tq:(r + 1) * tq] / den
            if want_lse:
                lse_rows[qb].append(ms[r] + jnp.log2(den))

    scores(0)
    scores(1)
    for u in range(len(units)):
        if u + 2 < len(units):
            scores(u + 2)
        consume(u)
    for qb in range(n_qb):
        rows = slice(qb * tq, (qb + 1) * tq)
        for p in range(N_PAIRS):
            sl = slice(p * LANES, (p + 1) * LANES)
            pair = jnp.concatenate([ot_sc[qb * N_HEADS + 2 * p],
                                    ot_sc[qb * N_HEADS + 2 * p + 1]], axis=0).T
            if has_gate:
                pair = pair * _silu(gate_ref[rows, sl].astype(F32))
            o_ref[rows, sl] = pair.astype(o_ref.dtype)
        if want_lse:
            lse_t = jnp.concatenate(
                lse_rows[qb] + [jnp.zeros((LANES - N_HEADS, tq), F32)], axis=0)
            lse_ref[rows, :] = lse_t.T


def _banded_attention(q, k, v, max_dist, dil=1, sinks=None, gate=None, want_lse=False):
    b, s, _ = q.shape
    kvw = k.shape[-1]
    n_kv = kvw // HEAD_DIM
    n_prev = -(-max_dist // BLK)
    sq = s // dil
    n_qb = max(d for d in range(1, BANDED_MAX_QB + 1) if (sq // BLK) % d == 0)
    tq = n_qb * BLK
    assert s % dil == 0 and sq % tq == 0
    view = lambda t: t.reshape(b, sq, dil * t.shape[-1])
    row = lambda g, i: (g // dil, i, g % dil)
    in_specs = [pl.BlockSpec((None, tq, MIX_WIDTH), row)]
    args = [view(q)]
    for arr in (k, v):
        for j in range(n_prev, 0, -1):
            in_specs.append(pl.BlockSpec(
                (None, BLK, kvw),
                lambda g, i, j=j: (g // dil, jnp.maximum(n_qb * i - j, 0), g % dil)))
            args.append(view(arr))
        in_specs.append(pl.BlockSpec((None, tq, kvw), row))
        args.append(view(arr))
    if sinks is not None:
        in_specs.append(pl.BlockSpec((1, LANES), lambda g, i: (0, 0)))
        args.append(jnp.pad(sinks.astype(F32) * LOG2E, (0, LANES - N_HEADS))[None, :])
    if gate is not None:
        in_specs.append(pl.BlockSpec((None, tq, MIX_WIDTH), row))
        args.append(view(gate))
    out_specs = [pl.BlockSpec((None, tq, MIX_WIDTH), row)]
    out_shape = [jax.ShapeDtypeStruct((b, sq, dil * MIX_WIDTH), BF16)]
    if want_lse:
        out_specs.append(pl.BlockSpec((None, tq, LANES), row))
        out_shape.append(jax.ShapeDtypeStruct((b, sq, dil * LANES), F32))
    res = pl.pallas_call(
        functools.partial(_banded_kernel, n_qb, max_dist, n_prev, n_kv, sinks is not None,
                          gate is not None, want_lse),
        grid=(b * dil, sq // tq),
        in_specs=in_specs,
        out_specs=out_specs,
        out_shape=out_shape,
        scratch_shapes=[pltpu.VMEM((n_qb * N_HEADS, HEAD_DIM, BLK), F32),
                        pltpu.VMEM((n_qb * n_kv, (n_prev + 1) * BLK,
                                    (N_HEADS // n_kv) * BLK), F32)],
        compiler_params=_cparams(("arbitrary", "arbitrary")),
        name="banded_attn",
    )(*args)
    res = [t.reshape(b, s, t.shape[-1] // dil) for t in res]
    return res if want_lse else res[0]


def _dil_combine_kernel(ng, *refs):
    o_refs, l_refs = refs[:ng], refs[ng:2 * ng]
    gate_ref, expand_ref, w_ref, x_ref, out_ref = refs[2 * ng:]

    lses = [l_ref[...] for l_ref in l_refs]
    mx = functools.reduce(jnp.maximum, lses)
    ws = [jnp.exp2(l - mx) for l in lses]
    tot = functools.reduce(lambda a, c: a + c, ws)
    inv = 1.0 / tot

    def widen(w):
        hi = w.astype(BF16)
        lo = (w - hi.astype(F32)).astype(BF16)
        return _dot(hi, expand_ref[...]) + _dot(lo, expand_ref[...])

    wide = [widen(w * inv) for w in ws[:-1]]
    wide.append(1.0 - functools.reduce(lambda a, c: a + c, wide))
    gated = []
    for p in range(N_PAIRS):
        sl = slice(p * LANES, (p + 1) * LANES)
        acc = None
        for gi in range(ng):
            t = wide[gi][:, sl] * o_refs[gi][:, sl].astype(F32)
            acc = t if acc is None else acc + t
        gated.append((acc * _silu(gate_ref[:, sl].astype(F32))).astype(BF16))
    out_ref[...] = x_ref[...] + _dot(jnp.concatenate(gated, axis=1), w_ref[...])


def _dil_combine_out_proj(os_, lses, gate, w_out, x):
    b, s, d = x.shape
    tm = PROJ_ROWS
    expand = jnp.asarray((np.arange(LANES)[:, None] == np.arange(MIX_WIDTH)[None, :] // HEAD_DIM)
                         .astype(np.float32), BF16)
    row = lambda bi, i: (bi, i, 0)
    in_specs = ([pl.BlockSpec((None, tm, MIX_WIDTH), row)] * len(os_)
                + [pl.BlockSpec((None, tm, LANES), row)] * len(lses))
    args = list(os_) + list(lses)
    in_specs += [pl.BlockSpec((None, tm, MIX_WIDTH), row),
                 pl.BlockSpec((LANES, MIX_WIDTH), lambda bi, i: (0, 0)),
                 pl.BlockSpec((MIX_WIDTH, d), lambda bi, i: (0, 0)),
                 pl.BlockSpec((None, tm, d), row)]
    args += [gate, expand, w_out.astype(BF16), x]
    return pl.pallas_call(
        functools.partial(_dil_combine_kernel, len(os_)),
        grid=(b, s // tm),
        in_specs=in_specs,
        out_specs=pl.BlockSpec((None, tm, d), row),
        out_shape=jax.ShapeDtypeStruct((b, s, d), F32),
        compiler_params=_cparams(("arbitrary", "arbitrary")),
        name="dil_combine_out_proj",
    )(*args)


N_BIAS_PIECES = 3


def _fox_decay_kernel(f_ref, b_ref, kb_ref):
    s = f_ref.shape[0]
    x = f_ref[...] + b_ref[...]
    logf = jnp.minimum(x, 0.0) - jnp.log1p(jnp.exp(-jnp.abs(x)))
    r = lax.broadcasted_iota(jnp.int32, (BLK, BLK), 0)
    c = lax.broadcasted_iota(jnp.int32, (BLK, BLK), 1)
    tri = (c <= r).astype(F32)
    lane = lax.broadcasted_iota(jnp.int32, (BLK, LANES), 1)
    carry = jnp.zeros((1, LANES), F32)
    for i in range(s // BLK):
        blk = logf[i * BLK:(i + 1) * BLK, :]
        cs = jnp.dot(tri, blk, preferred_element_type=F32, precision=lax.Precision.HIGHEST) + carry
        carry = cs[BLK - 1:BLK, :]
        rest = cs * (-LOG2E)
        out = jnp.zeros((BLK, LANES), F32)
        for j in range(N_BIAS_PIECES):
            piece = rest.astype(BF16).astype(F32)
            rest = rest - piece
            moved = piece if j == 0 else pltpu.roll(piece, N_HEADS * j, 1)
            out = jnp.where((lane >= N_HEADS * j) & (lane < N_HEADS * (j + 1)), moved, out)
        kb_ref[i * BLK:(i + 1) * BLK, :] = out.astype(BF16)


def _fox_decay(f_logit, b_f):
    b, s, _ = f_logit.shape
    b_pad = jnp.pad(b_f.astype(F32), (0, LANES - N_HEADS))[None, :]
    return pl.pallas_call(
        _fox_decay_kernel,
        grid=(b,),
        in_specs=[pl.BlockSpec((None, s, LANES), lambda bi: (bi, 0, 0)),
                  pl.BlockSpec((1, LANES), lambda bi: (0, 0))],
        out_specs=pl.BlockSpec((None, s, LANES), lambda bi: (bi, 0, 0)),
        out_shape=jax.ShapeDtypeStruct((b, s, LANES), BF16),
        compiler_params=_cparams(("arbitrary",)),
        name="fox_decay",
    )(f_logit, b_pad)


FOX_T = PROJ_ROWS
FOX_HEADS_PER_STEP = 8


def _fox_kernel(qt_ref, k_ref, kb_ref, vt_ref, gate_ref, o_ref, m_sc, acc_sc, st_sc):
    pi = pl.program_id(1)
    i = pl.program_id(2)
    t = FOX_T
    heads = range(FOX_HEADS_PER_STEP)
    row = lax.broadcasted_iota(jnp.int32, (LANES, t), 0)
    zeros = jnp.zeros((HEAD_DIM, t), BF16)
    qx = []
    for e in heads:
        h = FOX_HEADS_PER_STEP * pi + e
        pick = (row % N_HEADS == h) & (row < N_HEADS * N_BIAS_PIECES)
        sel = jnp.where(pick, 1.0, 0.0).astype(BF16)
        mine = qt_ref[0, e * HEAD_DIM:(e + 1) * HEAD_DIM, :]
        top = [mine, zeros] if e % 2 == 0 else [zeros, mine]
        qx.append(jnp.concatenate(top + [sel], axis=0))
        m_sc[e] = jnp.full((1, t), MASK_VALUE, F32)
        acc_sc[e] = jnp.zeros((ACC_ROWS, t), F32)

    tk = t // 2

    def scores(jt, hf, e):
        k0 = pl.multiple_of(jt * t + hf * tk, tk)
        pair = slice((e // 2) * LANES, (e // 2 + 1) * LANES)
        kx = jnp.concatenate([k_ref[pl.ds(k0, tk), pair], kb_ref[pl.ds(k0, tk), :]], axis=1)
        st_sc[e, hf] = _dot(kx, qx[e])

    def consume(jt, hf, e, diag):
        st = st_sc[e, hf]
        if diag:
            ok = (lax.broadcasted_iota(jnp.int32, (tk, t), 0) + hf * tk
                  <= lax.broadcasted_iota(jnp.int32, (tk, t), 1))
            st = jnp.where(ok, st, MASK_VALUE)
        m_old = m_sc[e]
        m_new = jnp.maximum(m_old, jnp.max(st, axis=0, keepdims=True))
        pt = jnp.exp2(st - m_new).astype(BF16)
        alpha = jnp.exp2(m_old - m_new)
        vt = vt_ref[jt, e * HEAD_DIM:(e + 1) * HEAD_DIM, hf * tk:(hf + 1) * tk]
        acc_sc[e] = alpha * acc_sc[e] + _dot(_with_ones(vt), pt)
        m_sc[e] = m_new

    for e in heads:
        scores(0, 0, e)

    def body(jt):
        for e in heads:
            scores(jt, 1, e)
            consume(jt, 0, e, False)
        for e in heads:
            scores(jt + 1, 0, e)
            consume(jt, 1, e, False)

    _loop_in_pairs(i, body)
    for e in heads:
        scores(i, 1, e)
        consume(i, 0, e, True)
    for e in heads:
        consume(i, 1, e, True)
    for p in range(FOX_HEADS_PER_STEP // 2):
        outs = []
        for e in (2 * p, 2 * p + 1):
            acc = acc_sc[e]
            outs.append(acc[:HEAD_DIM] / jnp.maximum(acc[HEAD_DIM:HEAD_DIM + 1], 1e-30))
        sl = slice(p * LANES, (p + 1) * LANES)
        out = jnp.concatenate(outs, axis=0).T
        o_ref[:, sl] = (out * _silu(gate_ref[:, sl].astype(F32))).astype(o_ref.dtype)


def _fox_attention(qt, k, kb, vt, gate):
    b, s, _ = k.shape
    t = FOX_T
    assert s % t == 0 and vt.shape == (b, s // t, MIX_WIDTH, t) and qt.shape == vt.shape
    tile = lambda bi, p, i: (bi, i, p)
    nh = FOX_HEADS_PER_STEP
    wd = nh * HEAD_DIM
    return pl.pallas_call(
        _fox_kernel,
        grid=(b, N_HEADS // nh, s // t),
        in_specs=[pl.BlockSpec((None, 1, wd, t), lambda bi, p, i: (bi, i, p, 0)),
                  pl.BlockSpec((None, s, wd), lambda bi, p, i: (bi, 0, p)),
                  pl.BlockSpec((None, s, LANES), lambda bi, p, i: (bi, 0, 0)),
                  pl.BlockSpec((None, s // t, wd, t), lambda bi, p, i: (bi, 0, p, 0)),
                  pl.BlockSpec((None, t, wd), tile)],
        out_specs=pl.BlockSpec((None, t, wd), tile),
        out_shape=jax.ShapeDtypeStruct((b, s, MIX_WIDTH), BF16),
        scratch_shapes=[pltpu.VMEM((nh, 1, t), F32), pltpu.VMEM((nh, ACC_ROWS, t), F32),
                        pltpu.VMEM((nh, 2, t // 2, t), F32)],
        compiler_params=_cparams(("arbitrary", "arbitrary", "arbitrary")),
        name="fox_attn",
    )(qt, k, kb, vt, gate)


def _gelu_tanh(x):
    return 0.5 * x * (1.0 + jnp.tanh(math.sqrt(2.0 / math.pi) * (x + 0.044715 * (x * x * x))))


def _nsa_compress_kernel(ak_ref, av_ref, pe_ref, wa_ref, wb_ref, w2_ref, w2t_ref, c_ref, s_ref,
                         kc_ref, vct_ref):
    nrow = ak_ref.shape[0]
    for idx, a_ref in enumerate((ak_ref, av_ref)):
        a = a_ref[...].astype(F32)
        xa = (a + pe_ref[idx, 0:1, :]).astype(BF16)
        xb = (a + pe_ref[idx, 1:2, :]).astype(BF16)
        ya = _dot(xa, wa_ref[idx])
        yb = _dot(xb, wb_ref[idx])
        hid = _gelu_tanh(ya + pltpu.roll(yb, nrow - 1, 0)).astype(BF16)
        if idx == 0:
            y = _dot(hid, w2_ref[...])
            kc_ref[...] = _apply_rope(y, c_ref[...], s_ref[...]).astype(kc_ref.dtype)
        else:
            vct_ref[...] = _dot_nt(w2t_ref[...], hid).astype(vct_ref.dtype)


def _nsa_compress(kc, vc, pe_k, w1_k, w2_k, pe_v, w1_v, w2_v, rope_c, rope_s):
    b, s, _ = kc.shape
    ns = s // CMP_STRIDE
    g = NSA_KV
    flat = CMP_STRIDE * g * HEAD_DIM

    def w1_halves(w1):
        w1r = w1.reshape(2, CMP_STRIDE, HEAD_DIM, CMP_HIDDEN)
        outs = []
        for hf in range(2):
            z = jnp.einsum('ldc,gh->lgdhc', w1r[hf], jnp.eye(g, dtype=F32))
            outs.append(z.reshape(flat, g * CMP_HIDDEN))
        return outs

    def pe_halves(pe):
        per = pe.reshape(2, CMP_STRIDE, 1, HEAD_DIM)
        return jnp.broadcast_to(per, (2, CMP_STRIDE, g, HEAD_DIM)).reshape(2, flat)

    def w2_bd(w2):
        z = jnp.einsum('cd,gh->gchd', w2, jnp.eye(g, dtype=F32))
        return z.reshape(g * CMP_HIDDEN, g * HEAD_DIM)

    ka, kb = w1_halves(w1_k)
    va, vb = w1_halves(w1_v)
    wa = jnp.stack([ka, va]).astype(BF16)
    wb = jnp.stack([kb, vb]).astype(BF16)
    w2 = w2_bd(w2_k).astype(BF16)
    w2t = w2_bd(w2_v).T.astype(BF16)
    pe = jnp.stack([pe_halves(pe_k), pe_halves(pe_v)]).astype(F32)
    whole = lambda bi: (0, 0, 0)
    per_b = lambda bi: (bi, 0, 0)
    return pl.pallas_call(
        _nsa_compress_kernel,
        grid=(b,),
        in_specs=[pl.BlockSpec((None, ns, flat), per_b),
                  pl.BlockSpec((None, ns, flat), per_b),
                  pl.BlockSpec((2, 2, flat), whole),
                  pl.BlockSpec((2, flat, g * CMP_HIDDEN), whole),
                  pl.BlockSpec((2, flat, g * CMP_HIDDEN), whole),
                  pl.BlockSpec((g * CMP_HIDDEN, LANES), lambda bi: (0, 0)),
                  pl.BlockSpec((LANES, g * CMP_HIDDEN), lambda bi: (0, 0)),
                  pl.BlockSpec((None, ns, LANES), per_b),
                  pl.BlockSpec((None, ns, LANES), per_b)],
        out_specs=[pl.BlockSpec((None, ns, LANES), per_b),
                   pl.BlockSpec((None, LANES, ns), per_b)],
        out_shape=[jax.ShapeDtypeStruct((b, ns, LANES), BF16),
                   jax.ShapeDtypeStruct((b, LANES, ns), BF16)],
        compiler_params=_cparams(("arbitrary",)),
        name="nsa_compress",
    )(kc.reshape(b, ns, flat), vc.reshape(b, ns, flat), pe, wa, wb, w2, w2t,
      rope_c.reshape(b, ns, LANES), rope_s.reshape(b, ns, LANES))


NSA_QB = 2
NSA_TK = 256
NSA_REP = N_HEADS // NSA_KV


def _nsa_kernel(n_cmp, n_sel, n_win, *refs):
    n_wblk = n_win + NSA_QB - 1
    q_ref, kc_ref, vct_ref, ks_ref, vst_ref = refs[:5]
    kw_refs = refs[5:5 + n_wblk]
    vwt_refs = refs[5 + n_wblk:5 + 2 * n_wblk]
    (gl_ref, gate_ref, ovt_ref, blk_ref, wout_ref, x_ref, fg_ref,
     o_ref) = refs[5 + 2 * n_wblk:13 + 2 * n_wblk]
    st_sc, stc_sc, stw_sc, m_sc, acc_sc, ocmp_sc, owin_sc = refs[13 + 2 * n_wblk:]

    n = pl.program_id(1)
    tq, tk, rep = BLK, NSA_TK, NSA_REP
    streams = [(qb, g) for qb in range(NSA_QB) for g in range(NSA_KV)]
    ncp = kc_ref.shape[0]
    t_lane = [(n * NSA_QB + qb) * tq + lax.broadcasted_iota(jnp.int32, (1, tq), 1)
              for qb in range(NSA_QB)]
    zeros = jnp.zeros((HEAD_DIM, tq), BF16)
    qg = []
    for qb, g in streams:
        tiles = []
        for r in range(rep):
            h = g * rep + r
            mine = q_ref[qb, h * HEAD_DIM:(h + 1) * HEAD_DIM, :]
            tiles.append(jnp.concatenate([mine, zeros] if g % 2 == 0 else [zeros, mine], axis=0))
        qg.append(jnp.concatenate(tiles, axis=1))

    ci = lax.broadcasted_iota(jnp.int32, (ncp, tq), 0)
    kwn = n_win * BLK
    krow = lax.broadcasted_iota(jnp.int32, (kwn, tq), 0) - (n_win - 1) * BLK
    dist = lax.broadcasted_iota(jnp.int32, (kwn, tq), 1) - krow
    band = (dist >= 0) & (dist <= NSA_WINDOW - 1)
    sel_rows = ovt_ref.shape[0]
    rowi = lax.broadcasted_iota(jnp.int32, (sel_rows, tq), 0)
    rowf = rowi.astype(F32)

    def cmp_scores(sg):
        stc_sc[sg] = _dot(kc_ref[...], qg[sg])

    def cmp_consume(sg):
        qb, g = streams[sg]
        cmask = (ci * CMP_STRIDE + (CMP_LEN - 1) <= t_lane[qb]) & (ci < n_cmp)
        ps = []
        for r in range(rep):
            s_r = jnp.where(cmask, stc_sc[sg, :, r * tq:(r + 1) * tq], MASK_VALUE)
            m = jnp.max(s_r, axis=0, keepdims=True)
            pr = jnp.where(cmask, jnp.exp2(s_r - m), 0.0)
            ps.append(pr.astype(BF16))
        lhs = jnp.concatenate([_with_ones(vct_ref[g * HEAD_DIM:(g + 1) * HEAD_DIM, :]),
                               ovt_ref[...]], axis=0)
        res = _dot(lhs, jnp.concatenate(ps, axis=1))
        inv = 1.0 / jnp.maximum(res[HEAD_DIM:HEAD_DIM + 1], 1e-30)
        ocmp_sc[sg] = res[:HEAD_DIM] * inv
        imp = None
        for r in range(rep):
            ql = slice(r * tq, (r + 1) * tq)
            t = res[ACC_ROWS:, ql] * inv[:, ql]
            imp = t if imp is None else imp + t
        return imp

    def select(sg, imp):
        cur = t_lane[streams[sg][0]] // SEL_LEN
        forced = (rowi == 0) | (rowi == cur)
        causal = rowi <= cur
        score = jnp.where(causal, jnp.where(forced, FORCED_SCORE, imp), MASK_VALUE)
        score = jnp.where(rowi < n_sel, score, PAD_SCORE)
        chosen = jnp.zeros((sel_rows, tq), F32)
        for _ in range(min(SEL_TOPK, n_sel)):
            mx = jnp.max(score, axis=0, keepdims=True)
            first = jnp.min(jnp.where(score == mx, rowf, float(LANES)), axis=0, keepdims=True)
            hit = rowf == first
            chosen = jnp.where(hit, 1.0, chosen)
            score = jnp.where(hit, PAD_SCORE, score)
        bias_t = jnp.where(causal & (chosen > 0.5), 0.0, MASK_VALUE)
        bias_t = jnp.concatenate([bias_t, jnp.zeros((LANES - sel_rows, tq), F32)], axis=0)
        bias_t = bias_t.astype(BF16)
        return jnp.concatenate([qg[sg], jnp.concatenate([bias_t] * rep, axis=1)], axis=0)

    def win_scores(sg):
        qb = streams[sg][0]
        kwcat = jnp.concatenate([kr[...] for kr in kw_refs[qb:qb + n_win]], axis=0)
        stw_sc[sg] = _dot(kwcat, qg[sg])

    def win_consume(sg):
        qb, g = streams[sg]
        wok = band & ((n * NSA_QB + qb) * BLK + krow >= 0)
        vwt = jnp.concatenate([vr[0] for vr in vwt_refs[qb:qb + n_win]], axis=1)
        pts = []
        for r in range(rep):
            s_r = jnp.where(wok, stw_sc[sg, :, r * tq:(r + 1) * tq], MASK_VALUE)
            m = jnp.max(s_r, axis=0, keepdims=True)
            pts.append(jnp.exp2(s_r - m).astype(BF16))
        owin_sc[sg] = _dot(_with_ones(vwt[g * HEAD_DIM:(g + 1) * HEAD_DIM, :]),
                           jnp.concatenate(pts, axis=1))

    ns = len(streams)
    cmp_scores(0)
    cmp_scores(1)
    imps = []
    for sg in range(ns):
        if sg + 2 < ns:
            cmp_scores(sg + 2)
        else:
            win_scores(sg + 2 - ns)
        imps.append(cmp_consume(sg))
    qsel = []
    for sg in range(ns):
        if sg + 2 < ns:
            win_scores(sg + 2)
        qsel.append(select(sg, imps[sg]))
        win_consume(sg)

    for sg in range(ns):
        m_sc[sg] = jnp.full((1, rep * tq), MASK_VALUE, F32)
        acc_sc[sg] = jnp.zeros((ACC_ROWS, rep * tq), F32)

    hrep = rep // 2
    wide = hrep * tq
    units = [(sg, hf) for sg in range(ns) for hf in range(2)]
    nu = len(units)

    def scores(c, u):
        sg, hf = units[u]
        k0 = pl.multiple_of(c * tk, tk)
        kx = jnp.concatenate([ks_ref[pl.ds(k0, tk), :], blk_ref[pl.ds(k0, tk), :]], axis=1)
        st_sc[u] = _dot(kx, qsel[sg][:, hf * wide:(hf + 1) * wide])

    def consume(c, u, last):
        sg, hf = units[u]
        qb, g = streams[sg]
        lanes = slice(hf * wide, (hf + 1) * wide)
        if last:
            tok_ok = c * tk + lax.broadcasted_iota(jnp.int32, (tk, tq), 0) <= t_lane[qb]
        m_old = m_sc[sg, :, lanes]
        ms, pts = [], []
        for r in range(hrep):
            s_r = st_sc[u, :, r * tq:(r + 1) * tq]
            if last:
                s_r = jnp.where(tok_ok, s_r, MASK_VALUE)
            m_new = jnp.maximum(m_old[:, r * tq:(r + 1) * tq], jnp.max(s_r, axis=0, keepdims=True))
            pts.append(jnp.exp2(s_r - m_new).astype(BF16))
            ms.append(m_new)
        m_new = jnp.concatenate(ms, axis=1)
        alpha = jnp.exp2(m_old - m_new)
        vt = vst_ref[c, g * HEAD_DIM:(g + 1) * HEAD_DIM, :]
        acc_sc[sg, :, lanes] = (alpha * acc_sc[sg, :, lanes]
                                + _dot(_with_ones(vt), jnp.concatenate(pts, axis=1)))
        m_sc[sg, :, lanes] = m_new

    n_steps = ((n * NSA_QB + 1) * tq + tk - 1) // tk
    scores(0, 0)
    scores(0, 1)

    def body(c):
        for u in range(nu):
            if u + 2 < nu:
                scores(c, u + 2)
            else:
                scores(c + 1, u + 2 - nu)
            consume(c, u, False)

    _loop_in_pairs(n_steps - 1, body)
    for u in range(nu):
        if u + 2 < nu:
            scores(n_steps - 1, u + 2)
        consume(n_steps - 1, u, True)

    for qb in range(NSA_QB):
        tok = slice(qb * tq, (qb + 1) * tq)
        gate_t = (1.0 / (1.0 + jnp.exp(-gl_ref[tok, :]))).T
        pairs = []
        for p in range(N_PAIRS):
            sl = slice(p * LANES, (p + 1) * LANES)
            g = (2 * p) // rep
            sg = qb * NSA_KV + g
            rows = []
            for e in range(2):
                h = 2 * p + e
                ql = slice((h - g * rep) * tq, (h - g * rep + 1) * tq)
                slc, win = acc_sc[sg][:, ql], owin_sc[sg][:, ql]
                branches = (ocmp_sc[sg][:, ql],
                            slc[:HEAD_DIM] / jnp.maximum(slc[HEAD_DIM:HEAD_DIM + 1], 1e-30),
                            win[:HEAD_DIM] / win[HEAD_DIM:HEAD_DIM + 1])
                out = None
                for j, br in enumerate(branches):
                    t = gate_t[3 * h + j:3 * h + j + 1, :] * br
                    out = t if out is None else out + t
                rows.append(out)
            pair = jnp.concatenate(rows, axis=0).T
            pairs.append((pair * _silu(gate_ref[tok, sl].astype(F32))).astype(BF16))
        y = x_ref[tok, :] + _dot(jnp.concatenate(pairs, axis=1), wout_ref[...])
        var = jnp.mean(y * y, axis=-1, keepdims=True)
        o_ref[tok, :] = y * lax.rsqrt(var + NORM_EPS) * fg_ref[...]


def _selection_overlap_t(n_cmp_pad, n_cmp, n_sel):
    cs = np.arange(n_cmp_pad) * CMP_STRIDE
    js = np.arange(LANES) * SEL_LEN
    ov = np.minimum(cs[None, :] + CMP_LEN, js[:, None] + SEL_LEN) - np.maximum(cs[None, :], js[:, None])
    ov = (np.clip(ov, 0, None) / CMP_LEN).astype(np.float32)
    ov[:, n_cmp:] = 0.0
    ov[n_sel:, :] = 0.0
    return ov


def _nsa_attention(q_t, k_cmp, v_cmp_t, ks, vs_t, kw, vw_t, g_logit, gate, w_out, x, final_gain):
    b, s, d = x.shape
    tq = NSA_QB * BLK
    ncp = k_cmp.shape[1]
    n_cmp = s // CMP_STRIDE - 1
    n_sel = s // SEL_LEN
    n_win = -(-(NSA_WINDOW - 1) // BLK) + 1
    assert n_sel <= LANES and s % NSA_TK == 0 and tq == NSA_TK
    sel_rows = -(-n_sel // 16) * 16
    ov_t = jnp.asarray(_selection_overlap_t(ncp, n_cmp, n_sel)[:sel_rows], BF16)
    row = lambda bi, i: (bi, i, 0)
    per_b = lambda bi, i: (bi, 0, 0)
    in_specs = [pl.BlockSpec((None, NSA_QB, MIX_WIDTH, BLK), lambda bi, i: (bi, i, 0, 0)),
                pl.BlockSpec((None, ncp, LANES), per_b),
                pl.BlockSpec((None, LANES, ncp), per_b),
                pl.BlockSpec((None, s, LANES), per_b),
                pl.BlockSpec((None, s // NSA_TK, LANES, NSA_TK), lambda bi, i: (bi, 0, 0, 0))]
    args = [q_t, k_cmp, v_cmp_t, ks, vs_t]
    for j in range(n_win - 1, -NSA_QB, -1):
        in_specs.append(pl.BlockSpec(
            (None, BLK, LANES), lambda bi, i, j=j: (bi, jnp.maximum(NSA_QB * i - j, 0), 0)))
        args.append(kw)
    for j in range(n_win - 1, -NSA_QB, -1):
        in_specs.append(pl.BlockSpec(
            (None, 1, LANES, BLK), lambda bi, i, j=j: (bi, jnp.maximum(NSA_QB * i - j, 0), 0, 0)))
        args.append(vw_t)
    block_onehot = jnp.asarray(
        (np.arange(s)[:, None] // SEL_LEN == np.arange(LANES)[None, :]).astype(np.float32), BF16)
    in_specs += [pl.BlockSpec((None, tq, LANES), row),
                 pl.BlockSpec((None, tq, MIX_WIDTH), row),
                 pl.BlockSpec((sel_rows, ncp), lambda bi, i: (0, 0)),
                 pl.BlockSpec((s, LANES), lambda bi, i: (0, 0)),
                 pl.BlockSpec((MIX_WIDTH, d), lambda bi, i: (0, 0), pipeline_mode=pl.Buffered(1)),
                 pl.BlockSpec((None, tq, d), row),
                 pl.BlockSpec((1, d), lambda bi, i: (0, 0))]
    args += [g_logit, gate, ov_t, block_onehot, w_out.astype(BF16), x, final_gain[None, :]]
    wide = NSA_REP * BLK
    ns = NSA_QB * NSA_KV
    return pl.pallas_call(
        functools.partial(_nsa_kernel, n_cmp, n_sel, n_win),
        grid=(b, s // tq),
        in_specs=in_specs,
        out_specs=pl.BlockSpec((None, tq, d), row),
        out_shape=jax.ShapeDtypeStruct((b, s, d), F32),
        scratch_shapes=[pltpu.VMEM((2 * ns, NSA_TK, wide // 2), F32),
                        pltpu.VMEM((ns, ncp, wide), F32),
                        pltpu.VMEM((ns, n_win * BLK, wide), F32),
                        pltpu.VMEM((ns, 1, wide), F32),
                        pltpu.VMEM((ns, ACC_ROWS, wide), F32),
                        pltpu.VMEM((ns, HEAD_DIM, wide), F32),
                        pltpu.VMEM((ns, ACC_ROWS, wide), F32)],
        compiler_params=_cparams(("arbitrary", "arbitrary")),
        name="nsa_attn",
    )(*args)


def _split(w, sizes):
    offs = np.cumsum([0] + list(sizes))
    return [w[:, int(offs[i]):int(offs[i + 1])] for i in range(len(sizes))]


def _swa_layer(x, rope, gain, w_in, sinks, w_out):
    kvw = SWA_KV * HEAD_DIM
    parts = _split(w_in, [MIX_WIDTH, kvw, kvw, MIX_WIDTH])
    segs = [_Seg(MIX_WIDTH, rope=True, scale=Q_SCALE), _Seg(kvw, rope=True), _Seg(kvw),
            _Seg(MIX_WIDTH)]
    q, k, v, gate = _norm_proj(x, gain, parts, segs, rope)
    o = _banded_attention(q, k, v, SWA_WINDOW - 1, sinks=sinks, gate=gate)
    return o, w_out


def _dilated_layer(x, rope, gain, w_in, w_out, residual=None):
    b, s, _ = x.shape
    kvw = DIL_KV * HEAD_DIM
    sizes, segs = [], []
    for window, dil in DIL_PATTERNS:
        assert s % (dil * BLK) == 0
        sizes += [MIX_WIDTH, kvw, kvw]
        segs += [_Seg(MIX_WIDTH, rope=True, scale=Q_SCALE), _Seg(kvw, rope=True), _Seg(kvw)]
    sizes.append(MIX_WIDTH)
    segs.append(_Seg(MIX_WIDTH))
    res = _norm_proj(x, gain, _split(w_in, sizes), segs, rope, residual)
    if residual is not None:
        x, res = res[0], res[1:]
    gate = res[-1]
    os_, lses = [], []
    for gi, (window, dil) in enumerate(DIL_PATTERNS):
        q, k, v = res[3 * gi:3 * gi + 3]
        o, lse = _banded_attention(q, k, v, window // dil, dil=dil, want_lse=True)
        os_.append(o)
        lses.append(lse)
    return _dil_combine_out_proj(os_, lses, gate, w_out, x)


def _fox_layer(x, gain, w_in, b_f, w_out):
    sizes = [MIX_WIDTH, MIX_WIDTH, MIX_WIDTH, N_HEADS, MIX_WIDTH]
    segs = [_Seg(MIX_WIDTH, scale=Q_SCALE, tile=FOX_T), _Seg(MIX_WIDTH),
            _Seg(MIX_WIDTH, tile=FOX_T), _Seg(LANES, dtype=F32), _Seg(MIX_WIDTH)]
    qt, k, vt, f_logit, gate = _norm_proj(x, gain, _split(w_in, sizes), segs)
    kb = _fox_decay(f_logit, b_f)
    return _fox_attention(qt, k, kb, vt, gate), w_out


def _nsa_layer(x, positions, rope, gain, w_in, pe_k, w1_k, w2_k, pe_v, w1_v, w2_v, w_out,
               residual, final_gain):
    b, s, _ = x.shape
    kvw = NSA_KV * HEAD_DIM
    sizes = [MIX_WIDTH] + [kvw] * 6 + [3 * N_HEADS, MIX_WIDTH]
    segs = [_Seg(MIX_WIDTH, rope=True, scale=Q_SCALE, tile=BLK), _Seg(kvw), _Seg(kvw),
            _Seg(kvw, rope=True), _Seg(kvw, tile=NSA_TK), _Seg(kvw, rope=True), _Seg(kvw, tile=BLK),
            _Seg(LANES, dtype=F32), _Seg(MIX_WIDTH)]
    x, q, kc, vc, ks, vs_t, kw, vw_t, g_logit, gate = _norm_proj(
        x, gain, _split(w_in, sizes), segs, rope, residual)
    ns = s // CMP_STRIDE
    cmp_pos = jnp.concatenate(
        [positions[:, CMP_LEN - 1::CMP_STRIDE], positions[:, -1:]], axis=1)[:, :ns]
    cmp_c, cmp_s = _rope_tables(cmp_pos.reshape(-1))
    k_cmp, v_cmp_t = _nsa_compress(kc, vc, pe_k, w1_k, w2_k, pe_v, w1_v, w2_v, cmp_c, cmp_s)
    return _nsa_attention(q, k_cmp, v_cmp_t, ks, vs_t, kw, vw_t, g_logit, gate, w_out, x,
                          final_gain)


def kernel(x, positions, norm_0, w_in_0, sinks_0, w_out_0, norm_1, w_in_1, w_out_1, norm_2, w_in_2, b_f_2, w_out_2, norm_3, w_in_3, cmp_pe_k_3, cmp_w1_k_3, cmp_w2_k_3, cmp_pe_v_3, cmp_w1_v_3, cmp_w2_v_3, w_out_3, final_norm):
    rope = _rope_tables(positions.reshape(-1))
    o, w = _swa_layer(x, rope, norm_0, w_in_0, sinks_0, w_out_0)
    x = _dilated_layer(x, rope, norm_1, w_in_1, w_out_1, residual=(o, w))
    o, w = _fox_layer(x, norm_2, w_in_2, b_f_2, w_out_2)
    return _nsa_layer(x, positions, rope, norm_3, w_in_3, cmp_pe_k_3, cmp_w1_k_3, cmp_w2_k_3,
                      cmp_pe_v_3, cmp_w1_v_3, cmp_w2_v_3, w_out_3, residual=(o, w),
                      final_gain=final_norm)
```

```python
import functools
import math

import numpy as np
import jax
import jax.numpy as jnp
from jax import lax
from jax.experimental import pallas as pl
from jax.experimental.pallas import tpu as pltpu

HEAD_DIM = 64
N_HEADS = 16
N_PAIRS = N_HEADS // 2
MIX_WIDTH = N_HEADS * HEAD_DIM
ROT_DIM = HEAD_DIM // 4
ROT_HALF = ROT_DIM // 2
ROPE_THETA = 500000.0
BLK = 128
LANES = 128
NORM_EPS = 1e-6
MASK_VALUE = -1e30
PAD_SCORE = -3e38
LOG2E = math.log2(math.e)
Q_SCALE = HEAD_DIM ** -0.5 * LOG2E

SWA_KV = 4
SWA_WINDOW = 128
DIL_KV = 4
DIL_PATTERNS = ((128, 1), (512, 4), (2048, 16))
NSA_KV = 2
CMP_LEN = 32
CMP_STRIDE = 16
CMP_HIDDEN = 256
SEL_LEN = 64
SEL_TOPK = 8
NSA_WINDOW = 256
FORCED_SCORE = 1e4

VMEM_LIMIT_BYTES = 56 * 1024 * 1024
PROJ_ROWS = 512
PROJ_COLS = 512
BF16_SUBLANES = 16
ACC_ROWS = HEAD_DIM + BF16_SUBLANES
ROPE_ROWS = 2048
F32 = jnp.float32
BF16 = jnp.bfloat16


def _cparams(sem):
    return pltpu.CompilerParams(dimension_semantics=sem, vmem_limit_bytes=VMEM_LIMIT_BYTES)


def _lane_half(shape):
    return lax.broadcasted_iota(jnp.int32, shape, 1) // HEAD_DIM


def _swap_halves(t):
    return jnp.concatenate([t[:, HEAD_DIM:], t[:, :HEAD_DIM]], axis=1)


def _head_query(qp, e, kv_half):
    qh = jnp.where(_lane_half(qp.shape) == e, qp, jnp.zeros_like(qp))
    if e != kv_half:
        qh = _swap_halves(qh)
    return qh


def _group_queries(q_ref, g, rep):
    tiles = []
    for r in range(rep):
        h = g * rep + r
        tiles.append(_head_query(q_ref[:, (h // 2) * LANES:(h // 2 + 1) * LANES], h % 2, g % 2))
    return jnp.concatenate(tiles, axis=0)


def _dot_nt(a, b):
    return lax.dot_general(a, b, (((1,), (1,)), ((), ())), preferred_element_type=F32)


def _dot(a, b):
    return jnp.dot(a, b, preferred_element_type=F32)


def _silu(x):
    return x * (1.0 / (1.0 + jnp.exp(-x)))


def _loop_in_pairs(count, body):
    odd = count % 2

    @pl.when(odd == 1)
    def _():
        body(0)

    def two(jj, carry):
        body(odd + 2 * jj)
        body(odd + 2 * jj + 1)
        return carry

    lax.fori_loop(0, count // 2, two, 0)


def _with_ones(vt):
    return jnp.concatenate([vt, jnp.ones((ACC_ROWS - HEAD_DIM, vt.shape[1]), BF16)], axis=0)


def _rope_table_kernel(pos_ref, inv_ref, c_ref, s_ref):
    pos = pos_ref[...].astype(F32)
    ang = pos * inv_ref[...]
    d = lax.broadcasted_iota(jnp.int32, ang.shape, 1) % HEAD_DIM
    cos = jnp.cos(ang)
    sin = jnp.sin(ang)
    c_ref[...] = jnp.where(d < ROT_DIM, cos, 1.0)
    s_ref[...] = jnp.where(d < ROT_HALF, -sin, jnp.where(d < ROT_DIM, sin, 0.0))


def _rope_tables(pos_flat):
    t = pos_flat.shape[0]
    rows = min(t, ROPE_ROWS)
    assert t % rows == 0
    inv = jnp.power(ROPE_THETA, -jnp.arange(ROT_HALF, dtype=F32) / ROT_HALF)
    inv_l = jnp.tile(inv, LANES // ROT_HALF)[None, :]
    out = jax.ShapeDtypeStruct((t, LANES), F32)
    return pl.pallas_call(
        _rope_table_kernel,
        grid=(t // rows,),
        in_specs=[pl.BlockSpec((rows, 1), lambda i: (i, 0)),
                  pl.BlockSpec((1, LANES), lambda i: (0, 0))],
        out_specs=[pl.BlockSpec((rows, LANES), lambda i: (i, 0))] * 2,
        out_shape=[out, out],
        compiler_params=_cparams(("arbitrary",)),
        name="rope_tables",
    )(pos_flat[:, None], inv_l)


def _apply_rope(y, c, s):
    outs = []
    for j in range(y.shape[1] // LANES):
        t = y[:, j * LANES:(j + 1) * LANES]
        d = lax.broadcasted_iota(jnp.int32, t.shape, 1) % HEAD_DIM
        partner = jnp.where(d < ROT_HALF, pltpu.roll(t, LANES - ROT_HALF, 1),
                            pltpu.roll(t, ROT_HALF, 1))
        outs.append(t * c + partner * s)
    return outs[0] if len(outs) == 1 else jnp.concatenate(outs, axis=1)


class _Seg:
    def __init__(self, width, rope=False, scale=None, dtype=BF16, dil=1, tile=None):
        self.width, self.rope, self.scale, self.dtype, self.dil = width, rope, scale, dtype, dil
        self.tile = tile
        assert not (tile and dil > 1)


def _norm_proj_kernel(segs, use_rope, has_residual, *refs):
    has_t = any(sg.tile and not sg.rope for sg in segs)
    x_ref, g_ref, w_ref = refs[:3]
    k = 3
    if has_t:
        wt_ref = refs[k]
        k += 1
    if use_rope:
        c_ref, s_ref = refs[k:k + 2]
        k += 2
    if has_residual:
        po_ref, pw_ref = refs[k:k + 2]
        k += 2
        xnew_ref = refs[k]
        k += 1
    out_refs = refs[k:k + len(segs)]
    stage_ref = refs[k + len(segs)] if any(sg.dil > 1 for sg in segs) else None

    x = x_ref[...]
    if has_residual:
        x = x + _dot(po_ref[...], pw_ref[...])
        xnew_ref[...] = x
    var = jnp.mean(x * x, axis=-1, keepdims=True)
    h = (x * lax.rsqrt(var + NORM_EPS) * g_ref[...]).astype(BF16)
    rows = x.shape[0]
    col = 0
    tcol = 0
    for sg, o_ref in zip(segs, out_refs):
        if sg.tile and not sg.rope:
            for c0 in range(0, sg.width, PROJ_COLS):
                cw = min(PROJ_COLS, sg.width - c0)
                yt = _dot_nt(wt_ref[tcol + c0:tcol + c0 + cw, :], h)
                if sg.scale is not None:
                    yt = yt * sg.scale
                for ti in range(rows // sg.tile):
                    o_ref[ti, c0:c0 + cw, :] = yt[:, ti * sg.tile:(ti + 1) * sg.tile].astype(sg.dtype)
            tcol += sg.width
            continue
        for c0 in range(0, sg.width, PROJ_COLS):
            cw = min(PROJ_COLS, sg.width - c0)
            y = _dot(h, w_ref[:, col + c0:col + c0 + cw])
            if sg.rope:
                y = _apply_rope(y, c_ref[...], s_ref[...])
            if sg.scale is not None:
                y = y * sg.scale
            if sg.tile:
                yt = y.T
                for ti in range(rows // sg.tile):
                    o_ref[ti, c0:c0 + cw, :] = yt[:, ti * sg.tile:(ti + 1) * sg.tile].astype(sg.dtype)
            elif sg.dil > 1:
                sub = rows // sg.dil
                for j in range(cw // LANES):
                    stage_ref[j] = y[:, j * LANES:(j + 1) * LANES]
                for r in range(sg.dil):
                    for j in range(cw // LANES):
                        lo = c0 + j * LANES
                        o_ref[r, :, lo:lo + LANES] = (
                            stage_ref[j, pl.ds(r, sub, stride=sg.dil), :].astype(sg.dtype))
            else:
                o_ref[:, c0:c0 + cw] = y.astype(sg.dtype)
        col += sg.width


def _norm_proj(x, gain, w_parts, segs, rope=None, residual=None):
    b, s, d = x.shape
    tm = PROJ_ROWS
    assert s % tm == 0
    w_cols, wt_rows = [], []
    for wp, sg in zip(w_parts, segs):
        if wp.shape[1] < sg.width:
            wp = jnp.pad(wp, ((0, 0), (0, sg.width - wp.shape[1])))
        if sg.tile and not sg.rope:
            wt_rows.append(wp.T)
        else:
            w_cols.append(wp)
    w = jnp.concatenate(w_cols, axis=1).astype(BF16)
    n = w.shape[1]
    use_rope = rope is not None
    once = dict(pipeline_mode=pl.Buffered(1))
    in_specs = [pl.BlockSpec((None, tm, d), lambda bi, i: (bi, i, 0)),
                pl.BlockSpec((1, d), lambda bi, i: (0, 0)),
                pl.BlockSpec((d, n), lambda bi, i: (0, 0), **once)]
    args = [x, gain[None, :], w]
    if wt_rows:
        wt = jnp.concatenate(wt_rows, axis=0).astype(BF16)
        in_specs.append(pl.BlockSpec(wt.shape, lambda bi, i: (0, 0), **once))
        args.append(wt)
    if use_rope:
        in_specs += [pl.BlockSpec((None, tm, LANES), lambda bi, i: (bi, i, 0))] * 2
        args += [rope[0].reshape(b, s, LANES), rope[1].reshape(b, s, LANES)]
    out_specs, out_shape = [], []
    if residual is not None:
        po, pw = residual
        in_specs += [pl.BlockSpec((None, tm, po.shape[-1]), lambda bi, i: (bi, i, 0)),
                     pl.BlockSpec(pw.shape, lambda bi, i: (0, 0), **once)]
        args += [po, pw.astype(BF16)]
        out_shape.append(jax.ShapeDtypeStruct((b, s, d), F32))
        out_specs.append(pl.BlockSpec((None, tm, d), lambda bi, i: (bi, i, 0)))
    for sg in segs:
        if sg.dil > 1:
            assert tm % sg.dil == 0
            out_shape.append(jax.ShapeDtypeStruct((b, sg.dil, s // sg.dil, sg.width), sg.dtype))
            out_specs.append(pl.BlockSpec((None, sg.dil, tm // sg.dil, sg.width),
                                          lambda bi, i: (bi, 0, i, 0)))
        elif sg.tile:
            assert tm % sg.tile == 0
            out_shape.append(jax.ShapeDtypeStruct((b, s // sg.tile, sg.width, sg.tile), sg.dtype))
            out_specs.append(pl.BlockSpec((None, tm // sg.tile, sg.width, sg.tile),
                                          lambda bi, i: (bi, i, 0, 0)))
        else:
            out_shape.append(jax.ShapeDtypeStruct((b, s, sg.width), sg.dtype))
            out_specs.append(pl.BlockSpec((None, tm, sg.width), lambda bi, i: (bi, i, 0)))
    scratch = ([pltpu.VMEM((PROJ_COLS // LANES, tm, LANES), F32)]
               if any(sg.dil > 1 for sg in segs) else [])
    return pl.pallas_call(
        functools.partial(_norm_proj_kernel, segs, use_rope, residual is not None),
        grid=(b, s // tm),
        in_specs=in_specs,
        out_specs=out_specs,
        out_shape=out_shape,
        scratch_shapes=scratch,
        compiler_params=_cparams(("arbitrary", "arbitrary")),
        name="norm_proj",
    )(*args)


BANDED_MAX_QB = 4


def _banded_kernel(n_qb, max_dist, n_prev, n_kv, has_sink, has_gate, want_lse, *refs):
    q_ref = refs[0]
    kp_refs, kc_ref = refs[1:1 + n_prev], refs[1 + n_prev]
    vp_refs, vc_ref = refs[2 + n_prev:2 + 2 * n_prev], refs[2 + 2 * n_prev]
    k_refs = list(kp_refs) + [kc_ref.at[a * BLK:(a + 1) * BLK] for a in range(n_qb)]
    v_refs = list(vp_refs) + [vc_ref.at[a * BLK:(a + 1) * BLK] for a in range(n_qb)]
    k = 3 + 2 * n_prev
    sink_ref = gate_ref = lse_ref = None
    if has_sink:
        sink_ref = refs[k]; k += 1
    if has_gate:
        gate_ref = refs[k]; k += 1
    o_ref = refs[k]; k += 1
    if want_lse:
        lse_ref = refs[k]; k += 1
    ot_sc = refs[k]
    st_sc = refs[k + 1]

    n = pl.program_id(1)
    tq = BLK
    kw = (n_prev + 1) * BLK
    rep = N_HEADS // n_kv
    krow = lax.broadcasted_iota(jnp.int32, (kw, tq), 0) - n_prev * BLK
    dist = lax.broadcasted_iota(jnp.int32, (kw, tq), 1) - krow
    band = (dist >= 0) & (dist <= max_dist)
    lse_rows = [[] for _ in range(n_qb)]
    units = [(qb, g) for qb in range(n_qb) for g in range(n_kv)]

    def scores(u):
        qb, g = units[u]
        sl = slice((g // 2) * LANES, (g // 2 + 1) * LANES)
        kcat = jnp.concatenate([kr[:, sl] for kr in k_refs[qb:qb + n_prev + 1]], axis=0)
        st_sc[u] = _dot_nt(kcat, _group_queries(q_ref.at[qb * tq:(qb + 1) * tq], g, rep))

    oks, vts = {}, {}

    def consume(u):
        qb, g = units[u]
        if qb not in oks:
            oks[qb] = band & ((n * n_qb + qb) * BLK + krow >= 0)
        ok = oks[qb]
        if (qb, g // 2) not in vts:
            sl = slice((g // 2) * LANES, (g // 2 + 1) * LANES)
            vcat = jnp.concatenate([vr[:, sl] for vr in v_refs[qb:qb + n_prev + 1]], axis=0)
            vts[qb, g // 2] = vcat.astype(F32).T.astype(BF16)
        vt = vts[qb, g // 2]
        ms, pts = [], []
        for r in range(rep):
            s_r = jnp.where(ok, st_sc[u, :, r * tq:(r + 1) * tq], MASK_VALUE)
            m = jnp.max(s_r, axis=0, keepdims=True)
            if has_sink:
                h = g * rep + r
                m = jnp.maximum(m, sink_ref[0:1, h:h + 1])
            pts.append(jnp.exp2(s_r - m).astype(BF16))
            ms.append(m)
        acc = _dot(_with_ones(vt[(g % 2) * HEAD_DIM:(g % 2 + 1) * HEAD_DIM, :]),
                   jnp.concatenate(pts, axis=1))
        for r in range(rep):
            h = g * rep + r
            den = acc[HEAD_DIM:HEAD_DIM + 1, r * tq:(r + 1) * tq]
            if has_sink:
                den = den + jnp.exp2(sink_ref[0:1, h:h + 1] - ms[r])
            ot_sc[qb * N_HEADS + h] = acc[:HEAD_DIM, r * tq:(r + 1) * tq] / den
            if want_lse:
                lse_rows[qb].append(ms[r] + jnp.log2(den))

    scores(0)
    scores(1)
    for u in range(len(units)):
        if u + 2 < len(units):
            scores(u + 2)
        consume(u)
    for qb in range(n_qb):
        rows = slice(qb * tq, (qb + 1) * tq)
        for p in range(N_PAIRS):
            sl = slice(p * LANES, (p + 1) * LANES)
            pair = jnp.concatenate([ot_sc[qb * N_HEADS + 2 * p],
                                    ot_sc[qb * N_HEADS + 2 * p + 1]], axis=0).T
            if has_gate:
                pair = pair * _silu(gate_ref[rows, sl].astype(F32))
            o_ref[rows, sl] = pair.astype(o_ref.dtype)
        if want_lse:
            lse_t = jnp.concatenate(
                lse_rows[qb] + [jnp.zeros((LANES - N_HEADS, tq), F32)], axis=0)
            lse_ref[rows, :] = lse_t.T


def _banded_attention(q, k, v, max_dist, sinks=None, gate=None, want_lse=False):
    bq, sq, _ = q.shape
    kvw = k.shape[-1]
    n_kv = kvw // HEAD_DIM
    n_prev = -(-max_dist // BLK)
    n_qb = max(d for d in range(1, BANDED_MAX_QB + 1) if (sq // BLK) % d == 0)
    tq = n_qb * BLK
    assert sq % tq == 0
    row = lambda b, i: (b, i, 0)
    in_specs = [pl.BlockSpec((None, tq, MIX_WIDTH), row)]
    args = [q]
    for arr in (k, v):
        for j in range(n_prev, 0, -1):
            in_specs.append(pl.BlockSpec(
                (None, BLK, kvw), lambda b, i, j=j: (b, jnp.maximum(n_qb * i - j, 0), 0)))
            args.append(arr)
        in_specs.append(pl.BlockSpec((None, tq, kvw), row))
        args.append(arr)
    if sinks is not None:
        in_specs.append(pl.BlockSpec((1, LANES), lambda b, i: (0, 0)))
        args.append(jnp.pad(sinks.astype(F32) * LOG2E, (0, LANES - N_HEADS))[None, :])
    if gate is not None:
        in_specs.append(pl.BlockSpec((None, tq, MIX_WIDTH), row))
        args.append(gate)
    out_specs = [pl.BlockSpec((None, tq, MIX_WIDTH), row)]
    out_shape = [jax.ShapeDtypeStruct((bq, sq, MIX_WIDTH), BF16)]
    if want_lse:
        out_specs.append(pl.BlockSpec((None, tq, LANES), row))
        out_shape.append(jax.ShapeDtypeStruct((bq, sq, LANES), F32))
    res = pl.pallas_call(
        functools.partial(_banded_kernel, n_qb, max_dist, n_prev, n_kv, sinks is not None,
                          gate is not None, want_lse),
        grid=(bq, sq // tq),
        in_specs=in_specs,
        out_specs=out_specs,
        out_shape=out_shape,
        scratch_shapes=[pltpu.VMEM((n_qb * N_HEADS, HEAD_DIM, BLK), F32),
                        pltpu.VMEM((n_qb * n_kv, (n_prev + 1) * BLK,
                                    (N_HEADS // n_kv) * BLK), F32)],
        compiler_params=_cparams(("arbitrary", "arbitrary")),
        name="banded_attn",
    )(*args)
    return res if want_lse else res[0]


def _dil_combine_kernel(dils, *refs):
    ng = len(dils)
    o_refs, l_refs = refs[:ng], refs[ng:2 * ng]
    gate_ref, expand_ref, w_ref, x_ref, out_ref, stage_ref, lstage_ref = refs[2 * ng:]
    rows = out_ref.shape[0]

    def natural(ref, dil, stage, slab):
        sl = slice(slab * LANES, (slab + 1) * LANES)
        if dil == 1:
            return ref[:, sl].astype(F32)
        sub = rows // dil
        for r in range(dil):
            stage[pl.ds(r, sub, stride=dil), :] = ref[r, :, sl].astype(F32)
        return stage[...]

    lses = [natural(l_refs[i], dils[i], lstage_ref, 0) for i in range(ng)]
    mx = functools.reduce(jnp.maximum, lses)
    ws = [jnp.exp2(l - mx) for l in lses]
    tot = functools.reduce(lambda a, c: a + c, ws)
    inv = 1.0 / tot

    def widen(w):
        hi = w.astype(BF16)
        lo = (w - hi.astype(F32)).astype(BF16)
        return _dot(hi, expand_ref[...]) + _dot(lo, expand_ref[...])

    wide = [widen(w * inv) for w in ws[:-1]]
    wide.append(1.0 - functools.reduce(lambda a, c: a + c, wide))
    gated = []
    for p in range(N_PAIRS):
        sl = slice(p * LANES, (p + 1) * LANES)
        acc = None
        for gi in range(ng):
            t = wide[gi][:, sl] * natural(o_refs[gi], dils[gi], stage_ref, p)
            acc = t if acc is None else acc + t
        gated.append((acc * _silu(gate_ref[:, sl].astype(F32))).astype(BF16))
    out_ref[...] = x_ref[...] + _dot(jnp.concatenate(gated, axis=1), w_ref[...])


def _dil_combine_out_proj(os_, lses, gate, dils, w_out, x):
    b, s, d = x.shape
    tm = PROJ_ROWS
    expand = jnp.asarray((np.arange(LANES)[:, None] == np.arange(MIX_WIDTH)[None, :] // HEAD_DIM)
                         .astype(np.float32), BF16)
    in_specs, args = [], []
    for arrs, width in ((os_, MIX_WIDTH), (lses, LANES)):
        for arr, dil in zip(arrs, dils):
            if dil == 1:
                in_specs.append(pl.BlockSpec((None, tm, width), lambda bi, i: (bi, i, 0)))
            else:
                in_specs.append(pl.BlockSpec((None, dil, tm // dil, width),
                                             lambda bi, i: (bi, 0, i, 0)))
            args.append(arr)
    row = lambda bi, i: (bi, i, 0)
    in_specs += [pl.BlockSpec((None, tm, MIX_WIDTH), row),
                 pl.BlockSpec((LANES, MIX_WIDTH), lambda bi, i: (0, 0)),
                 pl.BlockSpec((MIX_WIDTH, d), lambda bi, i: (0, 0)),
                 pl.BlockSpec((None, tm, d), row)]
    args += [gate, expand, w_out.astype(BF16), x]
    return pl.pallas_call(
        functools.partial(_dil_combine_kernel, dils),
        grid=(b, s // tm),
        in_specs=in_specs,
        out_specs=pl.BlockSpec((None, tm, d), row),
        out_shape=jax.ShapeDtypeStruct((b, s, d), F32),
        scratch_shapes=[pltpu.VMEM((tm, LANES), F32), pltpu.VMEM((tm, LANES), F32)],
        compiler_params=_cparams(("arbitrary", "arbitrary")),
        name="dil_combine_out_proj",
    )(*args)


N_BIAS_PIECES = 3


def _fox_decay_kernel(f_ref, b_ref, kb_ref):
    s = f_ref.shape[0]
    x = f_ref[...] + b_ref[...]
    logf = jnp.minimum(x, 0.0) - jnp.log1p(jnp.exp(-jnp.abs(x)))
    r = lax.broadcasted_iota(jnp.int32, (BLK, BLK), 0)
    c = lax.broadcasted_iota(jnp.int32, (BLK, BLK), 1)
    tri = (c <= r).astype(F32)
    lane = lax.broadcasted_iota(jnp.int32, (BLK, LANES), 1)
    carry = jnp.zeros((1, LANES), F32)
    for i in range(s // BLK):
        blk = logf[i * BLK:(i + 1) * BLK, :]
        cs = jnp.dot(tri, blk, preferred_element_type=F32, precision=lax.Precision.HIGHEST) + carry
        carry = cs[BLK - 1:BLK, :]
        rest = cs * (-LOG2E)
        out = jnp.zeros((BLK, LANES), F32)
        for j in range(N_BIAS_PIECES):
            piece = rest.astype(BF16).astype(F32)
            rest = rest - piece
            moved = piece if j == 0 else pltpu.roll(piece, N_HEADS * j, 1)
            out = jnp.where((lane >= N_HEADS * j) & (lane < N_HEADS * (j + 1)), moved, out)
        kb_ref[i * BLK:(i + 1) * BLK, :] = out.astype(BF16)


def _fox_decay(f_logit, b_f):
    b, s, _ = f_logit.shape
    b_pad = jnp.pad(b_f.astype(F32), (0, LANES - N_HEADS))[None, :]
    return pl.pallas_call(
        _fox_decay_kernel,
        grid=(b,),
        in_specs=[pl.BlockSpec((None, s, LANES), lambda bi: (bi, 0, 0)),
                  pl.BlockSpec((1, LANES), lambda bi: (0, 0))],
        out_specs=pl.BlockSpec((None, s, LANES), lambda bi: (bi, 0, 0)),
        out_shape=jax.ShapeDtypeStruct((b, s, LANES), BF16),
        compiler_params=_cparams(("arbitrary",)),
        name="fox_decay",
    )(f_logit, b_pad)


FOX_T = PROJ_ROWS
FOX_HEADS_PER_STEP = 8


def _fox_kernel(qt_ref, k_ref, kb_ref, vt_ref, gate_ref, o_ref, m_sc, acc_sc, st_sc):
    pi = pl.program_id(1)
    i = pl.program_id(2)
    t = FOX_T
    heads = range(FOX_HEADS_PER_STEP)
    row = lax.broadcasted_iota(jnp.int32, (LANES, t), 0)
    zeros = jnp.zeros((HEAD_DIM, t), BF16)
    qx = []
    for e in heads:
        h = FOX_HEADS_PER_STEP * pi + e
        pick = (row % N_HEADS == h) & (row < N_HEADS * N_BIAS_PIECES)
        sel = jnp.where(pick, 1.0, 0.0).astype(BF16)
        mine = qt_ref[0, e * HEAD_DIM:(e + 1) * HEAD_DIM, :]
        top = [mine, zeros] if e % 2 == 0 else [zeros, mine]
        qx.append(jnp.concatenate(top + [sel], axis=0))
        m_sc[e] = jnp.full((1, t), MASK_VALUE, F32)
        acc_sc[e] = jnp.zeros((ACC_ROWS, t), F32)

    tk = t // 2

    def scores(jt, hf, e, diag=False):
        k0 = pl.multiple_of(jt * t + hf * tk, tk)
        pair = slice((e // 2) * LANES, (e // 2 + 1) * LANES)
        kx = jnp.concatenate([k_ref[pl.ds(k0, tk), pair], kb_ref[pl.ds(k0, tk), :]], axis=1)
        q0 = hf * tk if diag else 0
        st_sc[e, hf, :, q0:] = _dot(kx, qx[e][:, q0:])

    def consume(jt, hf, e, diag):
        q0 = hf * tk if diag else 0
        nq = t - q0
        st = st_sc[e, hf, :, q0:]
        if diag:
            ok = (lax.broadcasted_iota(jnp.int32, (tk, nq), 0)
                  <= lax.broadcasted_iota(jnp.int32, (tk, nq), 1) + (q0 - hf * tk))
            st = jnp.where(ok, st, MASK_VALUE)
        m_old = m_sc[e, :, q0:]
        m_new = jnp.maximum(m_old, jnp.max(st, axis=0, keepdims=True))
        pt = jnp.exp2(st - m_new).astype(BF16)
        alpha = jnp.exp2(m_old - m_new)
        vt = vt_ref[jt, e * HEAD_DIM:(e + 1) * HEAD_DIM, hf * tk:(hf + 1) * tk]
        acc_sc[e, :, q0:] = alpha * acc_sc[e, :, q0:] + _dot(_with_ones(vt), pt)
        m_sc[e, :, q0:] = m_new

    for e in heads:
        scores(0, 0, e)

    def body(jt):
        for e in heads:
            scores(jt, 1, e)
            consume(jt, 0, e, False)
        for e in heads:
            scores(jt + 1, 0, e)
            consume(jt, 1, e, False)

    _loop_in_pairs(i, body)
    for e in heads:
        scores(i, 1, e, diag=True)
        consume(i, 0, e, True)
    for e in heads:
        consume(i, 1, e, True)
    for p in range(FOX_HEADS_PER_STEP // 2):
        outs = []
        for e in (2 * p, 2 * p + 1):
            acc = acc_sc[e]
            outs.append(acc[:HEAD_DIM] / jnp.maximum(acc[HEAD_DIM:HEAD_DIM + 1], 1e-30))
        sl = slice(p * LANES, (p + 1) * LANES)
        out = jnp.concatenate(outs, axis=0).T
        o_ref[:, sl] = (out * _silu(gate_ref[:, sl].astype(F32))).astype(o_ref.dtype)


def _fox_attention(qt, k, kb, vt, gate):
    b, s, _ = k.shape
    t = FOX_T
    assert s % t == 0 and vt.shape == (b, s // t, MIX_WIDTH, t) and qt.shape == vt.shape
    tile = lambda bi, p, i: (bi, i, p)
    nh = FOX_HEADS_PER_STEP
    wd = nh * HEAD_DIM
    return pl.pallas_call(
        _fox_kernel,
        grid=(b, N_HEADS // nh, s // t),
        in_specs=[pl.BlockSpec((None, 1, wd, t), lambda bi, p, i: (bi, i, p, 0)),
                  pl.BlockSpec((None, s, wd), lambda bi, p, i: (bi, 0, p)),
                  pl.BlockSpec((None, s, LANES), lambda bi, p, i: (bi, 0, 0)),
                  pl.BlockSpec((None, s // t, wd, t), lambda bi, p, i: (bi, 0, p, 0)),
                  pl.BlockSpec((None, t, wd), tile)],
        out_specs=pl.BlockSpec((None, t, wd), tile),
        out_shape=jax.ShapeDtypeStruct((b, s, MIX_WIDTH), BF16),
        scratch_shapes=[pltpu.VMEM((nh, 1, t), F32), pltpu.VMEM((nh, ACC_ROWS, t), F32),
                        pltpu.VMEM((nh, 2, t // 2, t), F32)],
        compiler_params=_cparams(("arbitrary", "arbitrary", "arbitrary")),
        name="fox_attn",
    )(qt, k, kb, vt, gate)


def _gelu_tanh(x):
    return 0.5 * x * (1.0 + jnp.tanh(math.sqrt(2.0 / math.pi) * (x + 0.044715 * (x * x * x))))


def _nsa_compress_kernel(ak_ref, av_ref, pe_ref, wa_ref, wb_ref, w2_ref, w2t_ref, c_ref, s_ref,
                         kc_ref, vct_ref):
    nrow = ak_ref.shape[0]
    for idx, a_ref in enumerate((ak_ref, av_ref)):
        a = a_ref[...].astype(F32)
        xa = (a + pe_ref[idx, 0:1, :]).astype(BF16)
        xb = (a + pe_ref[idx, 1:2, :]).astype(BF16)
        ya = _dot(xa, wa_ref[idx])
        yb = _dot(xb, wb_ref[idx])
        hid = _gelu_tanh(ya + pltpu.roll(yb, nrow - 1, 0)).astype(BF16)
        if idx == 0:
            y = _dot(hid, w2_ref[...])
            kc_ref[...] = _apply_rope(y, c_ref[...], s_ref[...]).astype(kc_ref.dtype)
        else:
            vct_ref[...] = _dot_nt(w2t_ref[...], hid).astype(vct_ref.dtype)


def _nsa_compress(kc, vc, pe_k, w1_k, w2_k, pe_v, w1_v, w2_v, rope_c, rope_s):
    b, s, _ = kc.shape
    ns = s // CMP_STRIDE
    g = NSA_KV
    flat = CMP_STRIDE * g * HEAD_DIM

    def w1_halves(w1):
        w1r = w1.reshape(2, CMP_STRIDE, HEAD_DIM, CMP_HIDDEN)
        outs = []
        for hf in range(2):
            z = jnp.einsum('ldc,gh->lgdhc', w1r[hf], jnp.eye(g, dtype=F32))
            outs.append(z.reshape(flat, g * CMP_HIDDEN))
        return outs

    def pe_halves(pe):
        per = pe.reshape(2, CMP_STRIDE, 1, HEAD_DIM)
        return jnp.broadcast_to(per, (2, CMP_STRIDE, g, HEAD_DIM)).reshape(2, flat)

    def w2_bd(w2):
        z = jnp.einsum('cd,gh->gchd', w2, jnp.eye(g, dtype=F32))
        return z.reshape(g * CMP_HIDDEN, g * HEAD_DIM)

    ka, kb = w1_halves(w1_k)
    va, vb = w1_halves(w1_v)
    wa = jnp.stack([ka, va]).astype(BF16)
    wb = jnp.stack([kb, vb]).astype(BF16)
    w2 = w2_bd(w2_k).astype(BF16)
    w2t = w2_bd(w2_v).T.astype(BF16)
    pe = jnp.stack([pe_halves(pe_k), pe_halves(pe_v)]).astype(F32)
    whole = lambda bi: (0, 0, 0)
    per_b = lambda bi: (bi, 0, 0)
    return pl.pallas_call(
        _nsa_compress_kernel,
        grid=(b,),
        in_specs=[pl.BlockSpec((None, ns, flat), per_b),
                  pl.BlockSpec((None, ns, flat), per_b),
                  pl.BlockSpec((2, 2, flat), whole),
                  pl.BlockSpec((2, flat, g * CMP_HIDDEN), whole),
                  pl.BlockSpec((2, flat, g * CMP_HIDDEN), whole),
                  pl.BlockSpec((g * CMP_HIDDEN, LANES), lambda bi: (0, 0)),
                  pl.BlockSpec((LANES, g * CMP_HIDDEN), lambda bi: (0, 0)),
                  pl.BlockSpec((None, ns, LANES), per_b),
                  pl.BlockSpec((None, ns, LANES), per_b)],
        out_specs=[pl.BlockSpec((None, ns, LANES), per_b),
                   pl.BlockSpec((None, LANES, ns), per_b)],
        out_shape=[jax.ShapeDtypeStruct((b, ns, LANES), BF16),
                   jax.ShapeDtypeStruct((b, LANES, ns), BF16)],
        compiler_params=_cparams(("arbitrary",)),
        name="nsa_compress",
    )(kc.reshape(b, ns, flat), vc.reshape(b, ns, flat), pe, wa, wb, w2, w2t,
      rope_c.reshape(b, ns, LANES), rope_s.reshape(b, ns, LANES))


NSA_QB = 2
NSA_TK = 256
NSA_REP = N_HEADS // NSA_KV


def _nsa_kernel(n_cmp, n_sel, n_win, *refs):
    n_wblk = n_win + NSA_QB - 1
    q_ref, kc_ref, vct_ref, ks_ref, vst_ref = refs[:5]
    kw_refs = refs[5:5 + n_wblk]
    vwt_refs = refs[5 + n_wblk:5 + 2 * n_wblk]
    (gl_ref, gate_ref, ovt_ref, blk_ref, wout_ref, x_ref, fg_ref,
     o_ref) = refs[5 + 2 * n_wblk:13 + 2 * n_wblk]
    st_sc, stc_sc, stw_sc, m_sc, acc_sc, ocmp_sc, owin_sc = refs[13 + 2 * n_wblk:]

    n = pl.program_id(1)
    tq, tk, rep = BLK, NSA_TK, NSA_REP
    streams = [(qb, g) for qb in range(NSA_QB) for g in range(NSA_KV)]
    ncp = kc_ref.shape[0]
    t_lane = [(n * NSA_QB + qb) * tq + lax.broadcasted_iota(jnp.int32, (1, tq), 1)
              for qb in range(NSA_QB)]
    zeros = jnp.zeros((HEAD_DIM, tq), BF16)
    qg = []
    for qb, g in streams:
        tiles = []
        for r in range(rep):
            h = g * rep + r
            mine = q_ref[qb, h * HEAD_DIM:(h + 1) * HEAD_DIM, :]
            tiles.append(jnp.concatenate([mine, zeros] if g % 2 == 0 else [zeros, mine], axis=0))
        qg.append(jnp.concatenate(tiles, axis=1))

    ci = lax.broadcasted_iota(jnp.int32, (ncp, tq), 0)
    kwn = n_win * BLK
    krow = lax.broadcasted_iota(jnp.int32, (kwn, tq), 0) - (n_win - 1) * BLK
    dist = lax.broadcasted_iota(jnp.int32, (kwn, tq), 1) - krow
    band = (dist >= 0) & (dist <= NSA_WINDOW - 1)
    sel_rows = ovt_ref.shape[0]
    rowi = lax.broadcasted_iota(jnp.int32, (sel_rows, tq), 0)
    rowf = rowi.astype(F32)

    def cmp_scores(sg):
        stc_sc[sg] = _dot(kc_ref[...], qg[sg])

    def cmp_consume(sg):
        qb, g = streams[sg]
        cmask = (ci * CMP_STRIDE + (CMP_LEN - 1) <= t_lane[qb]) & (ci < n_cmp)
        ps = []
        for r in range(rep):
            s_r = jnp.where(cmask, stc_sc[sg, :, r * tq:(r + 1) * tq], MASK_VALUE)
            m = jnp.max(s_r, axis=0, keepdims=True)
            pr = jnp.where(cmask, jnp.exp2(s_r - m), 0.0)
            ps.append(pr.astype(BF16))
        lhs = jnp.concatenate([_with_ones(vct_ref[g * HEAD_DIM:(g + 1) * HEAD_DIM, :]),
                               ovt_ref[...]], axis=0)
        res = _dot(lhs, jnp.concatenate(ps, axis=1))
        inv = 1.0 / jnp.maximum(res[HEAD_DIM:HEAD_DIM + 1], 1e-30)
        ocmp_sc[sg] = res[:HEAD_DIM] * inv
        imp = None
        for r in range(rep):
            ql = slice(r * tq, (r + 1) * tq)
            t = res[ACC_ROWS:, ql] * inv[:, ql]
            imp = t if imp is None else imp + t
        return imp

    def select(sg, imp):
        cur = t_lane[streams[sg][0]] // SEL_LEN
        forced = (rowi == 0) | (rowi == cur)
        causal = rowi <= cur
        score = jnp.where(causal, jnp.where(forced, FORCED_SCORE, imp), MASK_VALUE)
        score = jnp.where(rowi < n_sel, score, PAD_SCORE)
        chosen = jnp.zeros((sel_rows, tq), F32)
        for _ in range(min(SEL_TOPK, n_sel)):
            mx = jnp.max(score, axis=0, keepdims=True)
            first = jnp.min(jnp.where(score == mx, rowf, float(LANES)), axis=0, keepdims=True)
            hit = rowf == first
            chosen = jnp.where(hit, 1.0, chosen)
            score = jnp.where(hit, PAD_SCORE, score)
        bias_t = jnp.where(causal & (chosen > 0.5), 0.0, MASK_VALUE)
        bias_t = jnp.concatenate([bias_t, jnp.zeros((LANES - sel_rows, tq), F32)], axis=0)
        bias_t = bias_t.astype(BF16)
        return jnp.concatenate([qg[sg], jnp.concatenate([bias_t] * rep, axis=1)], axis=0)

    def win_scores(sg):
        qb = streams[sg][0]
        kwcat = jnp.concatenate([kr[...] for kr in kw_refs[qb:qb + n_win]], axis=0)
        stw_sc[sg] = _dot(kwcat, qg[sg])

    def win_consume(sg):
        qb, g = streams[sg]
        wok = band & ((n * NSA_QB + qb) * BLK + krow >= 0)
        vwt = jnp.concatenate([vr[0] for vr in vwt_refs[qb:qb + n_win]], axis=1)
        pts = []
        for r in range(rep):
            s_r = jnp.where(wok, stw_sc[sg, :, r * tq:(r + 1) * tq], MASK_VALUE)
            m = jnp.max(s_r, axis=0, keepdims=True)
            pts.append(jnp.exp2(s_r - m).astype(BF16))
        owin_sc[sg] = _dot(_with_ones(vwt[g * HEAD_DIM:(g + 1) * HEAD_DIM, :]),
                           jnp.concatenate(pts, axis=1))

    ns = len(streams)
    cmp_scores(0)
    cmp_scores(1)
    imps = []
    for sg in range(ns):
        if sg + 2 < ns:
            cmp_scores(sg + 2)
        else:
            win_scores(sg + 2 - ns)
        imps.append(cmp_consume(sg))
    qsel = []
    for sg in range(ns):
        if sg + 2 < ns:
            win_scores(sg + 2)
        qsel.append(select(sg, imps[sg]))
        win_consume(sg)

    for sg in range(ns):
        m_sc[sg] = jnp.full((1, rep * tq), MASK_VALUE, F32)
        acc_sc[sg] = jnp.zeros((ACC_ROWS, rep * tq), F32)

    hrep = rep // 2
    wide = hrep * tq
    units = [(sg, hf) for sg in range(ns) for hf in range(2)]
    nu = len(units)

    def scores(c, u):
        sg, hf = units[u]
        k0 = pl.multiple_of(c * tk, tk)
        kx = jnp.concatenate([ks_ref[pl.ds(k0, tk), :], blk_ref[pl.ds(k0, tk), :]], axis=1)
        st_sc[u] = _dot(kx, qsel[sg][:, hf * wide:(hf + 1) * wide])

    def consume(c, u, last):
        sg, hf = units[u]
        qb, g = streams[sg]
        lanes = slice(hf * wide, (hf + 1) * wide)
        if last:
            tok_ok = c * tk + lax.broadcasted_iota(jnp.int32, (tk, tq), 0) <= t_lane[qb]
        m_old = m_sc[sg, :, lanes]
        ms, pts = [], []
        for r in range(hrep):
            s_r = st_sc[u, :, r * tq:(r + 1) * tq]
            if last:
                s_r = jnp.where(tok_ok, s_r, MASK_VALUE)
            m_new = jnp.maximum(m_old[:, r * tq:(r + 1) * tq], jnp.max(s_r, axis=0, keepdims=True))
            pts.append(jnp.exp2(s_r - m_new).astype(BF16))
            ms.append(m_new)
        m_new = jnp.concatenate(ms, axis=1)
        alpha = jnp.exp2(m_old - m_new)
        vt = vst_ref[c, g * HEAD_DIM:(g + 1) * HEAD_DIM, :]
        acc_sc[sg, :, lanes] = (alpha * acc_sc[sg, :, lanes]
                                + _dot(_with_ones(vt), jnp.concatenate(pts, axis=1)))
        m_sc[sg, :, lanes] = m_new

    n_steps = ((n * NSA_QB + 1) * tq + tk - 1) // tk
    scores(0, 0)
    scores(0, 1)

    def body(c):
        for u in range(nu):
            if u + 2 < nu:
                scores(c, u + 2)
            else:
                scores(c + 1, u + 2 - nu)
            consume(c, u, False)

    _loop_in_pairs(n_steps - 1, body)
    for u in range(nu):
        if u + 2 < nu:
            scores(n_steps - 1, u + 2)
        consume(n_steps - 1, u, True)

    for qb in range(NSA_QB):
        tok = slice(qb * tq, (qb + 1) * tq)
        gate_t = (1.0 / (1.0 + jnp.exp(-gl_ref[tok, :]))).T
        pairs = []
        for p in range(N_PAIRS):
            sl = slice(p * LANES, (p + 1) * LANES)
            g = (2 * p) // rep
            sg = qb * NSA_KV + g
            rows = []
            for e in range(2):
                h = 2 * p + e
                ql = slice((h - g * rep) * tq, (h - g * rep + 1) * tq)
                slc, win = acc_sc[sg][:, ql], owin_sc[sg][:, ql]
                branches = (ocmp_sc[sg][:, ql],
                            slc[:HEAD_DIM] / jnp.maximum(slc[HEAD_DIM:HEAD_DIM + 1], 1e-30),
                            win[:HEAD_DIM] / win[HEAD_DIM:HEAD_DIM + 1])
                out = None
                for j, br in enumerate(branches):
                    t = gate_t[3 * h + j:3 * h + j + 1, :] * br
                    out = t if out is None else out + t
                rows.append(out)
            pair = jnp.concatenate(rows, axis=0).T
            pairs.append((pair * _silu(gate_ref[tok, sl].astype(F32))).astype(BF16))
        y = x_ref[tok, :] + _dot(jnp.concatenate(pairs, axis=1), wout_ref[...])
        var = jnp.mean(y * y, axis=-1, keepdims=True)
        o_ref[tok, :] = y * lax.rsqrt(var + NORM_EPS) * fg_ref[...]


def _selection_overlap_t(n_cmp_pad, n_cmp, n_sel):
    cs = np.arange(n_cmp_pad) * CMP_STRIDE
    js = np.arange(LANES) * SEL_LEN
    ov = np.minimum(cs[None, :] + CMP_LEN, js[:, None] + SEL_LEN) - np.maximum(cs[None, :], js[:, None])
    ov = (np.clip(ov, 0, None) / CMP_LEN).astype(np.float32)
    ov[:, n_cmp:] = 0.0
    ov[n_sel:, :] = 0.0
    return ov


def _nsa_attention(q_t, k_cmp, v_cmp_t, ks, vs_t, kw, vw_t, g_logit, gate, w_out, x, final_gain):
    b, s, d = x.shape
    tq = NSA_QB * BLK
    ncp = k_cmp.shape[1]
    n_cmp = s // CMP_STRIDE - 1
    n_sel = s // SEL_LEN
    n_win = -(-(NSA_WINDOW - 1) // BLK) + 1
    assert n_sel <= LANES and s % NSA_TK == 0 and tq == NSA_TK
    sel_rows = -(-n_sel // BF16_SUBLANES) * BF16_SUBLANES
    ov_t = jnp.asarray(_selection_overlap_t(ncp, n_cmp, n_sel)[:sel_rows], BF16)
    row = lambda bi, i: (bi, i, 0)
    per_b = lambda bi, i: (bi, 0, 0)
    in_specs = [pl.BlockSpec((None, NSA_QB, MIX_WIDTH, BLK), lambda bi, i: (bi, i, 0, 0)),
                pl.BlockSpec((None, ncp, LANES), per_b),
                pl.BlockSpec((None, LANES, ncp), per_b),
                pl.BlockSpec((None, s, LANES), per_b),
                pl.BlockSpec((None, s // NSA_TK, LANES, NSA_TK), lambda bi, i: (bi, 0, 0, 0))]
    args = [q_t, k_cmp, v_cmp_t, ks, vs_t]
    for j in range(n_win - 1, -NSA_QB, -1):
        in_specs.append(pl.BlockSpec(
            (None, BLK, LANES), lambda bi, i, j=j: (bi, jnp.maximum(NSA_QB * i - j, 0), 0)))
        args.append(kw)
    for j in range(n_win - 1, -NSA_QB, -1):
        in_specs.append(pl.BlockSpec(
            (None, 1, LANES, BLK), lambda bi, i, j=j: (bi, jnp.maximum(NSA_QB * i - j, 0), 0, 0)))
        args.append(vw_t)
    block_onehot = jnp.asarray(
        (np.arange(s)[:, None] // SEL_LEN == np.arange(LANES)[None, :]).astype(np.float32), BF16)
    in_specs += [pl.BlockSpec((None, tq, LANES), row),
                 pl.BlockSpec((None, tq, MIX_WIDTH), row),
                 pl.BlockSpec((sel_rows, ncp), lambda bi, i: (0, 0)),
                 pl.BlockSpec((s, LANES), lambda bi, i: (0, 0)),
                 pl.BlockSpec((MIX_WIDTH, d), lambda bi, i: (0, 0), pipeline_mode=pl.Buffered(1)),
                 pl.BlockSpec((None, tq, d), row),
                 pl.BlockSpec((1, d), lambda bi, i: (0, 0))]
    args += [g_logit, gate, ov_t, block_onehot, w_out.astype(BF16), x, final_gain[None, :]]
    wide = NSA_REP * BLK
    ns = NSA_QB * NSA_KV
    return pl.pallas_call(
        functools.partial(_nsa_kernel, n_cmp, n_sel, n_win),
        grid=(b, s // tq),
        in_specs=in_specs,
        out_specs=pl.BlockSpec((None, tq, d), row),
        out_shape=jax.ShapeDtypeStruct((b, s, d), F32),
        scratch_shapes=[pltpu.VMEM((2 * ns, NSA_TK, wide // 2), F32),
                        pltpu.VMEM((ns, ncp, wide), F32),
                        pltpu.VMEM((ns, n_win * BLK, wide), F32),
                        pltpu.VMEM((ns, 1, wide), F32),
                        pltpu.VMEM((ns, ACC_ROWS, wide), F32),
                        pltpu.VMEM((ns, HEAD_DIM, wide), F32),
                        pltpu.VMEM((ns, ACC_ROWS, wide), F32)],
        compiler_params=_cparams(("arbitrary", "arbitrary")),
        name="nsa_attn",
    )(*args)


def _split(w, sizes):
    offs = np.cumsum([0] + list(sizes))
    return [w[:, int(offs[i]):int(offs[i + 1])] for i in range(len(sizes))]


def _swa_layer(x, rope, gain, w_in, sinks, w_out):
    kvw = SWA_KV * HEAD_DIM
    parts = _split(w_in, [MIX_WIDTH, kvw, kvw, MIX_WIDTH])
    segs = [_Seg(MIX_WIDTH, rope=True, scale=Q_SCALE), _Seg(kvw, rope=True), _Seg(kvw),
            _Seg(MIX_WIDTH)]
    q, k, v, gate = _norm_proj(x, gain, parts, segs, rope)
    o = _banded_attention(q, k, v, SWA_WINDOW - 1, sinks=sinks, gate=gate)
    return o, w_out


def _dilated_layer(x, rope, gain, w_in, w_out, residual=None):
    b, s, _ = x.shape
    kvw = DIL_KV * HEAD_DIM
    sizes, segs = [], []
    for window, dil in DIL_PATTERNS:
        assert s % (dil * BLK) == 0
        sizes += [MIX_WIDTH, kvw, kvw]
        segs += [_Seg(MIX_WIDTH, rope=True, scale=Q_SCALE, dil=dil), _Seg(kvw, rope=True, dil=dil),
                 _Seg(kvw, dil=dil)]
    sizes.append(MIX_WIDTH)
    segs.append(_Seg(MIX_WIDTH))
    res = _norm_proj(x, gain, _split(w_in, sizes), segs, rope, residual)
    if residual is not None:
        x, res = res[0], res[1:]
    gate = res[-1]
    os_, lses, dils = [], [], []
    for gi, (window, dil) in enumerate(DIL_PATTERNS):
        q, k, v = res[3 * gi:3 * gi + 3]
        if dil > 1:
            q, k, v = (t.reshape(b * dil, s // dil, t.shape[-1]) for t in (q, k, v))
        o, lse = _banded_attention(q, k, v, window // dil, want_lse=True)
        if dil > 1:
            o = o.reshape(b, dil, s // dil, MIX_WIDTH)
            lse = lse.reshape(b, dil, s // dil, LANES)
        os_.append(o)
        lses.append(lse)
        dils.append(dil)
    return _dil_combine_out_proj(os_, lses, gate, tuple(dils), w_out, x)


def _fox_layer(x, gain, w_in, b_f, w_out):
    sizes = [MIX_WIDTH, MIX_WIDTH, MIX_WIDTH, N_HEADS, MIX_WIDTH]
    segs = [_Seg(MIX_WIDTH, scale=Q_SCALE, tile=FOX_T), _Seg(MIX_WIDTH),
            _Seg(MIX_WIDTH, tile=FOX_T), _Seg(LANES, dtype=F32), _Seg(MIX_WIDTH)]
    qt, k, vt, f_logit, gate = _norm_proj(x, gain, _split(w_in, sizes), segs)
    kb = _fox_decay(f_logit, b_f)
    return _fox_attention(qt, k, kb, vt, gate), w_out


def _nsa_layer(x, positions, rope, gain, w_in, pe_k, w1_k, w2_k, pe_v, w1_v, w2_v, w_out,
               residual, final_gain):
    b, s, _ = x.shape
    kvw = NSA_KV * HEAD_DIM
    sizes = [MIX_WIDTH] + [kvw] * 6 + [3 * N_HEADS, MIX_WIDTH]
    segs = [_Seg(MIX_WIDTH, rope=True, scale=Q_SCALE, tile=BLK), _Seg(kvw), _Seg(kvw),
            _Seg(kvw, rope=True), _Seg(kvw, tile=NSA_TK), _Seg(kvw, rope=True), _Seg(kvw, tile=BLK),
            _Seg(LANES, dtype=F32), _Seg(MIX_WIDTH)]
    x, q, kc, vc, ks, vs_t, kw, vw_t, g_logit, gate = _norm_proj(
        x, gain, _split(w_in, sizes), segs, rope, residual)
    ns = s // CMP_STRIDE
    cmp_pos = jnp.concatenate(
        [positions[:, CMP_LEN - 1::CMP_STRIDE], positions[:, -1:]], axis=1)[:, :ns]
    cmp_c, cmp_s = _rope_tables(cmp_pos.reshape(-1))
    k_cmp, v_cmp_t = _nsa_compress(kc, vc, pe_k, w1_k, w2_k, pe_v, w1_v, w2_v, cmp_c, cmp_s)
    return _nsa_attention(q, k_cmp, v_cmp_t, ks, vs_t, kw, vw_t, g_logit, gate, w_out, x,
                          final_gain)


def kernel(x, positions, norm_0, w_in_0, sinks_0, w_out_0, norm_1, w_in_1, w_out_1, norm_2, w_in_2, b_f_2, w_out_2, norm_3, w_in_3, cmp_pe_k_3, cmp_w1_k_3, cmp_w2_k_3, cmp_pe_v_3, cmp_w1_v_3, cmp_w2_v_3, w_out_3, final_norm):
    rope = _rope_tables(positions.reshape(-1))
    o, w = _swa_layer(x, rope, norm_0, w_in_0, sinks_0, w_out_0)
    x = _dilated_layer(x, rope, norm_1, w_in_1, w_out_1, residual=(o, w))
    o, w = _fox_layer(x, norm_2, w_in_2, b_f_2, w_out_2)
    return _nsa_layer(x, positions, rope, norm_3, w_in_3, cmp_pe_k_3, cmp_w1_k_3, cmp_w2_k_3,
                      cmp_pe_v_3, cmp_w1_v_3, cmp_w2_v_3, w_out_3, residual=(o, w),
                      final_gain=final_norm)
```

```python
import functools
import math

import numpy as np
import jax
import jax.numpy as jnp
from jax import lax
from jax.experimental import pallas as pl
from jax.experimental.pallas import tpu as pltpu

HEAD_DIM = 64
N_HEADS = 16
N_PAIRS = N_HEADS // 2
MIX_WIDTH = N_HEADS * HEAD_DIM
ROT_DIM = HEAD_DIM // 4
ROT_HALF = ROT_DIM // 2
ROPE_THETA = 500000.0
BLK = 128
LANES = 128
NORM_EPS = 1e-6
MASK_VALUE = -1e30
PAD_SCORE = -3e38
LOG2E = math.log2(math.e)
Q_SCALE = HEAD_DIM ** -0.5 * LOG2E

SWA_KV = 4
SWA_WINDOW = 128
DIL_KV = 4
DIL_PATTERNS = ((128, 1), (512, 4), (2048, 16))
NSA_KV = 2
CMP_LEN = 32
CMP_STRIDE = 16
CMP_HIDDEN = 256
SEL_LEN = 64
SEL_TOPK = 8
NSA_WINDOW = 256
FORCED_SCORE = 1e4

VMEM_LIMIT_BYTES = 56 * 1024 * 1024
PROJ_ROWS = 512
PROJ_COLS = 512
BF16_SUBLANES = 16
ACC_ROWS = HEAD_DIM + BF16_SUBLANES
ROPE_ROWS = 2048
F32 = jnp.float32
BF16 = jnp.bfloat16


def _cparams(sem):
    return pltpu.CompilerParams(dimension_semantics=sem, vmem_limit_bytes=VMEM_LIMIT_BYTES)


def _lane_half(shape):
    return lax.broadcasted_iota(jnp.int32, shape, 1) // HEAD_DIM


def _swap_halves(t):
    return jnp.concatenate([t[:, HEAD_DIM:], t[:, :HEAD_DIM]], axis=1)


def _head_query(qp, e, kv_half):
    qh = jnp.where(_lane_half(qp.shape) == e, qp, jnp.zeros_like(qp))
    if e != kv_half:
        qh = _swap_halves(qh)
    return qh


def _group_queries(q_ref, g, rep):
    tiles = []
    for r in range(rep):
        h = g * rep + r
        tiles.append(_head_query(q_ref[:, (h // 2) * LANES:(h // 2 + 1) * LANES], h % 2, g % 2))
    return jnp.concatenate(tiles, axis=0)


def _dot_nt(a, b):
    return lax.dot_general(a, b, (((1,), (1,)), ((), ())), preferred_element_type=F32)


def _dot(a, b):
    return jnp.dot(a, b, preferred_element_type=F32)


def _silu(x):
    return x * (1.0 / (1.0 + jnp.exp(-x)))


def _loop_in_pairs(count, body):
    odd = count % 2

    @pl.when(odd == 1)
    def _():
        body(0)

    def two(jj, carry):
        body(odd + 2 * jj)
        body(odd + 2 * jj + 1)
        return carry

    lax.fori_loop(0, count // 2, two, 0)


def _with_ones(vt):
    return jnp.concatenate([vt, jnp.ones((ACC_ROWS - HEAD_DIM, vt.shape[1]), BF16)], axis=0)


def _rope_table_kernel(pos_ref, inv_ref, c_ref, s_ref):
    pos = pos_ref[...].astype(F32)
    ang = pos * inv_ref[...]
    d = lax.broadcasted_iota(jnp.int32, ang.shape, 1) % HEAD_DIM
    cos = jnp.cos(ang)
    sin = jnp.sin(ang)
    c_ref[...] = jnp.where(d < ROT_DIM, cos, 1.0)
    s_ref[...] = jnp.where(d < ROT_HALF, -sin, jnp.where(d < ROT_DIM, sin, 0.0))


def _rope_tables(pos_flat):
    t = pos_flat.shape[0]
    rows = min(t, ROPE_ROWS)
    assert t % rows == 0
    inv = jnp.power(ROPE_THETA, -jnp.arange(ROT_HALF, dtype=F32) / ROT_HALF)
    inv_l = jnp.tile(inv, LANES // ROT_HALF)[None, :]
    out = jax.ShapeDtypeStruct((t, LANES), F32)
    return pl.pallas_call(
        _rope_table_kernel,
        grid=(t // rows,),
        in_specs=[pl.BlockSpec((rows, 1), lambda i: (i, 0)),
                  pl.BlockSpec((1, LANES), lambda i: (0, 0))],
        out_specs=[pl.BlockSpec((rows, LANES), lambda i: (i, 0))] * 2,
        out_shape=[out, out],
        compiler_params=_cparams(("arbitrary",)),
        name="rope_tables",
    )(pos_flat[:, None], inv_l)


def _apply_rope(y, c, s):
    outs = []
    for j in range(y.shape[1] // LANES):
        t = y[:, j * LANES:(j + 1) * LANES]
        d = lax.broadcasted_iota(jnp.int32, t.shape, 1) % HEAD_DIM
        partner = jnp.where(d < ROT_HALF, pltpu.roll(t, LANES - ROT_HALF, 1),
                            pltpu.roll(t, ROT_HALF, 1))
        outs.append(t * c + partner * s)
    return outs[0] if len(outs) == 1 else jnp.concatenate(outs, axis=1)


class _Seg:
    def __init__(self, width, rope=False, scale=None, dtype=BF16, dil=1, tile=None):
        self.width, self.rope, self.scale, self.dtype, self.dil = width, rope, scale, dtype, dil
        self.tile = tile
        assert not (tile and dil > 1)


def _norm_proj_kernel(segs, use_rope, has_residual, *refs):
    has_t = any(sg.tile and not sg.rope for sg in segs)
    x_ref, g_ref, w_ref = refs[:3]
    k = 3
    if has_t:
        wt_ref = refs[k]
        k += 1
    if use_rope:
        c_ref, s_ref = refs[k:k + 2]
        k += 2
    if has_residual:
        po_ref, pw_ref = refs[k:k + 2]
        k += 2
        xnew_ref = refs[k]
        k += 1
    out_refs = refs[k:k + len(segs)]
    stage_ref = refs[k + len(segs)] if any(sg.dil > 1 for sg in segs) else None

    x = x_ref[...]
    if has_residual:
        x = x + _dot(po_ref[...], pw_ref[...])
        xnew_ref[...] = x
    var = jnp.mean(x * x, axis=-1, keepdims=True)
    h = (x * lax.rsqrt(var + NORM_EPS) * g_ref[...]).astype(BF16)
    rows = x.shape[0]
    col = 0
    tcol = 0
    for sg, o_ref in zip(segs, out_refs):
        if sg.tile and not sg.rope:
            for c0 in range(0, sg.width, PROJ_COLS):
                cw = min(PROJ_COLS, sg.width - c0)
                yt = _dot_nt(wt_ref[tcol + c0:tcol + c0 + cw, :], h)
                if sg.scale is not None:
                    yt = yt * sg.scale
                for ti in range(rows // sg.tile):
                    o_ref[ti, c0:c0 + cw, :] = yt[:, ti * sg.tile:(ti + 1) * sg.tile].astype(sg.dtype)
            tcol += sg.width
            continue
        for c0 in range(0, sg.width, PROJ_COLS):
            cw = min(PROJ_COLS, sg.width - c0)
            y = _dot(h, w_ref[:, col + c0:col + c0 + cw])
            if sg.rope:
                y = _apply_rope(y, c_ref[...], s_ref[...])
            if sg.scale is not None:
                y = y * sg.scale
            if sg.tile:
                yt = y.T
                for ti in range(rows // sg.tile):
                    o_ref[ti, c0:c0 + cw, :] = yt[:, ti * sg.tile:(ti + 1) * sg.tile].astype(sg.dtype)
            elif sg.dil > 1:
                sub = rows // sg.dil
                for j in range(cw // LANES):
                    stage_ref[j] = y[:, j * LANES:(j + 1) * LANES]
                for r in range(sg.dil):
                    for j in range(cw // LANES):
                        lo = c0 + j * LANES
                        o_ref[r, :, lo:lo + LANES] = (
                            stage_ref[j, pl.ds(r, sub, stride=sg.dil), :].astype(sg.dtype))
            else:
                o_ref[:, c0:c0 + cw] = y.astype(sg.dtype)
        col += sg.width


def _norm_proj(x, gain, w_parts, segs, rope=None, residual=None):
    b, s, d = x.shape
    tm = PROJ_ROWS
    assert s % tm == 0
    w_cols, wt_rows = [], []
    for wp, sg in zip(w_parts, segs):
        if wp.shape[1] < sg.width:
            wp = jnp.pad(wp, ((0, 0), (0, sg.width - wp.shape[1])))
        if sg.tile and not sg.rope:
            wt_rows.append(wp.T)
        else:
            w_cols.append(wp)
    w = jnp.concatenate(w_cols, axis=1).astype(BF16)
    n = w.shape[1]
    use_rope = rope is not None
    once = dict(pipeline_mode=pl.Buffered(1))
    in_specs = [pl.BlockSpec((None, tm, d), lambda bi, i: (bi, i, 0)),
                pl.BlockSpec((1, d), lambda bi, i: (0, 0)),
                pl.BlockSpec((d, n), lambda bi, i: (0, 0), **once)]
    args = [x, gain[None, :], w]
    if wt_rows:
        wt = jnp.concatenate(wt_rows, axis=0).astype(BF16)
        in_specs.append(pl.BlockSpec(wt.shape, lambda bi, i: (0, 0), **once))
        args.append(wt)
    if use_rope:
        in_specs += [pl.BlockSpec((None, tm, LANES), lambda bi, i: (bi, i, 0))] * 2
        args += [rope[0].reshape(b, s, LANES), rope[1].reshape(b, s, LANES)]
    out_specs, out_shape = [], []
    if residual is not None:
        po, pw = residual
        in_specs += [pl.BlockSpec((None, tm, po.shape[-1]), lambda bi, i: (bi, i, 0)),
                     pl.BlockSpec(pw.shape, lambda bi, i: (0, 0), **once)]
        args += [po, pw.astype(BF16)]
        out_shape.append(jax.ShapeDtypeStruct((b, s, d), F32))
        out_specs.append(pl.BlockSpec((None, tm, d), lambda bi, i: (bi, i, 0)))
    for sg in segs:
        if sg.dil > 1:
            assert tm % sg.dil == 0
            out_shape.append(jax.ShapeDtypeStruct((b, sg.dil, s // sg.dil, sg.width), sg.dtype))
            out_specs.append(pl.BlockSpec((None, sg.dil, tm // sg.dil, sg.width),
                                          lambda bi, i: (bi, 0, i, 0)))
        elif sg.tile:
            assert tm % sg.tile == 0
            out_shape.append(jax.ShapeDtypeStruct((b, s // sg.tile, sg.width, sg.tile), sg.dtype))
            out_specs.append(pl.BlockSpec((None, tm // sg.tile, sg.width, sg.tile),
                                          lambda bi, i: (bi, i, 0, 0)))
        else:
            out_shape.append(jax.ShapeDtypeStruct((b, s, sg.width), sg.dtype))
            out_specs.append(pl.BlockSpec((None, tm, sg.width), lambda bi, i: (bi, i, 0)))
    scratch = ([pltpu.VMEM((PROJ_COLS // LANES, tm, LANES), F32)]
               if any(sg.dil > 1 for sg in segs) else [])
    return pl.pallas_call(
        functools.partial(_norm_proj_kernel, segs, use_rope, residual is not None),
        grid=(b, s // tm),
        in_specs=in_specs,
        out_specs=out_specs,
        out_shape=out_shape,
        scratch_shapes=scratch,
        compiler_params=_cparams(("arbitrary", "arbitrary")),
        name="norm_proj",
    )(*args)


BANDED_MAX_QB = 8


def _banded_kernel(n_qb, max_dist, n_prev, n_kv, has_sink, has_gate, want_lse, *refs):
    q_ref = refs[0]
    kp_refs, kc_ref = refs[1:1 + n_prev], refs[1 + n_prev]
    vp_refs, vc_ref = refs[2 + n_prev:2 + 2 * n_prev], refs[2 + 2 * n_prev]
    k_refs = list(kp_refs) + [kc_ref.at[a * BLK:(a + 1) * BLK] for a in range(n_qb)]
    v_refs = list(vp_refs) + [vc_ref.at[a * BLK:(a + 1) * BLK] for a in range(n_qb)]
    k = 3 + 2 * n_prev
    sink_ref = gate_ref = lse_ref = None
    if has_sink:
        sink_ref = refs[k]; k += 1
    if has_gate:
        gate_ref = refs[k]; k += 1
    o_ref = refs[k]; k += 1
    if want_lse:
        lse_ref = refs[k]; k += 1
    ot_sc = refs[k]
    st_sc = refs[k + 1]

    n = pl.program_id(1)
    tq = BLK
    kw = (n_prev + 1) * BLK
    rep = N_HEADS // n_kv
    krow = lax.broadcasted_iota(jnp.int32, (kw, tq), 0) - n_prev * BLK
    dist = lax.broadcasted_iota(jnp.int32, (kw, tq), 1) - krow
    band = (dist >= 0) & (dist <= max_dist)
    lse_rows = [[] for _ in range(n_qb)]
    units = [(qb, g) for qb in range(n_qb) for g in range(n_kv)]

    def scores(u):
        qb, g = units[u]
        sl = slice((g // 2) * LANES, (g // 2 + 1) * LANES)
        kcat = jnp.concatenate([kr[:, sl] for kr in k_refs[qb:qb + n_prev + 1]], axis=0)
        st_sc[u] = _dot_nt(kcat, _group_queries(q_ref.at[qb * tq:(qb + 1) * tq], g, rep))

    oks, vts = {}, {}

    def consume(u):
        qb, g = units[u]
        if qb not in oks:
            oks[qb] = band & ((n * n_qb + qb) * BLK + krow >= 0)
        ok = oks[qb]
        if (qb, g // 2) not in vts:
            sl = slice((g // 2) * LANES, (g // 2 + 1) * LANES)
            vcat = jnp.concatenate([vr[:, sl] for vr in v_refs[qb:qb + n_prev + 1]], axis=0)
            vts[qb, g // 2] = vcat.astype(F32).T.astype(BF16)
        vt = vts[qb, g // 2]
        ms, pts = [], []
        for r in range(rep):
            s_r = jnp.where(ok, st_sc[u, :, r * tq:(r + 1) * tq], MASK_VALUE)
            m = jnp.max(s_r, axis=0, keepdims=True)
            if has_sink:
                h = g * rep + r
                m = jnp.maximum(m, sink_ref[0:1, h:h + 1])
            pts.append(jnp.exp2(s_r - m).astype(BF16))
            ms.append(m)
        acc = _dot(_with_ones(vt[(g % 2) * HEAD_DIM:(g % 2 + 1) * HEAD_DIM, :]),
                   jnp.concatenate(pts, axis=1))
        for r in range(rep):
            h = g * rep + r
            den = acc[HEAD_DIM:HEAD_DIM + 1, r * tq:(r + 1) * tq]
            if has_sink:
                den = den + jnp.exp2(sink_ref[0:1, h:h + 1] - ms[r])
            ot_sc[qb * N_HEADS + h] = acc[:HEAD_DIM, r * tq:(r + 1) * tq] / den
            if want_lse:
                lse_rows[qb].append(ms[r] + jnp.log2(den))

    scores(0)
    scores(1)
    for u in range(len(units)):
        if u + 2 < len(units):
            scores(u + 2)
        consume(u)
    for qb in range(n_qb):
        rows = slice(qb * tq, (qb + 1) * tq)
        for p in range(N_PAIRS):
            sl = slice(p * LANES, (p + 1) * LANES)
            pair = jnp.concatenate([ot_sc[qb * N_HEADS + 2 * p],
                                    ot_sc[qb * N_HEADS + 2 * p + 1]], axis=0).T
            if has_gate:
                pair = pair * _silu(gate_ref[rows, sl].astype(F32))
            o_ref[rows, sl] = pair.astype(o_ref.dtype)
        if want_lse:
            lse_t = jnp.concatenate(
                lse_rows[qb] + [jnp.zeros((LANES - N_HEADS, tq), F32)], axis=0)
            lse_ref[rows, :] = lse_t.T


def _banded_attention(q, k, v, max_dist, sinks=None, gate=None, want_lse=False):
    bq, sq, _ = q.shape
    kvw = k.shape[-1]
    n_kv = kvw // HEAD_DIM
    n_prev = -(-max_dist // BLK)
    n_qb = max(d for d in range(1, BANDED_MAX_QB + 1) if (sq // BLK) % d == 0)
    tq = n_qb * BLK
    assert sq % tq == 0
    row = lambda b, i: (b, i, 0)
    in_specs = [pl.BlockSpec((None, tq, MIX_WIDTH), row)]
    args = [q]
    for arr in (k, v):
        for j in range(n_prev, 0, -1):
            in_specs.append(pl.BlockSpec(
                (None, BLK, kvw), lambda b, i, j=j: (b, jnp.maximum(n_qb * i - j, 0), 0)))
            args.append(arr)
        in_specs.append(pl.BlockSpec((None, tq, kvw), row))
        args.append(arr)
    if sinks is not None:
        in_specs.append(pl.BlockSpec((1, LANES), lambda b, i: (0, 0)))
        args.append(jnp.pad(sinks.astype(F32) * LOG2E, (0, LANES - N_HEADS))[None, :])
    if gate is not None:
        in_specs.append(pl.BlockSpec((None, tq, MIX_WIDTH), row))
        args.append(gate)
    out_specs = [pl.BlockSpec((None, tq, MIX_WIDTH), row)]
    out_shape = [jax.ShapeDtypeStruct((bq, sq, MIX_WIDTH), BF16)]
    if want_lse:
        out_specs.append(pl.BlockSpec((None, tq, LANES), row))
        out_shape.append(jax.ShapeDtypeStruct((bq, sq, LANES), F32))
    res = pl.pallas_call(
        functools.partial(_banded_kernel, n_qb, max_dist, n_prev, n_kv, sinks is not None,
                          gate is not None, want_lse),
        grid=(bq, sq // tq),
        in_specs=in_specs,
        out_specs=out_specs,
        out_shape=out_shape,
        scratch_shapes=[pltpu.VMEM((n_qb * N_HEADS, HEAD_DIM, BLK), F32),
                        pltpu.VMEM((n_qb * n_kv, (n_prev + 1) * BLK,
                                    (N_HEADS // n_kv) * BLK), F32)],
        compiler_params=_cparams(("arbitrary", "arbitrary")),
        name="banded_attn",
    )(*args)
    return res if want_lse else res[0]


def _dil_combine_kernel(dils, *refs):
    ng = len(dils)
    o_refs, l_refs = refs[:ng], refs[ng:2 * ng]
    gate_ref, expand_ref, w_ref, x_ref, out_ref, stage_ref, lstage_ref = refs[2 * ng:]
    rows = out_ref.shape[0]

    def natural(ref, dil, stage, slab):
        sl = slice(slab * LANES, (slab + 1) * LANES)
        if dil == 1:
            return ref[:, sl].astype(F32)
        sub = rows // dil
        for r in range(dil):
            stage[pl.ds(r, sub, stride=dil), :] = ref[r, :, sl].astype(F32)
        return stage[...]

    lses = [natural(l_refs[i], dils[i], lstage_ref, 0) for i in range(ng)]
    mx = functools.reduce(jnp.maximum, lses)
    ws = [jnp.exp2(l - mx) for l in lses]
    tot = functools.reduce(lambda a, c: a + c, ws)
    inv = 1.0 / tot

    def widen(w):
        hi = w.astype(BF16)
        lo = (w - hi.astype(F32)).astype(BF16)
        return _dot(hi, expand_ref[...]) + _dot(lo, expand_ref[...])

    wide = [widen(w * inv) for w in ws[:-1]]
    wide.append(1.0 - functools.reduce(lambda a, c: a + c, wide))
    gated = []
    for p in range(N_PAIRS):
        sl = slice(p * LANES, (p + 1) * LANES)
        acc = None
        for gi in range(ng):
            t = wide[gi][:, sl] * natural(o_refs[gi], dils[gi], stage_ref, p)
            acc = t if acc is None else acc + t
        gated.append((acc * _silu(gate_ref[:, sl].astype(F32))).astype(BF16))
    out_ref[...] = x_ref[...] + _dot(jnp.concatenate(gated, axis=1), w_ref[...])


def _dil_combine_out_proj(os_, lses, gate, dils, w_out, x):
    b, s, d = x.shape
    tm = PROJ_ROWS
    expand = jnp.asarray((np.arange(LANES)[:, None] == np.arange(MIX_WIDTH)[None, :] // HEAD_DIM)
                         .astype(np.float32), BF16)
    in_specs, args = [], []
    for arrs, width in ((os_, MIX_WIDTH), (lses, LANES)):
        for arr, dil in zip(arrs, dils):
            if dil == 1:
                in_specs.append(pl.BlockSpec((None, tm, width), lambda bi, i: (bi, i, 0)))
            else:
                in_specs.append(pl.BlockSpec((None, dil, tm // dil, width),
                                             lambda bi, i: (bi, 0, i, 0)))
            args.append(arr)
    row = lambda bi, i: (bi, i, 0)
    in_specs += [pl.BlockSpec((None, tm, MIX_WIDTH), row),
                 pl.BlockSpec((LANES, MIX_WIDTH), lambda bi, i: (0, 0)),
                 pl.BlockSpec((MIX_WIDTH, d), lambda bi, i: (0, 0)),
                 pl.BlockSpec((None, tm, d), row)]
    args += [gate, expand, w_out.astype(BF16), x]
    return pl.pallas_call(
        functools.partial(_dil_combine_kernel, dils),
        grid=(b, s // tm),
        in_specs=in_specs,
        out_specs=pl.BlockSpec((None, tm, d), row),
        out_shape=jax.ShapeDtypeStruct((b, s, d), F32),
        scratch_shapes=[pltpu.VMEM((tm, LANES), F32), pltpu.VMEM((tm, LANES), F32)],
        compiler_params=_cparams(("arbitrary", "arbitrary")),
        name="dil_combine_out_proj",
    )(*args)


N_BIAS_PIECES = 3


def _fox_decay_kernel(f_ref, b_ref, kb_ref):
    s = f_ref.shape[0]
    x = f_ref[...] + b_ref[...]
    logf = jnp.minimum(x, 0.0) - jnp.log1p(jnp.exp(-jnp.abs(x)))
    r = lax.broadcasted_iota(jnp.int32, (BLK, BLK), 0)
    c = lax.broadcasted_iota(jnp.int32, (BLK, BLK), 1)
    tri = (c <= r).astype(F32)
    lane = lax.broadcasted_iota(jnp.int32, (BLK, LANES), 1)
    carry = jnp.zeros((1, LANES), F32)
    for i in range(s // BLK):
        blk = logf[i * BLK:(i + 1) * BLK, :]
        cs = jnp.dot(tri, blk, preferred_element_type=F32, precision=lax.Precision.HIGHEST) + carry
        carry = cs[BLK - 1:BLK, :]
        rest = cs * (-LOG2E)
        out = jnp.zeros((BLK, LANES), F32)
        for j in range(N_BIAS_PIECES):
            piece = rest.astype(BF16).astype(F32)
            rest = rest - piece
            moved = piece if j == 0 else pltpu.roll(piece, N_HEADS * j, 1)
            out = jnp.where((lane >= N_HEADS * j) & (lane < N_HEADS * (j + 1)), moved, out)
        kb_ref[i * BLK:(i + 1) * BLK, :] = out.astype(BF16)


def _fox_decay(f_logit, b_f):
    b, s, _ = f_logit.shape
    b_pad = jnp.pad(b_f.astype(F32), (0, LANES - N_HEADS))[None, :]
    return pl.pallas_call(
        _fox_decay_kernel,
        grid=(b,),
        in_specs=[pl.BlockSpec((None, s, LANES), lambda bi: (bi, 0, 0)),
                  pl.BlockSpec((1, LANES), lambda bi: (0, 0))],
        out_specs=pl.BlockSpec((None, s, LANES), lambda bi: (bi, 0, 0)),
        out_shape=jax.ShapeDtypeStruct((b, s, LANES), BF16),
        compiler_params=_cparams(("arbitrary",)),
        name="fox_decay",
    )(f_logit, b_pad)


FOX_T = PROJ_ROWS
FOX_HEADS_PER_STEP = 8


def _fox_kernel(qt_ref, k_ref, kb_ref, vt_ref, gate_ref, o_ref, m_sc, acc_sc, st_sc):
    pi = pl.program_id(1)
    i = pl.program_id(2)
    t = FOX_T
    heads = range(FOX_HEADS_PER_STEP)
    row = lax.broadcasted_iota(jnp.int32, (LANES, t), 0)
    zeros = jnp.zeros((HEAD_DIM, t), BF16)
    qx = []
    for e in heads:
        h = FOX_HEADS_PER_STEP * pi + e
        pick = (row % N_HEADS == h) & (row < N_HEADS * N_BIAS_PIECES)
        sel = jnp.where(pick, 1.0, 0.0).astype(BF16)
        mine = qt_ref[0, e * HEAD_DIM:(e + 1) * HEAD_DIM, :]
        top = [mine, zeros] if e % 2 == 0 else [zeros, mine]
        qx.append(jnp.concatenate(top + [sel], axis=0))
        m_sc[e] = jnp.full((1, t), MASK_VALUE, F32)
        acc_sc[e] = jnp.zeros((ACC_ROWS, t), F32)

    tk = t // 2

    def scores(jt, hf, e, diag=False):
        k0 = pl.multiple_of(jt * t + hf * tk, tk)
        pair = slice((e // 2) * LANES, (e // 2 + 1) * LANES)
        kx = jnp.concatenate([k_ref[pl.ds(k0, tk), pair], kb_ref[pl.ds(k0, tk), :]], axis=1)
        q0 = hf * tk if diag else 0
        st_sc[e, hf, :, q0:] = _dot(kx, qx[e][:, q0:])

    def consume(jt, hf, e, diag):
        q0 = hf * tk if diag else 0
        nq = t - q0
        st = st_sc[e, hf, :, q0:]
        if diag:
            ok = (lax.broadcasted_iota(jnp.int32, (tk, nq), 0)
                  <= lax.broadcasted_iota(jnp.int32, (tk, nq), 1) + (q0 - hf * tk))
            st = jnp.where(ok, st, MASK_VALUE)
        m_old = m_sc[e, :, q0:]
        m_new = jnp.maximum(m_old, jnp.max(st, axis=0, keepdims=True))
        pt = jnp.exp2(st - m_new).astype(BF16)
        alpha = jnp.exp2(m_old - m_new)
        vt = vt_ref[jt, e * HEAD_DIM:(e + 1) * HEAD_DIM, hf * tk:(hf + 1) * tk]
        acc_sc[e, :, q0:] = alpha * acc_sc[e, :, q0:] + _dot(_with_ones(vt), pt)
        m_sc[e, :, q0:] = m_new

    for e in heads:
        scores(0, 0, e)

    def body(jt):
        for e in heads:
            scores(jt, 1, e)
            consume(jt, 0, e, False)
        for e in heads:
            scores(jt + 1, 0, e)
            consume(jt, 1, e, False)

    _loop_in_pairs(i, body)
    for e in heads:
        scores(i, 1, e, diag=True)
        consume(i, 0, e, True)
    for e in heads:
        consume(i, 1, e, True)
    for p in range(FOX_HEADS_PER_STEP // 2):
        outs = []
        for e in (2 * p, 2 * p + 1):
            acc = acc_sc[e]
            outs.append(acc[:HEAD_DIM] / jnp.maximum(acc[HEAD_DIM:HEAD_DIM + 1], 1e-30))
        sl = slice(p * LANES, (p + 1) * LANES)
        out = jnp.concatenate(outs, axis=0).T
        o_ref[:, sl] = (out * _silu(gate_ref[:, sl].astype(F32))).astype(o_ref.dtype)


def _fox_attention(qt, k, kb, vt, gate):
    b, s, _ = k.shape
    t = FOX_T
    assert s % t == 0 and vt.shape == (b, s // t, MIX_WIDTH, t) and qt.shape == vt.shape
    tile = lambda bi, p, i: (bi, i, p)
    nh = FOX_HEADS_PER_STEP
    wd = nh * HEAD_DIM
    return pl.pallas_call(
        _fox_kernel,
        grid=(b, N_HEADS // nh, s // t),
        in_specs=[pl.BlockSpec((None, 1, wd, t), lambda bi, p, i: (bi, i, p, 0)),
                  pl.BlockSpec((None, s, wd), lambda bi, p, i: (bi, 0, p)),
                  pl.BlockSpec((None, s, LANES), lambda bi, p, i: (bi, 0, 0)),
                  pl.BlockSpec((None, s // t, wd, t), lambda bi, p, i: (bi, 0, p, 0)),
                  pl.BlockSpec((None, t, wd), tile)],
        out_specs=pl.BlockSpec((None, t, wd), tile),
        out_shape=jax.ShapeDtypeStruct((b, s, MIX_WIDTH), BF16),
        scratch_shapes=[pltpu.VMEM((nh, 1, t), F32), pltpu.VMEM((nh, ACC_ROWS, t), F32),
                        pltpu.VMEM((nh, 2, t // 2, t), F32)],
        compiler_params=_cparams(("arbitrary", "arbitrary", "arbitrary")),
        name="fox_attn",
    )(qt, k, kb, vt, gate)


def _gelu_tanh(x):
    return 0.5 * x * (1.0 + jnp.tanh(math.sqrt(2.0 / math.pi) * (x + 0.044715 * (x * x * x))))


def _nsa_compress_kernel(ak_ref, av_ref, pe_ref, wa_ref, wb_ref, w2_ref, w2t_ref, c_ref, s_ref,
                         kc_ref, vct_ref):
    nrow = ak_ref.shape[0]
    for idx, a_ref in enumerate((ak_ref, av_ref)):
        a = a_ref[...].astype(F32)
        xa = (a + pe_ref[idx, 0:1, :]).astype(BF16)
        xb = (a + pe_ref[idx, 1:2, :]).astype(BF16)
        ya = _dot(xa, wa_ref[idx])
        yb = _dot(xb, wb_ref[idx])
        hid = _gelu_tanh(ya + pltpu.roll(yb, nrow - 1, 0)).astype(BF16)
        if idx == 0:
            y = _dot(hid, w2_ref[...])
            kc_ref[...] = _apply_rope(y, c_ref[...], s_ref[...]).astype(kc_ref.dtype)
        else:
            vct_ref[...] = _dot_nt(w2t_ref[...], hid).astype(vct_ref.dtype)


def _nsa_compress(kc, vc, pe_k, w1_k, w2_k, pe_v, w1_v, w2_v, rope_c, rope_s):
    b, s, _ = kc.shape
    ns = s // CMP_STRIDE
    g = NSA_KV
    flat = CMP_STRIDE * g * HEAD_DIM

    def w1_halves(w1):
        w1r = w1.reshape(2, CMP_STRIDE, HEAD_DIM, CMP_HIDDEN)
        outs = []
        for hf in range(2):
            z = jnp.einsum('ldc,gh->lgdhc', w1r[hf], jnp.eye(g, dtype=F32))
            outs.append(z.reshape(flat, g * CMP_HIDDEN))
        return outs

    def pe_halves(pe):
        per = pe.reshape(2, CMP_STRIDE, 1, HEAD_DIM)
        return jnp.broadcast_to(per, (2, CMP_STRIDE, g, HEAD_DIM)).reshape(2, flat)

    def w2_bd(w2):
        z = jnp.einsum('cd,gh->gchd', w2, jnp.eye(g, dtype=F32))
        return z.reshape(g * CMP_HIDDEN, g * HEAD_DIM)

    ka, kb = w1_halves(w1_k)
    va, vb = w1_halves(w1_v)
    wa = jnp.stack([ka, va]).astype(BF16)
    wb = jnp.stack([kb, vb]).astype(BF16)
    w2 = w2_bd(w2_k).astype(BF16)
    w2t = w2_bd(w2_v).T.astype(BF16)
    pe = jnp.stack([pe_halves(pe_k), pe_halves(pe_v)]).astype(F32)
    whole = lambda bi: (0, 0, 0)
    per_b = lambda bi: (bi, 0, 0)
    return pl.pallas_call(
        _nsa_compress_kernel,
        grid=(b,),
        in_specs=[pl.BlockSpec((None, ns, flat), per_b),
                  pl.BlockSpec((None, ns, flat), per_b),
                  pl.BlockSpec((2, 2, flat), whole),
                  pl.BlockSpec((2, flat, g * CMP_HIDDEN), whole),
                  pl.BlockSpec((2, flat, g * CMP_HIDDEN), whole),
                  pl.BlockSpec((g * CMP_HIDDEN, LANES), lambda bi: (0, 0)),
                  pl.BlockSpec((LANES, g * CMP_HIDDEN), lambda bi: (0, 0)),
                  pl.BlockSpec((None, ns, LANES), per_b),
                  pl.BlockSpec((None, ns, LANES), per_b)],
        out_specs=[pl.BlockSpec((None, ns, LANES), per_b),
                   pl.BlockSpec((None, LANES, ns), per_b)],
        out_shape=[jax.ShapeDtypeStruct((b, ns, LANES), BF16),
                   jax.ShapeDtypeStruct((b, LANES, ns), BF16)],
        compiler_params=_cparams(("arbitrary",)),
        name="nsa_compress",
    )(kc.reshape(b, ns, flat), vc.reshape(b, ns, flat), pe, wa, wb, w2, w2t,
      rope_c.reshape(b, ns, LANES), rope_s.reshape(b, ns, LANES))


NSA_QB = 2
NSA_TK = 256
NSA_REP = N_HEADS // NSA_KV


def _nsa_kernel(n_cmp, n_sel, n_win, *refs):
    n_wblk = n_win + NSA_QB - 1
    q_ref, kc_ref, vct_ref, ks_ref, vst_ref = refs[:5]
    kw_refs = refs[5:5 + n_wblk]
    vwt_refs = refs[5 + n_wblk:5 + 2 * n_wblk]
    (gl_ref, gate_ref, ovt_ref, blk_ref, wout_ref, x_ref, fg_ref,
     o_ref) = refs[5 + 2 * n_wblk:13 + 2 * n_wblk]
    st_sc, stc_sc, stw_sc, m_sc, acc_sc, ocmp_sc, owin_sc = refs[13 + 2 * n_wblk:]

    n = pl.program_id(1)
    tq, tk, rep = BLK, NSA_TK, NSA_REP
    streams = [(qb, g) for qb in range(NSA_QB) for g in range(NSA_KV)]
    ncp = kc_ref.shape[0]
    t_lane = [(n * NSA_QB + qb) * tq + lax.broadcasted_iota(jnp.int32, (1, tq), 1)
              for qb in range(NSA_QB)]
    zeros = jnp.zeros((HEAD_DIM, tq), BF16)
    qg = []
    for qb, g in streams:
        tiles = []
        for r in range(rep):
            h = g * rep + r
            mine = q_ref[qb, h * HEAD_DIM:(h + 1) * HEAD_DIM, :]
            tiles.append(jnp.concatenate([mine, zeros] if g % 2 == 0 else [zeros, mine], axis=0))
        qg.append(jnp.concatenate(tiles, axis=1))

    ci = lax.broadcasted_iota(jnp.int32, (ncp, tq), 0)
    kwn = n_win * BLK
    krow = lax.broadcasted_iota(jnp.int32, (kwn, tq), 0) - (n_win - 1) * BLK
    dist = lax.broadcasted_iota(jnp.int32, (kwn, tq), 1) - krow
    band = (dist >= 0) & (dist <= NSA_WINDOW - 1)
    sel_rows = ovt_ref.shape[0]
    rowi = lax.broadcasted_iota(jnp.int32, (sel_rows, tq), 0)
    rowf = rowi.astype(F32)

    def cmp_scores(sg):
        stc_sc[sg] = _dot(kc_ref[...], qg[sg])

    def cmp_consume(sg):
        qb, g = streams[sg]
        cmask = (ci * CMP_STRIDE + (CMP_LEN - 1) <= t_lane[qb]) & (ci < n_cmp)
        ps = []
        for r in range(rep):
            s_r = jnp.where(cmask, stc_sc[sg, :, r * tq:(r + 1) * tq], MASK_VALUE)
            m = jnp.max(s_r, axis=0, keepdims=True)
            pr = jnp.where(cmask, jnp.exp2(s_r - m), 0.0)
            ps.append(pr.astype(BF16))
        lhs = jnp.concatenate([_with_ones(vct_ref[g * HEAD_DIM:(g + 1) * HEAD_DIM, :]),
                               ovt_ref[...]], axis=0)
        res = _dot(lhs, jnp.concatenate(ps, axis=1))
        inv = 1.0 / jnp.maximum(res[HEAD_DIM:HEAD_DIM + 1], 1e-30)
        ocmp_sc[sg] = res[:HEAD_DIM] * inv
        imp = None
        for r in range(rep):
            ql = slice(r * tq, (r + 1) * tq)
            t = res[ACC_ROWS:, ql] * inv[:, ql]
            imp = t if imp is None else imp + t
        return imp

    def select(sg, imp):
        cur = t_lane[streams[sg][0]] // SEL_LEN
        forced = (rowi == 0) | (rowi == cur)
        causal = rowi <= cur
        score = jnp.where(causal, jnp.where(forced, FORCED_SCORE, imp), MASK_VALUE)
        score = jnp.where(rowi < n_sel, score, PAD_SCORE)
        chosen = jnp.zeros((sel_rows, tq), F32)
        for _ in range(min(SEL_TOPK, n_sel)):
            mx = jnp.max(score, axis=0, keepdims=True)
            first = jnp.min(jnp.where(score == mx, rowf, float(LANES)), axis=0, keepdims=True)
            hit = rowf == first
            chosen = jnp.where(hit, 1.0, chosen)
            score = jnp.where(hit, PAD_SCORE, score)
        bias_t = jnp.where(causal & (chosen > 0.5), 0.0, MASK_VALUE)
        bias_t = jnp.concatenate([bias_t, jnp.zeros((LANES - sel_rows, tq), F32)], axis=0)
        bias_t = bias_t.astype(BF16)
        return jnp.concatenate([qg[sg], jnp.concatenate([bias_t] * rep, axis=1)], axis=0)

    def win_scores(sg):
        qb = streams[sg][0]
        kwcat = jnp.concatenate([kr[...] for kr in kw_refs[qb:qb + n_win]], axis=0)
        stw_sc[sg] = _dot(kwcat, qg[sg])

    def win_consume(sg):
        qb, g = streams[sg]
        wok = band & ((n * NSA_QB + qb) * BLK + krow >= 0)
        vwt = jnp.concatenate([vr[0] for vr in vwt_refs[qb:qb + n_win]], axis=1)
        pts = []
        for r in range(rep):
            s_r = jnp.where(wok, stw_sc[sg, :, r * tq:(r + 1) * tq], MASK_VALUE)
            m = jnp.max(s_r, axis=0, keepdims=True)
            pts.append(jnp.exp2(s_r - m).astype(BF16))
        owin_sc[sg] = _dot(_with_ones(vwt[g * HEAD_DIM:(g + 1) * HEAD_DIM, :]),
                           jnp.concatenate(pts, axis=1))

    ns = len(streams)
    cmp_scores(0)
    cmp_scores(1)
    imps = []
    for sg in range(ns):
        if sg + 2 < ns:
            cmp_scores(sg + 2)
        else:
            win_scores(sg + 2 - ns)
        imps.append(cmp_consume(sg))
    qsel = []
    for sg in range(ns):
        if sg + 2 < ns:
            win_scores(sg + 2)
        qsel.append(select(sg, imps[sg]))
        win_consume(sg)

    for sg in range(ns):
        m_sc[sg] = jnp.full((1, rep * tq), MASK_VALUE, F32)
        acc_sc[sg] = jnp.zeros((ACC_ROWS, rep * tq), F32)

    hrep = rep // 2
    wide = hrep * tq
    units = [(sg, hf) for sg in range(ns) for hf in range(2)]
    nu = len(units)

    def scores(c, u):
        sg, hf = units[u]
        k0 = pl.multiple_of(c * tk, tk)
        kx = jnp.concatenate([ks_ref[pl.ds(k0, tk), :], blk_ref[pl.ds(k0, tk), :]], axis=1)
        st_sc[u] = _dot(kx, qsel[sg][:, hf * wide:(hf + 1) * wide])

    def consume(c, u, last):
        sg, hf = units[u]
        qb, g = streams[sg]
        lanes = slice(hf * wide, (hf + 1) * wide)
        if last:
            tok_ok = c * tk + lax.broadcasted_iota(jnp.int32, (tk, tq), 0) <= t_lane[qb]
        m_old = m_sc[sg, :, lanes]
        ms, pts = [], []
        for r in range(hrep):
            s_r = st_sc[u, :, r * tq:(r + 1) * tq]
            if last:
                s_r = jnp.where(tok_ok, s_r, MASK_VALUE)
            m_new = jnp.maximum(m_old[:, r * tq:(r + 1) * tq], jnp.max(s_r, axis=0, keepdims=True))
            pts.append(jnp.exp2(s_r - m_new).astype(BF16))
            ms.append(m_new)
        m_new = jnp.concatenate(ms, axis=1)
        alpha = jnp.exp2(m_old - m_new)
        vt = vst_ref[c, g * HEAD_DIM:(g + 1) * HEAD_DIM, :]
        acc_sc[sg, :, lanes] = (alpha * acc_sc[sg, :, lanes]
                                + _dot(_with_ones(vt), jnp.concatenate(pts, axis=1)))
        m_sc[sg, :, lanes] = m_new

    n_steps = ((n * NSA_QB + 1) * tq + tk - 1) // tk
    scores(0, 0)
    scores(0, 1)

    def body(c):
        for u in range(nu):
            if u + 2 < nu:
                scores(c, u + 2)
            else:
                scores(c + 1, u + 2 - nu)
            consume(c, u, False)

    _loop_in_pairs(n_steps - 1, body)
    for u in range(nu):
        if u + 2 < nu:
            scores(n_steps - 1, u + 2)
        consume(n_steps - 1, u, True)

    for qb in range(NSA_QB):
        tok = slice(qb * tq, (qb + 1) * tq)
        gate_t = (1.0 / (1.0 + jnp.exp(-gl_ref[tok, :]))).T
        pairs = []
        for p in range(N_PAIRS):
            sl = slice(p * LANES, (p + 1) * LANES)
            g = (2 * p) // rep
            sg = qb * NSA_KV + g
            rows = []
            for e in range(2):
                h = 2 * p + e
                ql = slice((h - g * rep) * tq, (h - g * rep + 1) * tq)
                slc, win = acc_sc[sg][:, ql], owin_sc[sg][:, ql]
                branches = (ocmp_sc[sg][:, ql],
                            slc[:HEAD_DIM] / jnp.maximum(slc[HEAD_DIM:HEAD_DIM + 1], 1e-30),
                            win[:HEAD_DIM] / win[HEAD_DIM:HEAD_DIM + 1])
                out = None
                for j, br in enumerate(branches):
                    t = gate_t[3 * h + j:3 * h + j + 1, :] * br
                    out = t if out is None else out + t
                rows.append(out)
            pair = jnp.concatenate(rows, axis=0).T
            pairs.append((pair * _silu(gate_ref[tok, sl].astype(F32))).astype(BF16))
        y = x_ref[tok, :] + _dot(jnp.concatenate(pairs, axis=1), wout_ref[...])
        var = jnp.mean(y * y, axis=-1, keepdims=True)
        o_ref[tok, :] = y * lax.rsqrt(var + NORM_EPS) * fg_ref[...]


def _selection_overlap_t(n_cmp_pad, n_cmp, n_sel):
    cs = np.arange(n_cmp_pad) * CMP_STRIDE
    js = np.arange(LANES) * SEL_LEN
    ov = np.minimum(cs[None, :] + CMP_LEN, js[:, None] + SEL_LEN) - np.maximum(cs[None, :], js[:, None])
    ov = (np.clip(ov, 0, None) / CMP_LEN).astype(np.float32)
    ov[:, n_cmp:] = 0.0
    ov[n_sel:, :] = 0.0
    return ov


def _nsa_attention(q_t, k_cmp, v_cmp_t, ks, vs_t, kw, vw_t, g_logit, gate, w_out, x, final_gain):
    b, s, d = x.shape
    tq = NSA_QB * BLK
    ncp = k_cmp.shape[1]
    n_cmp = s // CMP_STRIDE - 1
    n_sel = s // SEL_LEN
    n_win = -(-(NSA_WINDOW - 1) // BLK) + 1
    assert n_sel <= LANES and s % NSA_TK == 0 and tq == NSA_TK
    sel_rows = -(-n_sel // BF16_SUBLANES) * BF16_SUBLANES
    ov_t = jnp.asarray(_selection_overlap_t(ncp, n_cmp, n_sel)[:sel_rows], BF16)
    row = lambda bi, i: (bi, i, 0)
    per_b = lambda bi, i: (bi, 0, 0)
    in_specs = [pl.BlockSpec((None, NSA_QB, MIX_WIDTH, BLK), lambda bi, i: (bi, i, 0, 0)),
                pl.BlockSpec((None, ncp, LANES), per_b),
                pl.BlockSpec((None, LANES, ncp), per_b),
                pl.BlockSpec((None, s, LANES), per_b),
                pl.BlockSpec((None, s // NSA_TK, LANES, NSA_TK), lambda bi, i: (bi, 0, 0, 0))]
    args = [q_t, k_cmp, v_cmp_t, ks, vs_t]
    for j in range(n_win - 1, -NSA_QB, -1):
        in_specs.append(pl.BlockSpec(
            (None, BLK, LANES), lambda bi, i, j=j: (bi, jnp.maximum(NSA_QB * i - j, 0), 0)))
        args.append(kw)
    for j in range(n_win - 1, -NSA_QB, -1):
        in_specs.append(pl.BlockSpec(
            (None, 1, LANES, BLK), lambda bi, i, j=j: (bi, jnp.maximum(NSA_QB * i - j, 0), 0, 0)))
        args.append(vw_t)
    block_onehot = jnp.asarray(
        (np.arange(s)[:, None] // SEL_LEN == np.arange(LANES)[None, :]).astype(np.float32), BF16)
    in_specs += [pl.BlockSpec((None, tq, LANES), row),
                 pl.BlockSpec((None, tq, MIX_WIDTH), row),
                 pl.BlockSpec((sel_rows, ncp), lambda bi, i: (0, 0)),
                 pl.BlockSpec((s, LANES), lambda bi, i: (0, 0)),
                 pl.BlockSpec((MIX_WIDTH, d), lambda bi, i: (0, 0), pipeline_mode=pl.Buffered(1)),
                 pl.BlockSpec((None, tq, d), row),
                 pl.BlockSpec((1, d), lambda bi, i: (0, 0))]
    args += [g_logit, gate, ov_t, block_onehot, w_out.astype(BF16), x, final_gain[None, :]]
    wide = NSA_REP * BLK
    ns = NSA_QB * NSA_KV
    return pl.pallas_call(
        functools.partial(_nsa_kernel, n_cmp, n_sel, n_win),
        grid=(b, s // tq),
        in_specs=in_specs,
        out_specs=pl.BlockSpec((None, tq, d), row),
        out_shape=jax.ShapeDtypeStruct((b, s, d), F32),
        scratch_shapes=[pltpu.VMEM((2 * ns, NSA_TK, wide // 2), F32),
                        pltpu.VMEM((ns, ncp, wide), F32),
                        pltpu.VMEM((ns, n_win * BLK, wide), F32),
                        pltpu.VMEM((ns, 1, wide), F32),
                        pltpu.VMEM((ns, ACC_ROWS, wide), F32),
                        pltpu.VMEM((ns, HEAD_DIM, wide), F32),
                        pltpu.VMEM((ns, ACC_ROWS, wide), F32)],
        compiler_params=_cparams(("arbitrary", "arbitrary")),
        name="nsa_attn",
    )(*args)


def _split(w, sizes):
    offs = np.cumsum([0] + list(sizes))
    return [w[:, int(offs[i]):int(offs[i + 1])] for i in range(len(sizes))]


def _swa_layer(x, rope, gain, w_in, sinks, w_out):
    kvw = SWA_KV * HEAD_DIM
    parts = _split(w_in, [MIX_WIDTH, kvw, kvw, MIX_WIDTH])
    segs = [_Seg(MIX_WIDTH, rope=True, scale=Q_SCALE), _Seg(kvw, rope=True), _Seg(kvw),
            _Seg(MIX_WIDTH)]
    q, k, v, gate = _norm_proj(x, gain, parts, segs, rope)
    o = _banded_attention(q, k, v, SWA_WINDOW - 1, sinks=sinks, gate=gate)
    return o, w_out


def _dilated_layer(x, rope, gain, w_in, w_out, residual=None):
    b, s, _ = x.shape
    kvw = DIL_KV * HEAD_DIM
    sizes, segs = [], []
    for window, dil in DIL_PATTERNS:
        assert s % (dil * BLK) == 0
        sizes += [MIX_WIDTH, kvw, kvw]
        segs += [_Seg(MIX_WIDTH, rope=True, scale=Q_SCALE, dil=dil), _Seg(kvw, rope=True, dil=dil),
                 _Seg(kvw, dil=dil)]
    sizes.append(MIX_WIDTH)
    segs.append(_Seg(MIX_WIDTH))
    res = _norm_proj(x, gain, _split(w_in, sizes), segs, rope, residual)
    if residual is not None:
        x, res = res[0], res[1:]
    gate = res[-1]
    os_, lses, dils = [], [], []
    for gi, (window, dil) in enumerate(DIL_PATTERNS):
        q, k, v = res[3 * gi:3 * gi + 3]
        if dil > 1:
            q, k, v = (t.reshape(b * dil, s // dil, t.shape[-1]) for t in (q, k, v))
        o, lse = _banded_attention(q, k, v, window // dil, want_lse=True)
        if dil > 1:
            o = o.reshape(b, dil, s // dil, MIX_WIDTH)
            lse = lse.reshape(b, dil, s // dil, LANES)
        os_.append(o)
        lses.append(lse)
        dils.append(dil)
    return _dil_combine_out_proj(os_, lses, gate, tuple(dils), w_out, x)


def _fox_layer(x, gain, w_in, b_f, w_out):
    sizes = [MIX_WIDTH, MIX_WIDTH, MIX_WIDTH, N_HEADS, MIX_WIDTH]
    segs = [_Seg(MIX_WIDTH, scale=Q_SCALE, tile=FOX_T), _Seg(MIX_WIDTH),
            _Seg(MIX_WIDTH, tile=FOX_T), _Seg(LANES, dtype=F32), _Seg(MIX_WIDTH)]
    qt, k, vt, f_logit, gate = _norm_proj(x, gain, _split(w_in, sizes), segs)
    kb = _fox_decay(f_logit, b_f)
    return _fox_attention(qt, k, kb, vt, gate), w_out


def _nsa_layer(x, positions, rope, gain, w_in, pe_k, w1_k, w2_k, pe_v, w1_v, w2_v, w_out,
               residual, final_gain):
    b, s, _ = x.shape
    kvw = NSA_KV * HEAD_DIM
    sizes = [MIX_WIDTH] + [kvw] * 6 + [3 * N_HEADS, MIX_WIDTH]
    segs = [_Seg(MIX_WIDTH, rope=True, scale=Q_SCALE, tile=BLK), _Seg(kvw), _Seg(kvw),
            _Seg(kvw, rope=True), _Seg(kvw, tile=NSA_TK), _Seg(kvw, rope=True), _Seg(kvw, tile=BLK),
            _Seg(LANES, dtype=F32), _Seg(MIX_WIDTH)]
    x, q, kc, vc, ks, vs_t, kw, vw_t, g_logit, gate = _norm_proj(
        x, gain, _split(w_in, sizes), segs, rope, residual)
    ns = s // CMP_STRIDE
    cmp_pos = jnp.concatenate(
        [positions[:, CMP_LEN - 1::CMP_STRIDE], positions[:, -1:]], axis=1)[:, :ns]
    cmp_c, cmp_s = _rope_tables(cmp_pos.reshape(-1))
    k_cmp, v_cmp_t = _nsa_compress(kc, vc, pe_k, w1_k, w2_k, pe_v, w1_v, w2_v, cmp_c, cmp_s)
    return _nsa_attention(q, k_cmp, v_cmp_t, ks, vs_t, kw, vw_t, g_logit, gate, w_out, x,
                          final_gain)


def kernel(x, positions, norm_0, w_in_0, sinks_0, w_out_0, norm_1, w_in_1, w_out_1, norm_2, w_in_2, b_f_2, w_out_2, norm_3, w_in_3, cmp_pe_k_3, cmp_w1_k_3, cmp_w2_k_3, cmp_pe_v_3, cmp_w1_v_3, cmp_w2_v_3, w_out_3, final_norm):
    rope = _rope_tables(positions.reshape(-1))
    o, w = _swa_layer(x, rope, norm_0, w_in_0, sinks_0, w_out_0)
    x = _dilated_layer(x, rope, norm_1, w_in_1, w_out_1, residual=(o, w))
    o, w = _fox_layer(x, norm_2, w_in_2, b_f_2, w_out_2)
    return _nsa_layer(x, positions, rope, norm_3, w_in_3, cmp_pe_k_3, cmp_w1_k_3, cmp_w2_k_3,
                      cmp_pe_v_3, cmp_w1_v_3, cmp_w2_v_3, w_out_3, residual=(o, w),
                      final_gain=final_norm)
```

```python
import functools
import math

import numpy as np
import jax
import jax.numpy as jnp
from jax import lax
from jax.experimental import pallas as pl
from jax.experimental.pallas import tpu as pltpu

HEAD_DIM = 64
N_HEADS = 16
N_PAIRS = N_HEADS // 2
MIX_WIDTH = N_HEADS * HEAD_DIM
ROT_DIM = HEAD_DIM // 4
ROT_HALF = ROT_DIM // 2
ROPE_THETA = 500000.0
BLK = 128
LANES = 128
NORM_EPS = 1e-6
MASK_VALUE = -1e30
PAD_SCORE = -3e38
LOG2E = math.log2(math.e)
Q_SCALE = HEAD_DIM ** -0.5 * LOG2E

SWA_KV = 4
SWA_WINDOW = 128
DIL_KV = 4
DIL_PATTERNS = ((128, 1), (512, 4), (2048, 16))
NSA_KV = 2
CMP_LEN = 32
CMP_STRIDE = 16
CMP_HIDDEN = 256
SEL_LEN = 64
SEL_TOPK = 8
NSA_WINDOW = 256
FORCED_SCORE = 1e4

VMEM_LIMIT_BYTES = 56 * 1024 * 1024
PROJ_ROWS = 512
PROJ_COLS = 512
BF16_SUBLANES = 16
ACC_ROWS = HEAD_DIM + BF16_SUBLANES
ROPE_ROWS = 2048
F32 = jnp.float32
BF16 = jnp.bfloat16


def _cparams(sem):
    return pltpu.CompilerParams(dimension_semantics=sem, vmem_limit_bytes=VMEM_LIMIT_BYTES)


def _lane_half(shape):
    return lax.broadcasted_iota(jnp.int32, shape, 1) // HEAD_DIM


def _swap_halves(t):
    return jnp.concatenate([t[:, HEAD_DIM:], t[:, :HEAD_DIM]], axis=1)


def _head_query(qp, e, kv_half):
    qh = jnp.where(_lane_half(qp.shape) == e, qp, jnp.zeros_like(qp))
    if e != kv_half:
        qh = _swap_halves(qh)
    return qh


def _group_queries(q_ref, g, rep):
    tiles = []
    for r in range(rep):
        h = g * rep + r
        tiles.append(_head_query(q_ref[:, (h // 2) * LANES:(h // 2 + 1) * LANES], h % 2, g % 2))
    return jnp.concatenate(tiles, axis=0)


def _dot_nt(a, b):
    return lax.dot_general(a, b, (((1,), (1,)), ((), ())), preferred_element_type=F32)


def _dot(a, b):
    return jnp.dot(a, b, preferred_element_type=F32)


def _silu(x):
    return x * (1.0 / (1.0 + jnp.exp(-x)))


def _loop_in_pairs(count, body):
    odd = count % 2

    @pl.when(odd == 1)
    def _():
        body(0)

    def two(jj, carry):
        body(odd + 2 * jj)
        body(odd + 2 * jj + 1)
        return carry

    lax.fori_loop(0, count // 2, two, 0)


def _with_ones(vt):
    return jnp.concatenate([vt, jnp.ones((ACC_ROWS - HEAD_DIM, vt.shape[1]), BF16)], axis=0)


def _rope_table_kernel(pos_ref, inv_ref, c_ref, s_ref):
    pos = pos_ref[...].astype(F32)
    ang = pos * inv_ref[...]
    d = lax.broadcasted_iota(jnp.int32, ang.shape, 1) % HEAD_DIM
    cos = jnp.cos(ang)
    sin = jnp.sin(ang)
    c_ref[...] = jnp.where(d < ROT_DIM, cos, 1.0)
    s_ref[...] = jnp.where(d < ROT_HALF, -sin, jnp.where(d < ROT_DIM, sin, 0.0))


def _rope_tables(pos_flat):
    t = pos_flat.shape[0]
    rows = min(t, ROPE_ROWS)
    assert t % rows == 0
    inv = jnp.power(ROPE_THETA, -jnp.arange(ROT_HALF, dtype=F32) / ROT_HALF)
    inv_l = jnp.tile(inv, LANES // ROT_HALF)[None, :]
    out = jax.ShapeDtypeStruct((t, LANES), F32)
    return pl.pallas_call(
        _rope_table_kernel,
        grid=(t // rows,),
        in_specs=[pl.BlockSpec((rows, 1), lambda i: (i, 0)),
                  pl.BlockSpec((1, LANES), lambda i: (0, 0))],
        out_specs=[pl.BlockSpec((rows, LANES), lambda i: (i, 0))] * 2,
        out_shape=[out, out],
        compiler_params=_cparams(("arbitrary",)),
        name="rope_tables",
    )(pos_flat[:, None], inv_l)


def _apply_rope(y, c, s):
    outs = []
    for j in range(y.shape[1] // LANES):
        t = y[:, j * LANES:(j + 1) * LANES]
        d = lax.broadcasted_iota(jnp.int32, t.shape, 1) % HEAD_DIM
        partner = jnp.where(d < ROT_HALF, pltpu.roll(t, LANES - ROT_HALF, 1),
                            pltpu.roll(t, ROT_HALF, 1))
        outs.append(t * c + partner * s)
    return outs[0] if len(outs) == 1 else jnp.concatenate(outs, axis=1)


class _Seg:
    def __init__(self, width, rope=False, scale=None, dtype=BF16, dil=1, tile=None):
        self.width, self.rope, self.scale, self.dtype, self.dil = width, rope, scale, dtype, dil
        self.tile = tile
        assert not (tile and dil > 1)


def _norm_proj_kernel(segs, use_rope, has_residual, *refs):
    has_t = any(sg.tile and not sg.rope for sg in segs)
    x_ref, g_ref, w_ref = refs[:3]
    k = 3
    if has_t:
        wt_ref = refs[k]
        k += 1
    if use_rope:
        c_ref, s_ref = refs[k:k + 2]
        k += 2
    if has_residual:
        po_ref, pw_ref = refs[k:k + 2]
        k += 2
        xnew_ref = refs[k]
        k += 1
    out_refs = refs[k:k + len(segs)]
    stage_ref = refs[k + len(segs)] if any(sg.dil > 1 for sg in segs) else None

    x = x_ref[...]
    if has_residual:
        x = x + _dot(po_ref[...], pw_ref[...])
        xnew_ref[...] = x
    var = jnp.mean(x * x, axis=-1, keepdims=True)
    h = (x * lax.rsqrt(var + NORM_EPS) * g_ref[...]).astype(BF16)
    rows = x.shape[0]
    col = 0
    tcol = 0
    for sg, o_ref in zip(segs, out_refs):
        if sg.tile and not sg.rope:
            for c0 in range(0, sg.width, PROJ_COLS):
                cw = min(PROJ_COLS, sg.width - c0)
                yt = _dot_nt(wt_ref[tcol + c0:tcol + c0 + cw, :], h)
                if sg.scale is not None:
                    yt = yt * sg.scale
                for ti in range(rows // sg.tile):
                    o_ref[ti, c0:c0 + cw, :] = yt[:, ti * sg.tile:(ti + 1) * sg.tile].astype(sg.dtype)
            tcol += sg.width
            continue
        for c0 in range(0, sg.width, PROJ_COLS):
            cw = min(PROJ_COLS, sg.width - c0)
            y = _dot(h, w_ref[:, col + c0:col + c0 + cw])
            if sg.rope:
                y = _apply_rope(y, c_ref[...], s_ref[...])
            if sg.scale is not None:
                y = y * sg.scale
            if sg.tile:
                yt = y.T
                for ti in range(rows // sg.tile):
                    o_ref[ti, c0:c0 + cw, :] = yt[:, ti * sg.tile:(ti + 1) * sg.tile].astype(sg.dtype)
            elif sg.dil > 1:
                sub = rows // sg.dil
                for j in range(cw // LANES):
                    stage_ref[j] = y[:, j * LANES:(j + 1) * LANES]
                for r in range(sg.dil):
                    for j in range(cw // LANES):
                        lo = c0 + j * LANES
                        o_ref[r, :, lo:lo + LANES] = (
                            stage_ref[j, pl.ds(r, sub, stride=sg.dil), :].astype(sg.dtype))
            else:
                o_ref[:, c0:c0 + cw] = y.astype(sg.dtype)
        col += sg.width


def _norm_proj(x, gain, w_parts, segs, rope=None, residual=None):
    b, s, d = x.shape
    tm = PROJ_ROWS
    assert s % tm == 0
    w_cols, wt_rows = [], []
    for wp, sg in zip(w_parts, segs):
        if wp.shape[1] < sg.width:
            wp = jnp.pad(wp, ((0, 0), (0, sg.width - wp.shape[1])))
        if sg.tile and not sg.rope:
            wt_rows.append(wp.T)
        else:
            w_cols.append(wp)
    w = jnp.concatenate(w_cols, axis=1).astype(BF16)
    n = w.shape[1]
    use_rope = rope is not None
    once = dict(pipeline_mode=pl.Buffered(1))
    in_specs = [pl.BlockSpec((None, tm, d), lambda bi, i: (bi, i, 0)),
                pl.BlockSpec((1, d), lambda bi, i: (0, 0)),
                pl.BlockSpec((d, n), lambda bi, i: (0, 0), **once)]
    args = [x, gain[None, :], w]
    if wt_rows:
        wt = jnp.concatenate(wt_rows, axis=0).astype(BF16)
        in_specs.append(pl.BlockSpec(wt.shape, lambda bi, i: (0, 0), **once))
        args.append(wt)
    if use_rope:
        in_specs += [pl.BlockSpec((None, tm, LANES), lambda bi, i: (bi, i, 0))] * 2
        args += [rope[0].reshape(b, s, LANES), rope[1].reshape(b, s, LANES)]
    out_specs, out_shape = [], []
    if residual is not None:
        po, pw = residual
        in_specs += [pl.BlockSpec((None, tm, po.shape[-1]), lambda bi, i: (bi, i, 0)),
                     pl.BlockSpec(pw.shape, lambda bi, i: (0, 0), **once)]
        args += [po, pw.astype(BF16)]
        out_shape.append(jax.ShapeDtypeStruct((b, s, d), F32))
        out_specs.append(pl.BlockSpec((None, tm, d), lambda bi, i: (bi, i, 0)))
    for sg in segs:
        if sg.dil > 1:
            assert tm % sg.dil == 0
            out_shape.append(jax.ShapeDtypeStruct((b, sg.dil, s // sg.dil, sg.width), sg.dtype))
            out_specs.append(pl.BlockSpec((None, sg.dil, tm // sg.dil, sg.width),
                                          lambda bi, i: (bi, 0, i, 0)))
        elif sg.tile:
            assert tm % sg.tile == 0
            out_shape.append(jax.ShapeDtypeStruct((b, s // sg.tile, sg.width, sg.tile), sg.dtype))
            out_specs.append(pl.BlockSpec((None, tm // sg.tile, sg.width, sg.tile),
                                          lambda bi, i: (bi, i, 0, 0)))
        else:
            out_shape.append(jax.ShapeDtypeStruct((b, s, sg.width), sg.dtype))
            out_specs.append(pl.BlockSpec((None, tm, sg.width), lambda bi, i: (bi, i, 0)))
    scratch = ([pltpu.VMEM((PROJ_COLS // LANES, tm, LANES), F32)]
               if any(sg.dil > 1 for sg in segs) else [])
    return pl.pallas_call(
        functools.partial(_norm_proj_kernel, segs, use_rope, residual is not None),
        grid=(b, s // tm),
        in_specs=in_specs,
        out_specs=out_specs,
        out_shape=out_shape,
        scratch_shapes=scratch,
        compiler_params=_cparams(("arbitrary", "arbitrary")),
        name="norm_proj",
    )(*args)


BANDED_MAX_QB = 8


def _banded_kernel(n_seq, n_qb, max_dist, n_prev, n_kv, has_sink, has_gate, want_lse, *refs):
    q_ref = refs[0]
    kp_refs, kc_ref = refs[1:1 + n_prev], refs[1 + n_prev]
    vp_refs, vc_ref = refs[2 + n_prev:2 + 2 * n_prev], refs[2 + 2 * n_prev]

    def window(prev_refs, cur_ref, sq):
        return ([pr.at[sq] for pr in prev_refs]
                + [cur_ref.at[sq, a * BLK:(a + 1) * BLK] for a in range(n_qb)])

    k = 3 + 2 * n_prev
    sink_ref = gate_ref = lse_ref = None
    if has_sink:
        sink_ref = refs[k]; k += 1
    if has_gate:
        gate_ref = refs[k]; k += 1
    o_ref = refs[k]; k += 1
    if want_lse:
        lse_ref = refs[k]; k += 1
    ot_sc = refs[k]
    st_sc = refs[k + 1]

    n = pl.program_id(1)
    tq = BLK
    kw = (n_prev + 1) * BLK
    rep = N_HEADS // n_kv
    krow = lax.broadcasted_iota(jnp.int32, (kw, tq), 0) - n_prev * BLK
    dist = lax.broadcasted_iota(jnp.int32, (kw, tq), 1) - krow
    band = (dist >= 0) & (dist <= max_dist)
    lse_rows = [[] for _ in range(n_seq * n_qb)]
    units = [(sq, qb, g) for sq in range(n_seq) for qb in range(n_qb) for g in range(n_kv)]

    def scores(u):
        sq, qb, g = units[u]
        sl = slice((g // 2) * LANES, (g // 2 + 1) * LANES)
        k_refs = window(kp_refs, kc_ref, sq)
        kcat = jnp.concatenate([kr[:, sl] for kr in k_refs[qb:qb + n_prev + 1]], axis=0)
        st_sc[u] = _dot_nt(kcat, _group_queries(q_ref.at[sq, qb * tq:(qb + 1) * tq], g, rep))

    oks, vts = {}, {}

    def consume(u):
        sq, qb, g = units[u]
        vb = sq * n_qb + qb
        if qb not in oks:
            oks[qb] = band & ((n * n_qb + qb) * BLK + krow >= 0)
        ok = oks[qb]
        if (vb, g // 2) not in vts:
            sl = slice((g // 2) * LANES, (g // 2 + 1) * LANES)
            v_refs = window(vp_refs, vc_ref, sq)
            vcat = jnp.concatenate([vr[:, sl] for vr in v_refs[qb:qb + n_prev + 1]], axis=0)
            vts[vb, g // 2] = vcat.astype(F32).T.astype(BF16)
        vt = vts[vb, g // 2]
        ms, pts = [], []
        for r in range(rep):
            s_r = jnp.where(ok, st_sc[u, :, r * tq:(r + 1) * tq], MASK_VALUE)
            m = jnp.max(s_r, axis=0, keepdims=True)
            if has_sink:
                h = g * rep + r
                m = jnp.maximum(m, sink_ref[0:1, h:h + 1])
            pts.append(jnp.exp2(s_r - m).astype(BF16))
            ms.append(m)
        acc = _dot(_with_ones(vt[(g % 2) * HEAD_DIM:(g % 2 + 1) * HEAD_DIM, :]),
                   jnp.concatenate(pts, axis=1))
        for r in range(rep):
            h = g * rep + r
            den = acc[HEAD_DIM:HEAD_DIM + 1, r * tq:(r + 1) * tq]
            if has_sink:
                den = den + jnp.exp2(sink_ref[0:1, h:h + 1] - ms[r])
            ot_sc[vb * N_HEADS + h] = acc[:HEAD_DIM, r * tq:(r + 1) * tq] / den
            if want_lse:
                lse_rows[vb].append(ms[r] + jnp.log2(den))

    scores(0)
    scores(1)
    for u in range(len(units)):
        if u + 2 < len(units):
            scores(u + 2)
        consume(u)
    for vb in range(n_seq * n_qb):
        sq, qb = vb // n_qb, vb % n_qb
        rows = slice(qb * tq, (qb + 1) * tq)
        for p in range(N_PAIRS):
            sl = slice(p * LANES, (p + 1) * LANES)
            pair = jnp.concatenate([ot_sc[vb * N_HEADS + 2 * p],
                                    ot_sc[vb * N_HEADS + 2 * p + 1]], axis=0).T
            if has_gate:
                pair = pair * _silu(gate_ref[sq, rows, sl].astype(F32))
            o_ref[sq, rows, sl] = pair.astype(o_ref.dtype)
        if want_lse:
            lse_t = jnp.concatenate(
                lse_rows[vb] + [jnp.zeros((LANES - N_HEADS, tq), F32)], axis=0)
            lse_ref[sq, rows, :] = lse_t.T


def _banded_attention(q, k, v, max_dist, sinks=None, gate=None, want_lse=False):
    bq, sq, _ = q.shape
    kvw = k.shape[-1]
    n_kv = kvw // HEAD_DIM
    n_prev = -(-max_dist // BLK)
    n_qb = max(d for d in range(1, BANDED_MAX_QB + 1) if (sq // BLK) % d == 0)
    n_seq = max(d for d in range(1, BANDED_MAX_QB // n_qb + 1) if bq % d == 0)
    tq = n_qb * BLK
    assert sq % tq == 0
    row = lambda b, i: (b, i, 0)
    in_specs = [pl.BlockSpec((n_seq, tq, MIX_WIDTH), row)]
    args = [q]
    for arr in (k, v):
        for j in range(n_prev, 0, -1):
            in_specs.append(pl.BlockSpec(
                (n_seq, BLK, kvw), lambda b, i, j=j: (b, jnp.maximum(n_qb * i - j, 0), 0)))
            args.append(arr)
        in_specs.append(pl.BlockSpec((n_seq, tq, kvw), row))
        args.append(arr)
    if sinks is not None:
        in_specs.append(pl.BlockSpec((1, LANES), lambda b, i: (0, 0)))
        args.append(jnp.pad(sinks.astype(F32) * LOG2E, (0, LANES - N_HEADS))[None, :])
    if gate is not None:
        in_specs.append(pl.BlockSpec((n_seq, tq, MIX_WIDTH), row))
        args.append(gate)
    out_specs = [pl.BlockSpec((n_seq, tq, MIX_WIDTH), row)]
    out_shape = [jax.ShapeDtypeStruct((bq, sq, MIX_WIDTH), BF16)]
    if want_lse:
        out_specs.append(pl.BlockSpec((n_seq, tq, LANES), row))
        out_shape.append(jax.ShapeDtypeStruct((bq, sq, LANES), F32))
    n_vb = n_seq * n_qb
    res = pl.pallas_call(
        functools.partial(_banded_kernel, n_seq, n_qb, max_dist, n_prev, n_kv, sinks is not None,
                          gate is not None, want_lse),
        grid=(bq // n_seq, sq // tq),
        in_specs=in_specs,
        out_specs=out_specs,
        out_shape=out_shape,
        scratch_shapes=[pltpu.VMEM((n_vb * N_HEADS, HEAD_DIM, BLK), F32),
                        pltpu.VMEM((n_vb * n_kv, (n_prev + 1) * BLK,
                                    (N_HEADS // n_kv) * BLK), F32)],
        compiler_params=_cparams(("arbitrary", "arbitrary")),
        name="banded_attn",
    )(*args)
    return res if want_lse else res[0]


def _dil_combine_kernel(dils, *refs):
    ng = len(dils)
    o_refs, l_refs = refs[:ng], refs[ng:2 * ng]
    gate_ref, expand_ref, w_ref, x_ref, out_ref, stage_ref, lstage_ref = refs[2 * ng:]
    rows = out_ref.shape[0]

    def natural(ref, dil, stage, slab):
        sl = slice(slab * LANES, (slab + 1) * LANES)
        if dil == 1:
            return ref[:, sl].astype(F32)
        sub = rows // dil
        for r in range(dil):
            stage[pl.ds(r, sub, stride=dil), :] = ref[r, :, sl].astype(F32)
        return stage[...]

    lses = [natural(l_refs[i], dils[i], lstage_ref, 0) for i in range(ng)]
    mx = functools.reduce(jnp.maximum, lses)
    ws = [jnp.exp2(l - mx) for l in lses]
    tot = functools.reduce(lambda a, c: a + c, ws)
    inv = 1.0 / tot

    def widen(w):
        hi = w.astype(BF16)
        lo = (w - hi.astype(F32)).astype(BF16)
        return _dot(hi, expand_ref[...]) + _dot(lo, expand_ref[...])

    wide = [widen(w * inv) for w in ws[:-1]]
    wide.append(1.0 - functools.reduce(lambda a, c: a + c, wide))
    gated = []
    for p in range(N_PAIRS):
        sl = slice(p * LANES, (p + 1) * LANES)
        acc = None
        for gi in range(ng):
            t = wide[gi][:, sl] * natural(o_refs[gi], dils[gi], stage_ref, p)
            acc = t if acc is None else acc + t
        gated.append((acc * _silu(gate_ref[:, sl].astype(F32))).astype(BF16))
    out_ref[...] = x_ref[...] + _dot(jnp.concatenate(gated, axis=1), w_ref[...])


def _dil_combine_out_proj(os_, lses, gate, dils, w_out, x):
    b, s, d = x.shape
    tm = PROJ_ROWS
    expand = jnp.asarray((np.arange(LANES)[:, None] == np.arange(MIX_WIDTH)[None, :] // HEAD_DIM)
                         .astype(np.float32), BF16)
    in_specs, args = [], []
    for arrs, width in ((os_, MIX_WIDTH), (lses, LANES)):
        for arr, dil in zip(arrs, dils):
            if dil == 1:
                in_specs.append(pl.BlockSpec((None, tm, width), lambda bi, i: (bi, i, 0)))
            else:
                in_specs.append(pl.BlockSpec((None, dil, tm // dil, width),
                                             lambda bi, i: (bi, 0, i, 0)))
            args.append(arr)
    row = lambda bi, i: (bi, i, 0)
    in_specs += [pl.BlockSpec((None, tm, MIX_WIDTH), row),
                 pl.BlockSpec((LANES, MIX_WIDTH), lambda bi, i: (0, 0)),
                 pl.BlockSpec((MIX_WIDTH, d), lambda bi, i: (0, 0)),
                 pl.BlockSpec((None, tm, d), row)]
    args += [gate, expand, w_out.astype(BF16), x]
    return pl.pallas_call(
        functools.partial(_dil_combine_kernel, dils),
        grid=(b, s // tm),
        in_specs=in_specs,
        out_specs=pl.BlockSpec((None, tm, d), row),
        out_shape=jax.ShapeDtypeStruct((b, s, d), F32),
        scratch_shapes=[pltpu.VMEM((tm, LANES), F32), pltpu.VMEM((tm, LANES), F32)],
        compiler_params=_cparams(("arbitrary", "arbitrary")),
        name="dil_combine_out_proj",
    )(*args)


N_BIAS_PIECES = 3


def _fox_decay_kernel(f_ref, b_ref, kb_ref):
    s = f_ref.shape[0]
    x = f_ref[...] + b_ref[...]
    logf = jnp.minimum(x, 0.0) - jnp.log1p(jnp.exp(-jnp.abs(x)))
    r = lax.broadcasted_iota(jnp.int32, (BLK, BLK), 0)
    c = lax.broadcasted_iota(jnp.int32, (BLK, BLK), 1)
    tri = (c <= r).astype(F32)
    lane = lax.broadcasted_iota(jnp.int32, (BLK, LANES), 1)
    carry = jnp.zeros((1, LANES), F32)
    for i in range(s // BLK):
        blk = logf[i * BLK:(i + 1) * BLK, :]
        cs = jnp.dot(tri, blk, preferred_element_type=F32, precision=lax.Precision.HIGHEST) + carry
        carry = cs[BLK - 1:BLK, :]
        rest = cs * (-LOG2E)
        out = jnp.zeros((BLK, LANES), F32)
        for j in range(N_BIAS_PIECES):
            piece = rest.astype(BF16).astype(F32)
            rest = rest - piece
            moved = piece if j == 0 else pltpu.roll(piece, N_HEADS * j, 1)
            out = jnp.where((lane >= N_HEADS * j) & (lane < N_HEADS * (j + 1)), moved, out)
        kb_ref[i * BLK:(i + 1) * BLK, :] = out.astype(BF16)


def _fox_decay(f_logit, b_f):
    b, s, _ = f_logit.shape
    b_pad = jnp.pad(b_f.astype(F32), (0, LANES - N_HEADS))[None, :]
    return pl.pallas_call(
        _fox_decay_kernel,
        grid=(b,),
        in_specs=[pl.BlockSpec((None, s, LANES), lambda bi: (bi, 0, 0)),
                  pl.BlockSpec((1, LANES), lambda bi: (0, 0))],
        out_specs=pl.BlockSpec((None, s, LANES), lambda bi: (bi, 0, 0)),
        out_shape=jax.ShapeDtypeStruct((b, s, LANES), BF16),
        compiler_params=_cparams(("arbitrary",)),
        name="fox_decay",
    )(f_logit, b_pad)


FOX_T = PROJ_ROWS
FOX_HEADS_PER_STEP = 8


def _fox_kernel(qt_ref, k_ref, kb_ref, vt_ref, gate_ref, o_ref, m_sc, acc_sc, st_sc):
    pi = pl.program_id(1)
    i = pl.program_id(2)
    t = FOX_T
    heads = range(FOX_HEADS_PER_STEP)
    row = lax.broadcasted_iota(jnp.int32, (LANES, t), 0)
    zeros = jnp.zeros((HEAD_DIM, t), BF16)
    qx = []
    for e in heads:
        h = FOX_HEADS_PER_STEP * pi + e
        pick = (row % N_HEADS == h) & (row < N_HEADS * N_BIAS_PIECES)
        sel = jnp.where(pick, 1.0, 0.0).astype(BF16)
        mine = qt_ref[0, e * HEAD_DIM:(e + 1) * HEAD_DIM, :]
        top = [mine, zeros] if e % 2 == 0 else [zeros, mine]
        qx.append(jnp.concatenate(top + [sel], axis=0))
        m_sc[e] = jnp.full((1, t), MASK_VALUE, F32)
        acc_sc[e] = jnp.zeros((ACC_ROWS, t), F32)

    tk = t // 2

    def scores(jt, hf, e, diag=False):
        k0 = pl.multiple_of(jt * t + hf * tk, tk)
        pair = slice((e // 2) * LANES, (e // 2 + 1) * LANES)
        kx = jnp.concatenate([k_ref[pl.ds(k0, tk), pair], kb_ref[pl.ds(k0, tk), :]], axis=1)
        q0 = hf * tk if diag else 0
        st_sc[e, hf, :, q0:] = _dot(kx, qx[e][:, q0:])

    def consume(jt, hf, e, diag):
        q0 = hf * tk if diag else 0
        nq = t - q0
        st = st_sc[e, hf, :, q0:]
        if diag:
            ok = (lax.broadcasted_iota(jnp.int32, (tk, nq), 0)
                  <= lax.broadcasted_iota(jnp.int32, (tk, nq), 1) + (q0 - hf * tk))
            st = jnp.where(ok, st, MASK_VALUE)
        m_old = m_sc[e, :, q0:]
        m_new = jnp.maximum(m_old, jnp.max(st, axis=0, keepdims=True))
        pt = jnp.exp2(st - m_new).astype(BF16)
        alpha = jnp.exp2(m_old - m_new)
        vt = vt_ref[jt, e * HEAD_DIM:(e + 1) * HEAD_DIM, hf * tk:(hf + 1) * tk]
        acc_sc[e, :, q0:] = alpha * acc_sc[e, :, q0:] + _dot(_with_ones(vt), pt)
        m_sc[e, :, q0:] = m_new

    for e in heads:
        scores(0, 0, e)

    def body(jt):
        for e in heads:
            scores(jt, 1, e)
            consume(jt, 0, e, False)
        for e in heads:
            scores(jt + 1, 0, e)
            consume(jt, 1, e, False)

    _loop_in_pairs(i, body)
    for e in heads:
        scores(i, 1, e, diag=True)
        consume(i, 0, e, True)
    for e in heads:
        consume(i, 1, e, True)
    for p in range(FOX_HEADS_PER_STEP // 2):
        outs = []
        for e in (2 * p, 2 * p + 1):
            acc = acc_sc[e]
            outs.append(acc[:HEAD_DIM] / jnp.maximum(acc[HEAD_DIM:HEAD_DIM + 1], 1e-30))
        sl = slice(p * LANES, (p + 1) * LANES)
        out = jnp.concatenate(outs, axis=0).T
        o_ref[:, sl] = (out * _silu(gate_ref[:, sl].astype(F32))).astype(o_ref.dtype)


def _fox_attention(qt, k, kb, vt, gate):
    b, s, _ = k.shape
    t = FOX_T
    assert s % t == 0 and vt.shape == (b, s // t, MIX_WIDTH, t) and qt.shape == vt.shape
    tile = lambda bi, p, i: (bi, i, p)
    nh = FOX_HEADS_PER_STEP
    wd = nh * HEAD_DIM
    return pl.pallas_call(
        _fox_kernel,
        grid=(b, N_HEADS // nh, s // t),
        in_specs=[pl.BlockSpec((None, 1, wd, t), lambda bi, p, i: (bi, i, p, 0)),
                  pl.BlockSpec((None, s, wd), lambda bi, p, i: (bi, 0, p)),
                  pl.BlockSpec((None, s, LANES), lambda bi, p, i: (bi, 0, 0)),
                  pl.BlockSpec((None, s // t, wd, t), lambda bi, p, i: (bi, 0, p, 0)),
                  pl.BlockSpec((None, t, wd), tile)],
        out_specs=pl.BlockSpec((None, t, wd), tile),
        out_shape=jax.ShapeDtypeStruct((b, s, MIX_WIDTH), BF16),
        scratch_shapes=[pltpu.VMEM((nh, 1, t), F32), pltpu.VMEM((nh, ACC_ROWS, t), F32),
                        pltpu.VMEM((nh, 2, t // 2, t), F32)],
        compiler_params=_cparams(("arbitrary", "arbitrary", "arbitrary")),
        name="fox_attn",
    )(qt, k, kb, vt, gate)


def _gelu_tanh(x):
    return 0.5 * x * (1.0 + jnp.tanh(math.sqrt(2.0 / math.pi) * (x + 0.044715 * (x * x * x))))


def _nsa_compress_kernel(ak_ref, av_ref, pe_ref, wa_ref, wb_ref, w2_ref, w2t_ref, c_ref, s_ref,
                         kc_ref, vct_ref):
    nrow = ak_ref.shape[0]
    for idx, a_ref in enumerate((ak_ref, av_ref)):
        a = a_ref[...].astype(F32)
        xa = (a + pe_ref[idx, 0:1, :]).astype(BF16)
        xb = (a + pe_ref[idx, 1:2, :]).astype(BF16)
        ya = _dot(xa, wa_ref[idx])
        yb = _dot(xb, wb_ref[idx])
        hid = _gelu_tanh(ya + pltpu.roll(yb, nrow - 1, 0)).astype(BF16)
        if idx == 0:
            y = _dot(hid, w2_ref[...])
            kc_ref[...] = _apply_rope(y, c_ref[...], s_ref[...]).astype(kc_ref.dtype)
        else:
            vct_ref[...] = _dot_nt(w2t_ref[...], hid).astype(vct_ref.dtype)


def _nsa_compress(kc, vc, pe_k, w1_k, w2_k, pe_v, w1_v, w2_v, rope_c, rope_s):
    b, s, _ = kc.shape
    ns = s // CMP_STRIDE
    g = NSA_KV
    flat = CMP_STRIDE * g * HEAD_DIM

    def w1_halves(w1):
        w1r = w1.reshape(2, CMP_STRIDE, HEAD_DIM, CMP_HIDDEN)
        outs = []
        for hf in range(2):
            z = jnp.einsum('ldc,gh->lgdhc', w1r[hf], jnp.eye(g, dtype=F32))
            outs.append(z.reshape(flat, g * CMP_HIDDEN))
        return outs

    def pe_halves(pe):
        per = pe.reshape(2, CMP_STRIDE, 1, HEAD_DIM)
        return jnp.broadcast_to(per, (2, CMP_STRIDE, g, HEAD_DIM)).reshape(2, flat)

    def w2_bd(w2):
        z = jnp.einsum('cd,gh->gchd', w2, jnp.eye(g, dtype=F32))
        return z.reshape(g * CMP_HIDDEN, g * HEAD_DIM)

    ka, kb = w1_halves(w1_k)
    va, vb = w1_halves(w1_v)
    wa = jnp.stack([ka, va]).astype(BF16)
    wb = jnp.stack([kb, vb]).astype(BF16)
    w2 = w2_bd(w2_k).astype(BF16)
    w2t = w2_bd(w2_v).T.astype(BF16)
    pe = jnp.stack([pe_halves(pe_k), pe_halves(pe_v)]).astype(F32)
    whole = lambda bi: (0, 0, 0)
    per_b = lambda bi: (bi, 0, 0)
    return pl.pallas_call(
        _nsa_compress_kernel,
        grid=(b,),
        in_specs=[pl.BlockSpec((None, ns, flat), per_b),
                  pl.BlockSpec((None, ns, flat), per_b),
                  pl.BlockSpec((2, 2, flat), whole),
                  pl.BlockSpec((2, flat, g * CMP_HIDDEN), whole),
                  pl.BlockSpec((2, flat, g * CMP_HIDDEN), whole),
                  pl.BlockSpec((g * CMP_HIDDEN, LANES), lambda bi: (0, 0)),
                  pl.BlockSpec((LANES, g * CMP_HIDDEN), lambda bi: (0, 0)),
                  pl.BlockSpec((None, ns, LANES), per_b),
                  pl.BlockSpec((None, ns, LANES), per_b)],
        out_specs=[pl.BlockSpec((None, ns, LANES), per_b),
                   pl.BlockSpec((None, LANES, ns), per_b)],
        out_shape=[jax.ShapeDtypeStruct((b, ns, LANES), BF16),
                   jax.ShapeDtypeStruct((b, LANES, ns), BF16)],
        compiler_params=_cparams(("arbitrary",)),
        name="nsa_compress",
    )(kc.reshape(b, ns, flat), vc.reshape(b, ns, flat), pe, wa, wb, w2, w2t,
      rope_c.reshape(b, ns, LANES), rope_s.reshape(b, ns, LANES))


NSA_QB = 2
NSA_TK = 256
NSA_REP = N_HEADS // NSA_KV


def _nsa_kernel(n_cmp, n_sel, n_win, *refs):
    n_wblk = n_win + NSA_QB - 1
    q_ref, kc_ref, vct_ref, ks_ref, vst_ref = refs[:5]
    kw_refs = refs[5:5 + n_wblk]
    vwt_refs = refs[5 + n_wblk:5 + 2 * n_wblk]
    (gl_ref, gate_ref, ovt_ref, blk_ref, wout_ref, x_ref, fg_ref,
     o_ref) = refs[5 + 2 * n_wblk:13 + 2 * n_wblk]
    st_sc, stc_sc, stw_sc, m_sc, acc_sc, ocmp_sc, owin_sc = refs[13 + 2 * n_wblk:]

    n = pl.program_id(1)
    tq, tk, rep = BLK, NSA_TK, NSA_REP
    streams = [(qb, g) for qb in range(NSA_QB) for g in range(NSA_KV)]
    ncp = kc_ref.shape[0]
    t_lane = [(n * NSA_QB + qb) * tq + lax.broadcasted_iota(jnp.int32, (1, tq), 1)
              for qb in range(NSA_QB)]
    zeros = jnp.zeros((HEAD_DIM, tq), BF16)
    qg = []
    for qb, g in streams:
        tiles = []
        for r in range(rep):
            h = g * rep + r
            mine = q_ref[qb, h * HEAD_DIM:(h + 1) * HEAD_DIM, :]
            tiles.append(jnp.concatenate([mine, zeros] if g % 2 == 0 else [zeros, mine], axis=0))
        qg.append(jnp.concatenate(tiles, axis=1))

    ci = lax.broadcasted_iota(jnp.int32, (ncp, tq), 0)
    kwn = n_win * BLK
    krow = lax.broadcasted_iota(jnp.int32, (kwn, tq), 0) - (n_win - 1) * BLK
    dist = lax.broadcasted_iota(jnp.int32, (kwn, tq), 1) - krow
    band = (dist >= 0) & (dist <= NSA_WINDOW - 1)
    sel_rows = ovt_ref.shape[0]
    rowi = lax.broadcasted_iota(jnp.int32, (sel_rows, tq), 0)
    rowf = rowi.astype(F32)

    def cmp_scores(sg):
        stc_sc[sg] = _dot(kc_ref[...], qg[sg])

    def cmp_consume(sg):
        qb, g = streams[sg]
        cmask = (ci * CMP_STRIDE + (CMP_LEN - 1) <= t_lane[qb]) & (ci < n_cmp)
        ps = []
        for r in range(rep):
            s_r = jnp.where(cmask, stc_sc[sg, :, r * tq:(r + 1) * tq], MASK_VALUE)
            m = jnp.max(s_r, axis=0, keepdims=True)
            pr = jnp.where(cmask, jnp.exp2(s_r - m), 0.0)
            ps.append(pr.astype(BF16))
        lhs = jnp.concatenate([_with_ones(vct_ref[g * HEAD_DIM:(g + 1) * HEAD_DIM, :]),
                               ovt_ref[...]], axis=0)
        res = _dot(lhs, jnp.concatenate(ps, axis=1))
        inv = 1.0 / jnp.maximum(res[HEAD_DIM:HEAD_DIM + 1], 1e-30)
        ocmp_sc[sg] = res[:HEAD_DIM] * inv
        imp = None
        for r in range(rep):
            ql = slice(r * tq, (r + 1) * tq)
            t = res[ACC_ROWS:, ql] * inv[:, ql]
            imp = t if imp is None else imp + t
        return imp

    def select(sg, imp):
        cur = t_lane[streams[sg][0]] // SEL_LEN
        forced = (rowi == 0) | (rowi == cur)
        causal = rowi <= cur
        score = jnp.where(causal, jnp.where(forced, FORCED_SCORE, imp), MASK_VALUE)
        score = jnp.where(rowi < n_sel, score, PAD_SCORE)
        chosen = jnp.zeros((sel_rows, tq), F32)
        for _ in range(min(SEL_TOPK, n_sel)):
            mx = jnp.max(score, axis=0, keepdims=True)
            first = jnp.min(jnp.where(score == mx, rowf, float(LANES)), axis=0, keepdims=True)
            hit = rowf == first
            chosen = jnp.where(hit, 1.0, chosen)
            score = jnp.where(hit, PAD_SCORE, score)
        bias_t = jnp.where(causal & (chosen > 0.5), 0.0, MASK_VALUE)
        bias_t = jnp.concatenate([bias_t, jnp.zeros((LANES - sel_rows, tq), F32)], axis=0)
        bias_t = bias_t.astype(BF16)
        return jnp.concatenate([qg[sg], jnp.concatenate([bias_t] * rep, axis=1)], axis=0)

    def win_scores(sg):
        qb = streams[sg][0]
        kwcat = jnp.concatenate([kr[...] for kr in kw_refs[qb:qb + n_win]], axis=0)
        stw_sc[sg] = _dot(kwcat, qg[sg])

    def win_consume(sg):
        qb, g = streams[sg]
        wok = band & ((n * NSA_QB + qb) * BLK + krow >= 0)
        vwt = jnp.concatenate([vr[0] for vr in vwt_refs[qb:qb + n_win]], axis=1)
        pts = []
        for r in range(rep):
            s_r = jnp.where(wok, stw_sc[sg, :, r * tq:(r + 1) * tq], MASK_VALUE)
            m = jnp.max(s_r, axis=0, keepdims=True)
            pts.append(jnp.exp2(s_r - m).astype(BF16))
        owin_sc[sg] = _dot(_with_ones(vwt[g * HEAD_DIM:(g + 1) * HEAD_DIM, :]),
                           jnp.concatenate(pts, axis=1))

    ns = len(streams)
    cmp_scores(0)
    cmp_scores(1)
    imps = []
    for sg in range(ns):
        if sg + 2 < ns:
            cmp_scores(sg + 2)
        else:
            win_scores(sg + 2 - ns)
        imps.append(cmp_consume(sg))
    qsel = []
    for sg in range(ns):
        if sg + 2 < ns:
            win_scores(sg + 2)
        qsel.append(select(sg, imps[sg]))
        win_consume(sg)

    for sg in range(ns):
        m_sc[sg] = jnp.full((1, rep * tq), MASK_VALUE, F32)
        acc_sc[sg] = jnp.zeros((ACC_ROWS, rep * tq), F32)

    hrep = rep // 2
    wide = hrep * tq
    units = [(sg, hf) for sg in range(ns) for hf in range(2)]
    nu = len(units)

    def scores(c, u):
        sg, hf = units[u]
        k0 = pl.multiple_of(c * tk, tk)
        kx = jnp.concatenate([ks_ref[pl.ds(k0, tk), :], blk_ref[pl.ds(k0, tk), :]], axis=1)
        st_sc[u] = _dot(kx, qsel[sg][:, hf * wide:(hf + 1) * wide])

    def consume(c, u, last):
        sg, hf = units[u]
        qb, g = streams[sg]
        lanes = slice(hf * wide, (hf + 1) * wide)
        if last:
            tok_ok = c * tk + lax.broadcasted_iota(jnp.int32, (tk, tq), 0) <= t_lane[qb]
        m_old = m_sc[sg, :, lanes]
        ms, pts = [], []
        for r in range(hrep):
            s_r = st_sc[u, :, r * tq:(r + 1) * tq]
            if last:
                s_r = jnp.where(tok_ok, s_r, MASK_VALUE)
            m_new = jnp.maximum(m_old[:, r * tq:(r + 1) * tq], jnp.max(s_r, axis=0, keepdims=True))
            pts.append(jnp.exp2(s_r - m_new).astype(BF16))
            ms.append(m_new)
        m_new = jnp.concatenate(ms, axis=1)
        alpha = jnp.exp2(m_old - m_new)
        vt = vst_ref[c, g * HEAD_DIM:(g + 1) * HEAD_DIM, :]
        acc_sc[sg, :, lanes] = (alpha * acc_sc[sg, :, lanes]
                                + _dot(_with_ones(vt), jnp.concatenate(pts, axis=1)))
        m_sc[sg, :, lanes] = m_new

    n_steps = ((n * NSA_QB + 1) * tq + tk - 1) // tk
    scores(0, 0)
    scores(0, 1)

    def body(c):
        for u in range(nu):
            if u + 2 < nu:
                scores(c, u + 2)
            else:
                scores(c + 1, u + 2 - nu)
            consume(c, u, False)

    _loop_in_pairs(n_steps - 1, body)
    for u in range(nu):
        if u + 2 < nu:
            scores(n_steps - 1, u + 2)
        consume(n_steps - 1, u, True)

    for qb in range(NSA_QB):
        tok = slice(qb * tq, (qb + 1) * tq)
        gate_t = (1.0 / (1.0 + jnp.exp(-gl_ref[tok, :]))).T
        pairs = []
        for p in range(N_PAIRS):
            sl = slice(p * LANES, (p + 1) * LANES)
            g = (2 * p) // rep
            sg = qb * NSA_KV + g
            rows = []
            for e in range(2):
                h = 2 * p + e
                ql = slice((h - g * rep) * tq, (h - g * rep + 1) * tq)
                slc, win = acc_sc[sg][:, ql], owin_sc[sg][:, ql]
                branches = (ocmp_sc[sg][:, ql],
                            slc[:HEAD_DIM] / jnp.maximum(slc[HEAD_DIM:HEAD_DIM + 1], 1e-30),
                            win[:HEAD_DIM] / win[HEAD_DIM:HEAD_DIM + 1])
                out = None
                for j, br in enumerate(branches):
                    t = gate_t[3 * h + j:3 * h + j + 1, :] * br
                    out = t if out is None else out + t
                rows.append(out)
            pair = jnp.concatenate(rows, axis=0).T
            pairs.append((pair * _silu(gate_ref[tok, sl].astype(F32))).astype(BF16))
        y = x_ref[tok, :] + _dot(jnp.concatenate(pairs, axis=1), wout_ref[...])
        var = jnp.mean(y * y, axis=-1, keepdims=True)
        o_ref[tok, :] = y * lax.rsqrt(var + NORM_EPS) * fg_ref[...]


def _selection_overlap_t(n_cmp_pad, n_cmp, n_sel):
    cs = np.arange(n_cmp_pad) * CMP_STRIDE
    js = np.arange(LANES) * SEL_LEN
    ov = np.minimum(cs[None, :] + CMP_LEN, js[:, None] + SEL_LEN) - np.maximum(cs[None, :], js[:, None])
    ov = (np.clip(ov, 0, None) / CMP_LEN).astype(np.float32)
    ov[:, n_cmp:] = 0.0
    ov[n_sel:, :] = 0.0
    return ov


def _nsa_attention(q_t, k_cmp, v_cmp_t, ks, vs_t, kw, vw_t, g_logit, gate, w_out, x, final_gain):
    b, s, d = x.shape
    tq = NSA_QB * BLK
    ncp = k_cmp.shape[1]
    n_cmp = s // CMP_STRIDE - 1
    n_sel = s // SEL_LEN
    n_win = -(-(NSA_WINDOW - 1) // BLK) + 1
    assert n_sel <= LANES and s % NSA_TK == 0 and tq == NSA_TK
    sel_rows = -(-n_sel // BF16_SUBLANES) * BF16_SUBLANES
    ov_t = jnp.asarray(_selection_overlap_t(ncp, n_cmp, n_sel)[:sel_rows], BF16)
    row = lambda bi, i: (bi, i, 0)
    per_b = lambda bi, i: (bi, 0, 0)
    in_specs = [pl.BlockSpec((None, NSA_QB, MIX_WIDTH, BLK), lambda bi, i: (bi, i, 0, 0)),
                pl.BlockSpec((None, ncp, LANES), per_b),
                pl.BlockSpec((None, LANES, ncp), per_b),
                pl.BlockSpec((None, s, LANES), per_b),
                pl.BlockSpec((None, s // NSA_TK, LANES, NSA_TK), lambda bi, i: (bi, 0, 0, 0))]
    args = [q_t, k_cmp, v_cmp_t, ks, vs_t]
    for j in range(n_win - 1, -NSA_QB, -1):
        in_specs.append(pl.BlockSpec(
            (None, BLK, LANES), lambda bi, i, j=j: (bi, jnp.maximum(NSA_QB * i - j, 0), 0)))
        args.append(kw)
    for j in range(n_win - 1, -NSA_QB, -1):
        in_specs.append(pl.BlockSpec(
            (None, 1, LANES, BLK), lambda bi, i, j=j: (bi, jnp.maximum(NSA_QB * i - j, 0), 0, 0)))
        args.append(vw_t)
    block_onehot = jnp.asarray(
        (np.arange(s)[:, None] // SEL_LEN == np.arange(LANES)[None, :]).astype(np.float32), BF16)
    in_specs += [pl.BlockSpec((None, tq, LANES), row),
                 pl.BlockSpec((None, tq, MIX_WIDTH), row),
                 pl.BlockSpec((sel_rows, ncp), lambda bi, i: (0, 0)),
                 pl.BlockSpec((s, LANES), lambda bi, i: (0, 0)),
                 pl.BlockSpec((MIX_WIDTH, d), lambda bi, i: (0, 0), pipeline_mode=pl.Buffered(1)),
                 pl.BlockSpec((None, tq, d), row),
                 pl.BlockSpec((1, d), lambda bi, i: (0, 0))]
    args += [g_logit, gate, ov_t, block_onehot, w_out.astype(BF16), x, final_gain[None, :]]
    wide = NSA_REP * BLK
    ns = NSA_QB * NSA_KV
    return pl.pallas_call(
        functools.partial(_nsa_kernel, n_cmp, n_sel, n_win),
        grid=(b, s // tq),
        in_specs=in_specs,
        out_specs=pl.BlockSpec((None, tq, d), row),
        out_shape=jax.ShapeDtypeStruct((b, s, d), F32),
        scratch_shapes=[pltpu.VMEM((2 * ns, NSA_TK, wide // 2), F32),
                        pltpu.VMEM((ns, ncp, wide), F32),
                        pltpu.VMEM((ns, n_win * BLK, wide), F32),
                        pltpu.VMEM((ns, 1, wide), F32),
                        pltpu.VMEM((ns, ACC_ROWS, wide), F32),
                        pltpu.VMEM((ns, HEAD_DIM, wide), F32),
                        pltpu.VMEM((ns, ACC_ROWS, wide), F32)],
        compiler_params=_cparams(("arbitrary", "arbitrary")),
        name="nsa_attn",
    )(*args)


def _split(w, sizes):
    offs = np.cumsum([0] + list(sizes))
    return [w[:, int(offs[i]):int(offs[i + 1])] for i in range(len(sizes))]


def _swa_layer(x, rope, gain, w_in, sinks, w_out):
    kvw = SWA_KV * HEAD_DIM
    parts = _split(w_in, [MIX_WIDTH, kvw, kvw, MIX_WIDTH])
    segs = [_Seg(MIX_WIDTH, rope=True, scale=Q_SCALE), _Seg(kvw, rope=True), _Seg(kvw),
            _Seg(MIX_WIDTH)]
    q, k, v, gate = _norm_proj(x, gain, parts, segs, rope)
    o = _banded_attention(q, k, v, SWA_WINDOW - 1, sinks=sinks, gate=gate)
    return o, w_out


def _dilated_layer(x, rope, gain, w_in, w_out, residual=None):
    b, s, _ = x.shape
    kvw = DIL_KV * HEAD_DIM
    sizes, segs = [], []
    for window, dil in DIL_PATTERNS:
        assert s % (dil * BLK) == 0
        sizes += [MIX_WIDTH, kvw, kvw]
        segs += [_Seg(MIX_WIDTH, rope=True, scale=Q_SCALE, dil=dil), _Seg(kvw, rope=True, dil=dil),
                 _Seg(kvw, dil=dil)]
    sizes.append(MIX_WIDTH)
    segs.append(_Seg(MIX_WIDTH))
    res = _norm_proj(x, gain, _split(w_in, sizes), segs, rope, residual)
    if residual is not None:
        x, res = res[0], res[1:]
    gate = res[-1]
    os_, lses, dils = [], [], []
    for gi, (window, dil) in enumerate(DIL_PATTERNS):
        q, k, v = res[3 * gi:3 * gi + 3]
        if dil > 1:
            q, k, v = (t.reshape(b * dil, s // dil, t.shape[-1]) for t in (q, k, v))
        o, lse = _banded_attention(q, k, v, window // dil, want_lse=True)
        if dil > 1:
            o = o.reshape(b, dil, s // dil, MIX_WIDTH)
            lse = lse.reshape(b, dil, s // dil, LANES)
        os_.append(o)
        lses.append(lse)
        dils.append(dil)
    return _dil_combine_out_proj(os_, lses, gate, tuple(dils), w_out, x)


def _fox_layer(x, gain, w_in, b_f, w_out):
    sizes = [MIX_WIDTH, MIX_WIDTH, MIX_WIDTH, N_HEADS, MIX_WIDTH]
    segs = [_Seg(MIX_WIDTH, scale=Q_SCALE, tile=FOX_T), _Seg(MIX_WIDTH),
            _Seg(MIX_WIDTH, tile=FOX_T), _Seg(LANES, dtype=F32), _Seg(MIX_WIDTH)]
    qt, k, vt, f_logit, gate = _norm_proj(x, gain, _split(w_in, sizes), segs)
    kb = _fox_decay(f_logit, b_f)
    return _fox_attention(qt, k, kb, vt, gate), w_out


def _nsa_layer(x, positions, rope, gain, w_in, pe_k, w1_k, w2_k, pe_v, w1_v, w2_v, w_out,
               residual, final_gain):
    b, s, _ = x.shape
    kvw = NSA_KV * HEAD_DIM
    sizes = [MIX_WIDTH] + [kvw] * 6 + [3 * N_HEADS, MIX_WIDTH]
    segs = [_Seg(MIX_WIDTH, rope=True, scale=Q_SCALE, tile=BLK), _Seg(kvw), _Seg(kvw),
            _Seg(kvw, rope=True), _Seg(kvw, tile=NSA_TK), _Seg(kvw, rope=True), _Seg(kvw, tile=BLK),
            _Seg(LANES, dtype=F32), _Seg(MIX_WIDTH)]
    x, q, kc, vc, ks, vs_t, kw, vw_t, g_logit, gate = _norm_proj(
        x, gain, _split(w_in, sizes), segs, rope, residual)
    ns = s // CMP_STRIDE
    cmp_pos = jnp.concatenate(
        [positions[:, CMP_LEN - 1::CMP_STRIDE], positions[:, -1:]], axis=1)[:, :ns]
    cmp_c, cmp_s = _rope_tables(cmp_pos.reshape(-1))
    k_cmp, v_cmp_t = _nsa_compress(kc, vc, pe_k, w1_k, w2_k, pe_v, w1_v, w2_v, cmp_c, cmp_s)
    return _nsa_attention(q, k_cmp, v_cmp_t, ks, vs_t, kw, vw_t, g_logit, gate, w_out, x,
                          final_gain)


def kernel(x, positions, norm_0, w_in_0, sinks_0, w_out_0, norm_1, w_in_1, w_out_1, norm_2, w_in_2, b_f_2, w_out_2, norm_3, w_in_3, cmp_pe_k_3, cmp_w1_k_3, cmp_w2_k_3, cmp_pe_v_3, cmp_w1_v_3, cmp_w2_v_3, w_out_3, final_norm):
    rope = _rope_tables(positions.reshape(-1))
    o, w = _swa_layer(x, rope, norm_0, w_in_0, sinks_0, w_out_0)
    x = _dilated_layer(x, rope, norm_1, w_in_1, w_out_1, residual=(o, w))
    o, w = _fox_layer(x, norm_2, w_in_2, b_f_2, w_out_2)
    return _nsa_layer(x, positions, rope, norm_3, w_in_3, cmp_pe_k_3, cmp_w1_k_3, cmp_w2_k_3,
                      cmp_pe_v_3, cmp_w1_v_3, cmp_w2_v_3, w_out_3, residual=(o, w),
                      final_gain=final_norm)
```

```python
import functools
import math

import numpy as np
import jax
import jax.numpy as jnp
from jax import lax
from jax.experimental import pallas as pl
from jax.experimental.pallas import tpu as pltpu

HEAD_DIM = 64
N_HEADS = 16
N_PAIRS = N_HEADS // 2
MIX_WIDTH = N_HEADS * HEAD_DIM
ROT_DIM = HEAD_DIM // 4
ROT_HALF = ROT_DIM // 2
ROPE_THETA = 500000.0
BLK = 128
LANES = 128
NORM_EPS = 1e-6
MASK_VALUE = -1e30
PAD_SCORE = -3e38
LOG2E = math.log2(math.e)
Q_SCALE = HEAD_DIM ** -0.5 * LOG2E

SWA_KV = 4
SWA_WINDOW = 128
DIL_KV = 4
DIL_PATTERNS = ((128, 1), (512, 4), (2048, 16))
NSA_KV = 2
CMP_LEN = 32
CMP_STRIDE = 16
CMP_HIDDEN = 256
SEL_LEN = 64
SEL_TOPK = 8
NSA_WINDOW = 256
FORCED_SCORE = 1e4

VMEM_LIMIT_BYTES = 56 * 1024 * 1024
PROJ_ROWS = 512
PROJ_COLS = 512
BF16_SUBLANES = 16
ACC_ROWS = HEAD_DIM + BF16_SUBLANES
ROPE_ROWS = 2048
F32 = jnp.float32
BF16 = jnp.bfloat16


def _cparams(sem):
    return pltpu.CompilerParams(dimension_semantics=sem, vmem_limit_bytes=VMEM_LIMIT_BYTES)


def _lane_half(shape):
    return lax.broadcasted_iota(jnp.int32, shape, 1) // HEAD_DIM


def _swap_halves(t):
    return jnp.concatenate([t[:, HEAD_DIM:], t[:, :HEAD_DIM]], axis=1)


def _head_query(qp, e, kv_half):
    qh = jnp.where(_lane_half(qp.shape) == e, qp, jnp.zeros_like(qp))
    if e != kv_half:
        qh = _swap_halves(qh)
    return qh


def _group_queries(q_ref, g, rep):
    tiles = []
    for r in range(rep):
        h = g * rep + r
        tiles.append(_head_query(q_ref[:, (h // 2) * LANES:(h // 2 + 1) * LANES], h % 2, g % 2))
    return jnp.concatenate(tiles, axis=0)


def _dot_nt(a, b):
    return lax.dot_general(a, b, (((1,), (1,)), ((), ())), preferred_element_type=F32)


def _dot(a, b):
    return jnp.dot(a, b, preferred_element_type=F32)


def _silu(x):
    return x * (1.0 / (1.0 + jnp.exp(-x)))


def _loop_in_pairs(count, body):
    odd = count % 2

    @pl.when(odd == 1)
    def _():
        body(0)

    def two(jj, carry):
        body(odd + 2 * jj)
        body(odd + 2 * jj + 1)
        return carry

    lax.fori_loop(0, count // 2, two, 0)


def _with_ones(vt):
    return jnp.concatenate([vt, jnp.ones((ACC_ROWS - HEAD_DIM, vt.shape[1]), BF16)], axis=0)


def _rope_table_kernel(pos_ref, inv_ref, c_ref, s_ref):
    pos = pos_ref[...].astype(F32)
    ang = pos * inv_ref[...]
    d = lax.broadcasted_iota(jnp.int32, ang.shape, 1) % HEAD_DIM
    cos = jnp.cos(ang)
    sin = jnp.sin(ang)
    c_ref[...] = jnp.where(d < ROT_DIM, cos, 1.0)
    s_ref[...] = jnp.where(d < ROT_HALF, -sin, jnp.where(d < ROT_DIM, sin, 0.0))


def _rope_tables(pos_flat):
    t = pos_flat.shape[0]
    rows = min(t, ROPE_ROWS)
    assert t % rows == 0
    inv = jnp.power(ROPE_THETA, -jnp.arange(ROT_HALF, dtype=F32) / ROT_HALF)
    inv_l = jnp.tile(inv, LANES // ROT_HALF)[None, :]
    out = jax.ShapeDtypeStruct((t, LANES), F32)
    return pl.pallas_call(
        _rope_table_kernel,
        grid=(t // rows,),
        in_specs=[pl.BlockSpec((rows, 1), lambda i: (i, 0)),
                  pl.BlockSpec((1, LANES), lambda i: (0, 0))],
        out_specs=[pl.BlockSpec((rows, LANES), lambda i: (i, 0))] * 2,
        out_shape=[out, out],
        compiler_params=_cparams(("arbitrary",)),
        name="rope_tables",
    )(pos_flat[:, None], inv_l)


def _apply_rope(y, c, s):
    outs = []
    for j in range(y.shape[1] // LANES):
        t = y[:, j * LANES:(j + 1) * LANES]
        d = lax.broadcasted_iota(jnp.int32, t.shape, 1) % HEAD_DIM
        partner = jnp.where(d < ROT_HALF, pltpu.roll(t, LANES - ROT_HALF, 1),
                            pltpu.roll(t, ROT_HALF, 1))
        outs.append(t * c + partner * s)
    return outs[0] if len(outs) == 1 else jnp.concatenate(outs, axis=1)


class _Seg:
    def __init__(self, width, rope=False, scale=None, dtype=BF16, dil=1, tile=None):
        self.width, self.rope, self.scale, self.dtype, self.dil = width, rope, scale, dtype, dil
        self.tile = tile
        assert not (tile and dil > 1)


def _norm_proj_kernel(segs, use_rope, has_residual, *refs):
    has_t = any(sg.tile and not sg.rope for sg in segs)
    x_ref, g_ref, w_ref = refs[:3]
    k = 3
    if has_t:
        wt_ref = refs[k]
        k += 1
    if use_rope:
        c_ref, s_ref = refs[k:k + 2]
        k += 2
    if has_residual:
        po_ref, pw_ref = refs[k:k + 2]
        k += 2
        xnew_ref = refs[k]
        k += 1
    out_refs = refs[k:k + len(segs)]
    stage_ref = refs[k + len(segs)] if any(sg.dil > 1 for sg in segs) else None

    x = x_ref[...]
    if has_residual:
        x = x + _dot(po_ref[...], pw_ref[...])
        xnew_ref[...] = x
    var = jnp.mean(x * x, axis=-1, keepdims=True)
    h = (x * lax.rsqrt(var + NORM_EPS) * g_ref[...]).astype(BF16)
    rows = x.shape[0]
    col = 0
    tcol = 0
    for sg, o_ref in zip(segs, out_refs):
        if sg.tile and not sg.rope:
            for c0 in range(0, sg.width, PROJ_COLS):
                cw = min(PROJ_COLS, sg.width - c0)
                yt = _dot_nt(wt_ref[tcol + c0:tcol + c0 + cw, :], h)
                if sg.scale is not None:
                    yt = yt * sg.scale
                for ti in range(rows // sg.tile):
                    o_ref[ti, c0:c0 + cw, :] = yt[:, ti * sg.tile:(ti + 1) * sg.tile].astype(sg.dtype)
            tcol += sg.width
            continue
        for c0 in range(0, sg.width, PROJ_COLS):
            cw = min(PROJ_COLS, sg.width - c0)
            y = _dot(h, w_ref[:, col + c0:col + c0 + cw])
            if sg.rope:
                y = _apply_rope(y, c_ref[...], s_ref[...])
            if sg.scale is not None:
                y = y * sg.scale
            if sg.tile:
                yt = y.T
                for ti in range(rows // sg.tile):
                    o_ref[ti, c0:c0 + cw, :] = yt[:, ti * sg.tile:(ti + 1) * sg.tile].astype(sg.dtype)
            elif sg.dil > 1:
                sub = rows // sg.dil
                for j in range(cw // LANES):
                    stage_ref[j] = y[:, j * LANES:(j + 1) * LANES]
                for r in range(sg.dil):
                    for j in range(cw // LANES):
                        lo = c0 + j * LANES
                        o_ref[r, :, lo:lo + LANES] = (
                            stage_ref[j, pl.ds(r, sub, stride=sg.dil), :].astype(sg.dtype))
            else:
                o_ref[:, c0:c0 + cw] = y.astype(sg.dtype)
        col += sg.width


def _norm_proj(x, gain, w_parts, segs, rope=None, residual=None):
    b, s, d = x.shape
    tm = PROJ_ROWS
    assert s % tm == 0
    w_cols, wt_rows = [], []
    for wp, sg in zip(w_parts, segs):
        if wp.shape[1] < sg.width:
            wp = jnp.pad(wp, ((0, 0), (0, sg.width - wp.shape[1])))
        if sg.tile and not sg.rope:
            wt_rows.append(wp.T)
        else:
            w_cols.append(wp)
    w = jnp.concatenate(w_cols, axis=1).astype(BF16)
    n = w.shape[1]
    use_rope = rope is not None
    once = dict(pipeline_mode=pl.Buffered(1))
    in_specs = [pl.BlockSpec((None, tm, d), lambda bi, i: (bi, i, 0)),
                pl.BlockSpec((1, d), lambda bi, i: (0, 0)),
                pl.BlockSpec((d, n), lambda bi, i: (0, 0), **once)]
    args = [x, gain[None, :], w]
    if wt_rows:
        wt = jnp.concatenate(wt_rows, axis=0).astype(BF16)
        in_specs.append(pl.BlockSpec(wt.shape, lambda bi, i: (0, 0), **once))
        args.append(wt)
    if use_rope:
        in_specs += [pl.BlockSpec((None, tm, LANES), lambda bi, i: (bi, i, 0))] * 2
        args += [rope[0].reshape(b, s, LANES), rope[1].reshape(b, s, LANES)]
    out_specs, out_shape = [], []
    if residual is not None:
        po, pw = residual
        in_specs += [pl.BlockSpec((None, tm, po.shape[-1]), lambda bi, i: (bi, i, 0)),
                     pl.BlockSpec(pw.shape, lambda bi, i: (0, 0), **once)]
        args += [po, pw.astype(BF16)]
        out_shape.append(jax.ShapeDtypeStruct((b, s, d), F32))
        out_specs.append(pl.BlockSpec((None, tm, d), lambda bi, i: (bi, i, 0)))
    for sg in segs:
        if sg.dil > 1:
            assert tm % sg.dil == 0
            out_shape.append(jax.ShapeDtypeStruct((b, sg.dil, s // sg.dil, sg.width), sg.dtype))
            out_specs.append(pl.BlockSpec((None, sg.dil, tm // sg.dil, sg.width),
                                          lambda bi, i: (bi, 0, i, 0)))
        elif sg.tile:
            assert tm % sg.tile == 0
            out_shape.append(jax.ShapeDtypeStruct((b, s // sg.tile, sg.width, sg.tile), sg.dtype))
            out_specs.append(pl.BlockSpec((None, tm // sg.tile, sg.width, sg.tile),
                                          lambda bi, i: (bi, i, 0, 0)))
        else:
            out_shape.append(jax.ShapeDtypeStruct((b, s, sg.width), sg.dtype))
            out_specs.append(pl.BlockSpec((None, tm, sg.width), lambda bi, i: (bi, i, 0)))
    scratch = ([pltpu.VMEM((PROJ_COLS // LANES, tm, LANES), F32)]
               if any(sg.dil > 1 for sg in segs) else [])
    return pl.pallas_call(
        functools.partial(_norm_proj_kernel, segs, use_rope, residual is not None),
        grid=(b, s // tm),
        in_specs=in_specs,
        out_specs=out_specs,
        out_shape=out_shape,
        scratch_shapes=scratch,
        compiler_params=_cparams(("arbitrary", "arbitrary")),
        name="norm_proj",
    )(*args)


BANDED_MAX_QB = 8


def _banded_kernel(n_seq, n_qb, max_dist, n_prev, n_kv, has_sink, has_gate, want_lse, *refs):
    q_ref = refs[0]
    kp_refs, kc_ref = refs[1:1 + n_prev], refs[1 + n_prev]
    vp_refs, vc_ref = refs[2 + n_prev:2 + 2 * n_prev], refs[2 + 2 * n_prev]

    def window(prev_refs, cur_ref, sq):
        return ([pr.at[sq] for pr in prev_refs]
                + [cur_ref.at[sq, a * BLK:(a + 1) * BLK] for a in range(n_qb)])

    k = 3 + 2 * n_prev
    sink_ref = gate_ref = lse_ref = None
    if has_sink:
        sink_ref = refs[k]; k += 1
    if has_gate:
        gate_ref = refs[k]; k += 1
    o_ref = refs[k]; k += 1
    if want_lse:
        lse_ref = refs[k]; k += 1
    ot_sc = refs[k]
    st_sc = refs[k + 1]

    n = pl.program_id(1)
    tq = BLK
    kw = (n_prev + 1) * BLK
    rep = N_HEADS // n_kv
    krow = lax.broadcasted_iota(jnp.int32, (kw, tq), 0) - n_prev * BLK
    dist = lax.broadcasted_iota(jnp.int32, (kw, tq), 1) - krow
    band = (dist >= 0) & (dist <= max_dist)
    lse_rows = [[] for _ in range(n_seq * n_qb)]
    units = [(sq, qb, g) for sq in range(n_seq) for qb in range(n_qb) for g in range(n_kv)]

    def scores(u):
        sq, qb, g = units[u]
        sl = slice((g // 2) * LANES, (g // 2 + 1) * LANES)
        k_refs = window(kp_refs, kc_ref, sq)
        kcat = jnp.concatenate([kr[:, sl] for kr in k_refs[qb:qb + n_prev + 1]], axis=0)
        st_sc[u] = _dot_nt(kcat, _group_queries(q_ref.at[sq, qb * tq:(qb + 1) * tq], g, rep))

    oks, vts = {}, {}

    def consume(u):
        sq, qb, g = units[u]
        vb = sq * n_qb + qb
        if qb not in oks:
            oks[qb] = band & ((n * n_qb + qb) * BLK + krow >= 0)
        ok = oks[qb]
        if (vb, g // 2) not in vts:
            sl = slice((g // 2) * LANES, (g // 2 + 1) * LANES)
            v_refs = window(vp_refs, vc_ref, sq)
            vcat = jnp.concatenate([vr[:, sl] for vr in v_refs[qb:qb + n_prev + 1]], axis=0)
            vts[vb, g // 2] = vcat.astype(F32).T.astype(BF16)
        vt = vts[vb, g // 2]
        ms, pts = [], []
        for r in range(rep):
            s_r = jnp.where(ok, st_sc[u, :, r * tq:(r + 1) * tq], MASK_VALUE)
            m = jnp.max(s_r, axis=0, keepdims=True)
            if has_sink:
                h = g * rep + r
                m = jnp.maximum(m, sink_ref[0:1, h:h + 1])
            pts.append(jnp.exp2(s_r - m).astype(BF16))
            ms.append(m)
        acc = _dot(_with_ones(vt[(g % 2) * HEAD_DIM:(g % 2 + 1) * HEAD_DIM, :]),
                   jnp.concatenate(pts, axis=1))
        for r in range(rep):
            h = g * rep + r
            den = acc[HEAD_DIM:HEAD_DIM + 1, r * tq:(r + 1) * tq]
            if has_sink:
                den = den + jnp.exp2(sink_ref[0:1, h:h + 1] - ms[r])
            ot_sc[vb * N_HEADS + h] = acc[:HEAD_DIM, r * tq:(r + 1) * tq] / den
            if want_lse:
                lse_rows[vb].append(ms[r] + jnp.log2(den))

    scores(0)
    scores(1)
    for u in range(len(units)):
        if u + 2 < len(units):
            scores(u + 2)
        consume(u)
    for vb in range(n_seq * n_qb):
        sq, qb = vb // n_qb, vb % n_qb
        rows = slice(qb * tq, (qb + 1) * tq)
        for p in range(N_PAIRS):
            sl = slice(p * LANES, (p + 1) * LANES)
            pair = jnp.concatenate([ot_sc[vb * N_HEADS + 2 * p],
                                    ot_sc[vb * N_HEADS + 2 * p + 1]], axis=0).T
            if has_gate:
                pair = pair * _silu(gate_ref[sq, rows, sl].astype(F32))
            o_ref[sq, rows, sl] = pair.astype(o_ref.dtype)
        if want_lse:
            lse_t = jnp.concatenate(
                lse_rows[vb] + [jnp.zeros((LANES - N_HEADS, tq), F32)], axis=0)
            lse_ref[sq, rows, :] = lse_t.T


def _banded_attention(q, k, v, max_dist, sinks=None, gate=None, want_lse=False):
    bq, sq, _ = q.shape
    kvw = k.shape[-1]
    n_kv = kvw // HEAD_DIM
    n_prev = -(-max_dist // BLK)
    n_qb = max(d for d in range(1, BANDED_MAX_QB + 1) if (sq // BLK) % d == 0)
    n_seq = max(d for d in range(1, BANDED_MAX_QB // n_qb + 1) if bq % d == 0)
    tq = n_qb * BLK
    assert sq % tq == 0
    row = lambda b, i: (b, i, 0)
    in_specs = [pl.BlockSpec((n_seq, tq, MIX_WIDTH), row)]
    args = [q]
    for arr in (k, v):
        for j in range(n_prev, 0, -1):
            in_specs.append(pl.BlockSpec(
                (n_seq, BLK, kvw), lambda b, i, j=j: (b, jnp.maximum(n_qb * i - j, 0), 0)))
            args.append(arr)
        in_specs.append(pl.BlockSpec((n_seq, tq, kvw), row))
        args.append(arr)
    if sinks is not None:
        in_specs.append(pl.BlockSpec((1, LANES), lambda b, i: (0, 0)))
        args.append(jnp.pad(sinks.astype(F32) * LOG2E, (0, LANES - N_HEADS))[None, :])
    if gate is not None:
        in_specs.append(pl.BlockSpec((n_seq, tq, MIX_WIDTH), row))
        args.append(gate)
    out_specs = [pl.BlockSpec((n_seq, tq, MIX_WIDTH), row)]
    out_shape = [jax.ShapeDtypeStruct((bq, sq, MIX_WIDTH), BF16)]
    if want_lse:
        out_specs.append(pl.BlockSpec((n_seq, tq, LANES), row))
        out_shape.append(jax.ShapeDtypeStruct((bq, sq, LANES), F32))
    n_vb = n_seq * n_qb
    res = pl.pallas_call(
        functools.partial(_banded_kernel, n_seq, n_qb, max_dist, n_prev, n_kv, sinks is not None,
                          gate is not None, want_lse),
        grid=(bq // n_seq, sq // tq),
        in_specs=in_specs,
        out_specs=out_specs,
        out_shape=out_shape,
        scratch_shapes=[pltpu.VMEM((n_vb * N_HEADS, HEAD_DIM, BLK), F32),
                        pltpu.VMEM((n_vb * n_kv, (n_prev + 1) * BLK,
                                    (N_HEADS // n_kv) * BLK), F32)],
        compiler_params=_cparams(("arbitrary", "arbitrary")),
        name="banded_attn",
    )(*args)
    return res if want_lse else res[0]


def _dil_combine_kernel(dils, *refs):
    ng = len(dils)
    o_refs, l_refs = refs[:ng], refs[ng:2 * ng]
    gate_ref, expand_ref, w_ref, x_ref, out_ref, stage_ref, lstage_ref = refs[2 * ng:]
    rows = out_ref.shape[0]

    def natural(ref, dil, stage, slab):
        sl = slice(slab * LANES, (slab + 1) * LANES)
        if dil == 1:
            return ref[:, sl].astype(F32)
        sub = rows // dil
        for r in range(dil):
            stage[pl.ds(r, sub, stride=dil), :] = ref[r, :, sl].astype(F32)
        return stage[...]

    lses = [natural(l_refs[i], dils[i], lstage_ref, 0) for i in range(ng)]
    mx = functools.reduce(jnp.maximum, lses)
    ws = [jnp.exp2(l - mx) for l in lses]
    tot = functools.reduce(lambda a, c: a + c, ws)
    inv = 1.0 / tot

    def widen(w):
        hi = w.astype(BF16)
        lo = (w - hi.astype(F32)).astype(BF16)
        return _dot(hi, expand_ref[...]) + _dot(lo, expand_ref[...])

    wide = [widen(w * inv) for w in ws[:-1]]
    wide.append(1.0 - functools.reduce(lambda a, c: a + c, wide))
    gated = []
    for p in range(N_PAIRS):
        sl = slice(p * LANES, (p + 1) * LANES)
        acc = None
        for gi in range(ng):
            t = wide[gi][:, sl] * natural(o_refs[gi], dils[gi], stage_ref, p)
            acc = t if acc is None else acc + t
        gated.append((acc * _silu(gate_ref[:, sl].astype(F32))).astype(BF16))
    out_ref[...] = x_ref[...] + _dot(jnp.concatenate(gated, axis=1), w_ref[...])


def _dil_combine_out_proj(os_, lses, gate, dils, w_out, x):
    b, s, d = x.shape
    tm = PROJ_ROWS
    expand = jnp.asarray((np.arange(LANES)[:, None] == np.arange(MIX_WIDTH)[None, :] // HEAD_DIM)
                         .astype(np.float32), BF16)
    in_specs, args = [], []
    for arrs, width in ((os_, MIX_WIDTH), (lses, LANES)):
        for arr, dil in zip(arrs, dils):
            if dil == 1:
                in_specs.append(pl.BlockSpec((None, tm, width), lambda bi, i: (bi, i, 0)))
            else:
                in_specs.append(pl.BlockSpec((None, dil, tm // dil, width),
                                             lambda bi, i: (bi, 0, i, 0)))
            args.append(arr)
    row = lambda bi, i: (bi, i, 0)
    in_specs += [pl.BlockSpec((None, tm, MIX_WIDTH), row),
                 pl.BlockSpec((LANES, MIX_WIDTH), lambda bi, i: (0, 0)),
                 pl.BlockSpec((MIX_WIDTH, d), lambda bi, i: (0, 0)),
                 pl.BlockSpec((None, tm, d), row)]
    args += [gate, expand, w_out.astype(BF16), x]
    return pl.pallas_call(
        functools.partial(_dil_combine_kernel, dils),
        grid=(b, s // tm),
        in_specs=in_specs,
        out_specs=pl.BlockSpec((None, tm, d), row),
        out_shape=jax.ShapeDtypeStruct((b, s, d), F32),
        scratch_shapes=[pltpu.VMEM((tm, LANES), F32), pltpu.VMEM((tm, LANES), F32)],
        compiler_params=_cparams(("arbitrary", "arbitrary")),
        name="dil_combine_out_proj",
    )(*args)


N_BIAS_PIECES = 3


def _fox_decay_kernel(f_ref, b_ref, kb_ref):
    s = f_ref.shape[0]
    x = f_ref[...] + b_ref[...]
    logf = jnp.minimum(x, 0.0) - jnp.log1p(jnp.exp(-jnp.abs(x)))
    r = lax.broadcasted_iota(jnp.int32, (BLK, BLK), 0)
    c = lax.broadcasted_iota(jnp.int32, (BLK, BLK), 1)
    tri = (c <= r).astype(F32)
    lane = lax.broadcasted_iota(jnp.int32, (BLK, LANES), 1)
    carry = jnp.zeros((1, LANES), F32)
    for i in range(s // BLK):
        blk = logf[i * BLK:(i + 1) * BLK, :]
        cs = jnp.dot(tri, blk, preferred_element_type=F32, precision=lax.Precision.HIGHEST) + carry
        carry = cs[BLK - 1:BLK, :]
        rest = cs * (-LOG2E)
        out = jnp.zeros((BLK, LANES), F32)
        for j in range(N_BIAS_PIECES):
            piece = rest.astype(BF16).astype(F32)
            rest = rest - piece
            moved = piece if j == 0 else pltpu.roll(piece, N_HEADS * j, 1)
            out = jnp.where((lane >= N_HEADS * j) & (lane < N_HEADS * (j + 1)), moved, out)
        kb_ref[i * BLK:(i + 1) * BLK, :] = out.astype(BF16)


def _fox_decay(f_logit, b_f):
    b, s, _ = f_logit.shape
    b_pad = jnp.pad(b_f.astype(F32), (0, LANES - N_HEADS))[None, :]
    return pl.pallas_call(
        _fox_decay_kernel,
        grid=(b,),
        in_specs=[pl.BlockSpec((None, s, LANES), lambda bi: (bi, 0, 0)),
                  pl.BlockSpec((1, LANES), lambda bi: (0, 0))],
        out_specs=pl.BlockSpec((None, s, LANES), lambda bi: (bi, 0, 0)),
        out_shape=jax.ShapeDtypeStruct((b, s, LANES), BF16),
        compiler_params=_cparams(("arbitrary",)),
        name="fox_decay",
    )(f_logit, b_pad)


FOX_T = PROJ_ROWS
FOX_HEADS_PER_STEP = 8


def _fox_kernel(qt_ref, k_ref, kb_ref, vt_ref, gate_ref, o_ref, m_sc, acc_sc, st_sc):
    pi = pl.program_id(1)
    i = pl.program_id(2)
    t = FOX_T
    heads = range(FOX_HEADS_PER_STEP)
    row = lax.broadcasted_iota(jnp.int32, (LANES, t), 0)
    zeros = jnp.zeros((HEAD_DIM, t), BF16)
    qx = []
    for e in heads:
        h = FOX_HEADS_PER_STEP * pi + e
        pick = (row % N_HEADS == h) & (row < N_HEADS * N_BIAS_PIECES)
        sel = jnp.where(pick, 1.0, 0.0).astype(BF16)
        mine = qt_ref[0, e * HEAD_DIM:(e + 1) * HEAD_DIM, :]
        top = [mine, zeros] if e % 2 == 0 else [zeros, mine]
        qx.append(jnp.concatenate(top + [sel], axis=0))
        m_sc[e] = jnp.full((1, t), MASK_VALUE, F32)
        acc_sc[e] = jnp.zeros((ACC_ROWS, t), F32)

    tk = t // 2

    def scores(jt, hf, e, diag=False):
        k0 = pl.multiple_of(jt * t + hf * tk, tk)
        pair = slice((e // 2) * LANES, (e // 2 + 1) * LANES)
        kx = jnp.concatenate([k_ref[pl.ds(k0, tk), pair], kb_ref[pl.ds(k0, tk), :]], axis=1)
        q0 = hf * tk if diag else 0
        st_sc[e, hf, :, q0:] = _dot(kx, qx[e][:, q0:])

    def consume(jt, hf, e, diag):
        q0 = hf * tk if diag else 0
        nq = t - q0
        st = st_sc[e, hf, :, q0:]
        if diag:
            ok = (lax.broadcasted_iota(jnp.int32, (tk, nq), 0)
                  <= lax.broadcasted_iota(jnp.int32, (tk, nq), 1) + (q0 - hf * tk))
            st = jnp.where(ok, st, MASK_VALUE)
        m_old = m_sc[e, :, q0:]
        m_new = jnp.maximum(m_old, jnp.max(st, axis=0, keepdims=True))
        pt = jnp.exp2(st - m_new).astype(BF16)
        alpha = jnp.exp2(m_old - m_new)
        vt = vt_ref[jt, e * HEAD_DIM:(e + 1) * HEAD_DIM, hf * tk:(hf + 1) * tk]
        acc_sc[e, :, q0:] = alpha * acc_sc[e, :, q0:] + _dot(_with_ones(vt), pt)
        m_sc[e, :, q0:] = m_new

    for e in heads:
        scores(0, 0, e)

    def body(jt):
        for e in heads:
            scores(jt, 1, e)
            consume(jt, 0, e, False)
        for e in heads:
            scores(jt + 1, 0, e)
            consume(jt, 1, e, False)

    _loop_in_pairs(i, body)
    for e in heads:
        scores(i, 1, e, diag=True)
        consume(i, 0, e, True)
    for e in heads:
        consume(i, 1, e, True)
    for p in range(FOX_HEADS_PER_STEP // 2):
        outs = []
        for e in (2 * p, 2 * p + 1):
            acc = acc_sc[e]
            outs.append(acc[:HEAD_DIM] / jnp.maximum(acc[HEAD_DIM:HEAD_DIM + 1], 1e-30))
        sl = slice(p * LANES, (p + 1) * LANES)
        out = jnp.concatenate(outs, axis=0).T
        o_ref[:, sl] = (out * _silu(gate_ref[:, sl].astype(F32))).astype(o_ref.dtype)


def _fox_attention(qt, k, kb, vt, gate):
    b, s, _ = k.shape
    t = FOX_T
    assert s % t == 0 and vt.shape == (b, s // t, MIX_WIDTH, t) and qt.shape == vt.shape
    tile = lambda bi, p, i: (bi, i, p)
    nh = FOX_HEADS_PER_STEP
    wd = nh * HEAD_DIM
    return pl.pallas_call(
        _fox_kernel,
        grid=(b, N_HEADS // nh, s // t),
        in_specs=[pl.BlockSpec((None, 1, wd, t), lambda bi, p, i: (bi, i, p, 0)),
                  pl.BlockSpec((None, s, wd), lambda bi, p, i: (bi, 0, p)),
                  pl.BlockSpec((None, s, LANES), lambda bi, p, i: (bi, 0, 0)),
                  pl.BlockSpec((None, s // t, wd, t), lambda bi, p, i: (bi, 0, p, 0)),
                  pl.BlockSpec((None, t, wd), tile)],
        out_specs=pl.BlockSpec((None, t, wd), tile),
        out_shape=jax.ShapeDtypeStruct((b, s, MIX_WIDTH), BF16),
        scratch_shapes=[pltpu.VMEM((nh, 1, t), F32), pltpu.VMEM((nh, ACC_ROWS, t), F32),
                        pltpu.VMEM((nh, 2, t // 2, t), F32)],
        compiler_params=_cparams(("arbitrary", "arbitrary", "arbitrary")),
        name="fox_attn",
    )(qt, k, kb, vt, gate)


def _gelu_tanh(x):
    return 0.5 * x * (1.0 + jnp.tanh(math.sqrt(2.0 / math.pi) * (x + 0.044715 * (x * x * x))))


def _nsa_compress_kernel(ak_ref, av_ref, pe_ref, wa_ref, wb_ref, w2_ref, w2t_ref, c_ref, s_ref,
                         kc_ref, vct_ref):
    nrow = ak_ref.shape[0]
    for idx, a_ref in enumerate((ak_ref, av_ref)):
        a = a_ref[...].astype(F32)
        xa = (a + pe_ref[idx, 0:1, :]).astype(BF16)
        xb = (a + pe_ref[idx, 1:2, :]).astype(BF16)
        ya = _dot(xa, wa_ref[idx])
        yb = _dot(xb, wb_ref[idx])
        hid = _gelu_tanh(ya + pltpu.roll(yb, nrow - 1, 0)).astype(BF16)
        if idx == 0:
            y = _dot(hid, w2_ref[...])
            kc_ref[...] = _apply_rope(y, c_ref[...], s_ref[...]).astype(kc_ref.dtype)
        else:
            vct_ref[...] = _dot_nt(w2t_ref[...], hid).astype(vct_ref.dtype)


def _nsa_compress(kc, vc, pe_k, w1_k, w2_k, pe_v, w1_v, w2_v, rope_c, rope_s):
    b, s, _ = kc.shape
    ns = s // CMP_STRIDE
    g = NSA_KV
    flat = CMP_STRIDE * g * HEAD_DIM

    def w1_halves(w1):
        w1r = w1.reshape(2, CMP_STRIDE, HEAD_DIM, CMP_HIDDEN)
        outs = []
        for hf in range(2):
            z = jnp.einsum('ldc,gh->lgdhc', w1r[hf], jnp.eye(g, dtype=F32))
            outs.append(z.reshape(flat, g * CMP_HIDDEN))
        return outs

    def pe_halves(pe):
        per = pe.reshape(2, CMP_STRIDE, 1, HEAD_DIM)
        return jnp.broadcast_to(per, (2, CMP_STRIDE, g, HEAD_DIM)).reshape(2, flat)

    def w2_bd(w2):
        z = jnp.einsum('cd,gh->gchd', w2, jnp.eye(g, dtype=F32))
        return z.reshape(g * CMP_HIDDEN, g * HEAD_DIM)

    ka, kb = w1_halves(w1_k)
    va, vb = w1_halves(w1_v)
    wa = jnp.stack([ka, va]).astype(BF16)
    wb = jnp.stack([kb, vb]).astype(BF16)
    w2 = w2_bd(w2_k).astype(BF16)
    w2t = w2_bd(w2_v).T.astype(BF16)
    pe = jnp.stack([pe_halves(pe_k), pe_halves(pe_v)]).astype(F32)
    whole = lambda bi: (0, 0, 0)
    per_b = lambda bi: (bi, 0, 0)
    return pl.pallas_call(
        _nsa_compress_kernel,
        grid=(b,),
        in_specs=[pl.BlockSpec((None, ns, flat), per_b),
                  pl.BlockSpec((None, ns, flat), per_b),
                  pl.BlockSpec((2, 2, flat), whole),
                  pl.BlockSpec((2, flat, g * CMP_HIDDEN), whole),
                  pl.BlockSpec((2, flat, g * CMP_HIDDEN), whole),
                  pl.BlockSpec((g * CMP_HIDDEN, LANES), lambda bi: (0, 0)),
                  pl.BlockSpec((LANES, g * CMP_HIDDEN), lambda bi: (0, 0)),
                  pl.BlockSpec((None, ns, LANES), per_b),
                  pl.BlockSpec((None, ns, LANES), per_b)],
        out_specs=[pl.BlockSpec((None, ns, LANES), per_b),
                   pl.BlockSpec((None, LANES, ns), per_b)],
        out_shape=[jax.ShapeDtypeStruct((b, ns, LANES), BF16),
                   jax.ShapeDtypeStruct((b, LANES, ns), BF16)],
        compiler_params=_cparams(("arbitrary",)),
        name="nsa_compress",
    )(kc.reshape(b, ns, flat), vc.reshape(b, ns, flat), pe, wa, wb, w2, w2t,
      rope_c.reshape(b, ns, LANES), rope_s.reshape(b, ns, LANES))


NSA_QB = 2
NSA_TK = 256
NSA_REP = N_HEADS // NSA_KV


def _nsa_kernel(n_cmp, n_sel, n_win, *refs):
    n_wblk = n_win + NSA_QB - 1
    q_ref, kc_ref, vct_ref, ks_ref, vst_ref = refs[:5]
    kw_refs = refs[5:5 + n_wblk]
    vwt_refs = refs[5 + n_wblk:5 + 2 * n_wblk]
    (gl_ref, gate_ref, ovt_ref, blk_ref, wout_ref, x_ref, fg_ref,
     o_ref) = refs[5 + 2 * n_wblk:13 + 2 * n_wblk]
    st_sc, stc_sc, stw_sc, m_sc, acc_sc, ocmp_sc, owin_sc, imp_sc = refs[13 + 2 * n_wblk:]

    n = pl.program_id(1)
    tq, tk, rep = BLK, NSA_TK, NSA_REP
    streams = [(qb, g) for qb in range(NSA_QB) for g in range(NSA_KV)]
    ncp = kc_ref.shape[0]
    t_lane = [(n * NSA_QB + qb) * tq + lax.broadcasted_iota(jnp.int32, (1, tq), 1)
              for qb in range(NSA_QB)]
    zeros = jnp.zeros((HEAD_DIM, tq), BF16)
    qg = []
    for qb, g in streams:
        tiles = []
        for r in range(rep):
            h = g * rep + r
            mine = q_ref[qb, h * HEAD_DIM:(h + 1) * HEAD_DIM, :]
            tiles.append(jnp.concatenate([mine, zeros] if g % 2 == 0 else [zeros, mine], axis=0))
        qg.append(jnp.concatenate(tiles, axis=1))

    ci = lax.broadcasted_iota(jnp.int32, (ncp, tq), 0)
    kwn = n_win * BLK
    krow = lax.broadcasted_iota(jnp.int32, (kwn, tq), 0) - (n_win - 1) * BLK
    dist = lax.broadcasted_iota(jnp.int32, (kwn, tq), 1) - krow
    band = (dist >= 0) & (dist <= NSA_WINDOW - 1)
    sel_rows = ovt_ref.shape[0]
    rowi = lax.broadcasted_iota(jnp.int32, (sel_rows, tq), 0)
    rowf = rowi.astype(F32)

    def cmp_scores(sg, nrows):
        stc_sc[sg, :nrows] = _dot(kc_ref[:nrows, :], qg[sg])

    def cmp_consume(sg, nrows):
        qb, g = streams[sg]
        cmask = (ci[:nrows] * CMP_STRIDE + (CMP_LEN - 1) <= t_lane[qb]) & (ci[:nrows] < n_cmp)
        ps = []
        for r in range(rep):
            s_r = jnp.where(cmask, stc_sc[sg, :nrows, r * tq:(r + 1) * tq], MASK_VALUE)
            m = jnp.max(s_r, axis=0, keepdims=True)
            pr = jnp.where(cmask, jnp.exp2(s_r - m), 0.0)
            ps.append(pr.astype(BF16))
        lhs = jnp.concatenate([_with_ones(vct_ref[g * HEAD_DIM:(g + 1) * HEAD_DIM, :nrows]),
                               ovt_ref[:, :nrows]], axis=0)
        res = _dot(lhs, jnp.concatenate(ps, axis=1))
        inv = 1.0 / jnp.maximum(res[HEAD_DIM:HEAD_DIM + 1], 1e-30)
        ocmp_sc[sg] = res[:HEAD_DIM] * inv
        imp = None
        for r in range(rep):
            ql = slice(r * tq, (r + 1) * tq)
            t = res[ACC_ROWS:, ql] * inv[:, ql]
            imp = t if imp is None else imp + t
        return imp

    def select(sg, imp):
        cur = t_lane[streams[sg][0]] // SEL_LEN
        forced = (rowi == 0) | (rowi == cur)
        causal = rowi <= cur
        score = jnp.where(causal, jnp.where(forced, FORCED_SCORE, imp), MASK_VALUE)
        score = jnp.where(rowi < n_sel, score, PAD_SCORE)
        chosen = jnp.zeros((sel_rows, tq), F32)
        for _ in range(min(SEL_TOPK, n_sel)):
            mx = jnp.max(score, axis=0, keepdims=True)
            first = jnp.min(jnp.where(score == mx, rowf, float(LANES)), axis=0, keepdims=True)
            hit = rowf == first
            chosen = jnp.where(hit, 1.0, chosen)
            score = jnp.where(hit, PAD_SCORE, score)
        bias_t = jnp.where(causal & (chosen > 0.5), 0.0, MASK_VALUE)
        bias_t = jnp.concatenate([bias_t, jnp.zeros((LANES - sel_rows, tq), F32)], axis=0)
        bias_t = bias_t.astype(BF16)
        return jnp.concatenate([qg[sg], jnp.concatenate([bias_t] * rep, axis=1)], axis=0)

    def win_scores(sg):
        qb = streams[sg][0]
        kwcat = jnp.concatenate([kr[...] for kr in kw_refs[qb:qb + n_win]], axis=0)
        stw_sc[sg] = _dot(kwcat, qg[sg])

    def win_consume(sg):
        qb, g = streams[sg]
        wok = band & ((n * NSA_QB + qb) * BLK + krow >= 0)
        vwt = jnp.concatenate([vr[0] for vr in vwt_refs[qb:qb + n_win]], axis=1)
        pts = []
        for r in range(rep):
            s_r = jnp.where(wok, stw_sc[sg, :, r * tq:(r + 1) * tq], MASK_VALUE)
            m = jnp.max(s_r, axis=0, keepdims=True)
            pts.append(jnp.exp2(s_r - m).astype(BF16))
        owin_sc[sg] = _dot(_with_ones(vwt[g * HEAD_DIM:(g + 1) * HEAD_DIM, :]),
                           jnp.concatenate(pts, axis=1))

    ns = len(streams)

    def cmp_phase(nrows):
        cmp_scores(0, nrows)
        cmp_scores(1, nrows)
        for sg in range(ns):
            if sg + 2 < ns:
                cmp_scores(sg + 2, nrows)
            imp_sc[sg] = cmp_consume(sg, nrows)

    half = ncp // 2
    few = (n + 1) * (NSA_QB * BLK // CMP_STRIDE) <= half
    pl.when(few)(lambda: cmp_phase(half))
    pl.when(jnp.logical_not(few))(lambda: cmp_phase(ncp))

    win_scores(0)
    win_scores(1)
    qsel = []
    for sg in range(ns):
        if sg + 2 < ns:
            win_scores(sg + 2)
        qsel.append(select(sg, imp_sc[sg]))
        win_consume(sg)

    for sg in range(ns):
        m_sc[sg] = jnp.full((1, rep * tq), MASK_VALUE, F32)
        acc_sc[sg] = jnp.zeros((ACC_ROWS, rep * tq), F32)

    hrep = rep // 2
    wide = hrep * tq
    units = [(sg, hf) for sg in range(ns) for hf in range(2)]
    nu = len(units)

    def scores(c, u):
        sg, hf = units[u]
        k0 = pl.multiple_of(c * tk, tk)
        kx = jnp.concatenate([ks_ref[pl.ds(k0, tk), :], blk_ref[pl.ds(k0, tk), :]], axis=1)
        st_sc[u] = _dot(kx, qsel[sg][:, hf * wide:(hf + 1) * wide])

    def consume(c, u, last):
        sg, hf = units[u]
        qb, g = streams[sg]
        lanes = slice(hf * wide, (hf + 1) * wide)
        if last:
            tok_ok = c * tk + lax.broadcasted_iota(jnp.int32, (tk, tq), 0) <= t_lane[qb]
        m_old = m_sc[sg, :, lanes]
        ms, pts = [], []
        for r in range(hrep):
            s_r = st_sc[u, :, r * tq:(r + 1) * tq]
            if last:
                s_r = jnp.where(tok_ok, s_r, MASK_VALUE)
            m_new = jnp.maximum(m_old[:, r * tq:(r + 1) * tq], jnp.max(s_r, axis=0, keepdims=True))
            pts.append(jnp.exp2(s_r - m_new).astype(BF16))
            ms.append(m_new)
        m_new = jnp.concatenate(ms, axis=1)
        alpha = jnp.exp2(m_old - m_new)
        vt = vst_ref[c, g * HEAD_DIM:(g + 1) * HEAD_DIM, :]
        acc_sc[sg, :, lanes] = (alpha * acc_sc[sg, :, lanes]
                                + _dot(_with_ones(vt), jnp.concatenate(pts, axis=1)))
        m_sc[sg, :, lanes] = m_new

    n_steps = ((n * NSA_QB + 1) * tq + tk - 1) // tk
    scores(0, 0)
    scores(0, 1)

    def body(c):
        for u in range(nu):
            if u + 2 < nu:
                scores(c, u + 2)
            else:
                scores(c + 1, u + 2 - nu)
            consume(c, u, False)

    _loop_in_pairs(n_steps - 1, body)
    for u in range(nu):
        if u + 2 < nu:
            scores(n_steps - 1, u + 2)
        consume(n_steps - 1, u, True)

    for qb in range(NSA_QB):
        tok = slice(qb * tq, (qb + 1) * tq)
        gate_t = (1.0 / (1.0 + jnp.exp(-gl_ref[tok, :]))).T
        pairs = []
        for p in range(N_PAIRS):
            sl = slice(p * LANES, (p + 1) * LANES)
            g = (2 * p) // rep
            sg = qb * NSA_KV + g
            rows = []
            for e in range(2):
                h = 2 * p + e
                ql = slice((h - g * rep) * tq, (h - g * rep + 1) * tq)
                slc, win = acc_sc[sg][:, ql], owin_sc[sg][:, ql]
                branches = (ocmp_sc[sg][:, ql],
                            slc[:HEAD_DIM] / jnp.maximum(slc[HEAD_DIM:HEAD_DIM + 1], 1e-30),
                            win[:HEAD_DIM] / win[HEAD_DIM:HEAD_DIM + 1])
                out = None
                for j, br in enumerate(branches):
                    t = gate_t[3 * h + j:3 * h + j + 1, :] * br
                    out = t if out is None else out + t
                rows.append(out)
            pair = jnp.concatenate(rows, axis=0).T
            pairs.append((pair * _silu(gate_ref[tok, sl].astype(F32))).astype(BF16))
        y = x_ref[tok, :] + _dot(jnp.concatenate(pairs, axis=1), wout_ref[...])
        var = jnp.mean(y * y, axis=-1, keepdims=True)
        o_ref[tok, :] = y * lax.rsqrt(var + NORM_EPS) * fg_ref[...]


def _selection_overlap_t(n_cmp_pad, n_cmp, n_sel):
    cs = np.arange(n_cmp_pad) * CMP_STRIDE
    js = np.arange(LANES) * SEL_LEN
    ov = np.minimum(cs[None, :] + CMP_LEN, js[:, None] + SEL_LEN) - np.maximum(cs[None, :], js[:, None])
    ov = (np.clip(ov, 0, None) / CMP_LEN).astype(np.float32)
    ov[:, n_cmp:] = 0.0
    ov[n_sel:, :] = 0.0
    return ov


def _nsa_attention(q_t, k_cmp, v_cmp_t, ks, vs_t, kw, vw_t, g_logit, gate, w_out, x, final_gain):
    b, s, d = x.shape
    tq = NSA_QB * BLK
    ncp = k_cmp.shape[1]
    n_cmp = s // CMP_STRIDE - 1
    n_sel = s // SEL_LEN
    n_win = -(-(NSA_WINDOW - 1) // BLK) + 1
    assert n_sel <= LANES and s % NSA_TK == 0 and tq == NSA_TK
    sel_rows = -(-n_sel // BF16_SUBLANES) * BF16_SUBLANES
    ov_t = jnp.asarray(_selection_overlap_t(ncp, n_cmp, n_sel)[:sel_rows], BF16)
    row = lambda bi, i: (bi, i, 0)
    per_b = lambda bi, i: (bi, 0, 0)
    in_specs = [pl.BlockSpec((None, NSA_QB, MIX_WIDTH, BLK), lambda bi, i: (bi, i, 0, 0)),
                pl.BlockSpec((None, ncp, LANES), per_b),
                pl.BlockSpec((None, LANES, ncp), per_b),
                pl.BlockSpec((None, s, LANES), per_b),
                pl.BlockSpec((None, s // NSA_TK, LANES, NSA_TK), lambda bi, i: (bi, 0, 0, 0))]
    args = [q_t, k_cmp, v_cmp_t, ks, vs_t]
    for j in range(n_win - 1, -NSA_QB, -1):
        in_specs.append(pl.BlockSpec(
            (None, BLK, LANES), lambda bi, i, j=j: (bi, jnp.maximum(NSA_QB * i - j, 0), 0)))
        args.append(kw)
    for j in range(n_win - 1, -NSA_QB, -1):
        in_specs.append(pl.BlockSpec(
            (None, 1, LANES, BLK), lambda bi, i, j=j: (bi, jnp.maximum(NSA_QB * i - j, 0), 0, 0)))
        args.append(vw_t)
    block_onehot = jnp.asarray(
        (np.arange(s)[:, None] // SEL_LEN == np.arange(LANES)[None, :]).astype(np.float32), BF16)
    in_specs += [pl.BlockSpec((None, tq, LANES), row),
                 pl.BlockSpec((None, tq, MIX_WIDTH), row),
                 pl.BlockSpec((sel_rows, ncp), lambda bi, i: (0, 0)),
                 pl.BlockSpec((s, LANES), lambda bi, i: (0, 0)),
                 pl.BlockSpec((MIX_WIDTH, d), lambda bi, i: (0, 0), pipeline_mode=pl.Buffered(1)),
                 pl.BlockSpec((None, tq, d), row),
                 pl.BlockSpec((1, d), lambda bi, i: (0, 0))]
    args += [g_logit, gate, ov_t, block_onehot, w_out.astype(BF16), x, final_gain[None, :]]
    wide = NSA_REP * BLK
    ns = NSA_QB * NSA_KV
    return pl.pallas_call(
        functools.partial(_nsa_kernel, n_cmp, n_sel, n_win),
        grid=(b, s // tq),
        in_specs=in_specs,
        out_specs=pl.BlockSpec((None, tq, d), row),
        out_shape=jax.ShapeDtypeStruct((b, s, d), F32),
        scratch_shapes=[pltpu.VMEM((2 * ns, NSA_TK, wide // 2), F32),
                        pltpu.VMEM((ns, ncp, wide), F32),
                        pltpu.VMEM((ns, n_win * BLK, wide), F32),
                        pltpu.VMEM((ns, 1, wide), F32),
                        pltpu.VMEM((ns, ACC_ROWS, wide), F32),
                        pltpu.VMEM((ns, HEAD_DIM, wide), F32),
                        pltpu.VMEM((ns, ACC_ROWS, wide), F32),
                        pltpu.VMEM((ns, sel_rows, BLK), F32)],
        compiler_params=_cparams(("arbitrary", "arbitrary")),
        name="nsa_attn",
    )(*args)


def _split(w, sizes):
    offs = np.cumsum([0] + list(sizes))
    return [w[:, int(offs[i]):int(offs[i + 1])] for i in range(len(sizes))]


def _swa_layer(x, rope, gain, w_in, sinks, w_out):
    kvw = SWA_KV * HEAD_DIM
    parts = _split(w_in, [MIX_WIDTH, kvw, kvw, MIX_WIDTH])
    segs = [_Seg(MIX_WIDTH, rope=True, scale=Q_SCALE), _Seg(kvw, rope=True), _Seg(kvw),
            _Seg(MIX_WIDTH)]
    q, k, v, gate = _norm_proj(x, gain, parts, segs, rope)
    o = _banded_attention(q, k, v, SWA_WINDOW - 1, sinks=sinks, gate=gate)
    return o, w_out


def _dilated_layer(x, rope, gain, w_in, w_out, residual=None):
    b, s, _ = x.shape
    kvw = DIL_KV * HEAD_DIM
    sizes, segs = [], []
    for window, dil in DIL_PATTERNS:
        assert s % (dil * BLK) == 0
        sizes += [MIX_WIDTH, kvw, kvw]
        segs += [_Seg(MIX_WIDTH, rope=True, scale=Q_SCALE, dil=dil), _Seg(kvw, rope=True, dil=dil),
                 _Seg(kvw, dil=dil)]
    sizes.append(MIX_WIDTH)
    segs.append(_Seg(MIX_WIDTH))
    res = _norm_proj(x, gain, _split(w_in, sizes), segs, rope, residual)
    if residual is not None:
        x, res = res[0], res[1:]
    gate = res[-1]
    os_, lses, dils = [], [], []
    for gi, (window, dil) in enumerate(DIL_PATTERNS):
        q, k, v = res[3 * gi:3 * gi + 3]
        if dil > 1:
            q, k, v = (t.reshape(b * dil, s // dil, t.shape[-1]) for t in (q, k, v))
        o, lse = _banded_attention(q, k, v, window // dil, want_lse=True)
        if dil > 1:
            o = o.reshape(b, dil, s // dil, MIX_WIDTH)
            lse = lse.reshape(b, dil, s // dil, LANES)
        os_.append(o)
        lses.append(lse)
        dils.append(dil)
    return _dil_combine_out_proj(os_, lses, gate, tuple(dils), w_out, x)


def _fox_layer(x, gain, w_in, b_f, w_out):
    sizes = [MIX_WIDTH, MIX_WIDTH, MIX_WIDTH, N_HEADS, MIX_WIDTH]
    segs = [_Seg(MIX_WIDTH, scale=Q_SCALE, tile=FOX_T), _Seg(MIX_WIDTH),
            _Seg(MIX_WIDTH, tile=FOX_T), _Seg(LANES, dtype=F32), _Seg(MIX_WIDTH)]
    qt, k, vt, f_logit, gate = _norm_proj(x, gain, _split(w_in, sizes), segs)
    kb = _fox_decay(f_logit, b_f)
    return _fox_attention(qt, k, kb, vt, gate), w_out


def _nsa_layer(x, positions, rope, gain, w_in, pe_k, w1_k, w2_k, pe_v, w1_v, w2_v, w_out,
               residual, final_gain):
    b, s, _ = x.shape
    kvw = NSA_KV * HEAD_DIM
    sizes = [MIX_WIDTH] + [kvw] * 6 + [3 * N_HEADS, MIX_WIDTH]
    segs = [_Seg(MIX_WIDTH, rope=True, scale=Q_SCALE, tile=BLK), _Seg(kvw), _Seg(kvw),
            _Seg(kvw, rope=True), _Seg(kvw, tile=NSA_TK), _Seg(kvw, rope=True), _Seg(kvw, tile=BLK),
            _Seg(LANES, dtype=F32), _Seg(MIX_WIDTH)]
    x, q, kc, vc, ks, vs_t, kw, vw_t, g_logit, gate = _norm_proj(
        x, gain, _split(w_in, sizes), segs, rope, residual)
    ns = s // CMP_STRIDE
    cmp_pos = jnp.concatenate(
        [positions[:, CMP_LEN - 1::CMP_STRIDE], positions[:, -1:]], axis=1)[:, :ns]
    cmp_c, cmp_s = _rope_tables(cmp_pos.reshape(-1))
    k_cmp, v_cmp_t = _nsa_compress(kc, vc, pe_k, w1_k, w2_k, pe_v, w1_v, w2_v, cmp_c, cmp_s)
    return _nsa_attention(q, k_cmp, v_cmp_t, ks, vs_t, kw, vw_t, g_logit, gate, w_out, x,
                          final_gain)


def kernel(x, positions, norm_0, w_in_0, sinks_0, w_out_0, norm_1, w_in_1, w_out_1, norm_2, w_in_2, b_f_2, w_out_2, norm_3, w_in_3, cmp_pe_k_3, cmp_w1_k_3, cmp_w2_k_3, cmp_pe_v_3, cmp_w1_v_3, cmp_w2_v_3, w_out_3, final_norm):
    rope = _rope_tables(positions.reshape(-1))
    o, w = _swa_layer(x, rope, norm_0, w_in_0, sinks_0, w_out_0)
    x = _dilated_layer(x, rope, norm_1, w_in_1, w_out_1, residual=(o, w))
    o, w = _fox_layer(x, norm_2, w_in_2, b_f_2, w_out_2)
    return _nsa_layer(x, positions, rope, norm_3, w_in_3, cmp_pe_k_3, cmp_w1_k_3, cmp_w2_k_3,
                      cmp_pe_v_3, cmp_w1_v_3, cmp_w2_v_3, w_out_3, residual=(o, w),
                      final_gain=final_norm)
```

```python
import functools
import math

import numpy as np
import jax
import jax.numpy as jnp
from jax import lax
from jax.experimental import pallas as pl
from jax.experimental.pallas import tpu as pltpu

HEAD_DIM = 64
N_HEADS = 16
N_PAIRS = N_HEADS // 2
MIX_WIDTH = N_HEADS * HEAD_DIM
ROT_DIM = HEAD_DIM // 4
ROT_HALF = ROT_DIM // 2
ROPE_THETA = 500000.0
BLK = 128
LANES = 128
NORM_EPS = 1e-6
MASK_VALUE = -1e30
PAD_SCORE = -3e38
LOG2E = math.log2(math.e)
Q_SCALE = HEAD_DIM ** -0.5 * LOG2E

SWA_KV = 4
SWA_WINDOW = 128
DIL_KV = 4
DIL_PATTERNS = ((128, 1), (512, 4), (2048, 16))
NSA_KV = 2
CMP_LEN = 32
CMP_STRIDE = 16
CMP_HIDDEN = 256
SEL_LEN = 64
SEL_TOPK = 8
NSA_WINDOW = 256
FORCED_SCORE = 1e4

VMEM_LIMIT_BYTES = 56 * 1024 * 1024
PROJ_ROWS = 512
PROJ_COLS = 512
BF16_SUBLANES = 16
ACC_ROWS = HEAD_DIM + BF16_SUBLANES
ROPE_ROWS = 2048
F32 = jnp.float32
BF16 = jnp.bfloat16


def _cparams(sem):
    return pltpu.CompilerParams(dimension_semantics=sem, vmem_limit_bytes=VMEM_LIMIT_BYTES)


def _lane_half(shape):
    return lax.broadcasted_iota(jnp.int32, shape, 1) // HEAD_DIM


def _swap_halves(t):
    return jnp.concatenate([t[:, HEAD_DIM:], t[:, :HEAD_DIM]], axis=1)


def _head_query(qp, e, kv_half):
    qh = jnp.where(_lane_half(qp.shape) == e, qp, jnp.zeros_like(qp))
    if e != kv_half:
        qh = _swap_halves(qh)
    return qh


def _group_queries(q_ref, g, rep):
    tiles = []
    for r in range(rep):
        h = g * rep + r
        tiles.append(_head_query(q_ref[:, (h // 2) * LANES:(h // 2 + 1) * LANES], h % 2, g % 2))
    return jnp.concatenate(tiles, axis=0)


def _dot_nt(a, b):
    return lax.dot_general(a, b, (((1,), (1,)), ((), ())), preferred_element_type=F32)


def _dot(a, b):
    return jnp.dot(a, b, preferred_element_type=F32)


def _silu(x):
    return x * (1.0 / (1.0 + jnp.exp(-x)))


def _loop_in_pairs(count, body):
    odd = count % 2

    @pl.when(odd == 1)
    def _():
        body(0)

    def two(jj, carry):
        body(odd + 2 * jj)
        body(odd + 2 * jj + 1)
        return carry

    lax.fori_loop(0, count // 2, two, 0)


def _with_ones(vt):
    return jnp.concatenate([vt, jnp.ones((ACC_ROWS - HEAD_DIM, vt.shape[1]), BF16)], axis=0)


def _rope_table_kernel(pos_ref, inv_ref, c_ref, s_ref):
    pos = pos_ref[...].astype(F32)
    ang = pos * inv_ref[...]
    d = lax.broadcasted_iota(jnp.int32, ang.shape, 1) % HEAD_DIM
    cos = jnp.cos(ang)
    sin = jnp.sin(ang)
    c_ref[...] = jnp.where(d < ROT_DIM, cos, 1.0)
    s_ref[...] = jnp.where(d < ROT_HALF, -sin, jnp.where(d < ROT_DIM, sin, 0.0))


def _rope_tables(pos_flat):
    t = pos_flat.shape[0]
    rows = min(t, ROPE_ROWS)
    assert t % rows == 0
    inv = jnp.power(ROPE_THETA, -jnp.arange(ROT_HALF, dtype=F32) / ROT_HALF)
    inv_l = jnp.tile(inv, LANES // ROT_HALF)[None, :]
    out = jax.ShapeDtypeStruct((t, LANES), F32)
    return pl.pallas_call(
        _rope_table_kernel,
        grid=(t // rows,),
        in_specs=[pl.BlockSpec((rows, 1), lambda i: (i, 0)),
                  pl.BlockSpec((1, LANES), lambda i: (0, 0))],
        out_specs=[pl.BlockSpec((rows, LANES), lambda i: (i, 0))] * 2,
        out_shape=[out, out],
        compiler_params=_cparams(("arbitrary",)),
        name="rope_tables",
    )(pos_flat[:, None], inv_l)


def _apply_rope(y, c, s):
    outs = []
    for j in range(y.shape[1] // LANES):
        t = y[:, j * LANES:(j + 1) * LANES]
        d = lax.broadcasted_iota(jnp.int32, t.shape, 1) % HEAD_DIM
        partner = jnp.where(d < ROT_HALF, pltpu.roll(t, LANES - ROT_HALF, 1),
                            pltpu.roll(t, ROT_HALF, 1))
        outs.append(t * c + partner * s)
    return outs[0] if len(outs) == 1 else jnp.concatenate(outs, axis=1)


class _Seg:
    def __init__(self, width, rope=False, scale=None, dtype=BF16, dil=1, tile=None):
        self.width, self.rope, self.scale, self.dtype, self.dil = width, rope, scale, dtype, dil
        self.tile = tile
        assert not (tile and dil > 1)


def _norm_proj_kernel(segs, use_rope, has_residual, *refs):
    has_t = any(sg.tile and not sg.rope for sg in segs)
    x_ref, g_ref, w_ref = refs[:3]
    k = 3
    if has_t:
        wt_ref = refs[k]
        k += 1
    if use_rope:
        c_ref, s_ref = refs[k:k + 2]
        k += 2
    if has_residual:
        po_ref, pw_ref = refs[k:k + 2]
        k += 2
        if has_residual == 3:
            pg_ref = refs[k]
            k += 1
        xnew_ref = refs[k]
        k += 1
    out_refs = refs[k:k + len(segs)]
    stage_ref = refs[k + len(segs)] if any(sg.dil > 1 for sg in segs) else None

    x = x_ref[...]
    if has_residual:
        po = po_ref[...]
        if has_residual == 3:
            po = (po.astype(F32) * _silu(pg_ref[...].astype(F32))).astype(BF16)
        x = x + _dot(po, pw_ref[...])
        xnew_ref[...] = x
    var = jnp.mean(x * x, axis=-1, keepdims=True)
    h = (x * lax.rsqrt(var + NORM_EPS) * g_ref[...]).astype(BF16)
    rows = x.shape[0]
    col = 0
    tcol = 0
    for sg, o_ref in zip(segs, out_refs):
        if sg.tile and not sg.rope:
            for c0 in range(0, sg.width, PROJ_COLS):
                cw = min(PROJ_COLS, sg.width - c0)
                yt = _dot_nt(wt_ref[tcol + c0:tcol + c0 + cw, :], h)
                if sg.scale is not None:
                    yt = yt * sg.scale
                for ti in range(rows // sg.tile):
                    o_ref[ti, c0:c0 + cw, :] = yt[:, ti * sg.tile:(ti + 1) * sg.tile].astype(sg.dtype)
            tcol += sg.width
            continue
        for c0 in range(0, sg.width, PROJ_COLS):
            cw = min(PROJ_COLS, sg.width - c0)
            y = _dot(h, w_ref[:, col + c0:col + c0 + cw])
            if sg.rope:
                y = _apply_rope(y, c_ref[...], s_ref[...])
            if sg.scale is not None:
                y = y * sg.scale
            if sg.tile:
                yt = y.T
                for ti in range(rows // sg.tile):
                    o_ref[ti, c0:c0 + cw, :] = yt[:, ti * sg.tile:(ti + 1) * sg.tile].astype(sg.dtype)
            elif sg.dil > 1:
                sub = rows // sg.dil
                for j in range(cw // LANES):
                    stage_ref[j] = y[:, j * LANES:(j + 1) * LANES]
                for r in range(sg.dil):
                    for j in range(cw // LANES):
                        lo = c0 + j * LANES
                        o_ref[r, :, lo:lo + LANES] = (
                            stage_ref[j, pl.ds(r, sub, stride=sg.dil), :].astype(sg.dtype))
            else:
                o_ref[:, c0:c0 + cw] = y.astype(sg.dtype)
        col += sg.width


def _norm_proj(x, gain, w_parts, segs, rope=None, residual=None):
    b, s, d = x.shape
    tm = PROJ_ROWS
    assert s % tm == 0
    w_cols, wt_rows = [], []
    for wp, sg in zip(w_parts, segs):
        if wp.shape[1] < sg.width:
            wp = jnp.pad(wp, ((0, 0), (0, sg.width - wp.shape[1])))
        if sg.tile and not sg.rope:
            wt_rows.append(wp.T)
        else:
            w_cols.append(wp)
    w = jnp.concatenate(w_cols, axis=1).astype(BF16)
    n = w.shape[1]
    use_rope = rope is not None
    once = dict(pipeline_mode=pl.Buffered(1))
    in_specs = [pl.BlockSpec((None, tm, d), lambda bi, i: (bi, i, 0)),
                pl.BlockSpec((1, d), lambda bi, i: (0, 0)),
                pl.BlockSpec((d, n), lambda bi, i: (0, 0), **once)]
    args = [x, gain[None, :], w]
    if wt_rows:
        wt = jnp.concatenate(wt_rows, axis=0).astype(BF16)
        in_specs.append(pl.BlockSpec(wt.shape, lambda bi, i: (0, 0), **once))
        args.append(wt)
    if use_rope:
        in_specs += [pl.BlockSpec((None, tm, LANES), lambda bi, i: (bi, i, 0))] * 2
        args += [rope[0].reshape(b, s, LANES), rope[1].reshape(b, s, LANES)]
    out_specs, out_shape = [], []
    if residual is not None:
        po, pw = residual[:2]
        in_specs += [pl.BlockSpec((None, tm, po.shape[-1]), lambda bi, i: (bi, i, 0)),
                     pl.BlockSpec(pw.shape, lambda bi, i: (0, 0), **once)]
        args += [po, pw.astype(BF16)]
        if len(residual) == 3:
            in_specs.append(pl.BlockSpec((None, tm, po.shape[-1]), lambda bi, i: (bi, i, 0)))
            args.append(residual[2])
        out_shape.append(jax.ShapeDtypeStruct((b, s, d), F32))
        out_specs.append(pl.BlockSpec((None, tm, d), lambda bi, i: (bi, i, 0)))
    for sg in segs:
        if sg.dil > 1:
            assert tm % sg.dil == 0
            out_shape.append(jax.ShapeDtypeStruct((b, sg.dil, s // sg.dil, sg.width), sg.dtype))
            out_specs.append(pl.BlockSpec((None, sg.dil, tm // sg.dil, sg.width),
                                          lambda bi, i: (bi, 0, i, 0)))
        elif sg.tile:
            assert tm % sg.tile == 0
            out_shape.append(jax.ShapeDtypeStruct((b, s // sg.tile, sg.width, sg.tile), sg.dtype))
            out_specs.append(pl.BlockSpec((None, tm // sg.tile, sg.width, sg.tile),
                                          lambda bi, i: (bi, i, 0, 0)))
        else:
            out_shape.append(jax.ShapeDtypeStruct((b, s, sg.width), sg.dtype))
            out_specs.append(pl.BlockSpec((None, tm, sg.width), lambda bi, i: (bi, i, 0)))
    scratch = ([pltpu.VMEM((PROJ_COLS // LANES, tm, LANES), F32)]
               if any(sg.dil > 1 for sg in segs) else [])
    return pl.pallas_call(
        functools.partial(_norm_proj_kernel, segs, use_rope, len(residual) if residual else 0),
        grid=(b, s // tm),
        in_specs=in_specs,
        out_specs=out_specs,
        out_shape=out_shape,
        scratch_shapes=scratch,
        compiler_params=_cparams(("arbitrary", "arbitrary")),
        name="norm_proj",
    )(*args)


BANDED_MAX_QB = 8


def _banded_kernel(n_seq, n_qb, max_dist, n_prev, n_kv, has_sink, has_gate, want_lse, *refs):
    q_ref = refs[0]
    kp_refs, kc_ref = refs[1:1 + n_prev], refs[1 + n_prev]
    vp_refs, vc_ref = refs[2 + n_prev:2 + 2 * n_prev], refs[2 + 2 * n_prev]

    def window(prev_refs, cur_ref, sq):
        return ([pr.at[sq] for pr in prev_refs]
                + [cur_ref.at[sq, a * BLK:(a + 1) * BLK] for a in range(n_qb)])

    k = 3 + 2 * n_prev
    sink_ref = gate_ref = lse_ref = None
    if has_sink:
        sink_ref = refs[k]; k += 1
    if has_gate:
        gate_ref = refs[k]; k += 1
    o_ref = refs[k]; k += 1
    if want_lse:
        lse_ref = refs[k]; k += 1
    ot_sc = refs[k]
    st_sc = refs[k + 1]

    n = pl.program_id(1)
    tq = BLK
    kw = (n_prev + 1) * BLK
    rep = N_HEADS // n_kv
    krow = lax.broadcasted_iota(jnp.int32, (kw, tq), 0) - n_prev * BLK
    dist = lax.broadcasted_iota(jnp.int32, (kw, tq), 1) - krow
    band = (dist >= 0) & (dist <= max_dist)
    lse_rows = [[] for _ in range(n_seq * n_qb)]
    units = [(sq, qb, g) for sq in range(n_seq) for qb in range(n_qb) for g in range(n_kv)]

    def scores(u):
        sq, qb, g = units[u]
        sl = slice((g // 2) * LANES, (g // 2 + 1) * LANES)
        k_refs = window(kp_refs, kc_ref, sq)
        kcat = jnp.concatenate([kr[:, sl] for kr in k_refs[qb:qb + n_prev + 1]], axis=0)
        st_sc[u] = _dot_nt(kcat, _group_queries(q_ref.at[sq, qb * tq:(qb + 1) * tq], g, rep))

    oks, vts = {}, {}

    def consume(u):
        sq, qb, g = units[u]
        vb = sq * n_qb + qb
        if qb not in oks:
            oks[qb] = band & ((n * n_qb + qb) * BLK + krow >= 0)
        ok = oks[qb]
        if (vb, g // 2) not in vts:
            sl = slice((g // 2) * LANES, (g // 2 + 1) * LANES)
            v_refs = window(vp_refs, vc_ref, sq)
            vcat = jnp.concatenate([vr[:, sl] for vr in v_refs[qb:qb + n_prev + 1]], axis=0)
            vts[vb, g // 2] = vcat.astype(F32).T.astype(BF16)
        vt = vts[vb, g // 2]
        ms, pts = [], []
        for r in range(rep):
            s_r = jnp.where(ok, st_sc[u, :, r * tq:(r + 1) * tq], MASK_VALUE)
            m = jnp.max(s_r, axis=0, keepdims=True)
            if has_sink:
                h = g * rep + r
                m = jnp.maximum(m, sink_ref[0:1, h:h + 1])
            pts.append(jnp.exp2(s_r - m).astype(BF16))
            ms.append(m)
        acc = _dot(_with_ones(vt[(g % 2) * HEAD_DIM:(g % 2 + 1) * HEAD_DIM, :]),
                   jnp.concatenate(pts, axis=1))
        for r in range(rep):
            h = g * rep + r
            den = acc[HEAD_DIM:HEAD_DIM + 1, r * tq:(r + 1) * tq]
            if has_sink:
                den = den + jnp.exp2(sink_ref[0:1, h:h + 1] - ms[r])
            ot_sc[vb * N_HEADS + h] = acc[:HEAD_DIM, r * tq:(r + 1) * tq] / den
            if want_lse:
                lse_rows[vb].append(ms[r] + jnp.log2(den))

    scores(0)
    scores(1)
    for u in range(len(units)):
        if u + 2 < len(units):
            scores(u + 2)
        consume(u)
    for vb in range(n_seq * n_qb):
        sq, qb = vb // n_qb, vb % n_qb
        rows = slice(qb * tq, (qb + 1) * tq)
        for p in range(N_PAIRS):
            sl = slice(p * LANES, (p + 1) * LANES)
            pair = jnp.concatenate([ot_sc[vb * N_HEADS + 2 * p],
                                    ot_sc[vb * N_HEADS + 2 * p + 1]], axis=0).T
            if has_gate:
                pair = pair * _silu(gate_ref[sq, rows, sl].astype(F32))
            o_ref[sq, rows, sl] = pair.astype(o_ref.dtype)
        if want_lse:
            lse_t = jnp.concatenate(
                lse_rows[vb] + [jnp.zeros((LANES - N_HEADS, tq), F32)], axis=0)
            lse_ref[sq, rows, :] = lse_t.T


def _banded_attention(q, k, v, max_dist, sinks=None, gate=None, want_lse=False):
    bq, sq, _ = q.shape
    kvw = k.shape[-1]
    n_kv = kvw // HEAD_DIM
    n_prev = -(-max_dist // BLK)
    n_qb = max(d for d in range(1, BANDED_MAX_QB + 1) if (sq // BLK) % d == 0)
    n_seq = max(d for d in range(1, BANDED_MAX_QB // n_qb + 1) if bq % d == 0)
    tq = n_qb * BLK
    assert sq % tq == 0
    row = lambda b, i: (b, i, 0)
    in_specs = [pl.BlockSpec((n_seq, tq, MIX_WIDTH), row)]
    args = [q]
    for arr in (k, v):
        for j in range(n_prev, 0, -1):
            in_specs.append(pl.BlockSpec(
                (n_seq, BLK, kvw), lambda b, i, j=j: (b, jnp.maximum(n_qb * i - j, 0), 0)))
            args.append(arr)
        in_specs.append(pl.BlockSpec((n_seq, tq, kvw), row))
        args.append(arr)
    if sinks is not None:
        in_specs.append(pl.BlockSpec((1, LANES), lambda b, i: (0, 0)))
        args.append(jnp.pad(sinks.astype(F32) * LOG2E, (0, LANES - N_HEADS))[None, :])
    if gate is not None:
        in_specs.append(pl.BlockSpec((n_seq, tq, MIX_WIDTH), row))
        args.append(gate)
    out_specs = [pl.BlockSpec((n_seq, tq, MIX_WIDTH), row)]
    out_shape = [jax.ShapeDtypeStruct((bq, sq, MIX_WIDTH), BF16)]
    if want_lse:
        out_specs.append(pl.BlockSpec((n_seq, tq, LANES), row))
        out_shape.append(jax.ShapeDtypeStruct((bq, sq, LANES), F32))
    n_vb = n_seq * n_qb
    res = pl.pallas_call(
        functools.partial(_banded_kernel, n_seq, n_qb, max_dist, n_prev, n_kv, sinks is not None,
                          gate is not None, want_lse),
        grid=(bq // n_seq, sq // tq),
        in_specs=in_specs,
        out_specs=out_specs,
        out_shape=out_shape,
        scratch_shapes=[pltpu.VMEM((n_vb * N_HEADS, HEAD_DIM, BLK), F32),
                        pltpu.VMEM((n_vb * n_kv, (n_prev + 1) * BLK,
                                    (N_HEADS // n_kv) * BLK), F32)],
        compiler_params=_cparams(("arbitrary", "arbitrary")),
        name="banded_attn",
    )(*args)
    return res if want_lse else res[0]


def _dil_combine_kernel(dils, *refs):
    ng = len(dils)
    o_refs, l_refs = refs[:ng], refs[ng:2 * ng]
    gate_ref, expand_ref, w_ref, x_ref, out_ref, stage_ref, lstage_ref = refs[2 * ng:]
    rows = out_ref.shape[0]

    def natural(ref, dil, stage, slab):
        sl = slice(slab * LANES, (slab + 1) * LANES)
        if dil == 1:
            return ref[:, sl].astype(F32)
        sub = rows // dil
        for r in range(dil):
            stage[pl.ds(r, sub, stride=dil), :] = ref[r, :, sl].astype(F32)
        return stage[...]

    lses = [natural(l_refs[i], dils[i], lstage_ref, 0) for i in range(ng)]
    mx = functools.reduce(jnp.maximum, lses)
    ws = [jnp.exp2(l - mx) for l in lses]
    tot = functools.reduce(lambda a, c: a + c, ws)
    inv = 1.0 / tot

    def widen(w):
        hi = w.astype(BF16)
        lo = (w - hi.astype(F32)).astype(BF16)
        return _dot(hi, expand_ref[...]) + _dot(lo, expand_ref[...])

    wide = [widen(w * inv) for w in ws[:-1]]
    wide.append(1.0 - functools.reduce(lambda a, c: a + c, wide))
    gated = []
    for p in range(N_PAIRS):
        sl = slice(p * LANES, (p + 1) * LANES)
        acc = None
        for gi in range(ng):
            t = wide[gi][:, sl] * natural(o_refs[gi], dils[gi], stage_ref, p)
            acc = t if acc is None else acc + t
        gated.append((acc * _silu(gate_ref[:, sl].astype(F32))).astype(BF16))
    out_ref[...] = x_ref[...] + _dot(jnp.concatenate(gated, axis=1), w_ref[...])


def _dil_combine_out_proj(os_, lses, gate, dils, w_out, x):
    b, s, d = x.shape
    tm = PROJ_ROWS
    expand = jnp.asarray((np.arange(LANES)[:, None] == np.arange(MIX_WIDTH)[None, :] // HEAD_DIM)
                         .astype(np.float32), BF16)
    in_specs, args = [], []
    for arrs, width in ((os_, MIX_WIDTH), (lses, LANES)):
        for arr, dil in zip(arrs, dils):
            if dil == 1:
                in_specs.append(pl.BlockSpec((None, tm, width), lambda bi, i: (bi, i, 0)))
            else:
                in_specs.append(pl.BlockSpec((None, dil, tm // dil, width),
                                             lambda bi, i: (bi, 0, i, 0)))
            args.append(arr)
    row = lambda bi, i: (bi, i, 0)
    in_specs += [pl.BlockSpec((None, tm, MIX_WIDTH), row),
                 pl.BlockSpec((LANES, MIX_WIDTH), lambda bi, i: (0, 0)),
                 pl.BlockSpec((MIX_WIDTH, d), lambda bi, i: (0, 0)),
                 pl.BlockSpec((None, tm, d), row)]
    args += [gate, expand, w_out.astype(BF16), x]
    return pl.pallas_call(
        functools.partial(_dil_combine_kernel, dils),
        grid=(b, s // tm),
        in_specs=in_specs,
        out_specs=pl.BlockSpec((None, tm, d), row),
        out_shape=jax.ShapeDtypeStruct((b, s, d), F32),
        scratch_shapes=[pltpu.VMEM((tm, LANES), F32), pltpu.VMEM((tm, LANES), F32)],
        compiler_params=_cparams(("arbitrary", "arbitrary")),
        name="dil_combine_out_proj",
    )(*args)


N_BIAS_PIECES = 3


def _fox_decay_kernel(f_ref, b_ref, kb_ref):
    s = f_ref.shape[0]
    x = f_ref[...] + b_ref[...]
    logf = jnp.minimum(x, 0.0) - jnp.log1p(jnp.exp(-jnp.abs(x)))
    r = lax.broadcasted_iota(jnp.int32, (BLK, BLK), 0)
    c = lax.broadcasted_iota(jnp.int32, (BLK, BLK), 1)
    tri = (c <= r).astype(F32)
    lane = lax.broadcasted_iota(jnp.int32, (BLK, LANES), 1)
    carry = jnp.zeros((1, LANES), F32)
    for i in range(s // BLK):
        blk = logf[i * BLK:(i + 1) * BLK, :]
        cs = jnp.dot(tri, blk, preferred_element_type=F32, precision=lax.Precision.HIGHEST) + carry
        carry = cs[BLK - 1:BLK, :]
        rest = cs * (-LOG2E)
        out = jnp.zeros((BLK, LANES), F32)
        for j in range(N_BIAS_PIECES):
            piece = rest.astype(BF16).astype(F32)
            rest = rest - piece
            moved = piece if j == 0 else pltpu.roll(piece, N_HEADS * j, 1)
            out = jnp.where((lane >= N_HEADS * j) & (lane < N_HEADS * (j + 1)), moved, out)
        kb_ref[i * BLK:(i + 1) * BLK, :] = out.astype(BF16)


def _fox_decay(f_logit, b_f):
    b, s, _ = f_logit.shape
    b_pad = jnp.pad(b_f.astype(F32), (0, LANES - N_HEADS))[None, :]
    return pl.pallas_call(
        _fox_decay_kernel,
        grid=(b,),
        in_specs=[pl.BlockSpec((None, s, LANES), lambda bi: (bi, 0, 0)),
                  pl.BlockSpec((1, LANES), lambda bi: (0, 0))],
        out_specs=pl.BlockSpec((None, s, LANES), lambda bi: (bi, 0, 0)),
        out_shape=jax.ShapeDtypeStruct((b, s, LANES), BF16),
        compiler_params=_cparams(("arbitrary",)),
        name="fox_decay",
    )(f_logit, b_pad)


FOX_T = PROJ_ROWS
FOX_HEADS_PER_STEP = 8


def _fox_kernel(qt_ref, k_ref, kb_ref, vt_ref, gate_ref, o_ref, m_sc, acc_sc, st_sc):
    pi = pl.program_id(1)
    i = pl.program_id(2)
    t = FOX_T
    heads = range(FOX_HEADS_PER_STEP)
    row = lax.broadcasted_iota(jnp.int32, (LANES, t), 0)
    zeros = jnp.zeros((HEAD_DIM, t), BF16)
    qx = []
    for e in heads:
        h = FOX_HEADS_PER_STEP * pi + e
        pick = (row % N_HEADS == h) & (row < N_HEADS * N_BIAS_PIECES)
        sel = jnp.where(pick, 1.0, 0.0).astype(BF16)
        mine = qt_ref[0, e * HEAD_DIM:(e + 1) * HEAD_DIM, :]
        top = [mine, zeros] if e % 2 == 0 else [zeros, mine]
        qx.append(jnp.concatenate(top + [sel], axis=0))
        m_sc[e] = jnp.full((1, t), MASK_VALUE, F32)
        acc_sc[e] = jnp.zeros((ACC_ROWS, t), F32)

    tk = t // 2

    def scores(jt, hf, e, diag=False):
        k0 = pl.multiple_of(jt * t + hf * tk, tk)
        pair = slice((e // 2) * LANES, (e // 2 + 1) * LANES)
        kx = jnp.concatenate([k_ref[pl.ds(k0, tk), pair], kb_ref[pl.ds(k0, tk), :]], axis=1)
        q0 = hf * tk if diag else 0
        st_sc[e, hf, :, q0:] = _dot(kx, qx[e][:, q0:])

    def consume(jt, hf, e, diag):
        q0 = hf * tk if diag else 0
        nq = t - q0
        st = st_sc[e, hf, :, q0:]
        if diag:
            ok = (lax.broadcasted_iota(jnp.int32, (tk, nq), 0)
                  <= lax.broadcasted_iota(jnp.int32, (tk, nq), 1) + (q0 - hf * tk))
            st = jnp.where(ok, st, MASK_VALUE)
        m_old = m_sc[e, :, q0:]
        m_new = jnp.maximum(m_old, jnp.max(st, axis=0, keepdims=True))
        pt = jnp.exp2(st - m_new).astype(BF16)
        alpha = jnp.exp2(m_old - m_new)
        vt = vt_ref[jt, e * HEAD_DIM:(e + 1) * HEAD_DIM, hf * tk:(hf + 1) * tk]
        acc_sc[e, :, q0:] = alpha * acc_sc[e, :, q0:] + _dot(_with_ones(vt), pt)
        m_sc[e, :, q0:] = m_new

    for e in heads:
        scores(0, 0, e)

    def body(jt):
        for e in heads:
            scores(jt, 1, e)
            consume(jt, 0, e, False)
        for e in heads:
            scores(jt + 1, 0, e)
            consume(jt, 1, e, False)

    _loop_in_pairs(i, body)
    for e in heads:
        scores(i, 1, e, diag=True)
        consume(i, 0, e, True)
    for e in heads:
        consume(i, 1, e, True)
    for p in range(FOX_HEADS_PER_STEP // 2):
        outs = []
        for e in (2 * p, 2 * p + 1):
            acc = acc_sc[e]
            outs.append(acc[:HEAD_DIM] / jnp.maximum(acc[HEAD_DIM:HEAD_DIM + 1], 1e-30))
        sl = slice(p * LANES, (p + 1) * LANES)
        out = jnp.concatenate(outs, axis=0).T
        o_ref[:, sl] = (out * _silu(gate_ref[:, sl].astype(F32))).astype(o_ref.dtype)


def _fox_attention(qt, k, kb, vt, gate):
    b, s, _ = k.shape
    t = FOX_T
    assert s % t == 0 and vt.shape == (b, s // t, MIX_WIDTH, t) and qt.shape == vt.shape
    tile = lambda bi, p, i: (bi, i, p)
    nh = FOX_HEADS_PER_STEP
    wd = nh * HEAD_DIM
    return pl.pallas_call(
        _fox_kernel,
        grid=(b, N_HEADS // nh, s // t),
        in_specs=[pl.BlockSpec((None, 1, wd, t), lambda bi, p, i: (bi, i, p, 0)),
                  pl.BlockSpec((None, s, wd), lambda bi, p, i: (bi, 0, p)),
                  pl.BlockSpec((None, s, LANES), lambda bi, p, i: (bi, 0, 0)),
                  pl.BlockSpec((None, s // t, wd, t), lambda bi, p, i: (bi, 0, p, 0)),
                  pl.BlockSpec((None, t, wd), tile)],
        out_specs=pl.BlockSpec((None, t, wd), tile),
        out_shape=jax.ShapeDtypeStruct((b, s, MIX_WIDTH), BF16),
        scratch_shapes=[pltpu.VMEM((nh, 1, t), F32), pltpu.VMEM((nh, ACC_ROWS, t), F32),
                        pltpu.VMEM((nh, 2, t // 2, t), F32)],
        compiler_params=_cparams(("arbitrary", "arbitrary", "arbitrary")),
        name="fox_attn",
    )(qt, k, kb, vt, gate)


def _gelu_tanh(x):
    return 0.5 * x * (1.0 + jnp.tanh(math.sqrt(2.0 / math.pi) * (x + 0.044715 * (x * x * x))))


def _nsa_compress_kernel(ak_ref, av_ref, pe_ref, wa_ref, wb_ref, w2_ref, w2t_ref, c_ref, s_ref,
                         kc_ref, vct_ref):
    nrow = ak_ref.shape[0]
    for idx, a_ref in enumerate((ak_ref, av_ref)):
        a = a_ref[...].astype(F32)
        xa = (a + pe_ref[idx, 0:1, :]).astype(BF16)
        xb = (a + pe_ref[idx, 1:2, :]).astype(BF16)
        ya = _dot(xa, wa_ref[idx])
        yb = _dot(xb, wb_ref[idx])
        hid = _gelu_tanh(ya + pltpu.roll(yb, nrow - 1, 0)).astype(BF16)
        if idx == 0:
            y = _dot(hid, w2_ref[...])
            kc_ref[...] = _apply_rope(y, c_ref[...], s_ref[...]).astype(kc_ref.dtype)
        else:
            vct_ref[...] = _dot_nt(w2t_ref[...], hid).astype(vct_ref.dtype)


def _nsa_compress(kc, vc, pe_k, w1_k, w2_k, pe_v, w1_v, w2_v, rope_c, rope_s):
    b, s, _ = kc.shape
    ns = s // CMP_STRIDE
    g = NSA_KV
    flat = CMP_STRIDE * g * HEAD_DIM

    def w1_halves(w1):
        w1r = w1.reshape(2, CMP_STRIDE, HEAD_DIM, CMP_HIDDEN)
        outs = []
        for hf in range(2):
            z = jnp.einsum('ldc,gh->lgdhc', w1r[hf], jnp.eye(g, dtype=F32))
            outs.append(z.reshape(flat, g * CMP_HIDDEN))
        return outs

    def pe_halves(pe):
        per = pe.reshape(2, CMP_STRIDE, 1, HEAD_DIM)
        return jnp.broadcast_to(per, (2, CMP_STRIDE, g, HEAD_DIM)).reshape(2, flat)

    def w2_bd(w2):
        z = jnp.einsum('cd,gh->gchd', w2, jnp.eye(g, dtype=F32))
        return z.reshape(g * CMP_HIDDEN, g * HEAD_DIM)

    ka, kb = w1_halves(w1_k)
    va, vb = w1_halves(w1_v)
    wa = jnp.stack([ka, va]).astype(BF16)
    wb = jnp.stack([kb, vb]).astype(BF16)
    w2 = w2_bd(w2_k).astype(BF16)
    w2t = w2_bd(w2_v).T.astype(BF16)
    pe = jnp.stack([pe_halves(pe_k), pe_halves(pe_v)]).astype(F32)
    whole = lambda bi: (0, 0, 0)
    per_b = lambda bi: (bi, 0, 0)
    return pl.pallas_call(
        _nsa_compress_kernel,
        grid=(b,),
        in_specs=[pl.BlockSpec((None, ns, flat), per_b),
                  pl.BlockSpec((None, ns, flat), per_b),
                  pl.BlockSpec((2, 2, flat), whole),
                  pl.BlockSpec((2, flat, g * CMP_HIDDEN), whole),
                  pl.BlockSpec((2, flat, g * CMP_HIDDEN), whole),
                  pl.BlockSpec((g * CMP_HIDDEN, LANES), lambda bi: (0, 0)),
                  pl.BlockSpec((LANES, g * CMP_HIDDEN), lambda bi: (0, 0)),
                  pl.BlockSpec((None, ns, LANES), per_b),
                  pl.BlockSpec((None, ns, LANES), per_b)],
        out_specs=[pl.BlockSpec((None, ns, LANES), per_b),
                   pl.BlockSpec((None, LANES, ns), per_b)],
        out_shape=[jax.ShapeDtypeStruct((b, ns, LANES), BF16),
                   jax.ShapeDtypeStruct((b, LANES, ns), BF16)],
        compiler_params=_cparams(("arbitrary",)),
        name="nsa_compress",
    )(kc.reshape(b, ns, flat), vc.reshape(b, ns, flat), pe, wa, wb, w2, w2t,
      rope_c.reshape(b, ns, LANES), rope_s.reshape(b, ns, LANES))


NSA_QB = 2
NSA_TK = 256
NSA_REP = N_HEADS // NSA_KV


def _nsa_kernel(n_cmp, n_sel, n_win, *refs):
    n_wblk = n_win + NSA_QB - 1
    q_ref, kc_ref, vct_ref, ks_ref, vst_ref = refs[:5]
    kw_refs = refs[5:5 + n_wblk]
    vwt_refs = refs[5 + n_wblk:5 + 2 * n_wblk]
    (gl_ref, gate_ref, ovt_ref, blk_ref, wout_ref, x_ref, fg_ref,
     o_ref) = refs[5 + 2 * n_wblk:13 + 2 * n_wblk]
    st_sc, stc_sc, stw_sc, m_sc, acc_sc, ocmp_sc, owin_sc, imp_sc = refs[13 + 2 * n_wblk:]

    n = pl.program_id(1)
    tq, tk, rep = BLK, NSA_TK, NSA_REP
    streams = [(qb, g) for qb in range(NSA_QB) for g in range(NSA_KV)]
    ncp = kc_ref.shape[0]
    t_lane = [(n * NSA_QB + qb) * tq + lax.broadcasted_iota(jnp.int32, (1, tq), 1)
              for qb in range(NSA_QB)]
    zeros = jnp.zeros((HEAD_DIM, tq), BF16)
    qg = []
    for qb, g in streams:
        tiles = []
        for r in range(rep):
            h = g * rep + r
            mine = q_ref[qb, h * HEAD_DIM:(h + 1) * HEAD_DIM, :]
            tiles.append(jnp.concatenate([mine, zeros] if g % 2 == 0 else [zeros, mine], axis=0))
        qg.append(jnp.concatenate(tiles, axis=1))

    ci = lax.broadcasted_iota(jnp.int32, (ncp, tq), 0)
    kwn = n_win * BLK
    krow = lax.broadcasted_iota(jnp.int32, (kwn, tq), 0) - (n_win - 1) * BLK
    dist = lax.broadcasted_iota(jnp.int32, (kwn, tq), 1) - krow
    band = (dist >= 0) & (dist <= NSA_WINDOW - 1)
    sel_rows = ovt_ref.shape[0]
    rowi = lax.broadcasted_iota(jnp.int32, (sel_rows, tq), 0)
    rowf = rowi.astype(F32)

    def cmp_scores(sg, nrows):
        stc_sc[sg, :nrows] = _dot(kc_ref[:nrows, :], qg[sg])

    def cmp_consume(sg, nrows):
        qb, g = streams[sg]
        cmask = (ci[:nrows] * CMP_STRIDE + (CMP_LEN - 1) <= t_lane[qb]) & (ci[:nrows] < n_cmp)
        ps = []
        for r in range(rep):
            s_r = jnp.where(cmask, stc_sc[sg, :nrows, r * tq:(r + 1) * tq], MASK_VALUE)
            m = jnp.max(s_r, axis=0, keepdims=True)
            pr = jnp.where(cmask, jnp.exp2(s_r - m), 0.0)
            ps.append(pr.astype(BF16))
        lhs = jnp.concatenate([_with_ones(vct_ref[g * HEAD_DIM:(g + 1) * HEAD_DIM, :nrows]),
                               ovt_ref[:, :nrows]], axis=0)
        res = _dot(lhs, jnp.concatenate(ps, axis=1))
        inv = 1.0 / jnp.maximum(res[HEAD_DIM:HEAD_DIM + 1], 1e-30)
        ocmp_sc[sg] = res[:HEAD_DIM] * inv
        imp = None
        for r in range(rep):
            ql = slice(r * tq, (r + 1) * tq)
            t = res[ACC_ROWS:, ql] * inv[:, ql]
            imp = t if imp is None else imp + t
        return imp

    def select(sg, imp):
        cur = t_lane[streams[sg][0]] // SEL_LEN
        forced = (rowi == 0) | (rowi == cur)
        causal = rowi <= cur
        score = jnp.where(causal, jnp.where(forced, FORCED_SCORE, imp), MASK_VALUE)
        score = jnp.where(rowi < n_sel, score, PAD_SCORE)
        chosen = jnp.zeros((sel_rows, tq), F32)
        for _ in range(min(SEL_TOPK, n_sel)):
            mx = jnp.max(score, axis=0, keepdims=True)
            first = jnp.min(jnp.where(score == mx, rowf, float(LANES)), axis=0, keepdims=True)
            hit = rowf == first
            chosen = jnp.where(hit, 1.0, chosen)
            score = jnp.where(hit, PAD_SCORE, score)
        bias_t = jnp.where(causal & (chosen > 0.5), 0.0, MASK_VALUE)
        bias_t = jnp.concatenate([bias_t, jnp.zeros((LANES - sel_rows, tq), F32)], axis=0)
        bias_t = bias_t.astype(BF16)
        return jnp.concatenate([qg[sg], jnp.concatenate([bias_t] * rep, axis=1)], axis=0)

    def win_scores(sg):
        qb = streams[sg][0]
        kwcat = jnp.concatenate([kr[...] for kr in kw_refs[qb:qb + n_win]], axis=0)
        stw_sc[sg] = _dot(kwcat, qg[sg])

    def win_consume(sg):
        qb, g = streams[sg]
        wok = band & ((n * NSA_QB + qb) * BLK + krow >= 0)
        vwt = jnp.concatenate([vr[0] for vr in vwt_refs[qb:qb + n_win]], axis=1)
        pts = []
        for r in range(rep):
            s_r = jnp.where(wok, stw_sc[sg, :, r * tq:(r + 1) * tq], MASK_VALUE)
            m = jnp.max(s_r, axis=0, keepdims=True)
            pts.append(jnp.exp2(s_r - m).astype(BF16))
        owin_sc[sg] = _dot(_with_ones(vwt[g * HEAD_DIM:(g + 1) * HEAD_DIM, :]),
                           jnp.concatenate(pts, axis=1))

    ns = len(streams)

    def cmp_phase(nrows):
        cmp_scores(0, nrows)
        cmp_scores(1, nrows)
        for sg in range(ns):
            if sg + 2 < ns:
                cmp_scores(sg + 2, nrows)
            imp_sc[sg] = cmp_consume(sg, nrows)

    half = ncp // 2
    few = (n + 1) * (NSA_QB * BLK // CMP_STRIDE) <= half
    pl.when(few)(lambda: cmp_phase(half))
    pl.when(jnp.logical_not(few))(lambda: cmp_phase(ncp))

    win_scores(0)
    win_scores(1)
    qsel = []
    for sg in range(ns):
        if sg + 2 < ns:
            win_scores(sg + 2)
        qsel.append(select(sg, imp_sc[sg]))
        win_consume(sg)

    for sg in range(ns):
        m_sc[sg] = jnp.full((1, rep * tq), MASK_VALUE, F32)
        acc_sc[sg] = jnp.zeros((ACC_ROWS, rep * tq), F32)

    hrep = rep // 2
    wide = hrep * tq
    units = [(sg, hf) for sg in range(ns) for hf in range(2)]
    nu = len(units)

    def scores(c, u):
        sg, hf = units[u]
        k0 = pl.multiple_of(c * tk, tk)
        kx = jnp.concatenate([ks_ref[pl.ds(k0, tk), :], blk_ref[pl.ds(k0, tk), :]], axis=1)
        st_sc[u] = _dot(kx, qsel[sg][:, hf * wide:(hf + 1) * wide])

    def consume(c, u, last):
        sg, hf = units[u]
        qb, g = streams[sg]
        lanes = slice(hf * wide, (hf + 1) * wide)
        if last:
            tok_ok = c * tk + lax.broadcasted_iota(jnp.int32, (tk, tq), 0) <= t_lane[qb]
        m_old = m_sc[sg, :, lanes]
        ms, pts = [], []
        for r in range(hrep):
            s_r = st_sc[u, :, r * tq:(r + 1) * tq]
            if last:
                s_r = jnp.where(tok_ok, s_r, MASK_VALUE)
            m_new = jnp.maximum(m_old[:, r * tq:(r + 1) * tq], jnp.max(s_r, axis=0, keepdims=True))
            pts.append(jnp.exp2(s_r - m_new).astype(BF16))
            ms.append(m_new)
        m_new = jnp.concatenate(ms, axis=1)
        alpha = jnp.exp2(m_old - m_new)
        vt = vst_ref[c, g * HEAD_DIM:(g + 1) * HEAD_DIM, :]
        acc_sc[sg, :, lanes] = (alpha * acc_sc[sg, :, lanes]
                                + _dot(_with_ones(vt), jnp.concatenate(pts, axis=1)))
        m_sc[sg, :, lanes] = m_new

    n_steps = ((n * NSA_QB + 1) * tq + tk - 1) // tk
    scores(0, 0)
    scores(0, 1)

    def body(c):
        for u in range(nu):
            if u + 2 < nu:
                scores(c, u + 2)
            else:
                scores(c + 1, u + 2 - nu)
            consume(c, u, False)

    _loop_in_pairs(n_steps - 1, body)
    for u in range(nu):
        if u + 2 < nu:
            scores(n_steps - 1, u + 2)
        consume(n_steps - 1, u, True)

    for qb in range(NSA_QB):
        tok = slice(qb * tq, (qb + 1) * tq)
        gate_t = (1.0 / (1.0 + jnp.exp(-gl_ref[tok, :]))).T
        pairs = []
        for p in range(N_PAIRS):
            sl = slice(p * LANES, (p + 1) * LANES)
            g = (2 * p) // rep
            sg = qb * NSA_KV + g
            rows = []
            for e in range(2):
                h = 2 * p + e
                ql = slice((h - g * rep) * tq, (h - g * rep + 1) * tq)
                slc, win = acc_sc[sg][:, ql], owin_sc[sg][:, ql]
                branches = (ocmp_sc[sg][:, ql],
                            slc[:HEAD_DIM] / jnp.maximum(slc[HEAD_DIM:HEAD_DIM + 1], 1e-30),
                            win[:HEAD_DIM] / win[HEAD_DIM:HEAD_DIM + 1])
                out = None
                for j, br in enumerate(branches):
                    t = gate_t[3 * h + j:3 * h + j + 1, :] * br
                    out = t if out is None else out + t
                rows.append(out)
            pair = jnp.concatenate(rows, axis=0).T
            pairs.append((pair * _silu(gate_ref[tok, sl].astype(F32))).astype(BF16))
        y = x_ref[tok, :] + _dot(jnp.concatenate(pairs, axis=1), wout_ref[...])
        var = jnp.mean(y * y, axis=-1, keepdims=True)
        o_ref[tok, :] = y * lax.rsqrt(var + NORM_EPS) * fg_ref[...]


def _selection_overlap_t(n_cmp_pad, n_cmp, n_sel):
    cs = np.arange(n_cmp_pad) * CMP_STRIDE
    js = np.arange(LANES) * SEL_LEN
    ov = np.minimum(cs[None, :] + CMP_LEN, js[:, None] + SEL_LEN) - np.maximum(cs[None, :], js[:, None])
    ov = (np.clip(ov, 0, None) / CMP_LEN).astype(np.float32)
    ov[:, n_cmp:] = 0.0
    ov[n_sel:, :] = 0.0
    return ov


def _nsa_attention(q_t, k_cmp, v_cmp_t, ks, vs_t, kw, vw_t, g_logit, gate, w_out, x, final_gain):
    b, s, d = x.shape
    tq = NSA_QB * BLK
    ncp = k_cmp.shape[1]
    n_cmp = s // CMP_STRIDE - 1
    n_sel = s // SEL_LEN
    n_win = -(-(NSA_WINDOW - 1) // BLK) + 1
    assert n_sel <= LANES and s % NSA_TK == 0 and tq == NSA_TK
    sel_rows = -(-n_sel // BF16_SUBLANES) * BF16_SUBLANES
    ov_t = jnp.asarray(_selection_overlap_t(ncp, n_cmp, n_sel)[:sel_rows], BF16)
    row = lambda bi, i: (bi, i, 0)
    per_b = lambda bi, i: (bi, 0, 0)
    in_specs = [pl.BlockSpec((None, NSA_QB, MIX_WIDTH, BLK), lambda bi, i: (bi, i, 0, 0)),
                pl.BlockSpec((None, ncp, LANES), per_b),
                pl.BlockSpec((None, LANES, ncp), per_b),
                pl.BlockSpec((None, s, LANES), per_b),
                pl.BlockSpec((None, s // NSA_TK, LANES, NSA_TK), lambda bi, i: (bi, 0, 0, 0))]
    args = [q_t, k_cmp, v_cmp_t, ks, vs_t]
    for j in range(n_win - 1, -NSA_QB, -1):
        in_specs.append(pl.BlockSpec(
            (None, BLK, LANES), lambda bi, i, j=j: (bi, jnp.maximum(NSA_QB * i - j, 0), 0)))
        args.append(kw)
    for j in range(n_win - 1, -NSA_QB, -1):
        in_specs.append(pl.BlockSpec(
            (None, 1, LANES, BLK), lambda bi, i, j=j: (bi, jnp.maximum(NSA_QB * i - j, 0), 0, 0)))
        args.append(vw_t)
    block_onehot = jnp.asarray(
        (np.arange(s)[:, None] // SEL_LEN == np.arange(LANES)[None, :]).astype(np.float32), BF16)
    in_specs += [pl.BlockSpec((None, tq, LANES), row),
                 pl.BlockSpec((None, tq, MIX_WIDTH), row),
                 pl.BlockSpec((sel_rows, ncp), lambda bi, i: (0, 0)),
                 pl.BlockSpec((s, LANES), lambda bi, i: (0, 0)),
                 pl.BlockSpec((MIX_WIDTH, d), lambda bi, i: (0, 0), pipeline_mode=pl.Buffered(1)),
                 pl.BlockSpec((None, tq, d), row),
                 pl.BlockSpec((1, d), lambda bi, i: (0, 0))]
    args += [g_logit, gate, ov_t, block_onehot, w_out.astype(BF16), x, final_gain[None, :]]
    wide = NSA_REP * BLK
    ns = NSA_QB * NSA_KV
    return pl.pallas_call(
        functools.partial(_nsa_kernel, n_cmp, n_sel, n_win),
        grid=(b, s // tq),
        in_specs=in_specs,
        out_specs=pl.BlockSpec((None, tq, d), row),
        out_shape=jax.ShapeDtypeStruct((b, s, d), F32),
        scratch_shapes=[pltpu.VMEM((2 * ns, NSA_TK, wide // 2), F32),
                        pltpu.VMEM((ns, ncp, wide), F32),
                        pltpu.VMEM((ns, n_win * BLK, wide), F32),
                        pltpu.VMEM((ns, 1, wide), F32),
                        pltpu.VMEM((ns, ACC_ROWS, wide), F32),
                        pltpu.VMEM((ns, HEAD_DIM, wide), F32),
                        pltpu.VMEM((ns, ACC_ROWS, wide), F32),
                        pltpu.VMEM((ns, sel_rows, BLK), F32)],
        compiler_params=_cparams(("arbitrary", "arbitrary")),
        name="nsa_attn",
    )(*args)


def _split(w, sizes):
    offs = np.cumsum([0] + list(sizes))
    return [w[:, int(offs[i]):int(offs[i + 1])] for i in range(len(sizes))]


def _swa_layer(x, rope, gain, w_in, sinks, w_out):
    kvw = SWA_KV * HEAD_DIM
    parts = _split(w_in, [MIX_WIDTH, kvw, kvw, MIX_WIDTH])
    segs = [_Seg(MIX_WIDTH, rope=True, scale=Q_SCALE), _Seg(kvw, rope=True), _Seg(kvw),
            _Seg(MIX_WIDTH)]
    q, k, v, gate = _norm_proj(x, gain, parts, segs, rope)
    o = _banded_attention(q, k, v, SWA_WINDOW - 1, sinks=sinks)
    return o, w_out, gate


def _dilated_layer(x, rope, gain, w_in, w_out, residual=None):
    b, s, _ = x.shape
    kvw = DIL_KV * HEAD_DIM
    sizes, segs = [], []
    for window, dil in DIL_PATTERNS:
        assert s % (dil * BLK) == 0
        sizes += [MIX_WIDTH, kvw, kvw]
        segs += [_Seg(MIX_WIDTH, rope=True, scale=Q_SCALE, dil=dil), _Seg(kvw, rope=True, dil=dil),
                 _Seg(kvw, dil=dil)]
    sizes.append(MIX_WIDTH)
    segs.append(_Seg(MIX_WIDTH))
    res = _norm_proj(x, gain, _split(w_in, sizes), segs, rope, residual)
    if residual is not None:
        x, res = res[0], res[1:]
    gate = res[-1]
    os_, lses, dils = [], [], []
    for gi, (window, dil) in enumerate(DIL_PATTERNS):
        q, k, v = res[3 * gi:3 * gi + 3]
        if dil > 1:
            q, k, v = (t.reshape(b * dil, s // dil, t.shape[-1]) for t in (q, k, v))
        o, lse = _banded_attention(q, k, v, window // dil, want_lse=True)
        if dil > 1:
            o = o.reshape(b, dil, s // dil, MIX_WIDTH)
            lse = lse.reshape(b, dil, s // dil, LANES)
        os_.append(o)
        lses.append(lse)
        dils.append(dil)
    return _dil_combine_out_proj(os_, lses, gate, tuple(dils), w_out, x)


def _fox_layer(x, gain, w_in, b_f, w_out):
    sizes = [MIX_WIDTH, MIX_WIDTH, MIX_WIDTH, N_HEADS, MIX_WIDTH]
    segs = [_Seg(MIX_WIDTH, scale=Q_SCALE, tile=FOX_T), _Seg(MIX_WIDTH),
            _Seg(MIX_WIDTH, tile=FOX_T), _Seg(LANES, dtype=F32), _Seg(MIX_WIDTH)]
    qt, k, vt, f_logit, gate = _norm_proj(x, gain, _split(w_in, sizes), segs)
    kb = _fox_decay(f_logit, b_f)
    return _fox_attention(qt, k, kb, vt, gate), w_out


def _nsa_layer(x, positions, rope, gain, w_in, pe_k, w1_k, w2_k, pe_v, w1_v, w2_v, w_out,
               residual, final_gain):
    b, s, _ = x.shape
    kvw = NSA_KV * HEAD_DIM
    sizes = [MIX_WIDTH] + [kvw] * 6 + [3 * N_HEADS, MIX_WIDTH]
    segs = [_Seg(MIX_WIDTH, rope=True, scale=Q_SCALE, tile=BLK), _Seg(kvw), _Seg(kvw),
            _Seg(kvw, rope=True), _Seg(kvw, tile=NSA_TK), _Seg(kvw, rope=True), _Seg(kvw, tile=BLK),
            _Seg(LANES, dtype=F32), _Seg(MIX_WIDTH)]
    x, q, kc, vc, ks, vs_t, kw, vw_t, g_logit, gate = _norm_proj(
        x, gain, _split(w_in, sizes), segs, rope, residual)
    ns = s // CMP_STRIDE
    cmp_pos = jnp.concatenate(
        [positions[:, CMP_LEN - 1::CMP_STRIDE], positions[:, -1:]], axis=1)[:, :ns]
    cmp_c, cmp_s = _rope_tables(cmp_pos.reshape(-1))
    k_cmp, v_cmp_t = _nsa_compress(kc, vc, pe_k, w1_k, w2_k, pe_v, w1_v, w2_v, cmp_c, cmp_s)
    return _nsa_attention(q, k_cmp, v_cmp_t, ks, vs_t, kw, vw_t, g_logit, gate, w_out, x,
                          final_gain)


def kernel(x, positions, norm_0, w_in_0, sinks_0, w_out_0, norm_1, w_in_1, w_out_1, norm_2, w_in_2, b_f_2, w_out_2, norm_3, w_in_3, cmp_pe_k_3, cmp_w1_k_3, cmp_w2_k_3, cmp_pe_v_3, cmp_w1_v_3, cmp_w2_v_3, w_out_3, final_norm):
    rope = _rope_tables(positions.reshape(-1))
    o, w, gate = _swa_layer(x, rope, norm_0, w_in_0, sinks_0, w_out_0)
    x = _dilated_layer(x, rope, norm_1, w_in_1, w_out_1, residual=(o, w, gate))
    o, w = _fox_layer(x, norm_2, w_in_2, b_f_2, w_out_2)
    return _nsa_layer(x, positions, rope, norm_3, w_in_3, cmp_pe_k_3, cmp_w1_k_3, cmp_w2_k_3,
                      cmp_pe_v_3, cmp_w1_v_3, cmp_w2_v_3, w_out_3, residual=(o, w),
                      final_gain=final_norm)
```
